```python
import math
import jax, jax.numpy as jnp
from jax import lax
import numpy as np

D_MODEL = 2048
BATCH = 8
SEQ = 4096
DEPTH = 2

D_MIX = D_MODEL
HEAD_DIM = 64
N_Q_HEADS = 16
N_KV_HEADS = 4
GQ = N_Q_HEADS // N_KV_HEADS
WINDOW = 128
ATTN_BLOCK = 128
ROT_DIM = HEAD_DIM // 4
ROPE_THETA = 500000.0
CONV_GROUPS = 8
CONV_CH = CONV_GROUPS * HEAD_DIM
CONV_WIDTH = 31
SGU_HEADS = 8
SGU_CH = SGU_HEADS * HEAD_DIM
SGU_CHUNK = 128
D_FF = 5632
FFN_RESIDUAL_WEIGHT = 0.5
NORM_EPS = 1e-5

Q_END = N_Q_HEADS * HEAD_DIM
K_END = Q_END + N_KV_HEADS * HEAD_DIM
V_END = K_END + N_KV_HEADS * HEAD_DIM
CONV_END = V_END + 2 * CONV_CH
IN_COLS = CONV_END + 2 * SGU_CH

kernel_name = 'hybrid_parallel_conv_sgu_swa_block'


def rms_norm(x, g):
    xf = x.astype(jnp.float32)
    y = xf * lax.rsqrt(jnp.mean(xf * xf, axis=-1, keepdims=True) + NORM_EPS)
    return (y * g.astype(jnp.float32)).astype(x.dtype)


def layer_norm(x, g, b):
    xf = x.astype(jnp.float32)
    mu = jnp.mean(xf, axis=-1, keepdims=True)
    xc = xf - mu
    y = xc * lax.rsqrt(jnp.mean(xc * xc, axis=-1, keepdims=True) + NORM_EPS)
    return (y * g.astype(jnp.float32) + b.astype(jnp.float32)).astype(x.dtype)


def swiglu(h, w_in, w_out):
    gu = h @ w_in
    return (jax.nn.silu(gu[..., :D_FF]) * gu[..., D_FF:]) @ w_out


def rope_tables(positions):
    inv_freq = 1.0 / (ROPE_THETA ** (jnp.arange(0, ROT_DIM, 2, dtype=jnp.float32) / ROT_DIM))
    ang = positions.astype(jnp.float32)[..., None] * inv_freq
    return jnp.cos(ang), jnp.sin(ang)


def apply_partial_rope(t, cos, sin):
    half = ROT_DIM // 2
    t1 = t[..., :half].astype(jnp.float32)
    t2 = t[..., half:ROT_DIM].astype(jnp.float32)
    c = cos[:, :, None, :]
    s = sin[:, :, None, :]
    rot = jnp.concatenate([t1 * c - t2 * s, t2 * c + t1 * s], axis=-1).astype(t.dtype)
    return jnp.concatenate([rot, t[..., ROT_DIM:]], axis=-1)


def sliding_window_attention(q, k, v, sinks):
    B, S = q.shape[0], q.shape[1]
    nb = S // ATTN_BLOCK
    qb = q.reshape(B, nb, ATTN_BLOCK, N_KV_HEADS, GQ, HEAD_DIM)

    def with_prev(t):
        tb = t.reshape(B, nb, ATTN_BLOCK, N_KV_HEADS, HEAD_DIM)
        prev = jnp.concatenate([jnp.zeros_like(tb[:, :1]), tb[:, :-1]], axis=1)
        return jnp.concatenate([prev, tb], axis=2)

    kk, vv = with_prev(k), with_prev(v)
    scores = jnp.einsum('bnqhgd,bnkhd->bnhgqk', qb, kk).astype(jnp.float32) * (HEAD_DIM ** -0.5)
    qi = jnp.arange(ATTN_BLOCK)[:, None]
    kj = jnp.arange(2 * ATTN_BLOCK)[None, :]
    dist = qi + ATTN_BLOCK - kj
    band = (dist >= 0) & (dist < WINDOW)
    kpos = jnp.arange(nb)[:, None, None] * ATTN_BLOCK + kj[None] - ATTN_BLOCK
    mask = band[None] & (kpos >= 0)
    scores = jnp.where(mask[None, :, None, None], scores, jnp.float32(-1e30))
    s = sinks.astype(jnp.float32).reshape(N_KV_HEADS, GQ)[None, None, :, :, None, None]
    m = jnp.maximum(jnp.max(scores, axis=-1, keepdims=True), s)
    p = jnp.exp(scores - m)
    p = p / (jnp.sum(p, axis=-1, keepdims=True) + jnp.exp(s - m))
    out = jnp.einsum('bnhgqk,bnkhd->bnqhgd', p.astype(v.dtype), vv)
    return out.reshape(B, S, N_Q_HEADS * HEAD_DIM)


def conv_module(a, dw_w, dw_b, ln_g, ln_b):
    h = a[..., :CONV_CH] * jax.nn.sigmoid(a[..., CONV_CH:])
    hp = jnp.pad(h, ((0, 0), (CONV_WIDTH - 1, 0), (0, 0)))
    y = lax.conv_general_dilated(hp, dw_w[:, None, :].astype(h.dtype), window_strides=(1,),
                                 padding='VALID', dimension_numbers=('NWC', 'WIO', 'NWC'),
                                 feature_group_count=CONV_CH) + dw_b
    return jax.nn.silu(layer_norm(y, ln_g, ln_b))


def spatial_gating(a, ln_g, ln_b, w_s, b_s):
    B, S = a.shape[0], a.shape[1]
    u = a[..., :SGU_CH]
    v = layer_norm(a[..., SGU_CH:], ln_g, ln_b)
    vb = v.reshape(B, S // SGU_CHUNK, SGU_CHUNK, SGU_HEADS, HEAD_DIM)
    causal = jnp.tril(jnp.ones((SGU_CHUNK, SGU_CHUNK), dtype=bool))
    ws = jnp.where(causal[None], w_s, jnp.zeros_like(w_s))
    mixed = jnp.einsum('hts,bnshd->bnthd', ws, vb) + b_s.T[None, None, :, :, None]
    return u * mixed.reshape(B, S, SGU_CH)


def _fwd_setup_inputs(seed: int = 0) -> dict:
    key = jax.random.key(seed)
    ks = jax.random.split(key, 24)
    L, D, F = DEPTH, D_MODEL, D_FF
    nrm = lambda k, shape, scale: jax.random.normal(k, shape, jnp.float32) * scale
    gain = lambda k, shape: 1.0 + 0.05 * jax.random.normal(k, shape, jnp.float32)
    x = jax.random.normal(ks[0], (BATCH, SEQ, D), jnp.float32)
    offs = jax.random.randint(ks[1], (BATCH, 1), 0, 1024, dtype=jnp.int32)
    positions = (jnp.arange(SEQ, dtype=jnp.int32)[None, :] + offs).astype(jnp.int32)
    return {
        'x': x,
        'positions': positions,
        'norm_ffn1': gain(ks[2], (L, D)),
        'ffn1_w_in': nrm(ks[3], (L, D, 2 * F), D ** -0.5),
        'ffn1_w_out': nrm(ks[4], (L, F, D), F ** -0.5),
        'norm_mix': gain(ks[5], (L, D)),
        'w_in': nrm(ks[6], (L, D, IN_COLS), D ** -0.5),
        'conv_dw_w': nrm(ks[7], (L, CONV_WIDTH, CONV_CH), CONV_WIDTH ** -0.5),
        'conv_dw_b': nrm(ks[8], (L, CONV_CH), 0.02),
        'conv_ln_g': gain(ks[9], (L, CONV_CH)),
        'conv_ln_b': nrm(ks[10], (L, CONV_CH), 0.02),
        'sgu_ln_g': gain(ks[11], (L, SGU_CH)),
        'sgu_ln_b': nrm(ks[12], (L, SGU_CH), 0.02),
        'sgu_w': nrm(ks[13], (L, SGU_HEADS, SGU_CHUNK, SGU_CHUNK), SGU_CHUNK ** -0.5),
        'sgu_b': 1.0 + 0.1 * jax.random.normal(ks[14], (L, SGU_HEADS, SGU_CHUNK), jnp.float32),
        'attn_sinks': nrm(ks[15], (L, N_Q_HEADS), 0.5),
        'w_out': nrm(ks[16], (L, D_MIX, D), D_MIX ** -0.5),
        'norm_ffn2': gain(ks[17], (L, D)),
        'ffn2_w_in': nrm(ks[18], (L, D, 2 * F), D ** -0.5),
        'ffn2_w_out': nrm(ks[19], (L, F, D), F ** -0.5),
        'final_norm': gain(ks[20], (D,)),
    }


def _fwd_reference(x, positions, norm_ffn1, ffn1_w_in, ffn1_w_out, norm_mix, w_in, conv_dw_w, conv_dw_b,
              conv_ln_g, conv_ln_b, sgu_ln_g, sgu_ln_b, sgu_w, sgu_b, attn_sinks, w_out,
              norm_ffn2, ffn2_w_in, ffn2_w_out, final_norm):
    B, S = x.shape[0], x.shape[1]
    cos, sin = rope_tables(positions)
    for l in range(DEPTH):
        h = rms_norm(x, norm_ffn1[l])
        x = x + FFN_RESIDUAL_WEIGHT * swiglu(h, ffn1_w_in[l], ffn1_w_out[l])
        h = rms_norm(x, norm_mix[l])
        p = h @ w_in[l]
        q = apply_partial_rope(p[..., :Q_END].reshape(B, S, N_Q_HEADS, HEAD_DIM), cos, sin)
        k = apply_partial_rope(p[..., Q_END:K_END].reshape(B, S, N_KV_HEADS, HEAD_DIM), cos, sin)
        v = p[..., K_END:V_END].reshape(B, S, N_KV_HEADS, HEAD_DIM)
        attn = sliding_window_attention(q, k, v, attn_sinks[l])
        conv = conv_module(p[..., V_END:CONV_END], conv_dw_w[l], conv_dw_b[l],
                           conv_ln_g[l], conv_ln_b[l])
        sgu = spatial_gating(p[..., CONV_END:], sgu_ln_g[l], sgu_ln_b[l],
                             sgu_w[l], sgu_b[l])
        x = x + jnp.concatenate([attn, conv, sgu], axis=-1) @ w_out[l]
        h = rms_norm(x, norm_ffn2[l])
        x = x + FFN_RESIDUAL_WEIGHT * swiglu(h, ffn2_w_in[l], ffn2_w_out[l])
    return rms_norm(x, final_norm)


import jax as _jax
import jax.numpy as _jnp

TWIN_FORMAT = 'train_step'
FWD_PARAMS = ['x', 'positions', 'norm_ffn1', 'ffn1_w_in', 'ffn1_w_out', 'norm_mix', 'w_in', 'conv_dw_w', 'conv_dw_b', 'conv_ln_g', 'conv_ln_b', 'sgu_ln_g', 'sgu_ln_b', 'sgu_w', 'sgu_b', 'attn_sinks', 'w_out', 'norm_ffn2', 'ffn2_w_in', 'ffn2_w_out', 'final_norm']
TWIN_WEIGHTS = ['norm_ffn1', 'ffn1_w_in', 'ffn1_w_out', 'norm_mix', 'w_in', 'conv_dw_w', 'conv_dw_b', 'conv_ln_g', 'conv_ln_b', 'sgu_ln_g', 'sgu_ln_b', 'sgu_w', 'sgu_b', 'attn_sinks', 'w_out', 'norm_ffn2', 'ffn2_w_in', 'ffn2_w_out', 'final_norm']
TWIN_DIFF_INPUT = 'x'
TWIN_INPUTS = ['x', 'positions', 'norm_ffn1', 'ffn1_w_in', 'ffn1_w_out', 'norm_mix', 'w_in', 'conv_dw_w', 'conv_dw_b', 'conv_ln_g', 'conv_ln_b', 'sgu_ln_g', 'sgu_ln_b', 'sgu_w', 'sgu_b', 'attn_sinks', 'w_out', 'norm_ffn2', 'ffn2_w_in', 'ffn2_w_out', 'final_norm', 'loss_target', 'm_norm_ffn1', 'm_ffn1_w_in', 'm_ffn1_w_out', 'm_norm_mix', 'm_w_in', 'm_conv_dw_w', 'm_conv_dw_b', 'm_conv_ln_g', 'm_conv_ln_b', 'm_sgu_ln_g', 'm_sgu_ln_b', 'm_sgu_w', 'm_sgu_b', 'm_attn_sinks', 'm_w_out', 'm_norm_ffn2', 'm_ffn2_w_in', 'm_ffn2_w_out', 'm_final_norm', 'v_norm_ffn1', 'v_ffn1_w_in', 'v_ffn1_w_out', 'v_norm_mix', 'v_w_in', 'v_conv_dw_w', 'v_conv_dw_b', 'v_conv_ln_g', 'v_conv_ln_b', 'v_sgu_ln_g', 'v_sgu_ln_b', 'v_sgu_w', 'v_sgu_b', 'v_attn_sinks', 'v_w_out', 'v_norm_ffn2', 'v_ffn2_w_in', 'v_ffn2_w_out', 'v_final_norm']
TWIN_OUTPUTS = ['loss', 'grad_x', 'grad_norm_ffn1', 'grad_ffn1_w_in', 'grad_ffn1_w_out', 'grad_norm_mix', 'grad_w_in', 'grad_conv_dw_w', 'grad_conv_dw_b', 'grad_conv_ln_g', 'grad_conv_ln_b', 'grad_sgu_ln_g', 'grad_sgu_ln_b', 'grad_sgu_w', 'grad_sgu_b', 'grad_attn_sinks', 'grad_w_out', 'grad_norm_ffn2', 'grad_ffn2_w_in', 'grad_ffn2_w_out', 'grad_final_norm', 'delta_norm_ffn1', 'delta_ffn1_w_in', 'delta_ffn1_w_out', 'delta_norm_mix', 'delta_w_in', 'delta_conv_dw_w', 'delta_conv_dw_b', 'delta_conv_ln_g', 'delta_conv_ln_b', 'delta_sgu_ln_g', 'delta_sgu_ln_b', 'delta_sgu_w', 'delta_sgu_b', 'delta_attn_sinks', 'delta_w_out', 'delta_norm_ffn2', 'delta_ffn2_w_in', 'delta_ffn2_w_out', 'delta_final_norm', 'new_m_norm_ffn1', 'new_m_ffn1_w_in', 'new_m_ffn1_w_out', 'new_m_norm_mix', 'new_m_w_in', 'new_m_conv_dw_w', 'new_m_conv_dw_b', 'new_m_conv_ln_g', 'new_m_conv_ln_b', 'new_m_sgu_ln_g', 'new_m_sgu_ln_b', 'new_m_sgu_w', 'new_m_sgu_b', 'new_m_attn_sinks', 'new_m_w_out', 'new_m_norm_ffn2', 'new_m_ffn2_w_in', 'new_m_ffn2_w_out', 'new_m_final_norm', 'new_v_norm_ffn1', 'new_v_ffn1_w_in', 'new_v_ffn1_w_out', 'new_v_norm_mix', 'new_v_w_in', 'new_v_conv_dw_w', 'new_v_conv_dw_b', 'new_v_conv_ln_g', 'new_v_conv_ln_b', 'new_v_sgu_ln_g', 'new_v_sgu_ln_b', 'new_v_sgu_w', 'new_v_sgu_b', 'new_v_attn_sinks', 'new_v_w_out', 'new_v_norm_ffn2', 'new_v_ffn2_w_in', 'new_v_ffn2_w_out', 'new_v_final_norm']
TWIN_LEAF_KINDS = {'loss': 'loss', 'grad_x': 'grad_x', 'grad_norm_ffn1': 'grad_w', 'grad_ffn1_w_in': 'grad_w', 'grad_ffn1_w_out': 'grad_w', 'grad_norm_mix': 'grad_w', 'grad_w_in': 'grad_w', 'grad_conv_dw_w': 'grad_w', 'grad_conv_dw_b': 'grad_w', 'grad_conv_ln_g': 'grad_w', 'grad_conv_ln_b': 'grad_w', 'grad_sgu_ln_g': 'grad_w', 'grad_sgu_ln_b': 'grad_w', 'grad_sgu_w': 'grad_w', 'grad_sgu_b': 'grad_w', 'grad_attn_sinks': 'grad_w', 'grad_w_out': 'grad_w', 'grad_norm_ffn2': 'grad_w', 'grad_ffn2_w_in': 'grad_w', 'grad_ffn2_w_out': 'grad_w', 'grad_final_norm': 'grad_w', 'delta_norm_ffn1': 'delta_w', 'delta_ffn1_w_in': 'delta_w', 'delta_ffn1_w_out': 'delta_w', 'delta_norm_mix': 'delta_w', 'delta_w_in': 'delta_w', 'delta_conv_dw_w': 'delta_w', 'delta_conv_dw_b': 'delta_w', 'delta_conv_ln_g': 'delta_w', 'delta_conv_ln_b': 'delta_w', 'delta_sgu_ln_g': 'delta_w', 'delta_sgu_ln_b': 'delta_w', 'delta_sgu_w': 'delta_w', 'delta_sgu_b': 'delta_w', 'delta_attn_sinks': 'delta_w', 'delta_w_out': 'delta_w', 'delta_norm_ffn2': 'delta_w', 'delta_ffn2_w_in': 'delta_w', 'delta_ffn2_w_out': 'delta_w', 'delta_final_norm': 'delta_w', 'new_m_norm_ffn1': 'new_m', 'new_m_ffn1_w_in': 'new_m', 'new_m_ffn1_w_out': 'new_m', 'new_m_norm_mix': 'new_m', 'new_m_w_in': 'new_m', 'new_m_conv_dw_w': 'new_m', 'new_m_conv_dw_b': 'new_m', 'new_m_conv_ln_g': 'new_m', 'new_m_conv_ln_b': 'new_m', 'new_m_sgu_ln_g': 'new_m', 'new_m_sgu_ln_b': 'new_m', 'new_m_sgu_w': 'new_m', 'new_m_sgu_b': 'new_m', 'new_m_attn_sinks': 'new_m', 'new_m_w_out': 'new_m', 'new_m_norm_ffn2': 'new_m', 'new_m_ffn2_w_in': 'new_m', 'new_m_ffn2_w_out': 'new_m', 'new_m_final_norm': 'new_m', 'new_v_norm_ffn1': 'new_v', 'new_v_ffn1_w_in': 'new_v', 'new_v_ffn1_w_out': 'new_v', 'new_v_norm_mix': 'new_v', 'new_v_w_in': 'new_v', 'new_v_conv_dw_w': 'new_v', 'new_v_conv_dw_b': 'new_v', 'new_v_conv_ln_g': 'new_v', 'new_v_conv_ln_b': 'new_v', 'new_v_sgu_ln_g': 'new_v', 'new_v_sgu_ln_b': 'new_v', 'new_v_sgu_w': 'new_v', 'new_v_sgu_b': 'new_v', 'new_v_attn_sinks': 'new_v', 'new_v_w_out': 'new_v', 'new_v_norm_ffn2': 'new_v', 'new_v_ffn2_w_in': 'new_v', 'new_v_ffn2_w_out': 'new_v', 'new_v_final_norm': 'new_v'}


def _forward(args):
    return _fwd_reference(*[args[k] for k in FWD_PARAMS])


def _output_shape():
    def fwd():
        inp = _fwd_setup_inputs(0)
        return _fwd_reference(*[inp[k] for k in FWD_PARAMS])
    out = _jax.eval_shape(fwd)
    return out.shape, out.dtype

N_MICROBATCH = 1
ADAM_LR = 0.001
ADAM_B1 = 0.9
ADAM_B2 = 0.999
ADAM_EPS = 1e-08
ADAM_WD = 0.01
ADAM_STEP = 10
PER_EXAMPLE_BATCH_AXIS = {'x': 0, 'positions': 0, 'loss_target': 0}
SHARED_INPUTS = []
_WEIGHT_DTYPES = {'norm_ffn1': _jnp.float32, 'ffn1_w_in': _jnp.float32, 'ffn1_w_out': _jnp.float32, 'norm_mix': _jnp.float32, 'w_in': _jnp.float32, 'conv_dw_w': _jnp.float32, 'conv_dw_b': _jnp.float32, 'conv_ln_g': _jnp.float32, 'conv_ln_b': _jnp.float32, 'sgu_ln_g': _jnp.float32, 'sgu_ln_b': _jnp.float32, 'sgu_w': _jnp.float32, 'sgu_b': _jnp.float32, 'attn_sinks': _jnp.float32, 'w_out': _jnp.float32, 'norm_ffn2': _jnp.float32, 'ffn2_w_in': _jnp.float32, 'ffn2_w_out': _jnp.float32, 'final_norm': _jnp.float32}
MOMENT_SCALE = {'norm_ffn1': 3.736664e-02, 'ffn1_w_in': 1.600732e-02, 'ffn1_w_out': 2.612157e-02, 'norm_mix': 5.744778e-02, 'w_in': 4.360813e-02, 'conv_dw_w': 4.507880e-02, 'conv_dw_b': 1.210092e-01, 'conv_ln_g': 6.246085e-02, 'conv_ln_b': 7.302210e-02, 'sgu_ln_g': 5.045475e-02, 'sgu_ln_b': 4.887614e-02, 'sgu_w': 3.490218e-02, 'sgu_b': 5.069823e-02, 'attn_sinks': 8.753529e-03, 'w_out': 5.111780e-02, 'norm_ffn2': 2.925672e-02, 'ffn2_w_in': 1.241118e-02, 'ffn2_w_out': 2.030011e-02, 'final_norm': 1.605644e+01}


def _to_microbatches(a, axis):
    t = _jnp.moveaxis(a, axis, 0)
    t = t.reshape((N_MICROBATCH, t.shape[0] // N_MICROBATCH) + t.shape[1:])
    return _jnp.moveaxis(t, 1, axis + 1)


def setup_inputs(seed: int = 0) -> dict:
    inp = _fwd_setup_inputs(seed)
    key = _jax.random.fold_in(_jax.random.key(seed), 7919)
    shape, _ = _output_shape()
    out = dict(inp)
    out["loss_target"] = _jax.random.normal(_jax.random.fold_in(key, 0), shape, _jnp.float32)
    for i, name in enumerate(TWIN_WEIGHTS):
        w = inp[name].astype(_jnp.float32)
        if MOMENT_SCALE is None:
            s = _jnp.sqrt(_jnp.mean(_jnp.square(w)) + 1e-30)
        else:
            s = MOMENT_SCALE[name]
        km, kv = _jax.random.split(_jax.random.fold_in(key, i + 1))
        out[name] = w
        out["m_" + name] = s * _jax.random.normal(km, w.shape, _jnp.float32)
        out["v_" + name] = (s * s) * _jax.random.uniform(kv, w.shape, _jnp.float32, 0.5, 1.5)
    if N_MICROBATCH > 1:
        for name, axis in PER_EXAMPLE_BATCH_AXIS.items():
            out[name] = _to_microbatches(out[name], axis)
    return {'x': out['x'], 'positions': out['positions'], 'norm_ffn1': out['norm_ffn1'], 'ffn1_w_in': out['ffn1_w_in'], 'ffn1_w_out': out['ffn1_w_out'], 'norm_mix': out['norm_mix'], 'w_in': out['w_in'], 'conv_dw_w': out['conv_dw_w'], 'conv_dw_b': out['conv_dw_b'], 'conv_ln_g': out['conv_ln_g'], 'conv_ln_b': out['conv_ln_b'], 'sgu_ln_g': out['sgu_ln_g'], 'sgu_ln_b': out['sgu_ln_b'], 'sgu_w': out['sgu_w'], 'sgu_b': out['sgu_b'], 'attn_sinks': out['attn_sinks'], 'w_out': out['w_out'], 'norm_ffn2': out['norm_ffn2'], 'ffn2_w_in': out['ffn2_w_in'], 'ffn2_w_out': out['ffn2_w_out'], 'final_norm': out['final_norm'], 'loss_target': out['loss_target'], 'm_norm_ffn1': out['m_norm_ffn1'], 'm_ffn1_w_in': out['m_ffn1_w_in'], 'm_ffn1_w_out': out['m_ffn1_w_out'], 'm_norm_mix': out['m_norm_mix'], 'm_w_in': out['m_w_in'], 'm_conv_dw_w': out['m_conv_dw_w'], 'm_conv_dw_b': out['m_conv_dw_b'], 'm_conv_ln_g': out['m_conv_ln_g'], 'm_conv_ln_b': out['m_conv_ln_b'], 'm_sgu_ln_g': out['m_sgu_ln_g'], 'm_sgu_ln_b': out['m_sgu_ln_b'], 'm_sgu_w': out['m_sgu_w'], 'm_sgu_b': out['m_sgu_b'], 'm_attn_sinks': out['m_attn_sinks'], 'm_w_out': out['m_w_out'], 'm_norm_ffn2': out['m_norm_ffn2'], 'm_ffn2_w_in': out['m_ffn2_w_in'], 'm_ffn2_w_out': out['m_ffn2_w_out'], 'm_final_norm': out['m_final_norm'], 'v_norm_ffn1': out['v_norm_ffn1'], 'v_ffn1_w_in': out['v_ffn1_w_in'], 'v_ffn1_w_out': out['v_ffn1_w_out'], 'v_norm_mix': out['v_norm_mix'], 'v_w_in': out['v_w_in'], 'v_conv_dw_w': out['v_conv_dw_w'], 'v_conv_dw_b': out['v_conv_dw_b'], 'v_conv_ln_g': out['v_conv_ln_g'], 'v_conv_ln_b': out['v_conv_ln_b'], 'v_sgu_ln_g': out['v_sgu_ln_g'], 'v_sgu_ln_b': out['v_sgu_ln_b'], 'v_sgu_w': out['v_sgu_w'], 'v_sgu_b': out['v_sgu_b'], 'v_attn_sinks': out['v_attn_sinks'], 'v_w_out': out['v_w_out'], 'v_norm_ffn2': out['v_norm_ffn2'], 'v_ffn2_w_in': out['v_ffn2_w_in'], 'v_ffn2_w_out': out['v_ffn2_w_out'], 'v_final_norm': out['v_final_norm']}


def _loss(weights, diff, rest, loss_target):
    with _jax.named_scope("forward"):
        args = {**rest, TWIN_DIFF_INPUT: diff, **{k: w.astype(_WEIGHT_DTYPES[k]) for k, w in weights.items()}}
        y = _forward(args)
    with _jax.named_scope("loss_head"):
        err = _jnp.square(y.astype(_jnp.float32) - loss_target)
        return 0.5 * _jnp.sum(_jnp.mean(err, axis=-1)) if err.ndim else 0.5 * err


def _adamw(w, g, m, v):
    m = ADAM_B1 * m + (1.0 - ADAM_B1) * g
    v = ADAM_B2 * v + (1.0 - ADAM_B2) * _jnp.square(g)
    m_hat = m / (1.0 - ADAM_B1 ** ADAM_STEP)
    v_hat = v / (1.0 - ADAM_B2 ** ADAM_STEP)
    delta = -ADAM_LR * (m_hat / (_jnp.sqrt(v_hat) + ADAM_EPS) + ADAM_WD * w)
    return delta, m, v


def reference(x, positions, norm_ffn1, ffn1_w_in, ffn1_w_out, norm_mix, w_in, conv_dw_w, conv_dw_b, conv_ln_g, conv_ln_b, sgu_ln_g, sgu_ln_b, sgu_w, sgu_b, attn_sinks, w_out, norm_ffn2, ffn2_w_in, ffn2_w_out, final_norm, loss_target, m_norm_ffn1, m_ffn1_w_in, m_ffn1_w_out, m_norm_mix, m_w_in, m_conv_dw_w, m_conv_dw_b, m_conv_ln_g, m_conv_ln_b, m_sgu_ln_g, m_sgu_ln_b, m_sgu_w, m_sgu_b, m_attn_sinks, m_w_out, m_norm_ffn2, m_ffn2_w_in, m_ffn2_w_out, m_final_norm, v_norm_ffn1, v_ffn1_w_in, v_ffn1_w_out, v_norm_mix, v_w_in, v_conv_dw_w, v_conv_dw_b, v_conv_ln_g, v_conv_ln_b, v_sgu_ln_g, v_sgu_ln_b, v_sgu_w, v_sgu_b, v_attn_sinks, v_w_out, v_norm_ffn2, v_ffn2_w_in, v_ffn2_w_out, v_final_norm):
    given = dict(x=x, positions=positions, norm_ffn1=norm_ffn1, ffn1_w_in=ffn1_w_in, ffn1_w_out=ffn1_w_out, norm_mix=norm_mix, w_in=w_in, conv_dw_w=conv_dw_w, conv_dw_b=conv_dw_b, conv_ln_g=conv_ln_g, conv_ln_b=conv_ln_b, sgu_ln_g=sgu_ln_g, sgu_ln_b=sgu_ln_b, sgu_w=sgu_w, sgu_b=sgu_b, attn_sinks=attn_sinks, w_out=w_out, norm_ffn2=norm_ffn2, ffn2_w_in=ffn2_w_in, ffn2_w_out=ffn2_w_out, final_norm=final_norm, loss_target=loss_target, m_norm_ffn1=m_norm_ffn1, m_ffn1_w_in=m_ffn1_w_in, m_ffn1_w_out=m_ffn1_w_out, m_norm_mix=m_norm_mix, m_w_in=m_w_in, m_conv_dw_w=m_conv_dw_w, m_conv_dw_b=m_conv_dw_b, m_conv_ln_g=m_conv_ln_g, m_conv_ln_b=m_conv_ln_b, m_sgu_ln_g=m_sgu_ln_g, m_sgu_ln_b=m_sgu_ln_b, m_sgu_w=m_sgu_w, m_sgu_b=m_sgu_b, m_attn_sinks=m_attn_sinks, m_w_out=m_w_out, m_norm_ffn2=m_norm_ffn2, m_ffn2_w_in=m_ffn2_w_in, m_ffn2_w_out=m_ffn2_w_out, m_final_norm=m_final_norm, v_norm_ffn1=v_norm_ffn1, v_ffn1_w_in=v_ffn1_w_in, v_ffn1_w_out=v_ffn1_w_out, v_norm_mix=v_norm_mix, v_w_in=v_w_in, v_conv_dw_w=v_conv_dw_w, v_conv_dw_b=v_conv_dw_b, v_conv_ln_g=v_conv_ln_g, v_conv_ln_b=v_conv_ln_b, v_sgu_ln_g=v_sgu_ln_g, v_sgu_ln_b=v_sgu_ln_b, v_sgu_w=v_sgu_w, v_sgu_b=v_sgu_b, v_attn_sinks=v_attn_sinks, v_w_out=v_w_out, v_norm_ffn2=v_norm_ffn2, v_ffn2_w_in=v_ffn2_w_in, v_ffn2_w_out=v_ffn2_w_out, v_final_norm=v_final_norm)
    weights = {n: given[n] for n in TWIN_WEIGHTS}
    shared = {n: given[n] for n in SHARED_INPUTS}
    per_example = {n: given[n] for n in ['x', 'positions']}
    grad_fn = _jax.value_and_grad(_loss, argnums=(0, 1))

    def one_microbatch(ex, loss_target):
        ex = dict(ex)
        diff = ex.pop(TWIN_DIFF_INPUT)
        return grad_fn(weights, diff, {**shared, **ex}, loss_target)

    if N_MICROBATCH == 1:
        loss, (grad_w, grad_x) = one_microbatch(per_example, given["loss_target"])
    else:
        def body(carry, xs):
            loss_sum, grad_sum = carry
            l_k, (gw_k, gx_k) = one_microbatch(xs[0], xs[1])
            with _jax.named_scope("update"):
                return (loss_sum + l_k, _jax.tree.map(_jnp.add, grad_sum, gw_k)), gx_k

        init = (_jnp.zeros((), _jnp.float32), _jax.tree.map(_jnp.zeros_like, weights))
        (loss, grad_w), grad_x = _jax.lax.scan(body, init, (per_example, given["loss_target"]))
    with _jax.named_scope("update"):
        delta_w, new_m, new_v = {}, {}, {}
        for n in TWIN_WEIGHTS:
            delta_w[n], new_m[n], new_v[n] = _adamw(weights[n], grad_w[n], given["m_" + n], given["v_" + n])
    return (loss, grad_x, *[grad_w[n] for n in TWIN_WEIGHTS], *[delta_w[n] for n in TWIN_WEIGHTS],
            *[new_m[n] for n in TWIN_WEIGHTS], *[new_v[n] for n in TWIN_WEIGHTS])
```

```python
import functools
import math

import jax
import jax.numpy as jnp
from jax import lax
from jax.experimental import pallas as pl
from jax.experimental.pallas import tpu as pltpu

f32 = jnp.float32
bf16 = jnp.bfloat16

N_DEV = 8
HEAD_DIM = 64
N_Q_HEADS = 16
N_KV_HEADS = 4
GQ = N_Q_HEADS // N_KV_HEADS
BLK = 128
ROT_DIM = 16
ROPE_THETA = 500000.0
CONV_W = 31
CONV_CH = 512
SGU_CH = 512
SGU_HEADS = 8
Q_END = N_Q_HEADS * HEAD_DIM
KV_W = 2 * N_KV_HEADS * HEAD_DIM
V_END = Q_END + KV_W
IN_COLS = V_END + 2 * CONV_CH + 2 * SGU_CH
HALO = 32
NORM_EPS = 1e-5
FFN_RES = 0.5
ADAM_LR, ADAM_B1, ADAM_B2, ADAM_EPS, ADAM_WD, ADAM_STEP = 0.001, 0.9, 0.999, 1e-08, 0.01, 10
LANE = 128
VMEM_LIMIT = 56 * 2 ** 20
MESH = pl.DeviceIdType.MESH

NN = (((1,), (0,)), ((), ()))
NT = (((1,), (1,)), ((), ()))
TN = (((0,), (0,)), ((), ()))


def _tile(dim, pref):
    t = min(pref, dim)
    while dim % t:
        t //= 2
    return t


def _params(*sem):
    return pltpu.CompilerParams(dimension_semantics=sem, vmem_limit_bytes=VMEM_LIMIT)


def _sig(x):
    return 1.0 / (1.0 + jnp.exp(-x))


def _dot(a, b, dims=NN):
    return lax.dot_general(a, b, dims, preferred_element_type=f32)


def _rowsum8(x):
    return x.reshape(x.shape[0] // 8, 8, x.shape[1]).sum(axis=0)


def rmsnorm_fwd(x, g, name):
    S, D = x.shape
    tm = _tile(S, 512)

    def body(x_ref, g_ref, o_ref):
        xv = x_ref[...]
        r = lax.rsqrt(jnp.mean(xv * xv, axis=-1, keepdims=True) + NORM_EPS)
        o_ref[...] = (xv * r * g_ref[...]).astype(bf16)

    return pl.pallas_call(
        body, name=name, grid=(S // tm,),
        in_specs=[pl.BlockSpec((tm, D), lambda i: (i, 0)), pl.BlockSpec((1, D), lambda i: (0, 0))],
        out_specs=pl.BlockSpec((tm, D), lambda i: (i, 0)),
        out_shape=jax.ShapeDtypeStruct((S, D), bf16),
        compiler_params=_params("parallel"),
    )(x, g)


def rmsnorm_bwd(dh, x, g, dres, name):
    S, D = x.shape
    tm = _tile(S, 256)
    n = S // tm

    def body(dh_ref, x_ref, g_ref, dres_ref, dx_ref, dxb_ref, dg_ref, acc):
        i = pl.program_id(0)

        @pl.when(i == 0)
        def _():
            acc[...] = jnp.zeros_like(acc)

        xv = x_ref[...]
        r = lax.rsqrt(jnp.mean(xv * xv, axis=-1, keepdims=True) + NORM_EPS)
        xh = xv * r
        dy = dh_ref[...]
        gy = dy * g_ref[...]
        dx = dres_ref[...] + r * (gy - xh * jnp.mean(gy * xh, axis=-1, keepdims=True))
        dx_ref[...] = dx
        dxb_ref[...] = dx.astype(bf16)
        acc[...] += _rowsum8(dy * xh)

        @pl.when(i == n - 1)
        def _():
            dg_ref[...] = jnp.sum(acc[...], axis=0, keepdims=True)

    row = pl.BlockSpec((tm, D), lambda i: (i, 0))
    vec = pl.BlockSpec((1, D), lambda i: (0, 0))
    return pl.pallas_call(
        body, name=name, grid=(n,),
        in_specs=[row, row, vec, row],
        out_specs=[row, row, vec],
        out_shape=[jax.ShapeDtypeStruct((S, D), f32), jax.ShapeDtypeStruct((S, D), bf16),
                   jax.ShapeDtypeStruct((1, D), f32)],
        scratch_shapes=[pltpu.VMEM((8, D), f32)],
        compiler_params=_params("arbitrary"),
    )(dh, x, g, dres)


def final_loss(x, g, tgt, name):
    S, D = x.shape
    tm = _tile(S, 256)
    n = S // tm

    def body(x_ref, g_ref, t_ref, dx_ref, dxb_ref, dg_ref, loss_ref, acc):
        i = pl.program_id(0)

        @pl.when(i == 0)
        def _():
            acc[...] = jnp.zeros_like(acc)
            loss_ref[...] = jnp.zeros_like(loss_ref)

        xv = x_ref[...]
        gv = g_ref[...]
        r = lax.rsqrt(jnp.mean(xv * xv, axis=-1, keepdims=True) + NORM_EPS)
        xh = xv * r
        diff = xh * gv - t_ref[...]
        tok = jnp.mean(diff * diff, axis=-1, keepdims=True)
        loss_ref[...] += 0.5 * jnp.sum(tok, axis=0, keepdims=True)
        dy = diff / D
        gy = dy * gv
        dx = r * (gy - xh * jnp.mean(gy * xh, axis=-1, keepdims=True))
        dx_ref[...] = dx
        dxb_ref[...] = dx.astype(bf16)
        acc[...] += _rowsum8(dy * xh)

        @pl.when(i == n - 1)
        def _():
            dg_ref[...] = jnp.sum(acc[...], axis=0, keepdims=True)

    row = pl.BlockSpec((tm, D), lambda i: (i, 0))
    vec = pl.BlockSpec((1, D), lambda i: (0, 0))
    return pl.pallas_call(
        body, name=name, grid=(n,),
        in_specs=[row, vec, row],
        out_specs=[row, row, vec, pl.BlockSpec((1, 1), lambda i: (0, 0))],
        out_shape=[jax.ShapeDtypeStruct((S, D), f32), jax.ShapeDtypeStruct((S, D), bf16),
                   jax.ShapeDtypeStruct((1, D), f32), jax.ShapeDtypeStruct((1, 1), f32)],
        scratch_shapes=[pltpu.VMEM((8, D), f32)],
        compiler_params=_params("arbitrary"),
    )(x, g, tgt)


def ffn_in(h, w2, name):
    S, D = h.shape
    F = w2.shape[2]
    tm, tn = _tile(S, 512), _tile(F, 512)

    def body(h_ref, w_ref, gu_ref, a_ref):
        hv = h_ref[...]
        g = _dot(hv, w_ref[0])
        u = _dot(hv, w_ref[1])
        gu_ref[0] = g.astype(bf16)
        gu_ref[1] = u.astype(bf16)
        a_ref[...] = (g * _sig(g) * u).astype(bf16)

    return pl.pallas_call(
        body, name=name, grid=(S // tm, F // tn),
        in_specs=[pl.BlockSpec((tm, D), lambda i, j: (i, 0)), pl.BlockSpec((2, D, tn), lambda i, j: (0, 0, j))],
        out_specs=[pl.BlockSpec((2, tm, tn), lambda i, j: (0, i, j)), pl.BlockSpec((tm, tn), lambda i, j: (i, j))],
        out_shape=[jax.ShapeDtypeStruct((2, S, F), bf16), jax.ShapeDtypeStruct((S, F), bf16)],
        compiler_params=_params("parallel", "parallel"),
    )(h, w2)


def mm_res(a, w, x, scale, name):
    S, K = a.shape
    N = w.shape[1]
    tm, tn, tk = _tile(S, 512), _tile(N, 1024), _tile(K, 512)
    nk = K // tk

    def body(a_ref, w_ref, x_ref, o_ref):
        k = pl.program_id(2)
        part = _dot(a_ref[...], w_ref[...])

        @pl.when(k == 0)
        def _():
            o_ref[...] = part

        @pl.when(k > 0)
        def _():
            o_ref[...] += part

        @pl.when(k == nk - 1)
        def _():
            o_ref[...] = x_ref[...] + scale * o_ref[...]

    return pl.pallas_call(
        body, name=name, grid=(S // tm, N // tn, nk),
        in_specs=[pl.BlockSpec((tm, tk), lambda i, j, k: (i, k)), pl.BlockSpec((tk, tn), lambda i, j, k: (k, j)),
                  pl.BlockSpec((tm, tn), lambda i, j, k: (i, j))],
        out_specs=pl.BlockSpec((tm, tn), lambda i, j, k: (i, j)),
        out_shape=jax.ShapeDtypeStruct((S, N), f32),
        compiler_params=_params("parallel", "parallel", "arbitrary"),
    )(a, w, x)


def mm_nn(a, w, name):
    S, K = a.shape
    N = w.shape[1]
    tm, tn = _tile(S, 512), _tile(N, 512)

    def body(a_ref, w_ref, o_ref):
        o_ref[...] = _dot(a_ref[...], w_ref[...])

    return pl.pallas_call(
        body, name=name, grid=(S // tm, N // tn),
        in_specs=[pl.BlockSpec((tm, K), lambda i, j: (i, 0)), pl.BlockSpec((K, tn), lambda i, j: (0, j))],
        out_specs=pl.BlockSpec((tm, tn), lambda i, j: (i, j)),
        out_shape=jax.ShapeDtypeStruct((S, N), f32),
        compiler_params=_params("parallel", "parallel"),
    )(a, w)


def mm_nt(a, w, name, scale=1.0):
    G, S, K = a.shape
    N = w.shape[1]
    tm, tn, tk = _tile(S, 512), _tile(N, 1024), _tile(K, 512)
    nk = K // tk
    steps = G * nk

    def body(a_ref, w_ref, o_ref):
        k = pl.program_id(2)
        part = _dot(a_ref[...], w_ref[...], NT)

        @pl.when(k == 0)
        def _():
            o_ref[...] = part

        @pl.when(k > 0)
        def _():
            o_ref[...] += part

        if scale != 1.0:
            @pl.when(k == steps - 1)
            def _():
                o_ref[...] = scale * o_ref[...]

    return pl.pallas_call(
        body, name=name, grid=(S // tm, N // tn, steps),
        in_specs=[pl.BlockSpec((None, tm, tk), lambda i, j, k: (k // nk, i, k % nk)),
                  pl.BlockSpec((None, tn, tk), lambda i, j, k: (k // nk, j, k % nk))],
        out_specs=pl.BlockSpec((tm, tn), lambda i, j, k: (i, j)),
        out_shape=jax.ShapeDtypeStruct((S, N), f32),
        compiler_params=_params("parallel", "parallel", "arbitrary"),
    )(a, w)


def mm_tn(a, b, name, out_dtype=bf16):
    S, M = a.shape
    G, _, N = b.shape
    tm, tn, tk = _tile(M, 512), _tile(N, 512), _tile(S, 1024)
    nk = S // tk

    def body(a_ref, b_ref, o_ref, acc):
        k = pl.program_id(3)
        part = _dot(a_ref[...], b_ref[...], TN)

        @pl.when(k == 0)
        def _():
            acc[...] = part

        @pl.when(k > 0)
        def _():
            acc[...] += part

        @pl.when(k == nk - 1)
        def _():
            o_ref[...] = acc[...].astype(out_dtype)

    return pl.pallas_call(
        body, name=name, grid=(G, M // tm, N // tn, nk),
        in_specs=[pl.BlockSpec((tk, tm), lambda g, i, j, k: (k, i)),
                  pl.BlockSpec((None, tk, tn), lambda g, i, j, k: (g, k, j))],
        out_specs=pl.BlockSpec((None, tm, tn), lambda g, i, j, k: (g, i, j)),
        out_shape=jax.ShapeDtypeStruct((G, M, N), out_dtype),
        scratch_shapes=[pltpu.VMEM((tm, tn), f32)],
        compiler_params=_params("parallel", "parallel", "parallel", "arbitrary"),
    )(a, b)


def ffn_dact(dx, wout, gu, name):
    S, D = dx.shape
    F = wout.shape[0]
    tm, tn = _tile(S, 512), _tile(F, 512)

    def body(dx_ref, w_ref, gu_ref, o_ref):
        da = FFN_RES * _dot(dx_ref[...], w_ref[...], NT)
        g = gu_ref[0].astype(f32)
        u = gu_ref[1].astype(f32)
        sg = _sig(g)
        o_ref[0] = (da * u * (sg * (1.0 + g * (1.0 - sg)))).astype(bf16)
        o_ref[1] = (da * (g * sg)).astype(bf16)

    return pl.pallas_call(
        body, name=name, grid=(S // tm, F // tn),
        in_specs=[pl.BlockSpec((tm, D), lambda i, j: (i, 0)), pl.BlockSpec((tn, D), lambda i, j: (j, 0)),
                  pl.BlockSpec((2, tm, tn), lambda i, j: (0, i, j))],
        out_specs=pl.BlockSpec((2, tm, tn), lambda i, j: (0, i, j)),
        out_shape=jax.ShapeDtypeStruct((2, S, F), bf16),
        compiler_params=_params("parallel", "parallel"),
    )(dx, wout, gu)


def _rope(t, c, s1, s2):
    w = t.shape[1]
    return t * c + pltpu.roll(t, 8, 1) * s1 + pltpu.roll(t, w - 8, 1) * s2


def _rope_t(d, c, s1, s2):
    w = d.shape[1]
    return d * c + pltpu.roll(d * s1, w - 8, 1) + pltpu.roll(d * s2, 8, 1)


def _attn_mask(n):
    qi = lax.broadcasted_iota(jnp.int32, (BLK, 2 * BLK), 0)
    kj = lax.broadcasted_iota(jnp.int32, (BLK, 2 * BLK), 1)
    dist = qi + BLK - kj
    return (dist >= 0) & (dist < BLK) & ((kj >= BLK) | (n > 0))


def _softmax_sink(s, valid, sk):
    s = jnp.where(valid, s, -1e30)
    m = jnp.maximum(jnp.max(s, axis=-1, keepdims=True), sk)
    e = jnp.exp(s - m)
    es = jnp.exp(sk - m)
    inv = 1.0 / (jnp.sum(e, axis=-1, keepdims=True) + es)
    return e * inv, es * inv


def attn_fwd(p, rope_c, rope_s1, rope_s2, sinks, name):
    S = p.shape[0]
    nb = S // BLK
    kvb = Q_END // KV_W

    def body(sink_ref, q_ref, kvc_ref, kvp_ref, cc_ref, s1c_ref, s2c_ref, cp_ref, s1p_ref, s2p_ref, o_ref):
        n = pl.program_id(0)
        cc, s1c, s2c = cc_ref[...], s1c_ref[...], s2c_ref[...]
        cp, s1p, s2p = cp_ref[...], s1p_ref[...], s2p_ref[...]
        q = _rope(q_ref[...], jnp.tile(cc, (1, 8)), jnp.tile(s1c, (1, 8)), jnp.tile(s2c, (1, 8)))
        kc = _rope(kvc_ref[:, :256], jnp.tile(cc, (1, 2)), jnp.tile(s1c, (1, 2)), jnp.tile(s2c, (1, 2)))
        kp = _rope(kvp_ref[:, :256], jnp.tile(cp, (1, 2)), jnp.tile(s1p, (1, 2)), jnp.tile(s2p, (1, 2)))
        k = jnp.concatenate([kp, kc], axis=0).astype(bf16)
        v = jnp.concatenate([kvp_ref[:, 256:], kvc_ref[:, 256:]], axis=0).astype(bf16)
        q = q.astype(bf16)
        valid = _attn_mask(n)
        for h in range(N_KV_HEADS):
            kh = k[:, h * HEAD_DIM:(h + 1) * HEAD_DIM]
            vh = v[:, h * HEAD_DIM:(h + 1) * HEAD_DIM]
            for g in range(GQ):
                hq = h * GQ + g
                qh = q[:, hq * HEAD_DIM:(hq + 1) * HEAD_DIM]
                s = _dot(qh, kh, NT) * (HEAD_DIM ** -0.5)
                pr, _ = _softmax_sink(s, valid, sink_ref[hq])
                o = _dot(pr.astype(bf16), vh)
                o_ref[:, hq * HEAD_DIM:(hq + 1) * HEAD_DIM] = o.astype(bf16)

    tab_c = pl.BlockSpec((BLK, LANE), lambda n: (n, 0))
    tab_p = pl.BlockSpec((BLK, LANE), lambda n: (jnp.maximum(n - 1, 0), 0))
    return pl.pallas_call(
        body, name=name, grid=(nb,),
        in_specs=[pl.BlockSpec(memory_space=pltpu.SMEM),
                  pl.BlockSpec((BLK, Q_END), lambda n: (n, 0)),
                  pl.BlockSpec((BLK, KV_W), lambda n: (n, kvb)),
                  pl.BlockSpec((BLK, KV_W), lambda n: (jnp.maximum(n - 1, 0), kvb)),
                  tab_c, tab_c, tab_c, tab_p, tab_p, tab_p],
        out_specs=pl.BlockSpec((BLK, Q_END), lambda n: (n, 0)),
        out_shape=jax.ShapeDtypeStruct((S, Q_END), bf16),
        compiler_params=_params("parallel"),
    )(sinks, p, p, p, rope_c, rope_s1, rope_s2, rope_c, rope_s1, rope_s2)


def attn_bwd(p, dcat, rope_c, rope_s1, rope_s2, sinks, name):
    S = p.shape[0]
    nb = S // BLK
    kvb = Q_END // KV_W

    def body(sink_ref, q_ref, kvc_ref, kvp_ref, do_ref, cc_ref, s1c_ref, s2c_ref, cp_ref, s1p_ref, s2p_ref,
             dq_ref, dkv_ref, dsink_ref, carry, dq_scr, dkv_scr):
        n = pl.program_id(0)

        @pl.when(n == 0)
        def _():
            carry[...] = jnp.zeros_like(carry)
            dsink_ref[...] = jnp.zeros_like(dsink_ref)

        cp, s1p, s2p = cp_ref[...], s1p_ref[...], s2p_ref[...]
        cp2, s1p2, s2p2 = jnp.tile(cp, (1, 2)), jnp.tile(s1p, (1, 2)), jnp.tile(s2p, (1, 2))

        @pl.when(n < nb)
        def _():
            cc, s1c, s2c = cc_ref[...], s1c_ref[...], s2c_ref[...]
            cc8, s1c8, s2c8 = jnp.tile(cc, (1, 8)), jnp.tile(s1c, (1, 8)), jnp.tile(s2c, (1, 8))
            q = _rope(q_ref[...], cc8, s1c8, s2c8).astype(bf16)
            kc = _rope(kvc_ref[:, :256], jnp.tile(cc, (1, 2)), jnp.tile(s1c, (1, 2)), jnp.tile(s2c, (1, 2)))
            kp = _rope(kvp_ref[:, :256], cp2, s1p2, s2p2)
            k = jnp.concatenate([kp, kc], axis=0).astype(bf16)
            v = jnp.concatenate([kvp_ref[:, 256:], kvc_ref[:, 256:]], axis=0).astype(bf16)
            do = do_ref[...].astype(bf16)
            valid = _attn_mask(n)
            lane = lax.broadcasted_iota(jnp.int32, (1, LANE), 1)
            dsink = jnp.zeros((1, LANE), f32)
            for h in range(N_KV_HEADS):
                kh = k[:, h * HEAD_DIM:(h + 1) * HEAD_DIM]
                vh = v[:, h * HEAD_DIM:(h + 1) * HEAD_DIM]
                dkh = jnp.zeros((2 * BLK, HEAD_DIM), f32)
                dvh = jnp.zeros((2 * BLK, HEAD_DIM), f32)
                for g in range(GQ):
                    hq = h * GQ + g
                    qh = q[:, hq * HEAD_DIM:(hq + 1) * HEAD_DIM]
                    doh = do[:, hq * HEAD_DIM:(hq + 1) * HEAD_DIM]
                    s = _dot(qh, kh, NT) * (HEAD_DIM ** -0.5)
                    pr, ps = _softmax_sink(s, valid, sink_ref[hq])
                    dpr = _dot(doh, vh, NT)
                    dvh = dvh + _dot(pr.astype(bf16), doh, TN)
                    row = jnp.sum(pr * dpr, axis=-1, keepdims=True)
                    ds = (pr * (dpr - row) * (HEAD_DIM ** -0.5)).astype(bf16)
                    dsink = dsink + jnp.where(lane == hq, -jnp.sum(ps * row, axis=0, keepdims=True), 0.0)
                    dq_scr[:, hq * HEAD_DIM:(hq + 1) * HEAD_DIM] = _dot(ds, kh)
                    dkh = dkh + _dot(ds, qh, TN)
                dkv_scr[:, h * HEAD_DIM:(h + 1) * HEAD_DIM] = dkh
                dkv_scr[:, 256 + h * HEAD_DIM:256 + (h + 1) * HEAD_DIM] = dvh
            dsink_ref[...] += dsink
            dq_ref[...] = _rope_t(dq_scr[...], cc8, s1c8, s2c8).astype(bf16)

        prev = carry[...]

        @pl.when(n < nb)
        def _():
            dkv_scr[pl.ds(0, BLK), :] = dkv_scr[pl.ds(0, BLK), :] + prev

        @pl.when(n == nb)
        def _():
            dkv_scr[pl.ds(0, BLK), :] = prev

        done = dkv_scr[pl.ds(0, BLK), :]
        dkv_ref[:, :256] = _rope_t(done[:, :256], cp2, s1p2, s2p2).astype(bf16)
        dkv_ref[:, 256:] = done[:, 256:].astype(bf16)

        @pl.when(n < nb)
        def _():
            carry[...] = dkv_scr[pl.ds(BLK, BLK), :]

    cur = lambda n: jnp.minimum(n, nb - 1)
    prv = lambda n: jnp.maximum(n - 1, 0)
    tab_c = pl.BlockSpec((BLK, LANE), lambda n: (cur(n), 0))
    tab_p = pl.BlockSpec((BLK, LANE), lambda n: (prv(n), 0))
    return pl.pallas_call(
        body, name=name, grid=(nb + 1,),
        in_specs=[pl.BlockSpec(memory_space=pltpu.SMEM),
                  pl.BlockSpec((BLK, Q_END), lambda n: (cur(n), 0)),
                  pl.BlockSpec((BLK, KV_W), lambda n: (cur(n), kvb)),
                  pl.BlockSpec((BLK, KV_W), lambda n: (prv(n), kvb)),
                  pl.BlockSpec((BLK, Q_END), lambda n: (cur(n), 0)),
                  tab_c, tab_c, tab_c, tab_p, tab_p, tab_p],
        out_specs=[pl.BlockSpec((BLK, Q_END), lambda n: (cur(n), 0)),
                   pl.BlockSpec((BLK, KV_W), lambda n: (prv(n), 0)),
                   pl.BlockSpec((1, LANE), lambda n: (0, 0))],
        out_shape=[jax.ShapeDtypeStruct((S, Q_END), bf16), jax.ShapeDtypeStruct((S, KV_W), bf16),
                   jax.ShapeDtypeStruct((1, LANE), f32)],
        scratch_shapes=[pltpu.VMEM((BLK, KV_W), f32), pltpu.VMEM((BLK, Q_END), f32), pltpu.VMEM((2 * BLK, KV_W), f32)],
        compiler_params=_params("arbitrary"),
    )(sinks, p, p, p, dcat, rope_c, rope_s1, rope_s2, rope_c, rope_s1, rope_s2)


A1_BLK = V_END // CONV_CH
A2_BLK = A1_BLK + 1


def _ln_stats(y):
    mu = jnp.mean(y, axis=-1, keepdims=True)
    xc = y - mu
    rstd = lax.rsqrt(jnp.mean(xc * xc, axis=-1, keepdims=True) + NORM_EPS)
    return xc * rstd, rstd


def conv_fwd(p, w, b, lng, lnb, name):
    S = p.shape[0]
    T = _tile(S, 256)
    r = T // HALO

    def body(a1_ref, a2_ref, h1_ref, h2_ref, w_ref, b_ref, g_ref, bb_ref, o_ref, y_ref, scr):
        i = pl.program_id(0)
        halo = h1_ref[...] * _sig(h2_ref[...])
        scr[pl.ds(0, HALO), :] = jnp.where(i > 0, halo, 0.0)
        scr[pl.ds(HALO, T), :] = a1_ref[...] * _sig(a2_ref[...])
        acc = jnp.zeros((T, CONV_CH), f32) + b_ref[...]
        for j in range(CONV_W):
            acc = acc + scr[pl.ds(HALO - (CONV_W - 1) + j, T), :] * w_ref[j:j + 1, :]
        y_ref[...] = acc
        yh, _ = _ln_stats(acc)
        z = yh * g_ref[...] + bb_ref[...]
        o_ref[...] = (z * _sig(z)).astype(bf16)

    vec = pl.BlockSpec((1, CONV_CH), lambda i: (0, 0))
    halo_map = lambda i: jnp.maximum(i * r - 1, 0)
    return pl.pallas_call(
        body, name=name, grid=(S // T,),
        in_specs=[pl.BlockSpec((T, CONV_CH), lambda i: (i, A1_BLK)), pl.BlockSpec((T, CONV_CH), lambda i: (i, A2_BLK)),
                  pl.BlockSpec((HALO, CONV_CH), lambda i: (halo_map(i), A1_BLK)),
                  pl.BlockSpec((HALO, CONV_CH), lambda i: (halo_map(i), A2_BLK)),
                  pl.BlockSpec((HALO, CONV_CH), lambda i: (0, 0)), vec, vec, vec],
        out_specs=[pl.BlockSpec((T, CONV_CH), lambda i: (i, 0)), pl.BlockSpec((T, CONV_CH), lambda i: (i, 0))],
        out_shape=[jax.ShapeDtypeStruct((S, CONV_CH), bf16), jax.ShapeDtypeStruct((S, CONV_CH), f32)],
        scratch_shapes=[pltpu.VMEM((T + HALO, CONV_CH), f32)],
        compiler_params=_params("parallel"),
    )(p, p, p, p, w, b, lng, lnb)


def conv_bwd(p, y, dcat, w, lng, lnb, name):
    S = p.shape[0]
    T = _tile(S, 256)
    n = S // T
    r = T // HALO
    dcb = Q_END // CONV_CH

    def body(a1_ref, a2_ref, h1_ref, h2_ref, y_ref, yn_ref, do_ref, don_ref, w_ref, g_ref, bb_ref,
             da_ref, dw_ref, db_ref, dg_ref, dbb_ref, scr_h, scr_dy, acc_b, acc_g, acc_bb):
        i = pl.program_id(0)

        @pl.when(i == 0)
        def _():
            dw_ref[...] = jnp.zeros_like(dw_ref)
            acc_b[...] = jnp.zeros_like(acc_b)
            acc_g[...] = jnp.zeros_like(acc_g)
            acc_bb[...] = jnp.zeros_like(acc_bb)

        gv, bv = g_ref[...], bb_ref[...]

        def ln_silu_bwd(yv, dout):
            yh, rstd = _ln_stats(yv)
            z = yh * gv + bv
            sg = _sig(z)
            dz = dout * (sg * (1.0 + z * (1.0 - sg)))
            gz = dz * gv
            dy = rstd * (gz - jnp.mean(gz, axis=-1, keepdims=True) - yh * jnp.mean(gz * yh, axis=-1, keepdims=True))
            return dy, dz, yh

        dy, dz, yh = ln_silu_bwd(y_ref[...], do_ref[...])
        dyn, _, _ = ln_silu_bwd(yn_ref[...], don_ref[...])
        acc_g[...] += _rowsum8(dz * yh)
        acc_bb[...] += _rowsum8(dz)
        acc_b[...] += _rowsum8(dy)
        scr_dy[pl.ds(0, T), :] = dy
        scr_dy[pl.ds(T, HALO), :] = jnp.where(i < n - 1, dyn, 0.0)
        a1, a2 = a1_ref[...], a2_ref[...]
        sg2 = _sig(a2)
        halo = h1_ref[...] * _sig(h2_ref[...])
        scr_h[pl.ds(0, HALO), :] = jnp.where(i > 0, halo, 0.0)
        scr_h[pl.ds(HALO, T), :] = a1 * sg2
        dh = jnp.zeros((T, CONV_CH), f32)
        for j in range(CONV_W):
            dh = dh + scr_dy[pl.ds(CONV_W - 1 - j, T), :] * w_ref[j:j + 1, :]
            dw_ref[j:j + 1, :] += jnp.sum(dy * scr_h[pl.ds(HALO - (CONV_W - 1) + j, T), :], axis=0, keepdims=True)
        da_ref[:, :CONV_CH] = (dh * sg2).astype(bf16)
        da_ref[:, CONV_CH:] = (dh * a1 * sg2 * (1.0 - sg2)).astype(bf16)

        @pl.when(i == n - 1)
        def _():
            db_ref[...] = jnp.sum(acc_b[...], axis=0, keepdims=True)
            dg_ref[...] = jnp.sum(acc_g[...], axis=0, keepdims=True)
            dbb_ref[...] = jnp.sum(acc_bb[...], axis=0, keepdims=True)

    vec = pl.BlockSpec((1, CONV_CH), lambda i: (0, 0))
    tap = pl.BlockSpec((HALO, CONV_CH), lambda i: (0, 0))
    prev_map = lambda i: jnp.maximum(i * r - 1, 0)
    next_map = lambda i: jnp.minimum((i + 1) * r, S // HALO - 1)
    return pl.pallas_call(
        body, name=name, grid=(n,),
        in_specs=[pl.BlockSpec((T, CONV_CH), lambda i: (i, A1_BLK)), pl.BlockSpec((T, CONV_CH), lambda i: (i, A2_BLK)),
                  pl.BlockSpec((HALO, CONV_CH), lambda i: (prev_map(i), A1_BLK)),
                  pl.BlockSpec((HALO, CONV_CH), lambda i: (prev_map(i), A2_BLK)),
                  pl.BlockSpec((T, CONV_CH), lambda i: (i, 0)),
                  pl.BlockSpec((HALO, CONV_CH), lambda i: (next_map(i), 0)),
                  pl.BlockSpec((T, CONV_CH), lambda i: (i, dcb)),
                  pl.BlockSpec((HALO, CONV_CH), lambda i: (next_map(i), dcb)),
                  tap, vec, vec],
        out_specs=[pl.BlockSpec((T, 2 * CONV_CH), lambda i: (i, 0)), tap, vec, vec, vec],
        out_shape=[jax.ShapeDtypeStruct((S, 2 * CONV_CH), bf16), jax.ShapeDtypeStruct((HALO, CONV_CH), f32),
                   jax.ShapeDtypeStruct((1, CONV_CH), f32), jax.ShapeDtypeStruct((1, CONV_CH), f32),
                   jax.ShapeDtypeStruct((1, CONV_CH), f32)],
        scratch_shapes=[pltpu.VMEM((T + HALO, CONV_CH), f32), pltpu.VMEM((T + HALO, CONV_CH), f32),
                        pltpu.VMEM((8, CONV_CH), f32), pltpu.VMEM((8, CONV_CH), f32), pltpu.VMEM((8, CONV_CH), f32)],
        compiler_params=_params("arbitrary"),
    )(p, p, p, p, y, y, dcat, dcat, w, lng, lnb)


U_BLK = (V_END + 2 * CONV_CH) // SGU_CH
SV_BLK = U_BLK + 1


def _tril(w, transposed=False):
    row = lax.broadcasted_iota(jnp.int32, (BLK, BLK), 0)
    col = lax.broadcasted_iota(jnp.int32, (BLK, BLK), 1)
    keep = (col >= row) if transposed else (row >= col)
    return jnp.where(keep, w, 0.0)


def sgu_fwd(p, lng, lnb, w, bias, name):
    S = p.shape[0]
    T = _tile(S, 256)

    def body(u_ref, v_ref, g_ref, bb_ref, w_ref, bias_ref, o_ref):
        yh, _ = _ln_stats(v_ref[...])
        v = (yh * g_ref[...] + bb_ref[...]).astype(bf16)
        low = lax.broadcasted_iota(jnp.int32, (BLK, LANE), 1) < HEAD_DIM
        for pr in range(SGU_HEADS // 2):
            lanes = pl.ds(pr * LANE, LANE)
            w0 = _tril(w_ref[2 * pr]).astype(bf16)
            w1 = _tril(w_ref[2 * pr + 1]).astype(bf16)
            for c in range(T // BLK):
                rows = pl.ds(c * BLK, BLK)
                vp = v[c * BLK:(c + 1) * BLK, pr * LANE:(pr + 1) * LANE]
                mixed = jnp.where(low, _dot(w0, vp), _dot(w1, vp)) + bias_ref[:, lanes]
                o_ref[rows, lanes] = (u_ref[rows, lanes] * mixed).astype(bf16)

    vec = pl.BlockSpec((1, SGU_CH), lambda i: (0, 0))
    return pl.pallas_call(
        body, name=name, grid=(S // T,),
        in_specs=[pl.BlockSpec((T, SGU_CH), lambda i: (i, U_BLK)), pl.BlockSpec((T, SGU_CH), lambda i: (i, SV_BLK)),
                  vec, vec, pl.BlockSpec((SGU_HEADS, BLK, BLK), lambda i: (0, 0, 0)),
                  pl.BlockSpec((BLK, SGU_CH), lambda i: (0, 0))],
        out_specs=pl.BlockSpec((T, SGU_CH), lambda i: (i, 0)),
        out_shape=jax.ShapeDtypeStruct((S, SGU_CH), bf16),
        compiler_params=_params("parallel"),
    )(p, p, lng, lnb, w, bias)


def sgu_bwd(p, dcat, lng, lnb, w, wt, bias, name):
    S = p.shape[0]
    T = _tile(S, 256)
    n = S // T
    dsb = (Q_END + CONV_CH) // SGU_CH

    def body(u_ref, v_ref, do_ref, g_ref, bb_ref, w_ref, wt_ref, bias_ref,
             da_ref, dw_ref, db_ref, dg_ref, dbb_ref, dv_scr, acc_bias, acc_g, acc_bb):
        i = pl.program_id(0)

        @pl.when(i == 0)
        def _():
            dw_ref[...] = jnp.zeros_like(dw_ref)
            acc_bias[...] = jnp.zeros_like(acc_bias)
            acc_g[...] = jnp.zeros_like(acc_g)
            acc_bb[...] = jnp.zeros_like(acc_bb)

        gv = g_ref[...]
        yh, rstd = _ln_stats(v_ref[...])
        v = (yh * gv + bb_ref[...]).astype(bf16)
        low = lax.broadcasted_iota(jnp.int32, (BLK, LANE), 1) < HEAD_DIM
        for pr in range(SGU_HEADS // 2):
            lanes = pl.ds(pr * LANE, LANE)
            w0 = _tril(w_ref[2 * pr]).astype(bf16)
            w1 = _tril(w_ref[2 * pr + 1]).astype(bf16)
            wt0 = _tril(wt_ref[2 * pr], True).astype(bf16)
            wt1 = _tril(wt_ref[2 * pr + 1], True).astype(bf16)
            dw0 = jnp.zeros((BLK, BLK), f32)
            dw1 = jnp.zeros((BLK, BLK), f32)
            for c in range(T // BLK):
                rows = pl.ds(c * BLK, BLK)
                vp = v[c * BLK:(c + 1) * BLK, pr * LANE:(pr + 1) * LANE]
                mixed = jnp.where(low, _dot(w0, vp), _dot(w1, vp)) + bias_ref[:, lanes]
                do = do_ref[rows, lanes]
                da_ref[rows, lanes] = (do * mixed).astype(bf16)
                dm = do * u_ref[rows, lanes]
                acc_bias[:, lanes] += dm
                dmb = dm.astype(bf16)
                dv_scr[rows, lanes] = jnp.where(low, _dot(wt0, dmb), _dot(wt1, dmb))
                zero = jnp.zeros_like(dmb)
                dw0 = dw0 + _dot(jnp.where(low, dmb, zero), vp, NT)
                dw1 = dw1 + _dot(jnp.where(low, zero, dmb), vp, NT)
            dw_ref[2 * pr] += _tril(dw0)
            dw_ref[2 * pr + 1] += _tril(dw1)
        dv = dv_scr[...]
        acc_g[...] += _rowsum8(dv * yh)
        acc_bb[...] += _rowsum8(dv)
        gz = dv * gv
        dvr = rstd * (gz - jnp.mean(gz, axis=-1, keepdims=True) - yh * jnp.mean(gz * yh, axis=-1, keepdims=True))
        da_ref[:, SGU_CH:] = dvr.astype(bf16)

        @pl.when(i == n - 1)
        def _():
            ch = lax.broadcasted_iota(jnp.int32, (SGU_CH, LANE), 0) // HEAD_DIM
            hd = lax.broadcasted_iota(jnp.int32, (SGU_CH, LANE), 1)
            fold = jnp.where(ch == hd, 1.0, 0.0).astype(f32)
            db_ref[...] = jnp.dot(acc_bias[...], fold, preferred_element_type=f32, precision=lax.Precision.HIGHEST)
            dg_ref[...] = jnp.sum(acc_g[...], axis=0, keepdims=True)
            dbb_ref[...] = jnp.sum(acc_bb[...], axis=0, keepdims=True)

    vec = pl.BlockSpec((1, SGU_CH), lambda i: (0, 0))
    wsp = pl.BlockSpec((SGU_HEADS, BLK, BLK), lambda i: (0, 0, 0))
    return pl.pallas_call(
        body, name=name, grid=(n,),
        in_specs=[pl.BlockSpec((T, SGU_CH), lambda i: (i, U_BLK)), pl.BlockSpec((T, SGU_CH), lambda i: (i, SV_BLK)),
                  pl.BlockSpec((T, SGU_CH), lambda i: (i, dsb)), vec, vec, wsp, wsp,
                  pl.BlockSpec((BLK, SGU_CH), lambda i: (0, 0))],
        out_specs=[pl.BlockSpec((T, 2 * SGU_CH), lambda i: (i, 0)), wsp,
                   pl.BlockSpec((BLK, LANE), lambda i: (0, 0)), vec, vec],
        out_shape=[jax.ShapeDtypeStruct((S, 2 * SGU_CH), bf16), jax.ShapeDtypeStruct((SGU_HEADS, BLK, BLK), f32),
                   jax.ShapeDtypeStruct((BLK, LANE), f32), jax.ShapeDtypeStruct((1, SGU_CH), f32),
                   jax.ShapeDtypeStruct((1, SGU_CH), f32)],
        scratch_shapes=[pltpu.VMEM((T, SGU_CH), f32), pltpu.VMEM((BLK, SGU_CH), f32),
                        pltpu.VMEM((8, SGU_CH), f32), pltpu.VMEM((8, SGU_CH), f32)],
        compiler_params=_params("arbitrary"),
    )(p, p, dcat, lng, lnb, w, wt, bias)


def _me_and_peers():
    x, y, c = lax.axis_index("x"), lax.axis_index("y"), lax.axis_index("c")
    me = 4 * x + 2 * y + c
    peers = []
    for k in range(1, N_DEV):
        px, py, pc = x ^ (k >> 2), y ^ ((k >> 1) & 1), c ^ (k & 1)
        peers.append(((px, py, pc), 4 * px + 2 * py + pc))
    return me, peers


def exchange(arrs, gather, name):
    n_arr = len(arrs)

    def body(*refs):
        srcs, dsts = refs[:n_arr], refs[n_arr:2 * n_arr]
        send_sems, recv_sems, local_sems = refs[2 * n_arr:]
        me, peers = _me_and_peers()
        copies = []
        for a in range(n_arr):
            own = srcs[a] if gather else srcs[a].at[me]
            loc = pltpu.make_async_copy(own, dsts[a].at[me], local_sems.at[a])
            loc.start()
            copies.append(loc)
        sends = []
        for k, (dev, idx) in enumerate(peers):
            for a in range(n_arr):
                cp = pltpu.make_async_remote_copy(
                    src_ref=srcs[a] if gather else srcs[a].at[idx], dst_ref=dsts[a].at[me],
                    send_sem=send_sems.at[a, k], recv_sem=recv_sems.at[a, k], device_id=dev, device_id_type=MESH)
                cp.start()
                sends.append(cp)
        for k, (dev, idx) in enumerate(peers):
            for a in range(n_arr):
                pltpu.make_async_remote_copy(
                    src_ref=srcs[a] if gather else srcs[a].at[me], dst_ref=dsts[a].at[idx],
                    send_sem=send_sems.at[a, k], recv_sem=recv_sems.at[a, k], device_id=dev,
                    device_id_type=MESH).wait_recv()
        for cp in sends:
            cp.wait_send()
        for cp in copies:
            cp.wait()

    any_spec = pl.BlockSpec(memory_space=pl.ANY)
    out_shape = [jax.ShapeDtypeStruct(((N_DEV,) + a.shape) if gather else a.shape, a.dtype) for a in arrs]
    return pl.pallas_call(
        body, name=name,
        in_specs=[any_spec] * n_arr, out_specs=[any_spec] * n_arr, out_shape=out_shape,
        scratch_shapes=[pltpu.SemaphoreType.DMA((n_arr, N_DEV - 1)), pltpu.SemaphoreType.DMA((n_arr, N_DEV - 1)),
                        pltpu.SemaphoreType.DMA((n_arr,))],
        compiler_params=pltpu.CompilerParams(has_side_effects=True),
    )(*arrs)


def allreduce_small(buf, name):
    R = buf.shape[0]

    def body(src, out, land, send_sems, recv_sems):
        me, peers = _me_and_peers()
        land[me] = src[...]
        sends = []
        for k, (dev, idx) in enumerate(peers):
            cp = pltpu.make_async_remote_copy(src_ref=src, dst_ref=land.at[me], send_sem=send_sems.at[k],
                                              recv_sem=recv_sems.at[k], device_id=dev, device_id_type=MESH)
            cp.start()
            sends.append(cp)
        for k, (dev, idx) in enumerate(peers):
            pltpu.make_async_remote_copy(src_ref=src, dst_ref=land.at[idx], send_sem=send_sems.at[k],
                                         recv_sem=recv_sems.at[k], device_id=dev, device_id_type=MESH).wait_recv()
        for cp in sends:
            cp.wait_send()
        total = land[0]
        for j in range(1, N_DEV):
            total = total + land[j]
        out[...] = total

    vm = pl.BlockSpec(memory_space=pltpu.VMEM)
    return pl.pallas_call(
        body, name=name, in_specs=[vm], out_specs=vm, out_shape=jax.ShapeDtypeStruct((R, LANE), f32),
        scratch_shapes=[pltpu.VMEM((N_DEV, R, LANE), f32), pltpu.SemaphoreType.DMA((N_DEV - 1,)),
                        pltpu.SemaphoreType.DMA((N_DEV - 1,))],
        compiler_params=pltpu.CompilerParams(has_side_effects=True, vmem_limit_bytes=VMEM_LIMIT),
    )(buf)


def adamw(parts, w, m, v, name):
    L, R, C = w.shape
    P = parts[0].shape[0]
    tr = _tile(R, 128 if C > 1024 else 256)
    nr = R // tr
    c1 = 1.0 - ADAM_B1 ** ADAM_STEP
    c2 = 1.0 - ADAM_B2 ** ADAM_STEP

    def body(*refs):
        part_refs = refs[:L]
        w_ref, m_ref, v_ref, g_out, d_out, m_out, v_out = refs[L:]
        layer = pl.program_id(0)
        for l in range(L):
            @pl.when(layer == l)
            def _(l=l):
                g = part_refs[l][0].astype(f32)
                for q in range(1, P):
                    g = g + part_refs[l][q].astype(f32)
                mn = ADAM_B1 * m_ref[...] + (1.0 - ADAM_B1) * g
                vn = ADAM_B2 * v_ref[...] + (1.0 - ADAM_B2) * (g * g)
                g_out[...] = g
                m_out[...] = mn
                v_out[...] = vn
                d_out[...] = -ADAM_LR * ((mn / c1) / (jnp.sqrt(vn / c2) + ADAM_EPS) + ADAM_WD * w_ref[...])

    def part_spec(l):
        return pl.BlockSpec((P, tr, C), lambda a, i: (0, jnp.where(a == l, i, jnp.where(a < l, 0, nr - 1)), 0))

    slab = pl.BlockSpec((None, tr, C), lambda a, i: (a, i, 0))
    out = jax.ShapeDtypeStruct((L, R, C), f32)
    return pl.pallas_call(
        body, name=name, grid=(L, nr),
        in_specs=[part_spec(l) for l in range(L)] + [slab, slab, slab],
        out_specs=[slab, slab, slab, slab], out_shape=[out, out, out, out],
        compiler_params=_params("arbitrary", "arbitrary"),
    )(*parts, w, m, v)


PACK = 8 * LANE


def _pack(arrs):
    pieces = []
    for a in arrs:
        flat = a.astype(f32).reshape(-1)
        pad = (-flat.shape[0]) % PACK
        pieces.append(jnp.pad(flat, (0, pad)).reshape(-1, LANE))
    return jnp.concatenate(pieces, axis=0)


def _unpack(buf, shapes):
    out, row = [], 0
    for shp in shapes:
        size = math.prod(shp)
        rows = (size + PACK - 1) // PACK * (PACK // LANE)
        out.append(buf[row:row + rows].reshape(-1)[:size].reshape(shp))
        row += rows
    return out


def _rope_tables(positions):
    half = ROT_DIM // 2
    inv_freq = 1.0 / (ROPE_THETA ** (jnp.arange(0, ROT_DIM, 2, dtype=f32) / ROT_DIM))
    ang = positions.astype(f32)[:, None] * inv_freq
    cos, sin = jnp.cos(ang), jnp.sin(ang)
    S = positions.shape[0]
    zeros, ones = jnp.zeros((S, half), f32), jnp.ones((S, HEAD_DIM - ROT_DIM), f32)
    rest = jnp.zeros((S, HEAD_DIM - ROT_DIM), f32)
    c = jnp.concatenate([cos, cos, ones], axis=1)
    s1 = jnp.concatenate([zeros, sin, rest], axis=1)
    s2 = jnp.concatenate([-sin, zeros, rest], axis=1)
    return tuple(jnp.tile(t, (1, LANE // HEAD_DIM)) for t in (c, s1, s2))


def _cols_to_shards(g):
    lead, (R, N) = g.shape[:-2], g.shape[-2:]
    g = g.reshape(lead + (R, N_DEV, N // N_DEV))
    return jnp.moveaxis(g, -2, 0)


def _shards_to_cols(g):
    g = jnp.moveaxis(g, 0, -2)
    return g.reshape(g.shape[:-2] + (g.shape[-2] * g.shape[-1],))


def kernel(x, positions, norm_ffn1, ffn1_w_in, ffn1_w_out, norm_mix, w_in, conv_dw_w, conv_dw_b, conv_ln_g, conv_ln_b, sgu_ln_g, sgu_ln_b, sgu_w, sgu_b, attn_sinks, w_out, norm_ffn2, ffn2_w_in, ffn2_w_out, final_norm, loss_target, m_norm_ffn1, m_ffn1_w_in, m_ffn1_w_out, m_norm_mix, m_w_in, m_conv_dw_w, m_conv_dw_b, m_conv_ln_g, m_conv_ln_b, m_sgu_ln_g, m_sgu_ln_b, m_sgu_w, m_sgu_b, m_attn_sinks, m_w_out, m_norm_ffn2, m_ffn2_w_in, m_ffn2_w_out, m_final_norm, v_norm_ffn1, v_ffn1_w_in, v_ffn1_w_out, v_norm_mix, v_w_in, v_conv_dw_w, v_conv_dw_b, v_conv_ln_g, v_conv_ln_b, v_sgu_ln_g, v_sgu_ln_b, v_sgu_w, v_sgu_b, v_attn_sinks, v_w_out, v_norm_ffn2, v_ffn2_w_in, v_ffn2_w_out, v_final_norm):
    L = norm_ffn1.shape[0]
    S, D = x.shape[1], x.shape[2]
    F = ffn1_w_out.shape[1] * N_DEV
    me = 4 * lax.axis_index("x") + 2 * lax.axis_index("y") + lax.axis_index("c")
    x0 = x[0]
    rope_c, rope_s1, rope_s2 = _rope_tables(positions[0])
    cw = CONV_CH // N_DEV

    weights = []
    for l in range(L):
        taps = jnp.pad(conv_dw_w[l], ((0, HALO - CONV_W), (0, LANE - cw)))
        got = exchange([ffn1_w_in[l].astype(bf16), ffn1_w_out[l].astype(bf16), w_in[l].astype(bf16),
                        w_out[l].astype(bf16), ffn2_w_in[l].astype(bf16), ffn2_w_out[l].astype(bf16), taps],
                       True, f"gather_weights_{l}")
        f1i, f1o, wi, wo, f2i, f2o, tp = got
        weights.append(dict(
            f1_in=_shards_to_cols(f1i).reshape(D, 2, F).transpose(1, 0, 2),
            f1_out=f1o.reshape(F, D),
            w_in=_shards_to_cols(wi),
            w_out=wo.reshape(D, D),
            f2_in=_shards_to_cols(f2i).reshape(D, 2, F).transpose(1, 0, 2),
            f2_out=f2o.reshape(F, D),
            taps=_shards_to_cols(tp[:, :, :cw]),
        ))

    row = lambda a: a.reshape(1, -1)

    saved = []
    xs = x0
    for l in range(L):
        wts = weights[l]
        sv = dict(x_a=xs)
        h = rmsnorm_fwd(xs, row(norm_ffn1[l]), f"norm_ffn1_{l}")
        sv["h1"] = h
        sv["gu1"], sv["a1"] = ffn_in(h, wts["f1_in"], f"ffn1_in_{l}")
        xs = mm_res(sv["a1"], wts["f1_out"], xs, FFN_RES, f"ffn1_out_{l}")
        sv["x_b"] = xs
        h = rmsnorm_fwd(xs, row(norm_mix[l]), f"norm_mix_{l}")
        sv["h2"] = h
        p = mm_nn(h, wts["w_in"], f"mix_in_{l}")
        sv["p"] = p
        attn = attn_fwd(p, rope_c, rope_s1, rope_s2, attn_sinks[l], f"attn_fwd_{l}")
        conv, sv["conv_y"] = conv_fwd(p, wts["taps"], row(conv_dw_b[l]), row(conv_ln_g[l]), row(conv_ln_b[l]),
                                      f"conv_fwd_{l}")
        sv["sgu_bias"] = jnp.repeat(sgu_b[l].T, HEAD_DIM, axis=1)
        sgu = sgu_fwd(p, row(sgu_ln_g[l]), row(sgu_ln_b[l]), sgu_w[l], sv["sgu_bias"], f"sgu_fwd_{l}")
        cat = jnp.concatenate([attn, conv, sgu], axis=1)
        sv["cat"] = cat
        xs = mm_res(cat, wts["w_out"], xs, 1.0, f"mix_out_{l}")
        sv["x_c"] = xs
        h = rmsnorm_fwd(xs, row(norm_ffn2[l]), f"norm_ffn2_{l}")
        sv["h3"] = h
        sv["gu2"], sv["a2"] = ffn_in(h, wts["f2_in"], f"ffn2_in_{l}")
        xs = mm_res(sv["a2"], wts["f2_out"], xs, FFN_RES, f"ffn2_out_{l}")
        saved.append(sv)

    dx, dxb, d_final_norm, loss = final_loss(xs, row(final_norm), loss_target[0], "final_loss")

    small = [None] * L
    recv = [None] * L
    for l in reversed(range(L)):
        wts, sv = weights[l], saved[l]

        def ffn_bwd(dx, dxb, h, gu, a, w_in2, w_out2, x_in, g_norm, tag):
            dgu = ffn_dact(dxb, w_out2, gu, f"{tag}_dact_{l}")
            d_w_out = mm_tn(a, (FFN_RES * dxb)[None], f"{tag}_dwout_{l}")[0]
            d_w_in = mm_tn(h, dgu, f"{tag}_dwin_{l}")
            dh = mm_nt(dgu, w_in2, f"{tag}_dh_{l}")
            dx, dxb, dg = rmsnorm_bwd(dh, x_in, g_norm, dx, f"{tag}_dnorm_{l}")
            return dx, dxb, d_w_in, d_w_out, dg

        dx, dxb, d_f2_in, d_f2_out, d_norm_ffn2 = ffn_bwd(
            dx, dxb, sv["h3"], sv["gu2"], sv["a2"], wts["f2_in"], wts["f2_out"], sv["x_c"], row(norm_ffn2[l]), "ffn2")

        d_w_out = mm_tn(sv["cat"], dxb[None], f"mix_dwout_{l}")[0]
        dcat = mm_nt(dxb[None], wts["w_out"][None], f"mix_dcat_{l}")
        p = sv["p"]
        dq, dkv, d_sinks = attn_bwd(p, dcat, rope_c, rope_s1, rope_s2, attn_sinks[l], f"attn_bwd_{l}")
        da_conv, d_taps, d_conv_b, d_conv_g, d_conv_bb = conv_bwd(
            p, sv["conv_y"], dcat, wts["taps"], row(conv_ln_g[l]), row(conv_ln_b[l]), f"conv_bwd_{l}")
        da_sgu, d_sgu_w, d_sgu_bias, d_sgu_g, d_sgu_bb = sgu_bwd(
            p, dcat, row(sgu_ln_g[l]), row(sgu_ln_b[l]), sgu_w[l], jnp.swapaxes(sgu_w[l], 1, 2), sv["sgu_bias"],
            f"sgu_bwd_{l}")
        dp = jnp.concatenate([dq, dkv, da_conv, da_sgu], axis=1)
        d_w_in = mm_tn(sv["h2"], dp[None], f"mix_dwin_{l}")[0]
        dh = mm_nt(dp[None], wts["w_in"][None], f"mix_dh_{l}")
        dx, dxb, d_norm_mix = rmsnorm_bwd(dh, sv["x_b"], row(norm_mix[l]), dx, f"mix_dnorm_{l}")

        dx, dxb, d_f1_in, d_f1_out, d_norm_ffn1 = ffn_bwd(
            dx, dxb, sv["h1"], sv["gu1"], sv["a1"], wts["f1_in"], wts["f1_out"], sv["x_a"], row(norm_ffn1[l]), "ffn1")

        small[l] = dict(norm_ffn1=d_norm_ffn1[0], norm_mix=d_norm_mix[0], conv_dw_w=d_taps[:CONV_W],
                        conv_dw_b=d_conv_b[0], conv_ln_g=d_conv_g[0], conv_ln_b=d_conv_bb[0], sgu_ln_g=d_sgu_g[0],
                        sgu_ln_b=d_sgu_bb[0], sgu_w=d_sgu_w, sgu_b=d_sgu_bias[:, :SGU_HEADS].T,
                        attn_sinks=d_sinks[0, :N_Q_HEADS], norm_ffn2=d_norm_ffn2[0])

        def in_shards(d2):
            return _cols_to_shards(d2.transpose(1, 0, 2).reshape(D, 2 * F))

        recv[l] = exchange(
            [in_shards(d_f1_in), d_f1_out.reshape(N_DEV, F // N_DEV, D), _cols_to_shards(d_w_in),
             d_w_out.reshape(N_DEV, D // N_DEV, D), in_shards(d_f2_in), d_f2_out.reshape(N_DEV, F // N_DEV, D)],
            False, f"scatter_grads_{l}")

    grad_x = dx[None]

    big = {}
    for idx, (name, w, m, v) in enumerate([
            ("ffn1_w_in", ffn1_w_in, m_ffn1_w_in, v_ffn1_w_in), ("ffn1_w_out", ffn1_w_out, m_ffn1_w_out, v_ffn1_w_out),
            ("w_in", w_in, m_w_in, v_w_in), ("w_out", w_out, m_w_out, v_w_out),
            ("ffn2_w_in", ffn2_w_in, m_ffn2_w_in, v_ffn2_w_in), ("ffn2_w_out", ffn2_w_out, m_ffn2_w_out, v_ffn2_w_out)]):
        big[name] = adamw([recv[l][idx] for l in range(L)], w, m, v, f"adamw_{name}")

    small_names = ["norm_ffn1", "norm_mix", "conv_dw_w", "conv_dw_b", "conv_ln_g", "conv_ln_b", "sgu_ln_g", "sgu_ln_b",
                   "sgu_w", "sgu_b", "attn_sinks", "norm_ffn2"]
    local = [jnp.stack([small[l][n] for l in range(L)]) for n in small_names] + [d_final_norm[0], loss.reshape(1)]
    shapes = [a.shape for a in local]
    total = _unpack(allreduce_small(_pack(local), "sum_small_grads"), shapes)
    g_small = dict(zip(small_names + ["final_norm", "loss"], total))
    g_small["conv_dw_w"] = lax.dynamic_slice_in_dim(g_small["conv_dw_w"], me * cw, cw, axis=2)
    given = dict(norm_ffn1=(norm_ffn1, m_norm_ffn1, v_norm_ffn1), norm_mix=(norm_mix, m_norm_mix, v_norm_mix),
                 conv_dw_w=(conv_dw_w, m_conv_dw_w, v_conv_dw_w), conv_dw_b=(conv_dw_b, m_conv_dw_b, v_conv_dw_b),
                 conv_ln_g=(conv_ln_g, m_conv_ln_g, v_conv_ln_g), conv_ln_b=(conv_ln_b, m_conv_ln_b, v_conv_ln_b),
                 sgu_ln_g=(sgu_ln_g, m_sgu_ln_g, v_sgu_ln_g), sgu_ln_b=(sgu_ln_b, m_sgu_ln_b, v_sgu_ln_b),
                 sgu_w=(sgu_w, m_sgu_w, v_sgu_w), sgu_b=(sgu_b, m_sgu_b, v_sgu_b),
                 attn_sinks=(attn_sinks, m_attn_sinks, v_attn_sinks), norm_ffn2=(norm_ffn2, m_norm_ffn2, v_norm_ffn2),
                 final_norm=(final_norm, m_final_norm, v_final_norm))
    upd_names = small_names + ["final_norm"]
    upd_shapes = [given[n][0].shape for n in upd_names]
    packed = [_pack([g_small[n] for n in upd_names])[None]] + [_pack([given[n][k] for n in upd_names])[None] for k in range(3)]
    res = adamw([packed[0]], packed[1], packed[2], packed[3], "adamw_small")
    upd = [dict(zip(upd_names, _unpack(r[0], upd_shapes))) for r in res]

    order = ["norm_ffn1", "ffn1_w_in", "ffn1_w_out", "norm_mix", "w_in", "conv_dw_w", "conv_dw_b", "conv_ln_g",
             "conv_ln_b", "sgu_ln_g", "sgu_ln_b", "sgu_w", "sgu_b", "attn_sinks", "w_out", "norm_ffn2", "ffn2_w_in",
             "ffn2_w_out", "final_norm"]
    outs = [g_small["loss"].reshape(()), grad_x]
    for k in range(4):
        outs += [big[n][k] if n in big else upd[k][n] for n in order]
    return tuple(outs)
```

```python
import functools
import math

import jax
import jax.numpy as jnp
from jax import lax
from jax.experimental import pallas as pl
from jax.experimental.pallas import tpu as pltpu

f32 = jnp.float32
bf16 = jnp.bfloat16

N_DEV = 8
HEAD_DIM = 64
N_Q_HEADS = 16
N_KV_HEADS = 4
GQ = N_Q_HEADS // N_KV_HEADS
BLK = 128
ROT_DIM = 16
ROPE_THETA = 500000.0
CONV_W = 31
CONV_CH = 512
SGU_CH = 512
SGU_HEADS = 8
Q_END = N_Q_HEADS * HEAD_DIM
KV_W = 2 * N_KV_HEADS * HEAD_DIM
V_END = Q_END + KV_W
IN_COLS = V_END + 2 * CONV_CH + 2 * SGU_CH
HALO = 32
NORM_EPS = 1e-5
FFN_RES = 0.5
ADAM_LR, ADAM_B1, ADAM_B2, ADAM_EPS, ADAM_WD, ADAM_STEP = 0.001, 0.9, 0.999, 1e-08, 0.01, 10
LANE = 128
VMEM_LIMIT = 56 * 2 ** 20
MESH = pl.DeviceIdType.MESH

NN = (((1,), (0,)), ((), ()))
NT = (((1,), (1,)), ((), ()))
TN = (((0,), (0,)), ((), ()))


def _tile(dim, pref):
    t = min(pref, dim)
    while dim % t:
        t //= 2
    return t


def _params(*sem):
    return pltpu.CompilerParams(dimension_semantics=sem, vmem_limit_bytes=VMEM_LIMIT)


def _sig(x):
    return 1.0 / (1.0 + jnp.exp(-x))


def _dot(a, b, dims=NN):
    return lax.dot_general(a, b, dims, preferred_element_type=f32)


def _rowsum8(x):
    return x.reshape(x.shape[0] // 8, 8, x.shape[1]).sum(axis=0)


def rmsnorm_fwd(x, g, name):
    S, D = x.shape
    tm = _tile(S, 512)

    def body(x_ref, g_ref, o_ref, ot_ref):
        xv = x_ref[...]
        r = lax.rsqrt(jnp.mean(xv * xv, axis=-1, keepdims=True) + NORM_EPS)
        hb = (xv * r * g_ref[...]).astype(bf16)
        o_ref[...] = hb
        ot_ref[...] = hb.T

    return pl.pallas_call(
        body, name=name, grid=(S // tm,),
        in_specs=[pl.BlockSpec((tm, D), lambda i: (i, 0)), pl.BlockSpec((1, D), lambda i: (0, 0))],
        out_specs=[pl.BlockSpec((tm, D), lambda i: (i, 0)), pl.BlockSpec((D, tm), lambda i: (0, i))],
        out_shape=[jax.ShapeDtypeStruct((S, D), bf16), jax.ShapeDtypeStruct((D, S), bf16)],
        compiler_params=_params("parallel"),
    )(x, g)


def rmsnorm_bwd(dh, x, g, dres, name):
    S, D = x.shape
    tm = _tile(S, 256)
    n = S // tm

    def body(dh_ref, x_ref, g_ref, dres_ref, dx_ref, dxb_ref, dg_ref, acc):
        i = pl.program_id(0)

        @pl.when(i == 0)
        def _():
            acc[...] = jnp.zeros_like(acc)

        xv = x_ref[...]
        r = lax.rsqrt(jnp.mean(xv * xv, axis=-1, keepdims=True) + NORM_EPS)
        xh = xv * r
        dy = dh_ref[...]
        gy = dy * g_ref[...]
        dx = dres_ref[...] + r * (gy - xh * jnp.mean(gy * xh, axis=-1, keepdims=True))
        dx_ref[...] = dx
        dxb_ref[...] = dx.astype(bf16)
        acc[...] += _rowsum8(dy * xh)

        @pl.when(i == n - 1)
        def _():
            dg_ref[...] = jnp.sum(acc[...], axis=0, keepdims=True)

    row = pl.BlockSpec((tm, D), lambda i: (i, 0))
    vec = pl.BlockSpec((1, D), lambda i: (0, 0))
    return pl.pallas_call(
        body, name=name, grid=(n,),
        in_specs=[row, row, vec, row],
        out_specs=[row, row, vec],
        out_shape=[jax.ShapeDtypeStruct((S, D), f32), jax.ShapeDtypeStruct((S, D), bf16),
                   jax.ShapeDtypeStruct((1, D), f32)],
        scratch_shapes=[pltpu.VMEM((8, D), f32)],
        compiler_params=_params("arbitrary"),
    )(dh, x, g, dres)


def final_loss(x, g, tgt, name):
    S, D = x.shape
    tm = _tile(S, 256)
    n = S // tm

    def body(x_ref, g_ref, t_ref, dx_ref, dxb_ref, dg_ref, loss_ref, acc):
        i = pl.program_id(0)

        @pl.when(i == 0)
        def _():
            acc[...] = jnp.zeros_like(acc)
            loss_ref[...] = jnp.zeros_like(loss_ref)

        xv = x_ref[...]
        gv = g_ref[...]
        r = lax.rsqrt(jnp.mean(xv * xv, axis=-1, keepdims=True) + NORM_EPS)
        xh = xv * r
        diff = xh * gv - t_ref[...]
        tok = jnp.mean(diff * diff, axis=-1, keepdims=True)
        loss_ref[...] += 0.5 * jnp.sum(tok, axis=0, keepdims=True)
        dy = diff / D
        gy = dy * gv
        dx = r * (gy - xh * jnp.mean(gy * xh, axis=-1, keepdims=True))
        dx_ref[...] = dx
        dxb_ref[...] = dx.astype(bf16)
        acc[...] += _rowsum8(dy * xh)

        @pl.when(i == n - 1)
        def _():
            dg_ref[...] = jnp.sum(acc[...], axis=0, keepdims=True)

    row = pl.BlockSpec((tm, D), lambda i: (i, 0))
    vec = pl.BlockSpec((1, D), lambda i: (0, 0))
    return pl.pallas_call(
        body, name=name, grid=(n,),
        in_specs=[row, vec, row],
        out_specs=[row, row, vec, pl.BlockSpec((1, 1), lambda i: (0, 0))],
        out_shape=[jax.ShapeDtypeStruct((S, D), f32), jax.ShapeDtypeStruct((S, D), bf16),
                   jax.ShapeDtypeStruct((1, D), f32), jax.ShapeDtypeStruct((1, 1), f32)],
        scratch_shapes=[pltpu.VMEM((8, D), f32)],
        compiler_params=_params("arbitrary"),
    )(x, g, tgt)


def ffn_in(h, w2, name):
    S, D = h.shape
    F = w2.shape[2]
    tm, tn = _tile(S, 512), _tile(F, 512)

    def body(h_ref, w_ref, gu_ref, a_ref, at_ref):
        hv = h_ref[...]
        g = _dot(hv, w_ref[0])
        u = _dot(hv, w_ref[1])
        gu_ref[0] = g.astype(bf16)
        gu_ref[1] = u.astype(bf16)
        a = (g * _sig(g) * u).astype(bf16)
        a_ref[...] = a
        at_ref[...] = a.T

    return pl.pallas_call(
        body, name=name, grid=(S // tm, F // tn),
        in_specs=[pl.BlockSpec((tm, D), lambda i, j: (i, 0)), pl.BlockSpec((2, D, tn), lambda i, j: (0, 0, j))],
        out_specs=[pl.BlockSpec((2, tm, tn), lambda i, j: (0, i, j)), pl.BlockSpec((tm, tn), lambda i, j: (i, j)),
                   pl.BlockSpec((tn, tm), lambda i, j: (j, i))],
        out_shape=[jax.ShapeDtypeStruct((2, S, F), bf16), jax.ShapeDtypeStruct((S, F), bf16),
                   jax.ShapeDtypeStruct((F, S), bf16)],
        compiler_params=_params("parallel", "parallel"),
    )(h, w2)


def mm_res(a, w, x, scale, name):
    S, K = a.shape
    N = w.shape[1]
    tm, tn = _tile(S, 512), _tile(N, 512)

    def body(a_ref, w_ref, x_ref, o_ref):
        o_ref[...] = x_ref[...] + scale * _dot(a_ref[...], w_ref[...])

    return pl.pallas_call(
        body, name=name, grid=(S // tm, N // tn),
        in_specs=[pl.BlockSpec((tm, K), lambda i, j: (i, 0)), pl.BlockSpec((K, tn), lambda i, j: (0, j)),
                  pl.BlockSpec((tm, tn), lambda i, j: (i, j))],
        out_specs=pl.BlockSpec((tm, tn), lambda i, j: (i, j)),
        out_shape=jax.ShapeDtypeStruct((S, N), f32),
        compiler_params=_params("parallel", "parallel"),
    )(a, w, x)


def mm_nn(a, b, name, out_dtype=f32, scale=1.0):
    M, K = a.shape
    G, _, N = b.shape
    tm, tn = _tile(M, 512), _tile(N, 512)

    def body(a_ref, b_ref, o_ref):
        acc = _dot(a_ref[...], b_ref[...])
        o_ref[...] = (acc if scale == 1.0 else scale * acc).astype(out_dtype)

    return pl.pallas_call(
        body, name=name, grid=(G, M // tm, N // tn),
        in_specs=[pl.BlockSpec((tm, K), lambda g, i, j: (i, 0)), pl.BlockSpec((None, K, tn), lambda g, i, j: (g, 0, j))],
        out_specs=pl.BlockSpec((None, tm, tn), lambda g, i, j: (g, i, j)),
        out_shape=jax.ShapeDtypeStruct((G, M, N), out_dtype),
        compiler_params=_params("parallel", "parallel", "parallel"),
    )(a, b)


def mm_nt(a, w, name):
    G, S, K = a.shape
    N = w.shape[1]
    tm, tn = _tile(S, 512), _tile(N, 512)

    def body(a_ref, w_ref, o_ref):
        part = _dot(a_ref[...], w_ref[...], NT)
        if G == 1:
            o_ref[...] = part
        else:
            g = pl.program_id(2)

            @pl.when(g == 0)
            def _():
                o_ref[...] = part

            @pl.when(g > 0)
            def _():
                o_ref[...] += part

    return pl.pallas_call(
        body, name=name, grid=(S // tm, N // tn, G),
        in_specs=[pl.BlockSpec((None, tm, K), lambda i, j, g: (g, i, 0)),
                  pl.BlockSpec((None, tn, K), lambda i, j, g: (g, j, 0))],
        out_specs=pl.BlockSpec((tm, tn), lambda i, j, g: (i, j)),
        out_shape=jax.ShapeDtypeStruct((S, N), f32),
        compiler_params=_params("parallel", "parallel", "arbitrary"),
    )(a, w)


def ffn_dact(dx, wout, gu, name):
    S, D = dx.shape
    F = wout.shape[0]
    tm, tn = _tile(S, 512), _tile(F, 512)

    def body(dx_ref, w_ref, gu_ref, o_ref):
        da = FFN_RES * _dot(dx_ref[...], w_ref[...], NT)
        g = gu_ref[0].astype(f32)
        u = gu_ref[1].astype(f32)
        sg = _sig(g)
        o_ref[0] = (da * u * (sg * (1.0 + g * (1.0 - sg)))).astype(bf16)
        o_ref[1] = (da * (g * sg)).astype(bf16)

    return pl.pallas_call(
        body, name=name, grid=(S // tm, F // tn),
        in_specs=[pl.BlockSpec((tm, D), lambda i, j: (i, 0)), pl.BlockSpec((tn, D), lambda i, j: (j, 0)),
                  pl.BlockSpec((2, tm, tn), lambda i, j: (0, i, j))],
        out_specs=pl.BlockSpec((2, tm, tn), lambda i, j: (0, i, j)),
        out_shape=jax.ShapeDtypeStruct((2, S, F), bf16),
        compiler_params=_params("parallel", "parallel"),
    )(dx, wout, gu)


def _rope(t, c, s1, s2):
    w = t.shape[1]
    return t * c + pltpu.roll(t, 8, 1) * s1 + pltpu.roll(t, w - 8, 1) * s2


def _rope_t(d, c, s1, s2):
    w = d.shape[1]
    return d * c + pltpu.roll(d * s1, w - 8, 1) + pltpu.roll(d * s2, 8, 1)


def _attn_mask(n):
    qi = lax.broadcasted_iota(jnp.int32, (BLK, 2 * BLK), 0)
    kj = lax.broadcasted_iota(jnp.int32, (BLK, 2 * BLK), 1)
    dist = qi + BLK - kj
    return (dist >= 0) & (dist < BLK) & ((kj >= BLK) | (n > 0))


def _softmax_sink(s, valid, sk):
    s = jnp.where(valid, s, -1e30)
    m = jnp.maximum(jnp.max(s, axis=-1, keepdims=True), sk)
    e = jnp.exp(s - m)
    es = jnp.exp(sk - m)
    inv = 1.0 / (jnp.sum(e, axis=-1, keepdims=True) + es)
    return e * inv, es * inv


def attn_fwd(p, rope_c, rope_s1, rope_s2, sinks, name):
    S = p.shape[0]
    nb = S // BLK
    kvb = Q_END // KV_W

    def body(sink_ref, q_ref, kvc_ref, kvp_ref, cc_ref, s1c_ref, s2c_ref, cp_ref, s1p_ref, s2p_ref, o_ref):
        n = pl.program_id(0)
        cc, s1c, s2c = cc_ref[...], s1c_ref[...], s2c_ref[...]
        cp, s1p, s2p = cp_ref[...], s1p_ref[...], s2p_ref[...]
        q = _rope(q_ref[...], jnp.tile(cc, (1, 8)), jnp.tile(s1c, (1, 8)), jnp.tile(s2c, (1, 8)))
        kc = _rope(kvc_ref[:, :256], jnp.tile(cc, (1, 2)), jnp.tile(s1c, (1, 2)), jnp.tile(s2c, (1, 2)))
        kp = _rope(kvp_ref[:, :256], jnp.tile(cp, (1, 2)), jnp.tile(s1p, (1, 2)), jnp.tile(s2p, (1, 2)))
        k = jnp.concatenate([kp, kc], axis=0).astype(bf16)
        v = jnp.concatenate([kvp_ref[:, 256:], kvc_ref[:, 256:]], axis=0).astype(bf16)
        q = q.astype(bf16)
        valid = _attn_mask(n)
        for h in range(N_KV_HEADS):
            kh = k[:, h * HEAD_DIM:(h + 1) * HEAD_DIM]
            vh = v[:, h * HEAD_DIM:(h + 1) * HEAD_DIM]
            for g in range(GQ):
                hq = h * GQ + g
                qh = q[:, hq * HEAD_DIM:(hq + 1) * HEAD_DIM]
                s = _dot(qh, kh, NT) * (HEAD_DIM ** -0.5)
                pr, _ = _softmax_sink(s, valid, sink_ref[hq])
                o = _dot(pr.astype(bf16), vh)
                o_ref[:, hq * HEAD_DIM:(hq + 1) * HEAD_DIM] = o.astype(bf16)

    tab_c = pl.BlockSpec((BLK, LANE), lambda n: (n, 0))
    tab_p = pl.BlockSpec((BLK, LANE), lambda n: (jnp.maximum(n - 1, 0), 0))
    return pl.pallas_call(
        body, name=name, grid=(nb,),
        in_specs=[pl.BlockSpec(memory_space=pltpu.SMEM),
                  pl.BlockSpec((BLK, Q_END), lambda n: (n, 0)),
                  pl.BlockSpec((BLK, KV_W), lambda n: (n, kvb)),
                  pl.BlockSpec((BLK, KV_W), lambda n: (jnp.maximum(n - 1, 0), kvb)),
                  tab_c, tab_c, tab_c, tab_p, tab_p, tab_p],
        out_specs=pl.BlockSpec((BLK, Q_END), lambda n: (n, 0)),
        out_shape=jax.ShapeDtypeStruct((S, Q_END), bf16),
        compiler_params=_params("parallel"),
    )(sinks, p, p, p, rope_c, rope_s1, rope_s2, rope_c, rope_s1, rope_s2)


def attn_bwd(p, dcat, rope_c, rope_s1, rope_s2, sinks, name):
    S = p.shape[0]
    nb = S // BLK
    kvb = Q_END // KV_W

    def body(sink_ref, q_ref, kvc_ref, kvp_ref, do_ref, cc_ref, s1c_ref, s2c_ref, cp_ref, s1p_ref, s2p_ref,
             dq_ref, dkv_ref, dsink_ref, carry, dq_scr, dkv_scr):
        n = pl.program_id(0)

        @pl.when(n == 0)
        def _():
            carry[...] = jnp.zeros_like(carry)
            dsink_ref[...] = jnp.zeros_like(dsink_ref)

        cp, s1p, s2p = cp_ref[...], s1p_ref[...], s2p_ref[...]
        cp2, s1p2, s2p2 = jnp.tile(cp, (1, 2)), jnp.tile(s1p, (1, 2)), jnp.tile(s2p, (1, 2))

        @pl.when(n < nb)
        def _():
            cc, s1c, s2c = cc_ref[...], s1c_ref[...], s2c_ref[...]
            cc8, s1c8, s2c8 = jnp.tile(cc, (1, 8)), jnp.tile(s1c, (1, 8)), jnp.tile(s2c, (1, 8))
            q = _rope(q_ref[...], cc8, s1c8, s2c8).astype(bf16)
            kc = _rope(kvc_ref[:, :256], jnp.tile(cc, (1, 2)), jnp.tile(s1c, (1, 2)), jnp.tile(s2c, (1, 2)))
            kp = _rope(kvp_ref[:, :256], cp2, s1p2, s2p2)
            k = jnp.concatenate([kp, kc], axis=0).astype(bf16)
            v = jnp.concatenate([kvp_ref[:, 256:], kvc_ref[:, 256:]], axis=0).astype(bf16)
            do = do_ref[...].astype(bf16)
            valid = _attn_mask(n)
            lane = lax.broadcasted_iota(jnp.int32, (1, LANE), 1)
            dsink = jnp.zeros((1, LANE), f32)
            for h in range(N_KV_HEADS):
                kh = k[:, h * HEAD_DIM:(h + 1) * HEAD_DIM]
                vh = v[:, h * HEAD_DIM:(h + 1) * HEAD_DIM]
                dkh = jnp.zeros((2 * BLK, HEAD_DIM), f32)
                dvh = jnp.zeros((2 * BLK, HEAD_DIM), f32)
                for g in range(GQ):
                    hq = h * GQ + g
                    qh = q[:, hq * HEAD_DIM:(hq + 1) * HEAD_DIM]
                    doh = do[:, hq * HEAD_DIM:(hq + 1) * HEAD_DIM]
                    s = _dot(qh, kh, NT) * (HEAD_DIM ** -0.5)
                    pr, ps = _softmax_sink(s, valid, sink_ref[hq])
                    dpr = _dot(doh, vh, NT)
                    dvh = dvh + _dot(pr.astype(bf16), doh, TN)
                    row = jnp.sum(pr * dpr, axis=-1, keepdims=True)
                    ds = (pr * (dpr - row) * (HEAD_DIM ** -0.5)).astype(bf16)
                    dsink = dsink + jnp.where(lane == hq, -jnp.sum(ps * row, axis=0, keepdims=True), 0.0)
                    dq_scr[:, hq * HEAD_DIM:(hq + 1) * HEAD_DIM] = _dot(ds, kh)
                    dkh = dkh + _dot(ds, qh, TN)
                dkv_scr[:, h * HEAD_DIM:(h + 1) * HEAD_DIM] = dkh
                dkv_scr[:, 256 + h * HEAD_DIM:256 + (h + 1) * HEAD_DIM] = dvh
            dsink_ref[...] += dsink
            dq_ref[...] = _rope_t(dq_scr[...], cc8, s1c8, s2c8).astype(bf16)

        prev = carry[...]

        @pl.when(n < nb)
        def _():
            dkv_scr[pl.ds(0, BLK), :] = dkv_scr[pl.ds(0, BLK), :] + prev

        @pl.when(n == nb)
        def _():
            dkv_scr[pl.ds(0, BLK), :] = prev

        done = dkv_scr[pl.ds(0, BLK), :]
        dkv_ref[:, :256] = _rope_t(done[:, :256], cp2, s1p2, s2p2).astype(bf16)
        dkv_ref[:, 256:] = done[:, 256:].astype(bf16)

        @pl.when(n < nb)
        def _():
            carry[...] = dkv_scr[pl.ds(BLK, BLK), :]

    cur = lambda n: jnp.minimum(n, nb - 1)
    prv = lambda n: jnp.maximum(n - 1, 0)
    tab_c = pl.BlockSpec((BLK, LANE), lambda n: (cur(n), 0))
    tab_p = pl.BlockSpec((BLK, LANE), lambda n: (prv(n), 0))
    return pl.pallas_call(
        body, name=name, grid=(nb + 1,),
        in_specs=[pl.BlockSpec(memory_space=pltpu.SMEM),
                  pl.BlockSpec((BLK, Q_END), lambda n: (cur(n), 0)),
                  pl.BlockSpec((BLK, KV_W), lambda n: (cur(n), kvb)),
                  pl.BlockSpec((BLK, KV_W), lambda n: (prv(n), kvb)),
                  pl.BlockSpec((BLK, Q_END), lambda n: (cur(n), 0)),
                  tab_c, tab_c, tab_c, tab_p, tab_p, tab_p],
        out_specs=[pl.BlockSpec((BLK, Q_END), lambda n: (cur(n), 0)),
                   pl.BlockSpec((BLK, KV_W), lambda n: (prv(n), 0)),
                   pl.BlockSpec((1, LANE), lambda n: (0, 0))],
        out_shape=[jax.ShapeDtypeStruct((S, Q_END), bf16), jax.ShapeDtypeStruct((S, KV_W), bf16),
                   jax.ShapeDtypeStruct((1, LANE), f32)],
        scratch_shapes=[pltpu.VMEM((BLK, KV_W), f32), pltpu.VMEM((BLK, Q_END), f32), pltpu.VMEM((2 * BLK, KV_W), f32)],
        compiler_params=_params("arbitrary"),
    )(sinks, p, p, p, dcat, rope_c, rope_s1, rope_s2, rope_c, rope_s1, rope_s2)


A1_BLK = V_END // CONV_CH
A2_BLK = A1_BLK + 1


def _ln_stats(y):
    mu = jnp.mean(y, axis=-1, keepdims=True)
    xc = y - mu
    rstd = lax.rsqrt(jnp.mean(xc * xc, axis=-1, keepdims=True) + NORM_EPS)
    return xc * rstd, rstd


def conv_fwd(p, w, b, lng, lnb, name):
    S = p.shape[0]
    T = _tile(S, 256)
    r = T // HALO

    def body(a1_ref, a2_ref, h1_ref, h2_ref, w_ref, b_ref, g_ref, bb_ref, o_ref, y_ref, scr):
        i = pl.program_id(0)
        halo = h1_ref[...] * _sig(h2_ref[...])
        scr[pl.ds(0, HALO), :] = jnp.where(i > 0, halo, 0.0)
        scr[pl.ds(HALO, T), :] = a1_ref[...] * _sig(a2_ref[...])
        acc = jnp.zeros((T, CONV_CH), f32) + b_ref[...]
        for j in range(CONV_W):
            acc = acc + scr[pl.ds(HALO - (CONV_W - 1) + j, T), :] * w_ref[j:j + 1, :]
        y_ref[...] = acc
        yh, _ = _ln_stats(acc)
        z = yh * g_ref[...] + bb_ref[...]
        o_ref[...] = (z * _sig(z)).astype(bf16)

    vec = pl.BlockSpec((1, CONV_CH), lambda i: (0, 0))
    halo_map = lambda i: jnp.maximum(i * r - 1, 0)
    return pl.pallas_call(
        body, name=name, grid=(S // T,),
        in_specs=[pl.BlockSpec((T, CONV_CH), lambda i: (i, A1_BLK)), pl.BlockSpec((T, CONV_CH), lambda i: (i, A2_BLK)),
                  pl.BlockSpec((HALO, CONV_CH), lambda i: (halo_map(i), A1_BLK)),
                  pl.BlockSpec((HALO, CONV_CH), lambda i: (halo_map(i), A2_BLK)),
                  pl.BlockSpec((HALO, CONV_CH), lambda i: (0, 0)), vec, vec, vec],
        out_specs=[pl.BlockSpec((T, CONV_CH), lambda i: (i, 0)), pl.BlockSpec((T, CONV_CH), lambda i: (i, 0))],
        out_shape=[jax.ShapeDtypeStruct((S, CONV_CH), bf16), jax.ShapeDtypeStruct((S, CONV_CH), f32)],
        scratch_shapes=[pltpu.VMEM((T + HALO, CONV_CH), f32)],
        compiler_params=_params("parallel"),
    )(p, p, p, p, w, b, lng, lnb)


def conv_bwd(p, y, dcat, w, lng, lnb, name):
    S = p.shape[0]
    T = _tile(S, 256)
    n = S // T
    r = T // HALO
    dcb = Q_END // CONV_CH

    def body(a1_ref, a2_ref, h1_ref, h2_ref, y_ref, yn_ref, do_ref, don_ref, w_ref, g_ref, bb_ref,
             da_ref, dw_ref, db_ref, dg_ref, dbb_ref, scr_h, scr_dy, acc_b, acc_g, acc_bb):
        i = pl.program_id(0)

        @pl.when(i == 0)
        def _():
            dw_ref[...] = jnp.zeros_like(dw_ref)
            acc_b[...] = jnp.zeros_like(acc_b)
            acc_g[...] = jnp.zeros_like(acc_g)
            acc_bb[...] = jnp.zeros_like(acc_bb)

        gv, bv = g_ref[...], bb_ref[...]

        def ln_silu_bwd(yv, dout):
            yh, rstd = _ln_stats(yv)
            z = yh * gv + bv
            sg = _sig(z)
            dz = dout * (sg * (1.0 + z * (1.0 - sg)))
            gz = dz * gv
            dy = rstd * (gz - jnp.mean(gz, axis=-1, keepdims=True) - yh * jnp.mean(gz * yh, axis=-1, keepdims=True))
            return dy, dz, yh

        dy, dz, yh = ln_silu_bwd(y_ref[...], do_ref[...])
        dyn, _, _ = ln_silu_bwd(yn_ref[...], don_ref[...])
        acc_g[...] += _rowsum8(dz * yh)
        acc_bb[...] += _rowsum8(dz)
        acc_b[...] += _rowsum8(dy)
        scr_dy[pl.ds(0, T), :] = dy
        scr_dy[pl.ds(T, HALO), :] = jnp.where(i < n - 1, dyn, 0.0)
        a1, a2 = a1_ref[...], a2_ref[...]
        sg2 = _sig(a2)
        halo = h1_ref[...] * _sig(h2_ref[...])
        scr_h[pl.ds(0, HALO), :] = jnp.where(i > 0, halo, 0.0)
        scr_h[pl.ds(HALO, T), :] = a1 * sg2
        dh = jnp.zeros((T, CONV_CH), f32)
        for j in range(CONV_W):
            dh = dh + scr_dy[pl.ds(CONV_W - 1 - j, T), :] * w_ref[j:j + 1, :]
            dw_ref[j:j + 1, :] += jnp.sum(dy * scr_h[pl.ds(HALO - (CONV_W - 1) + j, T), :], axis=0, keepdims=True)
        da_ref[:, :CONV_CH] = (dh * sg2).astype(bf16)
        da_ref[:, CONV_CH:] = (dh * a1 * sg2 * (1.0 - sg2)).astype(bf16)

        @pl.when(i == n - 1)
        def _():
            db_ref[...] = jnp.sum(acc_b[...], axis=0, keepdims=True)
            dg_ref[...] = jnp.sum(acc_g[...], axis=0, keepdims=True)
            dbb_ref[...] = jnp.sum(acc_bb[...], axis=0, keepdims=True)

    vec = pl.BlockSpec((1, CONV_CH), lambda i: (0, 0))
    tap = pl.BlockSpec((HALO, CONV_CH), lambda i: (0, 0))
    prev_map = lambda i: jnp.maximum(i * r - 1, 0)
    next_map = lambda i: jnp.minimum((i + 1) * r, S // HALO - 1)
    return pl.pallas_call(
        body, name=name, grid=(n,),
        in_specs=[pl.BlockSpec((T, CONV_CH), lambda i: (i, A1_BLK)), pl.BlockSpec((T, CONV_CH), lambda i: (i, A2_BLK)),
                  pl.BlockSpec((HALO, CONV_CH), lambda i: (prev_map(i), A1_BLK)),
                  pl.BlockSpec((HALO, CONV_CH), lambda i: (prev_map(i), A2_BLK)),
                  pl.BlockSpec((T, CONV_CH), lambda i: (i, 0)),
                  pl.BlockSpec((HALO, CONV_CH), lambda i: (next_map(i), 0)),
                  pl.BlockSpec((T, CONV_CH), lambda i: (i, dcb)),
                  pl.BlockSpec((HALO, CONV_CH), lambda i: (next_map(i), dcb)),
                  tap, vec, vec],
        out_specs=[pl.BlockSpec((T, 2 * CONV_CH), lambda i: (i, 0)), tap, vec, vec, vec],
        out_shape=[jax.ShapeDtypeStruct((S, 2 * CONV_CH), bf16), jax.ShapeDtypeStruct((HALO, CONV_CH), f32),
                   jax.ShapeDtypeStruct((1, CONV_CH), f32), jax.ShapeDtypeStruct((1, CONV_CH), f32),
                   jax.ShapeDtypeStruct((1, CONV_CH), f32)],
        scratch_shapes=[pltpu.VMEM((T + HALO, CONV_CH), f32), pltpu.VMEM((T + HALO, CONV_CH), f32),
                        pltpu.VMEM((8, CONV_CH), f32), pltpu.VMEM((8, CONV_CH), f32), pltpu.VMEM((8, CONV_CH), f32)],
        compiler_params=_params("arbitrary"),
    )(p, p, p, p, y, y, dcat, dcat, w, lng, lnb)


U_BLK = (V_END + 2 * CONV_CH) // SGU_CH
SV_BLK = U_BLK + 1


def _tril(w, transposed=False):
    row = lax.broadcasted_iota(jnp.int32, (BLK, BLK), 0)
    col = lax.broadcasted_iota(jnp.int32, (BLK, BLK), 1)
    keep = (col >= row) if transposed else (row >= col)
    return jnp.where(keep, w, 0.0)


def sgu_fwd(p, lng, lnb, w, bias, name):
    S = p.shape[0]
    T = _tile(S, 256)

    def body(u_ref, v_ref, g_ref, bb_ref, w_ref, bias_ref, o_ref):
        yh, _ = _ln_stats(v_ref[...])
        v = (yh * g_ref[...] + bb_ref[...]).astype(bf16)
        low = lax.broadcasted_iota(jnp.int32, (BLK, LANE), 1) < HEAD_DIM
        for pr in range(SGU_HEADS // 2):
            lanes = pl.ds(pr * LANE, LANE)
            w0 = _tril(w_ref[2 * pr]).astype(bf16)
            w1 = _tril(w_ref[2 * pr + 1]).astype(bf16)
            for c in range(T // BLK):
                rows = pl.ds(c * BLK, BLK)
                vp = v[c * BLK:(c + 1) * BLK, pr * LANE:(pr + 1) * LANE]
                mixed = jnp.where(low, _dot(w0, vp), _dot(w1, vp)) + bias_ref[:, lanes]
                o_ref[rows, lanes] = (u_ref[rows, lanes] * mixed).astype(bf16)

    vec = pl.BlockSpec((1, SGU_CH), lambda i: (0, 0))
    return pl.pallas_call(
        body, name=name, grid=(S // T,),
        in_specs=[pl.BlockSpec((T, SGU_CH), lambda i: (i, U_BLK)), pl.BlockSpec((T, SGU_CH), lambda i: (i, SV_BLK)),
                  vec, vec, pl.BlockSpec((SGU_HEADS, BLK, BLK), lambda i: (0, 0, 0)),
                  pl.BlockSpec((BLK, SGU_CH), lambda i: (0, 0))],
        out_specs=pl.BlockSpec((T, SGU_CH), lambda i: (i, 0)),
        out_shape=jax.ShapeDtypeStruct((S, SGU_CH), bf16),
        compiler_params=_params("parallel"),
    )(p, p, lng, lnb, w, bias)


def sgu_bwd(p, dcat, lng, lnb, w, wt, bias, name):
    S = p.shape[0]
    T = _tile(S, 256)
    n = S // T
    dsb = (Q_END + CONV_CH) // SGU_CH

    def body(u_ref, v_ref, do_ref, g_ref, bb_ref, w_ref, wt_ref, bias_ref,
             da_ref, dw_ref, db_ref, dg_ref, dbb_ref, dv_scr, acc_bias, acc_g, acc_bb):
        i = pl.program_id(0)

        @pl.when(i == 0)
        def _():
            dw_ref[...] = jnp.zeros_like(dw_ref)
            acc_bias[...] = jnp.zeros_like(acc_bias)
            acc_g[...] = jnp.zeros_like(acc_g)
            acc_bb[...] = jnp.zeros_like(acc_bb)

        gv = g_ref[...]
        yh, rstd = _ln_stats(v_ref[...])
        v = (yh * gv + bb_ref[...]).astype(bf16)
        low = lax.broadcasted_iota(jnp.int32, (BLK, LANE), 1) < HEAD_DIM
        for pr in range(SGU_HEADS // 2):
            lanes = pl.ds(pr * LANE, LANE)
            w0 = _tril(w_ref[2 * pr]).astype(bf16)
            w1 = _tril(w_ref[2 * pr + 1]).astype(bf16)
            wt0 = _tril(wt_ref[2 * pr], True).astype(bf16)
            wt1 = _tril(wt_ref[2 * pr + 1], True).astype(bf16)
            dw0 = jnp.zeros((BLK, BLK), f32)
            dw1 = jnp.zeros((BLK, BLK), f32)
            for c in range(T // BLK):
                rows = pl.ds(c * BLK, BLK)
                vp = v[c * BLK:(c + 1) * BLK, pr * LANE:(pr + 1) * LANE]
                mixed = jnp.where(low, _dot(w0, vp), _dot(w1, vp)) + bias_ref[:, lanes]
                do = do_ref[rows, lanes]
                da_ref[rows, lanes] = (do * mixed).astype(bf16)
                dm = do * u_ref[rows, lanes]
                acc_bias[:, lanes] += dm
                dmb = dm.astype(bf16)
                dv_scr[rows, lanes] = jnp.where(low, _dot(wt0, dmb), _dot(wt1, dmb))
                zero = jnp.zeros_like(dmb)
                dw0 = dw0 + _dot(jnp.where(low, dmb, zero), vp, NT)
                dw1 = dw1 + _dot(jnp.where(low, zero, dmb), vp, NT)
            dw_ref[2 * pr] += _tril(dw0)
            dw_ref[2 * pr + 1] += _tril(dw1)
        dv = dv_scr[...]
        acc_g[...] += _rowsum8(dv * yh)
        acc_bb[...] += _rowsum8(dv)
        gz = dv * gv
        dvr = rstd * (gz - jnp.mean(gz, axis=-1, keepdims=True) - yh * jnp.mean(gz * yh, axis=-1, keepdims=True))
        da_ref[:, SGU_CH:] = dvr.astype(bf16)

        @pl.when(i == n - 1)
        def _():
            ch = lax.broadcasted_iota(jnp.int32, (SGU_CH, LANE), 0) // HEAD_DIM
            hd = lax.broadcasted_iota(jnp.int32, (SGU_CH, LANE), 1)
            fold = jnp.where(ch == hd, 1.0, 0.0).astype(f32)
            db_ref[...] = jnp.dot(acc_bias[...], fold, preferred_element_type=f32, precision=lax.Precision.HIGHEST)
            dg_ref[...] = jnp.sum(acc_g[...], axis=0, keepdims=True)
            dbb_ref[...] = jnp.sum(acc_bb[...], axis=0, keepdims=True)

    vec = pl.BlockSpec((1, SGU_CH), lambda i: (0, 0))
    wsp = pl.BlockSpec((SGU_HEADS, BLK, BLK), lambda i: (0, 0, 0))
    return pl.pallas_call(
        body, name=name, grid=(n,),
        in_specs=[pl.BlockSpec((T, SGU_CH), lambda i: (i, U_BLK)), pl.BlockSpec((T, SGU_CH), lambda i: (i, SV_BLK)),
                  pl.BlockSpec((T, SGU_CH), lambda i: (i, dsb)), vec, vec, wsp, wsp,
                  pl.BlockSpec((BLK, SGU_CH), lambda i: (0, 0))],
        out_specs=[pl.BlockSpec((T, 2 * SGU_CH), lambda i: (i, 0)), wsp,
                   pl.BlockSpec((BLK, LANE), lambda i: (0, 0)), vec, vec],
        out_shape=[jax.ShapeDtypeStruct((S, 2 * SGU_CH), bf16), jax.ShapeDtypeStruct((SGU_HEADS, BLK, BLK), f32),
                   jax.ShapeDtypeStruct((BLK, LANE), f32), jax.ShapeDtypeStruct((1, SGU_CH), f32),
                   jax.ShapeDtypeStruct((1, SGU_CH), f32)],
        scratch_shapes=[pltpu.VMEM((T, SGU_CH), f32), pltpu.VMEM((BLK, SGU_CH), f32),
                        pltpu.VMEM((8, SGU_CH), f32), pltpu.VMEM((8, SGU_CH), f32)],
        compiler_params=_params("arbitrary"),
    )(p, p, dcat, lng, lnb, w, wt, bias)


def _me_and_peers():
    x, y, c = lax.axis_index("x"), lax.axis_index("y"), lax.axis_index("c")
    me = 4 * x + 2 * y + c
    peers = []
    for k in range(1, N_DEV):
        px, py, pc = x ^ (k >> 2), y ^ ((k >> 1) & 1), c ^ (k & 1)
        peers.append(((px, py, pc), 4 * px + 2 * py + pc))
    return me, peers


def exchange(arrs, gather, name):
    n_arr = len(arrs)

    def body(*refs):
        srcs, dsts = refs[:n_arr], refs[n_arr:2 * n_arr]
        send_sems, recv_sems, local_sems = refs[2 * n_arr:]
        me, peers = _me_and_peers()
        copies = []
        for a in range(n_arr):
            own = srcs[a] if gather else srcs[a].at[me]
            loc = pltpu.make_async_copy(own, dsts[a].at[me], local_sems.at[a])
            loc.start()
            copies.append(loc)
        sends = []
        for k, (dev, idx) in enumerate(peers):
            for a in range(n_arr):
                cp = pltpu.make_async_remote_copy(
                    src_ref=srcs[a] if gather else srcs[a].at[idx], dst_ref=dsts[a].at[me],
                    send_sem=send_sems.at[a, k], recv_sem=recv_sems.at[a, k], device_id=dev, device_id_type=MESH)
                cp.start()
                sends.append(cp)
        for k, (dev, idx) in enumerate(peers):
            for a in range(n_arr):
                pltpu.make_async_remote_copy(
                    src_ref=srcs[a] if gather else srcs[a].at[me], dst_ref=dsts[a].at[idx],
                    send_sem=send_sems.at[a, k], recv_sem=recv_sems.at[a, k], device_id=dev,
                    device_id_type=MESH).wait_recv()
        for cp in sends:
            cp.wait_send()
        for cp in copies:
            cp.wait()

    any_spec = pl.BlockSpec(memory_space=pl.ANY)
    out_shape = [jax.ShapeDtypeStruct(((N_DEV,) + a.shape) if gather else a.shape, a.dtype) for a in arrs]
    return pl.pallas_call(
        body, name=name,
        in_specs=[any_spec] * n_arr, out_specs=[any_spec] * n_arr, out_shape=out_shape,
        scratch_shapes=[pltpu.SemaphoreType.DMA((n_arr, N_DEV - 1)), pltpu.SemaphoreType.DMA((n_arr, N_DEV - 1)),
                        pltpu.SemaphoreType.DMA((n_arr,))],
        compiler_params=pltpu.CompilerParams(has_side_effects=True),
    )(*arrs)


def allreduce_small(buf, name):
    R = buf.shape[0]

    def body(src, out, land, send_sems, recv_sems):
        me, peers = _me_and_peers()
        land[me] = src[...]
        sends = []
        for k, (dev, idx) in enumerate(peers):
            cp = pltpu.make_async_remote_copy(src_ref=src, dst_ref=land.at[me], send_sem=send_sems.at[k],
                                              recv_sem=recv_sems.at[k], device_id=dev, device_id_type=MESH)
            cp.start()
            sends.append(cp)
        for k, (dev, idx) in enumerate(peers):
            pltpu.make_async_remote_copy(src_ref=src, dst_ref=land.at[idx], send_sem=send_sems.at[k],
                                         recv_sem=recv_sems.at[k], device_id=dev, device_id_type=MESH).wait_recv()
        for cp in sends:
            cp.wait_send()
        total = land[0]
        for j in range(1, N_DEV):
            total = total + land[j]
        out[...] = total

    vm = pl.BlockSpec(memory_space=pltpu.VMEM)
    return pl.pallas_call(
        body, name=name, in_specs=[vm], out_specs=vm, out_shape=jax.ShapeDtypeStruct((R, LANE), f32),
        scratch_shapes=[pltpu.VMEM((N_DEV, R, LANE), f32), pltpu.SemaphoreType.DMA((N_DEV - 1,)),
                        pltpu.SemaphoreType.DMA((N_DEV - 1,))],
        compiler_params=pltpu.CompilerParams(has_side_effects=True, vmem_limit_bytes=VMEM_LIMIT),
    )(buf)


def adamw(parts, w, m, v, name):
    L, R, C = w.shape
    P = parts[0].shape[0]
    tr = _tile(R, 128 if C > 1024 else 256)
    nr = R // tr
    c1 = 1.0 - ADAM_B1 ** ADAM_STEP
    c2 = 1.0 - ADAM_B2 ** ADAM_STEP

    def body(*refs):
        part_refs = refs[:L]
        w_ref, m_ref, v_ref, g_out, d_out, m_out, v_out = refs[L:]
        layer = pl.program_id(0)
        for l in range(L):
            @pl.when(layer == l)
            def _(l=l):
                g = part_refs[l][0].astype(f32)
                for q in range(1, P):
                    g = g + part_refs[l][q].astype(f32)
                mn = ADAM_B1 * m_ref[...] + (1.0 - ADAM_B1) * g
                vn = ADAM_B2 * v_ref[...] + (1.0 - ADAM_B2) * (g * g)
                g_out[...] = g
                m_out[...] = mn
                v_out[...] = vn
                d_out[...] = -ADAM_LR * ((mn / c1) / (jnp.sqrt(vn / c2) + ADAM_EPS) + ADAM_WD * w_ref[...])

    def part_spec(l):
        return pl.BlockSpec((P, tr, C), lambda a, i: (0, jnp.where(a == l, i, jnp.where(a < l, 0, nr - 1)), 0))

    slab = pl.BlockSpec((None, tr, C), lambda a, i: (a, i, 0))
    out = jax.ShapeDtypeStruct((L, R, C), f32)
    return pl.pallas_call(
        body, name=name, grid=(L, nr),
        in_specs=[part_spec(l) for l in range(L)] + [slab, slab, slab],
        out_specs=[slab, slab, slab, slab], out_shape=[out, out, out, out],
        compiler_params=_params("arbitrary", "arbitrary"),
    )(*parts, w, m, v)


PACK = 8 * LANE


def _pack(arrs):
    pieces = []
    for a in arrs:
        flat = a.astype(f32).reshape(-1)
        pad = (-flat.shape[0]) % PACK
        pieces.append(jnp.pad(flat, (0, pad)).reshape(-1, LANE))
    return jnp.concatenate(pieces, axis=0)


def _unpack(buf, shapes):
    out, row = [], 0
    for shp in shapes:
        size = math.prod(shp)
        rows = (size + PACK - 1) // PACK * (PACK // LANE)
        out.append(buf[row:row + rows].reshape(-1)[:size].reshape(shp))
        row += rows
    return out


def _rope_tables(positions):
    half = ROT_DIM // 2
    inv_freq = 1.0 / (ROPE_THETA ** (jnp.arange(0, ROT_DIM, 2, dtype=f32) / ROT_DIM))
    ang = positions.astype(f32)[:, None] * inv_freq
    cos, sin = jnp.cos(ang), jnp.sin(ang)
    S = positions.shape[0]
    zeros, ones = jnp.zeros((S, half), f32), jnp.ones((S, HEAD_DIM - ROT_DIM), f32)
    rest = jnp.zeros((S, HEAD_DIM - ROT_DIM), f32)
    c = jnp.concatenate([cos, cos, ones], axis=1)
    s1 = jnp.concatenate([zeros, sin, rest], axis=1)
    s2 = jnp.concatenate([-sin, zeros, rest], axis=1)
    return tuple(jnp.tile(t, (1, LANE // HEAD_DIM)) for t in (c, s1, s2))


def _cols_to_shards(g):
    lead, (R, N) = g.shape[:-2], g.shape[-2:]
    g = g.reshape(lead + (R, N_DEV, N // N_DEV))
    return jnp.moveaxis(g, -2, 0)


def _shards_to_cols(g):
    g = jnp.moveaxis(g, 0, -2)
    return g.reshape(g.shape[:-2] + (g.shape[-2] * g.shape[-1],))


def kernel(x, positions, norm_ffn1, ffn1_w_in, ffn1_w_out, norm_mix, w_in, conv_dw_w, conv_dw_b, conv_ln_g, conv_ln_b, sgu_ln_g, sgu_ln_b, sgu_w, sgu_b, attn_sinks, w_out, norm_ffn2, ffn2_w_in, ffn2_w_out, final_norm, loss_target, m_norm_ffn1, m_ffn1_w_in, m_ffn1_w_out, m_norm_mix, m_w_in, m_conv_dw_w, m_conv_dw_b, m_conv_ln_g, m_conv_ln_b, m_sgu_ln_g, m_sgu_ln_b, m_sgu_w, m_sgu_b, m_attn_sinks, m_w_out, m_norm_ffn2, m_ffn2_w_in, m_ffn2_w_out, m_final_norm, v_norm_ffn1, v_ffn1_w_in, v_ffn1_w_out, v_norm_mix, v_w_in, v_conv_dw_w, v_conv_dw_b, v_conv_ln_g, v_conv_ln_b, v_sgu_ln_g, v_sgu_ln_b, v_sgu_w, v_sgu_b, v_attn_sinks, v_w_out, v_norm_ffn2, v_ffn2_w_in, v_ffn2_w_out, v_final_norm):
    L = norm_ffn1.shape[0]
    S, D = x.shape[1], x.shape[2]
    F = ffn1_w_out.shape[1] * N_DEV
    me = 4 * lax.axis_index("x") + 2 * lax.axis_index("y") + lax.axis_index("c")
    x0 = x[0]
    rope_c, rope_s1, rope_s2 = _rope_tables(positions[0])
    cw = CONV_CH // N_DEV

    weights = []
    for l in range(L):
        taps = jnp.pad(conv_dw_w[l], ((0, HALO - CONV_W), (0, LANE - cw)))
        got = exchange([ffn1_w_in[l].astype(bf16), ffn1_w_out[l].astype(bf16), w_in[l].astype(bf16),
                        w_out[l].astype(bf16), ffn2_w_in[l].astype(bf16), ffn2_w_out[l].astype(bf16), taps],
                       True, f"gather_weights_{l}")
        f1i, f1o, wi, wo, f2i, f2o, tp = got
        weights.append(dict(
            f1_in=_shards_to_cols(f1i).reshape(D, 2, F).transpose(1, 0, 2),
            f1_out=f1o.reshape(F, D),
            w_in=_shards_to_cols(wi),
            w_out=wo.reshape(D, D),
            f2_in=_shards_to_cols(f2i).reshape(D, 2, F).transpose(1, 0, 2),
            f2_out=f2o.reshape(F, D),
            taps=_shards_to_cols(tp[:, :, :cw]),
        ))

    row = lambda a: a.reshape(1, -1)

    saved = []
    xs = x0
    for l in range(L):
        wts = weights[l]
        sv = dict(x_a=xs)
        h, sv["h1t"] = rmsnorm_fwd(xs, row(norm_ffn1[l]), f"norm_ffn1_{l}")
        sv["gu1"], a, sv["a1t"] = ffn_in(h, wts["f1_in"], f"ffn1_in_{l}")
        xs = mm_res(a, wts["f1_out"], xs, FFN_RES, f"ffn1_out_{l}")
        sv["x_b"] = xs
        h, sv["h2t"] = rmsnorm_fwd(xs, row(norm_mix[l]), f"norm_mix_{l}")
        p = mm_nn(h, wts["w_in"][None], f"mix_in_{l}")[0]
        sv["p"] = p
        attn = attn_fwd(p, rope_c, rope_s1, rope_s2, attn_sinks[l], f"attn_fwd_{l}")
        conv, sv["conv_y"] = conv_fwd(p, wts["taps"], row(conv_dw_b[l]), row(conv_ln_g[l]), row(conv_ln_b[l]),
                                      f"conv_fwd_{l}")
        sv["sgu_bias"] = jnp.repeat(sgu_b[l].T, HEAD_DIM, axis=1)
        sgu = sgu_fwd(p, row(sgu_ln_g[l]), row(sgu_ln_b[l]), sgu_w[l], sv["sgu_bias"], f"sgu_fwd_{l}")
        cat = jnp.concatenate([attn, conv, sgu], axis=1)
        sv["catt"] = cat.T
        xs = mm_res(cat, wts["w_out"], xs, 1.0, f"mix_out_{l}")
        sv["x_c"] = xs
        h, sv["h3t"] = rmsnorm_fwd(xs, row(norm_ffn2[l]), f"norm_ffn2_{l}")
        sv["gu2"], a, sv["a2t"] = ffn_in(h, wts["f2_in"], f"ffn2_in_{l}")
        xs = mm_res(a, wts["f2_out"], xs, FFN_RES, f"ffn2_out_{l}")
        saved.append(sv)

    dx, dxb, d_final_norm, loss = final_loss(xs, row(final_norm), loss_target[0], "final_loss")

    small = [None] * L
    recv = [None] * L
    for l in reversed(range(L)):
        wts, sv = weights[l], saved[l]

        def ffn_bwd(dx, dxb, ht, gu, at, w_in2, w_out2, x_in, g_norm, tag):
            dgu = ffn_dact(dxb, w_out2, gu, f"{tag}_dact_{l}")
            d_w_out = mm_nn(at, dxb[None], f"{tag}_dwout_{l}", bf16, FFN_RES)[0]
            d_w_in = mm_nn(ht, dgu, f"{tag}_dwin_{l}", bf16)
            dh = mm_nt(dgu, w_in2, f"{tag}_dh_{l}")
            dx, dxb, dg = rmsnorm_bwd(dh, x_in, g_norm, dx, f"{tag}_dnorm_{l}")
            return dx, dxb, d_w_in, d_w_out, dg

        dx, dxb, d_f2_in, d_f2_out, d_norm_ffn2 = ffn_bwd(
            dx, dxb, sv["h3t"], sv["gu2"], sv["a2t"], wts["f2_in"], wts["f2_out"], sv["x_c"], row(norm_ffn2[l]), "ffn2")

        d_w_out = mm_nn(sv["catt"], dxb[None], f"mix_dwout_{l}", bf16)[0]
        dcat = mm_nt(dxb[None], wts["w_out"][None], f"mix_dcat_{l}")
        p = sv["p"]
        dq, dkv, d_sinks = attn_bwd(p, dcat, rope_c, rope_s1, rope_s2, attn_sinks[l], f"attn_bwd_{l}")
        da_conv, d_taps, d_conv_b, d_conv_g, d_conv_bb = conv_bwd(
            p, sv["conv_y"], dcat, wts["taps"], row(conv_ln_g[l]), row(conv_ln_b[l]), f"conv_bwd_{l}")
        da_sgu, d_sgu_w, d_sgu_bias, d_sgu_g, d_sgu_bb = sgu_bwd(
            p, dcat, row(sgu_ln_g[l]), row(sgu_ln_b[l]), sgu_w[l], jnp.swapaxes(sgu_w[l], 1, 2), sv["sgu_bias"],
            f"sgu_bwd_{l}")
        dp = jnp.concatenate([dq, dkv, da_conv, da_sgu], axis=1)
        d_w_in = mm_nn(sv["h2t"], dp[None], f"mix_dwin_{l}", bf16)[0]
        dh = mm_nt(dp[None], wts["w_in"][None], f"mix_dh_{l}")
        dx, dxb, d_norm_mix = rmsnorm_bwd(dh, sv["x_b"], row(norm_mix[l]), dx, f"mix_dnorm_{l}")

        dx, dxb, d_f1_in, d_f1_out, d_norm_ffn1 = ffn_bwd(
            dx, dxb, sv["h1t"], sv["gu1"], sv["a1t"], wts["f1_in"], wts["f1_out"], sv["x_a"], row(norm_ffn1[l]), "ffn1")

        small[l] = dict(norm_ffn1=d_norm_ffn1[0], norm_mix=d_norm_mix[0], conv_dw_w=d_taps[:CONV_W],
                        conv_dw_b=d_conv_b[0], conv_ln_g=d_conv_g[0], conv_ln_b=d_conv_bb[0], sgu_ln_g=d_sgu_g[0],
                        sgu_ln_b=d_sgu_bb[0], sgu_w=d_sgu_w, sgu_b=d_sgu_bias[:, :SGU_HEADS].T,
                        attn_sinks=d_sinks[0, :N_Q_HEADS], norm_ffn2=d_norm_ffn2[0])

        def in_shards(d2):
            return _cols_to_shards(d2.transpose(1, 0, 2).reshape(D, 2 * F))

        recv[l] = exchange(
            [in_shards(d_f1_in), d_f1_out.reshape(N_DEV, F // N_DEV, D), _cols_to_shards(d_w_in),
             d_w_out.reshape(N_DEV, D // N_DEV, D), in_shards(d_f2_in), d_f2_out.reshape(N_DEV, F // N_DEV, D)],
            False, f"scatter_grads_{l}")

    grad_x = dx[None]

    big = {}
    for idx, (name, w, m, v) in enumerate([
            ("ffn1_w_in", ffn1_w_in, m_ffn1_w_in, v_ffn1_w_in), ("ffn1_w_out", ffn1_w_out, m_ffn1_w_out, v_ffn1_w_out),
            ("w_in", w_in, m_w_in, v_w_in), ("w_out", w_out, m_w_out, v_w_out),
            ("ffn2_w_in", ffn2_w_in, m_ffn2_w_in, v_ffn2_w_in), ("ffn2_w_out", ffn2_w_out, m_ffn2_w_out, v_ffn2_w_out)]):
        big[name] = adamw([recv[l][idx] for l in range(L)], w, m, v, f"adamw_{name}")

    small_names = ["norm_ffn1", "norm_mix", "conv_dw_w", "conv_dw_b", "conv_ln_g", "conv_ln_b", "sgu_ln_g", "sgu_ln_b",
                   "sgu_w", "sgu_b", "attn_sinks", "norm_ffn2"]
    local = [jnp.stack([small[l][n] for l in range(L)]) for n in small_names] + [d_final_norm[0], loss.reshape(1)]
    shapes = [a.shape for a in local]
    total = _unpack(allreduce_small(_pack(local), "sum_small_grads"), shapes)
    g_small = dict(zip(small_names + ["final_norm", "loss"], total))
    g_small["conv_dw_w"] = lax.dynamic_slice_in_dim(g_small["conv_dw_w"], me * cw, cw, axis=2)
    given = dict(norm_ffn1=(norm_ffn1, m_norm_ffn1, v_norm_ffn1), norm_mix=(norm_mix, m_norm_mix, v_norm_mix),
                 conv_dw_w=(conv_dw_w, m_conv_dw_w, v_conv_dw_w), conv_dw_b=(conv_dw_b, m_conv_dw_b, v_conv_dw_b),
                 conv_ln_g=(conv_ln_g, m_conv_ln_g, v_conv_ln_g), conv_ln_b=(conv_ln_b, m_conv_ln_b, v_conv_ln_b),
                 sgu_ln_g=(sgu_ln_g, m_sgu_ln_g, v_sgu_ln_g), sgu_ln_b=(sgu_ln_b, m_sgu_ln_b, v_sgu_ln_b),
                 sgu_w=(sgu_w, m_sgu_w, v_sgu_w), sgu_b=(sgu_b, m_sgu_b, v_sgu_b),
                 attn_sinks=(attn_sinks, m_attn_sinks, v_attn_sinks), norm_ffn2=(norm_ffn2, m_norm_ffn2, v_norm_ffn2),
                 final_norm=(final_norm, m_final_norm, v_final_norm))
    upd_names = small_names + ["final_norm"]
    upd_shapes = [given[n][0].shape for n in upd_names]
    packed = [_pack([g_small[n] for n in upd_names])[None]] + [_pack([given[n][k] for n in upd_names])[None] for k in range(3)]
    res = adamw([packed[0]], packed[1], packed[2], packed[3], "adamw_small")
    upd = [dict(zip(upd_names, _unpack(r[0], upd_shapes))) for r in res]

    order = ["norm_ffn1", "ffn1_w_in", "ffn1_w_out", "norm_mix", "w_in", "conv_dw_w", "conv_dw_b", "conv_ln_g",
             "conv_ln_b", "sgu_ln_g", "sgu_ln_b", "sgu_w", "sgu_b", "attn_sinks", "w_out", "norm_ffn2", "ffn2_w_in",
             "ffn2_w_out", "final_norm"]
    outs = [g_small["loss"].reshape(()), grad_x]
    for k in range(4):
        outs += [big[n][k] if n in big else upd[k][n] for n in order]
    return tuple(outs)
```

```python
import functools
import math

import jax
import jax.numpy as jnp
from jax import lax
from jax.experimental import pallas as pl
from jax.experimental.pallas import tpu as pltpu

f32 = jnp.float32
bf16 = jnp.bfloat16

N_DEV = 8
HEAD_DIM = 64
N_Q_HEADS = 16
N_KV_HEADS = 4
GQ = N_Q_HEADS // N_KV_HEADS
BLK = 128
ROT_DIM = 16
ROPE_THETA = 500000.0
CONV_W = 31
CONV_CH = 512
SGU_CH = 512
SGU_HEADS = 8
Q_END = N_Q_HEADS * HEAD_DIM
KV_W = 2 * N_KV_HEADS * HEAD_DIM
V_END = Q_END + KV_W
IN_COLS = V_END + 2 * CONV_CH + 2 * SGU_CH
HALO = 32
NORM_EPS = 1e-5
FFN_RES = 0.5
ADAM_LR, ADAM_B1, ADAM_B2, ADAM_EPS, ADAM_WD, ADAM_STEP = 0.001, 0.9, 0.999, 1e-08, 0.01, 10
LANE = 128
VMEM_LIMIT = 56 * 2 ** 20
MESH = pl.DeviceIdType.MESH

NN = (((1,), (0,)), ((), ()))
NT = (((1,), (1,)), ((), ()))
TN = (((0,), (0,)), ((), ()))


def _tile(dim, pref):
    t = min(pref, dim)
    while dim % t:
        t //= 2
    return t


def _params(*sem):
    return pltpu.CompilerParams(dimension_semantics=sem, vmem_limit_bytes=VMEM_LIMIT)


def _sig(x):
    return 1.0 / (1.0 + jnp.exp(-x))


def _dot(a, b, dims=NN):
    return lax.dot_general(a, b, dims, preferred_element_type=f32)


def _rowsum8(x):
    return x.reshape(x.shape[0] // 8, 8, x.shape[1]).sum(axis=0)


def rmsnorm_fwd(x, g, name, deps=()):
    S, D = x.shape
    tm = _tile(S, 512)

    def body(x_ref, g_ref, *rest):
        o_ref, ot_ref = rest[-2:]
        xv = x_ref[...]
        r = lax.rsqrt(jnp.mean(xv * xv, axis=-1, keepdims=True) + NORM_EPS)
        hb = (xv * r * g_ref[...]).astype(bf16)
        o_ref[...] = hb
        ot_ref[...] = hb.T

    return pl.pallas_call(
        body, name=name, grid=(S // tm,),
        in_specs=[pl.BlockSpec((tm, D), lambda i: (i, 0)), pl.BlockSpec((1, D), lambda i: (0, 0))] + [ANY] * len(deps),
        out_specs=[pl.BlockSpec((tm, D), lambda i: (i, 0)), pl.BlockSpec((D, tm), lambda i: (0, i))],
        out_shape=[jax.ShapeDtypeStruct((S, D), bf16), jax.ShapeDtypeStruct((D, S), bf16)],
        compiler_params=_params("parallel"),
    )(x, g, *deps)


def rmsnorm_bwd(dh, x, g, dres, name):
    S, D = x.shape
    tm = _tile(S, 256)
    n = S // tm

    def body(dh_ref, x_ref, g_ref, dres_ref, dx_ref, dxb_ref, dg_ref, acc):
        i = pl.program_id(0)

        @pl.when(i == 0)
        def _():
            acc[...] = jnp.zeros_like(acc)

        xv = x_ref[...]
        r = lax.rsqrt(jnp.mean(xv * xv, axis=-1, keepdims=True) + NORM_EPS)
        xh = xv * r
        dy = dh_ref[...]
        gy = dy * g_ref[...]
        dx = dres_ref[...] + r * (gy - xh * jnp.mean(gy * xh, axis=-1, keepdims=True))
        dx_ref[...] = dx
        dxb_ref[...] = dx.astype(bf16)
        acc[...] += _rowsum8(dy * xh)

        @pl.when(i == n - 1)
        def _():
            dg_ref[...] = jnp.sum(acc[...], axis=0, keepdims=True)

    row = pl.BlockSpec((tm, D), lambda i: (i, 0))
    vec = pl.BlockSpec((1, D), lambda i: (0, 0))
    return pl.pallas_call(
        body, name=name, grid=(n,),
        in_specs=[row, row, vec, row],
        out_specs=[row, row, vec],
        out_shape=[jax.ShapeDtypeStruct((S, D), f32), jax.ShapeDtypeStruct((S, D), bf16),
                   jax.ShapeDtypeStruct((1, D), f32)],
        scratch_shapes=[pltpu.VMEM((8, D), f32)],
        compiler_params=_params("arbitrary"),
    )(dh, x, g, dres)


def final_loss(x, g, tgt, name):
    S, D = x.shape
    tm = _tile(S, 256)
    n = S // tm

    def body(x_ref, g_ref, t_ref, dx_ref, dxb_ref, dg_ref, loss_ref, acc):
        i = pl.program_id(0)

        @pl.when(i == 0)
        def _():
            acc[...] = jnp.zeros_like(acc)
            loss_ref[...] = jnp.zeros_like(loss_ref)

        xv = x_ref[...]
        gv = g_ref[...]
        r = lax.rsqrt(jnp.mean(xv * xv, axis=-1, keepdims=True) + NORM_EPS)
        xh = xv * r
        diff = xh * gv - t_ref[...]
        tok = jnp.mean(diff * diff, axis=-1, keepdims=True)
        loss_ref[...] += 0.5 * jnp.sum(tok, axis=0, keepdims=True)
        dy = diff / D
        gy = dy * gv
        dx = r * (gy - xh * jnp.mean(gy * xh, axis=-1, keepdims=True))
        dx_ref[...] = dx
        dxb_ref[...] = dx.astype(bf16)
        acc[...] += _rowsum8(dy * xh)

        @pl.when(i == n - 1)
        def _():
            dg_ref[...] = jnp.sum(acc[...], axis=0, keepdims=True)

    row = pl.BlockSpec((tm, D), lambda i: (i, 0))
    vec = pl.BlockSpec((1, D), lambda i: (0, 0))
    return pl.pallas_call(
        body, name=name, grid=(n,),
        in_specs=[row, vec, row],
        out_specs=[row, row, vec, pl.BlockSpec((1, 1), lambda i: (0, 0))],
        out_shape=[jax.ShapeDtypeStruct((S, D), f32), jax.ShapeDtypeStruct((S, D), bf16),
                   jax.ShapeDtypeStruct((1, D), f32), jax.ShapeDtypeStruct((1, 1), f32)],
        scratch_shapes=[pltpu.VMEM((8, D), f32)],
        compiler_params=_params("arbitrary"),
    )(x, g, tgt)


def ffn_in(h, w2, name):
    S, D = h.shape
    F = w2.shape[2]
    tm, tn = _tile(S, 512), _tile(F, 512)

    def body(h_ref, w_ref, gu_ref, a_ref, at_ref):
        hv = h_ref[...]
        g = _dot(hv, w_ref[0])
        u = _dot(hv, w_ref[1])
        gu_ref[0] = g.astype(bf16)
        gu_ref[1] = u.astype(bf16)
        a = (g * _sig(g) * u).astype(bf16)
        a_ref[...] = a
        at_ref[...] = a.T

    return pl.pallas_call(
        body, name=name, grid=(S // tm, F // tn),
        in_specs=[pl.BlockSpec((tm, D), lambda i, j: (i, 0)), pl.BlockSpec((2, D, tn), lambda i, j: (0, 0, j))],
        out_specs=[pl.BlockSpec((2, tm, tn), lambda i, j: (0, i, j)), pl.BlockSpec((tm, tn), lambda i, j: (i, j)),
                   pl.BlockSpec((tn, tm), lambda i, j: (j, i))],
        out_shape=[jax.ShapeDtypeStruct((2, S, F), bf16), jax.ShapeDtypeStruct((S, F), bf16),
                   jax.ShapeDtypeStruct((F, S), bf16)],
        compiler_params=_params("parallel", "parallel"),
    )(h, w2)


def mm_res(a, w, x, scale, name):
    S, K = a.shape
    N = w.shape[1]
    tm, tn = _tile(S, 512), _tile(N, 512)

    def body(a_ref, w_ref, x_ref, o_ref):
        o_ref[...] = x_ref[...] + scale * _dot(a_ref[...], w_ref[...])

    return pl.pallas_call(
        body, name=name, grid=(S // tm, N // tn),
        in_specs=[pl.BlockSpec((tm, K), lambda i, j: (i, 0)), pl.BlockSpec((K, tn), lambda i, j: (0, j)),
                  pl.BlockSpec((tm, tn), lambda i, j: (i, j))],
        out_specs=pl.BlockSpec((tm, tn), lambda i, j: (i, j)),
        out_shape=jax.ShapeDtypeStruct((S, N), f32),
        compiler_params=_params("parallel", "parallel"),
    )(a, w, x)


def mm_nn(a, b, name, out_dtype=f32, scale=1.0, deps=()):
    M, K = a.shape
    G, _, N = b.shape
    tm, tn = _tile(M, 512), _tile(N, 512)

    def body(a_ref, b_ref, *rest):
        acc = _dot(a_ref[...], b_ref[...])
        rest[-1][...] = (acc if scale == 1.0 else scale * acc).astype(out_dtype)

    return pl.pallas_call(
        body, name=name, grid=(G, M // tm, N // tn),
        in_specs=[pl.BlockSpec((tm, K), lambda g, i, j: (i, 0)),
                  pl.BlockSpec((None, K, tn), lambda g, i, j: (g, 0, j))] + [ANY] * len(deps),
        out_specs=pl.BlockSpec((None, tm, tn), lambda g, i, j: (g, i, j)),
        out_shape=jax.ShapeDtypeStruct((G, M, N), out_dtype),
        compiler_params=_params("parallel", "parallel", "parallel"),
    )(a, b, *deps)


def mm_nt(a, w, name):
    G, S, K = a.shape
    N = w.shape[1]
    tm, tn = _tile(S, 512), _tile(N, 512)

    def body(a_ref, w_ref, o_ref):
        part = _dot(a_ref[...], w_ref[...], NT)
        if G == 1:
            o_ref[...] = part
        else:
            g = pl.program_id(2)

            @pl.when(g == 0)
            def _():
                o_ref[...] = part

            @pl.when(g > 0)
            def _():
                o_ref[...] += part

    return pl.pallas_call(
        body, name=name, grid=(S // tm, N // tn, G),
        in_specs=[pl.BlockSpec((None, tm, K), lambda i, j, g: (g, i, 0)),
                  pl.BlockSpec((None, tn, K), lambda i, j, g: (g, j, 0))],
        out_specs=pl.BlockSpec((tm, tn), lambda i, j, g: (i, j)),
        out_shape=jax.ShapeDtypeStruct((S, N), f32),
        compiler_params=_params("parallel", "parallel", "arbitrary"),
    )(a, w)


def ffn_dact(dx, wout, gu, name):
    S, D = dx.shape
    F = wout.shape[0]
    tm, tn = _tile(S, 512), _tile(F, 512)

    def body(dx_ref, w_ref, gu_ref, o_ref):
        da = FFN_RES * _dot(dx_ref[...], w_ref[...], NT)
        g = gu_ref[0].astype(f32)
        u = gu_ref[1].astype(f32)
        sg = _sig(g)
        o_ref[0] = (da * u * (sg * (1.0 + g * (1.0 - sg)))).astype(bf16)
        o_ref[1] = (da * (g * sg)).astype(bf16)

    return pl.pallas_call(
        body, name=name, grid=(S // tm, F // tn),
        in_specs=[pl.BlockSpec((tm, D), lambda i, j: (i, 0)), pl.BlockSpec((tn, D), lambda i, j: (j, 0)),
                  pl.BlockSpec((2, tm, tn), lambda i, j: (0, i, j))],
        out_specs=pl.BlockSpec((2, tm, tn), lambda i, j: (0, i, j)),
        out_shape=jax.ShapeDtypeStruct((2, S, F), bf16),
        compiler_params=_params("parallel", "parallel"),
    )(dx, wout, gu)


def _rope(t, c, s1, s2):
    w = t.shape[1]
    return t * c + pltpu.roll(t, 8, 1) * s1 + pltpu.roll(t, w - 8, 1) * s2


def _rope_t(d, c, s1, s2):
    w = d.shape[1]
    return d * c + pltpu.roll(d * s1, w - 8, 1) + pltpu.roll(d * s2, 8, 1)


def _attn_mask(n):
    qi = lax.broadcasted_iota(jnp.int32, (BLK, 2 * BLK), 0)
    kj = lax.broadcasted_iota(jnp.int32, (BLK, 2 * BLK), 1)
    dist = qi + BLK - kj
    return (dist >= 0) & (dist < BLK) & ((kj >= BLK) | (n > 0))


def _softmax_sink(s, valid, sk):
    s = jnp.where(valid, s, -1e30)
    m = jnp.maximum(jnp.max(s, axis=-1, keepdims=True), sk)
    e = jnp.exp(s - m)
    es = jnp.exp(sk - m)
    inv = 1.0 / (jnp.sum(e, axis=-1, keepdims=True) + es)
    return e * inv, es * inv


def attn_fwd(p, rope_c, rope_s1, rope_s2, sinks, name):
    S = p.shape[0]
    nb = S // BLK
    kvb = Q_END // KV_W

    def body(sink_ref, q_ref, kvc_ref, kvp_ref, cc_ref, s1c_ref, s2c_ref, cp_ref, s1p_ref, s2p_ref, o_ref):
        n = pl.program_id(0)
        cc, s1c, s2c = cc_ref[...], s1c_ref[...], s2c_ref[...]
        cp, s1p, s2p = cp_ref[...], s1p_ref[...], s2p_ref[...]
        q = _rope(q_ref[...], jnp.tile(cc, (1, 8)), jnp.tile(s1c, (1, 8)), jnp.tile(s2c, (1, 8)))
        kc = _rope(kvc_ref[:, :256], jnp.tile(cc, (1, 2)), jnp.tile(s1c, (1, 2)), jnp.tile(s2c, (1, 2)))
        kp = _rope(kvp_ref[:, :256], jnp.tile(cp, (1, 2)), jnp.tile(s1p, (1, 2)), jnp.tile(s2p, (1, 2)))
        k = jnp.concatenate([kp, kc], axis=0).astype(bf16)
        v = jnp.concatenate([kvp_ref[:, 256:], kvc_ref[:, 256:]], axis=0).astype(bf16)
        q = q.astype(bf16)
        valid = _attn_mask(n)
        for h in range(N_KV_HEADS):
            kh = k[:, h * HEAD_DIM:(h + 1) * HEAD_DIM]
            vh = v[:, h * HEAD_DIM:(h + 1) * HEAD_DIM]
            for g in range(GQ):
                hq = h * GQ + g
                qh = q[:, hq * HEAD_DIM:(hq + 1) * HEAD_DIM]
                s = _dot(qh, kh, NT) * (HEAD_DIM ** -0.5)
                pr, _ = _softmax_sink(s, valid, sink_ref[hq])
                o = _dot(pr.astype(bf16), vh)
                o_ref[:, hq * HEAD_DIM:(hq + 1) * HEAD_DIM] = o.astype(bf16)

    tab_c = pl.BlockSpec((BLK, LANE), lambda n: (n, 0))
    tab_p = pl.BlockSpec((BLK, LANE), lambda n: (jnp.maximum(n - 1, 0), 0))
    return pl.pallas_call(
        body, name=name, grid=(nb,),
        in_specs=[pl.BlockSpec(memory_space=pltpu.SMEM),
                  pl.BlockSpec((BLK, Q_END), lambda n: (n, 0)),
                  pl.BlockSpec((BLK, KV_W), lambda n: (n, kvb)),
                  pl.BlockSpec((BLK, KV_W), lambda n: (jnp.maximum(n - 1, 0), kvb)),
                  tab_c, tab_c, tab_c, tab_p, tab_p, tab_p],
        out_specs=pl.BlockSpec((BLK, Q_END), lambda n: (n, 0)),
        out_shape=jax.ShapeDtypeStruct((S, Q_END), bf16),
        compiler_params=_params("parallel"),
    )(sinks, p, p, p, rope_c, rope_s1, rope_s2, rope_c, rope_s1, rope_s2)


def attn_bwd(p, dcat, rope_c, rope_s1, rope_s2, sinks, name):
    S = p.shape[0]
    nb = S // BLK
    kvb = Q_END // KV_W

    def body(sink_ref, q_ref, kvc_ref, kvp_ref, do_ref, cc_ref, s1c_ref, s2c_ref, cp_ref, s1p_ref, s2p_ref,
             dq_ref, dkv_ref, dsink_ref, carry, dq_scr, dkv_scr):
        n = pl.program_id(0)

        @pl.when(n == 0)
        def _():
            carry[...] = jnp.zeros_like(carry)
            dsink_ref[...] = jnp.zeros_like(dsink_ref)

        cp, s1p, s2p = cp_ref[...], s1p_ref[...], s2p_ref[...]
        cp2, s1p2, s2p2 = jnp.tile(cp, (1, 2)), jnp.tile(s1p, (1, 2)), jnp.tile(s2p, (1, 2))

        @pl.when(n < nb)
        def _():
            cc, s1c, s2c = cc_ref[...], s1c_ref[...], s2c_ref[...]
            cc8, s1c8, s2c8 = jnp.tile(cc, (1, 8)), jnp.tile(s1c, (1, 8)), jnp.tile(s2c, (1, 8))
            q = _rope(q_ref[...], cc8, s1c8, s2c8).astype(bf16)
            kc = _rope(kvc_ref[:, :256], jnp.tile(cc, (1, 2)), jnp.tile(s1c, (1, 2)), jnp.tile(s2c, (1, 2)))
            kp = _rope(kvp_ref[:, :256], cp2, s1p2, s2p2)
            k = jnp.concatenate([kp, kc], axis=0).astype(bf16)
            v = jnp.concatenate([kvp_ref[:, 256:], kvc_ref[:, 256:]], axis=0).astype(bf16)
            do = do_ref[...].astype(bf16)
            valid = _attn_mask(n)
            lane = lax.broadcasted_iota(jnp.int32, (1, LANE), 1)
            dsink = jnp.zeros((1, LANE), f32)
            for h in range(N_KV_HEADS):
                kh = k[:, h * HEAD_DIM:(h + 1) * HEAD_DIM]
                vh = v[:, h * HEAD_DIM:(h + 1) * HEAD_DIM]
                dkh = jnp.zeros((2 * BLK, HEAD_DIM), f32)
                dvh = jnp.zeros((2 * BLK, HEAD_DIM), f32)
                for g in range(GQ):
                    hq = h * GQ + g
                    qh = q[:, hq * HEAD_DIM:(hq + 1) * HEAD_DIM]
                    doh = do[:, hq * HEAD_DIM:(hq + 1) * HEAD_DIM]
                    s = _dot(qh, kh, NT) * (HEAD_DIM ** -0.5)
                    pr, ps = _softmax_sink(s, valid, sink_ref[hq])
                    dpr = _dot(doh, vh, NT)
                    dvh = dvh + _dot(pr.astype(bf16), doh, TN)
                    row = jnp.sum(pr * dpr, axis=-1, keepdims=True)
                    ds = (pr * (dpr - row) * (HEAD_DIM ** -0.5)).astype(bf16)
                    dsink = dsink + jnp.where(lane == hq, -jnp.sum(ps * row, axis=0, keepdims=True), 0.0)
                    dq_scr[:, hq * HEAD_DIM:(hq + 1) * HEAD_DIM] = _dot(ds, kh)
                    dkh = dkh + _dot(ds, qh, TN)
                dkv_scr[:, h * HEAD_DIM:(h + 1) * HEAD_DIM] = dkh
                dkv_scr[:, 256 + h * HEAD_DIM:256 + (h + 1) * HEAD_DIM] = dvh
            dsink_ref[...] += dsink
            dq_ref[...] = _rope_t(dq_scr[...], cc8, s1c8, s2c8).astype(bf16)

        prev = carry[...]

        @pl.when(n < nb)
        def _():
            dkv_scr[pl.ds(0, BLK), :] = dkv_scr[pl.ds(0, BLK), :] + prev

        @pl.when(n == nb)
        def _():
            dkv_scr[pl.ds(0, BLK), :] = prev

        done = dkv_scr[pl.ds(0, BLK), :]
        dkv_ref[:, :256] = _rope_t(done[:, :256], cp2, s1p2, s2p2).astype(bf16)
        dkv_ref[:, 256:] = done[:, 256:].astype(bf16)

        @pl.when(n < nb)
        def _():
            carry[...] = dkv_scr[pl.ds(BLK, BLK), :]

    cur = lambda n: jnp.minimum(n, nb - 1)
    prv = lambda n: jnp.maximum(n - 1, 0)
    tab_c = pl.BlockSpec((BLK, LANE), lambda n: (cur(n), 0))
    tab_p = pl.BlockSpec((BLK, LANE), lambda n: (prv(n), 0))
    return pl.pallas_call(
        body, name=name, grid=(nb + 1,),
        in_specs=[pl.BlockSpec(memory_space=pltpu.SMEM),
                  pl.BlockSpec((BLK, Q_END), lambda n: (cur(n), 0)),
                  pl.BlockSpec((BLK, KV_W), lambda n: (cur(n), kvb)),
                  pl.BlockSpec((BLK, KV_W), lambda n: (prv(n), kvb)),
                  pl.BlockSpec((BLK, Q_END), lambda n: (cur(n), 0)),
                  tab_c, tab_c, tab_c, tab_p, tab_p, tab_p],
        out_specs=[pl.BlockSpec((BLK, Q_END), lambda n: (cur(n), 0)),
                   pl.BlockSpec((BLK, KV_W), lambda n: (prv(n), 0)),
                   pl.BlockSpec((1, LANE), lambda n: (0, 0))],
        out_shape=[jax.ShapeDtypeStruct((S, Q_END), bf16), jax.ShapeDtypeStruct((S, KV_W), bf16),
                   jax.ShapeDtypeStruct((1, LANE), f32)],
        scratch_shapes=[pltpu.VMEM((BLK, KV_W), f32), pltpu.VMEM((BLK, Q_END), f32), pltpu.VMEM((2 * BLK, KV_W), f32)],
        compiler_params=_params("arbitrary"),
    )(sinks, p, p, p, dcat, rope_c, rope_s1, rope_s2, rope_c, rope_s1, rope_s2)


A1_BLK = V_END // CONV_CH
A2_BLK = A1_BLK + 1


def _ln_stats(y):
    mu = jnp.mean(y, axis=-1, keepdims=True)
    xc = y - mu
    rstd = lax.rsqrt(jnp.mean(xc * xc, axis=-1, keepdims=True) + NORM_EPS)
    return xc * rstd, rstd


def conv_fwd(p, w, b, lng, lnb, name):
    S = p.shape[0]
    T = _tile(S, 256)
    r = T // HALO

    def body(a1_ref, a2_ref, h1_ref, h2_ref, w_ref, b_ref, g_ref, bb_ref, o_ref, y_ref, scr):
        i = pl.program_id(0)
        halo = h1_ref[...] * _sig(h2_ref[...])
        scr[pl.ds(0, HALO), :] = jnp.where(i > 0, halo, 0.0)
        scr[pl.ds(HALO, T), :] = a1_ref[...] * _sig(a2_ref[...])
        acc = jnp.zeros((T, CONV_CH), f32) + b_ref[...]
        for j in range(CONV_W):
            acc = acc + scr[pl.ds(HALO - (CONV_W - 1) + j, T), :] * w_ref[j:j + 1, :]
        y_ref[...] = acc
        yh, _ = _ln_stats(acc)
        z = yh * g_ref[...] + bb_ref[...]
        o_ref[...] = (z * _sig(z)).astype(bf16)

    vec = pl.BlockSpec((1, CONV_CH), lambda i: (0, 0))
    halo_map = lambda i: jnp.maximum(i * r - 1, 0)
    return pl.pallas_call(
        body, name=name, grid=(S // T,),
        in_specs=[pl.BlockSpec((T, CONV_CH), lambda i: (i, A1_BLK)), pl.BlockSpec((T, CONV_CH), lambda i: (i, A2_BLK)),
                  pl.BlockSpec((HALO, CONV_CH), lambda i: (halo_map(i), A1_BLK)),
                  pl.BlockSpec((HALO, CONV_CH), lambda i: (halo_map(i), A2_BLK)),
                  pl.BlockSpec((HALO, CONV_CH), lambda i: (0, 0)), vec, vec, vec],
        out_specs=[pl.BlockSpec((T, CONV_CH), lambda i: (i, 0)), pl.BlockSpec((T, CONV_CH), lambda i: (i, 0))],
        out_shape=[jax.ShapeDtypeStruct((S, CONV_CH), bf16), jax.ShapeDtypeStruct((S, CONV_CH), f32)],
        scratch_shapes=[pltpu.VMEM((T + HALO, CONV_CH), f32)],
        compiler_params=_params("parallel"),
    )(p, p, p, p, w, b, lng, lnb)


def conv_bwd(p, y, dcat, w, lng, lnb, name):
    S = p.shape[0]
    T = _tile(S, 256)
    n = S // T
    r = T // HALO
    dcb = Q_END // CONV_CH

    def body(a1_ref, a2_ref, h1_ref, h2_ref, y_ref, yn_ref, do_ref, don_ref, w_ref, g_ref, bb_ref,
             da_ref, dw_ref, db_ref, dg_ref, dbb_ref, scr_h, scr_dy, acc_b, acc_g, acc_bb):
        i = pl.program_id(0)

        @pl.when(i == 0)
        def _():
            dw_ref[...] = jnp.zeros_like(dw_ref)
            acc_b[...] = jnp.zeros_like(acc_b)
            acc_g[...] = jnp.zeros_like(acc_g)
            acc_bb[...] = jnp.zeros_like(acc_bb)

        gv, bv = g_ref[...], bb_ref[...]

        def ln_silu_bwd(yv, dout):
            yh, rstd = _ln_stats(yv)
            z = yh * gv + bv
            sg = _sig(z)
            dz = dout * (sg * (1.0 + z * (1.0 - sg)))
            gz = dz * gv
            dy = rstd * (gz - jnp.mean(gz, axis=-1, keepdims=True) - yh * jnp.mean(gz * yh, axis=-1, keepdims=True))
            return dy, dz, yh

        dy, dz, yh = ln_silu_bwd(y_ref[...], do_ref[...])
        dyn, _, _ = ln_silu_bwd(yn_ref[...], don_ref[...])
        acc_g[...] += _rowsum8(dz * yh)
        acc_bb[...] += _rowsum8(dz)
        acc_b[...] += _rowsum8(dy)
        scr_dy[pl.ds(0, T), :] = dy
        scr_dy[pl.ds(T, HALO), :] = jnp.where(i < n - 1, dyn, 0.0)
        a1, a2 = a1_ref[...], a2_ref[...]
        sg2 = _sig(a2)
        halo = h1_ref[...] * _sig(h2_ref[...])
        scr_h[pl.ds(0, HALO), :] = jnp.where(i > 0, halo, 0.0)
        scr_h[pl.ds(HALO, T), :] = a1 * sg2
        dh = jnp.zeros((T, CONV_CH), f32)
        for j in range(CONV_W):
            dh = dh + scr_dy[pl.ds(CONV_W - 1 - j, T), :] * w_ref[j:j + 1, :]
            dw_ref[j:j + 1, :] += jnp.sum(dy * scr_h[pl.ds(HALO - (CONV_W - 1) + j, T), :], axis=0, keepdims=True)
        da_ref[:, :CONV_CH] = (dh * sg2).astype(bf16)
        da_ref[:, CONV_CH:] = (dh * a1 * sg2 * (1.0 - sg2)).astype(bf16)

        @pl.when(i == n - 1)
        def _():
            db_ref[...] = jnp.sum(acc_b[...], axis=0, keepdims=True)
            dg_ref[...] = jnp.sum(acc_g[...], axis=0, keepdims=True)
            dbb_ref[...] = jnp.sum(acc_bb[...], axis=0, keepdims=True)

    vec = pl.BlockSpec((1, CONV_CH), lambda i: (0, 0))
    tap = pl.BlockSpec((HALO, CONV_CH), lambda i: (0, 0))
    prev_map = lambda i: jnp.maximum(i * r - 1, 0)
    next_map = lambda i: jnp.minimum((i + 1) * r, S // HALO - 1)
    return pl.pallas_call(
        body, name=name, grid=(n,),
        in_specs=[pl.BlockSpec((T, CONV_CH), lambda i: (i, A1_BLK)), pl.BlockSpec((T, CONV_CH), lambda i: (i, A2_BLK)),
                  pl.BlockSpec((HALO, CONV_CH), lambda i: (prev_map(i), A1_BLK)),
                  pl.BlockSpec((HALO, CONV_CH), lambda i: (prev_map(i), A2_BLK)),
                  pl.BlockSpec((T, CONV_CH), lambda i: (i, 0)),
                  pl.BlockSpec((HALO, CONV_CH), lambda i: (next_map(i), 0)),
                  pl.BlockSpec((T, CONV_CH), lambda i: (i, dcb)),
                  pl.BlockSpec((HALO, CONV_CH), lambda i: (next_map(i), dcb)),
                  tap, vec, vec],
        out_specs=[pl.BlockSpec((T, 2 * CONV_CH), lambda i: (i, 0)), tap, vec, vec, vec],
        out_shape=[jax.ShapeDtypeStruct((S, 2 * CONV_CH), bf16), jax.ShapeDtypeStruct((HALO, CONV_CH), f32),
                   jax.ShapeDtypeStruct((1, CONV_CH), f32), jax.ShapeDtypeStruct((1, CONV_CH), f32),
                   jax.ShapeDtypeStruct((1, CONV_CH), f32)],
        scratch_shapes=[pltpu.VMEM((T + HALO, CONV_CH), f32), pltpu.VMEM((T + HALO, CONV_CH), f32),
                        pltpu.VMEM((8, CONV_CH), f32), pltpu.VMEM((8, CONV_CH), f32), pltpu.VMEM((8, CONV_CH), f32)],
        compiler_params=_params("arbitrary"),
    )(p, p, p, p, y, y, dcat, dcat, w, lng, lnb)


U_BLK = (V_END + 2 * CONV_CH) // SGU_CH
SV_BLK = U_BLK + 1


def _tril(w, transposed=False):
    row = lax.broadcasted_iota(jnp.int32, (BLK, BLK), 0)
    col = lax.broadcasted_iota(jnp.int32, (BLK, BLK), 1)
    keep = (col >= row) if transposed else (row >= col)
    return jnp.where(keep, w, 0.0)


def sgu_fwd(p, lng, lnb, w, bias, name):
    S = p.shape[0]
    T = _tile(S, 256)

    def body(u_ref, v_ref, g_ref, bb_ref, w_ref, bias_ref, o_ref):
        yh, _ = _ln_stats(v_ref[...])
        v = (yh * g_ref[...] + bb_ref[...]).astype(bf16)
        low = lax.broadcasted_iota(jnp.int32, (BLK, LANE), 1) < HEAD_DIM
        for pr in range(SGU_HEADS // 2):
            lanes = pl.ds(pr * LANE, LANE)
            w0 = _tril(w_ref[2 * pr]).astype(bf16)
            w1 = _tril(w_ref[2 * pr + 1]).astype(bf16)
            for c in range(T // BLK):
                rows = pl.ds(c * BLK, BLK)
                vp = v[c * BLK:(c + 1) * BLK, pr * LANE:(pr + 1) * LANE]
                mixed = jnp.where(low, _dot(w0, vp), _dot(w1, vp)) + bias_ref[:, lanes]
                o_ref[rows, lanes] = (u_ref[rows, lanes] * mixed).astype(bf16)

    vec = pl.BlockSpec((1, SGU_CH), lambda i: (0, 0))
    return pl.pallas_call(
        body, name=name, grid=(S // T,),
        in_specs=[pl.BlockSpec((T, SGU_CH), lambda i: (i, U_BLK)), pl.BlockSpec((T, SGU_CH), lambda i: (i, SV_BLK)),
                  vec, vec, pl.BlockSpec((SGU_HEADS, BLK, BLK), lambda i: (0, 0, 0)),
                  pl.BlockSpec((BLK, SGU_CH), lambda i: (0, 0))],
        out_specs=pl.BlockSpec((T, SGU_CH), lambda i: (i, 0)),
        out_shape=jax.ShapeDtypeStruct((S, SGU_CH), bf16),
        compiler_params=_params("parallel"),
    )(p, p, lng, lnb, w, bias)


def sgu_bwd(p, dcat, lng, lnb, w, wt, bias, name):
    S = p.shape[0]
    T = _tile(S, 256)
    n = S // T
    dsb = (Q_END + CONV_CH) // SGU_CH

    def body(u_ref, v_ref, do_ref, g_ref, bb_ref, w_ref, wt_ref, bias_ref,
             da_ref, dw_ref, db_ref, dg_ref, dbb_ref, dv_scr, acc_bias, acc_g, acc_bb):
        i = pl.program_id(0)

        @pl.when(i == 0)
        def _():
            dw_ref[...] = jnp.zeros_like(dw_ref)
            acc_bias[...] = jnp.zeros_like(acc_bias)
            acc_g[...] = jnp.zeros_like(acc_g)
            acc_bb[...] = jnp.zeros_like(acc_bb)

        gv = g_ref[...]
        yh, rstd = _ln_stats(v_ref[...])
        v = (yh * gv + bb_ref[...]).astype(bf16)
        low = lax.broadcasted_iota(jnp.int32, (BLK, LANE), 1) < HEAD_DIM
        for pr in range(SGU_HEADS // 2):
            lanes = pl.ds(pr * LANE, LANE)
            w0 = _tril(w_ref[2 * pr]).astype(bf16)
            w1 = _tril(w_ref[2 * pr + 1]).astype(bf16)
            wt0 = _tril(wt_ref[2 * pr], True).astype(bf16)
            wt1 = _tril(wt_ref[2 * pr + 1], True).astype(bf16)
            dw0 = jnp.zeros((BLK, BLK), f32)
            dw1 = jnp.zeros((BLK, BLK), f32)
            for c in range(T // BLK):
                rows = pl.ds(c * BLK, BLK)
                vp = v[c * BLK:(c + 1) * BLK, pr * LANE:(pr + 1) * LANE]
                mixed = jnp.where(low, _dot(w0, vp), _dot(w1, vp)) + bias_ref[:, lanes]
                do = do_ref[rows, lanes]
                da_ref[rows, lanes] = (do * mixed).astype(bf16)
                dm = do * u_ref[rows, lanes]
                acc_bias[:, lanes] += dm
                dmb = dm.astype(bf16)
                dv_scr[rows, lanes] = jnp.where(low, _dot(wt0, dmb), _dot(wt1, dmb))
                zero = jnp.zeros_like(dmb)
                dw0 = dw0 + _dot(jnp.where(low, dmb, zero), vp, NT)
                dw1 = dw1 + _dot(jnp.where(low, zero, dmb), vp, NT)
            dw_ref[2 * pr] += _tril(dw0)
            dw_ref[2 * pr + 1] += _tril(dw1)
        dv = dv_scr[...]
        acc_g[...] += _rowsum8(dv * yh)
        acc_bb[...] += _rowsum8(dv)
        gz = dv * gv
        dvr = rstd * (gz - jnp.mean(gz, axis=-1, keepdims=True) - yh * jnp.mean(gz * yh, axis=-1, keepdims=True))
        da_ref[:, SGU_CH:] = dvr.astype(bf16)

        @pl.when(i == n - 1)
        def _():
            ch = lax.broadcasted_iota(jnp.int32, (SGU_CH, LANE), 0) // HEAD_DIM
            hd = lax.broadcasted_iota(jnp.int32, (SGU_CH, LANE), 1)
            fold = jnp.where(ch == hd, 1.0, 0.0).astype(f32)
            db_ref[...] = jnp.dot(acc_bias[...], fold, preferred_element_type=f32, precision=lax.Precision.HIGHEST)
            dg_ref[...] = jnp.sum(acc_g[...], axis=0, keepdims=True)
            dbb_ref[...] = jnp.sum(acc_bb[...], axis=0, keepdims=True)

    vec = pl.BlockSpec((1, SGU_CH), lambda i: (0, 0))
    wsp = pl.BlockSpec((SGU_HEADS, BLK, BLK), lambda i: (0, 0, 0))
    return pl.pallas_call(
        body, name=name, grid=(n,),
        in_specs=[pl.BlockSpec((T, SGU_CH), lambda i: (i, U_BLK)), pl.BlockSpec((T, SGU_CH), lambda i: (i, SV_BLK)),
                  pl.BlockSpec((T, SGU_CH), lambda i: (i, dsb)), vec, vec, wsp, wsp,
                  pl.BlockSpec((BLK, SGU_CH), lambda i: (0, 0))],
        out_specs=[pl.BlockSpec((T, 2 * SGU_CH), lambda i: (i, 0)), wsp,
                   pl.BlockSpec((BLK, LANE), lambda i: (0, 0)), vec, vec],
        out_shape=[jax.ShapeDtypeStruct((S, 2 * SGU_CH), bf16), jax.ShapeDtypeStruct((SGU_HEADS, BLK, BLK), f32),
                   jax.ShapeDtypeStruct((BLK, LANE), f32), jax.ShapeDtypeStruct((1, SGU_CH), f32),
                   jax.ShapeDtypeStruct((1, SGU_CH), f32)],
        scratch_shapes=[pltpu.VMEM((T, SGU_CH), f32), pltpu.VMEM((BLK, SGU_CH), f32),
                        pltpu.VMEM((8, SGU_CH), f32), pltpu.VMEM((8, SGU_CH), f32)],
        compiler_params=_params("arbitrary"),
    )(p, p, dcat, lng, lnb, w, wt, bias)


def _me_and_peers():
    x, y, c = lax.axis_index("x"), lax.axis_index("y"), lax.axis_index("c")
    me = 4 * x + 2 * y + c
    peers = []
    for k in range(1, N_DEV):
        px, py, pc = x ^ (k >> 2), y ^ ((k >> 1) & 1), c ^ (k & 1)
        peers.append(((px, py, pc), 4 * px + 2 * py + pc))
    return me, peers


HBM = pl.BlockSpec(memory_space=pltpu.HBM)
SEM = pl.BlockSpec(memory_space=pltpu.SEMAPHORE)
ANY = pl.BlockSpec(memory_space=pl.ANY)
EFFECT = pltpu.SideEffectType.DATAFLOW_SIDE_EFFECTING


def _flip(x, y, c, k):
    px, py, pc = x ^ (k >> 2), y ^ ((k >> 1) & 1), c ^ (k & 1)
    return (px, py, pc), 4 * px + 2 * py + pc


def _routes_gather(x, y, c):
    me = 4 * x + 2 * y + c
    out = []
    for k in (1, 2, 4, 6):
        dev, idx = _flip(x, y, c, k)
        out.append((dev, None, me, idx))
    return out


def _routes_pair(x, y, c):
    dev, _ = _flip(x, y, c, 1)
    return [(dev, 2 * q + (1 - c), q, q) for q in range(N_DEV // 2)]


def _routes_chips(x, y, c):
    out = []
    for k in (2, 4, 6):
        dev, idx = _flip(x, y, c, k)
        out.append((dev, idx // 2, 2 * x + y, idx // 2))
    return out


def _copies(routes, srcs, lands, send_sems, recv_sems, incoming):
    x, y, c = lax.axis_index("x"), lax.axis_index("y"), lax.axis_index("c")
    out = []
    n = len(srcs)
    for k, (dev, src_slot, dst_slot, recv_slot) in enumerate(routes(x, y, c)):
        for a in range(n):
            src = srcs[a] if src_slot is None else srcs[a].at[src_slot]
            out.append(pltpu.make_async_remote_copy(
                src_ref=src, dst_ref=lands[a].at[recv_slot if incoming else dst_slot],
                send_sem=send_sems.at[k * n + a], recv_sem=recv_sems.at[k * n + a], device_id=dev, device_id_type=MESH))
    return out


def _pin(a):
    return pltpu.with_memory_space_constraint(a, pltpu.HBM)


def split_start(srcs, lands, routes, name):
    n = len(srcs)
    n_routes = len(routes(0, 0, 0))

    def body(*refs):
        src, land = refs[:n], refs[n:2 * n]
        send_sems, recv_sems, token = refs[2 * n], refs[2 * n + 1], refs[-1]
        for cp in _copies(routes, src, land, send_sems, recv_sems, False):
            cp.start()
        token[...] = jnp.zeros_like(token)

    thru = [pltpu.HBM(a.shape, a.dtype) for a in list(srcs) + list(lands)]
    res = pl.pallas_call(
        body, name=name,
        out_shape=(pltpu.SemaphoreType.DMA((n * n_routes,)), pltpu.SemaphoreType.DMA((n * n_routes,)), *thru,
                   jax.ShapeDtypeStruct((8, LANE), f32)),
        in_specs=[HBM] * (2 * n), out_specs=(SEM, SEM, *([HBM] * (2 * n)), pl.BlockSpec(memory_space=pltpu.VMEM)),
        input_output_aliases={i: 2 + i for i in range(2 * n)},
        compiler_params=pltpu.CompilerParams(has_side_effects=EFFECT),
    )(*[_pin(a) for a in srcs], *[_pin(a) for a in lands])
    return res[0], res[1], list(res[2:2 + n]), list(res[2 + n:2 + 2 * n]), res[-1]


def split_wait(started, after, routes, name):
    send_sems, recv_sems, srcs, lands, _ = started
    n = len(srcs)

    def body(*refs):
        src, land = refs[:n], refs[n:2 * n]
        send_s, recv_s = refs[2 * n], refs[2 * n + 1]
        for cp in _copies(routes, src, land, send_s, recv_s, True):
            cp.wait_send()
            cp.wait_recv()

    thru = [pltpu.HBM(a.shape, a.dtype) for a in srcs + lands]
    res = pl.pallas_call(
        body, name=name, out_shape=tuple(thru),
        in_specs=[HBM] * (2 * n) + [SEM, SEM, ANY], out_specs=tuple([HBM] * (2 * n)),
        input_output_aliases={i: i for i in range(2 * n)},
        compiler_params=pltpu.CompilerParams(has_side_effects=EFFECT),
    )(*srcs, *lands, send_sems, recv_sems, after)
    return list(res[:n]), list(res[n:])


def gather_forward(srcs, lands, name):
    n = len(srcs)

    def body(*refs):
        src, land = refs[:n], refs[n:2 * n]
        send_sems, recv_sems, local_sems = refs[3 * n:]
        out = refs[2 * n:3 * n]
        x, y, c = lax.axis_index("x"), lax.axis_index("y"), lax.axis_index("c")
        me = 4 * x + 2 * y + c
        sib, _ = _flip(x, y, c, 1)
        local = [pltpu.make_async_copy(src[a], out[a].at[me], local_sems.at[a]) for a in range(n)]
        for cp in local:
            cp.start()
        sends = []
        for k, mask in enumerate((2, 4, 6)):
            _, idx = _flip(x, y, c, mask)
            for a in range(n):
                cp = pltpu.make_async_remote_copy(
                    src_ref=land[a].at[idx], dst_ref=out[a].at[idx], send_sem=send_sems.at[a, k],
                    recv_sem=recv_sems.at[a, k], device_id=sib, device_id_type=MESH)
                cp.start()
                sends.append(cp)
        for k, mask in enumerate((2, 4, 6)):
            _, idx = _flip(x, y, c, mask ^ 1)
            for a in range(n):
                pltpu.make_async_remote_copy(
                    src_ref=land[a].at[idx], dst_ref=out[a].at[idx], send_sem=send_sems.at[a, k],
                    recv_sem=recv_sems.at[a, k], device_id=sib, device_id_type=MESH).wait_recv()
        for cp in sends:
            cp.wait_send()
        for cp in local:
            cp.wait()

    return pl.pallas_call(
        body, name=name,
        in_specs=[ANY] * (2 * n), out_specs=[ANY] * n,
        out_shape=[jax.ShapeDtypeStruct(a.shape, a.dtype) for a in lands],
        input_output_aliases={n + a: a for a in range(n)},
        scratch_shapes=[pltpu.SemaphoreType.DMA((n, 3)), pltpu.SemaphoreType.DMA((n, 3)), pltpu.SemaphoreType.DMA((n,))],
        compiler_params=pltpu.CompilerParams(has_side_effects=True),
    )(*srcs, *lands)


def chip_sum(parts, land, c_idx, name):
    _, R, C = parts.shape
    tr = _tile(R, 256)

    def body(c_ref, p_ref, l_ref, o_ref):
        o_ref[...] = (p_ref[...].astype(f32) + l_ref[...].astype(f32)).astype(bf16)

    return pl.pallas_call(
        body, name=name,
        grid_spec=pltpu.PrefetchScalarGridSpec(
            num_scalar_prefetch=1, grid=(N_DEV // 2, R // tr),
            in_specs=[pl.BlockSpec((None, tr, C), lambda q, i, c_ref: (2 * q + c_ref[0], i, 0)),
                      pl.BlockSpec((None, tr, C), lambda q, i, c_ref: (q, i, 0))],
            out_specs=pl.BlockSpec((None, tr, C), lambda q, i, c_ref: (q, i, 0))),
        out_shape=jax.ShapeDtypeStruct((N_DEV // 2, R, C), bf16),
        compiler_params=_params("parallel", "parallel"),
    )(c_idx, parts, land)


def allreduce_small(buf, name):
    R = buf.shape[0]

    def body(src, out, land, send_sems, recv_sems):
        me, peers = _me_and_peers()
        land[me] = src[...]
        sends = []
        for k, (dev, idx) in enumerate(peers):
            cp = pltpu.make_async_remote_copy(src_ref=src, dst_ref=land.at[me], send_sem=send_sems.at[k],
                                              recv_sem=recv_sems.at[k], device_id=dev, device_id_type=MESH)
            cp.start()
            sends.append(cp)
        for k, (dev, idx) in enumerate(peers):
            pltpu.make_async_remote_copy(src_ref=src, dst_ref=land.at[idx], send_sem=send_sems.at[k],
                                         recv_sem=recv_sems.at[k], device_id=dev, device_id_type=MESH).wait_recv()
        for cp in sends:
            cp.wait_send()
        total = land[0]
        for j in range(1, N_DEV):
            total = total + land[j]
        out[...] = total

    vm = pl.BlockSpec(memory_space=pltpu.VMEM)
    return pl.pallas_call(
        body, name=name, in_specs=[vm], out_specs=vm, out_shape=jax.ShapeDtypeStruct((R, LANE), f32),
        scratch_shapes=[pltpu.VMEM((N_DEV, R, LANE), f32), pltpu.SemaphoreType.DMA((N_DEV - 1,)),
                        pltpu.SemaphoreType.DMA((N_DEV - 1,))],
        compiler_params=pltpu.CompilerParams(has_side_effects=True, vmem_limit_bytes=VMEM_LIMIT),
    )(buf)


def adamw(parts, owns, chip, w, m, v, name):
    L, R, C = w.shape
    P = parts[0].shape[0]
    tr = _tile(R, 128 if C > 1024 else 256)
    nr = R // tr
    c1 = 1.0 - ADAM_B1 ** ADAM_STEP
    c2 = 1.0 - ADAM_B2 ** ADAM_STEP
    n_own = L if owns is not None else 0

    def body(chip_ref, *refs):
        part_refs, own_refs = refs[:L], refs[L:L + n_own]
        w_ref, m_ref, v_ref, g_out, d_out, m_out, v_out = refs[L + n_own:]
        layer = pl.program_id(0)
        for l in range(L):
            @pl.when(layer == l)
            def _(l=l):
                g = None
                for q in range(P):
                    term = part_refs[l][q].astype(f32)
                    if n_own:
                        term = jnp.where(chip_ref[0] == q, own_refs[l][...].astype(f32), term)
                    g = term if g is None else g + term
                mn = ADAM_B1 * m_ref[...] + (1.0 - ADAM_B1) * g
                vn = ADAM_B2 * v_ref[...] + (1.0 - ADAM_B2) * (g * g)
                g_out[...] = g
                m_out[...] = mn
                v_out[...] = vn
                d_out[...] = -ADAM_LR * ((mn / c1) / (jnp.sqrt(vn / c2) + ADAM_EPS) + ADAM_WD * w_ref[...])

    def rows(l, a, i):
        return jnp.where(a == l, i, jnp.where(a < l, 0, nr - 1))

    def part_spec(l):
        return pl.BlockSpec((P, tr, C), lambda a, i, chip_ref: (0, rows(l, a, i), 0))

    def own_spec(l):
        return pl.BlockSpec((None, tr, C), lambda a, i, chip_ref: (chip_ref[0], rows(l, a, i), 0))

    slab = pl.BlockSpec((None, tr, C), lambda a, i, chip_ref: (a, i, 0))
    out = jax.ShapeDtypeStruct((L, R, C), f32)
    return pl.pallas_call(
        body, name=name,
        grid_spec=pltpu.PrefetchScalarGridSpec(
            num_scalar_prefetch=1, grid=(L, nr),
            in_specs=[part_spec(l) for l in range(L)] + [own_spec(l) for l in range(n_own)] + [slab, slab, slab],
            out_specs=[slab, slab, slab, slab]),
        out_shape=[out, out, out, out],
        compiler_params=_params("arbitrary", "arbitrary"),
    )(chip, *parts, *(owns or []), w, m, v)


PACK = 8 * LANE


def _pack(arrs):
    pieces = []
    for a in arrs:
        flat = a.astype(f32).reshape(-1)
        pad = (-flat.shape[0]) % PACK
        pieces.append(jnp.pad(flat, (0, pad)).reshape(-1, LANE))
    return jnp.concatenate(pieces, axis=0)


def _unpack(buf, shapes):
    out, row = [], 0
    for shp in shapes:
        size = math.prod(shp)
        rows = (size + PACK - 1) // PACK * (PACK // LANE)
        out.append(buf[row:row + rows].reshape(-1)[:size].reshape(shp))
        row += rows
    return out


def _rope_tables(positions):
    half = ROT_DIM // 2
    inv_freq = 1.0 / (ROPE_THETA ** (jnp.arange(0, ROT_DIM, 2, dtype=f32) / ROT_DIM))
    ang = positions.astype(f32)[:, None] * inv_freq
    cos, sin = jnp.cos(ang), jnp.sin(ang)
    S = positions.shape[0]
    zeros, ones = jnp.zeros((S, half), f32), jnp.ones((S, HEAD_DIM - ROT_DIM), f32)
    rest = jnp.zeros((S, HEAD_DIM - ROT_DIM), f32)
    c = jnp.concatenate([cos, cos, ones], axis=1)
    s1 = jnp.concatenate([zeros, sin, rest], axis=1)
    s2 = jnp.concatenate([-sin, zeros, rest], axis=1)
    return tuple(jnp.tile(t, (1, LANE // HEAD_DIM)) for t in (c, s1, s2))


def _cols_to_shards(g):
    lead, (R, N) = g.shape[:-2], g.shape[-2:]
    g = g.reshape(lead + (R, N_DEV, N // N_DEV))
    return jnp.moveaxis(g, -2, 0)


def _shards_to_cols(g):
    g = jnp.moveaxis(g, 0, -2)
    return g.reshape(g.shape[:-2] + (g.shape[-2] * g.shape[-1],))


def kernel(x, positions, norm_ffn1, ffn1_w_in, ffn1_w_out, norm_mix, w_in, conv_dw_w, conv_dw_b, conv_ln_g, conv_ln_b, sgu_ln_g, sgu_ln_b, sgu_w, sgu_b, attn_sinks, w_out, norm_ffn2, ffn2_w_in, ffn2_w_out, final_norm, loss_target, m_norm_ffn1, m_ffn1_w_in, m_ffn1_w_out, m_norm_mix, m_w_in, m_conv_dw_w, m_conv_dw_b, m_conv_ln_g, m_conv_ln_b, m_sgu_ln_g, m_sgu_ln_b, m_sgu_w, m_sgu_b, m_attn_sinks, m_w_out, m_norm_ffn2, m_ffn2_w_in, m_ffn2_w_out, m_final_norm, v_norm_ffn1, v_ffn1_w_in, v_ffn1_w_out, v_norm_mix, v_w_in, v_conv_dw_w, v_conv_dw_b, v_conv_ln_g, v_conv_ln_b, v_sgu_ln_g, v_sgu_ln_b, v_sgu_w, v_sgu_b, v_attn_sinks, v_w_out, v_norm_ffn2, v_ffn2_w_in, v_ffn2_w_out, v_final_norm):
    L = norm_ffn1.shape[0]
    S, D = x.shape[1], x.shape[2]
    F = ffn1_w_out.shape[1] * N_DEV
    me = 4 * lax.axis_index("x") + 2 * lax.axis_index("y") + lax.axis_index("c")
    x0 = x[0]
    rope_c, rope_s1, rope_s2 = _rope_tables(positions[0])
    cw = CONV_CH // N_DEV

    c_idx = lax.axis_index("c").astype(jnp.int32).reshape(1)
    chip = (2 * lax.axis_index("x") + lax.axis_index("y")).astype(jnp.int32).reshape(1)
    no_chip = jnp.zeros((1,), jnp.int32)

    def gather_start(l):
        taps = jnp.pad(conv_dw_w[l], ((0, HALO - CONV_W), (0, LANE - cw)))
        srcs = [ffn1_w_in[l].astype(bf16), ffn1_w_out[l].astype(bf16), w_in[l].astype(bf16),
                w_out[l].astype(bf16), ffn2_w_in[l].astype(bf16), ffn2_w_out[l].astype(bf16), taps]
        lands = [lax.empty((N_DEV,) + a.shape, a.dtype) for a in srcs]
        return split_start(srcs, lands, _routes_gather, f"gather_start_{l}")

    def gather_finish(started, after, l):
        srcs, lands = split_wait(started, after, _routes_gather, f"gather_wait_{l}")
        f1i, f1o, wi, wo, f2i, f2o, tp = gather_forward(srcs, lands, f"gather_forward_{l}")
        return dict(
            f1_in=_shards_to_cols(f1i).reshape(D, 2, F).transpose(1, 0, 2),
            f1_out=f1o.reshape(F, D),
            w_in=_shards_to_cols(wi),
            w_out=wo.reshape(D, D),
            f2_in=_shards_to_cols(f2i).reshape(D, 2, F).transpose(1, 0, 2),
            f2_out=f2o.reshape(F, D),
            taps=_shards_to_cols(tp[:, :, :cw]),
        )

    row = lambda a: a.reshape(1, -1)

    weights, saved = [], []
    xs = x0
    pending = gather_start(0)
    for l in range(L):
        wts = gather_finish(pending, pending[4] if l == 0 else xs, l)
        weights.append(wts)
        pending = gather_start(l + 1) if l + 1 < L else None
        sv = dict(x_a=xs)
        h, sv["h1t"] = rmsnorm_fwd(xs, row(norm_ffn1[l]), f"norm_ffn1_{l}", deps=(pending[4],) if pending else ())
        sv["gu1"], a, sv["a1t"] = ffn_in(h, wts["f1_in"], f"ffn1_in_{l}")
        xs = mm_res(a, wts["f1_out"], xs, FFN_RES, f"ffn1_out_{l}")
        sv["x_b"] = xs
        h, sv["h2t"] = rmsnorm_fwd(xs, row(norm_mix[l]), f"norm_mix_{l}")
        p = mm_nn(h, wts["w_in"][None], f"mix_in_{l}")[0]
        sv["p"] = p
        attn = attn_fwd(p, rope_c, rope_s1, rope_s2, attn_sinks[l], f"attn_fwd_{l}")
        conv, sv["conv_y"] = conv_fwd(p, wts["taps"], row(conv_dw_b[l]), row(conv_ln_g[l]), row(conv_ln_b[l]),
                                      f"conv_fwd_{l}")
        sv["sgu_bias"] = jnp.repeat(sgu_b[l].T, HEAD_DIM, axis=1)
        sgu = sgu_fwd(p, row(sgu_ln_g[l]), row(sgu_ln_b[l]), sgu_w[l], sv["sgu_bias"], f"sgu_fwd_{l}")
        cat = jnp.concatenate([attn, conv, sgu], axis=1)
        sv["catt"] = cat.T
        xs = mm_res(cat, wts["w_out"], xs, 1.0, f"mix_out_{l}")
        sv["x_c"] = xs
        h, sv["h3t"] = rmsnorm_fwd(xs, row(norm_ffn2[l]), f"norm_ffn2_{l}")
        sv["gu2"], a, sv["a2t"] = ffn_in(h, wts["f2_in"], f"ffn2_in_{l}")
        xs = mm_res(a, wts["f2_out"], xs, FFN_RES, f"ffn2_out_{l}")
        saved.append(sv)

    dx, dxb, d_final_norm, loss = final_loss(xs, row(final_norm), loss_target[0], "final_loss")

    def pair_to_chips(started, after, l):
        grads, landed = split_wait(started, after, _routes_pair, f"pair_wait_{l}")
        sums = [chip_sum(g, ld, c_idx, f"chip_sum_{l}_{a}") for a, (g, ld) in enumerate(zip(grads, landed))]
        return split_start(sums, [lax.empty(s.shape, s.dtype) for s in sums], _routes_chips, f"chips_start_{l}")

    small = [None] * L
    chip_sums, chip_lands = [None] * L, [None] * L
    pair_pending = chips_pending = None
    for l in reversed(range(L)):
        wts, sv = weights[l], saved[l]

        def ffn_bwd(dx, dxb, ht, gu, at, w_in2, w_out2, x_in, g_norm, tag, hook=None):
            dgu = ffn_dact(dxb, w_out2, gu, f"{tag}_dact_{l}")
            deps = hook(dgu) if hook else ()
            d_w_out = mm_nn(at, dxb[None], f"{tag}_dwout_{l}", bf16, FFN_RES, deps=deps)[0]
            d_w_in = mm_nn(ht, dgu, f"{tag}_dwin_{l}", bf16)
            dh = mm_nt(dgu, w_in2, f"{tag}_dh_{l}")
            dx, dxb, dg = rmsnorm_bwd(dh, x_in, g_norm, dx, f"{tag}_dnorm_{l}")
            return dx, dxb, d_w_in, d_w_out, dg

        def hook(dgu):
            nonlocal chips_pending
            if pair_pending is None:
                return ()
            chips_pending = pair_to_chips(pair_pending, dgu, l + 1)
            return (chips_pending[4],)

        dx, dxb, d_f2_in, d_f2_out, d_norm_ffn2 = ffn_bwd(
            dx, dxb, sv["h3t"], sv["gu2"], sv["a2t"], wts["f2_in"], wts["f2_out"], sv["x_c"], row(norm_ffn2[l]), "ffn2",
            hook)

        d_w_out = mm_nn(sv["catt"], dxb[None], f"mix_dwout_{l}", bf16)[0]
        dcat = mm_nt(dxb[None], wts["w_out"][None], f"mix_dcat_{l}")
        p = sv["p"]
        dq, dkv, d_sinks = attn_bwd(p, dcat, rope_c, rope_s1, rope_s2, attn_sinks[l], f"attn_bwd_{l}")
        da_conv, d_taps, d_conv_b, d_conv_g, d_conv_bb = conv_bwd(
            p, sv["conv_y"], dcat, wts["taps"], row(conv_ln_g[l]), row(conv_ln_b[l]), f"conv_bwd_{l}")
        da_sgu, d_sgu_w, d_sgu_bias, d_sgu_g, d_sgu_bb = sgu_bwd(
            p, dcat, row(sgu_ln_g[l]), row(sgu_ln_b[l]), sgu_w[l], jnp.swapaxes(sgu_w[l], 1, 2), sv["sgu_bias"],
            f"sgu_bwd_{l}")
        dp = jnp.concatenate([dq, dkv, da_conv, da_sgu], axis=1)
        d_w_in = mm_nn(sv["h2t"], dp[None], f"mix_dwin_{l}", bf16)[0]
        dh = mm_nt(dp[None], wts["w_in"][None], f"mix_dh_{l}")
        dx, dxb, d_norm_mix = rmsnorm_bwd(dh, sv["x_b"], row(norm_mix[l]), dx, f"mix_dnorm_{l}")

        dx, dxb, d_f1_in, d_f1_out, d_norm_ffn1 = ffn_bwd(
            dx, dxb, sv["h1t"], sv["gu1"], sv["a1t"], wts["f1_in"], wts["f1_out"], sv["x_a"], row(norm_ffn1[l]), "ffn1")

        small[l] = dict(norm_ffn1=d_norm_ffn1[0], norm_mix=d_norm_mix[0], conv_dw_w=d_taps[:CONV_W],
                        conv_dw_b=d_conv_b[0], conv_ln_g=d_conv_g[0], conv_ln_b=d_conv_bb[0], sgu_ln_g=d_sgu_g[0],
                        sgu_ln_b=d_sgu_bb[0], sgu_w=d_sgu_w, sgu_b=d_sgu_bias[:, :SGU_HEADS].T,
                        attn_sinks=d_sinks[0, :N_Q_HEADS], norm_ffn2=d_norm_ffn2[0])

        def in_shards(d2):
            return _cols_to_shards(d2.transpose(1, 0, 2).reshape(D, 2 * F))

        if chips_pending is not None:
            chip_sums[l + 1], chip_lands[l + 1] = split_wait(chips_pending, dx, _routes_chips, f"chips_wait_{l + 1}")
            chips_pending = None
        grads = [in_shards(d_f1_in), d_f1_out.reshape(N_DEV, F // N_DEV, D), _cols_to_shards(d_w_in),
                 d_w_out.reshape(N_DEV, D // N_DEV, D), in_shards(d_f2_in), d_f2_out.reshape(N_DEV, F // N_DEV, D)]
        pair_pending = split_start(grads, [lax.empty((N_DEV // 2,) + g.shape[1:], g.dtype) for g in grads],
                                   _routes_pair, f"pair_start_{l}")

    chips_pending = pair_to_chips(pair_pending, pair_pending[4], 0)
    chip_sums[0], chip_lands[0] = split_wait(chips_pending, chips_pending[4], _routes_chips, "chips_wait_0")
    grad_x = dx[None]

    big = {}
    for idx, (name, w, m, v) in enumerate([
            ("ffn1_w_in", ffn1_w_in, m_ffn1_w_in, v_ffn1_w_in), ("ffn1_w_out", ffn1_w_out, m_ffn1_w_out, v_ffn1_w_out),
            ("w_in", w_in, m_w_in, v_w_in), ("w_out", w_out, m_w_out, v_w_out),
            ("ffn2_w_in", ffn2_w_in, m_ffn2_w_in, v_ffn2_w_in), ("ffn2_w_out", ffn2_w_out, m_ffn2_w_out, v_ffn2_w_out)]):
        big[name] = adamw([chip_lands[l][idx] for l in range(L)], [chip_sums[l][idx] for l in range(L)], chip, w, m, v,
                          f"adamw_{name}")

    small_names = ["norm_ffn1", "norm_mix", "conv_dw_w", "conv_dw_b", "conv_ln_g", "conv_ln_b", "sgu_ln_g", "sgu_ln_b",
                   "sgu_w", "sgu_b", "attn_sinks", "norm_ffn2"]
    local = [jnp.stack([small[l][n] for l in range(L)]) for n in small_names] + [d_final_norm[0], loss.reshape(1)]
    shapes = [a.shape for a in local]
    total = _unpack(allreduce_small(_pack(local), "sum_small_grads"), shapes)
    g_small = dict(zip(small_names + ["final_norm", "loss"], total))
    g_small["conv_dw_w"] = lax.dynamic_slice_in_dim(g_small["conv_dw_w"], me * cw, cw, axis=2)
    given = dict(norm_ffn1=(norm_ffn1, m_norm_ffn1, v_norm_ffn1), norm_mix=(norm_mix, m_norm_mix, v_norm_mix),
                 conv_dw_w=(conv_dw_w, m_conv_dw_w, v_conv_dw_w), conv_dw_b=(conv_dw_b, m_conv_dw_b, v_conv_dw_b),
                 conv_ln_g=(conv_ln_g, m_conv_ln_g, v_conv_ln_g), conv_ln_b=(conv_ln_b, m_conv_ln_b, v_conv_ln_b),
                 sgu_ln_g=(sgu_ln_g, m_sgu_ln_g, v_sgu_ln_g), sgu_ln_b=(sgu_ln_b, m_sgu_ln_b, v_sgu_ln_b),
                 sgu_w=(sgu_w, m_sgu_w, v_sgu_w), sgu_b=(sgu_b, m_sgu_b, v_sgu_b),
                 attn_sinks=(attn_sinks, m_attn_sinks, v_attn_sinks), norm_ffn2=(norm_ffn2, m_norm_ffn2, v_norm_ffn2),
                 final_norm=(final_norm, m_final_norm, v_final_norm))
    upd_names = small_names + ["final_norm"]
    upd_shapes = [given[n][0].shape for n in upd_names]
    packed = [_pack([g_small[n] for n in upd_names])[None]] + [_pack([given[n][k] for n in upd_names])[None] for k in range(3)]
    res = adamw([packed[0]], None, no_chip, packed[1], packed[2], packed[3], "adamw_small")
    upd = [dict(zip(upd_names, _unpack(r[0], upd_shapes))) for r in res]

    order = ["norm_ffn1", "ffn1_w_in", "ffn1_w_out", "norm_mix", "w_in", "conv_dw_w", "conv_dw_b", "conv_ln_g",
             "conv_ln_b", "sgu_ln_g", "sgu_ln_b", "sgu_w", "sgu_b", "attn_sinks", "w_out", "norm_ffn2", "ffn2_w_in",
             "ffn2_w_out", "final_norm"]
    outs = [g_small["loss"].reshape(()), grad_x]
    for k in range(4):
        outs += [big[n][k] if n in big else upd[k][n] for n in order]
    return tuple(outs)
```

```python
import functools
import math

import jax
import jax.numpy as jnp
from jax import lax
from jax.experimental import pallas as pl
from jax.experimental.pallas import tpu as pltpu

f32 = jnp.float32
bf16 = jnp.bfloat16

N_DEV = 8
HEAD_DIM = 64
N_Q_HEADS = 16
N_KV_HEADS = 4
GQ = N_Q_HEADS // N_KV_HEADS
BLK = 128
ROT_DIM = 16
ROPE_THETA = 500000.0
CONV_W = 31
CONV_CH = 512
SGU_CH = 512
SGU_HEADS = 8
Q_END = N_Q_HEADS * HEAD_DIM
KV_W = 2 * N_KV_HEADS * HEAD_DIM
V_END = Q_END + KV_W
IN_COLS = V_END + 2 * CONV_CH + 2 * SGU_CH
HALO = 32
NORM_EPS = 1e-5
FFN_RES = 0.5
ADAM_LR, ADAM_B1, ADAM_B2, ADAM_EPS, ADAM_WD, ADAM_STEP = 0.001, 0.9, 0.999, 1e-08, 0.01, 10
LANE = 128
VMEM_LIMIT = 56 * 2 ** 20
MESH = pl.DeviceIdType.MESH

NN = (((1,), (0,)), ((), ()))
NT = (((1,), (1,)), ((), ()))
TN = (((0,), (0,)), ((), ()))


def _tile(dim, pref):
    t = min(pref, dim)
    while dim % t:
        t //= 2
    return t


def _params(*sem):
    return pltpu.CompilerParams(dimension_semantics=sem, vmem_limit_bytes=VMEM_LIMIT)


def _sig(x):
    return 1.0 / (1.0 + jnp.exp(-x))


def _dot(a, b, dims=NN):
    return lax.dot_general(a, b, dims, preferred_element_type=f32)


def _rowsum8(x):
    return x.reshape(x.shape[0] // 8, 8, x.shape[1]).sum(axis=0)


def rmsnorm_fwd(x, g, name, deps=()):
    S, D = x.shape
    tm = _tile(S, 512)

    def body(x_ref, g_ref, *rest):
        o_ref, ot_ref = rest[-2:]
        xv = x_ref[...]
        r = lax.rsqrt(jnp.mean(xv * xv, axis=-1, keepdims=True) + NORM_EPS)
        hb = (xv * r * g_ref[...]).astype(bf16)
        o_ref[...] = hb
        ot_ref[...] = hb.T

    return pl.pallas_call(
        body, name=name, grid=(S // tm,),
        in_specs=[pl.BlockSpec((tm, D), lambda i: (i, 0)), pl.BlockSpec((1, D), lambda i: (0, 0))] + [ANY] * len(deps),
        out_specs=[pl.BlockSpec((tm, D), lambda i: (i, 0)), pl.BlockSpec((D, tm), lambda i: (0, i))],
        out_shape=[jax.ShapeDtypeStruct((S, D), bf16), jax.ShapeDtypeStruct((D, S), bf16)],
        compiler_params=_params("parallel"),
    )(x, g, *deps)


def rmsnorm_bwd(dh, x, g, dres, name, deps=()):
    S, D = x.shape
    tm = _tile(S, 256)
    n = S // tm

    def body(dh_ref, x_ref, g_ref, dres_ref, *rest):
        dx_ref, dxb_ref, dg_ref, acc = rest[-4:]
        i = pl.program_id(0)

        @pl.when(i == 0)
        def _():
            acc[...] = jnp.zeros_like(acc)

        xv = x_ref[...]
        r = lax.rsqrt(jnp.mean(xv * xv, axis=-1, keepdims=True) + NORM_EPS)
        xh = xv * r
        dy = dh_ref[...]
        gy = dy * g_ref[...]
        dx = dres_ref[...] + r * (gy - xh * jnp.mean(gy * xh, axis=-1, keepdims=True))
        dx_ref[...] = dx
        dxb_ref[...] = dx.astype(bf16)
        acc[...] += _rowsum8(dy * xh)

        @pl.when(i == n - 1)
        def _():
            dg_ref[...] = jnp.sum(acc[...], axis=0, keepdims=True)

    row = pl.BlockSpec((tm, D), lambda i: (i, 0))
    vec = pl.BlockSpec((1, D), lambda i: (0, 0))
    return pl.pallas_call(
        body, name=name, grid=(n,),
        in_specs=[row, row, vec, row] + [ANY] * len(deps),
        out_specs=[row, row, vec],
        out_shape=[jax.ShapeDtypeStruct((S, D), f32), jax.ShapeDtypeStruct((S, D), bf16),
                   jax.ShapeDtypeStruct((1, D), f32)],
        scratch_shapes=[pltpu.VMEM((8, D), f32)],
        compiler_params=_params("arbitrary"),
    )(dh, x, g, dres, *deps)


def final_loss(x, g, tgt, name):
    S, D = x.shape
    tm = _tile(S, 256)
    n = S // tm

    def body(x_ref, g_ref, t_ref, dx_ref, dxb_ref, dg_ref, loss_ref, acc):
        i = pl.program_id(0)

        @pl.when(i == 0)
        def _():
            acc[...] = jnp.zeros_like(acc)
            loss_ref[...] = jnp.zeros_like(loss_ref)

        xv = x_ref[...]
        gv = g_ref[...]
        r = lax.rsqrt(jnp.mean(xv * xv, axis=-1, keepdims=True) + NORM_EPS)
        xh = xv * r
        diff = xh * gv - t_ref[...]
        tok = jnp.mean(diff * diff, axis=-1, keepdims=True)
        loss_ref[...] += 0.5 * jnp.sum(tok, axis=0, keepdims=True)
        dy = diff / D
        gy = dy * gv
        dx = r * (gy - xh * jnp.mean(gy * xh, axis=-1, keepdims=True))
        dx_ref[...] = dx
        dxb_ref[...] = dx.astype(bf16)
        acc[...] += _rowsum8(dy * xh)

        @pl.when(i == n - 1)
        def _():
            dg_ref[...] = jnp.sum(acc[...], axis=0, keepdims=True)

    row = pl.BlockSpec((tm, D), lambda i: (i, 0))
    vec = pl.BlockSpec((1, D), lambda i: (0, 0))
    return pl.pallas_call(
        body, name=name, grid=(n,),
        in_specs=[row, vec, row],
        out_specs=[row, row, vec, pl.BlockSpec((1, 1), lambda i: (0, 0))],
        out_shape=[jax.ShapeDtypeStruct((S, D), f32), jax.ShapeDtypeStruct((S, D), bf16),
                   jax.ShapeDtypeStruct((1, D), f32), jax.ShapeDtypeStruct((1, 1), f32)],
        scratch_shapes=[pltpu.VMEM((8, D), f32)],
        compiler_params=_params("arbitrary"),
    )(x, g, tgt)


def ffn_in(h, w2, name):
    S, D = h.shape
    F = w2.shape[2]
    tm, tn = _tile(S, 512), _tile(F, 512)

    def body(h_ref, w_ref, gu_ref, a_ref, at_ref):
        hv = h_ref[...]
        g = _dot(hv, w_ref[0])
        u = _dot(hv, w_ref[1])
        gu_ref[0] = g.astype(bf16)
        gu_ref[1] = u.astype(bf16)
        a = (g * _sig(g) * u).astype(bf16)
        a_ref[...] = a
        at_ref[...] = a.T

    return pl.pallas_call(
        body, name=name, grid=(S // tm, F // tn),
        in_specs=[pl.BlockSpec((tm, D), lambda i, j: (i, 0)), pl.BlockSpec((2, D, tn), lambda i, j: (0, 0, j))],
        out_specs=[pl.BlockSpec((2, tm, tn), lambda i, j: (0, i, j)), pl.BlockSpec((tm, tn), lambda i, j: (i, j)),
                   pl.BlockSpec((tn, tm), lambda i, j: (j, i))],
        out_shape=[jax.ShapeDtypeStruct((2, S, F), bf16), jax.ShapeDtypeStruct((S, F), bf16),
                   jax.ShapeDtypeStruct((F, S), bf16)],
        compiler_params=_params("parallel", "parallel"),
    )(h, w2)


def mm_res(a, w, x, scale, name):
    S, K = a.shape
    N = w.shape[1]
    tm, tn = _tile(S, 512), _tile(N, 512)

    def body(a_ref, w_ref, x_ref, o_ref):
        o_ref[...] = x_ref[...] + scale * _dot(a_ref[...], w_ref[...])

    return pl.pallas_call(
        body, name=name, grid=(S // tm, N // tn),
        in_specs=[pl.BlockSpec((tm, K), lambda i, j: (i, 0)), pl.BlockSpec((K, tn), lambda i, j: (0, j)),
                  pl.BlockSpec((tm, tn), lambda i, j: (i, j))],
        out_specs=pl.BlockSpec((tm, tn), lambda i, j: (i, j)),
        out_shape=jax.ShapeDtypeStruct((S, N), f32),
        compiler_params=_params("parallel", "parallel"),
    )(a, w, x)


def mm_nn(a, b, name, out_dtype=f32, scale=1.0, deps=()):
    M, K = a.shape
    G, _, N = b.shape
    tm, tn = _tile(M, 512), _tile(N, 512)

    def body(a_ref, b_ref, *rest):
        acc = _dot(a_ref[...], b_ref[...])
        rest[-1][...] = (acc if scale == 1.0 else scale * acc).astype(out_dtype)

    return pl.pallas_call(
        body, name=name, grid=(G, M // tm, N // tn),
        in_specs=[pl.BlockSpec((tm, K), lambda g, i, j: (i, 0)),
                  pl.BlockSpec((None, K, tn), lambda g, i, j: (g, 0, j))] + [ANY] * len(deps),
        out_specs=pl.BlockSpec((None, tm, tn), lambda g, i, j: (g, i, j)),
        out_shape=jax.ShapeDtypeStruct((G, M, N), out_dtype),
        compiler_params=_params("parallel", "parallel", "parallel"),
    )(a, b, *deps)


def mm_nt(a, w, name, deps=()):
    G, S, K = a.shape
    N = w.shape[1]
    tm, tn = _tile(S, 512), _tile(N, 512)

    def body(a_ref, w_ref, *rest):
        o_ref = rest[-1]
        part = _dot(a_ref[...], w_ref[...], NT)
        if G == 1:
            o_ref[...] = part
        else:
            g = pl.program_id(2)

            @pl.when(g == 0)
            def _():
                o_ref[...] = part

            @pl.when(g > 0)
            def _():
                o_ref[...] += part

    return pl.pallas_call(
        body, name=name, grid=(S // tm, N // tn, G),
        in_specs=[pl.BlockSpec((None, tm, K), lambda i, j, g: (g, i, 0)),
                  pl.BlockSpec((None, tn, K), lambda i, j, g: (g, j, 0))] + [ANY] * len(deps),
        out_specs=pl.BlockSpec((tm, tn), lambda i, j, g: (i, j)),
        out_shape=jax.ShapeDtypeStruct((S, N), f32),
        compiler_params=_params("parallel", "parallel", "arbitrary"),
    )(a, w, *deps)


def ffn_dact(dx, wout, gu, name):
    S, D = dx.shape
    F = wout.shape[0]
    tm, tn = _tile(S, 512), _tile(F, 512)

    def body(dx_ref, w_ref, gu_ref, o_ref):
        da = FFN_RES * _dot(dx_ref[...], w_ref[...], NT)
        g = gu_ref[0].astype(f32)
        u = gu_ref[1].astype(f32)
        sg = _sig(g)
        o_ref[0] = (da * u * (sg * (1.0 + g * (1.0 - sg)))).astype(bf16)
        o_ref[1] = (da * (g * sg)).astype(bf16)

    return pl.pallas_call(
        body, name=name, grid=(S // tm, F // tn),
        in_specs=[pl.BlockSpec((tm, D), lambda i, j: (i, 0)), pl.BlockSpec((tn, D), lambda i, j: (j, 0)),
                  pl.BlockSpec((2, tm, tn), lambda i, j: (0, i, j))],
        out_specs=pl.BlockSpec((2, tm, tn), lambda i, j: (0, i, j)),
        out_shape=jax.ShapeDtypeStruct((2, S, F), bf16),
        compiler_params=_params("parallel", "parallel"),
    )(dx, wout, gu)


def _rope(t, c, s1, s2):
    w = t.shape[1]
    return t * c + pltpu.roll(t, 8, 1) * s1 + pltpu.roll(t, w - 8, 1) * s2


def _rope_t(d, c, s1, s2):
    w = d.shape[1]
    return d * c + pltpu.roll(d * s1, w - 8, 1) + pltpu.roll(d * s2, 8, 1)


def _attn_mask(n):
    qi = lax.broadcasted_iota(jnp.int32, (BLK, 2 * BLK), 0)
    kj = lax.broadcasted_iota(jnp.int32, (BLK, 2 * BLK), 1)
    dist = qi + BLK - kj
    return (dist >= 0) & (dist < BLK) & ((kj >= BLK) | (n > 0))


def _softmax_sink(s, valid, sk):
    s = jnp.where(valid, s, -1e30)
    m = jnp.maximum(jnp.max(s, axis=-1, keepdims=True), sk)
    e = jnp.exp(s - m)
    es = jnp.exp(sk - m)
    inv = 1.0 / (jnp.sum(e, axis=-1, keepdims=True) + es)
    return e * inv, es * inv


def attn_fwd(p, rope_c, rope_s1, rope_s2, sinks, name):
    S = p.shape[0]
    nb = S // BLK
    kvb = Q_END // KV_W

    def body(sink_ref, q_ref, kvc_ref, kvp_ref, cc_ref, s1c_ref, s2c_ref, cp_ref, s1p_ref, s2p_ref, o_ref):
        n = pl.program_id(0)
        cc, s1c, s2c = cc_ref[...], s1c_ref[...], s2c_ref[...]
        cp, s1p, s2p = cp_ref[...], s1p_ref[...], s2p_ref[...]
        q = _rope(q_ref[...], jnp.tile(cc, (1, 8)), jnp.tile(s1c, (1, 8)), jnp.tile(s2c, (1, 8)))
        kc = _rope(kvc_ref[:, :256], jnp.tile(cc, (1, 2)), jnp.tile(s1c, (1, 2)), jnp.tile(s2c, (1, 2)))
        kp = _rope(kvp_ref[:, :256], jnp.tile(cp, (1, 2)), jnp.tile(s1p, (1, 2)), jnp.tile(s2p, (1, 2)))
        k = jnp.concatenate([kp, kc], axis=0).astype(bf16)
        v = jnp.concatenate([kvp_ref[:, 256:], kvc_ref[:, 256:]], axis=0).astype(bf16)
        q = q.astype(bf16)
        valid = _attn_mask(n)
        for h in range(N_KV_HEADS):
            kh = k[:, h * HEAD_DIM:(h + 1) * HEAD_DIM]
            vh = v[:, h * HEAD_DIM:(h + 1) * HEAD_DIM]
            for g in range(GQ):
                hq = h * GQ + g
                qh = q[:, hq * HEAD_DIM:(hq + 1) * HEAD_DIM]
                s = _dot(qh, kh, NT) * (HEAD_DIM ** -0.5)
                pr, _ = _softmax_sink(s, valid, sink_ref[hq])
                o = _dot(pr.astype(bf16), vh)
                o_ref[:, hq * HEAD_DIM:(hq + 1) * HEAD_DIM] = o.astype(bf16)

    tab_c = pl.BlockSpec((BLK, LANE), lambda n: (n, 0))
    tab_p = pl.BlockSpec((BLK, LANE), lambda n: (jnp.maximum(n - 1, 0), 0))
    return pl.pallas_call(
        body, name=name, grid=(nb,),
        in_specs=[pl.BlockSpec(memory_space=pltpu.SMEM),
                  pl.BlockSpec((BLK, Q_END), lambda n: (n, 0)),
                  pl.BlockSpec((BLK, KV_W), lambda n: (n, kvb)),
                  pl.BlockSpec((BLK, KV_W), lambda n: (jnp.maximum(n - 1, 0), kvb)),
                  tab_c, tab_c, tab_c, tab_p, tab_p, tab_p],
        out_specs=pl.BlockSpec((BLK, Q_END), lambda n: (n, 0)),
        out_shape=jax.ShapeDtypeStruct((S, Q_END), bf16),
        compiler_params=_params("parallel"),
    )(sinks, p, p, p, rope_c, rope_s1, rope_s2, rope_c, rope_s1, rope_s2)


def attn_bwd(p, dcat, rope_c, rope_s1, rope_s2, sinks, name):
    S = p.shape[0]
    nb = S // BLK
    kvb = Q_END // KV_W

    def body(sink_ref, q_ref, kvc_ref, kvp_ref, do_ref, cc_ref, s1c_ref, s2c_ref, cp_ref, s1p_ref, s2p_ref,
             dq_ref, dkv_ref, dsink_ref, carry, dq_scr, dkv_scr):
        n = pl.program_id(0)

        @pl.when(n == 0)
        def _():
            carry[...] = jnp.zeros_like(carry)
            dsink_ref[...] = jnp.zeros_like(dsink_ref)

        cp, s1p, s2p = cp_ref[...], s1p_ref[...], s2p_ref[...]
        cp2, s1p2, s2p2 = jnp.tile(cp, (1, 2)), jnp.tile(s1p, (1, 2)), jnp.tile(s2p, (1, 2))

        @pl.when(n < nb)
        def _():
            cc, s1c, s2c = cc_ref[...], s1c_ref[...], s2c_ref[...]
            cc8, s1c8, s2c8 = jnp.tile(cc, (1, 8)), jnp.tile(s1c, (1, 8)), jnp.tile(s2c, (1, 8))
            q = _rope(q_ref[...], cc8, s1c8, s2c8).astype(bf16)
            kc = _rope(kvc_ref[:, :256], jnp.tile(cc, (1, 2)), jnp.tile(s1c, (1, 2)), jnp.tile(s2c, (1, 2)))
            kp = _rope(kvp_ref[:, :256], cp2, s1p2, s2p2)
            k = jnp.concatenate([kp, kc], axis=0).astype(bf16)
            v = jnp.concatenate([kvp_ref[:, 256:], kvc_ref[:, 256:]], axis=0).astype(bf16)
            do = do_ref[...].astype(bf16)
            valid = _attn_mask(n)
            lane = lax.broadcasted_iota(jnp.int32, (1, LANE), 1)
            dsink = jnp.zeros((1, LANE), f32)
            for h in range(N_KV_HEADS):
                kh = k[:, h * HEAD_DIM:(h + 1) * HEAD_DIM]
                vh = v[:, h * HEAD_DIM:(h + 1) * HEAD_DIM]
                dkh = jnp.zeros((2 * BLK, HEAD_DIM), f32)
                dvh = jnp.zeros((2 * BLK, HEAD_DIM), f32)
                for g in range(GQ):
                    hq = h * GQ + g
                    qh = q[:, hq * HEAD_DIM:(hq + 1) * HEAD_DIM]
                    doh = do[:, hq * HEAD_DIM:(hq + 1) * HEAD_DIM]
                    s = _dot(qh, kh, NT) * (HEAD_DIM ** -0.5)
                    pr, ps = _softmax_sink(s, valid, sink_ref[hq])
                    dpr = _dot(doh, vh, NT)
                    dvh = dvh + _dot(pr.astype(bf16), doh, TN)
                    row = jnp.sum(pr * dpr, axis=-1, keepdims=True)
                    ds = (pr * (dpr - row) * (HEAD_DIM ** -0.5)).astype(bf16)
                    dsink = dsink + jnp.where(lane == hq, -jnp.sum(ps * row, axis=0, keepdims=True), 0.0)
                    dq_scr[:, hq * HEAD_DIM:(hq + 1) * HEAD_DIM] = _dot(ds, kh)
                    dkh = dkh + _dot(ds, qh, TN)
                dkv_scr[:, h * HEAD_DIM:(h + 1) * HEAD_DIM] = dkh
                dkv_scr[:, 256 + h * HEAD_DIM:256 + (h + 1) * HEAD_DIM] = dvh
            dsink_ref[...] += dsink
            dq_ref[...] = _rope_t(dq_scr[...], cc8, s1c8, s2c8).astype(bf16)

        prev = carry[...]

        @pl.when(n < nb)
        def _():
            dkv_scr[pl.ds(0, BLK), :] = dkv_scr[pl.ds(0, BLK), :] + prev

        @pl.when(n == nb)
        def _():
            dkv_scr[pl.ds(0, BLK), :] = prev

        done = dkv_scr[pl.ds(0, BLK), :]
        dkv_ref[:, :256] = _rope_t(done[:, :256], cp2, s1p2, s2p2).astype(bf16)
        dkv_ref[:, 256:] = done[:, 256:].astype(bf16)

        @pl.when(n < nb)
        def _():
            carry[...] = dkv_scr[pl.ds(BLK, BLK), :]

    cur = lambda n: jnp.minimum(n, nb - 1)
    prv = lambda n: jnp.maximum(n - 1, 0)
    tab_c = pl.BlockSpec((BLK, LANE), lambda n: (cur(n), 0))
    tab_p = pl.BlockSpec((BLK, LANE), lambda n: (prv(n), 0))
    return pl.pallas_call(
        body, name=name, grid=(nb + 1,),
        in_specs=[pl.BlockSpec(memory_space=pltpu.SMEM),
                  pl.BlockSpec((BLK, Q_END), lambda n: (cur(n), 0)),
                  pl.BlockSpec((BLK, KV_W), lambda n: (cur(n), kvb)),
                  pl.BlockSpec((BLK, KV_W), lambda n: (prv(n), kvb)),
                  pl.BlockSpec((BLK, Q_END), lambda n: (cur(n), 0)),
                  tab_c, tab_c, tab_c, tab_p, tab_p, tab_p],
        out_specs=[pl.BlockSpec((BLK, Q_END), lambda n: (cur(n), 0)),
                   pl.BlockSpec((BLK, KV_W), lambda n: (prv(n), 0)),
                   pl.BlockSpec((1, LANE), lambda n: (0, 0))],
        out_shape=[jax.ShapeDtypeStruct((S, Q_END), bf16), jax.ShapeDtypeStruct((S, KV_W), bf16),
                   jax.ShapeDtypeStruct((1, LANE), f32)],
        scratch_shapes=[pltpu.VMEM((BLK, KV_W), f32), pltpu.VMEM((BLK, Q_END), f32), pltpu.VMEM((2 * BLK, KV_W), f32)],
        compiler_params=_params("arbitrary"),
    )(sinks, p, p, p, dcat, rope_c, rope_s1, rope_s2, rope_c, rope_s1, rope_s2)


A1_BLK = V_END // CONV_CH
A2_BLK = A1_BLK + 1


def _ln_stats(y):
    mu = jnp.mean(y, axis=-1, keepdims=True)
    xc = y - mu
    rstd = lax.rsqrt(jnp.mean(xc * xc, axis=-1, keepdims=True) + NORM_EPS)
    return xc * rstd, rstd


def conv_fwd(p, w, b, lng, lnb, name):
    S = p.shape[0]
    T = _tile(S, 256)
    r = T // HALO

    def body(a1_ref, a2_ref, h1_ref, h2_ref, w_ref, b_ref, g_ref, bb_ref, o_ref, y_ref, scr):
        i = pl.program_id(0)
        halo = h1_ref[...] * _sig(h2_ref[...])
        scr[pl.ds(0, HALO), :] = jnp.where(i > 0, halo, 0.0)
        scr[pl.ds(HALO, T), :] = a1_ref[...] * _sig(a2_ref[...])
        acc = jnp.zeros((T, CONV_CH), f32) + b_ref[...]
        for j in range(CONV_W):
            acc = acc + scr[pl.ds(HALO - (CONV_W - 1) + j, T), :] * w_ref[j:j + 1, :]
        y_ref[...] = acc
        yh, _ = _ln_stats(acc)
        z = yh * g_ref[...] + bb_ref[...]
        o_ref[...] = (z * _sig(z)).astype(bf16)

    vec = pl.BlockSpec((1, CONV_CH), lambda i: (0, 0))
    halo_map = lambda i: jnp.maximum(i * r - 1, 0)
    return pl.pallas_call(
        body, name=name, grid=(S // T,),
        in_specs=[pl.BlockSpec((T, CONV_CH), lambda i: (i, A1_BLK)), pl.BlockSpec((T, CONV_CH), lambda i: (i, A2_BLK)),
                  pl.BlockSpec((HALO, CONV_CH), lambda i: (halo_map(i), A1_BLK)),
                  pl.BlockSpec((HALO, CONV_CH), lambda i: (halo_map(i), A2_BLK)),
                  pl.BlockSpec((HALO, CONV_CH), lambda i: (0, 0)), vec, vec, vec],
        out_specs=[pl.BlockSpec((T, CONV_CH), lambda i: (i, 0)), pl.BlockSpec((T, CONV_CH), lambda i: (i, 0))],
        out_shape=[jax.ShapeDtypeStruct((S, CONV_CH), bf16), jax.ShapeDtypeStruct((S, CONV_CH), f32)],
        scratch_shapes=[pltpu.VMEM((T + HALO, CONV_CH), f32)],
        compiler_params=_params("parallel"),
    )(p, p, p, p, w, b, lng, lnb)


def conv_bwd(p, y, dcat, w, lng, lnb, name):
    S = p.shape[0]
    T = _tile(S, 256)
    n = S // T
    r = T // HALO
    dcb = Q_END // CONV_CH

    def body(a1_ref, a2_ref, h1_ref, h2_ref, y_ref, yn_ref, do_ref, don_ref, w_ref, g_ref, bb_ref,
             da_ref, dw_ref, db_ref, dg_ref, dbb_ref, scr_h, scr_dy, acc_b, acc_g, acc_bb):
        i = pl.program_id(0)

        @pl.when(i == 0)
        def _():
            dw_ref[...] = jnp.zeros_like(dw_ref)
            acc_b[...] = jnp.zeros_like(acc_b)
            acc_g[...] = jnp.zeros_like(acc_g)
            acc_bb[...] = jnp.zeros_like(acc_bb)

        gv, bv = g_ref[...], bb_ref[...]

        def ln_silu_bwd(yv, dout):
            yh, rstd = _ln_stats(yv)
            z = yh * gv + bv
            sg = _sig(z)
            dz = dout * (sg * (1.0 + z * (1.0 - sg)))
            gz = dz * gv
            dy = rstd * (gz - jnp.mean(gz, axis=-1, keepdims=True) - yh * jnp.mean(gz * yh, axis=-1, keepdims=True))
            return dy, dz, yh

        dy, dz, yh = ln_silu_bwd(y_ref[...], do_ref[...])
        dyn, _, _ = ln_silu_bwd(yn_ref[...], don_ref[...])
        acc_g[...] += _rowsum8(dz * yh)
        acc_bb[...] += _rowsum8(dz)
        acc_b[...] += _rowsum8(dy)
        scr_dy[pl.ds(0, T), :] = dy
        scr_dy[pl.ds(T, HALO), :] = jnp.where(i < n - 1, dyn, 0.0)
        a1, a2 = a1_ref[...], a2_ref[...]
        sg2 = _sig(a2)
        halo = h1_ref[...] * _sig(h2_ref[...])
        scr_h[pl.ds(0, HALO), :] = jnp.where(i > 0, halo, 0.0)
        scr_h[pl.ds(HALO, T), :] = a1 * sg2
        dh = jnp.zeros((T, CONV_CH), f32)
        for j in range(CONV_W):
            dh = dh + scr_dy[pl.ds(CONV_W - 1 - j, T), :] * w_ref[j:j + 1, :]
            dw_ref[j:j + 1, :] += jnp.sum(dy * scr_h[pl.ds(HALO - (CONV_W - 1) + j, T), :], axis=0, keepdims=True)
        da_ref[:, :CONV_CH] = (dh * sg2).astype(bf16)
        da_ref[:, CONV_CH:] = (dh * a1 * sg2 * (1.0 - sg2)).astype(bf16)

        @pl.when(i == n - 1)
        def _():
            db_ref[...] = jnp.sum(acc_b[...], axis=0, keepdims=True)
            dg_ref[...] = jnp.sum(acc_g[...], axis=0, keepdims=True)
            dbb_ref[...] = jnp.sum(acc_bb[...], axis=0, keepdims=True)

    vec = pl.BlockSpec((1, CONV_CH), lambda i: (0, 0))
    tap = pl.BlockSpec((HALO, CONV_CH), lambda i: (0, 0))
    prev_map = lambda i: jnp.maximum(i * r - 1, 0)
    next_map = lambda i: jnp.minimum((i + 1) * r, S // HALO - 1)
    return pl.pallas_call(
        body, name=name, grid=(n,),
        in_specs=[pl.BlockSpec((T, CONV_CH), lambda i: (i, A1_BLK)), pl.BlockSpec((T, CONV_CH), lambda i: (i, A2_BLK)),
                  pl.BlockSpec((HALO, CONV_CH), lambda i: (prev_map(i), A1_BLK)),
                  pl.BlockSpec((HALO, CONV_CH), lambda i: (prev_map(i), A2_BLK)),
                  pl.BlockSpec((T, CONV_CH), lambda i: (i, 0)),
                  pl.BlockSpec((HALO, CONV_CH), lambda i: (next_map(i), 0)),
                  pl.BlockSpec((T, CONV_CH), lambda i: (i, dcb)),
                  pl.BlockSpec((HALO, CONV_CH), lambda i: (next_map(i), dcb)),
                  tap, vec, vec],
        out_specs=[pl.BlockSpec((T, 2 * CONV_CH), lambda i: (i, 0)), tap, vec, vec, vec],
        out_shape=[jax.ShapeDtypeStruct((S, 2 * CONV_CH), bf16), jax.ShapeDtypeStruct((HALO, CONV_CH), f32),
                   jax.ShapeDtypeStruct((1, CONV_CH), f32), jax.ShapeDtypeStruct((1, CONV_CH), f32),
                   jax.ShapeDtypeStruct((1, CONV_CH), f32)],
        scratch_shapes=[pltpu.VMEM((T + HALO, CONV_CH), f32), pltpu.VMEM((T + HALO, CONV_CH), f32),
                        pltpu.VMEM((8, CONV_CH), f32), pltpu.VMEM((8, CONV_CH), f32), pltpu.VMEM((8, CONV_CH), f32)],
        compiler_params=_params("arbitrary"),
    )(p, p, p, p, y, y, dcat, dcat, w, lng, lnb)


U_BLK = (V_END + 2 * CONV_CH) // SGU_CH
SV_BLK = U_BLK + 1


def _tril(w, transposed=False):
    row = lax.broadcasted_iota(jnp.int32, (BLK, BLK), 0)
    col = lax.broadcasted_iota(jnp.int32, (BLK, BLK), 1)
    keep = (col >= row) if transposed else (row >= col)
    return jnp.where(keep, w, 0.0)


def sgu_fwd(p, lng, lnb, w, bias, name):
    S = p.shape[0]
    T = _tile(S, 256)

    def body(u_ref, v_ref, g_ref, bb_ref, w_ref, bias_ref, o_ref):
        yh, _ = _ln_stats(v_ref[...])
        v = (yh * g_ref[...] + bb_ref[...]).astype(bf16)
        low = lax.broadcasted_iota(jnp.int32, (BLK, LANE), 1) < HEAD_DIM
        for pr in range(SGU_HEADS // 2):
            lanes = pl.ds(pr * LANE, LANE)
            w0 = _tril(w_ref[2 * pr]).astype(bf16)
            w1 = _tril(w_ref[2 * pr + 1]).astype(bf16)
            for c in range(T // BLK):
                rows = pl.ds(c * BLK, BLK)
                vp = v[c * BLK:(c + 1) * BLK, pr * LANE:(pr + 1) * LANE]
                mixed = jnp.where(low, _dot(w0, vp), _dot(w1, vp)) + bias_ref[:, lanes]
                o_ref[rows, lanes] = (u_ref[rows, lanes] * mixed).astype(bf16)

    vec = pl.BlockSpec((1, SGU_CH), lambda i: (0, 0))
    return pl.pallas_call(
        body, name=name, grid=(S // T,),
        in_specs=[pl.BlockSpec((T, SGU_CH), lambda i: (i, U_BLK)), pl.BlockSpec((T, SGU_CH), lambda i: (i, SV_BLK)),
                  vec, vec, pl.BlockSpec((SGU_HEADS, BLK, BLK), lambda i: (0, 0, 0)),
                  pl.BlockSpec((BLK, SGU_CH), lambda i: (0, 0))],
        out_specs=pl.BlockSpec((T, SGU_CH), lambda i: (i, 0)),
        out_shape=jax.ShapeDtypeStruct((S, SGU_CH), bf16),
        compiler_params=_params("parallel"),
    )(p, p, lng, lnb, w, bias)


def sgu_bwd(p, dcat, lng, lnb, w, wt, bias, name):
    S = p.shape[0]
    T = _tile(S, 256)
    n = S // T
    dsb = (Q_END + CONV_CH) // SGU_CH

    def body(u_ref, v_ref, do_ref, g_ref, bb_ref, w_ref, wt_ref, bias_ref,
             da_ref, dw_ref, db_ref, dg_ref, dbb_ref, dv_scr, acc_bias, acc_g, acc_bb):
        i = pl.program_id(0)

        @pl.when(i == 0)
        def _():
            dw_ref[...] = jnp.zeros_like(dw_ref)
            acc_bias[...] = jnp.zeros_like(acc_bias)
            acc_g[...] = jnp.zeros_like(acc_g)
            acc_bb[...] = jnp.zeros_like(acc_bb)

        gv = g_ref[...]
        yh, rstd = _ln_stats(v_ref[...])
        v = (yh * gv + bb_ref[...]).astype(bf16)
        low = lax.broadcasted_iota(jnp.int32, (BLK, LANE), 1) < HEAD_DIM
        for pr in range(SGU_HEADS // 2):
            lanes = pl.ds(pr * LANE, LANE)
            w0 = _tril(w_ref[2 * pr]).astype(bf16)
            w1 = _tril(w_ref[2 * pr + 1]).astype(bf16)
            wt0 = _tril(wt_ref[2 * pr], True).astype(bf16)
            wt1 = _tril(wt_ref[2 * pr + 1], True).astype(bf16)
            dw0 = jnp.zeros((BLK, BLK), f32)
            dw1 = jnp.zeros((BLK, BLK), f32)
            for c in range(T // BLK):
                rows = pl.ds(c * BLK, BLK)
                vp = v[c * BLK:(c + 1) * BLK, pr * LANE:(pr + 1) * LANE]
                mixed = jnp.where(low, _dot(w0, vp), _dot(w1, vp)) + bias_ref[:, lanes]
                do = do_ref[rows, lanes]
                da_ref[rows, lanes] = (do * mixed).astype(bf16)
                dm = do * u_ref[rows, lanes]
                acc_bias[:, lanes] += dm
                dmb = dm.astype(bf16)
                dv_scr[rows, lanes] = jnp.where(low, _dot(wt0, dmb), _dot(wt1, dmb))
                zero = jnp.zeros_like(dmb)
                dw0 = dw0 + _dot(jnp.where(low, dmb, zero), vp, NT)
                dw1 = dw1 + _dot(jnp.where(low, zero, dmb), vp, NT)
            dw_ref[2 * pr] += _tril(dw0)
            dw_ref[2 * pr + 1] += _tril(dw1)
        dv = dv_scr[...]
        acc_g[...] += _rowsum8(dv * yh)
        acc_bb[...] += _rowsum8(dv)
        gz = dv * gv
        dvr = rstd * (gz - jnp.mean(gz, axis=-1, keepdims=True) - yh * jnp.mean(gz * yh, axis=-1, keepdims=True))
        da_ref[:, SGU_CH:] = dvr.astype(bf16)

        @pl.when(i == n - 1)
        def _():
            ch = lax.broadcasted_iota(jnp.int32, (SGU_CH, LANE), 0) // HEAD_DIM
            hd = lax.broadcasted_iota(jnp.int32, (SGU_CH, LANE), 1)
            fold = jnp.where(ch == hd, 1.0, 0.0).astype(f32)
            db_ref[...] = jnp.dot(acc_bias[...], fold, preferred_element_type=f32, precision=lax.Precision.HIGHEST)
            dg_ref[...] = jnp.sum(acc_g[...], axis=0, keepdims=True)
            dbb_ref[...] = jnp.sum(acc_bb[...], axis=0, keepdims=True)

    vec = pl.BlockSpec((1, SGU_CH), lambda i: (0, 0))
    wsp = pl.BlockSpec((SGU_HEADS, BLK, BLK), lambda i: (0, 0, 0))
    return pl.pallas_call(
        body, name=name, grid=(n,),
        in_specs=[pl.BlockSpec((T, SGU_CH), lambda i: (i, U_BLK)), pl.BlockSpec((T, SGU_CH), lambda i: (i, SV_BLK)),
                  pl.BlockSpec((T, SGU_CH), lambda i: (i, dsb)), vec, vec, wsp, wsp,
                  pl.BlockSpec((BLK, SGU_CH), lambda i: (0, 0))],
        out_specs=[pl.BlockSpec((T, 2 * SGU_CH), lambda i: (i, 0)), wsp,
                   pl.BlockSpec((BLK, LANE), lambda i: (0, 0)), vec, vec],
        out_shape=[jax.ShapeDtypeStruct((S, 2 * SGU_CH), bf16), jax.ShapeDtypeStruct((SGU_HEADS, BLK, BLK), f32),
                   jax.ShapeDtypeStruct((BLK, LANE), f32), jax.ShapeDtypeStruct((1, SGU_CH), f32),
                   jax.ShapeDtypeStruct((1, SGU_CH), f32)],
        scratch_shapes=[pltpu.VMEM((T, SGU_CH), f32), pltpu.VMEM((BLK, SGU_CH), f32),
                        pltpu.VMEM((8, SGU_CH), f32), pltpu.VMEM((8, SGU_CH), f32)],
        compiler_params=_params("arbitrary"),
    )(p, p, dcat, lng, lnb, w, wt, bias)


def _me_and_peers():
    x, y, c = lax.axis_index("x"), lax.axis_index("y"), lax.axis_index("c")
    me = 4 * x + 2 * y + c
    peers = []
    for k in range(1, N_DEV):
        px, py, pc = x ^ (k >> 2), y ^ ((k >> 1) & 1), c ^ (k & 1)
        peers.append(((px, py, pc), 4 * px + 2 * py + pc))
    return me, peers


HBM = pl.BlockSpec(memory_space=pltpu.HBM)
SEM = pl.BlockSpec(memory_space=pltpu.SEMAPHORE)
ANY = pl.BlockSpec(memory_space=pl.ANY)
EFFECT = pltpu.SideEffectType.DATAFLOW_SIDE_EFFECTING


def _flip(x, y, c, k):
    px, py, pc = x ^ (k >> 2), y ^ ((k >> 1) & 1), c ^ (k & 1)
    return (px, py, pc), 4 * px + 2 * py + pc


def _routes_gather(x, y, c):
    me = 4 * x + 2 * y + c
    out = []
    for k in (1, 2, 4, 6):
        dev, idx = _flip(x, y, c, k)
        out.append((dev, None, me, idx))
    return out


def _routes_pair(x, y, c):
    dev, _ = _flip(x, y, c, 1)
    return [(dev, 2 * q + (1 - c), q, q) for q in range(N_DEV // 2)]


def _routes_chips(x, y, c):
    out = []
    for k in (2, 4, 6):
        dev, idx = _flip(x, y, c, k)
        out.append((dev, idx // 2, 2 * x + y, idx // 2))
    return out


def _routes_forward(x, y, c):
    sib, _ = _flip(x, y, c, 1)
    out = []
    for k in (2, 4, 6):
        _, idx = _flip(x, y, c, k)
        out.append((sib, idx, idx, idx ^ 1))
    return out


def _copies(routes, srcs, lands, send_sems, recv_sems, incoming):
    x, y, c = lax.axis_index("x"), lax.axis_index("y"), lax.axis_index("c")
    out = []
    n = len(lands)
    if srcs is None:
        srcs = lands
    for k, (dev, src_slot, dst_slot, recv_slot) in enumerate(routes(x, y, c)):
        for a in range(n):
            src = srcs[a] if src_slot is None else srcs[a].at[src_slot]
            out.append(pltpu.make_async_remote_copy(
                src_ref=src, dst_ref=lands[a].at[recv_slot if incoming else dst_slot],
                send_sem=send_sems.at[k * n + a], recv_sem=recv_sems.at[k * n + a], device_id=dev, device_id_type=MESH))
    return out


def _pin(a):
    return pltpu.with_memory_space_constraint(a, pltpu.HBM)


def split_start(srcs, lands, routes, name, deps=()):
    n = len(lands)
    ns = 0 if srcs is None else n
    n_routes = len(routes(0, 0, 0))
    ops = ([] if srcs is None else list(srcs)) + list(lands)

    def body(*refs):
        src, land = (refs[:n] if ns else None), refs[ns:ns + n]
        first_out = ns + n + len(deps)
        send_sems, recv_sems, token = refs[first_out], refs[first_out + 1], refs[-1]
        for cp in _copies(routes, src, land, send_sems, recv_sems, False):
            cp.start()
        token[...] = jnp.zeros_like(token)

    thru = [pltpu.HBM(a.shape, a.dtype) for a in ops]
    res = pl.pallas_call(
        body, name=name,
        out_shape=(pltpu.SemaphoreType.DMA((n * n_routes,)), pltpu.SemaphoreType.DMA((n * n_routes,)), *thru,
                   jax.ShapeDtypeStruct((8, LANE), f32)),
        in_specs=[HBM] * len(ops) + [ANY] * len(deps),
        out_specs=(SEM, SEM, *([HBM] * len(ops)), pl.BlockSpec(memory_space=pltpu.VMEM)),
        input_output_aliases={i: 2 + i for i in range(len(ops))},
        compiler_params=pltpu.CompilerParams(has_side_effects=EFFECT),
    )(*[_pin(a) for a in ops], *deps)
    return res[0], res[1], (list(res[2:2 + n]) if ns else None), list(res[2 + ns:2 + ns + n]), res[-1]


def split_wait(started, after, routes, name):
    send_sems, recv_sems, srcs, lands, _ = started
    n = len(lands)
    ns = 0 if srcs is None else n
    ops = ([] if srcs is None else list(srcs)) + list(lands)

    def body(*refs):
        src, land = (refs[:n] if ns else None), refs[ns:ns + n]
        send_s, recv_s = refs[ns + n], refs[ns + n + 1]
        for cp in _copies(routes, src, land, send_s, recv_s, True):
            cp.wait_send()
            cp.wait_recv()

    thru = [pltpu.HBM(a.shape, a.dtype) for a in ops]
    res = pl.pallas_call(
        body, name=name, out_shape=tuple(thru),
        in_specs=[HBM] * len(ops) + [SEM, SEM, ANY], out_specs=tuple([HBM] * len(ops)),
        input_output_aliases={i: i for i in range(len(ops))},
        compiler_params=pltpu.CompilerParams(has_side_effects=EFFECT),
    )(*ops, send_sems, recv_sems, after)
    return (list(res[:n]) if ns else None), list(res[ns:ns + n])


def chip_sum(parts, land, c_idx, name):
    _, R, C = parts.shape
    tr = _tile(R, 256)

    def body(c_ref, p_ref, l_ref, o_ref):
        o_ref[...] = (p_ref[...].astype(f32) + l_ref[...].astype(f32)).astype(bf16)

    return pl.pallas_call(
        body, name=name,
        grid_spec=pltpu.PrefetchScalarGridSpec(
            num_scalar_prefetch=1, grid=(N_DEV // 2, R // tr),
            in_specs=[pl.BlockSpec((None, tr, C), lambda q, i, c_ref: (2 * q + c_ref[0], i, 0)),
                      pl.BlockSpec((None, tr, C), lambda q, i, c_ref: (q, i, 0))],
            out_specs=pl.BlockSpec((None, tr, C), lambda q, i, c_ref: (q, i, 0))),
        out_shape=jax.ShapeDtypeStruct((N_DEV // 2, R, C), bf16),
        compiler_params=_params("parallel", "parallel"),
    )(c_idx, parts, land)


def allreduce_small(buf, name):
    R = buf.shape[0]

    def body(src, out, land, send_sems, recv_sems):
        me, peers = _me_and_peers()
        land[me] = src[...]
        sends = []
        for k, (dev, idx) in enumerate(peers):
            cp = pltpu.make_async_remote_copy(src_ref=src, dst_ref=land.at[me], send_sem=send_sems.at[k],
                                              recv_sem=recv_sems.at[k], device_id=dev, device_id_type=MESH)
            cp.start()
            sends.append(cp)
        for k, (dev, idx) in enumerate(peers):
            pltpu.make_async_remote_copy(src_ref=src, dst_ref=land.at[idx], send_sem=send_sems.at[k],
                                         recv_sem=recv_sems.at[k], device_id=dev, device_id_type=MESH).wait_recv()
        for cp in sends:
            cp.wait_send()
        total = land[0]
        for j in range(1, N_DEV):
            total = total + land[j]
        out[...] = total

    vm = pl.BlockSpec(memory_space=pltpu.VMEM)
    return pl.pallas_call(
        body, name=name, in_specs=[vm], out_specs=vm, out_shape=jax.ShapeDtypeStruct((R, LANE), f32),
        scratch_shapes=[pltpu.VMEM((N_DEV, R, LANE), f32), pltpu.SemaphoreType.DMA((N_DEV - 1,)),
                        pltpu.SemaphoreType.DMA((N_DEV - 1,))],
        compiler_params=pltpu.CompilerParams(has_side_effects=True, vmem_limit_bytes=VMEM_LIMIT),
    )(buf)


def adamw(parts, owns, chip, w, m, v, name):
    L, R, C = w.shape
    P = parts[0].shape[0]
    tr = _tile(R, 128 if C > 1024 else 256)
    nr = R // tr
    c1 = 1.0 - ADAM_B1 ** ADAM_STEP
    c2 = 1.0 - ADAM_B2 ** ADAM_STEP
    n_own = L if owns is not None else 0

    def body(chip_ref, *refs):
        part_refs, own_refs = refs[:L], refs[L:L + n_own]
        w_ref, m_ref, v_ref, g_out, d_out, m_out, v_out = refs[L + n_own:]
        layer = pl.program_id(0)
        for l in range(L):
            @pl.when(layer == l)
            def _(l=l):
                g = None
                for q in range(P):
                    term = part_refs[l][q].astype(f32)
                    if n_own:
                        term = jnp.where(chip_ref[0] == q, own_refs[l][...].astype(f32), term)
                    g = term if g is None else g + term
                mn = ADAM_B1 * m_ref[...] + (1.0 - ADAM_B1) * g
                vn = ADAM_B2 * v_ref[...] + (1.0 - ADAM_B2) * (g * g)
                g_out[...] = g
                m_out[...] = mn
                v_out[...] = vn
                d_out[...] = -ADAM_LR * ((mn / c1) / (jnp.sqrt(vn / c2) + ADAM_EPS) + ADAM_WD * w_ref[...])

    def rows(l, a, i):
        return jnp.where(a == l, i, jnp.where(a < l, 0, nr - 1))

    def part_spec(l):
        return pl.BlockSpec((P, tr, C), lambda a, i, chip_ref: (0, rows(l, a, i), 0))

    def own_spec(l):
        return pl.BlockSpec((None, tr, C), lambda a, i, chip_ref: (chip_ref[0], rows(l, a, i), 0))

    slab = pl.BlockSpec((None, tr, C), lambda a, i, chip_ref: (a, i, 0))
    out = jax.ShapeDtypeStruct((L, R, C), f32)
    return pl.pallas_call(
        body, name=name,
        grid_spec=pltpu.PrefetchScalarGridSpec(
            num_scalar_prefetch=1, grid=(L, nr),
            in_specs=[part_spec(l) for l in range(L)] + [own_spec(l) for l in range(n_own)] + [slab, slab, slab],
            out_specs=[slab, slab, slab, slab]),
        out_shape=[out, out, out, out],
        compiler_params=_params("arbitrary", "arbitrary"),
    )(chip, *parts, *(owns or []), w, m, v)


PACK = 8 * LANE


def _pack(arrs):
    pieces = []
    for a in arrs:
        flat = a.astype(f32).reshape(-1)
        pad = (-flat.shape[0]) % PACK
        pieces.append(jnp.pad(flat, (0, pad)).reshape(-1, LANE))
    return jnp.concatenate(pieces, axis=0)


def _unpack(buf, shapes):
    out, row = [], 0
    for shp in shapes:
        size = math.prod(shp)
        rows = (size + PACK - 1) // PACK * (PACK // LANE)
        out.append(buf[row:row + rows].reshape(-1)[:size].reshape(shp))
        row += rows
    return out


def _rope_tables(positions):
    half = ROT_DIM // 2
    inv_freq = 1.0 / (ROPE_THETA ** (jnp.arange(0, ROT_DIM, 2, dtype=f32) / ROT_DIM))
    ang = positions.astype(f32)[:, None] * inv_freq
    cos, sin = jnp.cos(ang), jnp.sin(ang)
    S = positions.shape[0]
    zeros, ones = jnp.zeros((S, half), f32), jnp.ones((S, HEAD_DIM - ROT_DIM), f32)
    rest = jnp.zeros((S, HEAD_DIM - ROT_DIM), f32)
    c = jnp.concatenate([cos, cos, ones], axis=1)
    s1 = jnp.concatenate([zeros, sin, rest], axis=1)
    s2 = jnp.concatenate([-sin, zeros, rest], axis=1)
    return tuple(jnp.tile(t, (1, LANE // HEAD_DIM)) for t in (c, s1, s2))


def _cols_to_shards(g):
    lead, (R, N) = g.shape[:-2], g.shape[-2:]
    g = g.reshape(lead + (R, N_DEV, N // N_DEV))
    return jnp.moveaxis(g, -2, 0)


def _shards_to_cols(g):
    g = jnp.moveaxis(g, 0, -2)
    return g.reshape(g.shape[:-2] + (g.shape[-2] * g.shape[-1],))


def kernel(x, positions, norm_ffn1, ffn1_w_in, ffn1_w_out, norm_mix, w_in, conv_dw_w, conv_dw_b, conv_ln_g, conv_ln_b, sgu_ln_g, sgu_ln_b, sgu_w, sgu_b, attn_sinks, w_out, norm_ffn2, ffn2_w_in, ffn2_w_out, final_norm, loss_target, m_norm_ffn1, m_ffn1_w_in, m_ffn1_w_out, m_norm_mix, m_w_in, m_conv_dw_w, m_conv_dw_b, m_conv_ln_g, m_conv_ln_b, m_sgu_ln_g, m_sgu_ln_b, m_sgu_w, m_sgu_b, m_attn_sinks, m_w_out, m_norm_ffn2, m_ffn2_w_in, m_ffn2_w_out, m_final_norm, v_norm_ffn1, v_ffn1_w_in, v_ffn1_w_out, v_norm_mix, v_w_in, v_conv_dw_w, v_conv_dw_b, v_conv_ln_g, v_conv_ln_b, v_sgu_ln_g, v_sgu_ln_b, v_sgu_w, v_sgu_b, v_attn_sinks, v_w_out, v_norm_ffn2, v_ffn2_w_in, v_ffn2_w_out, v_final_norm):
    L = norm_ffn1.shape[0]
    S, D = x.shape[1], x.shape[2]
    F = ffn1_w_out.shape[1] * N_DEV
    me = 4 * lax.axis_index("x") + 2 * lax.axis_index("y") + lax.axis_index("c")
    x0 = x[0]
    rope_c, rope_s1, rope_s2 = _rope_tables(positions[0])
    cw = CONV_CH // N_DEV

    c_idx = lax.axis_index("c").astype(jnp.int32).reshape(1)
    chip = (2 * lax.axis_index("x") + lax.axis_index("y")).astype(jnp.int32).reshape(1)
    no_chip = jnp.zeros((1,), jnp.int32)

    row = lambda a: a.reshape(1, -1)
    groups = ("ffn1", "mix", "ffn2")
    order_fwd = [(l, g) for l in range(L) for g in groups]

    def group_srcs(l, grp):
        if grp == "ffn1":
            return [ffn1_w_in[l].astype(bf16), ffn1_w_out[l].astype(bf16)]
        if grp == "ffn2":
            return [ffn2_w_in[l].astype(bf16), ffn2_w_out[l].astype(bf16)]
        taps = jnp.pad(conv_dw_w[l], ((0, HALO - CONV_W), (0, LANE - cw)))
        return [w_in[l].astype(bf16), w_out[l].astype(bf16), taps]

    def gather_start(k, deps=()):
        l, grp = order_fwd[k]
        srcs = group_srcs(l, grp)
        return split_start(srcs, [lax.empty((N_DEV,) + a.shape, a.dtype) for a in srcs], _routes_gather,
                           f"gather_start_{grp}_{l}", deps)

    def gather_forward(k, started, after):
        l, grp = order_fwd[k]
        srcs, lands = split_wait(started, after, _routes_gather, f"gather_wait_{grp}_{l}")
        return srcs, split_start(None, lands, _routes_forward, f"forward_start_{grp}_{l}")

    def gather_finish(k, srcs, started, after):
        l, grp = order_fwd[k]
        _, lands = split_wait(started, after, _routes_forward, f"forward_wait_{grp}_{l}")
        full = [lax.dynamic_update_index_in_dim(ld, s, me, 0) for ld, s in zip(lands, srcs)]
        if grp == "mix":
            return dict(w_in=_shards_to_cols(full[0]), w_out=full[1].reshape(D, D),
                        taps=_shards_to_cols(full[2][:, :, :cw]))
        return dict(w_in=_shards_to_cols(full[0]).reshape(D, 2, F).transpose(1, 0, 2),
                    w_out=full[1].reshape(F, D))

    def ffn_fwd(xs, wts, g_norm, tag, sv):
        sv["x_in"] = xs
        h, sv["ht"] = rmsnorm_fwd(xs, g_norm, f"norm_{tag}")
        sv["gu"], a, sv["at"] = ffn_in(h, wts["w_in"], f"{tag}_in")
        return mm_res(a, wts["w_out"], xs, FFN_RES, f"{tag}_out")

    def mix_fwd(xs, wts, l, sv):
        sv["x_in"] = xs
        h, sv["ht"] = rmsnorm_fwd(xs, row(norm_mix[l]), f"norm_mix_{l}")
        p = mm_nn(h, wts["w_in"][None], f"mix_in_{l}")[0]
        sv["p"] = p
        attn = attn_fwd(p, rope_c, rope_s1, rope_s2, attn_sinks[l], f"attn_fwd_{l}")
        conv, sv["conv_y"] = conv_fwd(p, wts["taps"], row(conv_dw_b[l]), row(conv_ln_g[l]), row(conv_ln_b[l]),
                                      f"conv_fwd_{l}")
        sv["sgu_bias"] = jnp.repeat(sgu_b[l].T, HEAD_DIM, axis=1)
        sgu = sgu_fwd(p, row(sgu_ln_g[l]), row(sgu_ln_b[l]), sgu_w[l], sv["sgu_bias"], f"sgu_fwd_{l}")
        cat = jnp.concatenate([attn, conv, sgu], axis=1)
        sv["catt"] = cat.T
        return mm_res(cat, wts["w_out"], xs, 1.0, f"mix_out_{l}")

    weights, saved = {}, {}
    xs = x0
    first = gather_start(0)
    srcs, fwd = gather_forward(0, first, first[4])
    for k, (l, grp) in enumerate(order_fwd):
        nxt = gather_start(k + 1, (fwd[4],)) if k + 1 < len(order_fwd) else None
        wts = gather_finish(k, srcs, fwd, nxt[4] if nxt else fwd[4])
        sv = {}
        if grp == "ffn1":
            xs = ffn_fwd(xs, wts, row(norm_ffn1[l]), f"ffn1_{l}", sv)
        elif grp == "ffn2":
            xs = ffn_fwd(xs, wts, row(norm_ffn2[l]), f"ffn2_{l}", sv)
        else:
            xs = mix_fwd(xs, wts, l, sv)
        weights[l, grp], saved[l, grp] = wts, sv
        if nxt:
            srcs, fwd = gather_forward(k + 1, nxt, xs)

    dx, dxb, d_final_norm, loss = final_loss(xs, row(final_norm), loss_target[0], "final_loss")

    def scatter_start(grads, tag):
        return split_start(grads, [lax.empty((N_DEV // 2,) + g.shape[1:], g.dtype) for g in grads], _routes_pair,
                           f"pair_start_{tag}")

    def pair_to_chips(started, after, tag):
        grads, landed = split_wait(started, after, _routes_pair, f"pair_wait_{tag}")
        sums = [chip_sum(g, ld, c_idx, f"chip_sum_{tag}_{a}") for a, (g, ld) in enumerate(zip(grads, landed))]
        return split_start(sums, [lax.empty(s.shape, s.dtype) for s in sums], _routes_chips, f"chips_start_{tag}")

    def in_shards(d2):
        return _cols_to_shards(d2.transpose(1, 0, 2).reshape(D, 2 * F))

    def ffn_bwd(dx, dxb, wts, sv, g_norm, tag):
        dgu = ffn_dact(dxb, wts["w_out"], sv["gu"], f"{tag}_dact")
        d_w_out = mm_nn(sv["at"], dxb[None], f"{tag}_dwout", bf16, FFN_RES)[0]
        d_w_in = mm_nn(sv["ht"], dgu, f"{tag}_dwin", bf16)
        pair = scatter_start([in_shards(d_w_in), d_w_out.reshape(N_DEV, F // N_DEV, D)], tag)
        dh = mm_nt(dgu, wts["w_in"], f"{tag}_dh", deps=(pair[4],))
        chips = pair_to_chips(pair, dh, tag)
        dx, dxb, dg = rmsnorm_bwd(dh, sv["x_in"], g_norm, dx, f"{tag}_dnorm", deps=(chips[4],))
        return dx, dxb, dg, chips

    small = [None] * L
    chips_pending = {}
    for l in reversed(range(L)):
        dx, dxb, d_norm_ffn2, chips_pending[l, "ffn2"] = ffn_bwd(
            dx, dxb, weights[l, "ffn2"], saved[l, "ffn2"], row(norm_ffn2[l]), f"ffn2_{l}")

        wts, sv = weights[l, "mix"], saved[l, "mix"]
        d_w_out = mm_nn(sv["catt"], dxb[None], f"mix_dwout_{l}", bf16)[0]
        dcat = mm_nt(dxb[None], wts["w_out"][None], f"mix_dcat_{l}", deps=(d_w_out,))
        p = sv["p"]
        dq, dkv, d_sinks = attn_bwd(p, dcat, rope_c, rope_s1, rope_s2, attn_sinks[l], f"attn_bwd_{l}")
        da_conv, d_taps, d_conv_b, d_conv_g, d_conv_bb = conv_bwd(
            p, sv["conv_y"], dcat, wts["taps"], row(conv_ln_g[l]), row(conv_ln_b[l]), f"conv_bwd_{l}")
        da_sgu, d_sgu_w, d_sgu_bias, d_sgu_g, d_sgu_bb = sgu_bwd(
            p, dcat, row(sgu_ln_g[l]), row(sgu_ln_b[l]), sgu_w[l], jnp.swapaxes(sgu_w[l], 1, 2), sv["sgu_bias"],
            f"sgu_bwd_{l}")
        dp = jnp.concatenate([dq, dkv, da_conv, da_sgu], axis=1)
        d_w_in = mm_nn(sv["ht"], dp[None], f"mix_dwin_{l}", bf16)[0]
        pair = scatter_start([_cols_to_shards(d_w_in), d_w_out.reshape(N_DEV, D // N_DEV, D)], f"mix_{l}")
        dh = mm_nt(dp[None], wts["w_in"][None], f"mix_dh_{l}", deps=(pair[4],))
        chips_pending[l, "mix"] = pair_to_chips(pair, dh, f"mix_{l}")
        dx, dxb, d_norm_mix = rmsnorm_bwd(dh, sv["x_in"], row(norm_mix[l]), dx, f"mix_dnorm_{l}",
                                          deps=(chips_pending[l, "mix"][4],))

        dx, dxb, d_norm_ffn1, chips_pending[l, "ffn1"] = ffn_bwd(
            dx, dxb, weights[l, "ffn1"], saved[l, "ffn1"], row(norm_ffn1[l]), f"ffn1_{l}")

        small[l] = dict(norm_ffn1=d_norm_ffn1[0], norm_mix=d_norm_mix[0], conv_dw_w=d_taps[:CONV_W],
                        conv_dw_b=d_conv_b[0], conv_ln_g=d_conv_g[0], conv_ln_b=d_conv_bb[0], sgu_ln_g=d_sgu_g[0],
                        sgu_ln_b=d_sgu_bb[0], sgu_w=d_sgu_w, sgu_b=d_sgu_bias[:, :SGU_HEADS].T,
                        attn_sinks=d_sinks[0, :N_Q_HEADS], norm_ffn2=d_norm_ffn2[0])

    grad_x = dx[None]

    def landed(grp, after):
        got = [split_wait(chips_pending[l, grp], after, _routes_chips, f"chips_wait_{grp}_{l}") for l in range(L)]
        return [g[0] for g in got], [g[1] for g in got]

    big = {}
    after = dx
    for grp, names in (("ffn2", (("ffn2_w_in", ffn2_w_in, m_ffn2_w_in, v_ffn2_w_in),
                                 ("ffn2_w_out", ffn2_w_out, m_ffn2_w_out, v_ffn2_w_out))),
                       ("mix", (("w_in", w_in, m_w_in, v_w_in), ("w_out", w_out, m_w_out, v_w_out))),
                       ("ffn1", (("ffn1_w_in", ffn1_w_in, m_ffn1_w_in, v_ffn1_w_in),
                                 ("ffn1_w_out", ffn1_w_out, m_ffn1_w_out, v_ffn1_w_out)))):
        sums, lands = landed(grp, after)
        for idx, (name, w, m, v) in enumerate(names):
            big[name] = adamw([lands[l][idx] for l in range(L)], [sums[l][idx] for l in range(L)], chip, w, m, v,
                              f"adamw_{name}")
            after = big[name][0]

    small_names = ["norm_ffn1", "norm_mix", "conv_dw_w", "conv_dw_b", "conv_ln_g", "conv_ln_b", "sgu_ln_g", "sgu_ln_b",
                   "sgu_w", "sgu_b", "attn_sinks", "norm_ffn2"]
    local = [jnp.stack([small[l][n] for l in range(L)]) for n in small_names] + [d_final_norm[0], loss.reshape(1)]
    shapes = [a.shape for a in local]
    total = _unpack(allreduce_small(_pack(local), "sum_small_grads"), shapes)
    g_small = dict(zip(small_names + ["final_norm", "loss"], total))
    g_small["conv_dw_w"] = lax.dynamic_slice_in_dim(g_small["conv_dw_w"], me * cw, cw, axis=2)
    given = dict(norm_ffn1=(norm_ffn1, m_norm_ffn1, v_norm_ffn1), norm_mix=(norm_mix, m_norm_mix, v_norm_mix),
                 conv_dw_w=(conv_dw_w, m_conv_dw_w, v_conv_dw_w), conv_dw_b=(conv_dw_b, m_conv_dw_b, v_conv_dw_b),
                 conv_ln_g=(conv_ln_g, m_conv_ln_g, v_conv_ln_g), conv_ln_b=(conv_ln_b, m_conv_ln_b, v_conv_ln_b),
                 sgu_ln_g=(sgu_ln_g, m_sgu_ln_g, v_sgu_ln_g), sgu_ln_b=(sgu_ln_b, m_sgu_ln_b, v_sgu_ln_b),
                 sgu_w=(sgu_w, m_sgu_w, v_sgu_w), sgu_b=(sgu_b, m_sgu_b, v_sgu_b),
                 attn_sinks=(attn_sinks, m_attn_sinks, v_attn_sinks), norm_ffn2=(norm_ffn2, m_norm_ffn2, v_norm_ffn2),
                 final_norm=(final_norm, m_final_norm, v_final_norm))
    upd_names = small_names + ["final_norm"]
    upd_shapes = [given[n][0].shape for n in upd_names]
    packed = [_pack([g_small[n] for n in upd_names])[None]] + [_pack([given[n][k] for n in upd_names])[None] for k in range(3)]
    res = adamw([packed[0]], None, no_chip, packed[1], packed[2], packed[3], "adamw_small")
    upd = [dict(zip(upd_names, _unpack(r[0], upd_shapes))) for r in res]

    order = ["norm_ffn1", "ffn1_w_in", "ffn1_w_out", "norm_mix", "w_in", "conv_dw_w", "conv_dw_b", "conv_ln_g",
             "conv_ln_b", "sgu_ln_g", "sgu_ln_b", "sgu_w", "sgu_b", "attn_sinks", "w_out", "norm_ffn2", "ffn2_w_in",
             "ffn2_w_out", "final_norm"]
    outs = [g_small["loss"].reshape(()), grad_x]
    for k in range(4):
        outs += [big[n][k] if n in big else upd[k][n] for n in order]
    return tuple(outs)
```

```python
import functools
import math

import jax
import jax.numpy as jnp
from jax import lax
from jax.experimental import pallas as pl
from jax.experimental.pallas import tpu as pltpu

f32 = jnp.float32
bf16 = jnp.bfloat16

N_DEV = 8
HEAD_DIM = 64
N_Q_HEADS = 16
N_KV_HEADS = 4
GQ = N_Q_HEADS // N_KV_HEADS
BLK = 128
ROT_DIM = 16
ROPE_THETA = 500000.0
CONV_W = 31
CONV_CH = 512
SGU_CH = 512
SGU_HEADS = 8
Q_END = N_Q_HEADS * HEAD_DIM
KV_W = 2 * N_KV_HEADS * HEAD_DIM
V_END = Q_END + KV_W
IN_COLS = V_END + 2 * CONV_CH + 2 * SGU_CH
HALO = 32
NORM_EPS = 1e-5
FFN_RES = 0.5
ADAM_LR, ADAM_B1, ADAM_B2, ADAM_EPS, ADAM_WD, ADAM_STEP = 0.001, 0.9, 0.999, 1e-08, 0.01, 10
LANE = 128
VMEM_LIMIT = 56 * 2 ** 20
MESH = pl.DeviceIdType.MESH

NN = (((1,), (0,)), ((), ()))
NT = (((1,), (1,)), ((), ()))
TN = (((0,), (0,)), ((), ()))


def _tile(dim, pref):
    t = min(pref, dim)
    while dim % t:
        t //= 2
    return t


def _params(*sem):
    return pltpu.CompilerParams(dimension_semantics=sem, vmem_limit_bytes=VMEM_LIMIT)


def _sig(x):
    return 1.0 / (1.0 + jnp.exp(-x))


def _dot(a, b, dims=NN):
    return lax.dot_general(a, b, dims, preferred_element_type=f32)


def _rowsum8(x):
    return x.reshape(x.shape[0] // 8, 8, x.shape[1]).sum(axis=0)


def rmsnorm_fwd(x, g, name, deps=()):
    S, D = x.shape
    tm = _tile(S, 512)

    def body(x_ref, g_ref, *rest):
        o_ref, ot_ref = rest[-2:]
        xv = x_ref[...]
        r = lax.rsqrt(jnp.mean(xv * xv, axis=-1, keepdims=True) + NORM_EPS)
        hb = (xv * r * g_ref[...]).astype(bf16)
        o_ref[...] = hb
        ot_ref[...] = hb.T

    return pl.pallas_call(
        body, name=name, grid=(S // tm,),
        in_specs=[pl.BlockSpec((tm, D), lambda i: (i, 0)), pl.BlockSpec((1, D), lambda i: (0, 0))] + [ANY] * len(deps),
        out_specs=[pl.BlockSpec((tm, D), lambda i: (i, 0)), pl.BlockSpec((D, tm), lambda i: (0, i))],
        out_shape=[jax.ShapeDtypeStruct((S, D), bf16), jax.ShapeDtypeStruct((D, S), bf16)],
        compiler_params=_params("parallel"),
    )(x, g, *deps)


def rmsnorm_bwd(dh, x, g, dres, name, deps=()):
    S, D = x.shape
    tm = _tile(S, 256)
    n = S // tm

    def body(dh_ref, x_ref, g_ref, dres_ref, *rest):
        dx_ref, dxb_ref, dg_ref, acc = rest[-4:]
        i = pl.program_id(0)

        @pl.when(i == 0)
        def _():
            acc[...] = jnp.zeros_like(acc)

        xv = x_ref[...]
        r = lax.rsqrt(jnp.mean(xv * xv, axis=-1, keepdims=True) + NORM_EPS)
        xh = xv * r
        dy = dh_ref[...]
        gy = dy * g_ref[...]
        dx = dres_ref[...] + r * (gy - xh * jnp.mean(gy * xh, axis=-1, keepdims=True))
        dx_ref[...] = dx
        dxb_ref[...] = dx.astype(bf16)
        acc[...] += _rowsum8(dy * xh)

        @pl.when(i == n - 1)
        def _():
            dg_ref[...] = jnp.sum(acc[...], axis=0, keepdims=True)

    row = pl.BlockSpec((tm, D), lambda i: (i, 0))
    vec = pl.BlockSpec((1, D), lambda i: (0, 0))
    return pl.pallas_call(
        body, name=name, grid=(n,),
        in_specs=[row, row, vec, row] + [ANY] * len(deps),
        out_specs=[row, row, vec],
        out_shape=[jax.ShapeDtypeStruct((S, D), f32), jax.ShapeDtypeStruct((S, D), bf16),
                   jax.ShapeDtypeStruct((1, D), f32)],
        scratch_shapes=[pltpu.VMEM((8, D), f32)],
        compiler_params=_params("arbitrary"),
    )(dh, x, g, dres, *deps)


def final_loss(x, g, tgt, name):
    S, D = x.shape
    tm = _tile(S, 256)
    n = S // tm

    def body(x_ref, g_ref, t_ref, dx_ref, dxb_ref, dg_ref, loss_ref, acc):
        i = pl.program_id(0)

        @pl.when(i == 0)
        def _():
            acc[...] = jnp.zeros_like(acc)
            loss_ref[...] = jnp.zeros_like(loss_ref)

        xv = x_ref[...]
        gv = g_ref[...]
        r = lax.rsqrt(jnp.mean(xv * xv, axis=-1, keepdims=True) + NORM_EPS)
        xh = xv * r
        diff = xh * gv - t_ref[...]
        tok = jnp.mean(diff * diff, axis=-1, keepdims=True)
        loss_ref[...] += 0.5 * jnp.sum(tok, axis=0, keepdims=True)
        dy = diff / D
        gy = dy * gv
        dx = r * (gy - xh * jnp.mean(gy * xh, axis=-1, keepdims=True))
        dx_ref[...] = dx
        dxb_ref[...] = dx.astype(bf16)
        acc[...] += _rowsum8(dy * xh)

        @pl.when(i == n - 1)
        def _():
            dg_ref[...] = jnp.sum(acc[...], axis=0, keepdims=True)

    row = pl.BlockSpec((tm, D), lambda i: (i, 0))
    vec = pl.BlockSpec((1, D), lambda i: (0, 0))
    return pl.pallas_call(
        body, name=name, grid=(n,),
        in_specs=[row, vec, row],
        out_specs=[row, row, vec, pl.BlockSpec((1, 1), lambda i: (0, 0))],
        out_shape=[jax.ShapeDtypeStruct((S, D), f32), jax.ShapeDtypeStruct((S, D), bf16),
                   jax.ShapeDtypeStruct((1, D), f32), jax.ShapeDtypeStruct((1, 1), f32)],
        scratch_shapes=[pltpu.VMEM((8, D), f32)],
        compiler_params=_params("arbitrary"),
    )(x, g, tgt)


def ffn_in(h, w2, name):
    S, D = h.shape
    F = w2.shape[2]
    tm, tn = _tile(S, 512), _tile(F, 512)

    def body(h_ref, w_ref, gu_ref, a_ref, at_ref):
        hv = h_ref[...]
        g = _dot(hv, w_ref[0])
        u = _dot(hv, w_ref[1])
        gu_ref[0] = g.astype(bf16)
        gu_ref[1] = u.astype(bf16)
        a = (g * _sig(g) * u).astype(bf16)
        a_ref[...] = a
        at_ref[...] = a.T

    return pl.pallas_call(
        body, name=name, grid=(S // tm, F // tn),
        in_specs=[pl.BlockSpec((tm, D), lambda i, j: (i, 0)), pl.BlockSpec((2, D, tn), lambda i, j: (0, 0, j))],
        out_specs=[pl.BlockSpec((2, tm, tn), lambda i, j: (0, i, j)), pl.BlockSpec((tm, tn), lambda i, j: (i, j)),
                   pl.BlockSpec((tn, tm), lambda i, j: (j, i))],
        out_shape=[jax.ShapeDtypeStruct((2, S, F), bf16), jax.ShapeDtypeStruct((S, F), bf16),
                   jax.ShapeDtypeStruct((F, S), bf16)],
        compiler_params=_params("parallel", "parallel"),
    )(h, w2)


def mm_res(a, w, x, scale, name):
    S, K = a.shape
    N = w.shape[1]
    tm, tn = _tile(S, 512), _tile(N, 512)

    def body(a_ref, w_ref, x_ref, o_ref):
        o_ref[...] = x_ref[...] + scale * _dot(a_ref[...], w_ref[...])

    return pl.pallas_call(
        body, name=name, grid=(S // tm, N // tn),
        in_specs=[pl.BlockSpec((tm, K), lambda i, j: (i, 0)), pl.BlockSpec((K, tn), lambda i, j: (0, j)),
                  pl.BlockSpec((tm, tn), lambda i, j: (i, j))],
        out_specs=pl.BlockSpec((tm, tn), lambda i, j: (i, j)),
        out_shape=jax.ShapeDtypeStruct((S, N), f32),
        compiler_params=_params("parallel", "parallel"),
    )(a, w, x)


def mm_nn(a, b, name, out_dtype=f32, scale=1.0, deps=()):
    M, K = a.shape
    G, _, N = b.shape
    tm, tn = _tile(M, 512), _tile(N, 512)

    def body(a_ref, b_ref, *rest):
        acc = _dot(a_ref[...], b_ref[...])
        rest[-1][...] = (acc if scale == 1.0 else scale * acc).astype(out_dtype)

    return pl.pallas_call(
        body, name=name, grid=(G, M // tm, N // tn),
        in_specs=[pl.BlockSpec((tm, K), lambda g, i, j: (i, 0)),
                  pl.BlockSpec((None, K, tn), lambda g, i, j: (g, 0, j))] + [ANY] * len(deps),
        out_specs=pl.BlockSpec((None, tm, tn), lambda g, i, j: (g, i, j)),
        out_shape=jax.ShapeDtypeStruct((G, M, N), out_dtype),
        compiler_params=_params("parallel", "parallel", "parallel"),
    )(a, b, *deps)


def mm_nt(a, w, name, deps=()):
    G, S, K = a.shape
    N = w.shape[1]
    tm, tn = _tile(S, 512), _tile(N, 512)

    def body(a_ref, w_ref, *rest):
        o_ref = rest[-1]
        part = _dot(a_ref[...], w_ref[...], NT)
        if G == 1:
            o_ref[...] = part
        else:
            g = pl.program_id(2)

            @pl.when(g == 0)
            def _():
                o_ref[...] = part

            @pl.when(g > 0)
            def _():
                o_ref[...] += part

    return pl.pallas_call(
        body, name=name, grid=(S // tm, N // tn, G),
        in_specs=[pl.BlockSpec((None, tm, K), lambda i, j, g: (g, i, 0)),
                  pl.BlockSpec((None, tn, K), lambda i, j, g: (g, j, 0))] + [ANY] * len(deps),
        out_specs=pl.BlockSpec((tm, tn), lambda i, j, g: (i, j)),
        out_shape=jax.ShapeDtypeStruct((S, N), f32),
        compiler_params=_params("parallel", "parallel", "arbitrary"),
    )(a, w, *deps)


def ffn_dact(dx, wout, gu, name):
    S, D = dx.shape
    F = wout.shape[0]
    tm, tn = _tile(S, 512), _tile(F, 512)

    def body(dx_ref, w_ref, gu_ref, o_ref):
        da = FFN_RES * _dot(dx_ref[...], w_ref[...], NT)
        g = gu_ref[0].astype(f32)
        u = gu_ref[1].astype(f32)
        sg = _sig(g)
        o_ref[0] = (da * u * (sg * (1.0 + g * (1.0 - sg)))).astype(bf16)
        o_ref[1] = (da * (g * sg)).astype(bf16)

    return pl.pallas_call(
        body, name=name, grid=(S // tm, F // tn),
        in_specs=[pl.BlockSpec((tm, D), lambda i, j: (i, 0)), pl.BlockSpec((tn, D), lambda i, j: (j, 0)),
                  pl.BlockSpec((2, tm, tn), lambda i, j: (0, i, j))],
        out_specs=pl.BlockSpec((2, tm, tn), lambda i, j: (0, i, j)),
        out_shape=jax.ShapeDtypeStruct((2, S, F), bf16),
        compiler_params=_params("parallel", "parallel"),
    )(dx, wout, gu)


def _rope(t, c, s1, s2):
    w = t.shape[1]
    return t * c + pltpu.roll(t, 8, 1) * s1 + pltpu.roll(t, w - 8, 1) * s2


def _rope_t(d, c, s1, s2):
    w = d.shape[1]
    return d * c + pltpu.roll(d * s1, w - 8, 1) + pltpu.roll(d * s2, 8, 1)


def _attn_mask(n):
    qi = lax.broadcasted_iota(jnp.int32, (BLK, 2 * BLK), 0)
    kj = lax.broadcasted_iota(jnp.int32, (BLK, 2 * BLK), 1)
    dist = qi + BLK - kj
    return (dist >= 0) & (dist < BLK) & ((kj >= BLK) | (n > 0))


def _softmax_sink(s, valid, sk):
    s = jnp.where(valid, s, -1e30)
    m = jnp.maximum(jnp.max(s, axis=-1, keepdims=True), sk)
    e = jnp.exp(s - m)
    es = jnp.exp(sk - m)
    inv = 1.0 / (jnp.sum(e, axis=-1, keepdims=True) + es)
    return e * inv, es * inv


def attn_fwd(p, rope_c, rope_s1, rope_s2, sinks, name):
    S = p.shape[0]
    nb = S // BLK
    kvb = Q_END // KV_W

    def body(sink_ref, q_ref, kvc_ref, kvp_ref, cc_ref, s1c_ref, s2c_ref, cp_ref, s1p_ref, s2p_ref, o_ref):
        n = pl.program_id(0)
        cc, s1c, s2c = cc_ref[...], s1c_ref[...], s2c_ref[...]
        cp, s1p, s2p = cp_ref[...], s1p_ref[...], s2p_ref[...]
        q = _rope(q_ref[...], jnp.tile(cc, (1, 8)), jnp.tile(s1c, (1, 8)), jnp.tile(s2c, (1, 8)))
        kc = _rope(kvc_ref[:, :256], jnp.tile(cc, (1, 2)), jnp.tile(s1c, (1, 2)), jnp.tile(s2c, (1, 2)))
        kp = _rope(kvp_ref[:, :256], jnp.tile(cp, (1, 2)), jnp.tile(s1p, (1, 2)), jnp.tile(s2p, (1, 2)))
        k = jnp.concatenate([kp, kc], axis=0).astype(bf16)
        v = jnp.concatenate([kvp_ref[:, 256:], kvc_ref[:, 256:]], axis=0).astype(bf16)
        q = q.astype(bf16)
        valid = _attn_mask(n)
        for h in range(N_KV_HEADS):
            kh = k[:, h * HEAD_DIM:(h + 1) * HEAD_DIM]
            vh = v[:, h * HEAD_DIM:(h + 1) * HEAD_DIM]
            for g in range(GQ):
                hq = h * GQ + g
                qh = q[:, hq * HEAD_DIM:(hq + 1) * HEAD_DIM]
                s = _dot(qh, kh, NT) * (HEAD_DIM ** -0.5)
                pr, _ = _softmax_sink(s, valid, sink_ref[hq])
                o = _dot(pr.astype(bf16), vh)
                o_ref[:, hq * HEAD_DIM:(hq + 1) * HEAD_DIM] = o.astype(bf16)

    tab_c = pl.BlockSpec((BLK, LANE), lambda n: (n, 0))
    tab_p = pl.BlockSpec((BLK, LANE), lambda n: (jnp.maximum(n - 1, 0), 0))
    return pl.pallas_call(
        body, name=name, grid=(nb,),
        in_specs=[pl.BlockSpec(memory_space=pltpu.SMEM),
                  pl.BlockSpec((BLK, Q_END), lambda n: (n, 0)),
                  pl.BlockSpec((BLK, KV_W), lambda n: (n, kvb)),
                  pl.BlockSpec((BLK, KV_W), lambda n: (jnp.maximum(n - 1, 0), kvb)),
                  tab_c, tab_c, tab_c, tab_p, tab_p, tab_p],
        out_specs=pl.BlockSpec((BLK, Q_END), lambda n: (n, 0)),
        out_shape=jax.ShapeDtypeStruct((S, Q_END), bf16),
        compiler_params=_params("parallel"),
    )(sinks, p, p, p, rope_c, rope_s1, rope_s2, rope_c, rope_s1, rope_s2)


def attn_bwd(p, dcat, rope_c, rope_s1, rope_s2, sinks, name):
    S = p.shape[0]
    nb = S // BLK
    kvb = Q_END // KV_W

    def body(sink_ref, q_ref, kvc_ref, kvp_ref, do_ref, cc_ref, s1c_ref, s2c_ref, cp_ref, s1p_ref, s2p_ref,
             dq_ref, dkv_ref, dsink_ref, carry, dq_scr, dkv_scr):
        n = pl.program_id(0)

        @pl.when(n == 0)
        def _():
            carry[...] = jnp.zeros_like(carry)
            dsink_ref[...] = jnp.zeros_like(dsink_ref)

        cp, s1p, s2p = cp_ref[...], s1p_ref[...], s2p_ref[...]
        cp2, s1p2, s2p2 = jnp.tile(cp, (1, 2)), jnp.tile(s1p, (1, 2)), jnp.tile(s2p, (1, 2))

        @pl.when(n < nb)
        def _():
            cc, s1c, s2c = cc_ref[...], s1c_ref[...], s2c_ref[...]
            cc8, s1c8, s2c8 = jnp.tile(cc, (1, 8)), jnp.tile(s1c, (1, 8)), jnp.tile(s2c, (1, 8))
            q = _rope(q_ref[...], cc8, s1c8, s2c8).astype(bf16)
            kc = _rope(kvc_ref[:, :256], jnp.tile(cc, (1, 2)), jnp.tile(s1c, (1, 2)), jnp.tile(s2c, (1, 2)))
            kp = _rope(kvp_ref[:, :256], cp2, s1p2, s2p2)
            k = jnp.concatenate([kp, kc], axis=0).astype(bf16)
            v = jnp.concatenate([kvp_ref[:, 256:], kvc_ref[:, 256:]], axis=0).astype(bf16)
            do = do_ref[...].astype(bf16)
            valid = _attn_mask(n)
            lane = lax.broadcasted_iota(jnp.int32, (1, LANE), 1)
            dsink = jnp.zeros((1, LANE), f32)
            for h in range(N_KV_HEADS):
                kh = k[:, h * HEAD_DIM:(h + 1) * HEAD_DIM]
                vh = v[:, h * HEAD_DIM:(h + 1) * HEAD_DIM]
                dkh = jnp.zeros((2 * BLK, HEAD_DIM), f32)
                dvh = jnp.zeros((2 * BLK, HEAD_DIM), f32)
                for g in range(GQ):
                    hq = h * GQ + g
                    qh = q[:, hq * HEAD_DIM:(hq + 1) * HEAD_DIM]
                    doh = do[:, hq * HEAD_DIM:(hq + 1) * HEAD_DIM]
                    s = _dot(qh, kh, NT) * (HEAD_DIM ** -0.5)
                    pr, ps = _softmax_sink(s, valid, sink_ref[hq])
                    dpr = _dot(doh, vh, NT)
                    dvh = dvh + _dot(pr.astype(bf16), doh, TN)
                    row = jnp.sum(pr * dpr, axis=-1, keepdims=True)
                    ds = (pr * (dpr - row) * (HEAD_DIM ** -0.5)).astype(bf16)
                    dsink = dsink + jnp.where(lane == hq, -jnp.sum(ps * row, axis=0, keepdims=True), 0.0)
                    dq_scr[:, hq * HEAD_DIM:(hq + 1) * HEAD_DIM] = _dot(ds, kh)
                    dkh = dkh + _dot(ds, qh, TN)
                dkv_scr[:, h * HEAD_DIM:(h + 1) * HEAD_DIM] = dkh
                dkv_scr[:, 256 + h * HEAD_DIM:256 + (h + 1) * HEAD_DIM] = dvh
            dsink_ref[...] += dsink
            dq_ref[...] = _rope_t(dq_scr[...], cc8, s1c8, s2c8).astype(bf16)

        prev = carry[...]

        @pl.when(n < nb)
        def _():
            dkv_scr[pl.ds(0, BLK), :] = dkv_scr[pl.ds(0, BLK), :] + prev

        @pl.when(n == nb)
        def _():
            dkv_scr[pl.ds(0, BLK), :] = prev

        done = dkv_scr[pl.ds(0, BLK), :]
        dkv_ref[:, :256] = _rope_t(done[:, :256], cp2, s1p2, s2p2).astype(bf16)
        dkv_ref[:, 256:] = done[:, 256:].astype(bf16)

        @pl.when(n < nb)
        def _():
            carry[...] = dkv_scr[pl.ds(BLK, BLK), :]

    cur = lambda n: jnp.minimum(n, nb - 1)
    prv = lambda n: jnp.maximum(n - 1, 0)
    tab_c = pl.BlockSpec((BLK, LANE), lambda n: (cur(n), 0))
    tab_p = pl.BlockSpec((BLK, LANE), lambda n: (prv(n), 0))
    return pl.pallas_call(
        body, name=name, grid=(nb + 1,),
        in_specs=[pl.BlockSpec(memory_space=pltpu.SMEM),
                  pl.BlockSpec((BLK, Q_END), lambda n: (cur(n), 0)),
                  pl.BlockSpec((BLK, KV_W), lambda n: (cur(n), kvb)),
                  pl.BlockSpec((BLK, KV_W), lambda n: (prv(n), kvb)),
                  pl.BlockSpec((BLK, Q_END), lambda n: (cur(n), 0)),
                  tab_c, tab_c, tab_c, tab_p, tab_p, tab_p],
        out_specs=[pl.BlockSpec((BLK, Q_END), lambda n: (cur(n), 0)),
                   pl.BlockSpec((BLK, KV_W), lambda n: (prv(n), 0)),
                   pl.BlockSpec((1, LANE), lambda n: (0, 0))],
        out_shape=[jax.ShapeDtypeStruct((S, Q_END), bf16), jax.ShapeDtypeStruct((S, KV_W), bf16),
                   jax.ShapeDtypeStruct((1, LANE), f32)],
        scratch_shapes=[pltpu.VMEM((BLK, KV_W), f32), pltpu.VMEM((BLK, Q_END), f32), pltpu.VMEM((2 * BLK, KV_W), f32)],
        compiler_params=_params("arbitrary"),
    )(sinks, p, p, p, dcat, rope_c, rope_s1, rope_s2, rope_c, rope_s1, rope_s2)


A1_BLK = V_END // CONV_CH
A2_BLK = A1_BLK + 1


def _ln_stats(y):
    mu = jnp.mean(y, axis=-1, keepdims=True)
    xc = y - mu
    rstd = lax.rsqrt(jnp.mean(xc * xc, axis=-1, keepdims=True) + NORM_EPS)
    return xc * rstd, rstd


def conv_fwd(p, w, b, lng, lnb, name):
    S = p.shape[0]
    T = _tile(S, 256)
    r = T // HALO

    def body(a1_ref, a2_ref, h1_ref, h2_ref, w_ref, b_ref, g_ref, bb_ref, o_ref, y_ref, scr):
        i = pl.program_id(0)
        halo = h1_ref[...] * _sig(h2_ref[...])
        scr[pl.ds(0, HALO), :] = jnp.where(i > 0, halo, 0.0)
        scr[pl.ds(HALO, T), :] = a1_ref[...] * _sig(a2_ref[...])
        acc = jnp.zeros((T, CONV_CH), f32) + b_ref[...]
        for j in range(CONV_W):
            acc = acc + scr[pl.ds(HALO - (CONV_W - 1) + j, T), :] * w_ref[j:j + 1, :]
        y_ref[...] = acc
        yh, _ = _ln_stats(acc)
        z = yh * g_ref[...] + bb_ref[...]
        o_ref[...] = (z * _sig(z)).astype(bf16)

    vec = pl.BlockSpec((1, CONV_CH), lambda i: (0, 0))
    halo_map = lambda i: jnp.maximum(i * r - 1, 0)
    return pl.pallas_call(
        body, name=name, grid=(S // T,),
        in_specs=[pl.BlockSpec((T, CONV_CH), lambda i: (i, A1_BLK)), pl.BlockSpec((T, CONV_CH), lambda i: (i, A2_BLK)),
                  pl.BlockSpec((HALO, CONV_CH), lambda i: (halo_map(i), A1_BLK)),
                  pl.BlockSpec((HALO, CONV_CH), lambda i: (halo_map(i), A2_BLK)),
                  pl.BlockSpec((HALO, CONV_CH), lambda i: (0, 0)), vec, vec, vec],
        out_specs=[pl.BlockSpec((T, CONV_CH), lambda i: (i, 0)), pl.BlockSpec((T, CONV_CH), lambda i: (i, 0))],
        out_shape=[jax.ShapeDtypeStruct((S, CONV_CH), bf16), jax.ShapeDtypeStruct((S, CONV_CH), f32)],
        scratch_shapes=[pltpu.VMEM((T + HALO, CONV_CH), f32)],
        compiler_params=_params("parallel"),
    )(p, p, p, p, w, b, lng, lnb)


def conv_bwd(p, y, dcat, w, lng, lnb, name):
    S = p.shape[0]
    T = _tile(S, 256)
    n = S // T
    r = T // HALO
    dcb = Q_END // CONV_CH

    def body(a1_ref, a2_ref, h1_ref, h2_ref, y_ref, yn_ref, do_ref, don_ref, w_ref, g_ref, bb_ref,
             da_ref, dw_ref, db_ref, dg_ref, dbb_ref, scr_h, scr_dy, acc_b, acc_g, acc_bb):
        i = pl.program_id(0)

        @pl.when(i == 0)
        def _():
            dw_ref[...] = jnp.zeros_like(dw_ref)
            acc_b[...] = jnp.zeros_like(acc_b)
            acc_g[...] = jnp.zeros_like(acc_g)
            acc_bb[...] = jnp.zeros_like(acc_bb)

        gv, bv = g_ref[...], bb_ref[...]

        def ln_silu_bwd(yv, dout):
            yh, rstd = _ln_stats(yv)
            z = yh * gv + bv
            sg = _sig(z)
            dz = dout * (sg * (1.0 + z * (1.0 - sg)))
            gz = dz * gv
            dy = rstd * (gz - jnp.mean(gz, axis=-1, keepdims=True) - yh * jnp.mean(gz * yh, axis=-1, keepdims=True))
            return dy, dz, yh

        dy, dz, yh = ln_silu_bwd(y_ref[...], do_ref[...])
        dyn, _, _ = ln_silu_bwd(yn_ref[...], don_ref[...])
        acc_g[...] += _rowsum8(dz * yh)
        acc_bb[...] += _rowsum8(dz)
        acc_b[...] += _rowsum8(dy)
        scr_dy[pl.ds(0, T), :] = dy
        scr_dy[pl.ds(T, HALO), :] = jnp.where(i < n - 1, dyn, 0.0)
        a1, a2 = a1_ref[...], a2_ref[...]
        sg2 = _sig(a2)
        halo = h1_ref[...] * _sig(h2_ref[...])
        scr_h[pl.ds(0, HALO), :] = jnp.where(i > 0, halo, 0.0)
        scr_h[pl.ds(HALO, T), :] = a1 * sg2
        dh = jnp.zeros((T, CONV_CH), f32)
        for j in range(CONV_W):
            dh = dh + scr_dy[pl.ds(CONV_W - 1 - j, T), :] * w_ref[j:j + 1, :]
            dw_ref[j:j + 1, :] += jnp.sum(dy * scr_h[pl.ds(HALO - (CONV_W - 1) + j, T), :], axis=0, keepdims=True)
        da_ref[:, :CONV_CH] = (dh * sg2).astype(bf16)
        da_ref[:, CONV_CH:] = (dh * a1 * sg2 * (1.0 - sg2)).astype(bf16)

        @pl.when(i == n - 1)
        def _():
            db_ref[...] = jnp.sum(acc_b[...], axis=0, keepdims=True)
            dg_ref[...] = jnp.sum(acc_g[...], axis=0, keepdims=True)
            dbb_ref[...] = jnp.sum(acc_bb[...], axis=0, keepdims=True)

    vec = pl.BlockSpec((1, CONV_CH), lambda i: (0, 0))
    tap = pl.BlockSpec((HALO, CONV_CH), lambda i: (0, 0))
    prev_map = lambda i: jnp.maximum(i * r - 1, 0)
    next_map = lambda i: jnp.minimum((i + 1) * r, S // HALO - 1)
    return pl.pallas_call(
        body, name=name, grid=(n,),
        in_specs=[pl.BlockSpec((T, CONV_CH), lambda i: (i, A1_BLK)), pl.BlockSpec((T, CONV_CH), lambda i: (i, A2_BLK)),
                  pl.BlockSpec((HALO, CONV_CH), lambda i: (prev_map(i), A1_BLK)),
                  pl.BlockSpec((HALO, CONV_CH), lambda i: (prev_map(i), A2_BLK)),
                  pl.BlockSpec((T, CONV_CH), lambda i: (i, 0)),
                  pl.BlockSpec((HALO, CONV_CH), lambda i: (next_map(i), 0)),
                  pl.BlockSpec((T, CONV_CH), lambda i: (i, dcb)),
                  pl.BlockSpec((HALO, CONV_CH), lambda i: (next_map(i), dcb)),
                  tap, vec, vec],
        out_specs=[pl.BlockSpec((T, 2 * CONV_CH), lambda i: (i, 0)), tap, vec, vec, vec],
        out_shape=[jax.ShapeDtypeStruct((S, 2 * CONV_CH), bf16), jax.ShapeDtypeStruct((HALO, CONV_CH), f32),
                   jax.ShapeDtypeStruct((1, CONV_CH), f32), jax.ShapeDtypeStruct((1, CONV_CH), f32),
                   jax.ShapeDtypeStruct((1, CONV_CH), f32)],
        scratch_shapes=[pltpu.VMEM((T + HALO, CONV_CH), f32), pltpu.VMEM((T + HALO, CONV_CH), f32),
                        pltpu.VMEM((8, CONV_CH), f32), pltpu.VMEM((8, CONV_CH), f32), pltpu.VMEM((8, CONV_CH), f32)],
        compiler_params=_params("arbitrary"),
    )(p, p, p, p, y, y, dcat, dcat, w, lng, lnb)


U_BLK = (V_END + 2 * CONV_CH) // SGU_CH
SV_BLK = U_BLK + 1


def _tril(w, transposed=False):
    row = lax.broadcasted_iota(jnp.int32, (BLK, BLK), 0)
    col = lax.broadcasted_iota(jnp.int32, (BLK, BLK), 1)
    keep = (col >= row) if transposed else (row >= col)
    return jnp.where(keep, w, 0.0)


def sgu_fwd(p, lng, lnb, w, bias, name):
    S = p.shape[0]
    T = _tile(S, 256)

    def body(u_ref, v_ref, g_ref, bb_ref, w_ref, bias_ref, o_ref):
        yh, _ = _ln_stats(v_ref[...])
        v = (yh * g_ref[...] + bb_ref[...]).astype(bf16)
        low = lax.broadcasted_iota(jnp.int32, (BLK, LANE), 1) < HEAD_DIM
        for pr in range(SGU_HEADS // 2):
            lanes = pl.ds(pr * LANE, LANE)
            w0 = _tril(w_ref[2 * pr]).astype(bf16)
            w1 = _tril(w_ref[2 * pr + 1]).astype(bf16)
            for c in range(T // BLK):
                rows = pl.ds(c * BLK, BLK)
                vp = v[c * BLK:(c + 1) * BLK, pr * LANE:(pr + 1) * LANE]
                mixed = jnp.where(low, _dot(w0, vp), _dot(w1, vp)) + bias_ref[:, lanes]
                o_ref[rows, lanes] = (u_ref[rows, lanes] * mixed).astype(bf16)

    vec = pl.BlockSpec((1, SGU_CH), lambda i: (0, 0))
    return pl.pallas_call(
        body, name=name, grid=(S // T,),
        in_specs=[pl.BlockSpec((T, SGU_CH), lambda i: (i, U_BLK)), pl.BlockSpec((T, SGU_CH), lambda i: (i, SV_BLK)),
                  vec, vec, pl.BlockSpec((SGU_HEADS, BLK, BLK), lambda i: (0, 0, 0)),
                  pl.BlockSpec((BLK, SGU_CH), lambda i: (0, 0))],
        out_specs=pl.BlockSpec((T, SGU_CH), lambda i: (i, 0)),
        out_shape=jax.ShapeDtypeStruct((S, SGU_CH), bf16),
        compiler_params=_params("parallel"),
    )(p, p, lng, lnb, w, bias)


def sgu_bwd(p, dcat, lng, lnb, w, wt, bias, name):
    S = p.shape[0]
    T = _tile(S, 256)
    n = S // T
    dsb = (Q_END + CONV_CH) // SGU_CH

    def body(u_ref, v_ref, do_ref, g_ref, bb_ref, w_ref, wt_ref, bias_ref,
             da_ref, dw_ref, db_ref, dg_ref, dbb_ref, dv_scr, acc_bias, acc_g, acc_bb):
        i = pl.program_id(0)

        @pl.when(i == 0)
        def _():
            dw_ref[...] = jnp.zeros_like(dw_ref)
            acc_bias[...] = jnp.zeros_like(acc_bias)
            acc_g[...] = jnp.zeros_like(acc_g)
            acc_bb[...] = jnp.zeros_like(acc_bb)

        gv = g_ref[...]
        yh, rstd = _ln_stats(v_ref[...])
        v = (yh * gv + bb_ref[...]).astype(bf16)
        low = lax.broadcasted_iota(jnp.int32, (BLK, LANE), 1) < HEAD_DIM
        for pr in range(SGU_HEADS // 2):
            lanes = pl.ds(pr * LANE, LANE)
            w0 = _tril(w_ref[2 * pr]).astype(bf16)
            w1 = _tril(w_ref[2 * pr + 1]).astype(bf16)
            wt0 = _tril(wt_ref[2 * pr], True).astype(bf16)
            wt1 = _tril(wt_ref[2 * pr + 1], True).astype(bf16)
            dw0 = jnp.zeros((BLK, BLK), f32)
            dw1 = jnp.zeros((BLK, BLK), f32)
            for c in range(T // BLK):
                rows = pl.ds(c * BLK, BLK)
                vp = v[c * BLK:(c + 1) * BLK, pr * LANE:(pr + 1) * LANE]
                mixed = jnp.where(low, _dot(w0, vp), _dot(w1, vp)) + bias_ref[:, lanes]
                do = do_ref[rows, lanes]
                da_ref[rows, lanes] = (do * mixed).astype(bf16)
                dm = do * u_ref[rows, lanes]
                acc_bias[:, lanes] += dm
                dmb = dm.astype(bf16)
                dv_scr[rows, lanes] = jnp.where(low, _dot(wt0, dmb), _dot(wt1, dmb))
                zero = jnp.zeros_like(dmb)
                dw0 = dw0 + _dot(jnp.where(low, dmb, zero), vp, NT)
                dw1 = dw1 + _dot(jnp.where(low, zero, dmb), vp, NT)
            dw_ref[2 * pr] += _tril(dw0)
            dw_ref[2 * pr + 1] += _tril(dw1)
        dv = dv_scr[...]
        acc_g[...] += _rowsum8(dv * yh)
        acc_bb[...] += _rowsum8(dv)
        gz = dv * gv
        dvr = rstd * (gz - jnp.mean(gz, axis=-1, keepdims=True) - yh * jnp.mean(gz * yh, axis=-1, keepdims=True))
        da_ref[:, SGU_CH:] = dvr.astype(bf16)

        @pl.when(i == n - 1)
        def _():
            ch = lax.broadcasted_iota(jnp.int32, (SGU_CH, LANE), 0) // HEAD_DIM
            hd = lax.broadcasted_iota(jnp.int32, (SGU_CH, LANE), 1)
            fold = jnp.where(ch == hd, 1.0, 0.0).astype(f32)
            db_ref[...] = jnp.dot(acc_bias[...], fold, preferred_element_type=f32, precision=lax.Precision.HIGHEST)
            dg_ref[...] = jnp.sum(acc_g[...], axis=0, keepdims=True)
            dbb_ref[...] = jnp.sum(acc_bb[...], axis=0, keepdims=True)

    vec = pl.BlockSpec((1, SGU_CH), lambda i: (0, 0))
    wsp = pl.BlockSpec((SGU_HEADS, BLK, BLK), lambda i: (0, 0, 0))
    return pl.pallas_call(
        body, name=name, grid=(n,),
        in_specs=[pl.BlockSpec((T, SGU_CH), lambda i: (i, U_BLK)), pl.BlockSpec((T, SGU_CH), lambda i: (i, SV_BLK)),
                  pl.BlockSpec((T, SGU_CH), lambda i: (i, dsb)), vec, vec, wsp, wsp,
                  pl.BlockSpec((BLK, SGU_CH), lambda i: (0, 0))],
        out_specs=[pl.BlockSpec((T, 2 * SGU_CH), lambda i: (i, 0)), wsp,
                   pl.BlockSpec((BLK, LANE), lambda i: (0, 0)), vec, vec],
        out_shape=[jax.ShapeDtypeStruct((S, 2 * SGU_CH), bf16), jax.ShapeDtypeStruct((SGU_HEADS, BLK, BLK), f32),
                   jax.ShapeDtypeStruct((BLK, LANE), f32), jax.ShapeDtypeStruct((1, SGU_CH), f32),
                   jax.ShapeDtypeStruct((1, SGU_CH), f32)],
        scratch_shapes=[pltpu.VMEM((T, SGU_CH), f32), pltpu.VMEM((BLK, SGU_CH), f32),
                        pltpu.VMEM((8, SGU_CH), f32), pltpu.VMEM((8, SGU_CH), f32)],
        compiler_params=_params("arbitrary"),
    )(p, p, dcat, lng, lnb, w, wt, bias)


HBM = pl.BlockSpec(memory_space=pltpu.HBM)
SEM = pl.BlockSpec(memory_space=pltpu.SEMAPHORE)
ANY = pl.BlockSpec(memory_space=pl.ANY)
EFFECT = pltpu.SideEffectType.DATAFLOW_SIDE_EFFECTING


def _flip(x, y, c, k):
    px, py, pc = x ^ (k >> 2), y ^ ((k >> 1) & 1), c ^ (k & 1)
    return (px, py, pc), 4 * px + 2 * py + pc


def _routes_gather(x, y, c):
    me = 4 * x + 2 * y + c
    out = []
    for k in (1, 2, 4, 6):
        dev, idx = _flip(x, y, c, k)
        out.append((dev, None, me, idx))
    return out


def _routes_pair(x, y, c):
    dev, _ = _flip(x, y, c, 1)
    return [(dev, 2 * q + (1 - c), q, q) for q in range(N_DEV // 2)]


def _routes_chips(x, y, c):
    out = []
    for k in (2, 4, 6):
        dev, idx = _flip(x, y, c, k)
        out.append((dev, idx // 2, 2 * x + y, idx // 2))
    return out


def _routes_forward(x, y, c):
    sib, _ = _flip(x, y, c, 1)
    out = []
    for k in (2, 4, 6):
        _, idx = _flip(x, y, c, k)
        out.append((sib, idx, idx, idx ^ 1))
    return out


def _routes_all(x, y, c):
    me = 4 * x + 2 * y + c
    out = []
    for k in range(1, N_DEV):
        dev, idx = _flip(x, y, c, k)
        out.append((dev, None, me, idx))
    return out


def _slot(ref, slot, kind):
    if slot is None:
        return ref
    if kind == "cols":
        width = ref.shape[2] // (N_DEV // 2)
        return ref.at[slot // (N_DEV // 2), :, pl.ds(pl.multiple_of((slot % (N_DEV // 2)) * width, LANE), width)]
    return ref.at[slot]


def _copies(routes, srcs, lands, send_sems, recv_sems, incoming, src_kinds, land_kinds):
    x, y, c = lax.axis_index("x"), lax.axis_index("y"), lax.axis_index("c")
    out = []
    n = len(lands)
    if srcs is None:
        srcs, src_kinds = lands, land_kinds
    for k, (dev, src_slot, dst_slot, recv_slot) in enumerate(routes(x, y, c)):
        for a in range(n):
            out.append(pltpu.make_async_remote_copy(
                src_ref=_slot(srcs[a], src_slot, src_kinds[a]),
                dst_ref=_slot(lands[a], recv_slot if incoming else dst_slot, land_kinds[a]),
                send_sem=send_sems.at[k * n + a], recv_sem=recv_sems.at[k * n + a], device_id=dev, device_id_type=MESH))
    return out


def _pin(a):
    return pltpu.with_memory_space_constraint(a, pltpu.HBM)


def split_start(srcs, lands, routes, name, deps=(), src_kinds=None, land_kinds=None):
    n = len(lands)
    ns = 0 if srcs is None else n
    n_routes = len(routes(0, 0, 0))
    ops = ([] if srcs is None else list(srcs)) + list(lands)
    src_kinds = src_kinds or ["rows"] * n
    land_kinds = land_kinds or ["rows"] * n

    def body(*refs):
        src, land = (refs[:n] if ns else None), refs[ns:ns + n]
        first_out = ns + n + len(deps)
        send_sems, recv_sems, token = refs[first_out], refs[first_out + 1], refs[-1]
        for cp in _copies(routes, src, land, send_sems, recv_sems, False, src_kinds, land_kinds):
            cp.start()
        token[...] = jnp.zeros_like(token)

    thru = [pltpu.HBM(a.shape, a.dtype) for a in ops]
    res = pl.pallas_call(
        body, name=name,
        out_shape=(pltpu.SemaphoreType.DMA((n * n_routes,)), pltpu.SemaphoreType.DMA((n * n_routes,)), *thru,
                   jax.ShapeDtypeStruct((8, LANE), f32)),
        in_specs=[HBM] * len(ops) + [ANY] * len(deps),
        out_specs=(SEM, SEM, *([HBM] * len(ops)), pl.BlockSpec(memory_space=pltpu.VMEM)),
        input_output_aliases={i: 2 + i for i in range(len(ops))},
        compiler_params=pltpu.CompilerParams(has_side_effects=EFFECT),
    )(*[_pin(a) for a in ops], *deps)
    return (res[0], res[1], (list(res[2:2 + n]) if ns else None), list(res[2 + ns:2 + ns + n]), res[-1],
            (src_kinds, land_kinds))


def split_wait(started, after, routes, name):
    send_sems, recv_sems, srcs, lands, _, (src_kinds, land_kinds) = started
    n = len(lands)
    ns = 0 if srcs is None else n
    ops = ([] if srcs is None else list(srcs)) + list(lands)

    def body(*refs):
        src, land = (refs[:n] if ns else None), refs[ns:ns + n]
        send_s, recv_s = refs[ns + n], refs[ns + n + 1]
        for cp in _copies(routes, src, land, send_s, recv_s, True, src_kinds, land_kinds):
            cp.wait_send()
            cp.wait_recv()

    thru = [pltpu.HBM(a.shape, a.dtype) for a in ops]
    res = pl.pallas_call(
        body, name=name, out_shape=tuple(thru),
        in_specs=[HBM] * len(ops) + [SEM, SEM, ANY], out_specs=tuple([HBM] * len(ops)),
        input_output_aliases={i: i for i in range(len(ops))},
        compiler_params=pltpu.CompilerParams(has_side_effects=EFFECT),
    )(*ops, send_sems, recv_sems, after)
    return (list(res[:n]) if ns else None), list(res[ns:ns + n])


def chip_sum(parts, land, c_idx, kind, name):
    _, R, C = land.shape
    tr = _tile(R, 256)
    half = N_DEV // 2

    def body(c_ref, p_ref, l_ref, o_ref):
        o_ref[...] = (p_ref[...].astype(f32) + l_ref[...].astype(f32)).astype(bf16)

    if kind == "cols":
        mine = lambda q, i, c_ref: ((2 * q + c_ref[0]) // half, i, (2 * q + c_ref[0]) % half)
    else:
        mine = lambda q, i, c_ref: (2 * q + c_ref[0], i, 0)
    return pl.pallas_call(
        body, name=name,
        grid_spec=pltpu.PrefetchScalarGridSpec(
            num_scalar_prefetch=1, grid=(half, R // tr),
            in_specs=[pl.BlockSpec((None, tr, C), mine), pl.BlockSpec((None, tr, C), lambda q, i, c_ref: (q, i, 0))],
            out_specs=pl.BlockSpec((None, tr, C), lambda q, i, c_ref: (q, i, 0))),
        out_shape=jax.ShapeDtypeStruct((half, R, C), bf16),
        compiler_params=_params("parallel", "parallel"),
    )(c_idx, parts, land)


def sum_slots(parts, name):
    P, R, C = parts.shape
    tr = _tile(R, 512)

    def body(p_ref, o_ref):
        total = p_ref[0]
        for j in range(1, P):
            total = total + p_ref[j]
        o_ref[...] = total

    return pl.pallas_call(
        body, name=name, grid=(R // tr,),
        in_specs=[pl.BlockSpec((P, tr, C), lambda i: (0, i, 0))],
        out_specs=pl.BlockSpec((tr, C), lambda i: (i, 0)),
        out_shape=jax.ShapeDtypeStruct((R, C), f32),
        compiler_params=_params("parallel"),
    )(parts)


def adamw(parts, owns, chip, w, m, v, name):
    L, R, C = w.shape
    P = parts[0].shape[0]
    tr = _tile(R, 128 if C > 1024 else 256)
    nr = R // tr
    c1 = 1.0 - ADAM_B1 ** ADAM_STEP
    c2 = 1.0 - ADAM_B2 ** ADAM_STEP
    n_own = L if owns is not None else 0

    def body(chip_ref, *refs):
        part_refs, own_refs = refs[:L], refs[L:L + n_own]
        w_ref, m_ref, v_ref, g_out, d_out, m_out, v_out = refs[L + n_own:]
        layer = pl.program_id(0)
        for l in range(L):
            @pl.when(layer == l)
            def _(l=l):
                g = None
                for q in range(P):
                    term = part_refs[l][q].astype(f32)
                    if n_own:
                        term = jnp.where(chip_ref[0] == q, own_refs[l][...].astype(f32), term)
                    g = term if g is None else g + term
                mn = ADAM_B1 * m_ref[...] + (1.0 - ADAM_B1) * g
                vn = ADAM_B2 * v_ref[...] + (1.0 - ADAM_B2) * (g * g)
                g_out[...] = g
                m_out[...] = mn
                v_out[...] = vn
                d_out[...] = -ADAM_LR * ((mn / c1) / (jnp.sqrt(vn / c2) + ADAM_EPS) + ADAM_WD * w_ref[...])

    def rows(l, a, i):
        return jnp.where(a == l, i, jnp.where(a < l, 0, nr - 1))

    def part_spec(l):
        return pl.BlockSpec((P, tr, C), lambda a, i, chip_ref: (0, rows(l, a, i), 0))

    def own_spec(l):
        return pl.BlockSpec((None, tr, C), lambda a, i, chip_ref: (chip_ref[0], rows(l, a, i), 0))

    slab = pl.BlockSpec((None, tr, C), lambda a, i, chip_ref: (a, i, 0))
    out = jax.ShapeDtypeStruct((L, R, C), f32)
    return pl.pallas_call(
        body, name=name,
        grid_spec=pltpu.PrefetchScalarGridSpec(
            num_scalar_prefetch=1, grid=(L, nr),
            in_specs=[part_spec(l) for l in range(L)] + [own_spec(l) for l in range(n_own)] + [slab, slab, slab],
            out_specs=[slab, slab, slab, slab]),
        out_shape=[out, out, out, out],
        compiler_params=_params("arbitrary", "arbitrary"),
    )(chip, *parts, *(owns or []), w, m, v)


PACK = 8 * LANE


def _pack(arrs):
    pieces = []
    for a in arrs:
        flat = a.astype(f32).reshape(-1)
        pad = (-flat.shape[0]) % PACK
        pieces.append(jnp.pad(flat, (0, pad)).reshape(-1, LANE))
    return jnp.concatenate(pieces, axis=0)


def _unpack(buf, shapes):
    out, row = [], 0
    for shp in shapes:
        size = math.prod(shp)
        rows = (size + PACK - 1) // PACK * (PACK // LANE)
        out.append(buf[row:row + rows].reshape(-1)[:size].reshape(shp))
        row += rows
    return out


def _rope_tables(positions):
    half = ROT_DIM // 2
    inv_freq = 1.0 / (ROPE_THETA ** (jnp.arange(0, ROT_DIM, 2, dtype=f32) / ROT_DIM))
    ang = positions.astype(f32)[:, None] * inv_freq
    cos, sin = jnp.cos(ang), jnp.sin(ang)
    S = positions.shape[0]
    zeros, ones = jnp.zeros((S, half), f32), jnp.ones((S, HEAD_DIM - ROT_DIM), f32)
    rest = jnp.zeros((S, HEAD_DIM - ROT_DIM), f32)
    c = jnp.concatenate([cos, cos, ones], axis=1)
    s1 = jnp.concatenate([zeros, sin, rest], axis=1)
    s2 = jnp.concatenate([-sin, zeros, rest], axis=1)
    return tuple(jnp.tile(t, (1, LANE // HEAD_DIM)) for t in (c, s1, s2))


def _cols_to_shards(g):
    lead, (R, N) = g.shape[:-2], g.shape[-2:]
    g = g.reshape(lead + (R, N_DEV, N // N_DEV))
    return jnp.moveaxis(g, -2, 0)


def _shards_to_cols(g):
    g = jnp.moveaxis(g, 0, -2)
    return g.reshape(g.shape[:-2] + (g.shape[-2] * g.shape[-1],))


def kernel(x, positions, norm_ffn1, ffn1_w_in, ffn1_w_out, norm_mix, w_in, conv_dw_w, conv_dw_b, conv_ln_g, conv_ln_b, sgu_ln_g, sgu_ln_b, sgu_w, sgu_b, attn_sinks, w_out, norm_ffn2, ffn2_w_in, ffn2_w_out, final_norm, loss_target, m_norm_ffn1, m_ffn1_w_in, m_ffn1_w_out, m_norm_mix, m_w_in, m_conv_dw_w, m_conv_dw_b, m_conv_ln_g, m_conv_ln_b, m_sgu_ln_g, m_sgu_ln_b, m_sgu_w, m_sgu_b, m_attn_sinks, m_w_out, m_norm_ffn2, m_ffn2_w_in, m_ffn2_w_out, m_final_norm, v_norm_ffn1, v_ffn1_w_in, v_ffn1_w_out, v_norm_mix, v_w_in, v_conv_dw_w, v_conv_dw_b, v_conv_ln_g, v_conv_ln_b, v_sgu_ln_g, v_sgu_ln_b, v_sgu_w, v_sgu_b, v_attn_sinks, v_w_out, v_norm_ffn2, v_ffn2_w_in, v_ffn2_w_out, v_final_norm):
    L = norm_ffn1.shape[0]
    S, D = x.shape[1], x.shape[2]
    F = ffn1_w_out.shape[1] * N_DEV
    me = 4 * lax.axis_index("x") + 2 * lax.axis_index("y") + lax.axis_index("c")
    x0 = x[0]
    rope_c, rope_s1, rope_s2 = _rope_tables(positions[0])
    cw = CONV_CH // N_DEV

    c_idx = lax.axis_index("c").astype(jnp.int32).reshape(1)
    chip = (2 * lax.axis_index("x") + lax.axis_index("y")).astype(jnp.int32).reshape(1)
    no_chip = jnp.zeros((1,), jnp.int32)

    row = lambda a: a.reshape(1, -1)
    groups = ("ffn1", "mix", "ffn2")
    order_fwd = [(l, g) for l in range(L) for g in groups]

    def group_srcs(l, grp):
        if grp == "ffn1":
            return [ffn1_w_in[l].astype(bf16), ffn1_w_out[l].astype(bf16)]
        if grp == "ffn2":
            return [ffn2_w_in[l].astype(bf16), ffn2_w_out[l].astype(bf16)]
        taps = jnp.pad(conv_dw_w[l], ((0, HALO - CONV_W), (0, LANE - cw)))
        return [w_in[l].astype(bf16), w_out[l].astype(bf16), taps]

    def kinds_of(grp):
        return ["rows"] * 3 if grp == "mix" else ["cols", "rows"]

    def gather_start(k, deps=()):
        l, grp = order_fwd[k]
        srcs = group_srcs(l, grp)
        lands = [lax.empty((2, D, F) if kind == "cols" else (N_DEV,) + a.shape, a.dtype)
                 for a, kind in zip(srcs, kinds_of(grp))]
        return split_start(srcs, lands, _routes_gather, f"gather_start_{grp}_{l}", deps, land_kinds=kinds_of(grp))

    def gather_forward(k, started, after):
        l, grp = order_fwd[k]
        srcs, lands = split_wait(started, after, _routes_gather, f"gather_wait_{grp}_{l}")
        return srcs, split_start(None, lands, _routes_forward, f"forward_start_{grp}_{l}", land_kinds=kinds_of(grp))

    def gather_finish(k, srcs, started, after):
        l, grp = order_fwd[k]
        _, lands = split_wait(started, after, _routes_forward, f"forward_wait_{grp}_{l}")
        half = N_DEV // 2
        full = [lax.dynamic_update_slice(ld, s[None], (me // half, 0, (me % half) * s.shape[1])) if kind == "cols"
                else lax.dynamic_update_index_in_dim(ld, s, me, 0) for ld, s, kind in zip(lands, srcs, kinds_of(grp))]
        if grp == "mix":
            return dict(w_in=_shards_to_cols(full[0]), w_out=full[1].reshape(D, D),
                        taps=_shards_to_cols(full[2][:, :, :cw]))
        return dict(w_in=full[0], w_out=full[1].reshape(F, D))

    def ffn_fwd(xs, wts, g_norm, tag, sv):
        sv["x_in"] = xs
        h, sv["ht"] = rmsnorm_fwd(xs, g_norm, f"norm_{tag}")
        sv["gu"], a, sv["at"] = ffn_in(h, wts["w_in"], f"{tag}_in")
        return mm_res(a, wts["w_out"], xs, FFN_RES, f"{tag}_out")

    def mix_fwd(xs, wts, l, sv):
        sv["x_in"] = xs
        h, sv["ht"] = rmsnorm_fwd(xs, row(norm_mix[l]), f"norm_mix_{l}")
        p = mm_nn(h, wts["w_in"][None], f"mix_in_{l}")[0]
        sv["p"] = p
        attn = attn_fwd(p, rope_c, rope_s1, rope_s2, attn_sinks[l], f"attn_fwd_{l}")
        conv, sv["conv_y"] = conv_fwd(p, wts["taps"], row(conv_dw_b[l]), row(conv_ln_g[l]), row(conv_ln_b[l]),
                                      f"conv_fwd_{l}")
        sv["sgu_bias"] = jnp.repeat(sgu_b[l].T, HEAD_DIM, axis=1)
        sgu = sgu_fwd(p, row(sgu_ln_g[l]), row(sgu_ln_b[l]), sgu_w[l], sv["sgu_bias"], f"sgu_fwd_{l}")
        cat = jnp.concatenate([attn, conv, sgu], axis=1)
        sv["catt"] = cat.T
        return mm_res(cat, wts["w_out"], xs, 1.0, f"mix_out_{l}")

    weights, saved = {}, {}
    xs = x0
    first = gather_start(0)
    srcs, fwd = gather_forward(0, first, first[4])
    for k, (l, grp) in enumerate(order_fwd):
        nxt = gather_start(k + 1, (fwd[4],)) if k + 1 < len(order_fwd) else None
        wts = gather_finish(k, srcs, fwd, nxt[4] if nxt else fwd[4])
        sv = {}
        if grp == "ffn1":
            xs = ffn_fwd(xs, wts, row(norm_ffn1[l]), f"ffn1_{l}", sv)
        elif grp == "ffn2":
            xs = ffn_fwd(xs, wts, row(norm_ffn2[l]), f"ffn2_{l}", sv)
        else:
            xs = mix_fwd(xs, wts, l, sv)
        weights[l, grp], saved[l, grp] = wts, sv
        if nxt:
            srcs, fwd = gather_forward(k + 1, nxt, xs)

    dx, dxb, d_final_norm, loss = final_loss(xs, row(final_norm), loss_target[0], "final_loss")

    def scatter_start(grads, kinds, tag):
        half = N_DEV // 2
        lands = [lax.empty((half, g.shape[1], g.shape[2] // half) if kind == "cols" else (half,) + g.shape[1:], g.dtype)
                 for g, kind in zip(grads, kinds)]
        return split_start(grads, lands, _routes_pair, f"pair_start_{tag}", src_kinds=kinds)

    def pair_to_chips(started, after, tag):
        kinds = started[5][0]
        grads, landed = split_wait(started, after, _routes_pair, f"pair_wait_{tag}")
        sums = [chip_sum(g, ld, c_idx, kind, f"chip_sum_{tag}_{a}")
                for a, (g, ld, kind) in enumerate(zip(grads, landed, kinds))]
        return split_start(sums, [lax.empty(s.shape, s.dtype) for s in sums], _routes_chips, f"chips_start_{tag}")

    def ffn_bwd(dx, dxb, wts, sv, g_norm, tag):
        dgu = ffn_dact(dxb, wts["w_out"], sv["gu"], f"{tag}_dact")
        d_w_out = mm_nn(sv["at"], dxb[None], f"{tag}_dwout", bf16, FFN_RES)[0]
        d_w_in = mm_nn(sv["ht"], dgu, f"{tag}_dwin", bf16)
        pair = scatter_start([d_w_in, d_w_out.reshape(N_DEV, F // N_DEV, D)], ["cols", "rows"], tag)
        dh = mm_nt(dgu, wts["w_in"], f"{tag}_dh", deps=(pair[4],))
        chips = pair_to_chips(pair, dh, tag)
        dx, dxb, dg = rmsnorm_bwd(dh, sv["x_in"], g_norm, dx, f"{tag}_dnorm", deps=(chips[4],))
        return dx, dxb, dg, chips

    small = [None] * L
    chips_pending = {}
    for l in reversed(range(L)):
        dx, dxb, d_norm_ffn2, chips_pending[l, "ffn2"] = ffn_bwd(
            dx, dxb, weights[l, "ffn2"], saved[l, "ffn2"], row(norm_ffn2[l]), f"ffn2_{l}")

        wts, sv = weights[l, "mix"], saved[l, "mix"]
        d_w_out = mm_nn(sv["catt"], dxb[None], f"mix_dwout_{l}", bf16)[0]
        dcat = mm_nt(dxb[None], wts["w_out"][None], f"mix_dcat_{l}", deps=(d_w_out,))
        p = sv["p"]
        dq, dkv, d_sinks = attn_bwd(p, dcat, rope_c, rope_s1, rope_s2, attn_sinks[l], f"attn_bwd_{l}")
        da_conv, d_taps, d_conv_b, d_conv_g, d_conv_bb = conv_bwd(
            p, sv["conv_y"], dcat, wts["taps"], row(conv_ln_g[l]), row(conv_ln_b[l]), f"conv_bwd_{l}")
        da_sgu, d_sgu_w, d_sgu_bias, d_sgu_g, d_sgu_bb = sgu_bwd(
            p, dcat, row(sgu_ln_g[l]), row(sgu_ln_b[l]), sgu_w[l], jnp.swapaxes(sgu_w[l], 1, 2), sv["sgu_bias"],
            f"sgu_bwd_{l}")
        dp = jnp.concatenate([dq, dkv, da_conv, da_sgu], axis=1)
        d_w_in = mm_nn(sv["ht"], dp[None], f"mix_dwin_{l}", bf16)[0]
        pair = scatter_start([_cols_to_shards(d_w_in), d_w_out.reshape(N_DEV, D // N_DEV, D)], ["rows", "rows"],
                             f"mix_{l}")
        dh = mm_nt(dp[None], wts["w_in"][None], f"mix_dh_{l}", deps=(pair[4],))
        chips_pending[l, "mix"] = pair_to_chips(pair, dh, f"mix_{l}")
        dx, dxb, d_norm_mix = rmsnorm_bwd(dh, sv["x_in"], row(norm_mix[l]), dx, f"mix_dnorm_{l}",
                                          deps=(chips_pending[l, "mix"][4],))

        dx, dxb, d_norm_ffn1, chips_pending[l, "ffn1"] = ffn_bwd(
            dx, dxb, weights[l, "ffn1"], saved[l, "ffn1"], row(norm_ffn1[l]), f"ffn1_{l}")

        small[l] = dict(norm_ffn1=d_norm_ffn1[0], norm_mix=d_norm_mix[0], conv_dw_w=d_taps[:CONV_W],
                        conv_dw_b=d_conv_b[0], conv_ln_g=d_conv_g[0], conv_ln_b=d_conv_bb[0], sgu_ln_g=d_sgu_g[0],
                        sgu_ln_b=d_sgu_bb[0], sgu_w=d_sgu_w, sgu_b=d_sgu_bias[:, :SGU_HEADS].T,
                        attn_sinks=d_sinks[0, :N_Q_HEADS], norm_ffn2=d_norm_ffn2[0])

    grad_x = dx[None]

    small_names = ["norm_ffn1", "norm_mix", "conv_dw_w", "conv_dw_b", "conv_ln_g", "conv_ln_b", "sgu_ln_g", "sgu_ln_b",
                   "sgu_w", "sgu_b", "attn_sinks", "norm_ffn2"]
    local = [jnp.stack([small[l][n] for l in range(L)]) for n in small_names] + [d_final_norm[0], loss.reshape(1)]
    shapes = [a.shape for a in local]
    local = _pack(local)
    small_pending = split_start([local], [lax.empty((N_DEV,) + local.shape, f32)], _routes_all, "small_start")

    def landed(grp, after):
        got = [split_wait(chips_pending[l, grp], after, _routes_chips, f"chips_wait_{grp}_{l}") for l in range(L)]
        return [g[0] for g in got], [g[1] for g in got]

    big = {}
    after = dx
    for grp, names in (("ffn2", (("ffn2_w_in", ffn2_w_in, m_ffn2_w_in, v_ffn2_w_in),
                                 ("ffn2_w_out", ffn2_w_out, m_ffn2_w_out, v_ffn2_w_out))),
                       ("mix", (("w_in", w_in, m_w_in, v_w_in), ("w_out", w_out, m_w_out, v_w_out))),
                       ("ffn1", (("ffn1_w_in", ffn1_w_in, m_ffn1_w_in, v_ffn1_w_in),
                                 ("ffn1_w_out", ffn1_w_out, m_ffn1_w_out, v_ffn1_w_out)))):
        sums, lands = landed(grp, after)
        for idx, (name, w, m, v) in enumerate(names):
            big[name] = adamw([lands[l][idx] for l in range(L)], [sums[l][idx] for l in range(L)], chip, w, m, v,
                              f"adamw_{name}")
            after = big[name][0]

    (own,), (others,) = split_wait(small_pending, after, _routes_all, "small_wait")
    total = _unpack(sum_slots(lax.dynamic_update_index_in_dim(others, own, me, 0), "sum_small_grads"), shapes)
    g_small = dict(zip(small_names + ["final_norm", "loss"], total))
    g_small["conv_dw_w"] = lax.dynamic_slice_in_dim(g_small["conv_dw_w"], me * cw, cw, axis=2)
    given = dict(norm_ffn1=(norm_ffn1, m_norm_ffn1, v_norm_ffn1), norm_mix=(norm_mix, m_norm_mix, v_norm_mix),
                 conv_dw_w=(conv_dw_w, m_conv_dw_w, v_conv_dw_w), conv_dw_b=(conv_dw_b, m_conv_dw_b, v_conv_dw_b),
                 conv_ln_g=(conv_ln_g, m_conv_ln_g, v_conv_ln_g), conv_ln_b=(conv_ln_b, m_conv_ln_b, v_conv_ln_b),
                 sgu_ln_g=(sgu_ln_g, m_sgu_ln_g, v_sgu_ln_g), sgu_ln_b=(sgu_ln_b, m_sgu_ln_b, v_sgu_ln_b),
                 sgu_w=(sgu_w, m_sgu_w, v_sgu_w), sgu_b=(sgu_b, m_sgu_b, v_sgu_b),
                 attn_sinks=(attn_sinks, m_attn_sinks, v_attn_sinks), norm_ffn2=(norm_ffn2, m_norm_ffn2, v_norm_ffn2),
                 final_norm=(final_norm, m_final_norm, v_final_norm))
    upd_names = small_names + ["final_norm"]
    upd_shapes = [given[n][0].shape for n in upd_names]
    packed = [_pack([g_small[n] for n in upd_names])[None]] + [_pack([given[n][k] for n in upd_names])[None] for k in range(3)]
    res = adamw([packed[0]], None, no_chip, packed[1], packed[2], packed[3], "adamw_small")
    upd = [dict(zip(upd_names, _unpack(r[0], upd_shapes))) for r in res]

    order = ["norm_ffn1", "ffn1_w_in", "ffn1_w_out", "norm_mix", "w_in", "conv_dw_w", "conv_dw_b", "conv_ln_g",
             "conv_ln_b", "sgu_ln_g", "sgu_ln_b", "sgu_w", "sgu_b", "attn_sinks", "w_out", "norm_ffn2", "ffn2_w_in",
             "ffn2_w_out", "final_norm"]
    outs = [g_small["loss"].reshape(()), grad_x]
    for k in range(4):
        outs += [big[n][k] if n in big else upd[k][n] for n in order]
    return tuple(outs)
```

```python
import functools
import math

import jax
import jax.numpy as jnp
from jax import lax
from jax.experimental import pallas as pl
from jax.experimental.pallas import tpu as pltpu

f32 = jnp.float32
bf16 = jnp.bfloat16

N_DEV = 8
HEAD_DIM = 64
N_Q_HEADS = 16
N_KV_HEADS = 4
GQ = N_Q_HEADS // N_KV_HEADS
BLK = 128
ROT_DIM = 16
ROPE_THETA = 500000.0
CONV_W = 31
CONV_CH = 512
SGU_CH = 512
SGU_HEADS = 8
Q_END = N_Q_HEADS * HEAD_DIM
KV_W = 2 * N_KV_HEADS * HEAD_DIM
V_END = Q_END + KV_W
IN_COLS = V_END + 2 * CONV_CH + 2 * SGU_CH
HALO = 32
NORM_EPS = 1e-5
FFN_RES = 0.5
ADAM_LR, ADAM_B1, ADAM_B2, ADAM_EPS, ADAM_WD, ADAM_STEP = 0.001, 0.9, 0.999, 1e-08, 0.01, 10
LANE = 128
VMEM_LIMIT = 56 * 2 ** 20
MESH = pl.DeviceIdType.MESH

NN = (((1,), (0,)), ((), ()))
NT = (((1,), (1,)), ((), ()))
TN = (((0,), (0,)), ((), ()))


def _tile(dim, pref):
    t = min(pref, dim)
    while dim % t:
        t //= 2
    return t


def _params(*sem):
    return pltpu.CompilerParams(dimension_semantics=sem, vmem_limit_bytes=VMEM_LIMIT)


def _sig(x):
    return 1.0 / (1.0 + jnp.exp(-x))


def _dot(a, b, dims=NN):
    return lax.dot_general(a, b, dims, preferred_element_type=f32)


def _rowsum8(x):
    return x.reshape(x.shape[0] // 8, 8, x.shape[1]).sum(axis=0)


def rmsnorm_fwd(x, g, name, deps=()):
    S, D = x.shape
    tm = _tile(S, 512)

    def body(x_ref, g_ref, *rest):
        o_ref, ot_ref = rest[-2:]
        xv = x_ref[...]
        r = lax.rsqrt(jnp.mean(xv * xv, axis=-1, keepdims=True) + NORM_EPS)
        hb = (xv * r * g_ref[...]).astype(bf16)
        o_ref[...] = hb
        ot_ref[...] = hb.T

    return pl.pallas_call(
        body, name=name, grid=(S // tm,),
        in_specs=[pl.BlockSpec((tm, D), lambda i: (i, 0)), pl.BlockSpec((1, D), lambda i: (0, 0))] + [ANY] * len(deps),
        out_specs=[pl.BlockSpec((tm, D), lambda i: (i, 0)), pl.BlockSpec((D, tm), lambda i: (0, i))],
        out_shape=[jax.ShapeDtypeStruct((S, D), bf16), jax.ShapeDtypeStruct((D, S), bf16)],
        compiler_params=_params("parallel"),
    )(x, g, *deps)


def rmsnorm_bwd(dh, x, g, dres, name, deps=()):
    S, D = x.shape
    tm = _tile(S, 256)
    n = S // tm

    def body(dh_ref, x_ref, g_ref, dres_ref, *rest):
        dx_ref, dxb_ref, dg_ref, acc = rest[-4:]
        i = pl.program_id(0)

        @pl.when(i == 0)
        def _():
            acc[...] = jnp.zeros_like(acc)

        xv = x_ref[...]
        r = lax.rsqrt(jnp.mean(xv * xv, axis=-1, keepdims=True) + NORM_EPS)
        xh = xv * r
        dy = dh_ref[...]
        gy = dy * g_ref[...]
        dx = dres_ref[...] + r * (gy - xh * jnp.mean(gy * xh, axis=-1, keepdims=True))
        dx_ref[...] = dx
        dxb_ref[...] = dx.astype(bf16)
        acc[...] += _rowsum8(dy * xh)

        @pl.when(i == n - 1)
        def _():
            dg_ref[...] = jnp.sum(acc[...], axis=0, keepdims=True)

    row = pl.BlockSpec((tm, D), lambda i: (i, 0))
    vec = pl.BlockSpec((1, D), lambda i: (0, 0))
    return pl.pallas_call(
        body, name=name, grid=(n,),
        in_specs=[row, row, vec, row] + [ANY] * len(deps),
        out_specs=[row, row, vec],
        out_shape=[jax.ShapeDtypeStruct((S, D), f32), jax.ShapeDtypeStruct((S, D), bf16),
                   jax.ShapeDtypeStruct((1, D), f32)],
        scratch_shapes=[pltpu.VMEM((8, D), f32)],
        compiler_params=_params("arbitrary"),
    )(dh, x, g, dres, *deps)


def final_loss(x, g, tgt, name):
    S, D = x.shape
    tm = _tile(S, 256)
    n = S // tm

    def body(x_ref, g_ref, t_ref, dx_ref, dxb_ref, dg_ref, loss_ref, acc):
        i = pl.program_id(0)

        @pl.when(i == 0)
        def _():
            acc[...] = jnp.zeros_like(acc)
            loss_ref[...] = jnp.zeros_like(loss_ref)

        xv = x_ref[...]
        gv = g_ref[...]
        r = lax.rsqrt(jnp.mean(xv * xv, axis=-1, keepdims=True) + NORM_EPS)
        xh = xv * r
        diff = xh * gv - t_ref[...]
        tok = jnp.mean(diff * diff, axis=-1, keepdims=True)
        loss_ref[...] += 0.5 * jnp.sum(tok, axis=0, keepdims=True)
        dy = diff / D
        gy = dy * gv
        dx = r * (gy - xh * jnp.mean(gy * xh, axis=-1, keepdims=True))
        dx_ref[...] = dx
        dxb_ref[...] = dx.astype(bf16)
        acc[...] += _rowsum8(dy * xh)

        @pl.when(i == n - 1)
        def _():
            dg_ref[...] = jnp.sum(acc[...], axis=0, keepdims=True)

    row = pl.BlockSpec((tm, D), lambda i: (i, 0))
    vec = pl.BlockSpec((1, D), lambda i: (0, 0))
    return pl.pallas_call(
        body, name=name, grid=(n,),
        in_specs=[row, vec, row],
        out_specs=[row, row, vec, pl.BlockSpec((1, 1), lambda i: (0, 0))],
        out_shape=[jax.ShapeDtypeStruct((S, D), f32), jax.ShapeDtypeStruct((S, D), bf16),
                   jax.ShapeDtypeStruct((1, D), f32), jax.ShapeDtypeStruct((1, 1), f32)],
        scratch_shapes=[pltpu.VMEM((8, D), f32)],
        compiler_params=_params("arbitrary"),
    )(x, g, tgt)


def ffn_in(h, w2, name):
    S, D = h.shape
    F = w2.shape[2]
    tm, tn = _tile(S, 512), _tile(F, 512)

    def body(h_ref, w_ref, gu_ref, a_ref, at_ref):
        hv = h_ref[...]
        g = _dot(hv, w_ref[0])
        u = _dot(hv, w_ref[1])
        gu_ref[0] = g.astype(bf16)
        gu_ref[1] = u.astype(bf16)
        a = (g * _sig(g) * u).astype(bf16)
        a_ref[...] = a
        at_ref[...] = a.T

    return pl.pallas_call(
        body, name=name, grid=(S // tm, F // tn),
        in_specs=[pl.BlockSpec((tm, D), lambda i, j: (i, 0)), pl.BlockSpec((2, D, tn), lambda i, j: (0, 0, j))],
        out_specs=[pl.BlockSpec((2, tm, tn), lambda i, j: (0, i, j)), pl.BlockSpec((tm, tn), lambda i, j: (i, j)),
                   pl.BlockSpec((tn, tm), lambda i, j: (j, i))],
        out_shape=[jax.ShapeDtypeStruct((2, S, F), bf16), jax.ShapeDtypeStruct((S, F), bf16),
                   jax.ShapeDtypeStruct((F, S), bf16)],
        compiler_params=_params("parallel", "parallel"),
    )(h, w2)


def mm_res(a, w, x, scale, name):
    S, K = a.shape
    N = w.shape[1]
    tm, tn = _tile(S, 512), _tile(N, 512)

    def body(a_ref, w_ref, x_ref, o_ref):
        o_ref[...] = x_ref[...] + scale * _dot(a_ref[...], w_ref[...])

    return pl.pallas_call(
        body, name=name, grid=(S // tm, N // tn),
        in_specs=[pl.BlockSpec((tm, K), lambda i, j: (i, 0)), pl.BlockSpec((K, tn), lambda i, j: (0, j)),
                  pl.BlockSpec((tm, tn), lambda i, j: (i, j))],
        out_specs=pl.BlockSpec((tm, tn), lambda i, j: (i, j)),
        out_shape=jax.ShapeDtypeStruct((S, N), f32),
        compiler_params=_params("parallel", "parallel"),
    )(a, w, x)


def mm_nn(a, b, name, out_dtype=f32, scale=1.0, deps=()):
    M, K = a.shape
    G, _, N = b.shape
    tm, tn = _tile(M, 512), _tile(N, 512)

    def body(a_ref, b_ref, *rest):
        acc = _dot(a_ref[...], b_ref[...])
        rest[-1][...] = (acc if scale == 1.0 else scale * acc).astype(out_dtype)

    return pl.pallas_call(
        body, name=name, grid=(G, M // tm, N // tn),
        in_specs=[pl.BlockSpec((tm, K), lambda g, i, j: (i, 0)),
                  pl.BlockSpec((None, K, tn), lambda g, i, j: (g, 0, j))] + [ANY] * len(deps),
        out_specs=pl.BlockSpec((None, tm, tn), lambda g, i, j: (g, i, j)),
        out_shape=jax.ShapeDtypeStruct((G, M, N), out_dtype),
        compiler_params=_params("parallel", "parallel", "parallel"),
    )(a, b, *deps)


def mm_nt(a, w, name, deps=()):
    G, S, K = a.shape
    N = w.shape[1]
    tm, tn = _tile(S, 512), _tile(N, 512)

    def body(a_ref, w_ref, *rest):
        o_ref = rest[-1]
        part = _dot(a_ref[...], w_ref[...], NT)
        if G == 1:
            o_ref[...] = part
        else:
            g = pl.program_id(2)

            @pl.when(g == 0)
            def _():
                o_ref[...] = part

            @pl.when(g > 0)
            def _():
                o_ref[...] += part

    return pl.pallas_call(
        body, name=name, grid=(S // tm, N // tn, G),
        in_specs=[pl.BlockSpec((None, tm, K), lambda i, j, g: (g, i, 0)),
                  pl.BlockSpec((None, tn, K), lambda i, j, g: (g, j, 0))] + [ANY] * len(deps),
        out_specs=pl.BlockSpec((tm, tn), lambda i, j, g: (i, j)),
        out_shape=jax.ShapeDtypeStruct((S, N), f32),
        compiler_params=_params("parallel", "parallel", "arbitrary"),
    )(a, w, *deps)


def ffn_dact(dx, wout, gu, name):
    S, D = dx.shape
    F = wout.shape[0]
    tm, tn = _tile(S, 512), _tile(F, 512)

    def body(dx_ref, w_ref, gu_ref, o_ref):
        da = FFN_RES * _dot(dx_ref[...], w_ref[...], NT)
        g = gu_ref[0].astype(f32)
        u = gu_ref[1].astype(f32)
        sg = _sig(g)
        o_ref[0] = (da * u * (sg * (1.0 + g * (1.0 - sg)))).astype(bf16)
        o_ref[1] = (da * (g * sg)).astype(bf16)

    return pl.pallas_call(
        body, name=name, grid=(S // tm, F // tn),
        in_specs=[pl.BlockSpec((tm, D), lambda i, j: (i, 0)), pl.BlockSpec((tn, D), lambda i, j: (j, 0)),
                  pl.BlockSpec((2, tm, tn), lambda i, j: (0, i, j))],
        out_specs=pl.BlockSpec((2, tm, tn), lambda i, j: (0, i, j)),
        out_shape=jax.ShapeDtypeStruct((2, S, F), bf16),
        compiler_params=_params("parallel", "parallel"),
    )(dx, wout, gu)


def _rope(t, c, s1, s2):
    w = t.shape[1]
    return t * c + pltpu.roll(t, 8, 1) * s1 + pltpu.roll(t, w - 8, 1) * s2


def _rope_t(d, c, s1, s2):
    w = d.shape[1]
    return d * c + pltpu.roll(d * s1, w - 8, 1) + pltpu.roll(d * s2, 8, 1)


def _attn_mask(n):
    qi = lax.broadcasted_iota(jnp.int32, (BLK, 2 * BLK), 0)
    kj = lax.broadcasted_iota(jnp.int32, (BLK, 2 * BLK), 1)
    dist = qi + BLK - kj
    return (dist >= 0) & (dist < BLK) & ((kj >= BLK) | (n > 0))


def _softmax_sink(s, valid, sk):
    s = jnp.where(valid, s, -1e30)
    m = jnp.maximum(jnp.max(s, axis=-1, keepdims=True), sk)
    e = jnp.exp(s - m)
    es = jnp.exp(sk - m)
    inv = 1.0 / (jnp.sum(e, axis=-1, keepdims=True) + es)
    return e * inv, es * inv


def attn_fwd(p, rope_c, rope_s1, rope_s2, sinks, name):
    S = p.shape[0]
    nb = S // BLK
    kvb = Q_END // KV_W

    def body(sink_ref, q_ref, kvc_ref, kvp_ref, cc_ref, s1c_ref, s2c_ref, cp_ref, s1p_ref, s2p_ref, o_ref):
        n = pl.program_id(0)
        cc, s1c, s2c = cc_ref[...], s1c_ref[...], s2c_ref[...]
        cp, s1p, s2p = cp_ref[...], s1p_ref[...], s2p_ref[...]
        q = _rope(q_ref[...], jnp.tile(cc, (1, 8)), jnp.tile(s1c, (1, 8)), jnp.tile(s2c, (1, 8)))
        kc = _rope(kvc_ref[:, :256], jnp.tile(cc, (1, 2)), jnp.tile(s1c, (1, 2)), jnp.tile(s2c, (1, 2)))
        kp = _rope(kvp_ref[:, :256], jnp.tile(cp, (1, 2)), jnp.tile(s1p, (1, 2)), jnp.tile(s2p, (1, 2)))
        k = jnp.concatenate([kp, kc], axis=0).astype(bf16)
        v = jnp.concatenate([kvp_ref[:, 256:], kvc_ref[:, 256:]], axis=0).astype(bf16)
        q = q.astype(bf16)
        valid = _attn_mask(n)
        for h in range(N_KV_HEADS):
            kh = k[:, h * HEAD_DIM:(h + 1) * HEAD_DIM]
            vh = v[:, h * HEAD_DIM:(h + 1) * HEAD_DIM]
            for g in range(GQ):
                hq = h * GQ + g
                qh = q[:, hq * HEAD_DIM:(hq + 1) * HEAD_DIM]
                s = _dot(qh, kh, NT) * (HEAD_DIM ** -0.5)
                pr, _ = _softmax_sink(s, valid, sink_ref[hq])
                o = _dot(pr.astype(bf16), vh)
                o_ref[:, hq * HEAD_DIM:(hq + 1) * HEAD_DIM] = o.astype(bf16)

    tab_c = pl.BlockSpec((BLK, LANE), lambda n: (n, 0))
    tab_p = pl.BlockSpec((BLK, LANE), lambda n: (jnp.maximum(n - 1, 0), 0))
    return pl.pallas_call(
        body, name=name, grid=(nb,),
        in_specs=[pl.BlockSpec(memory_space=pltpu.SMEM),
                  pl.BlockSpec((BLK, Q_END), lambda n: (n, 0)),
                  pl.BlockSpec((BLK, KV_W), lambda n: (n, kvb)),
                  pl.BlockSpec((BLK, KV_W), lambda n: (jnp.maximum(n - 1, 0), kvb)),
                  tab_c, tab_c, tab_c, tab_p, tab_p, tab_p],
        out_specs=pl.BlockSpec((BLK, Q_END), lambda n: (n, 0)),
        out_shape=jax.ShapeDtypeStruct((S, Q_END), bf16),
        compiler_params=_params("parallel"),
    )(sinks, p, p, p, rope_c, rope_s1, rope_s2, rope_c, rope_s1, rope_s2)


def attn_bwd(p, dcat, rope_c, rope_s1, rope_s2, sinks, name):
    S = p.shape[0]
    nb = S // BLK
    kvb = Q_END // KV_W

    def body(sink_ref, q_ref, kvc_ref, kvp_ref, do_ref, cc_ref, s1c_ref, s2c_ref, cp_ref, s1p_ref, s2p_ref,
             dq_ref, dkv_ref, dsink_ref, carry, dq_scr, dkv_scr):
        n = pl.program_id(0)

        @pl.when(n == 0)
        def _():
            carry[...] = jnp.zeros_like(carry)
            dsink_ref[...] = jnp.zeros_like(dsink_ref)

        cp, s1p, s2p = cp_ref[...], s1p_ref[...], s2p_ref[...]
        cp2, s1p2, s2p2 = jnp.tile(cp, (1, 2)), jnp.tile(s1p, (1, 2)), jnp.tile(s2p, (1, 2))

        @pl.when(n < nb)
        def _():
            cc, s1c, s2c = cc_ref[...], s1c_ref[...], s2c_ref[...]
            cc8, s1c8, s2c8 = jnp.tile(cc, (1, 8)), jnp.tile(s1c, (1, 8)), jnp.tile(s2c, (1, 8))
            q = _rope(q_ref[...], cc8, s1c8, s2c8).astype(bf16)
            kc = _rope(kvc_ref[:, :256], jnp.tile(cc, (1, 2)), jnp.tile(s1c, (1, 2)), jnp.tile(s2c, (1, 2)))
            kp = _rope(kvp_ref[:, :256], cp2, s1p2, s2p2)
            k = jnp.concatenate([kp, kc], axis=0).astype(bf16)
            v = jnp.concatenate([kvp_ref[:, 256:], kvc_ref[:, 256:]], axis=0).astype(bf16)
            do = do_ref[...].astype(bf16)
            valid = _attn_mask(n)
            lane = lax.broadcasted_iota(jnp.int32, (1, LANE), 1)
            dsink = jnp.zeros((1, LANE), f32)
            for h in range(N_KV_HEADS):
                kh = k[:, h * HEAD_DIM:(h + 1) * HEAD_DIM]
                vh = v[:, h * HEAD_DIM:(h + 1) * HEAD_DIM]
                dkh = jnp.zeros((2 * BLK, HEAD_DIM), f32)
                dvh = jnp.zeros((2 * BLK, HEAD_DIM), f32)
                for g in range(GQ):
                    hq = h * GQ + g
                    qh = q[:, hq * HEAD_DIM:(hq + 1) * HEAD_DIM]
                    doh = do[:, hq * HEAD_DIM:(hq + 1) * HEAD_DIM]
                    s = _dot(qh, kh, NT) * (HEAD_DIM ** -0.5)
                    pr, ps = _softmax_sink(s, valid, sink_ref[hq])
                    dpr = _dot(doh, vh, NT)
                    dvh = dvh + _dot(pr.astype(bf16), doh, TN)
                    row = jnp.sum(pr * dpr, axis=-1, keepdims=True)
                    ds = (pr * (dpr - row) * (HEAD_DIM ** -0.5)).astype(bf16)
                    dsink = dsink + jnp.where(lane == hq, -jnp.sum(ps * row, axis=0, keepdims=True), 0.0)
                    dq_scr[:, hq * HEAD_DIM:(hq + 1) * HEAD_DIM] = _dot(ds, kh)
                    dkh = dkh + _dot(ds, qh, TN)
                dkv_scr[:, h * HEAD_DIM:(h + 1) * HEAD_DIM] = dkh
                dkv_scr[:, 256 + h * HEAD_DIM:256 + (h + 1) * HEAD_DIM] = dvh
            dsink_ref[...] += dsink
            dq_ref[...] = _rope_t(dq_scr[...], cc8, s1c8, s2c8).astype(bf16)

        prev = carry[...]

        @pl.when(n < nb)
        def _():
            dkv_scr[pl.ds(0, BLK), :] = dkv_scr[pl.ds(0, BLK), :] + prev

        @pl.when(n == nb)
        def _():
            dkv_scr[pl.ds(0, BLK), :] = prev

        done = dkv_scr[pl.ds(0, BLK), :]
        dkv_ref[:, :256] = _rope_t(done[:, :256], cp2, s1p2, s2p2).astype(bf16)
        dkv_ref[:, 256:] = done[:, 256:].astype(bf16)

        @pl.when(n < nb)
        def _():
            carry[...] = dkv_scr[pl.ds(BLK, BLK), :]

    cur = lambda n: jnp.minimum(n, nb - 1)
    prv = lambda n: jnp.maximum(n - 1, 0)
    tab_c = pl.BlockSpec((BLK, LANE), lambda n: (cur(n), 0))
    tab_p = pl.BlockSpec((BLK, LANE), lambda n: (prv(n), 0))
    return pl.pallas_call(
        body, name=name, grid=(nb + 1,),
        in_specs=[pl.BlockSpec(memory_space=pltpu.SMEM),
                  pl.BlockSpec((BLK, Q_END), lambda n: (cur(n), 0)),
                  pl.BlockSpec((BLK, KV_W), lambda n: (cur(n), kvb)),
                  pl.BlockSpec((BLK, KV_W), lambda n: (prv(n), kvb)),
                  pl.BlockSpec((BLK, Q_END), lambda n: (cur(n), 0)),
                  tab_c, tab_c, tab_c, tab_p, tab_p, tab_p],
        out_specs=[pl.BlockSpec((BLK, Q_END), lambda n: (cur(n), 0)),
                   pl.BlockSpec((BLK, KV_W), lambda n: (prv(n), 0)),
                   pl.BlockSpec((1, LANE), lambda n: (0, 0))],
        out_shape=[jax.ShapeDtypeStruct((S, Q_END), bf16), jax.ShapeDtypeStruct((S, KV_W), bf16),
                   jax.ShapeDtypeStruct((1, LANE), f32)],
        scratch_shapes=[pltpu.VMEM((BLK, KV_W), f32), pltpu.VMEM((BLK, Q_END), f32), pltpu.VMEM((2 * BLK, KV_W), f32)],
        compiler_params=_params("arbitrary"),
    )(sinks, p, p, p, dcat, rope_c, rope_s1, rope_s2, rope_c, rope_s1, rope_s2)


A1_BLK = V_END // CONV_CH
A2_BLK = A1_BLK + 1


def _ln_stats(y):
    mu = jnp.mean(y, axis=-1, keepdims=True)
    xc = y - mu
    rstd = lax.rsqrt(jnp.mean(xc * xc, axis=-1, keepdims=True) + NORM_EPS)
    return xc * rstd, rstd


def conv_fwd(p, w, b, lng, lnb, name):
    S = p.shape[0]
    T = _tile(S, 256)
    r = T // HALO

    def body(a1_ref, a2_ref, h1_ref, h2_ref, w_ref, b_ref, g_ref, bb_ref, o_ref, y_ref, scr):
        i = pl.program_id(0)
        halo = h1_ref[...] * _sig(h2_ref[...])
        scr[pl.ds(0, HALO), :] = jnp.where(i > 0, halo, 0.0)
        scr[pl.ds(HALO, T), :] = a1_ref[...] * _sig(a2_ref[...])
        acc = jnp.zeros((T, CONV_CH), f32) + b_ref[...]
        for j in range(CONV_W):
            acc = acc + scr[pl.ds(HALO - (CONV_W - 1) + j, T), :] * w_ref[j:j + 1, :]
        y_ref[...] = acc
        yh, _ = _ln_stats(acc)
        z = yh * g_ref[...] + bb_ref[...]
        o_ref[...] = (z * _sig(z)).astype(bf16)

    vec = pl.BlockSpec((1, CONV_CH), lambda i: (0, 0))
    halo_map = lambda i: jnp.maximum(i * r - 1, 0)
    return pl.pallas_call(
        body, name=name, grid=(S // T,),
        in_specs=[pl.BlockSpec((T, CONV_CH), lambda i: (i, A1_BLK)), pl.BlockSpec((T, CONV_CH), lambda i: (i, A2_BLK)),
                  pl.BlockSpec((HALO, CONV_CH), lambda i: (halo_map(i), A1_BLK)),
                  pl.BlockSpec((HALO, CONV_CH), lambda i: (halo_map(i), A2_BLK)),
                  pl.BlockSpec((HALO, CONV_CH), lambda i: (0, 0)), vec, vec, vec],
        out_specs=[pl.BlockSpec((T, CONV_CH), lambda i: (i, 0)), pl.BlockSpec((T, CONV_CH), lambda i: (i, 0))],
        out_shape=[jax.ShapeDtypeStruct((S, CONV_CH), bf16), jax.ShapeDtypeStruct((S, CONV_CH), f32)],
        scratch_shapes=[pltpu.VMEM((T + HALO, CONV_CH), f32)],
        compiler_params=_params("parallel"),
    )(p, p, p, p, w, b, lng, lnb)


def conv_bwd(p, y, dcat, w, lng, lnb, name):
    S = p.shape[0]
    T = _tile(S, 256)
    n = S // T
    r = T // HALO
    dcb = Q_END // CONV_CH

    def body(a1_ref, a2_ref, h1_ref, h2_ref, y_ref, yn_ref, do_ref, don_ref, w_ref, g_ref, bb_ref,
             da_ref, dw_ref, db_ref, dg_ref, dbb_ref, scr_h, scr_dy, acc_b, acc_g, acc_bb):
        i = pl.program_id(0)

        @pl.when(i == 0)
        def _():
            dw_ref[...] = jnp.zeros_like(dw_ref)
            acc_b[...] = jnp.zeros_like(acc_b)
            acc_g[...] = jnp.zeros_like(acc_g)
            acc_bb[...] = jnp.zeros_like(acc_bb)

        gv, bv = g_ref[...], bb_ref[...]

        def ln_silu_bwd(yv, dout):
            yh, rstd = _ln_stats(yv)
            z = yh * gv + bv
            sg = _sig(z)
            dz = dout * (sg * (1.0 + z * (1.0 - sg)))
            gz = dz * gv
            dy = rstd * (gz - jnp.mean(gz, axis=-1, keepdims=True) - yh * jnp.mean(gz * yh, axis=-1, keepdims=True))
            return dy, dz, yh

        dy, dz, yh = ln_silu_bwd(y_ref[...], do_ref[...])
        dyn, _, _ = ln_silu_bwd(yn_ref[...], don_ref[...])
        acc_g[...] += _rowsum8(dz * yh)
        acc_bb[...] += _rowsum8(dz)
        acc_b[...] += _rowsum8(dy)
        scr_dy[pl.ds(0, T), :] = dy
        scr_dy[pl.ds(T, HALO), :] = jnp.where(i < n - 1, dyn, 0.0)
        a1, a2 = a1_ref[...], a2_ref[...]
        sg2 = _sig(a2)
        halo = h1_ref[...] * _sig(h2_ref[...])
        scr_h[pl.ds(0, HALO), :] = jnp.where(i > 0, halo, 0.0)
        scr_h[pl.ds(HALO, T), :] = a1 * sg2
        dh = jnp.zeros((T, CONV_CH), f32)
        for j in range(CONV_W):
            dh = dh + scr_dy[pl.ds(CONV_W - 1 - j, T), :] * w_ref[j:j + 1, :]
            dw_ref[j:j + 1, :] += jnp.sum(dy * scr_h[pl.ds(HALO - (CONV_W - 1) + j, T), :], axis=0, keepdims=True)
        da_ref[:, :CONV_CH] = (dh * sg2).astype(bf16)
        da_ref[:, CONV_CH:] = (dh * a1 * sg2 * (1.0 - sg2)).astype(bf16)

        @pl.when(i == n - 1)
        def _():
            db_ref[...] = jnp.sum(acc_b[...], axis=0, keepdims=True)
            dg_ref[...] = jnp.sum(acc_g[...], axis=0, keepdims=True)
            dbb_ref[...] = jnp.sum(acc_bb[...], axis=0, keepdims=True)

    vec = pl.BlockSpec((1, CONV_CH), lambda i: (0, 0))
    tap = pl.BlockSpec((HALO, CONV_CH), lambda i: (0, 0))
    prev_map = lambda i: jnp.maximum(i * r - 1, 0)
    next_map = lambda i: jnp.minimum((i + 1) * r, S // HALO - 1)
    return pl.pallas_call(
        body, name=name, grid=(n,),
        in_specs=[pl.BlockSpec((T, CONV_CH), lambda i: (i, A1_BLK)), pl.BlockSpec((T, CONV_CH), lambda i: (i, A2_BLK)),
                  pl.BlockSpec((HALO, CONV_CH), lambda i: (prev_map(i), A1_BLK)),
                  pl.BlockSpec((HALO, CONV_CH), lambda i: (prev_map(i), A2_BLK)),
                  pl.BlockSpec((T, CONV_CH), lambda i: (i, 0)),
                  pl.BlockSpec((HALO, CONV_CH), lambda i: (next_map(i), 0)),
                  pl.BlockSpec((T, CONV_CH), lambda i: (i, dcb)),
                  pl.BlockSpec((HALO, CONV_CH), lambda i: (next_map(i), dcb)),
                  tap, vec, vec],
        out_specs=[pl.BlockSpec((T, 2 * CONV_CH), lambda i: (i, 0)), tap, vec, vec, vec],
        out_shape=[jax.ShapeDtypeStruct((S, 2 * CONV_CH), bf16), jax.ShapeDtypeStruct((HALO, CONV_CH), f32),
                   jax.ShapeDtypeStruct((1, CONV_CH), f32), jax.ShapeDtypeStruct((1, CONV_CH), f32),
                   jax.ShapeDtypeStruct((1, CONV_CH), f32)],
        scratch_shapes=[pltpu.VMEM((T + HALO, CONV_CH), f32), pltpu.VMEM((T + HALO, CONV_CH), f32),
                        pltpu.VMEM((8, CONV_CH), f32), pltpu.VMEM((8, CONV_CH), f32), pltpu.VMEM((8, CONV_CH), f32)],
        compiler_params=_params("arbitrary"),
    )(p, p, p, p, y, y, dcat, dcat, w, lng, lnb)


U_BLK = (V_END + 2 * CONV_CH) // SGU_CH
SV_BLK = U_BLK + 1


def _tril(w, transposed=False):
    row = lax.broadcasted_iota(jnp.int32, (BLK, BLK), 0)
    col = lax.broadcasted_iota(jnp.int32, (BLK, BLK), 1)
    keep = (col >= row) if transposed else (row >= col)
    return jnp.where(keep, w, 0.0)


def sgu_fwd(p, lng, lnb, w, bias, name):
    S = p.shape[0]
    T = _tile(S, 256)

    def body(u_ref, v_ref, g_ref, bb_ref, w_ref, bias_ref, o_ref):
        yh, _ = _ln_stats(v_ref[...])
        v = (yh * g_ref[...] + bb_ref[...]).astype(bf16)
        low = lax.broadcasted_iota(jnp.int32, (BLK, LANE), 1) < HEAD_DIM
        for pr in range(SGU_HEADS // 2):
            lanes = pl.ds(pr * LANE, LANE)
            w0 = _tril(w_ref[2 * pr]).astype(bf16)
            w1 = _tril(w_ref[2 * pr + 1]).astype(bf16)
            for c in range(T // BLK):
                rows = pl.ds(c * BLK, BLK)
                vp = v[c * BLK:(c + 1) * BLK, pr * LANE:(pr + 1) * LANE]
                mixed = jnp.where(low, _dot(w0, vp), _dot(w1, vp)) + bias_ref[:, lanes]
                o_ref[rows, lanes] = (u_ref[rows, lanes] * mixed).astype(bf16)

    vec = pl.BlockSpec((1, SGU_CH), lambda i: (0, 0))
    return pl.pallas_call(
        body, name=name, grid=(S // T,),
        in_specs=[pl.BlockSpec((T, SGU_CH), lambda i: (i, U_BLK)), pl.BlockSpec((T, SGU_CH), lambda i: (i, SV_BLK)),
                  vec, vec, pl.BlockSpec((SGU_HEADS, BLK, BLK), lambda i: (0, 0, 0)),
                  pl.BlockSpec((BLK, SGU_CH), lambda i: (0, 0))],
        out_specs=pl.BlockSpec((T, SGU_CH), lambda i: (i, 0)),
        out_shape=jax.ShapeDtypeStruct((S, SGU_CH), bf16),
        compiler_params=_params("parallel"),
    )(p, p, lng, lnb, w, bias)


def sgu_bwd(p, dcat, lng, lnb, w, wt, bias, name):
    S = p.shape[0]
    T = _tile(S, 256)
    n = S // T
    dsb = (Q_END + CONV_CH) // SGU_CH

    def body(u_ref, v_ref, do_ref, g_ref, bb_ref, w_ref, wt_ref, bias_ref,
             da_ref, dw_ref, db_ref, dg_ref, dbb_ref, dv_scr, acc_bias, acc_g, acc_bb):
        i = pl.program_id(0)

        @pl.when(i == 0)
        def _():
            dw_ref[...] = jnp.zeros_like(dw_ref)
            acc_bias[...] = jnp.zeros_like(acc_bias)
            acc_g[...] = jnp.zeros_like(acc_g)
            acc_bb[...] = jnp.zeros_like(acc_bb)

        gv = g_ref[...]
        yh, rstd = _ln_stats(v_ref[...])
        v = (yh * gv + bb_ref[...]).astype(bf16)
        low = lax.broadcasted_iota(jnp.int32, (BLK, LANE), 1) < HEAD_DIM
        for pr in range(SGU_HEADS // 2):
            lanes = pl.ds(pr * LANE, LANE)
            w0 = _tril(w_ref[2 * pr]).astype(bf16)
            w1 = _tril(w_ref[2 * pr + 1]).astype(bf16)
            wt0 = _tril(wt_ref[2 * pr], True).astype(bf16)
            wt1 = _tril(wt_ref[2 * pr + 1], True).astype(bf16)
            dw0 = jnp.zeros((BLK, BLK), f32)
            dw1 = jnp.zeros((BLK, BLK), f32)
            for c in range(T // BLK):
                rows = pl.ds(c * BLK, BLK)
                vp = v[c * BLK:(c + 1) * BLK, pr * LANE:(pr + 1) * LANE]
                mixed = jnp.where(low, _dot(w0, vp), _dot(w1, vp)) + bias_ref[:, lanes]
                do = do_ref[rows, lanes]
                da_ref[rows, lanes] = (do * mixed).astype(bf16)
                dm = do * u_ref[rows, lanes]
                acc_bias[:, lanes] += dm
                dmb = dm.astype(bf16)
                dv_scr[rows, lanes] = jnp.where(low, _dot(wt0, dmb), _dot(wt1, dmb))
                zero = jnp.zeros_like(dmb)
                dw0 = dw0 + _dot(jnp.where(low, dmb, zero), vp, NT)
                dw1 = dw1 + _dot(jnp.where(low, zero, dmb), vp, NT)
            dw_ref[2 * pr] += _tril(dw0)
            dw_ref[2 * pr + 1] += _tril(dw1)
        dv = dv_scr[...]
        acc_g[...] += _rowsum8(dv * yh)
        acc_bb[...] += _rowsum8(dv)
        gz = dv * gv
        dvr = rstd * (gz - jnp.mean(gz, axis=-1, keepdims=True) - yh * jnp.mean(gz * yh, axis=-1, keepdims=True))
        da_ref[:, SGU_CH:] = dvr.astype(bf16)

        @pl.when(i == n - 1)
        def _():
            ch = lax.broadcasted_iota(jnp.int32, (SGU_CH, LANE), 0) // HEAD_DIM
            hd = lax.broadcasted_iota(jnp.int32, (SGU_CH, LANE), 1)
            fold = jnp.where(ch == hd, 1.0, 0.0).astype(f32)
            db_ref[...] = jnp.dot(acc_bias[...], fold, preferred_element_type=f32, precision=lax.Precision.HIGHEST)
            dg_ref[...] = jnp.sum(acc_g[...], axis=0, keepdims=True)
            dbb_ref[...] = jnp.sum(acc_bb[...], axis=0, keepdims=True)

    vec = pl.BlockSpec((1, SGU_CH), lambda i: (0, 0))
    wsp = pl.BlockSpec((SGU_HEADS, BLK, BLK), lambda i: (0, 0, 0))
    return pl.pallas_call(
        body, name=name, grid=(n,),
        in_specs=[pl.BlockSpec((T, SGU_CH), lambda i: (i, U_BLK)), pl.BlockSpec((T, SGU_CH), lambda i: (i, SV_BLK)),
                  pl.BlockSpec((T, SGU_CH), lambda i: (i, dsb)), vec, vec, wsp, wsp,
                  pl.BlockSpec((BLK, SGU_CH), lambda i: (0, 0))],
        out_specs=[pl.BlockSpec((T, 2 * SGU_CH), lambda i: (i, 0)), wsp,
                   pl.BlockSpec((BLK, LANE), lambda i: (0, 0)), vec, vec],
        out_shape=[jax.ShapeDtypeStruct((S, 2 * SGU_CH), bf16), jax.ShapeDtypeStruct((SGU_HEADS, BLK, BLK), f32),
                   jax.ShapeDtypeStruct((BLK, LANE), f32), jax.ShapeDtypeStruct((1, SGU_CH), f32),
                   jax.ShapeDtypeStruct((1, SGU_CH), f32)],
        scratch_shapes=[pltpu.VMEM((T, SGU_CH), f32), pltpu.VMEM((BLK, SGU_CH), f32),
                        pltpu.VMEM((8, SGU_CH), f32), pltpu.VMEM((8, SGU_CH), f32)],
        compiler_params=_params("arbitrary"),
    )(p, p, dcat, lng, lnb, w, wt, bias)


HBM = pl.BlockSpec(memory_space=pltpu.HBM)
SEM = pl.BlockSpec(memory_space=pltpu.SEMAPHORE)
ANY = pl.BlockSpec(memory_space=pl.ANY)
EFFECT = pltpu.SideEffectType.DATAFLOW_SIDE_EFFECTING


def _flip(x, y, c, k):
    px, py, pc = x ^ (k >> 2), y ^ ((k >> 1) & 1), c ^ (k & 1)
    return (px, py, pc), 4 * px + 2 * py + pc


def _routes_gather(x, y, c):
    me = 4 * x + 2 * y + c
    out = []
    for k in (1, 2, 4, 6):
        dev, idx = _flip(x, y, c, k)
        out.append((dev, None, me, idx))
    return out


def _routes_pair(x, y, c):
    dev, _ = _flip(x, y, c, 1)
    return [(dev, 2 * q + (1 - c), q, q) for q in range(N_DEV // 2)]


def _routes_chips(x, y, c):
    out = []
    for k in (2, 4, 6):
        dev, idx = _flip(x, y, c, k)
        out.append((dev, idx // 2, 2 * x + y, idx // 2))
    return out


def _routes_forward(x, y, c):
    sib, _ = _flip(x, y, c, 1)
    out = []
    for k in (2, 4, 6):
        _, idx = _flip(x, y, c, k)
        out.append((sib, idx, idx, idx ^ 1))
    return out


def _routes_all(x, y, c):
    me = 4 * x + 2 * y + c
    out = []
    for k in range(1, N_DEV):
        dev, idx = _flip(x, y, c, k)
        out.append((dev, None, me, idx))
    return out


def _slot(ref, slot, kind):
    if slot is None:
        return ref
    if kind == "cols":
        width = ref.shape[2] // (N_DEV // 2)
        return ref.at[slot // (N_DEV // 2), :, pl.ds(pl.multiple_of((slot % (N_DEV // 2)) * width, LANE), width)]
    return ref.at[slot]


def _copies(routes, srcs, lands, send_sems, recv_sems, incoming, src_kinds, land_kinds):
    x, y, c = lax.axis_index("x"), lax.axis_index("y"), lax.axis_index("c")
    out = []
    n = len(lands)
    if srcs is None:
        srcs, src_kinds = lands, land_kinds
    for k, (dev, src_slot, dst_slot, recv_slot) in enumerate(routes(x, y, c)):
        for a in range(n):
            out.append(pltpu.make_async_remote_copy(
                src_ref=_slot(srcs[a], src_slot, src_kinds[a]),
                dst_ref=_slot(lands[a], recv_slot if incoming else dst_slot, land_kinds[a]),
                send_sem=send_sems.at[k * n + a], recv_sem=recv_sems.at[k * n + a], device_id=dev, device_id_type=MESH))
    return out


def _pin(a):
    return pltpu.with_memory_space_constraint(a, pltpu.HBM)


def split_start(srcs, lands, routes, name, deps=(), src_kinds=None, land_kinds=None):
    n = len(lands)
    ns = 0 if srcs is None else n
    n_routes = len(routes(0, 0, 0))
    ops = ([] if srcs is None else list(srcs)) + list(lands)
    src_kinds = src_kinds or ["rows"] * n
    land_kinds = land_kinds or ["rows"] * n

    def body(*refs):
        src, land = (refs[:n] if ns else None), refs[ns:ns + n]
        first_out = ns + n + len(deps)
        send_sems, recv_sems, token = refs[first_out], refs[first_out + 1], refs[-1]
        for cp in _copies(routes, src, land, send_sems, recv_sems, False, src_kinds, land_kinds):
            cp.start()
        token[...] = jnp.zeros_like(token)

    thru = [pltpu.HBM(a.shape, a.dtype) for a in ops]
    res = pl.pallas_call(
        body, name=name,
        out_shape=(pltpu.SemaphoreType.DMA((n * n_routes,)), pltpu.SemaphoreType.DMA((n * n_routes,)), *thru,
                   jax.ShapeDtypeStruct((8, LANE), f32)),
        in_specs=[HBM] * len(ops) + [ANY] * len(deps),
        out_specs=(SEM, SEM, *([HBM] * len(ops)), pl.BlockSpec(memory_space=pltpu.VMEM)),
        input_output_aliases={i: 2 + i for i in range(len(ops))},
        compiler_params=pltpu.CompilerParams(has_side_effects=EFFECT),
    )(*[_pin(a) for a in ops], *deps)
    return (res[0], res[1], (list(res[2:2 + n]) if ns else None), list(res[2 + ns:2 + ns + n]), res[-1],
            (src_kinds, land_kinds))


def split_wait(started, after, routes, name):
    send_sems, recv_sems, srcs, lands, _, (src_kinds, land_kinds) = started
    n = len(lands)
    ns = 0 if srcs is None else n
    ops = ([] if srcs is None else list(srcs)) + list(lands)
    afters = list(after) if isinstance(after, (list, tuple)) else [after]

    def body(*refs):
        src, land = (refs[:n] if ns else None), refs[ns:ns + n]
        send_s, recv_s = refs[ns + n], refs[ns + n + 1]
        for cp in _copies(routes, src, land, send_s, recv_s, True, src_kinds, land_kinds):
            cp.wait_send()
            cp.wait_recv()

    thru = [pltpu.HBM(a.shape, a.dtype) for a in ops]
    res = pl.pallas_call(
        body, name=name, out_shape=tuple(thru),
        in_specs=[HBM] * len(ops) + [SEM, SEM] + [ANY] * len(afters), out_specs=tuple([HBM] * len(ops)),
        input_output_aliases={i: i for i in range(len(ops))},
        compiler_params=pltpu.CompilerParams(has_side_effects=EFFECT),
    )(*ops, send_sems, recv_sems, *afters)
    return (list(res[:n]) if ns else None), list(res[ns:ns + n])


def chip_sum(parts, land, c_idx, kind, name):
    _, R, C = land.shape
    tr = _tile(R, 256)
    half = N_DEV // 2

    def body(c_ref, p_ref, l_ref, o_ref):
        o_ref[...] = (p_ref[...].astype(f32) + l_ref[...].astype(f32)).astype(bf16)

    if kind == "cols":
        mine = lambda q, i, c_ref: ((2 * q + c_ref[0]) // half, i, (2 * q + c_ref[0]) % half)
    else:
        mine = lambda q, i, c_ref: (2 * q + c_ref[0], i, 0)
    return pl.pallas_call(
        body, name=name,
        grid_spec=pltpu.PrefetchScalarGridSpec(
            num_scalar_prefetch=1, grid=(half, R // tr),
            in_specs=[pl.BlockSpec((None, tr, C), mine), pl.BlockSpec((None, tr, C), lambda q, i, c_ref: (q, i, 0))],
            out_specs=pl.BlockSpec((None, tr, C), lambda q, i, c_ref: (q, i, 0))),
        out_shape=jax.ShapeDtypeStruct((half, R, C), bf16),
        compiler_params=_params("parallel", "parallel"),
    )(c_idx, parts, land)


def place_own(land, src, me_idx, kind, name):
    R, C = src.shape
    tr = _tile(R, 512)
    half = N_DEV // 2
    if kind == "cols":
        where = lambda i, m: (m[0] // half, i, m[0] % half)
    else:
        where = lambda i, m: (m[0], i, 0)

    def body(m_ref, land_ref, src_ref, out_ref):
        out_ref[...] = src_ref[...]

    return pl.pallas_call(
        body, name=name,
        grid_spec=pltpu.PrefetchScalarGridSpec(
            num_scalar_prefetch=1, grid=(R // tr,),
            in_specs=[ANY, pl.BlockSpec((tr, C), lambda i, m: (i, 0))],
            out_specs=pl.BlockSpec((None, tr, C), where)),
        out_shape=jax.ShapeDtypeStruct(land.shape, land.dtype),
        input_output_aliases={1: 0},
        compiler_params=_params("arbitrary"),
    )(me_idx, land, src)


def sum_slots(parts, name):
    P, R, C = parts.shape
    tr = _tile(R, 512)

    def body(p_ref, o_ref):
        total = p_ref[0]
        for j in range(1, P):
            total = total + p_ref[j]
        o_ref[...] = total

    return pl.pallas_call(
        body, name=name, grid=(R // tr,),
        in_specs=[pl.BlockSpec((P, tr, C), lambda i: (0, i, 0))],
        out_specs=pl.BlockSpec((tr, C), lambda i: (i, 0)),
        out_shape=jax.ShapeDtypeStruct((R, C), f32),
        compiler_params=_params("parallel"),
    )(parts)


def adamw(parts, owns, chip, w, m, v, name):
    L, R, C = w.shape
    P = parts[0].shape[0]
    tr = _tile(R, 128 if C > 1024 else 256)
    nr = R // tr
    c1 = 1.0 - ADAM_B1 ** ADAM_STEP
    c2 = 1.0 - ADAM_B2 ** ADAM_STEP
    n_own = L if owns is not None else 0

    def body(chip_ref, *refs):
        part_refs, own_refs = refs[:L], refs[L:L + n_own]
        w_ref, m_ref, v_ref, g_out, d_out, m_out, v_out = refs[L + n_own:]
        layer = pl.program_id(0)
        for l in range(L):
            @pl.when(layer == l)
            def _(l=l):
                g = None
                for q in range(P):
                    term = part_refs[l][q].astype(f32)
                    if n_own:
                        term = jnp.where(chip_ref[0] == q, own_refs[l][...].astype(f32), term)
                    g = term if g is None else g + term
                mn = ADAM_B1 * m_ref[...] + (1.0 - ADAM_B1) * g
                vn = ADAM_B2 * v_ref[...] + (1.0 - ADAM_B2) * (g * g)
                g_out[...] = g
                m_out[...] = mn
                v_out[...] = vn
                d_out[...] = -ADAM_LR * ((mn / c1) / (jnp.sqrt(vn / c2) + ADAM_EPS) + ADAM_WD * w_ref[...])

    def rows(l, a, i):
        return jnp.where(a == l, i, jnp.where(a < l, 0, nr - 1))

    def part_spec(l):
        return pl.BlockSpec((P, tr, C), lambda a, i, chip_ref: (0, rows(l, a, i), 0))

    def own_spec(l):
        return pl.BlockSpec((None, tr, C), lambda a, i, chip_ref: (chip_ref[0], rows(l, a, i), 0))

    slab = pl.BlockSpec((None, tr, C), lambda a, i, chip_ref: (a, i, 0))
    out = jax.ShapeDtypeStruct((L, R, C), f32)
    return pl.pallas_call(
        body, name=name,
        grid_spec=pltpu.PrefetchScalarGridSpec(
            num_scalar_prefetch=1, grid=(L, nr),
            in_specs=[part_spec(l) for l in range(L)] + [own_spec(l) for l in range(n_own)] + [slab, slab, slab],
            out_specs=[slab, slab, slab, slab]),
        out_shape=[out, out, out, out],
        compiler_params=_params("arbitrary", "arbitrary"),
    )(chip, *parts, *(owns or []), w, m, v)


PACK = 8 * LANE


def _pack(arrs):
    pieces = []
    for a in arrs:
        flat = a.astype(f32).reshape(-1)
        pad = (-flat.shape[0]) % PACK
        pieces.append(jnp.pad(flat, (0, pad)).reshape(-1, LANE))
    return jnp.concatenate(pieces, axis=0)


def _unpack(buf, shapes):
    out, row = [], 0
    for shp in shapes:
        size = math.prod(shp)
        rows = (size + PACK - 1) // PACK * (PACK // LANE)
        out.append(buf[row:row + rows].reshape(-1)[:size].reshape(shp))
        row += rows
    return out


def _rope_tables(positions):
    half = ROT_DIM // 2
    inv_freq = 1.0 / (ROPE_THETA ** (jnp.arange(0, ROT_DIM, 2, dtype=f32) / ROT_DIM))
    ang = positions.astype(f32)[:, None] * inv_freq
    cos, sin = jnp.cos(ang), jnp.sin(ang)
    S = positions.shape[0]
    zeros, ones = jnp.zeros((S, half), f32), jnp.ones((S, HEAD_DIM - ROT_DIM), f32)
    rest = jnp.zeros((S, HEAD_DIM - ROT_DIM), f32)
    c = jnp.concatenate([cos, cos, ones], axis=1)
    s1 = jnp.concatenate([zeros, sin, rest], axis=1)
    s2 = jnp.concatenate([-sin, zeros, rest], axis=1)
    return tuple(jnp.tile(t, (1, LANE // HEAD_DIM)) for t in (c, s1, s2))


def _cols_to_shards(g):
    lead, (R, N) = g.shape[:-2], g.shape[-2:]
    g = g.reshape(lead + (R, N_DEV, N // N_DEV))
    return jnp.moveaxis(g, -2, 0)


def _shards_to_cols(g):
    g = jnp.moveaxis(g, 0, -2)
    return g.reshape(g.shape[:-2] + (g.shape[-2] * g.shape[-1],))


def kernel(x, positions, norm_ffn1, ffn1_w_in, ffn1_w_out, norm_mix, w_in, conv_dw_w, conv_dw_b, conv_ln_g, conv_ln_b, sgu_ln_g, sgu_ln_b, sgu_w, sgu_b, attn_sinks, w_out, norm_ffn2, ffn2_w_in, ffn2_w_out, final_norm, loss_target, m_norm_ffn1, m_ffn1_w_in, m_ffn1_w_out, m_norm_mix, m_w_in, m_conv_dw_w, m_conv_dw_b, m_conv_ln_g, m_conv_ln_b, m_sgu_ln_g, m_sgu_ln_b, m_sgu_w, m_sgu_b, m_attn_sinks, m_w_out, m_norm_ffn2, m_ffn2_w_in, m_ffn2_w_out, m_final_norm, v_norm_ffn1, v_ffn1_w_in, v_ffn1_w_out, v_norm_mix, v_w_in, v_conv_dw_w, v_conv_dw_b, v_conv_ln_g, v_conv_ln_b, v_sgu_ln_g, v_sgu_ln_b, v_sgu_w, v_sgu_b, v_attn_sinks, v_w_out, v_norm_ffn2, v_ffn2_w_in, v_ffn2_w_out, v_final_norm):
    L = norm_ffn1.shape[0]
    S, D = x.shape[1], x.shape[2]
    F = ffn1_w_out.shape[1] * N_DEV
    me = 4 * lax.axis_index("x") + 2 * lax.axis_index("y") + lax.axis_index("c")
    x0 = x[0]
    rope_c, rope_s1, rope_s2 = _rope_tables(positions[0])
    cw = CONV_CH // N_DEV

    c_idx = lax.axis_index("c").astype(jnp.int32).reshape(1)
    chip = (2 * lax.axis_index("x") + lax.axis_index("y")).astype(jnp.int32).reshape(1)
    no_chip = jnp.zeros((1,), jnp.int32)
    me_idx = me.astype(jnp.int32).reshape(1)

    row = lambda a: a.reshape(1, -1)
    groups = ("ffn1", "mix", "ffn2")
    order_fwd = [(l, g) for l in range(L) for g in groups]

    def group_srcs(l, grp):
        if grp == "ffn1":
            return [ffn1_w_in[l].astype(bf16), ffn1_w_out[l].astype(bf16)]
        if grp == "ffn2":
            return [ffn2_w_in[l].astype(bf16), ffn2_w_out[l].astype(bf16)]
        taps = jnp.pad(conv_dw_w[l], ((0, HALO - CONV_W), (0, LANE - cw)))
        return [w_in[l].astype(bf16), w_out[l].astype(bf16), taps]

    def kinds_of(grp):
        return ["rows"] * 3 if grp == "mix" else ["cols", "rows"]

    def gather_start(k, deps=()):
        l, grp = order_fwd[k]
        srcs = group_srcs(l, grp)
        lands = [lax.empty((2, D, F) if kind == "cols" else (N_DEV,) + a.shape, a.dtype)
                 for a, kind in zip(srcs, kinds_of(grp))]
        return split_start(srcs, lands, _routes_gather, f"gather_start_{grp}_{l}", deps, land_kinds=kinds_of(grp))

    def gather_forward(k, started, after):
        l, grp = order_fwd[k]
        srcs, lands = split_wait(started, after, _routes_gather, f"gather_wait_{grp}_{l}")
        return srcs, split_start(None, lands, _routes_forward, f"forward_start_{grp}_{l}", land_kinds=kinds_of(grp))

    def gather_finish(k, srcs, started, after):
        l, grp = order_fwd[k]
        _, lands = split_wait(started, after, _routes_forward, f"forward_wait_{grp}_{l}")
        full = [place_own(ld, s, me_idx, kind, f"own_{grp}_{l}_{a}")
                for a, (ld, s, kind) in enumerate(zip(lands, srcs, kinds_of(grp)))]
        if grp == "mix":
            return dict(w_in=_shards_to_cols(full[0]), w_out=full[1].reshape(D, D),
                        taps=_shards_to_cols(full[2][:, :, :cw]))
        return dict(w_in=full[0], w_out=full[1].reshape(F, D))

    def ffn_fwd(xs, wts, g_norm, tag, sv):
        sv["x_in"] = xs
        h, sv["ht"] = rmsnorm_fwd(xs, g_norm, f"norm_{tag}")
        sv["gu"], a, sv["at"] = ffn_in(h, wts["w_in"], f"{tag}_in")
        return mm_res(a, wts["w_out"], xs, FFN_RES, f"{tag}_out")

    def mix_fwd(xs, wts, l, sv):
        sv["x_in"] = xs
        h, sv["ht"] = rmsnorm_fwd(xs, row(norm_mix[l]), f"norm_mix_{l}")
        p = mm_nn(h, wts["w_in"][None], f"mix_in_{l}")[0]
        sv["p"] = p
        attn = attn_fwd(p, rope_c, rope_s1, rope_s2, attn_sinks[l], f"attn_fwd_{l}")
        conv, sv["conv_y"] = conv_fwd(p, wts["taps"], row(conv_dw_b[l]), row(conv_ln_g[l]), row(conv_ln_b[l]),
                                      f"conv_fwd_{l}")
        sv["sgu_bias"] = jnp.repeat(sgu_b[l].T, HEAD_DIM, axis=1)
        sgu = sgu_fwd(p, row(sgu_ln_g[l]), row(sgu_ln_b[l]), sgu_w[l], sv["sgu_bias"], f"sgu_fwd_{l}")
        cat = jnp.concatenate([attn, conv, sgu], axis=1)
        sv["catt"] = cat.T
        return mm_res(cat, wts["w_out"], xs, 1.0, f"mix_out_{l}")

    weights, saved = {}, {}
    xs = x0
    first = gather_start(0)
    srcs, fwd = gather_forward(0, first, first[4])
    for k, (l, grp) in enumerate(order_fwd):
        nxt = gather_start(k + 1, (fwd[4],)) if k + 1 < len(order_fwd) else None
        wts = gather_finish(k, srcs, fwd, nxt[4] if nxt else fwd[4])
        sv = {}
        if grp == "ffn1":
            xs = ffn_fwd(xs, wts, row(norm_ffn1[l]), f"ffn1_{l}", sv)
        elif grp == "ffn2":
            xs = ffn_fwd(xs, wts, row(norm_ffn2[l]), f"ffn2_{l}", sv)
        else:
            xs = mix_fwd(xs, wts, l, sv)
        weights[l, grp], saved[l, grp] = wts, sv
        if nxt:
            srcs, fwd = gather_forward(k + 1, nxt, xs)

    dx, dxb, d_final_norm, loss = final_loss(xs, row(final_norm), loss_target[0], "final_loss")

    def scatter_start(grads, kinds, tag):
        half = N_DEV // 2
        lands = [lax.empty((half, g.shape[1], g.shape[2] // half) if kind == "cols" else (half,) + g.shape[1:], g.dtype)
                 for g, kind in zip(grads, kinds)]
        return split_start(grads, lands, _routes_pair, f"pair_start_{tag}", src_kinds=kinds)

    def pair_to_chips(started, after, tag):
        kinds = started[5][0]
        grads, landed = split_wait(started, after, _routes_pair, f"pair_wait_{tag}")
        sums = [chip_sum(g, ld, c_idx, kind, f"chip_sum_{tag}_{a}")
                for a, (g, ld, kind) in enumerate(zip(grads, landed, kinds))]
        return split_start(sums, [lax.empty(s.shape, s.dtype) for s in sums], _routes_chips, f"chips_start_{tag}")

    def ffn_bwd(dx, dxb, wts, sv, g_norm, tag, deps=()):
        dgu = ffn_dact(dxb, wts["w_out"], sv["gu"], f"{tag}_dact")
        d_w_out = mm_nn(sv["at"], dxb[None], f"{tag}_dwout", bf16, FFN_RES, deps=deps)[0]
        d_w_in = mm_nn(sv["ht"], dgu, f"{tag}_dwin", bf16)
        pair = scatter_start([d_w_in, d_w_out.reshape(N_DEV, F // N_DEV, D)], ["cols", "rows"], tag)
        dh = mm_nt(dgu, wts["w_in"], f"{tag}_dh", deps=(pair[4],))
        chips = pair_to_chips(pair, dh, tag)
        dx, dxb, dg = rmsnorm_bwd(dh, sv["x_in"], g_norm, dx, f"{tag}_dnorm", deps=(chips[4],))
        return dx, dxb, dg, chips

    small = [None] * L
    chips_pending = {}
    for l in reversed(range(L)):
        dx, dxb, d_norm_ffn2, chips_pending[l, "ffn2"] = ffn_bwd(
            dx, dxb, weights[l, "ffn2"], saved[l, "ffn2"], row(norm_ffn2[l]), f"ffn2_{l}")

        wts, sv = weights[l, "mix"], saved[l, "mix"]
        d_w_out = mm_nn(sv["catt"], dxb[None], f"mix_dwout_{l}", bf16)[0]
        dcat = mm_nt(dxb[None], wts["w_out"][None], f"mix_dcat_{l}", deps=(d_w_out,))
        p = sv["p"]
        dq, dkv, d_sinks = attn_bwd(p, dcat, rope_c, rope_s1, rope_s2, attn_sinks[l], f"attn_bwd_{l}")
        da_conv, d_taps, d_conv_b, d_conv_g, d_conv_bb = conv_bwd(
            p, sv["conv_y"], dcat, wts["taps"], row(conv_ln_g[l]), row(conv_ln_b[l]), f"conv_bwd_{l}")
        da_sgu, d_sgu_w, d_sgu_bias, d_sgu_g, d_sgu_bb = sgu_bwd(
            p, dcat, row(sgu_ln_g[l]), row(sgu_ln_b[l]), sgu_w[l], jnp.swapaxes(sgu_w[l], 1, 2), sv["sgu_bias"],
            f"sgu_bwd_{l}")
        dp = jnp.concatenate([dq, dkv, da_conv, da_sgu], axis=1)
        d_w_in = mm_nn(sv["ht"], dp[None], f"mix_dwin_{l}", bf16)[0]
        pair = scatter_start([_cols_to_shards(d_w_in), d_w_out.reshape(N_DEV, D // N_DEV, D)], ["rows", "rows"],
                             f"mix_{l}")
        dh = mm_nt(dp[None], wts["w_in"][None], f"mix_dh_{l}", deps=(pair[4],))
        chips_pending[l, "mix"] = pair_to_chips(pair, dh, f"mix_{l}")
        dx, dxb, d_norm_mix = rmsnorm_bwd(dh, sv["x_in"], row(norm_mix[l]), dx, f"mix_dnorm_{l}",
                                          deps=(chips_pending[l, "mix"][4],))

        small[l] = dict(norm_mix=d_norm_mix[0], conv_dw_w=d_taps[:CONV_W],
                        conv_dw_b=d_conv_b[0], conv_ln_g=d_conv_g[0], conv_ln_b=d_conv_bb[0], sgu_ln_g=d_sgu_g[0],
                        sgu_ln_b=d_sgu_bb[0], sgu_w=d_sgu_w, sgu_b=d_sgu_bias[:, :SGU_HEADS].T,
                        attn_sinks=d_sinks[0, :N_Q_HEADS], norm_ffn2=d_norm_ffn2[0])
        if l == 0:
            early_names = ["norm_mix", "conv_dw_w", "conv_dw_b", "conv_ln_g", "conv_ln_b", "sgu_ln_g", "sgu_ln_b",
                           "sgu_w", "sgu_b", "attn_sinks", "norm_ffn2"]
            early = [jnp.stack([small[k][n] for k in range(L)]) for n in early_names]
            early += [d_final_norm[0], loss.reshape(1)]
            early_shapes = [a.shape for a in early]
            early = _pack(early)
            early_pending = split_start([early], [lax.empty((N_DEV,) + early.shape, f32)], _routes_all,
                                        "small_start", deps=(dxb,))
            early_token = (early_pending[4],)
        else:
            early_token = ()

        dx, dxb, d_norm_ffn1, chips_pending[l, "ffn1"] = ffn_bwd(
            dx, dxb, weights[l, "ffn1"], saved[l, "ffn1"], row(norm_ffn1[l]), f"ffn1_{l}", early_token)
        small[l]["norm_ffn1"] = d_norm_ffn1[0]

    grad_x = dx[None]
    late = _pack([jnp.stack([small[l]["norm_ffn1"] for l in range(L)])])
    late_pending = split_start([late], [lax.empty((N_DEV,) + late.shape, f32)], _routes_all, "late_start", deps=(dx,))

    def landed(grp, after):
        got = [split_wait(chips_pending[l, grp], after, _routes_chips, f"chips_wait_{grp}_{l}") for l in range(L)]
        return [g[0] for g in got], [g[1] for g in got]

    given = dict(norm_ffn1=(norm_ffn1, m_norm_ffn1, v_norm_ffn1), norm_mix=(norm_mix, m_norm_mix, v_norm_mix),
                 conv_dw_w=(conv_dw_w, m_conv_dw_w, v_conv_dw_w), conv_dw_b=(conv_dw_b, m_conv_dw_b, v_conv_dw_b),
                 conv_ln_g=(conv_ln_g, m_conv_ln_g, v_conv_ln_g), conv_ln_b=(conv_ln_b, m_conv_ln_b, v_conv_ln_b),
                 sgu_ln_g=(sgu_ln_g, m_sgu_ln_g, v_sgu_ln_g), sgu_ln_b=(sgu_ln_b, m_sgu_ln_b, v_sgu_ln_b),
                 sgu_w=(sgu_w, m_sgu_w, v_sgu_w), sgu_b=(sgu_b, m_sgu_b, v_sgu_b),
                 attn_sinks=(attn_sinks, m_attn_sinks, v_attn_sinks), norm_ffn2=(norm_ffn2, m_norm_ffn2, v_norm_ffn2),
                 final_norm=(final_norm, m_final_norm, v_final_norm))

    def small_update(pending, after, names, shapes, tag):
        (own,), (others,) = split_wait(pending, after, _routes_all, f"{tag}_wait")
        total = _unpack(sum_slots(place_own(others, own, me_idx, "rows", f"{tag}_own"), f"{tag}_sum"), shapes)
        g = dict(zip(names, total))
        if "conv_dw_w" in g:
            g["conv_dw_w"] = lax.dynamic_slice_in_dim(g["conv_dw_w"], me * cw, cw, axis=2)
        upd_names = [n for n in names if n in given]
        upd_shapes = [given[n][0].shape for n in upd_names]
        packed = [_pack([g[n] for n in upd_names])[None]] + [_pack([given[n][k] for n in upd_names])[None]
                                                              for k in range(3)]
        res = adamw([packed[0]], None, no_chip, packed[1], packed[2], packed[3], f"{tag}_adamw")
        return g, [dict(zip(upd_names, _unpack(r[0], upd_shapes))) for r in res], res[0]

    big = {}
    done = [dx]

    def big_update(grp, names):
        sums, lands = landed(grp, done)
        for idx, (name, w, m, v) in enumerate(names):
            big[name] = adamw([lands[l][idx] for l in range(L)], [sums[l][idx] for l in range(L)], chip, w, m, v,
                              f"adamw_{name}")
            done.append(big[name][0])

    big_update("ffn2", (("ffn2_w_in", ffn2_w_in, m_ffn2_w_in, v_ffn2_w_in),
                        ("ffn2_w_out", ffn2_w_out, m_ffn2_w_out, v_ffn2_w_out)))
    big_update("mix", (("w_in", w_in, m_w_in, v_w_in), ("w_out", w_out, m_w_out, v_w_out)))
    g_early, upd_early, marker = small_update(early_pending, done, early_names + ["final_norm", "loss"], early_shapes,
                                              "small")
    done.append(marker)
    big_update("ffn1", (("ffn1_w_in", ffn1_w_in, m_ffn1_w_in, v_ffn1_w_in),
                        ("ffn1_w_out", ffn1_w_out, m_ffn1_w_out, v_ffn1_w_out)))
    _, upd_late, _ = small_update(late_pending, done, ["norm_ffn1"], [(L, D)], "late")
    upd = [{**upd_early[k], **upd_late[k]} for k in range(4)]

    order = ["norm_ffn1", "ffn1_w_in", "ffn1_w_out", "norm_mix", "w_in", "conv_dw_w", "conv_dw_b", "conv_ln_g",
             "conv_ln_b", "sgu_ln_g", "sgu_ln_b", "sgu_w", "sgu_b", "attn_sinks", "w_out", "norm_ffn2", "ffn2_w_in",
             "ffn2_w_out", "final_norm"]
    outs = [g_early["loss"].reshape(()), grad_x]
    for k in range(4):
        outs += [big[n][k] if n in big else upd[k][n] for n in order]
    return tuple(outs)
```

```python
import functools
import math

import jax
import jax.numpy as jnp
from jax import lax
from jax.experimental import pallas as pl
from jax.experimental.pallas import tpu as pltpu

f32 = jnp.float32
bf16 = jnp.bfloat16

N_DEV = 8
HEAD_DIM = 64
N_Q_HEADS = 16
N_KV_HEADS = 4
GQ = N_Q_HEADS // N_KV_HEADS
BLK = 128
ROT_DIM = 16
ROPE_THETA = 500000.0
CONV_W = 31
CONV_CH = 512
SGU_CH = 512
SGU_HEADS = 8
Q_END = N_Q_HEADS * HEAD_DIM
KV_W = 2 * N_KV_HEADS * HEAD_DIM
V_END = Q_END + KV_W
IN_COLS = V_END + 2 * CONV_CH + 2 * SGU_CH
HALO = 32
NORM_EPS = 1e-5
FFN_RES = 0.5
ADAM_LR, ADAM_B1, ADAM_B2, ADAM_EPS, ADAM_WD, ADAM_STEP = 0.001, 0.9, 0.999, 1e-08, 0.01, 10
LANE = 128
VMEM_LIMIT = 56 * 2 ** 20
MESH = pl.DeviceIdType.MESH

NN = (((1,), (0,)), ((), ()))
NT = (((1,), (1,)), ((), ()))
TN = (((0,), (0,)), ((), ()))


def _tile(dim, pref):
    t = min(pref, dim)
    while dim % t:
        t //= 2
    return t


def _params(*sem):
    return pltpu.CompilerParams(dimension_semantics=sem, vmem_limit_bytes=VMEM_LIMIT)


def _sig(x):
    return 1.0 / (1.0 + jnp.exp(-x))


def _dot(a, b, dims=NN):
    return lax.dot_general(a, b, dims, preferred_element_type=f32)


def _rowsum8(x):
    return x.reshape(x.shape[0] // 8, 8, x.shape[1]).sum(axis=0)


def rmsnorm_fwd(x, g, name, deps=()):
    S, D = x.shape
    tm = _tile(S, 512)

    def body(x_ref, g_ref, *rest):
        o_ref, ot_ref = rest[-2:]
        xv = x_ref[...]
        r = lax.rsqrt(jnp.mean(xv * xv, axis=-1, keepdims=True) + NORM_EPS)
        hb = (xv * r * g_ref[...]).astype(bf16)
        o_ref[...] = hb
        ot_ref[...] = hb.T

    return pl.pallas_call(
        body, name=name, grid=(S // tm,),
        in_specs=[pl.BlockSpec((tm, D), lambda i: (i, 0)), pl.BlockSpec((1, D), lambda i: (0, 0))] + [ANY] * len(deps),
        out_specs=[pl.BlockSpec((tm, D), lambda i: (i, 0)), pl.BlockSpec((D, tm), lambda i: (0, i))],
        out_shape=[jax.ShapeDtypeStruct((S, D), bf16), jax.ShapeDtypeStruct((D, S), bf16)],
        compiler_params=_params("parallel"),
    )(x, g, *deps)


def rmsnorm_bwd(dh, x, g, dres, name, deps=()):
    S, D = x.shape
    tm = _tile(S, 256)
    n = S // tm

    def body(dh_ref, x_ref, g_ref, dres_ref, *rest):
        dx_ref, dxb_ref, dg_ref, acc = rest[-4:]
        i = pl.program_id(0)

        @pl.when(i == 0)
        def _():
            acc[...] = jnp.zeros_like(acc)

        xv = x_ref[...]
        r = lax.rsqrt(jnp.mean(xv * xv, axis=-1, keepdims=True) + NORM_EPS)
        xh = xv * r
        dy = dh_ref[...]
        gy = dy * g_ref[...]
        dx = dres_ref[...] + r * (gy - xh * jnp.mean(gy * xh, axis=-1, keepdims=True))
        dx_ref[...] = dx
        dxb_ref[...] = dx.astype(bf16)
        acc[...] += _rowsum8(dy * xh)

        @pl.when(i == n - 1)
        def _():
            dg_ref[...] = jnp.sum(acc[...], axis=0, keepdims=True)

    row = pl.BlockSpec((tm, D), lambda i: (i, 0))
    vec = pl.BlockSpec((1, D), lambda i: (0, 0))
    return pl.pallas_call(
        body, name=name, grid=(n,),
        in_specs=[row, row, vec, row] + [ANY] * len(deps),
        out_specs=[row, row, vec],
        out_shape=[jax.ShapeDtypeStruct((S, D), f32), jax.ShapeDtypeStruct((S, D), bf16),
                   jax.ShapeDtypeStruct((1, D), f32)],
        scratch_shapes=[pltpu.VMEM((8, D), f32)],
        compiler_params=_params("arbitrary"),
    )(dh, x, g, dres, *deps)


def final_loss(x, g, tgt, name):
    S, D = x.shape
    tm = _tile(S, 256)
    n = S // tm

    def body(x_ref, g_ref, t_ref, dx_ref, dxb_ref, dg_ref, loss_ref, acc):
        i = pl.program_id(0)

        @pl.when(i == 0)
        def _():
            acc[...] = jnp.zeros_like(acc)
            loss_ref[...] = jnp.zeros_like(loss_ref)

        xv = x_ref[...]
        gv = g_ref[...]
        r = lax.rsqrt(jnp.mean(xv * xv, axis=-1, keepdims=True) + NORM_EPS)
        xh = xv * r
        diff = xh * gv - t_ref[...]
        tok = jnp.mean(diff * diff, axis=-1, keepdims=True)
        loss_ref[...] += 0.5 * jnp.sum(tok, axis=0, keepdims=True)
        dy = diff / D
        gy = dy * gv
        dx = r * (gy - xh * jnp.mean(gy * xh, axis=-1, keepdims=True))
        dx_ref[...] = dx
        dxb_ref[...] = dx.astype(bf16)
        acc[...] += _rowsum8(dy * xh)

        @pl.when(i == n - 1)
        def _():
            dg_ref[...] = jnp.sum(acc[...], axis=0, keepdims=True)

    row = pl.BlockSpec((tm, D), lambda i: (i, 0))
    vec = pl.BlockSpec((1, D), lambda i: (0, 0))
    return pl.pallas_call(
        body, name=name, grid=(n,),
        in_specs=[row, vec, row],
        out_specs=[row, row, vec, pl.BlockSpec((1, 1), lambda i: (0, 0))],
        out_shape=[jax.ShapeDtypeStruct((S, D), f32), jax.ShapeDtypeStruct((S, D), bf16),
                   jax.ShapeDtypeStruct((1, D), f32), jax.ShapeDtypeStruct((1, 1), f32)],
        scratch_shapes=[pltpu.VMEM((8, D), f32)],
        compiler_params=_params("arbitrary"),
    )(x, g, tgt)


def ffn_in(h, w2, name):
    S, D = h.shape
    F = w2.shape[2]
    tm, tn = _tile(S, 512), _tile(F, 512)

    def body(h_ref, w_ref, gu_ref, a_ref, at_ref):
        hv = h_ref[...]
        g = _dot(hv, w_ref[0])
        u = _dot(hv, w_ref[1])
        gu_ref[0] = g.astype(bf16)
        gu_ref[1] = u.astype(bf16)
        a = (g * _sig(g) * u).astype(bf16)
        a_ref[...] = a
        at_ref[...] = a.T

    return pl.pallas_call(
        body, name=name, grid=(S // tm, F // tn),
        in_specs=[pl.BlockSpec((tm, D), lambda i, j: (i, 0)), pl.BlockSpec((2, D, tn), lambda i, j: (0, 0, j))],
        out_specs=[pl.BlockSpec((2, tm, tn), lambda i, j: (0, i, j)), pl.BlockSpec((tm, tn), lambda i, j: (i, j)),
                   pl.BlockSpec((tn, tm), lambda i, j: (j, i))],
        out_shape=[jax.ShapeDtypeStruct((2, S, F), bf16), jax.ShapeDtypeStruct((S, F), bf16),
                   jax.ShapeDtypeStruct((F, S), bf16)],
        compiler_params=_params("parallel", "parallel"),
    )(h, w2)


def mm_res(a, w, x, scale, name):
    S, K = a.shape
    N = w.shape[1]
    tm, tn = _tile(S, 512), _tile(N, 512)

    def body(a_ref, w_ref, x_ref, o_ref):
        o_ref[...] = x_ref[...] + scale * _dot(a_ref[...], w_ref[...])

    return pl.pallas_call(
        body, name=name, grid=(S // tm, N // tn),
        in_specs=[pl.BlockSpec((tm, K), lambda i, j: (i, 0)), pl.BlockSpec((K, tn), lambda i, j: (0, j)),
                  pl.BlockSpec((tm, tn), lambda i, j: (i, j))],
        out_specs=pl.BlockSpec((tm, tn), lambda i, j: (i, j)),
        out_shape=jax.ShapeDtypeStruct((S, N), f32),
        compiler_params=_params("parallel", "parallel"),
    )(a, w, x)


def mm_nn(a, b, name, out_dtype=f32, scale=1.0, deps=()):
    M, K = a.shape
    G, _, N = b.shape
    tm, tn = _tile(M, 512), _tile(N, 512)

    def body(a_ref, b_ref, *rest):
        acc = _dot(a_ref[...], b_ref[...])
        rest[-1][...] = (acc if scale == 1.0 else scale * acc).astype(out_dtype)

    return pl.pallas_call(
        body, name=name, grid=(G, M // tm, N // tn),
        in_specs=[pl.BlockSpec((tm, K), lambda g, i, j: (i, 0)),
                  pl.BlockSpec((None, K, tn), lambda g, i, j: (g, 0, j))] + [ANY] * len(deps),
        out_specs=pl.BlockSpec((None, tm, tn), lambda g, i, j: (g, i, j)),
        out_shape=jax.ShapeDtypeStruct((G, M, N), out_dtype),
        compiler_params=_params("parallel", "parallel", "parallel"),
    )(a, b, *deps)


def mm_nt(a, w, name, deps=()):
    G, S, K = a.shape
    N = w.shape[1]
    tm, tn = _tile(S, 512), _tile(N, 512)

    def body(a_ref, w_ref, *rest):
        o_ref = rest[-1]
        part = _dot(a_ref[...], w_ref[...], NT)
        if G == 1:
            o_ref[...] = part
        else:
            g = pl.program_id(2)

            @pl.when(g == 0)
            def _():
                o_ref[...] = part

            @pl.when(g > 0)
            def _():
                o_ref[...] += part

    return pl.pallas_call(
        body, name=name, grid=(S // tm, N // tn, G),
        in_specs=[pl.BlockSpec((None, tm, K), lambda i, j, g: (g, i, 0)),
                  pl.BlockSpec((None, tn, K), lambda i, j, g: (g, j, 0))] + [ANY] * len(deps),
        out_specs=pl.BlockSpec((tm, tn), lambda i, j, g: (i, j)),
        out_shape=jax.ShapeDtypeStruct((S, N), f32),
        compiler_params=_params("parallel", "parallel", "arbitrary"),
    )(a, w, *deps)


def ffn_dact(dx, wout, gu, name):
    S, D = dx.shape
    F = wout.shape[0]
    tm, tn = _tile(S, 512), _tile(F, 512)

    def body(dx_ref, w_ref, gu_ref, o_ref):
        da = FFN_RES * _dot(dx_ref[...], w_ref[...], NT)
        g = gu_ref[0].astype(f32)
        u = gu_ref[1].astype(f32)
        sg = _sig(g)
        o_ref[0] = (da * u * (sg * (1.0 + g * (1.0 - sg)))).astype(bf16)
        o_ref[1] = (da * (g * sg)).astype(bf16)

    return pl.pallas_call(
        body, name=name, grid=(S // tm, F // tn),
        in_specs=[pl.BlockSpec((tm, D), lambda i, j: (i, 0)), pl.BlockSpec((tn, D), lambda i, j: (j, 0)),
                  pl.BlockSpec((2, tm, tn), lambda i, j: (0, i, j))],
        out_specs=pl.BlockSpec((2, tm, tn), lambda i, j: (0, i, j)),
        out_shape=jax.ShapeDtypeStruct((2, S, F), bf16),
        compiler_params=_params("parallel", "parallel"),
    )(dx, wout, gu)


def _rope(t, c, s1, s2):
    w = t.shape[1]
    return t * c + pltpu.roll(t, 8, 1) * s1 + pltpu.roll(t, w - 8, 1) * s2


def _rope_t(d, c, s1, s2):
    w = d.shape[1]
    return d * c + pltpu.roll(d * s1, w - 8, 1) + pltpu.roll(d * s2, 8, 1)


def _attn_mask(n):
    qi = lax.broadcasted_iota(jnp.int32, (BLK, 2 * BLK), 0)
    kj = lax.broadcasted_iota(jnp.int32, (BLK, 2 * BLK), 1)
    dist = qi + BLK - kj
    return (dist >= 0) & (dist < BLK) & ((kj >= BLK) | (n > 0))


def _softmax_sink(s, valid, sk):
    s = jnp.where(valid, s, -1e30)
    m = jnp.maximum(jnp.max(s, axis=-1, keepdims=True), sk)
    e = jnp.exp(s - m)
    es = jnp.exp(sk - m)
    inv = 1.0 / (jnp.sum(e, axis=-1, keepdims=True) + es)
    return e * inv, es * inv


def attn_fwd(p, rope_c, rope_s1, rope_s2, sinks, name):
    S = p.shape[0]
    nb = S // BLK
    kvb = Q_END // KV_W

    def body(sink_ref, q_ref, kvc_ref, kvp_ref, cc_ref, s1c_ref, s2c_ref, cp_ref, s1p_ref, s2p_ref, o_ref):
        n = pl.program_id(0)
        cc, s1c, s2c = cc_ref[...], s1c_ref[...], s2c_ref[...]
        cp, s1p, s2p = cp_ref[...], s1p_ref[...], s2p_ref[...]
        q = _rope(q_ref[...], jnp.tile(cc, (1, 8)), jnp.tile(s1c, (1, 8)), jnp.tile(s2c, (1, 8)))
        kc = _rope(kvc_ref[:, :256], jnp.tile(cc, (1, 2)), jnp.tile(s1c, (1, 2)), jnp.tile(s2c, (1, 2)))
        kp = _rope(kvp_ref[:, :256], jnp.tile(cp, (1, 2)), jnp.tile(s1p, (1, 2)), jnp.tile(s2p, (1, 2)))
        k = jnp.concatenate([kp, kc], axis=0).astype(bf16)
        v = jnp.concatenate([kvp_ref[:, 256:], kvc_ref[:, 256:]], axis=0).astype(bf16)
        q = q.astype(bf16)
        valid = _attn_mask(n)
        for h in range(N_KV_HEADS):
            kh = k[:, h * HEAD_DIM:(h + 1) * HEAD_DIM]
            vh = v[:, h * HEAD_DIM:(h + 1) * HEAD_DIM]
            for g in range(GQ):
                hq = h * GQ + g
                qh = q[:, hq * HEAD_DIM:(hq + 1) * HEAD_DIM]
                s = _dot(qh, kh, NT) * (HEAD_DIM ** -0.5)
                pr, _ = _softmax_sink(s, valid, sink_ref[hq])
                o = _dot(pr.astype(bf16), vh)
                o_ref[:, hq * HEAD_DIM:(hq + 1) * HEAD_DIM] = o.astype(bf16)

    tab_c = pl.BlockSpec((BLK, LANE), lambda n: (n, 0))
    tab_p = pl.BlockSpec((BLK, LANE), lambda n: (jnp.maximum(n - 1, 0), 0))
    return pl.pallas_call(
        body, name=name, grid=(nb,),
        in_specs=[pl.BlockSpec(memory_space=pltpu.SMEM),
                  pl.BlockSpec((BLK, Q_END), lambda n: (n, 0)),
                  pl.BlockSpec((BLK, KV_W), lambda n: (n, kvb)),
                  pl.BlockSpec((BLK, KV_W), lambda n: (jnp.maximum(n - 1, 0), kvb)),
                  tab_c, tab_c, tab_c, tab_p, tab_p, tab_p],
        out_specs=pl.BlockSpec((BLK, Q_END), lambda n: (n, 0)),
        out_shape=jax.ShapeDtypeStruct((S, Q_END), bf16),
        compiler_params=_params("parallel"),
    )(sinks, p, p, p, rope_c, rope_s1, rope_s2, rope_c, rope_s1, rope_s2)


def attn_bwd(p, dcat, rope_c, rope_s1, rope_s2, sinks, name):
    S = p.shape[0]
    nb = S // BLK
    kvb = Q_END // KV_W

    def body(sink_ref, q_ref, kvc_ref, kvp_ref, do_ref, cc_ref, s1c_ref, s2c_ref, cp_ref, s1p_ref, s2p_ref,
             dq_ref, dkv_ref, dsink_ref, carry, dq_scr, dkv_scr):
        n = pl.program_id(0)

        @pl.when(n == 0)
        def _():
            carry[...] = jnp.zeros_like(carry)
            dsink_ref[...] = jnp.zeros_like(dsink_ref)

        cp, s1p, s2p = cp_ref[...], s1p_ref[...], s2p_ref[...]
        cp2, s1p2, s2p2 = jnp.tile(cp, (1, 2)), jnp.tile(s1p, (1, 2)), jnp.tile(s2p, (1, 2))

        @pl.when(n < nb)
        def _():
            cc, s1c, s2c = cc_ref[...], s1c_ref[...], s2c_ref[...]
            cc8, s1c8, s2c8 = jnp.tile(cc, (1, 8)), jnp.tile(s1c, (1, 8)), jnp.tile(s2c, (1, 8))
            q = _rope(q_ref[...], cc8, s1c8, s2c8).astype(bf16)
            kc = _rope(kvc_ref[:, :256], jnp.tile(cc, (1, 2)), jnp.tile(s1c, (1, 2)), jnp.tile(s2c, (1, 2)))
            kp = _rope(kvp_ref[:, :256], cp2, s1p2, s2p2)
            k = jnp.concatenate([kp, kc], axis=0).astype(bf16)
            v = jnp.concatenate([kvp_ref[:, 256:], kvc_ref[:, 256:]], axis=0).astype(bf16)
            do = do_ref[...].astype(bf16)
            valid = _attn_mask(n)
            lane = lax.broadcasted_iota(jnp.int32, (1, LANE), 1)
            dsink = jnp.zeros((1, LANE), f32)
            for h in range(N_KV_HEADS):
                kh = k[:, h * HEAD_DIM:(h + 1) * HEAD_DIM]
                vh = v[:, h * HEAD_DIM:(h + 1) * HEAD_DIM]
                dkh = jnp.zeros((2 * BLK, HEAD_DIM), f32)
                dvh = jnp.zeros((2 * BLK, HEAD_DIM), f32)
                for g in range(GQ):
                    hq = h * GQ + g
                    qh = q[:, hq * HEAD_DIM:(hq + 1) * HEAD_DIM]
                    doh = do[:, hq * HEAD_DIM:(hq + 1) * HEAD_DIM]
                    s = _dot(qh, kh, NT) * (HEAD_DIM ** -0.5)
                    pr, ps = _softmax_sink(s, valid, sink_ref[hq])
                    dpr = _dot(doh, vh, NT)
                    dvh = dvh + _dot(pr.astype(bf16), doh, TN)
                    row = jnp.sum(pr * dpr, axis=-1, keepdims=True)
                    ds = (pr * (dpr - row) * (HEAD_DIM ** -0.5)).astype(bf16)
                    dsink = dsink + jnp.where(lane == hq, -jnp.sum(ps * row, axis=0, keepdims=True), 0.0)
                    dq_scr[:, hq * HEAD_DIM:(hq + 1) * HEAD_DIM] = _dot(ds, kh)
                    dkh = dkh + _dot(ds, qh, TN)
                dkv_scr[:, h * HEAD_DIM:(h + 1) * HEAD_DIM] = dkh
                dkv_scr[:, 256 + h * HEAD_DIM:256 + (h + 1) * HEAD_DIM] = dvh
            dsink_ref[...] += dsink
            dq_ref[...] = _rope_t(dq_scr[...], cc8, s1c8, s2c8).astype(bf16)

        prev = carry[...]

        @pl.when(n < nb)
        def _():
            dkv_scr[pl.ds(0, BLK), :] = dkv_scr[pl.ds(0, BLK), :] + prev

        @pl.when(n == nb)
        def _():
            dkv_scr[pl.ds(0, BLK), :] = prev

        done = dkv_scr[pl.ds(0, BLK), :]
        dkv_ref[:, :256] = _rope_t(done[:, :256], cp2, s1p2, s2p2).astype(bf16)
        dkv_ref[:, 256:] = done[:, 256:].astype(bf16)

        @pl.when(n < nb)
        def _():
            carry[...] = dkv_scr[pl.ds(BLK, BLK), :]

    cur = lambda n: jnp.minimum(n, nb - 1)
    prv = lambda n: jnp.maximum(n - 1, 0)
    tab_c = pl.BlockSpec((BLK, LANE), lambda n: (cur(n), 0))
    tab_p = pl.BlockSpec((BLK, LANE), lambda n: (prv(n), 0))
    return pl.pallas_call(
        body, name=name, grid=(nb + 1,),
        in_specs=[pl.BlockSpec(memory_space=pltpu.SMEM),
                  pl.BlockSpec((BLK, Q_END), lambda n: (cur(n), 0)),
                  pl.BlockSpec((BLK, KV_W), lambda n: (cur(n), kvb)),
                  pl.BlockSpec((BLK, KV_W), lambda n: (prv(n), kvb)),
                  pl.BlockSpec((BLK, Q_END), lambda n: (cur(n), 0)),
                  tab_c, tab_c, tab_c, tab_p, tab_p, tab_p],
        out_specs=[pl.BlockSpec((BLK, Q_END), lambda n: (cur(n), 0)),
                   pl.BlockSpec((BLK, KV_W), lambda n: (prv(n), 0)),
                   pl.BlockSpec((1, LANE), lambda n: (0, 0))],
        out_shape=[jax.ShapeDtypeStruct((S, Q_END), bf16), jax.ShapeDtypeStruct((S, KV_W), bf16),
                   jax.ShapeDtypeStruct((1, LANE), f32)],
        scratch_shapes=[pltpu.VMEM((BLK, KV_W), f32), pltpu.VMEM((BLK, Q_END), f32), pltpu.VMEM((2 * BLK, KV_W), f32)],
        compiler_params=_params("arbitrary"),
    )(sinks, p, p, p, dcat, rope_c, rope_s1, rope_s2, rope_c, rope_s1, rope_s2)


A1_BLK = V_END // CONV_CH
A2_BLK = A1_BLK + 1


def _ln_stats(y):
    mu = jnp.mean(y, axis=-1, keepdims=True)
    xc = y - mu
    rstd = lax.rsqrt(jnp.mean(xc * xc, axis=-1, keepdims=True) + NORM_EPS)
    return xc * rstd, rstd


def conv_fwd(p, w, b, lng, lnb, name):
    S = p.shape[0]
    T = _tile(S, 256)
    r = T // HALO

    def body(a1_ref, a2_ref, h1_ref, h2_ref, w_ref, b_ref, g_ref, bb_ref, o_ref, y_ref, scr):
        i = pl.program_id(0)
        halo = h1_ref[...] * _sig(h2_ref[...])
        scr[pl.ds(0, HALO), :] = jnp.where(i > 0, halo, 0.0)
        scr[pl.ds(HALO, T), :] = a1_ref[...] * _sig(a2_ref[...])
        acc = jnp.zeros((T, CONV_CH), f32) + b_ref[...]
        for j in range(CONV_W):
            acc = acc + scr[pl.ds(HALO - (CONV_W - 1) + j, T), :] * w_ref[j:j + 1, :]
        y_ref[...] = acc
        yh, _ = _ln_stats(acc)
        z = yh * g_ref[...] + bb_ref[...]
        o_ref[...] = (z * _sig(z)).astype(bf16)

    vec = pl.BlockSpec((1, CONV_CH), lambda i: (0, 0))
    halo_map = lambda i: jnp.maximum(i * r - 1, 0)
    return pl.pallas_call(
        body, name=name, grid=(S // T,),
        in_specs=[pl.BlockSpec((T, CONV_CH), lambda i: (i, A1_BLK)), pl.BlockSpec((T, CONV_CH), lambda i: (i, A2_BLK)),
                  pl.BlockSpec((HALO, CONV_CH), lambda i: (halo_map(i), A1_BLK)),
                  pl.BlockSpec((HALO, CONV_CH), lambda i: (halo_map(i), A2_BLK)),
                  pl.BlockSpec((HALO, CONV_CH), lambda i: (0, 0)), vec, vec, vec],
        out_specs=[pl.BlockSpec((T, CONV_CH), lambda i: (i, 0)), pl.BlockSpec((T, CONV_CH), lambda i: (i, 0))],
        out_shape=[jax.ShapeDtypeStruct((S, CONV_CH), bf16), jax.ShapeDtypeStruct((S, CONV_CH), f32)],
        scratch_shapes=[pltpu.VMEM((T + HALO, CONV_CH), f32)],
        compiler_params=_params("parallel"),
    )(p, p, p, p, w, b, lng, lnb)


def conv_bwd(p, y, dcat, w, lng, lnb, name):
    S = p.shape[0]
    T = _tile(S, 256)
    n = S // T
    r = T // HALO
    dcb = Q_END // CONV_CH

    def body(a1_ref, a2_ref, h1_ref, h2_ref, y_ref, yn_ref, do_ref, don_ref, w_ref, g_ref, bb_ref,
             da_ref, dw_ref, db_ref, dg_ref, dbb_ref, scr_h, scr_dy, acc_b, acc_g, acc_bb):
        i = pl.program_id(0)

        @pl.when(i == 0)
        def _():
            dw_ref[...] = jnp.zeros_like(dw_ref)
            acc_b[...] = jnp.zeros_like(acc_b)
            acc_g[...] = jnp.zeros_like(acc_g)
            acc_bb[...] = jnp.zeros_like(acc_bb)

        gv, bv = g_ref[...], bb_ref[...]

        def ln_silu_bwd(yv, dout):
            yh, rstd = _ln_stats(yv)
            z = yh * gv + bv
            sg = _sig(z)
            dz = dout * (sg * (1.0 + z * (1.0 - sg)))
            gz = dz * gv
            dy = rstd * (gz - jnp.mean(gz, axis=-1, keepdims=True) - yh * jnp.mean(gz * yh, axis=-1, keepdims=True))
            return dy, dz, yh

        dy, dz, yh = ln_silu_bwd(y_ref[...], do_ref[...])
        dyn, _, _ = ln_silu_bwd(yn_ref[...], don_ref[...])
        acc_g[...] += _rowsum8(dz * yh)
        acc_bb[...] += _rowsum8(dz)
        acc_b[...] += _rowsum8(dy)
        scr_dy[pl.ds(0, T), :] = dy
        scr_dy[pl.ds(T, HALO), :] = jnp.where(i < n - 1, dyn, 0.0)
        a1, a2 = a1_ref[...], a2_ref[...]
        sg2 = _sig(a2)
        halo = h1_ref[...] * _sig(h2_ref[...])
        scr_h[pl.ds(0, HALO), :] = jnp.where(i > 0, halo, 0.0)
        scr_h[pl.ds(HALO, T), :] = a1 * sg2
        dh = jnp.zeros((T, CONV_CH), f32)
        for j in range(CONV_W):
            dh = dh + scr_dy[pl.ds(CONV_W - 1 - j, T), :] * w_ref[j:j + 1, :]
            dw_ref[j:j + 1, :] += jnp.sum(dy * scr_h[pl.ds(HALO - (CONV_W - 1) + j, T), :], axis=0, keepdims=True)
        da_ref[:, :CONV_CH] = (dh * sg2).astype(bf16)
        da_ref[:, CONV_CH:] = (dh * a1 * sg2 * (1.0 - sg2)).astype(bf16)

        @pl.when(i == n - 1)
        def _():
            db_ref[...] = jnp.sum(acc_b[...], axis=0, keepdims=True)
            dg_ref[...] = jnp.sum(acc_g[...], axis=0, keepdims=True)
            dbb_ref[...] = jnp.sum(acc_bb[...], axis=0, keepdims=True)

    vec = pl.BlockSpec((1, CONV_CH), lambda i: (0, 0))
    tap = pl.BlockSpec((HALO, CONV_CH), lambda i: (0, 0))
    prev_map = lambda i: jnp.maximum(i * r - 1, 0)
    next_map = lambda i: jnp.minimum((i + 1) * r, S // HALO - 1)
    return pl.pallas_call(
        body, name=name, grid=(n,),
        in_specs=[pl.BlockSpec((T, CONV_CH), lambda i: (i, A1_BLK)), pl.BlockSpec((T, CONV_CH), lambda i: (i, A2_BLK)),
                  pl.BlockSpec((HALO, CONV_CH), lambda i: (prev_map(i), A1_BLK)),
                  pl.BlockSpec((HALO, CONV_CH), lambda i: (prev_map(i), A2_BLK)),
                  pl.BlockSpec((T, CONV_CH), lambda i: (i, 0)),
                  pl.BlockSpec((HALO, CONV_CH), lambda i: (next_map(i), 0)),
                  pl.BlockSpec((T, CONV_CH), lambda i: (i, dcb)),
                  pl.BlockSpec((HALO, CONV_CH), lambda i: (next_map(i), dcb)),
                  tap, vec, vec],
        out_specs=[pl.BlockSpec((T, 2 * CONV_CH), lambda i: (i, 0)), tap, vec, vec, vec],
        out_shape=[jax.ShapeDtypeStruct((S, 2 * CONV_CH), bf16), jax.ShapeDtypeStruct((HALO, CONV_CH), f32),
                   jax.ShapeDtypeStruct((1, CONV_CH), f32), jax.ShapeDtypeStruct((1, CONV_CH), f32),
                   jax.ShapeDtypeStruct((1, CONV_CH), f32)],
        scratch_shapes=[pltpu.VMEM((T + HALO, CONV_CH), f32), pltpu.VMEM((T + HALO, CONV_CH), f32),
                        pltpu.VMEM((8, CONV_CH), f32), pltpu.VMEM((8, CONV_CH), f32), pltpu.VMEM((8, CONV_CH), f32)],
        compiler_params=_params("arbitrary"),
    )(p, p, p, p, y, y, dcat, dcat, w, lng, lnb)


U_BLK = (V_END + 2 * CONV_CH) // SGU_CH
SV_BLK = U_BLK + 1


def _tril(w, transposed=False):
    row = lax.broadcasted_iota(jnp.int32, (BLK, BLK), 0)
    col = lax.broadcasted_iota(jnp.int32, (BLK, BLK), 1)
    keep = (col >= row) if transposed else (row >= col)
    return jnp.where(keep, w, 0.0)


def sgu_fwd(p, lng, lnb, w, bias, name):
    S = p.shape[0]
    T = _tile(S, 256)

    def body(u_ref, v_ref, g_ref, bb_ref, w_ref, bias_ref, o_ref):
        yh, _ = _ln_stats(v_ref[...])
        v = (yh * g_ref[...] + bb_ref[...]).astype(bf16)
        low = lax.broadcasted_iota(jnp.int32, (BLK, LANE), 1) < HEAD_DIM
        for pr in range(SGU_HEADS // 2):
            lanes = pl.ds(pr * LANE, LANE)
            w0 = _tril(w_ref[2 * pr]).astype(bf16)
            w1 = _tril(w_ref[2 * pr + 1]).astype(bf16)
            for c in range(T // BLK):
                rows = pl.ds(c * BLK, BLK)
                vp = v[c * BLK:(c + 1) * BLK, pr * LANE:(pr + 1) * LANE]
                mixed = jnp.where(low, _dot(w0, vp), _dot(w1, vp)) + bias_ref[:, lanes]
                o_ref[rows, lanes] = (u_ref[rows, lanes] * mixed).astype(bf16)

    vec = pl.BlockSpec((1, SGU_CH), lambda i: (0, 0))
    return pl.pallas_call(
        body, name=name, grid=(S // T,),
        in_specs=[pl.BlockSpec((T, SGU_CH), lambda i: (i, U_BLK)), pl.BlockSpec((T, SGU_CH), lambda i: (i, SV_BLK)),
                  vec, vec, pl.BlockSpec((SGU_HEADS, BLK, BLK), lambda i: (0, 0, 0)),
                  pl.BlockSpec((BLK, SGU_CH), lambda i: (0, 0))],
        out_specs=pl.BlockSpec((T, SGU_CH), lambda i: (i, 0)),
        out_shape=jax.ShapeDtypeStruct((S, SGU_CH), bf16),
        compiler_params=_params("parallel"),
    )(p, p, lng, lnb, w, bias)


def sgu_bwd(p, dcat, lng, lnb, w, wt, bias, name):
    S = p.shape[0]
    T = _tile(S, 256)
    n = S // T
    dsb = (Q_END + CONV_CH) // SGU_CH

    def body(u_ref, v_ref, do_ref, g_ref, bb_ref, w_ref, wt_ref, bias_ref,
             da_ref, dw_ref, db_ref, dg_ref, dbb_ref, dv_scr, acc_bias, acc_g, acc_bb):
        i = pl.program_id(0)

        @pl.when(i == 0)
        def _():
            dw_ref[...] = jnp.zeros_like(dw_ref)
            acc_bias[...] = jnp.zeros_like(acc_bias)
            acc_g[...] = jnp.zeros_like(acc_g)
            acc_bb[...] = jnp.zeros_like(acc_bb)

        gv = g_ref[...]
        yh, rstd = _ln_stats(v_ref[...])
        v = (yh * gv + bb_ref[...]).astype(bf16)
        low = lax.broadcasted_iota(jnp.int32, (BLK, LANE), 1) < HEAD_DIM
        for pr in range(SGU_HEADS // 2):
            lanes = pl.ds(pr * LANE, LANE)
            w0 = _tril(w_ref[2 * pr]).astype(bf16)
            w1 = _tril(w_ref[2 * pr + 1]).astype(bf16)
            wt0 = _tril(wt_ref[2 * pr], True).astype(bf16)
            wt1 = _tril(wt_ref[2 * pr + 1], True).astype(bf16)
            dw0 = jnp.zeros((BLK, BLK), f32)
            dw1 = jnp.zeros((BLK, BLK), f32)
            for c in range(T // BLK):
                rows = pl.ds(c * BLK, BLK)
                vp = v[c * BLK:(c + 1) * BLK, pr * LANE:(pr + 1) * LANE]
                mixed = jnp.where(low, _dot(w0, vp), _dot(w1, vp)) + bias_ref[:, lanes]
                do = do_ref[rows, lanes]
                da_ref[rows, lanes] = (do * mixed).astype(bf16)
                dm = do * u_ref[rows, lanes]
                acc_bias[:, lanes] += dm
                dmb = dm.astype(bf16)
                dv_scr[rows, lanes] = jnp.where(low, _dot(wt0, dmb), _dot(wt1, dmb))
                zero = jnp.zeros_like(dmb)
                dw0 = dw0 + _dot(jnp.where(low, dmb, zero), vp, NT)
                dw1 = dw1 + _dot(jnp.where(low, zero, dmb), vp, NT)
            dw_ref[2 * pr] += _tril(dw0)
            dw_ref[2 * pr + 1] += _tril(dw1)
        dv = dv_scr[...]
        acc_g[...] += _rowsum8(dv * yh)
        acc_bb[...] += _rowsum8(dv)
        gz = dv * gv
        dvr = rstd * (gz - jnp.mean(gz, axis=-1, keepdims=True) - yh * jnp.mean(gz * yh, axis=-1, keepdims=True))
        da_ref[:, SGU_CH:] = dvr.astype(bf16)

        @pl.when(i == n - 1)
        def _():
            ch = lax.broadcasted_iota(jnp.int32, (SGU_CH, LANE), 0) // HEAD_DIM
            hd = lax.broadcasted_iota(jnp.int32, (SGU_CH, LANE), 1)
            fold = jnp.where(ch == hd, 1.0, 0.0).astype(f32)
            db_ref[...] = jnp.dot(acc_bias[...], fold, preferred_element_type=f32, precision=lax.Precision.HIGHEST)
            dg_ref[...] = jnp.sum(acc_g[...], axis=0, keepdims=True)
            dbb_ref[...] = jnp.sum(acc_bb[...], axis=0, keepdims=True)

    vec = pl.BlockSpec((1, SGU_CH), lambda i: (0, 0))
    wsp = pl.BlockSpec((SGU_HEADS, BLK, BLK), lambda i: (0, 0, 0))
    return pl.pallas_call(
        body, name=name, grid=(n,),
        in_specs=[pl.BlockSpec((T, SGU_CH), lambda i: (i, U_BLK)), pl.BlockSpec((T, SGU_CH), lambda i: (i, SV_BLK)),
                  pl.BlockSpec((T, SGU_CH), lambda i: (i, dsb)), vec, vec, wsp, wsp,
                  pl.BlockSpec((BLK, SGU_CH), lambda i: (0, 0))],
        out_specs=[pl.BlockSpec((T, 2 * SGU_CH), lambda i: (i, 0)), wsp,
                   pl.BlockSpec((BLK, LANE), lambda i: (0, 0)), vec, vec],
        out_shape=[jax.ShapeDtypeStruct((S, 2 * SGU_CH), bf16), jax.ShapeDtypeStruct((SGU_HEADS, BLK, BLK), f32),
                   jax.ShapeDtypeStruct((BLK, LANE), f32), jax.ShapeDtypeStruct((1, SGU_CH), f32),
                   jax.ShapeDtypeStruct((1, SGU_CH), f32)],
        scratch_shapes=[pltpu.VMEM((T, SGU_CH), f32), pltpu.VMEM((BLK, SGU_CH), f32),
                        pltpu.VMEM((8, SGU_CH), f32), pltpu.VMEM((8, SGU_CH), f32)],
        compiler_params=_params("arbitrary"),
    )(p, p, dcat, lng, lnb, w, wt, bias)


HBM = pl.BlockSpec(memory_space=pltpu.HBM)
SEM = pl.BlockSpec(memory_space=pltpu.SEMAPHORE)
ANY = pl.BlockSpec(memory_space=pl.ANY)
EFFECT = pltpu.SideEffectType.DATAFLOW_SIDE_EFFECTING


def _flip(x, y, c, k):
    px, py, pc = x ^ (k >> 2), y ^ ((k >> 1) & 1), c ^ (k & 1)
    return (px, py, pc), 4 * px + 2 * py + pc


def _routes_gather(x, y, c):
    me = 4 * x + 2 * y + c
    out = []
    for k in (1, 2, 4, 6):
        dev, idx = _flip(x, y, c, k)
        out.append((dev, None, me, idx))
    return out


def _routes_pair(x, y, c):
    dev, _ = _flip(x, y, c, 1)
    return [(dev, 2 * q + (1 - c), q, q) for q in range(N_DEV // 2)]


def _routes_chips(x, y, c):
    out = []
    for k in (2, 4, 6):
        dev, idx = _flip(x, y, c, k)
        out.append((dev, idx // 2, 2 * x + y, idx // 2))
    return out


def _routes_forward(x, y, c):
    sib, _ = _flip(x, y, c, 1)
    out = []
    for k in (2, 4, 6):
        _, idx = _flip(x, y, c, k)
        out.append((sib, idx, idx, idx ^ 1))
    return out


def _routes_all(x, y, c):
    me = 4 * x + 2 * y + c
    out = []
    for k in range(1, N_DEV):
        dev, idx = _flip(x, y, c, k)
        out.append((dev, None, me, idx))
    return out


def _slot(ref, slot, kind):
    if slot is None:
        return ref
    if kind == "cols":
        width = ref.shape[2] // (N_DEV // 2)
        return ref.at[slot // (N_DEV // 2), :, pl.ds(pl.multiple_of((slot % (N_DEV // 2)) * width, LANE), width)]
    return ref.at[slot]


def _copies(routes, srcs, lands, send_sems, recv_sems, incoming, src_kinds, land_kinds):
    x, y, c = lax.axis_index("x"), lax.axis_index("y"), lax.axis_index("c")
    out = []
    n = len(lands)
    if srcs is None:
        srcs, src_kinds = lands, land_kinds
    for k, (dev, src_slot, dst_slot, recv_slot) in enumerate(routes(x, y, c)):
        for a in range(n):
            out.append(pltpu.make_async_remote_copy(
                src_ref=_slot(srcs[a], src_slot, src_kinds[a]),
                dst_ref=_slot(lands[a], recv_slot if incoming else dst_slot, land_kinds[a]),
                send_sem=send_sems.at[k * n + a], recv_sem=recv_sems.at[k * n + a], device_id=dev, device_id_type=MESH))
    return out


def _pin(a):
    return pltpu.with_memory_space_constraint(a, pltpu.HBM)


def split_start(srcs, lands, routes, name, deps=(), src_kinds=None, land_kinds=None):
    n = len(lands)
    ns = 0 if srcs is None else n
    n_routes = len(routes(0, 0, 0))
    ops = ([] if srcs is None else list(srcs)) + list(lands)
    src_kinds = src_kinds or ["rows"] * n
    land_kinds = land_kinds or ["rows"] * n

    def body(*refs):
        src, land = (refs[:n] if ns else None), refs[ns:ns + n]
        first_out = ns + n + len(deps)
        send_sems, recv_sems, token = refs[first_out], refs[first_out + 1], refs[-1]
        for cp in _copies(routes, src, land, send_sems, recv_sems, False, src_kinds, land_kinds):
            cp.start()
        token[...] = jnp.zeros_like(token)

    thru = [pltpu.HBM(a.shape, a.dtype) for a in ops]
    res = pl.pallas_call(
        body, name=name,
        out_shape=(pltpu.SemaphoreType.DMA((n * n_routes,)), pltpu.SemaphoreType.DMA((n * n_routes,)), *thru,
                   jax.ShapeDtypeStruct((8, LANE), f32)),
        in_specs=[HBM] * len(ops) + [ANY] * len(deps),
        out_specs=(SEM, SEM, *([HBM] * len(ops)), pl.BlockSpec(memory_space=pltpu.VMEM)),
        input_output_aliases={i: 2 + i for i in range(len(ops))},
        compiler_params=pltpu.CompilerParams(has_side_effects=EFFECT),
    )(*[_pin(a) for a in ops], *deps)
    return (res[0], res[1], (list(res[2:2 + n]) if ns else None), list(res[2 + ns:2 + ns + n]), res[-1],
            (src_kinds, land_kinds))


def split_wait(started, after, routes, name):
    send_sems, recv_sems, srcs, lands, _, (src_kinds, land_kinds) = started
    n = len(lands)
    ns = 0 if srcs is None else n
    ops = ([] if srcs is None else list(srcs)) + list(lands)
    afters = list(after) if isinstance(after, (list, tuple)) else [after]

    def body(*refs):
        src, land = (refs[:n] if ns else None), refs[ns:ns + n]
        send_s, recv_s = refs[ns + n], refs[ns + n + 1]
        for cp in _copies(routes, src, land, send_s, recv_s, True, src_kinds, land_kinds):
            cp.wait_send()
            cp.wait_recv()

    thru = [pltpu.HBM(a.shape, a.dtype) for a in ops]
    res = pl.pallas_call(
        body, name=name, out_shape=tuple(thru),
        in_specs=[HBM] * len(ops) + [SEM, SEM] + [ANY] * len(afters), out_specs=tuple([HBM] * len(ops)),
        input_output_aliases={i: i for i in range(len(ops))},
        compiler_params=pltpu.CompilerParams(has_side_effects=EFFECT),
    )(*ops, send_sems, recv_sems, *afters)
    return (list(res[:n]) if ns else None), list(res[ns:ns + n])


def chip_sum(parts, land, c_idx, kind, name):
    _, R, C = land.shape
    tr = _tile(R, 256)
    half = N_DEV // 2

    def body(c_ref, p_ref, l_ref, o_ref):
        o_ref[...] = (p_ref[...].astype(f32) + l_ref[...].astype(f32)).astype(bf16)

    if kind == "cols":
        mine = lambda q, i, c_ref: ((2 * q + c_ref[0]) // half, i, (2 * q + c_ref[0]) % half)
    else:
        mine = lambda q, i, c_ref: (2 * q + c_ref[0], i, 0)
    return pl.pallas_call(
        body, name=name,
        grid_spec=pltpu.PrefetchScalarGridSpec(
            num_scalar_prefetch=1, grid=(half, R // tr),
            in_specs=[pl.BlockSpec((None, tr, C), mine), pl.BlockSpec((None, tr, C), lambda q, i, c_ref: (q, i, 0))],
            out_specs=pl.BlockSpec((None, tr, C), lambda q, i, c_ref: (q, i, 0))),
        out_shape=jax.ShapeDtypeStruct((half, R, C), bf16),
        compiler_params=_params("parallel", "parallel"),
    )(c_idx, parts, land)


def place_own(land, src, me_idx, kind, name):
    R, C = src.shape
    tr = _tile(R, 512)
    half = N_DEV // 2
    if kind == "cols":
        where = lambda i, m: (m[0] // half, i, m[0] % half)
    else:
        where = lambda i, m: (m[0], i, 0)

    def body(m_ref, land_ref, src_ref, out_ref):
        out_ref[...] = src_ref[...]

    return pl.pallas_call(
        body, name=name,
        grid_spec=pltpu.PrefetchScalarGridSpec(
            num_scalar_prefetch=1, grid=(R // tr,),
            in_specs=[ANY, pl.BlockSpec((tr, C), lambda i, m: (i, 0))],
            out_specs=pl.BlockSpec((None, tr, C), where)),
        out_shape=jax.ShapeDtypeStruct(land.shape, land.dtype),
        input_output_aliases={1: 0},
        compiler_params=_params("arbitrary"),
    )(me_idx, land, src)


def sum_slots(parts, name):
    P, R, C = parts.shape
    tr = _tile(R, 512)

    def body(p_ref, o_ref):
        total = p_ref[0]
        for j in range(1, P):
            total = total + p_ref[j]
        o_ref[...] = total

    return pl.pallas_call(
        body, name=name, grid=(R // tr,),
        in_specs=[pl.BlockSpec((P, tr, C), lambda i: (0, i, 0))],
        out_specs=pl.BlockSpec((tr, C), lambda i: (i, 0)),
        out_shape=jax.ShapeDtypeStruct((R, C), f32),
        compiler_params=_params("parallel"),
    )(parts)


def adamw(parts, owns, chip, w, m, v, name):
    L, R, C = w.shape
    P = parts[0].shape[0]
    tr = _tile(R, 128 if C > 1024 else 256)
    nr = R // tr
    c1 = 1.0 - ADAM_B1 ** ADAM_STEP
    c2 = 1.0 - ADAM_B2 ** ADAM_STEP
    n_own = L if owns is not None else 0

    def body(chip_ref, *refs):
        part_refs, own_refs = refs[:L], refs[L:L + n_own]
        w_ref, m_ref, v_ref, g_out, d_out, m_out, v_out = refs[L + n_own:]
        layer = pl.program_id(0)
        for l in range(L):
            @pl.when(layer == l)
            def _(l=l):
                g = None
                for q in range(P):
                    term = part_refs[l][q].astype(f32)
                    if n_own:
                        term = jnp.where(chip_ref[0] == q, own_refs[l][...].astype(f32), term)
                    g = term if g is None else g + term
                mn = ADAM_B1 * m_ref[...] + (1.0 - ADAM_B1) * g
                vn = ADAM_B2 * v_ref[...] + (1.0 - ADAM_B2) * (g * g)
                g_out[...] = g
                m_out[...] = mn
                v_out[...] = vn
                d_out[...] = -ADAM_LR * ((mn / c1) / (jnp.sqrt(vn / c2) + ADAM_EPS) + ADAM_WD * w_ref[...])

    def rows(l, a, i):
        return jnp.where(a == l, i, jnp.where(a < l, 0, nr - 1))

    def part_spec(l):
        return pl.BlockSpec((P, tr, C), lambda a, i, chip_ref: (0, rows(l, a, i), 0))

    def own_spec(l):
        return pl.BlockSpec((None, tr, C), lambda a, i, chip_ref: (chip_ref[0], rows(l, a, i), 0))

    slab = pl.BlockSpec((None, tr, C), lambda a, i, chip_ref: (a, i, 0))
    out = jax.ShapeDtypeStruct((L, R, C), f32)
    return pl.pallas_call(
        body, name=name,
        grid_spec=pltpu.PrefetchScalarGridSpec(
            num_scalar_prefetch=1, grid=(L, nr),
            in_specs=[part_spec(l) for l in range(L)] + [own_spec(l) for l in range(n_own)] + [slab, slab, slab],
            out_specs=[slab, slab, slab, slab]),
        out_shape=[out, out, out, out],
        compiler_params=_params("arbitrary", "arbitrary"),
    )(chip, *parts, *(owns or []), w, m, v)


PACK = 8 * LANE
PACK_ROWS = 256


def _pack(arrs):
    pieces = []
    for a in arrs:
        flat = a.astype(f32).reshape(-1)
        pad = (-flat.shape[0]) % PACK
        pieces.append(jnp.pad(flat, (0, pad)).reshape(-1, LANE))
    rows = sum(p.shape[0] for p in pieces)
    if rows > PACK_ROWS and rows % PACK_ROWS:
        pieces.append(jnp.zeros((PACK_ROWS - rows % PACK_ROWS, LANE), f32))
    return jnp.concatenate(pieces, axis=0)


def _unpack(buf, shapes):
    out, row = [], 0
    for shp in shapes:
        size = math.prod(shp)
        rows = (size + PACK - 1) // PACK * (PACK // LANE)
        out.append(buf[row:row + rows].reshape(-1)[:size].reshape(shp))
        row += rows
    return out


def _rope_tables(positions):
    half = ROT_DIM // 2
    inv_freq = 1.0 / (ROPE_THETA ** (jnp.arange(0, ROT_DIM, 2, dtype=f32) / ROT_DIM))
    ang = positions.astype(f32)[:, None] * inv_freq
    cos, sin = jnp.cos(ang), jnp.sin(ang)
    S = positions.shape[0]
    zeros, ones = jnp.zeros((S, half), f32), jnp.ones((S, HEAD_DIM - ROT_DIM), f32)
    rest = jnp.zeros((S, HEAD_DIM - ROT_DIM), f32)
    c = jnp.concatenate([cos, cos, ones], axis=1)
    s1 = jnp.concatenate([zeros, sin, rest], axis=1)
    s2 = jnp.concatenate([-sin, zeros, rest], axis=1)
    return tuple(jnp.tile(t, (1, LANE // HEAD_DIM)) for t in (c, s1, s2))


def _cols_to_shards(g):
    lead, (R, N) = g.shape[:-2], g.shape[-2:]
    g = g.reshape(lead + (R, N_DEV, N // N_DEV))
    return jnp.moveaxis(g, -2, 0)


def _shards_to_cols(g):
    g = jnp.moveaxis(g, 0, -2)
    return g.reshape(g.shape[:-2] + (g.shape[-2] * g.shape[-1],))


def kernel(x, positions, norm_ffn1, ffn1_w_in, ffn1_w_out, norm_mix, w_in, conv_dw_w, conv_dw_b, conv_ln_g, conv_ln_b, sgu_ln_g, sgu_ln_b, sgu_w, sgu_b, attn_sinks, w_out, norm_ffn2, ffn2_w_in, ffn2_w_out, final_norm, loss_target, m_norm_ffn1, m_ffn1_w_in, m_ffn1_w_out, m_norm_mix, m_w_in, m_conv_dw_w, m_conv_dw_b, m_conv_ln_g, m_conv_ln_b, m_sgu_ln_g, m_sgu_ln_b, m_sgu_w, m_sgu_b, m_attn_sinks, m_w_out, m_norm_ffn2, m_ffn2_w_in, m_ffn2_w_out, m_final_norm, v_norm_ffn1, v_ffn1_w_in, v_ffn1_w_out, v_norm_mix, v_w_in, v_conv_dw_w, v_conv_dw_b, v_conv_ln_g, v_conv_ln_b, v_sgu_ln_g, v_sgu_ln_b, v_sgu_w, v_sgu_b, v_attn_sinks, v_w_out, v_norm_ffn2, v_ffn2_w_in, v_ffn2_w_out, v_final_norm):
    L = norm_ffn1.shape[0]
    S, D = x.shape[1], x.shape[2]
    F = ffn1_w_out.shape[1] * N_DEV
    me = 4 * lax.axis_index("x") + 2 * lax.axis_index("y") + lax.axis_index("c")
    x0 = x[0]
    rope_c, rope_s1, rope_s2 = _rope_tables(positions[0])
    cw = CONV_CH // N_DEV

    c_idx = lax.axis_index("c").astype(jnp.int32).reshape(1)
    chip = (2 * lax.axis_index("x") + lax.axis_index("y")).astype(jnp.int32).reshape(1)
    no_chip = jnp.zeros((1,), jnp.int32)
    me_idx = me.astype(jnp.int32).reshape(1)

    row = lambda a: a.reshape(1, -1)
    order_fwd = [(l, g) for l in range(L) for g in (("ffn1_in", "ffn1_out") if l == 0 else ("ffn1",)) + ("mix", "ffn2")]

    def group_srcs(l, grp):
        if grp == "mix":
            taps = jnp.pad(conv_dw_w[l], ((0, HALO - CONV_W), (0, LANE - cw)))
            return [w_in[l].astype(bf16), w_out[l].astype(bf16), taps]
        both = ([ffn2_w_in[l], ffn2_w_out[l]] if grp == "ffn2" else [ffn1_w_in[l], ffn1_w_out[l]])
        both = [a.astype(bf16) for a in both]
        return both[:1] if grp == "ffn1_in" else both[1:] if grp == "ffn1_out" else both

    def kinds_of(grp):
        return {"mix": ["rows"] * 3, "ffn1_in": ["cols"], "ffn1_out": ["rows"]}.get(grp, ["cols", "rows"])

    def gather_start(k, deps=()):
        l, grp = order_fwd[k]
        srcs = group_srcs(l, grp)
        lands = [lax.empty((2, D, F) if kind == "cols" else (N_DEV,) + a.shape, a.dtype)
                 for a, kind in zip(srcs, kinds_of(grp))]
        return split_start(srcs, lands, _routes_gather, f"gather_start_{grp}_{l}", deps, land_kinds=kinds_of(grp))

    def gather_forward(k, started, after):
        l, grp = order_fwd[k]
        srcs, lands = split_wait(started, after, _routes_gather, f"gather_wait_{grp}_{l}")
        return srcs, split_start(None, lands, _routes_forward, f"forward_start_{grp}_{l}", land_kinds=kinds_of(grp))

    def gather_finish(k, srcs, started, after):
        l, grp = order_fwd[k]
        _, lands = split_wait(started, after, _routes_forward, f"forward_wait_{grp}_{l}")
        full = [place_own(ld, s, me_idx, kind, f"own_{grp}_{l}_{a}")
                for a, (ld, s, kind) in enumerate(zip(lands, srcs, kinds_of(grp)))]
        if grp == "mix":
            return dict(w_in=_shards_to_cols(full[0]), w_out=full[1].reshape(D, D),
                        taps=_shards_to_cols(full[2][:, :, :cw]))
        named = dict(zip(["w_in", "w_out"] if len(full) == 2 else ["w_in" if grp == "ffn1_in" else "w_out"], full))
        if "w_out" in named:
            named["w_out"] = named["w_out"].reshape(F, D)
        return named

    def ffn_head(xs, wts, g_norm, tag, sv):
        sv["x_in"] = xs
        h, sv["ht"] = rmsnorm_fwd(xs, g_norm, f"norm_{tag}")
        sv["gu"], a, sv["at"] = ffn_in(h, wts["w_in"], f"{tag}_in")
        return a

    def ffn_fwd(xs, wts, g_norm, tag, sv, mid):
        a = ffn_head(xs, wts, g_norm, tag, sv)
        mid(a)
        return mm_res(a, wts["w_out"], xs, FFN_RES, f"{tag}_out")

    def mix_fwd(xs, wts, l, sv, mid):
        sv["x_in"] = xs
        h, sv["ht"] = rmsnorm_fwd(xs, row(norm_mix[l]), f"norm_mix_{l}")
        p = mm_nn(h, wts["w_in"][None], f"mix_in_{l}")[0]
        sv["p"] = p
        attn = attn_fwd(p, rope_c, rope_s1, rope_s2, attn_sinks[l], f"attn_fwd_{l}")
        conv, sv["conv_y"] = conv_fwd(p, wts["taps"], row(conv_dw_b[l]), row(conv_ln_g[l]), row(conv_ln_b[l]),
                                      f"conv_fwd_{l}")
        sv["sgu_bias"] = jnp.repeat(sgu_b[l].T, HEAD_DIM, axis=1)
        sgu = sgu_fwd(p, row(sgu_ln_g[l]), row(sgu_ln_b[l]), sgu_w[l], sv["sgu_bias"], f"sgu_fwd_{l}")
        cat = jnp.concatenate([attn, conv, sgu], axis=1)
        sv["catt"] = cat.T
        mid(cat)
        return mm_res(cat, wts["w_out"], xs, 1.0, f"mix_out_{l}")

    weights, saved = {}, {}
    xs = x0
    first = gather_start(0)
    state = dict(zip(("srcs", "fwd"), gather_forward(0, first, first[4])))
    for k, (l, grp) in enumerate(order_fwd):
        nxt = gather_start(k + 1, (state["fwd"][4],)) if k + 1 < len(order_fwd) else None
        wts = gather_finish(k, state["srcs"], state["fwd"], nxt[4] if nxt else state["fwd"][4])

        def mid(after, k=k, nxt=nxt):
            if nxt:
                state["srcs"], state["fwd"] = gather_forward(k + 1, nxt, after)

        if grp == "ffn1_in":
            sv = saved[l, "ffn1"] = {}
            weights[l, "ffn1"] = wts
            head = ffn_head(xs, wts, row(norm_ffn1[l]), f"ffn1_{l}", sv)
            mid(head)
        elif grp == "ffn1_out":
            weights[l, "ffn1"].update(wts)
            mid(head)
            xs = mm_res(head, wts["w_out"], xs, FFN_RES, f"ffn1_{l}_out")
        else:
            sv = saved[l, grp] = {}
            weights[l, grp] = wts
            if grp == "mix":
                xs = mix_fwd(xs, wts, l, sv, mid)
            else:
                xs = ffn_fwd(xs, wts, row(norm_ffn1[l] if grp == "ffn1" else norm_ffn2[l]), f"{grp}_{l}", sv, mid)

    dx, dxb, d_final_norm, loss = final_loss(xs, row(final_norm), loss_target[0], "final_loss")

    def scatter_start(grads, kinds, tag):
        half = N_DEV // 2
        lands = [lax.empty((half, g.shape[1], g.shape[2] // half) if kind == "cols" else (half,) + g.shape[1:], g.dtype)
                 for g, kind in zip(grads, kinds)]
        return split_start(grads, lands, _routes_pair, f"pair_start_{tag}", src_kinds=kinds)

    def pair_to_chips(started, after, tag):
        kinds = started[5][0]
        grads, landed = split_wait(started, after, _routes_pair, f"pair_wait_{tag}")
        sums = [chip_sum(g, ld, c_idx, kind, f"chip_sum_{tag}_{a}")
                for a, (g, ld, kind) in enumerate(zip(grads, landed, kinds))]
        return split_start(sums, [lax.empty(s.shape, s.dtype) for s in sums], _routes_chips, f"chips_start_{tag}")

    def ffn_bwd(dx, dxb, wts, sv, g_norm, tag, deps=()):
        dgu = ffn_dact(dxb, wts["w_out"], sv["gu"], f"{tag}_dact")
        d_w_out = mm_nn(sv["at"], dxb[None], f"{tag}_dwout", bf16, FFN_RES, deps=deps)[0]
        d_w_in = mm_nn(sv["ht"], dgu, f"{tag}_dwin", bf16)
        pair = scatter_start([d_w_in, d_w_out.reshape(N_DEV, F // N_DEV, D)], ["cols", "rows"], tag)
        dh = mm_nt(dgu, wts["w_in"], f"{tag}_dh", deps=(pair[4],))
        chips = pair_to_chips(pair, dh, tag)
        dx, dxb, dg = rmsnorm_bwd(dh, sv["x_in"], g_norm, dx, f"{tag}_dnorm", deps=(chips[4],))
        return dx, dxb, dg, chips

    small = [None] * L
    chips_pending = {}
    for l in reversed(range(L)):
        dx, dxb, d_norm_ffn2, chips_pending[l, "ffn2"] = ffn_bwd(
            dx, dxb, weights[l, "ffn2"], saved[l, "ffn2"], row(norm_ffn2[l]), f"ffn2_{l}")

        wts, sv = weights[l, "mix"], saved[l, "mix"]
        d_w_out = mm_nn(sv["catt"], dxb[None], f"mix_dwout_{l}", bf16)[0]
        dcat = mm_nt(dxb[None], wts["w_out"][None], f"mix_dcat_{l}", deps=(d_w_out,))
        p = sv["p"]
        dq, dkv, d_sinks = attn_bwd(p, dcat, rope_c, rope_s1, rope_s2, attn_sinks[l], f"attn_bwd_{l}")
        da_conv, d_taps, d_conv_b, d_conv_g, d_conv_bb = conv_bwd(
            p, sv["conv_y"], dcat, wts["taps"], row(conv_ln_g[l]), row(conv_ln_b[l]), f"conv_bwd_{l}")
        da_sgu, d_sgu_w, d_sgu_bias, d_sgu_g, d_sgu_bb = sgu_bwd(
            p, dcat, row(sgu_ln_g[l]), row(sgu_ln_b[l]), sgu_w[l], jnp.swapaxes(sgu_w[l], 1, 2), sv["sgu_bias"],
            f"sgu_bwd_{l}")
        dp = jnp.concatenate([dq, dkv, da_conv, da_sgu], axis=1)
        d_w_in = mm_nn(sv["ht"], dp[None], f"mix_dwin_{l}", bf16)[0]
        pair = scatter_start([_cols_to_shards(d_w_in), d_w_out.reshape(N_DEV, D // N_DEV, D)], ["rows", "rows"],
                             f"mix_{l}")
        dh = mm_nt(dp[None], wts["w_in"][None], f"mix_dh_{l}", deps=(pair[4],))
        chips_pending[l, "mix"] = pair_to_chips(pair, dh, f"mix_{l}")
        dx, dxb, d_norm_mix = rmsnorm_bwd(dh, sv["x_in"], row(norm_mix[l]), dx, f"mix_dnorm_{l}",
                                          deps=(chips_pending[l, "mix"][4],))

        small[l] = dict(norm_mix=d_norm_mix[0], conv_dw_w=d_taps[:CONV_W],
                        conv_dw_b=d_conv_b[0], conv_ln_g=d_conv_g[0], conv_ln_b=d_conv_bb[0], sgu_ln_g=d_sgu_g[0],
                        sgu_ln_b=d_sgu_bb[0], sgu_w=d_sgu_w, sgu_b=d_sgu_bias[:, :SGU_HEADS].T,
                        attn_sinks=d_sinks[0, :N_Q_HEADS], norm_ffn2=d_norm_ffn2[0])
        if l == 0:
            early_names = ["norm_mix", "conv_dw_w", "conv_dw_b", "conv_ln_g", "conv_ln_b", "sgu_ln_g", "sgu_ln_b",
                           "sgu_w", "sgu_b", "attn_sinks", "norm_ffn2"]
            early = [jnp.stack([small[k][n] for k in range(L)]) for n in early_names]
            early += [d_final_norm[0], loss.reshape(1)]
            early_shapes = [a.shape for a in early]
            early = _pack(early)
            early_pending = split_start([early], [lax.empty((N_DEV,) + early.shape, f32)], _routes_all,
                                        "small_start", deps=(dxb,))
            early_token = (early_pending[4],)
        else:
            early_token = ()

        dx, dxb, d_norm_ffn1, chips_pending[l, "ffn1"] = ffn_bwd(
            dx, dxb, weights[l, "ffn1"], saved[l, "ffn1"], row(norm_ffn1[l]), f"ffn1_{l}", early_token)
        small[l]["norm_ffn1"] = d_norm_ffn1[0]

    grad_x = dx[None]
    late = _pack([jnp.stack([small[l]["norm_ffn1"] for l in range(L)])])
    late_pending = split_start([late], [lax.empty((N_DEV,) + late.shape, f32)], _routes_all, "late_start", deps=(dx,))

    def landed(grp, after):
        got = [split_wait(chips_pending[l, grp], after, _routes_chips, f"chips_wait_{grp}_{l}") for l in range(L)]
        return [g[0] for g in got], [g[1] for g in got]

    given = dict(norm_ffn1=(norm_ffn1, m_norm_ffn1, v_norm_ffn1), norm_mix=(norm_mix, m_norm_mix, v_norm_mix),
                 conv_dw_w=(conv_dw_w, m_conv_dw_w, v_conv_dw_w), conv_dw_b=(conv_dw_b, m_conv_dw_b, v_conv_dw_b),
                 conv_ln_g=(conv_ln_g, m_conv_ln_g, v_conv_ln_g), conv_ln_b=(conv_ln_b, m_conv_ln_b, v_conv_ln_b),
                 sgu_ln_g=(sgu_ln_g, m_sgu_ln_g, v_sgu_ln_g), sgu_ln_b=(sgu_ln_b, m_sgu_ln_b, v_sgu_ln_b),
                 sgu_w=(sgu_w, m_sgu_w, v_sgu_w), sgu_b=(sgu_b, m_sgu_b, v_sgu_b),
                 attn_sinks=(attn_sinks, m_attn_sinks, v_attn_sinks), norm_ffn2=(norm_ffn2, m_norm_ffn2, v_norm_ffn2),
                 final_norm=(final_norm, m_final_norm, v_final_norm))

    def small_update(pending, after, names, shapes, tag):
        (own,), (others,) = split_wait(pending, after, _routes_all, f"{tag}_wait")
        total = _unpack(sum_slots(place_own(others, own, me_idx, "rows", f"{tag}_own"), f"{tag}_sum"), shapes)
        g = dict(zip(names, total))
        if "conv_dw_w" in g:
            g["conv_dw_w"] = lax.dynamic_slice_in_dim(g["conv_dw_w"], me * cw, cw, axis=2)
        upd_names = [n for n in names if n in given]
        upd_shapes = [given[n][0].shape for n in upd_names]
        packed = [_pack([g[n] for n in upd_names])[None]] + [_pack([given[n][k] for n in upd_names])[None]
                                                              for k in range(3)]
        res = adamw([packed[0]], None, no_chip, packed[1], packed[2], packed[3], f"{tag}_adamw")
        return g, [dict(zip(upd_names, _unpack(r[0], upd_shapes))) for r in res], res[0]

    big = {}
    done = [dx]

    def big_update(grp, names):
        sums, lands = landed(grp, done)
        for idx, (name, w, m, v) in enumerate(names):
            big[name] = adamw([lands[l][idx] for l in range(L)], [sums[l][idx] for l in range(L)], chip, w, m, v,
                              f"adamw_{name}")
            done.append(big[name][0])

    big_update("ffn2", (("ffn2_w_in", ffn2_w_in, m_ffn2_w_in, v_ffn2_w_in),
                        ("ffn2_w_out", ffn2_w_out, m_ffn2_w_out, v_ffn2_w_out)))
    big_update("mix", (("w_in", w_in, m_w_in, v_w_in), ("w_out", w_out, m_w_out, v_w_out)))
    g_early, upd_early, marker = small_update(early_pending, done, early_names + ["final_norm", "loss"], early_shapes,
                                              "small")
    done.append(marker)
    big_update("ffn1", (("ffn1_w_in", ffn1_w_in, m_ffn1_w_in, v_ffn1_w_in),
                        ("ffn1_w_out", ffn1_w_out, m_ffn1_w_out, v_ffn1_w_out)))
    _, upd_late, _ = small_update(late_pending, done, ["norm_ffn1"], [(L, D)], "late")
    upd = [{**upd_early[k], **upd_late[k]} for k in range(4)]

    order = ["norm_ffn1", "ffn1_w_in", "ffn1_w_out", "norm_mix", "w_in", "conv_dw_w", "conv_dw_b", "conv_ln_g",
             "conv_ln_b", "sgu_ln_g", "sgu_ln_b", "sgu_w", "sgu_b", "attn_sinks", "w_out", "norm_ffn2", "ffn2_w_in",
             "ffn2_w_out", "final_norm"]
    outs = [g_early["loss"].reshape(()), grad_x]
    for k in range(4):
        outs += [big[n][k] if n in big else upd[k][n] for n in order]
    return tuple(outs)
```

```python
import functools
import math

import jax
import jax.numpy as jnp
from jax import lax
from jax.experimental import pallas as pl
from jax.experimental.pallas import tpu as pltpu

f32 = jnp.float32
bf16 = jnp.bfloat16

N_DEV = 8
HEAD_DIM = 64
N_Q_HEADS = 16
N_KV_HEADS = 4
GQ = N_Q_HEADS // N_KV_HEADS
BLK = 128
ROT_DIM = 16
ROPE_THETA = 500000.0
CONV_W = 31
CONV_CH = 512
SGU_CH = 512
SGU_HEADS = 8
Q_END = N_Q_HEADS * HEAD_DIM
KV_W = 2 * N_KV_HEADS * HEAD_DIM
V_END = Q_END + KV_W
IN_COLS = V_END + 2 * CONV_CH + 2 * SGU_CH
HALO = 32
NORM_EPS = 1e-5
FFN_RES = 0.5
ADAM_LR, ADAM_B1, ADAM_B2, ADAM_EPS, ADAM_WD, ADAM_STEP = 0.001, 0.9, 0.999, 1e-08, 0.01, 10
LANE = 128
VMEM_LIMIT = 56 * 2 ** 20
MESH = pl.DeviceIdType.MESH

NN = (((1,), (0,)), ((), ()))
NT = (((1,), (1,)), ((), ()))
TN = (((0,), (0,)), ((), ()))


def _tile(dim, pref):
    t = min(pref, dim)
    while dim % t:
        t //= 2
    return t


def _params(*sem):
    return pltpu.CompilerParams(dimension_semantics=sem, vmem_limit_bytes=VMEM_LIMIT)


def _sig(x):
    return 1.0 / (1.0 + jnp.exp(-x))


def _dot(a, b, dims=NN):
    return lax.dot_general(a, b, dims, preferred_element_type=f32)


def _rowsum8(x):
    return x.reshape(x.shape[0] // 8, 8, x.shape[1]).sum(axis=0)


def rmsnorm_fwd(x, g, name, deps=()):
    S, D = x.shape
    tm = _tile(S, 512)

    def body(x_ref, g_ref, *rest):
        o_ref, ot_ref = rest[-2:]
        xv = x_ref[...]
        r = lax.rsqrt(jnp.mean(xv * xv, axis=-1, keepdims=True) + NORM_EPS)
        hb = (xv * r * g_ref[...]).astype(bf16)
        o_ref[...] = hb
        ot_ref[...] = hb.T

    return pl.pallas_call(
        body, name=name, grid=(S // tm,),
        in_specs=[pl.BlockSpec((tm, D), lambda i: (i, 0)), pl.BlockSpec((1, D), lambda i: (0, 0))] + [ANY] * len(deps),
        out_specs=[pl.BlockSpec((tm, D), lambda i: (i, 0)), pl.BlockSpec((D, tm), lambda i: (0, i))],
        out_shape=[jax.ShapeDtypeStruct((S, D), bf16), jax.ShapeDtypeStruct((D, S), bf16)],
        compiler_params=_params("parallel"),
    )(x, g, *deps)


def rmsnorm_bwd(dh, x, g, dres, name, deps=()):
    S, D = x.shape
    tm = _tile(S, 256)
    n = S // tm

    def body(dh_ref, x_ref, g_ref, dres_ref, *rest):
        dx_ref, dxb_ref, dg_ref, acc = rest[-4:]
        i = pl.program_id(0)

        @pl.when(i == 0)
        def _():
            acc[...] = jnp.zeros_like(acc)

        xv = x_ref[...]
        r = lax.rsqrt(jnp.mean(xv * xv, axis=-1, keepdims=True) + NORM_EPS)
        xh = xv * r
        dy = dh_ref[...]
        gy = dy * g_ref[...]
        dx = dres_ref[...] + r * (gy - xh * jnp.mean(gy * xh, axis=-1, keepdims=True))
        dx_ref[...] = dx
        dxb_ref[...] = dx.astype(bf16)
        acc[...] += _rowsum8(dy * xh)

        @pl.when(i == n - 1)
        def _():
            dg_ref[...] = jnp.sum(acc[...], axis=0, keepdims=True)

    row = pl.BlockSpec((tm, D), lambda i: (i, 0))
    vec = pl.BlockSpec((1, D), lambda i: (0, 0))
    return pl.pallas_call(
        body, name=name, grid=(n,),
        in_specs=[row, row, vec, row] + [ANY] * len(deps),
        out_specs=[row, row, vec],
        out_shape=[jax.ShapeDtypeStruct((S, D), f32), jax.ShapeDtypeStruct((S, D), bf16),
                   jax.ShapeDtypeStruct((1, D), f32)],
        scratch_shapes=[pltpu.VMEM((8, D), f32)],
        compiler_params=_params("arbitrary"),
    )(dh, x, g, dres, *deps)


def final_loss(x, g, tgt, name):
    S, D = x.shape
    tm = _tile(S, 256)
    n = S // tm

    def body(x_ref, g_ref, t_ref, dx_ref, dxb_ref, dg_ref, loss_ref, acc):
        i = pl.program_id(0)

        @pl.when(i == 0)
        def _():
            acc[...] = jnp.zeros_like(acc)
            loss_ref[...] = jnp.zeros_like(loss_ref)

        xv = x_ref[...]
        gv = g_ref[...]
        r = lax.rsqrt(jnp.mean(xv * xv, axis=-1, keepdims=True) + NORM_EPS)
        xh = xv * r
        diff = xh * gv - t_ref[...]
        tok = jnp.mean(diff * diff, axis=-1, keepdims=True)
        loss_ref[...] += 0.5 * jnp.sum(tok, axis=0, keepdims=True)
        dy = diff / D
        gy = dy * gv
        dx = r * (gy - xh * jnp.mean(gy * xh, axis=-1, keepdims=True))
        dx_ref[...] = dx
        dxb_ref[...] = dx.astype(bf16)
        acc[...] += _rowsum8(dy * xh)

        @pl.when(i == n - 1)
        def _():
            dg_ref[...] = jnp.sum(acc[...], axis=0, keepdims=True)

    row = pl.BlockSpec((tm, D), lambda i: (i, 0))
    vec = pl.BlockSpec((1, D), lambda i: (0, 0))
    return pl.pallas_call(
        body, name=name, grid=(n,),
        in_specs=[row, vec, row],
        out_specs=[row, row, vec, pl.BlockSpec((1, 1), lambda i: (0, 0))],
        out_shape=[jax.ShapeDtypeStruct((S, D), f32), jax.ShapeDtypeStruct((S, D), bf16),
                   jax.ShapeDtypeStruct((1, D), f32), jax.ShapeDtypeStruct((1, 1), f32)],
        scratch_shapes=[pltpu.VMEM((8, D), f32)],
        compiler_params=_params("arbitrary"),
    )(x, g, tgt)


def ffn_in(h, w2, name):
    S, D = h.shape
    F = w2.shape[2]
    tm, tn = _tile(S, 512), _tile(F, 512)

    def body(h_ref, w_ref, gu_ref, a_ref, at_ref):
        hv = h_ref[...]
        g = _dot(hv, w_ref[0])
        u = _dot(hv, w_ref[1])
        gu_ref[0] = g.astype(bf16)
        gu_ref[1] = u.astype(bf16)
        a = (g * _sig(g) * u).astype(bf16)
        a_ref[...] = a
        at_ref[...] = a.T

    return pl.pallas_call(
        body, name=name, grid=(S // tm, F // tn),
        in_specs=[pl.BlockSpec((tm, D), lambda i, j: (i, 0)), pl.BlockSpec((2, D, tn), lambda i, j: (0, 0, j))],
        out_specs=[pl.BlockSpec((2, tm, tn), lambda i, j: (0, i, j)), pl.BlockSpec((tm, tn), lambda i, j: (i, j)),
                   pl.BlockSpec((tn, tm), lambda i, j: (j, i))],
        out_shape=[jax.ShapeDtypeStruct((2, S, F), bf16), jax.ShapeDtypeStruct((S, F), bf16),
                   jax.ShapeDtypeStruct((F, S), bf16)],
        compiler_params=_params("parallel", "parallel"),
    )(h, w2)


def mm_res(a, w, x, scale, name, deps=()):
    S, K = a.shape
    N = w.shape[1]
    tm, tn = _tile(S, 512), _tile(N, 512)

    def body(a_ref, w_ref, x_ref, *rest):
        rest[-1][...] = x_ref[...] + scale * _dot(a_ref[...], w_ref[...])

    return pl.pallas_call(
        body, name=name, grid=(S // tm, N // tn),
        in_specs=[pl.BlockSpec((tm, K), lambda i, j: (i, 0)), pl.BlockSpec((K, tn), lambda i, j: (0, j)),
                  pl.BlockSpec((tm, tn), lambda i, j: (i, j))] + [ANY] * len(deps),
        out_specs=pl.BlockSpec((tm, tn), lambda i, j: (i, j)),
        out_shape=jax.ShapeDtypeStruct((S, N), f32),
        compiler_params=_params("parallel", "parallel"),
    )(a, w, x, *deps)


def mm_nn(a, b, name, out_dtype=f32, scale=1.0, deps=()):
    M, K = a.shape
    G, _, N = b.shape
    tm, tn = _tile(M, 512), _tile(N, 512)

    def body(a_ref, b_ref, *rest):
        acc = _dot(a_ref[...], b_ref[...])
        rest[-1][...] = (acc if scale == 1.0 else scale * acc).astype(out_dtype)

    return pl.pallas_call(
        body, name=name, grid=(G, M // tm, N // tn),
        in_specs=[pl.BlockSpec((tm, K), lambda g, i, j: (i, 0)),
                  pl.BlockSpec((None, K, tn), lambda g, i, j: (g, 0, j))] + [ANY] * len(deps),
        out_specs=pl.BlockSpec((None, tm, tn), lambda g, i, j: (g, i, j)),
        out_shape=jax.ShapeDtypeStruct((G, M, N), out_dtype),
        compiler_params=_params("parallel", "parallel", "parallel"),
    )(a, b, *deps)


def mm_nt(a, w, name, deps=()):
    G, S, K = a.shape
    N = w.shape[1]
    tm, tn = _tile(S, 512), _tile(N, 512)

    def body(a_ref, w_ref, *rest):
        o_ref = rest[-1]
        part = _dot(a_ref[...], w_ref[...], NT)
        if G == 1:
            o_ref[...] = part
        else:
            g = pl.program_id(2)

            @pl.when(g == 0)
            def _():
                o_ref[...] = part

            @pl.when(g > 0)
            def _():
                o_ref[...] += part

    return pl.pallas_call(
        body, name=name, grid=(S // tm, N // tn, G),
        in_specs=[pl.BlockSpec((None, tm, K), lambda i, j, g: (g, i, 0)),
                  pl.BlockSpec((None, tn, K), lambda i, j, g: (g, j, 0))] + [ANY] * len(deps),
        out_specs=pl.BlockSpec((tm, tn), lambda i, j, g: (i, j)),
        out_shape=jax.ShapeDtypeStruct((S, N), f32),
        compiler_params=_params("parallel", "parallel", "arbitrary"),
    )(a, w, *deps)


def ffn_dact(dx, wout, gu, name):
    S, D = dx.shape
    F = wout.shape[0]
    tm, tn = _tile(S, 512), _tile(F, 512)

    def body(dx_ref, w_ref, gu_ref, o_ref):
        da = FFN_RES * _dot(dx_ref[...], w_ref[...], NT)
        g = gu_ref[0].astype(f32)
        u = gu_ref[1].astype(f32)
        sg = _sig(g)
        o_ref[0] = (da * u * (sg * (1.0 + g * (1.0 - sg)))).astype(bf16)
        o_ref[1] = (da * (g * sg)).astype(bf16)

    return pl.pallas_call(
        body, name=name, grid=(S // tm, F // tn),
        in_specs=[pl.BlockSpec((tm, D), lambda i, j: (i, 0)), pl.BlockSpec((tn, D), lambda i, j: (j, 0)),
                  pl.BlockSpec((2, tm, tn), lambda i, j: (0, i, j))],
        out_specs=pl.BlockSpec((2, tm, tn), lambda i, j: (0, i, j)),
        out_shape=jax.ShapeDtypeStruct((2, S, F), bf16),
        compiler_params=_params("parallel", "parallel"),
    )(dx, wout, gu)


def _rope(t, c, s1, s2):
    w = t.shape[1]
    return t * c + pltpu.roll(t, 8, 1) * s1 + pltpu.roll(t, w - 8, 1) * s2


def _rope_t(d, c, s1, s2):
    w = d.shape[1]
    return d * c + pltpu.roll(d * s1, w - 8, 1) + pltpu.roll(d * s2, 8, 1)


def _attn_mask(n):
    qi = lax.broadcasted_iota(jnp.int32, (BLK, 2 * BLK), 0)
    kj = lax.broadcasted_iota(jnp.int32, (BLK, 2 * BLK), 1)
    dist = qi + BLK - kj
    return (dist >= 0) & (dist < BLK) & ((kj >= BLK) | (n > 0))


def _softmax_sink(s, valid, sk):
    s = jnp.where(valid, s, -1e30)
    m = jnp.maximum(jnp.max(s, axis=-1, keepdims=True), sk)
    e = jnp.exp(s - m)
    es = jnp.exp(sk - m)
    inv = 1.0 / (jnp.sum(e, axis=-1, keepdims=True) + es)
    return e * inv, es * inv


def attn_fwd(p, rope_c, rope_s1, rope_s2, sinks, name):
    S = p.shape[0]
    nb = S // BLK
    kvb = Q_END // KV_W

    def body(sink_ref, q_ref, kvc_ref, kvp_ref, cc_ref, s1c_ref, s2c_ref, cp_ref, s1p_ref, s2p_ref, o_ref):
        n = pl.program_id(0)
        cc, s1c, s2c = cc_ref[...], s1c_ref[...], s2c_ref[...]
        cp, s1p, s2p = cp_ref[...], s1p_ref[...], s2p_ref[...]
        q = _rope(q_ref[...], jnp.tile(cc, (1, 8)), jnp.tile(s1c, (1, 8)), jnp.tile(s2c, (1, 8)))
        kc = _rope(kvc_ref[:, :256], jnp.tile(cc, (1, 2)), jnp.tile(s1c, (1, 2)), jnp.tile(s2c, (1, 2)))
        kp = _rope(kvp_ref[:, :256], jnp.tile(cp, (1, 2)), jnp.tile(s1p, (1, 2)), jnp.tile(s2p, (1, 2)))
        k = jnp.concatenate([kp, kc], axis=0).astype(bf16)
        v = jnp.concatenate([kvp_ref[:, 256:], kvc_ref[:, 256:]], axis=0).astype(bf16)
        q = q.astype(bf16)
        valid = _attn_mask(n)
        for h in range(N_KV_HEADS):
            kh = k[:, h * HEAD_DIM:(h + 1) * HEAD_DIM]
            vh = v[:, h * HEAD_DIM:(h + 1) * HEAD_DIM]
            for g in range(GQ):
                hq = h * GQ + g
                qh = q[:, hq * HEAD_DIM:(hq + 1) * HEAD_DIM]
                s = _dot(qh, kh, NT) * (HEAD_DIM ** -0.5)
                pr, _ = _softmax_sink(s, valid, sink_ref[hq])
                o = _dot(pr.astype(bf16), vh)
                o_ref[:, hq * HEAD_DIM:(hq + 1) * HEAD_DIM] = o.astype(bf16)

    tab_c = pl.BlockSpec((BLK, LANE), lambda n: (n, 0))
    tab_p = pl.BlockSpec((BLK, LANE), lambda n: (jnp.maximum(n - 1, 0), 0))
    return pl.pallas_call(
        body, name=name, grid=(nb,),
        in_specs=[pl.BlockSpec(memory_space=pltpu.SMEM),
                  pl.BlockSpec((BLK, Q_END), lambda n: (n, 0)),
                  pl.BlockSpec((BLK, KV_W), lambda n: (n, kvb)),
                  pl.BlockSpec((BLK, KV_W), lambda n: (jnp.maximum(n - 1, 0), kvb)),
                  tab_c, tab_c, tab_c, tab_p, tab_p, tab_p],
        out_specs=pl.BlockSpec((BLK, Q_END), lambda n: (n, 0)),
        out_shape=jax.ShapeDtypeStruct((S, Q_END), bf16),
        compiler_params=_params("parallel"),
    )(sinks, p, p, p, rope_c, rope_s1, rope_s2, rope_c, rope_s1, rope_s2)


def attn_bwd(p, dcat, rope_c, rope_s1, rope_s2, sinks, name):
    S = p.shape[0]
    nb = S // BLK
    kvb = Q_END // KV_W

    def body(sink_ref, q_ref, kvc_ref, kvp_ref, do_ref, cc_ref, s1c_ref, s2c_ref, cp_ref, s1p_ref, s2p_ref,
             dq_ref, dkv_ref, dsink_ref, carry, dq_scr, dkv_scr):
        n = pl.program_id(0)

        @pl.when(n == 0)
        def _():
            carry[...] = jnp.zeros_like(carry)
            dsink_ref[...] = jnp.zeros_like(dsink_ref)

        cp, s1p, s2p = cp_ref[...], s1p_ref[...], s2p_ref[...]
        cp2, s1p2, s2p2 = jnp.tile(cp, (1, 2)), jnp.tile(s1p, (1, 2)), jnp.tile(s2p, (1, 2))

        @pl.when(n < nb)
        def _():
            cc, s1c, s2c = cc_ref[...], s1c_ref[...], s2c_ref[...]
            cc8, s1c8, s2c8 = jnp.tile(cc, (1, 8)), jnp.tile(s1c, (1, 8)), jnp.tile(s2c, (1, 8))
            q = _rope(q_ref[...], cc8, s1c8, s2c8).astype(bf16)
            kc = _rope(kvc_ref[:, :256], jnp.tile(cc, (1, 2)), jnp.tile(s1c, (1, 2)), jnp.tile(s2c, (1, 2)))
            kp = _rope(kvp_ref[:, :256], cp2, s1p2, s2p2)
            k = jnp.concatenate([kp, kc], axis=0).astype(bf16)
            v = jnp.concatenate([kvp_ref[:, 256:], kvc_ref[:, 256:]], axis=0).astype(bf16)
            do = do_ref[...].astype(bf16)
            valid = _attn_mask(n)
            lane = lax.broadcasted_iota(jnp.int32, (1, LANE), 1)
            dsink = jnp.zeros((1, LANE), f32)
            for h in range(N_KV_HEADS):
                kh = k[:, h * HEAD_DIM:(h + 1) * HEAD_DIM]
                vh = v[:, h * HEAD_DIM:(h + 1) * HEAD_DIM]
                dkh = jnp.zeros((2 * BLK, HEAD_DIM), f32)
                dvh = jnp.zeros((2 * BLK, HEAD_DIM), f32)
                for g in range(GQ):
                    hq = h * GQ + g
                    qh = q[:, hq * HEAD_DIM:(hq + 1) * HEAD_DIM]
                    doh = do[:, hq * HEAD_DIM:(hq + 1) * HEAD_DIM]
                    s = _dot(qh, kh, NT) * (HEAD_DIM ** -0.5)
                    pr, ps = _softmax_sink(s, valid, sink_ref[hq])
                    dpr = _dot(doh, vh, NT)
                    dvh = dvh + _dot(pr.astype(bf16), doh, TN)
                    row = jnp.sum(pr * dpr, axis=-1, keepdims=True)
                    ds = (pr * (dpr - row) * (HEAD_DIM ** -0.5)).astype(bf16)
                    dsink = dsink + jnp.where(lane == hq, -jnp.sum(ps * row, axis=0, keepdims=True), 0.0)
                    dq_scr[:, hq * HEAD_DIM:(hq + 1) * HEAD_DIM] = _dot(ds, kh)
                    dkh = dkh + _dot(ds, qh, TN)
                dkv_scr[:, h * HEAD_DIM:(h + 1) * HEAD_DIM] = dkh
                dkv_scr[:, 256 + h * HEAD_DIM:256 + (h + 1) * HEAD_DIM] = dvh
            dsink_ref[...] += dsink
            dq_ref[...] = _rope_t(dq_scr[...], cc8, s1c8, s2c8).astype(bf16)

        prev = carry[...]

        @pl.when(n < nb)
        def _():
            dkv_scr[pl.ds(0, BLK), :] = dkv_scr[pl.ds(0, BLK), :] + prev

        @pl.when(n == nb)
        def _():
            dkv_scr[pl.ds(0, BLK), :] = prev

        done = dkv_scr[pl.ds(0, BLK), :]
        dkv_ref[:, :256] = _rope_t(done[:, :256], cp2, s1p2, s2p2).astype(bf16)
        dkv_ref[:, 256:] = done[:, 256:].astype(bf16)

        @pl.when(n < nb)
        def _():
            carry[...] = dkv_scr[pl.ds(BLK, BLK), :]

    cur = lambda n: jnp.minimum(n, nb - 1)
    prv = lambda n: jnp.maximum(n - 1, 0)
    tab_c = pl.BlockSpec((BLK, LANE), lambda n: (cur(n), 0))
    tab_p = pl.BlockSpec((BLK, LANE), lambda n: (prv(n), 0))
    return pl.pallas_call(
        body, name=name, grid=(nb + 1,),
        in_specs=[pl.BlockSpec(memory_space=pltpu.SMEM),
                  pl.BlockSpec((BLK, Q_END), lambda n: (cur(n), 0)),
                  pl.BlockSpec((BLK, KV_W), lambda n: (cur(n), kvb)),
                  pl.BlockSpec((BLK, KV_W), lambda n: (prv(n), kvb)),
                  pl.BlockSpec((BLK, Q_END), lambda n: (cur(n), 0)),
                  tab_c, tab_c, tab_c, tab_p, tab_p, tab_p],
        out_specs=[pl.BlockSpec((BLK, Q_END), lambda n: (cur(n), 0)),
                   pl.BlockSpec((BLK, KV_W), lambda n: (prv(n), 0)),
                   pl.BlockSpec((1, LANE), lambda n: (0, 0))],
        out_shape=[jax.ShapeDtypeStruct((S, Q_END), bf16), jax.ShapeDtypeStruct((S, KV_W), bf16),
                   jax.ShapeDtypeStruct((1, LANE), f32)],
        scratch_shapes=[pltpu.VMEM((BLK, KV_W), f32), pltpu.VMEM((BLK, Q_END), f32), pltpu.VMEM((2 * BLK, KV_W), f32)],
        compiler_params=_params("arbitrary"),
    )(sinks, p, p, p, dcat, rope_c, rope_s1, rope_s2, rope_c, rope_s1, rope_s2)


A1_BLK = V_END // CONV_CH
A2_BLK = A1_BLK + 1


def _ln_stats(y):
    mu = jnp.mean(y, axis=-1, keepdims=True)
    xc = y - mu
    rstd = lax.rsqrt(jnp.mean(xc * xc, axis=-1, keepdims=True) + NORM_EPS)
    return xc * rstd, rstd


def conv_fwd(p, w, b, lng, lnb, name):
    S = p.shape[0]
    T = _tile(S, 256)
    r = T // HALO

    def body(a1_ref, a2_ref, h1_ref, h2_ref, w_ref, b_ref, g_ref, bb_ref, o_ref, y_ref, scr):
        i = pl.program_id(0)
        halo = h1_ref[...] * _sig(h2_ref[...])
        scr[pl.ds(0, HALO), :] = jnp.where(i > 0, halo, 0.0)
        scr[pl.ds(HALO, T), :] = a1_ref[...] * _sig(a2_ref[...])
        acc = jnp.zeros((T, CONV_CH), f32) + b_ref[...]
        for j in range(CONV_W):
            acc = acc + scr[pl.ds(HALO - (CONV_W - 1) + j, T), :] * w_ref[j:j + 1, :]
        y_ref[...] = acc
        yh, _ = _ln_stats(acc)
        z = yh * g_ref[...] + bb_ref[...]
        o_ref[...] = (z * _sig(z)).astype(bf16)

    vec = pl.BlockSpec((1, CONV_CH), lambda i: (0, 0))
    halo_map = lambda i: jnp.maximum(i * r - 1, 0)
    return pl.pallas_call(
        body, name=name, grid=(S // T,),
        in_specs=[pl.BlockSpec((T, CONV_CH), lambda i: (i, A1_BLK)), pl.BlockSpec((T, CONV_CH), lambda i: (i, A2_BLK)),
                  pl.BlockSpec((HALO, CONV_CH), lambda i: (halo_map(i), A1_BLK)),
                  pl.BlockSpec((HALO, CONV_CH), lambda i: (halo_map(i), A2_BLK)),
                  pl.BlockSpec((HALO, CONV_CH), lambda i: (0, 0)), vec, vec, vec],
        out_specs=[pl.BlockSpec((T, CONV_CH), lambda i: (i, 0)), pl.BlockSpec((T, CONV_CH), lambda i: (i, 0))],
        out_shape=[jax.ShapeDtypeStruct((S, CONV_CH), bf16), jax.ShapeDtypeStruct((S, CONV_CH), f32)],
        scratch_shapes=[pltpu.VMEM((T + HALO, CONV_CH), f32)],
        compiler_params=_params("parallel"),
    )(p, p, p, p, w, b, lng, lnb)


def conv_bwd(p, y, dcat, w, lng, lnb, name):
    S = p.shape[0]
    T = _tile(S, 256)
    n = S // T
    r = T // HALO
    dcb = Q_END // CONV_CH

    def body(a1_ref, a2_ref, h1_ref, h2_ref, y_ref, yn_ref, do_ref, don_ref, w_ref, g_ref, bb_ref,
             da_ref, dw_ref, db_ref, dg_ref, dbb_ref, scr_h, scr_dy, acc_b, acc_g, acc_bb):
        i = pl.program_id(0)

        @pl.when(i == 0)
        def _():
            dw_ref[...] = jnp.zeros_like(dw_ref)
            acc_b[...] = jnp.zeros_like(acc_b)
            acc_g[...] = jnp.zeros_like(acc_g)
            acc_bb[...] = jnp.zeros_like(acc_bb)

        gv, bv = g_ref[...], bb_ref[...]

        def ln_silu_bwd(yv, dout):
            yh, rstd = _ln_stats(yv)
            z = yh * gv + bv
            sg = _sig(z)
            dz = dout * (sg * (1.0 + z * (1.0 - sg)))
            gz = dz * gv
            dy = rstd * (gz - jnp.mean(gz, axis=-1, keepdims=True) - yh * jnp.mean(gz * yh, axis=-1, keepdims=True))
            return dy, dz, yh

        dy, dz, yh = ln_silu_bwd(y_ref[...], do_ref[...])
        dyn, _, _ = ln_silu_bwd(yn_ref[...], don_ref[...])
        acc_g[...] += _rowsum8(dz * yh)
        acc_bb[...] += _rowsum8(dz)
        acc_b[...] += _rowsum8(dy)
        scr_dy[pl.ds(0, T), :] = dy
        scr_dy[pl.ds(T, HALO), :] = jnp.where(i < n - 1, dyn, 0.0)
        a1, a2 = a1_ref[...], a2_ref[...]
        sg2 = _sig(a2)
        halo = h1_ref[...] * _sig(h2_ref[...])
        scr_h[pl.ds(0, HALO), :] = jnp.where(i > 0, halo, 0.0)
        scr_h[pl.ds(HALO, T), :] = a1 * sg2
        dh = jnp.zeros((T, CONV_CH), f32)
        for j in range(CONV_W):
            dh = dh + scr_dy[pl.ds(CONV_W - 1 - j, T), :] * w_ref[j:j + 1, :]
            dw_ref[j:j + 1, :] += jnp.sum(dy * scr_h[pl.ds(HALO - (CONV_W - 1) + j, T), :], axis=0, keepdims=True)
        da_ref[:, :CONV_CH] = (dh * sg2).astype(bf16)
        da_ref[:, CONV_CH:] = (dh * a1 * sg2 * (1.0 - sg2)).astype(bf16)

        @pl.when(i == n - 1)
        def _():
            db_ref[...] = jnp.sum(acc_b[...], axis=0, keepdims=True)
            dg_ref[...] = jnp.sum(acc_g[...], axis=0, keepdims=True)
            dbb_ref[...] = jnp.sum(acc_bb[...], axis=0, keepdims=True)

    vec = pl.BlockSpec((1, CONV_CH), lambda i: (0, 0))
    tap = pl.BlockSpec((HALO, CONV_CH), lambda i: (0, 0))
    prev_map = lambda i: jnp.maximum(i * r - 1, 0)
    next_map = lambda i: jnp.minimum((i + 1) * r, S // HALO - 1)
    return pl.pallas_call(
        body, name=name, grid=(n,),
        in_specs=[pl.BlockSpec((T, CONV_CH), lambda i: (i, A1_BLK)), pl.BlockSpec((T, CONV_CH), lambda i: (i, A2_BLK)),
                  pl.BlockSpec((HALO, CONV_CH), lambda i: (prev_map(i), A1_BLK)),
                  pl.BlockSpec((HALO, CONV_CH), lambda i: (prev_map(i), A2_BLK)),
                  pl.BlockSpec((T, CONV_CH), lambda i: (i, 0)),
                  pl.BlockSpec((HALO, CONV_CH), lambda i: (next_map(i), 0)),
                  pl.BlockSpec((T, CONV_CH), lambda i: (i, dcb)),
                  pl.BlockSpec((HALO, CONV_CH), lambda i: (next_map(i), dcb)),
                  tap, vec, vec],
        out_specs=[pl.BlockSpec((T, 2 * CONV_CH), lambda i: (i, 0)), tap, vec, vec, vec],
        out_shape=[jax.ShapeDtypeStruct((S, 2 * CONV_CH), bf16), jax.ShapeDtypeStruct((HALO, CONV_CH), f32),
                   jax.ShapeDtypeStruct((1, CONV_CH), f32), jax.ShapeDtypeStruct((1, CONV_CH), f32),
                   jax.ShapeDtypeStruct((1, CONV_CH), f32)],
        scratch_shapes=[pltpu.VMEM((T + HALO, CONV_CH), f32), pltpu.VMEM((T + HALO, CONV_CH), f32),
                        pltpu.VMEM((8, CONV_CH), f32), pltpu.VMEM((8, CONV_CH), f32), pltpu.VMEM((8, CONV_CH), f32)],
        compiler_params=_params("arbitrary"),
    )(p, p, p, p, y, y, dcat, dcat, w, lng, lnb)


U_BLK = (V_END + 2 * CONV_CH) // SGU_CH
SV_BLK = U_BLK + 1


def _tril(w, transposed=False):
    row = lax.broadcasted_iota(jnp.int32, (BLK, BLK), 0)
    col = lax.broadcasted_iota(jnp.int32, (BLK, BLK), 1)
    keep = (col >= row) if transposed else (row >= col)
    return jnp.where(keep, w, 0.0)


def sgu_fwd(p, lng, lnb, w, bias, name):
    S = p.shape[0]
    T = _tile(S, 256)

    def body(u_ref, v_ref, g_ref, bb_ref, w_ref, bias_ref, o_ref):
        yh, _ = _ln_stats(v_ref[...])
        v = (yh * g_ref[...] + bb_ref[...]).astype(bf16)
        low = lax.broadcasted_iota(jnp.int32, (BLK, LANE), 1) < HEAD_DIM
        for pr in range(SGU_HEADS // 2):
            lanes = pl.ds(pr * LANE, LANE)
            w0 = _tril(w_ref[2 * pr]).astype(bf16)
            w1 = _tril(w_ref[2 * pr + 1]).astype(bf16)
            for c in range(T // BLK):
                rows = pl.ds(c * BLK, BLK)
                vp = v[c * BLK:(c + 1) * BLK, pr * LANE:(pr + 1) * LANE]
                mixed = jnp.where(low, _dot(w0, vp), _dot(w1, vp)) + bias_ref[:, lanes]
                o_ref[rows, lanes] = (u_ref[rows, lanes] * mixed).astype(bf16)

    vec = pl.BlockSpec((1, SGU_CH), lambda i: (0, 0))
    return pl.pallas_call(
        body, name=name, grid=(S // T,),
        in_specs=[pl.BlockSpec((T, SGU_CH), lambda i: (i, U_BLK)), pl.BlockSpec((T, SGU_CH), lambda i: (i, SV_BLK)),
                  vec, vec, pl.BlockSpec((SGU_HEADS, BLK, BLK), lambda i: (0, 0, 0)),
                  pl.BlockSpec((BLK, SGU_CH), lambda i: (0, 0))],
        out_specs=pl.BlockSpec((T, SGU_CH), lambda i: (i, 0)),
        out_shape=jax.ShapeDtypeStruct((S, SGU_CH), bf16),
        compiler_params=_params("parallel"),
    )(p, p, lng, lnb, w, bias)


def sgu_bwd(p, dcat, lng, lnb, w, wt, bias, name):
    S = p.shape[0]
    T = _tile(S, 256)
    n = S // T
    dsb = (Q_END + CONV_CH) // SGU_CH

    def body(u_ref, v_ref, do_ref, g_ref, bb_ref, w_ref, wt_ref, bias_ref,
             da_ref, dw_ref, db_ref, dg_ref, dbb_ref, dv_scr, acc_bias, acc_g, acc_bb):
        i = pl.program_id(0)

        @pl.when(i == 0)
        def _():
            dw_ref[...] = jnp.zeros_like(dw_ref)
            acc_bias[...] = jnp.zeros_like(acc_bias)
            acc_g[...] = jnp.zeros_like(acc_g)
            acc_bb[...] = jnp.zeros_like(acc_bb)

        gv = g_ref[...]
        yh, rstd = _ln_stats(v_ref[...])
        v = (yh * gv + bb_ref[...]).astype(bf16)
        low = lax.broadcasted_iota(jnp.int32, (BLK, LANE), 1) < HEAD_DIM
        for pr in range(SGU_HEADS // 2):
            lanes = pl.ds(pr * LANE, LANE)
            w0 = _tril(w_ref[2 * pr]).astype(bf16)
            w1 = _tril(w_ref[2 * pr + 1]).astype(bf16)
            wt0 = _tril(wt_ref[2 * pr], True).astype(bf16)
            wt1 = _tril(wt_ref[2 * pr + 1], True).astype(bf16)
            dw0 = jnp.zeros((BLK, BLK), f32)
            dw1 = jnp.zeros((BLK, BLK), f32)
            for c in range(T // BLK):
                rows = pl.ds(c * BLK, BLK)
                vp = v[c * BLK:(c + 1) * BLK, pr * LANE:(pr + 1) * LANE]
                mixed = jnp.where(low, _dot(w0, vp), _dot(w1, vp)) + bias_ref[:, lanes]
                do = do_ref[rows, lanes]
                da_ref[rows, lanes] = (do * mixed).astype(bf16)
                dm = do * u_ref[rows, lanes]
                acc_bias[:, lanes] += dm
                dmb = dm.astype(bf16)
                dv_scr[rows, lanes] = jnp.where(low, _dot(wt0, dmb), _dot(wt1, dmb))
                zero = jnp.zeros_like(dmb)
                dw0 = dw0 + _dot(jnp.where(low, dmb, zero), vp, NT)
                dw1 = dw1 + _dot(jnp.where(low, zero, dmb), vp, NT)
            dw_ref[2 * pr] += _tril(dw0)
            dw_ref[2 * pr + 1] += _tril(dw1)
        dv = dv_scr[...]
        acc_g[...] += _rowsum8(dv * yh)
        acc_bb[...] += _rowsum8(dv)
        gz = dv * gv
        dvr = rstd * (gz - jnp.mean(gz, axis=-1, keepdims=True) - yh * jnp.mean(gz * yh, axis=-1, keepdims=True))
        da_ref[:, SGU_CH:] = dvr.astype(bf16)

        @pl.when(i == n - 1)
        def _():
            ch = lax.broadcasted_iota(jnp.int32, (SGU_CH, LANE), 0) // HEAD_DIM
            hd = lax.broadcasted_iota(jnp.int32, (SGU_CH, LANE), 1)
            fold = jnp.where(ch == hd, 1.0, 0.0).astype(f32)
            db_ref[...] = jnp.dot(acc_bias[...], fold, preferred_element_type=f32, precision=lax.Precision.HIGHEST)
            dg_ref[...] = jnp.sum(acc_g[...], axis=0, keepdims=True)
            dbb_ref[...] = jnp.sum(acc_bb[...], axis=0, keepdims=True)

    vec = pl.BlockSpec((1, SGU_CH), lambda i: (0, 0))
    wsp = pl.BlockSpec((SGU_HEADS, BLK, BLK), lambda i: (0, 0, 0))
    return pl.pallas_call(
        body, name=name, grid=(n,),
        in_specs=[pl.BlockSpec((T, SGU_CH), lambda i: (i, U_BLK)), pl.BlockSpec((T, SGU_CH), lambda i: (i, SV_BLK)),
                  pl.BlockSpec((T, SGU_CH), lambda i: (i, dsb)), vec, vec, wsp, wsp,
                  pl.BlockSpec((BLK, SGU_CH), lambda i: (0, 0))],
        out_specs=[pl.BlockSpec((T, 2 * SGU_CH), lambda i: (i, 0)), wsp,
                   pl.BlockSpec((BLK, LANE), lambda i: (0, 0)), vec, vec],
        out_shape=[jax.ShapeDtypeStruct((S, 2 * SGU_CH), bf16), jax.ShapeDtypeStruct((SGU_HEADS, BLK, BLK), f32),
                   jax.ShapeDtypeStruct((BLK, LANE), f32), jax.ShapeDtypeStruct((1, SGU_CH), f32),
                   jax.ShapeDtypeStruct((1, SGU_CH), f32)],
        scratch_shapes=[pltpu.VMEM((T, SGU_CH), f32), pltpu.VMEM((BLK, SGU_CH), f32),
                        pltpu.VMEM((8, SGU_CH), f32), pltpu.VMEM((8, SGU_CH), f32)],
        compiler_params=_params("arbitrary"),
    )(p, p, dcat, lng, lnb, w, wt, bias)


HBM = pl.BlockSpec(memory_space=pltpu.HBM)
SEM = pl.BlockSpec(memory_space=pltpu.SEMAPHORE)
ANY = pl.BlockSpec(memory_space=pl.ANY)
EFFECT = pltpu.SideEffectType.DATAFLOW_SIDE_EFFECTING


def _flip(x, y, c, k):
    px, py, pc = x ^ (k >> 2), y ^ ((k >> 1) & 1), c ^ (k & 1)
    return (px, py, pc), 4 * px + 2 * py + pc


def _routes_gather(x, y, c):
    me = 4 * x + 2 * y + c
    out = []
    for k in (1, 2, 4, 6):
        dev, idx = _flip(x, y, c, k)
        out.append((dev, None, me, idx))
    return out


def _routes_pair(x, y, c):
    dev, _ = _flip(x, y, c, 1)
    return [(dev, 2 * q + (1 - c), q, q) for q in range(N_DEV // 2)]


def _routes_chips(x, y, c):
    out = []
    for k in (2, 4, 6):
        dev, idx = _flip(x, y, c, k)
        out.append((dev, idx // 2, 2 * x + y, idx // 2))
    return out


def _routes_forward(x, y, c):
    sib, _ = _flip(x, y, c, 1)
    out = []
    for k in (2, 4, 6):
        _, idx = _flip(x, y, c, k)
        out.append((sib, idx, idx, idx ^ 1))
    return out


def _routes_all(x, y, c):
    me = 4 * x + 2 * y + c
    out = []
    for k in range(1, N_DEV):
        dev, idx = _flip(x, y, c, k)
        out.append((dev, None, me, idx))
    return out


def _slot(ref, slot, kind):
    if slot is None:
        return ref
    if kind == "cols":
        width = ref.shape[2] // (N_DEV // 2)
        return ref.at[slot // (N_DEV // 2), :, pl.ds(pl.multiple_of((slot % (N_DEV // 2)) * width, LANE), width)]
    return ref.at[slot]


def _copies(routes, srcs, lands, send_sems, recv_sems, incoming, src_kinds, land_kinds):
    x, y, c = lax.axis_index("x"), lax.axis_index("y"), lax.axis_index("c")
    out = []
    n = len(lands)
    if srcs is None:
        srcs, src_kinds = lands, land_kinds
    for k, (dev, src_slot, dst_slot, recv_slot) in enumerate(routes(x, y, c)):
        for a in range(n):
            out.append(pltpu.make_async_remote_copy(
                src_ref=_slot(srcs[a], src_slot, src_kinds[a]),
                dst_ref=_slot(lands[a], recv_slot if incoming else dst_slot, land_kinds[a]),
                send_sem=send_sems.at[k * n + a], recv_sem=recv_sems.at[k * n + a], device_id=dev, device_id_type=MESH))
    return out


def _pin(a):
    return pltpu.with_memory_space_constraint(a, pltpu.HBM)


def split_start(srcs, lands, routes, name, deps=(), src_kinds=None, land_kinds=None):
    n = len(lands)
    ns = 0 if srcs is None else n
    n_routes = len(routes(0, 0, 0))
    ops = ([] if srcs is None else list(srcs)) + list(lands)
    src_kinds = src_kinds or ["rows"] * n
    land_kinds = land_kinds or ["rows"] * n

    def body(*refs):
        src, land = (refs[:n] if ns else None), refs[ns:ns + n]
        first_out = ns + n + len(deps)
        send_sems, recv_sems, token = refs[first_out], refs[first_out + 1], refs[-1]
        for cp in _copies(routes, src, land, send_sems, recv_sems, False, src_kinds, land_kinds):
            cp.start()
        token[...] = jnp.zeros_like(token)

    thru = [pltpu.HBM(a.shape, a.dtype) for a in ops]
    res = pl.pallas_call(
        body, name=name,
        out_shape=(pltpu.SemaphoreType.DMA((n * n_routes,)), pltpu.SemaphoreType.DMA((n * n_routes,)), *thru,
                   jax.ShapeDtypeStruct((8, LANE), f32)),
        in_specs=[HBM] * len(ops) + [ANY] * len(deps),
        out_specs=(SEM, SEM, *([HBM] * len(ops)), pl.BlockSpec(memory_space=pltpu.VMEM)),
        input_output_aliases={i: 2 + i for i in range(len(ops))},
        compiler_params=pltpu.CompilerParams(has_side_effects=EFFECT),
    )(*[_pin(a) for a in ops], *deps)
    return (res[0], res[1], (list(res[2:2 + n]) if ns else None), list(res[2 + ns:2 + ns + n]), res[-1],
            (src_kinds, land_kinds))


def split_wait(started, after, routes, name):
    send_sems, recv_sems, srcs, lands, _, (src_kinds, land_kinds) = started
    n = len(lands)
    ns = 0 if srcs is None else n
    ops = ([] if srcs is None else list(srcs)) + list(lands)
    afters = list(after) if isinstance(after, (list, tuple)) else [after]

    def body(*refs):
        src, land = (refs[:n] if ns else None), refs[ns:ns + n]
        send_s, recv_s = refs[ns + n], refs[ns + n + 1]
        for cp in _copies(routes, src, land, send_s, recv_s, True, src_kinds, land_kinds):
            cp.wait_send()
            cp.wait_recv()

    thru = [pltpu.HBM(a.shape, a.dtype) for a in ops]
    res = pl.pallas_call(
        body, name=name, out_shape=tuple(thru),
        in_specs=[HBM] * len(ops) + [SEM, SEM] + [ANY] * len(afters), out_specs=tuple([HBM] * len(ops)),
        input_output_aliases={i: i for i in range(len(ops))},
        compiler_params=pltpu.CompilerParams(has_side_effects=EFFECT),
    )(*ops, send_sems, recv_sems, *afters)
    return (list(res[:n]) if ns else None), list(res[ns:ns + n])


def chip_sum(parts, land, c_idx, kind, name):
    _, R, C = land.shape
    tr = R if R * C * 2 <= 3 * 2 ** 20 else _tile(R, 512)
    half = N_DEV // 2

    def body(c_ref, p_ref, l_ref, o_ref):
        o_ref[...] = (p_ref[...].astype(f32) + l_ref[...].astype(f32)).astype(bf16)

    if kind == "cols":
        mine = lambda q, i, c_ref: ((2 * q + c_ref[0]) // half, i, (2 * q + c_ref[0]) % half)
    else:
        mine = lambda q, i, c_ref: (2 * q + c_ref[0], i, 0)
    return pl.pallas_call(
        body, name=name,
        grid_spec=pltpu.PrefetchScalarGridSpec(
            num_scalar_prefetch=1, grid=(half, R // tr),
            in_specs=[pl.BlockSpec((None, tr, C), mine), pl.BlockSpec((None, tr, C), lambda q, i, c_ref: (q, i, 0))],
            out_specs=pl.BlockSpec((None, tr, C), lambda q, i, c_ref: (q, i, 0))),
        out_shape=jax.ShapeDtypeStruct((half, R, C), bf16),
        compiler_params=_params("parallel", "parallel"),
    )(c_idx, parts, land)


def place_own(land, src, me_idx, kind, name):
    R, C = src.shape
    tr = _tile(R, 512)
    half = N_DEV // 2
    if kind == "cols":
        where = lambda i, m: (m[0] // half, i, m[0] % half)
    else:
        where = lambda i, m: (m[0], i, 0)

    def body(m_ref, land_ref, src_ref, out_ref):
        out_ref[...] = src_ref[...]

    return pl.pallas_call(
        body, name=name,
        grid_spec=pltpu.PrefetchScalarGridSpec(
            num_scalar_prefetch=1, grid=(R // tr,),
            in_specs=[ANY, pl.BlockSpec((tr, C), lambda i, m: (i, 0))],
            out_specs=pl.BlockSpec((None, tr, C), where)),
        out_shape=jax.ShapeDtypeStruct(land.shape, land.dtype),
        input_output_aliases={1: 0},
        compiler_params=_params("arbitrary"),
    )(me_idx, land, src)


def sum_slots(parts, name):
    P, R, C = parts.shape
    tr = _tile(R, 512)

    def body(p_ref, o_ref):
        total = p_ref[0]
        for j in range(1, P):
            total = total + p_ref[j]
        o_ref[...] = total

    return pl.pallas_call(
        body, name=name, grid=(R // tr,),
        in_specs=[pl.BlockSpec((P, tr, C), lambda i: (0, i, 0))],
        out_specs=pl.BlockSpec((tr, C), lambda i: (i, 0)),
        out_shape=jax.ShapeDtypeStruct((R, C), f32),
        compiler_params=_params("parallel"),
    )(parts)


def adamw(parts, owns, chip, w, m, v, name):
    L, R, C = w.shape
    P = parts[0].shape[0]
    tr = _tile(R, 128 if C > 1024 else 256)
    nr = R // tr
    c1 = 1.0 - ADAM_B1 ** ADAM_STEP
    c2 = 1.0 - ADAM_B2 ** ADAM_STEP
    n_own = L if owns is not None else 0

    def body(chip_ref, *refs):
        part_refs, own_refs = refs[:L], refs[L:L + n_own]
        w_ref, m_ref, v_ref, g_out, d_out, m_out, v_out = refs[L + n_own:]
        layer = pl.program_id(0)
        for l in range(L):
            @pl.when(layer == l)
            def _(l=l):
                g = None
                for q in range(P):
                    term = part_refs[l][q].astype(f32)
                    if n_own:
                        term = jnp.where(chip_ref[0] == q, own_refs[l][...].astype(f32), term)
                    g = term if g is None else g + term
                mn = ADAM_B1 * m_ref[...] + (1.0 - ADAM_B1) * g
                vn = ADAM_B2 * v_ref[...] + (1.0 - ADAM_B2) * (g * g)
                g_out[...] = g
                m_out[...] = mn
                v_out[...] = vn
                d_out[...] = -ADAM_LR * ((mn / c1) / (jnp.sqrt(vn / c2) + ADAM_EPS) + ADAM_WD * w_ref[...])

    def rows(l, a, i):
        return jnp.where(a == l, i, jnp.where(a < l, 0, nr - 1))

    def part_spec(l):
        return pl.BlockSpec((P, tr, C), lambda a, i, chip_ref: (0, rows(l, a, i), 0))

    def own_spec(l):
        return pl.BlockSpec((None, tr, C), lambda a, i, chip_ref: (chip_ref[0], rows(l, a, i), 0))

    slab = pl.BlockSpec((None, tr, C), lambda a, i, chip_ref: (a, i, 0))
    out = jax.ShapeDtypeStruct((L, R, C), f32)
    return pl.pallas_call(
        body, name=name,
        grid_spec=pltpu.PrefetchScalarGridSpec(
            num_scalar_prefetch=1, grid=(L, nr),
            in_specs=[part_spec(l) for l in range(L)] + [own_spec(l) for l in range(n_own)] + [slab, slab, slab],
            out_specs=[slab, slab, slab, slab]),
        out_shape=[out, out, out, out],
        compiler_params=_params("arbitrary", "arbitrary"),
    )(chip, *parts, *(owns or []), w, m, v)


PACK = 8 * LANE
PACK_ROWS = 256


def _pack(arrs):
    pieces = []
    for a in arrs:
        flat = a.astype(f32).reshape(-1)
        pad = (-flat.shape[0]) % PACK
        pieces.append(jnp.pad(flat, (0, pad)).reshape(-1, LANE))
    rows = sum(p.shape[0] for p in pieces)
    if rows > PACK_ROWS and rows % PACK_ROWS:
        pieces.append(jnp.zeros((PACK_ROWS - rows % PACK_ROWS, LANE), f32))
    return jnp.concatenate(pieces, axis=0)


def _unpack(buf, shapes):
    out, row = [], 0
    for shp in shapes:
        size = math.prod(shp)
        rows = (size + PACK - 1) // PACK * (PACK // LANE)
        out.append(buf[row:row + rows].reshape(-1)[:size].reshape(shp))
        row += rows
    return out


def _rope_tables(positions):
    half = ROT_DIM // 2
    inv_freq = 1.0 / (ROPE_THETA ** (jnp.arange(0, ROT_DIM, 2, dtype=f32) / ROT_DIM))
    ang = positions.astype(f32)[:, None] * inv_freq
    cos, sin = jnp.cos(ang), jnp.sin(ang)
    S = positions.shape[0]
    zeros, ones = jnp.zeros((S, half), f32), jnp.ones((S, HEAD_DIM - ROT_DIM), f32)
    rest = jnp.zeros((S, HEAD_DIM - ROT_DIM), f32)
    c = jnp.concatenate([cos, cos, ones], axis=1)
    s1 = jnp.concatenate([zeros, sin, rest], axis=1)
    s2 = jnp.concatenate([-sin, zeros, rest], axis=1)
    return tuple(jnp.tile(t, (1, LANE // HEAD_DIM)) for t in (c, s1, s2))


def _cols_to_shards(g):
    lead, (R, N) = g.shape[:-2], g.shape[-2:]
    g = g.reshape(lead + (R, N_DEV, N // N_DEV))
    return jnp.moveaxis(g, -2, 0)


def _shards_to_cols(g):
    g = jnp.moveaxis(g, 0, -2)
    return g.reshape(g.shape[:-2] + (g.shape[-2] * g.shape[-1],))


def kernel(x, positions, norm_ffn1, ffn1_w_in, ffn1_w_out, norm_mix, w_in, conv_dw_w, conv_dw_b, conv_ln_g, conv_ln_b, sgu_ln_g, sgu_ln_b, sgu_w, sgu_b, attn_sinks, w_out, norm_ffn2, ffn2_w_in, ffn2_w_out, final_norm, loss_target, m_norm_ffn1, m_ffn1_w_in, m_ffn1_w_out, m_norm_mix, m_w_in, m_conv_dw_w, m_conv_dw_b, m_conv_ln_g, m_conv_ln_b, m_sgu_ln_g, m_sgu_ln_b, m_sgu_w, m_sgu_b, m_attn_sinks, m_w_out, m_norm_ffn2, m_ffn2_w_in, m_ffn2_w_out, m_final_norm, v_norm_ffn1, v_ffn1_w_in, v_ffn1_w_out, v_norm_mix, v_w_in, v_conv_dw_w, v_conv_dw_b, v_conv_ln_g, v_conv_ln_b, v_sgu_ln_g, v_sgu_ln_b, v_sgu_w, v_sgu_b, v_attn_sinks, v_w_out, v_norm_ffn2, v_ffn2_w_in, v_ffn2_w_out, v_final_norm):
    L = norm_ffn1.shape[0]
    S, D = x.shape[1], x.shape[2]
    F = ffn1_w_out.shape[1] * N_DEV
    me = 4 * lax.axis_index("x") + 2 * lax.axis_index("y") + lax.axis_index("c")
    x0 = x[0]
    rope_c, rope_s1, rope_s2 = _rope_tables(positions[0])
    cw = CONV_CH // N_DEV

    c_idx = lax.axis_index("c").astype(jnp.int32).reshape(1)
    chip = (2 * lax.axis_index("x") + lax.axis_index("y")).astype(jnp.int32).reshape(1)
    no_chip = jnp.zeros((1,), jnp.int32)
    me_idx = me.astype(jnp.int32).reshape(1)

    row = lambda a: a.reshape(1, -1)
    order_fwd = [(l, g) for l in range(L) for g in (("ffn1_in", "ffn1_out") if l == 0 else ("ffn1",)) + ("mix", "ffn2")]

    def group_srcs(l, grp):
        if grp == "mix":
            taps = jnp.pad(conv_dw_w[l], ((0, HALO - CONV_W), (0, LANE - cw)))
            return [w_in[l].astype(bf16), w_out[l].astype(bf16), taps]
        both = ([ffn2_w_in[l], ffn2_w_out[l]] if grp == "ffn2" else [ffn1_w_in[l], ffn1_w_out[l]])
        both = [a.astype(bf16) for a in both]
        return both[:1] if grp == "ffn1_in" else both[1:] if grp == "ffn1_out" else both

    def kinds_of(grp):
        return {"mix": ["rows"] * 3, "ffn1_in": ["cols"], "ffn1_out": ["rows"]}.get(grp, ["cols", "rows"])

    def gather_start(k, deps=()):
        l, grp = order_fwd[k]
        srcs = group_srcs(l, grp)
        lands = [lax.empty((2, D, F) if kind == "cols" else (N_DEV,) + a.shape, a.dtype)
                 for a, kind in zip(srcs, kinds_of(grp))]
        return split_start(srcs, lands, _routes_gather, f"gather_start_{grp}_{l}", deps, land_kinds=kinds_of(grp))

    def gather_forward(k, started, after):
        l, grp = order_fwd[k]
        srcs, lands = split_wait(started, after, _routes_gather, f"gather_wait_{grp}_{l}")
        return srcs, split_start(None, lands, _routes_forward, f"forward_start_{grp}_{l}", land_kinds=kinds_of(grp))

    def gather_finish(k, srcs, started, after):
        l, grp = order_fwd[k]
        _, lands = split_wait(started, after, _routes_forward, f"forward_wait_{grp}_{l}")
        full = [place_own(ld, s, me_idx, kind, f"own_{grp}_{l}_{a}")
                for a, (ld, s, kind) in enumerate(zip(lands, srcs, kinds_of(grp)))]
        if grp == "mix":
            return dict(w_in=_shards_to_cols(full[0]), w_out=full[1].reshape(D, D),
                        taps=_shards_to_cols(full[2][:, :, :cw]))
        named = dict(zip(["w_in", "w_out"] if len(full) == 2 else ["w_in" if grp == "ffn1_in" else "w_out"], full))
        if "w_out" in named:
            named["w_out"] = named["w_out"].reshape(F, D)
        return named

    def ffn_head(xs, wts, g_norm, tag, sv):
        sv["x_in"] = xs
        h, sv["ht"] = rmsnorm_fwd(xs, g_norm, f"norm_{tag}")
        sv["gu"], a, sv["at"] = ffn_in(h, wts["w_in"], f"{tag}_in")
        return a

    def ffn_fwd(xs, wts, g_norm, tag, sv, mid):
        a = ffn_head(xs, wts, g_norm, tag, sv)
        return mm_res(a, wts["w_out"], xs, FFN_RES, f"{tag}_out", deps=mid(a))

    def mix_fwd(xs, wts, l, sv, mid):
        sv["x_in"] = xs
        h, sv["ht"] = rmsnorm_fwd(xs, row(norm_mix[l]), f"norm_mix_{l}")
        p = mm_nn(h, wts["w_in"][None], f"mix_in_{l}")[0]
        sv["p"] = p
        attn = attn_fwd(p, rope_c, rope_s1, rope_s2, attn_sinks[l], f"attn_fwd_{l}")
        conv, sv["conv_y"] = conv_fwd(p, wts["taps"], row(conv_dw_b[l]), row(conv_ln_g[l]), row(conv_ln_b[l]),
                                      f"conv_fwd_{l}")
        sv["sgu_bias"] = jnp.repeat(sgu_b[l].T, HEAD_DIM, axis=1)
        sgu = sgu_fwd(p, row(sgu_ln_g[l]), row(sgu_ln_b[l]), sgu_w[l], sv["sgu_bias"], f"sgu_fwd_{l}")
        cat = jnp.concatenate([attn, conv, sgu], axis=1)
        sv["catt"] = cat.T
        return mm_res(cat, wts["w_out"], xs, 1.0, f"mix_out_{l}", deps=mid(cat))

    weights, saved = {}, {}
    xs = x0
    first = gather_start(0)
    state = dict(zip(("srcs", "fwd"), gather_forward(0, first, first[4])))
    for k, (l, grp) in enumerate(order_fwd):
        nxt = gather_start(k + 1, (state["fwd"][4],)) if k + 1 < len(order_fwd) else None
        wts = gather_finish(k, state["srcs"], state["fwd"], nxt[4] if nxt else state["fwd"][4])

        def mid(after, k=k, nxt=nxt):
            if not nxt:
                return ()
            state["srcs"], state["fwd"] = gather_forward(k + 1, nxt, after)
            return (state["fwd"][4],)

        if grp == "ffn1_in":
            sv = saved[l, "ffn1"] = {}
            weights[l, "ffn1"] = wts
            head = ffn_head(xs, wts, row(norm_ffn1[l]), f"ffn1_{l}", sv)
            mid(head)
        elif grp == "ffn1_out":
            weights[l, "ffn1"].update(wts)
            xs = mm_res(head, wts["w_out"], xs, FFN_RES, f"ffn1_{l}_out")
            mid(xs)
        else:
            sv = saved[l, grp] = {}
            weights[l, grp] = wts
            if grp == "mix":
                xs = mix_fwd(xs, wts, l, sv, mid)
            else:
                xs = ffn_fwd(xs, wts, row(norm_ffn1[l] if grp == "ffn1" else norm_ffn2[l]), f"{grp}_{l}", sv, mid)

    dx, dxb, d_final_norm, loss = final_loss(xs, row(final_norm), loss_target[0], "final_loss")

    def scatter_start(grads, kinds, tag):
        half = N_DEV // 2
        lands = [lax.empty((half, g.shape[1], g.shape[2] // half) if kind == "cols" else (half,) + g.shape[1:], g.dtype)
                 for g, kind in zip(grads, kinds)]
        return split_start(grads, lands, _routes_pair, f"pair_start_{tag}", src_kinds=kinds)

    def pair_to_chips(started, after, tag):
        kinds = started[5][0]
        grads, landed = split_wait(started, after, _routes_pair, f"pair_wait_{tag}")
        sums = [chip_sum(g, ld, c_idx, kind, f"chip_sum_{tag}_{a}")
                for a, (g, ld, kind) in enumerate(zip(grads, landed, kinds))]
        return split_start(sums, [lax.empty(s.shape, s.dtype) for s in sums], _routes_chips, f"chips_start_{tag}")

    def ffn_bwd(dx, dxb, wts, sv, g_norm, tag, deps=()):
        dgu = ffn_dact(dxb, wts["w_out"], sv["gu"], f"{tag}_dact")
        d_w_out = mm_nn(sv["at"], dxb[None], f"{tag}_dwout", bf16, FFN_RES, deps=deps)[0]
        d_w_in = mm_nn(sv["ht"], dgu, f"{tag}_dwin", bf16)
        pair = scatter_start([d_w_in, d_w_out.reshape(N_DEV, F // N_DEV, D)], ["cols", "rows"], tag)
        dh = mm_nt(dgu, wts["w_in"], f"{tag}_dh", deps=(pair[4],))
        chips = pair_to_chips(pair, dh, tag)
        dx, dxb, dg = rmsnorm_bwd(dh, sv["x_in"], g_norm, dx, f"{tag}_dnorm", deps=(chips[4],))
        return dx, dxb, dg, chips

    small = [None] * L
    chips_pending = {}
    for l in reversed(range(L)):
        dx, dxb, d_norm_ffn2, chips_pending[l, "ffn2"] = ffn_bwd(
            dx, dxb, weights[l, "ffn2"], saved[l, "ffn2"], row(norm_ffn2[l]), f"ffn2_{l}")

        wts, sv = weights[l, "mix"], saved[l, "mix"]
        d_w_out = mm_nn(sv["catt"], dxb[None], f"mix_dwout_{l}", bf16)[0]
        dcat = mm_nt(dxb[None], wts["w_out"][None], f"mix_dcat_{l}", deps=(d_w_out,))
        p = sv["p"]
        dq, dkv, d_sinks = attn_bwd(p, dcat, rope_c, rope_s1, rope_s2, attn_sinks[l], f"attn_bwd_{l}")
        da_conv, d_taps, d_conv_b, d_conv_g, d_conv_bb = conv_bwd(
            p, sv["conv_y"], dcat, wts["taps"], row(conv_ln_g[l]), row(conv_ln_b[l]), f"conv_bwd_{l}")
        da_sgu, d_sgu_w, d_sgu_bias, d_sgu_g, d_sgu_bb = sgu_bwd(
            p, dcat, row(sgu_ln_g[l]), row(sgu_ln_b[l]), sgu_w[l], jnp.swapaxes(sgu_w[l], 1, 2), sv["sgu_bias"],
            f"sgu_bwd_{l}")
        dp = jnp.concatenate([dq, dkv, da_conv, da_sgu], axis=1)
        d_w_in = mm_nn(sv["ht"], dp[None], f"mix_dwin_{l}", bf16)[0]
        pair = scatter_start([_cols_to_shards(d_w_in), d_w_out.reshape(N_DEV, D // N_DEV, D)], ["rows", "rows"],
                             f"mix_{l}")
        dh = mm_nt(dp[None], wts["w_in"][None], f"mix_dh_{l}", deps=(pair[4],))
        chips_pending[l, "mix"] = pair_to_chips(pair, dh, f"mix_{l}")
        dx, dxb, d_norm_mix = rmsnorm_bwd(dh, sv["x_in"], row(norm_mix[l]), dx, f"mix_dnorm_{l}",
                                          deps=(chips_pending[l, "mix"][4],))

        small[l] = dict(norm_mix=d_norm_mix[0], conv_dw_w=d_taps[:CONV_W],
                        conv_dw_b=d_conv_b[0], conv_ln_g=d_conv_g[0], conv_ln_b=d_conv_bb[0], sgu_ln_g=d_sgu_g[0],
                        sgu_ln_b=d_sgu_bb[0], sgu_w=d_sgu_w, sgu_b=d_sgu_bias[:, :SGU_HEADS].T,
                        attn_sinks=d_sinks[0, :N_Q_HEADS], norm_ffn2=d_norm_ffn2[0])
        if l == 0:
            early_names = ["norm_mix", "conv_dw_w", "conv_dw_b", "conv_ln_g", "conv_ln_b", "sgu_ln_g", "sgu_ln_b",
                           "sgu_w", "sgu_b", "attn_sinks", "norm_ffn2"]
            early = [jnp.stack([small[k][n] for k in range(L)]) for n in early_names]
            early += [d_final_norm[0], loss.reshape(1)]
            early_shapes = [a.shape for a in early]
            early = _pack(early)
            early_pending = split_start([early], [lax.empty((N_DEV,) + early.shape, f32)], _routes_all,
                                        "small_start", deps=(dxb,))
            early_token = (early_pending[4],)
        else:
            early_token = ()

        dx, dxb, d_norm_ffn1, chips_pending[l, "ffn1"] = ffn_bwd(
            dx, dxb, weights[l, "ffn1"], saved[l, "ffn1"], row(norm_ffn1[l]), f"ffn1_{l}", early_token)
        small[l]["norm_ffn1"] = d_norm_ffn1[0]

    grad_x = dx[None]
    late = _pack([jnp.stack([small[l]["norm_ffn1"] for l in range(L)])])
    late_pending = split_start([late], [lax.empty((N_DEV,) + late.shape, f32)], _routes_all, "late_start", deps=(dx,))

    def landed(grp, after):
        got = [split_wait(chips_pending[l, grp], after, _routes_chips, f"chips_wait_{grp}_{l}") for l in range(L)]
        return [g[0] for g in got], [g[1] for g in got]

    given = dict(norm_ffn1=(norm_ffn1, m_norm_ffn1, v_norm_ffn1), norm_mix=(norm_mix, m_norm_mix, v_norm_mix),
                 conv_dw_w=(conv_dw_w, m_conv_dw_w, v_conv_dw_w), conv_dw_b=(conv_dw_b, m_conv_dw_b, v_conv_dw_b),
                 conv_ln_g=(conv_ln_g, m_conv_ln_g, v_conv_ln_g), conv_ln_b=(conv_ln_b, m_conv_ln_b, v_conv_ln_b),
                 sgu_ln_g=(sgu_ln_g, m_sgu_ln_g, v_sgu_ln_g), sgu_ln_b=(sgu_ln_b, m_sgu_ln_b, v_sgu_ln_b),
                 sgu_w=(sgu_w, m_sgu_w, v_sgu_w), sgu_b=(sgu_b, m_sgu_b, v_sgu_b),
                 attn_sinks=(attn_sinks, m_attn_sinks, v_attn_sinks), norm_ffn2=(norm_ffn2, m_norm_ffn2, v_norm_ffn2),
                 final_norm=(final_norm, m_final_norm, v_final_norm))

    def small_update(pending, after, names, shapes, tag):
        (own,), (others,) = split_wait(pending, after, _routes_all, f"{tag}_wait")
        total = _unpack(sum_slots(place_own(others, own, me_idx, "rows", f"{tag}_own"), f"{tag}_sum"), shapes)
        g = dict(zip(names, total))
        if "conv_dw_w" in g:
            g["conv_dw_w"] = lax.dynamic_slice_in_dim(g["conv_dw_w"], me * cw, cw, axis=2)
        upd_names = [n for n in names if n in given]
        upd_shapes = [given[n][0].shape for n in upd_names]
        packed = [_pack([g[n] for n in upd_names])[None]] + [_pack([given[n][k] for n in upd_names])[None]
                                                              for k in range(3)]
        res = adamw([packed[0]], None, no_chip, packed[1], packed[2], packed[3], f"{tag}_adamw")
        return g, [dict(zip(upd_names, _unpack(r[0], upd_shapes))) for r in res], res[0]

    big = {}
    done = [dx]

    def big_update(grp, names):
        sums, lands = landed(grp, done)
        for idx, (name, w, m, v) in enumerate(names):
            big[name] = adamw([lands[l][idx] for l in range(L)], [sums[l][idx] for l in range(L)], chip, w, m, v,
                              f"adamw_{name}")
            done.append(big[name][0])

    big_update("ffn2", (("ffn2_w_in", ffn2_w_in, m_ffn2_w_in, v_ffn2_w_in),
                        ("ffn2_w_out", ffn2_w_out, m_ffn2_w_out, v_ffn2_w_out)))
    big_update("mix", (("w_in", w_in, m_w_in, v_w_in), ("w_out", w_out, m_w_out, v_w_out)))
    g_early, upd_early, marker = small_update(early_pending, done, early_names + ["final_norm", "loss"], early_shapes,
                                              "small")
    done.append(marker)
    big_update("ffn1", (("ffn1_w_in", ffn1_w_in, m_ffn1_w_in, v_ffn1_w_in),
                        ("ffn1_w_out", ffn1_w_out, m_ffn1_w_out, v_ffn1_w_out)))
    _, upd_late, _ = small_update(late_pending, done, ["norm_ffn1"], [(L, D)], "late")
    upd = [{**upd_early[k], **upd_late[k]} for k in range(4)]

    order = ["norm_ffn1", "ffn1_w_in", "ffn1_w_out", "norm_mix", "w_in", "conv_dw_w", "conv_dw_b", "conv_ln_g",
             "conv_ln_b", "sgu_ln_g", "sgu_ln_b", "sgu_w", "sgu_b", "attn_sinks", "w_out", "norm_ffn2", "ffn2_w_in",
             "ffn2_w_out", "final_norm"]
    outs = [g_early["loss"].reshape(()), grad_x]
    for k in range(4):
        outs += [big[n][k] if n in big else upd[k][n] for n in order]
    return tuple(outs)
```

```python
import functools
import math

import jax
import jax.numpy as jnp
from jax import lax
from jax.experimental import pallas as pl
from jax.experimental.pallas import tpu as pltpu

f32 = jnp.float32
bf16 = jnp.bfloat16

N_DEV = 8
HEAD_DIM = 64
N_Q_HEADS = 16
N_KV_HEADS = 4
GQ = N_Q_HEADS // N_KV_HEADS
BLK = 128
ROT_DIM = 16
ROPE_THETA = 500000.0
CONV_W = 31
CONV_CH = 512
SGU_CH = 512
SGU_HEADS = 8
Q_END = N_Q_HEADS * HEAD_DIM
KV_W = 2 * N_KV_HEADS * HEAD_DIM
V_END = Q_END + KV_W
IN_COLS = V_END + 2 * CONV_CH + 2 * SGU_CH
HALO = 32
NORM_EPS = 1e-5
FFN_RES = 0.5
ADAM_LR, ADAM_B1, ADAM_B2, ADAM_EPS, ADAM_WD, ADAM_STEP = 0.001, 0.9, 0.999, 1e-08, 0.01, 10
LANE = 128
VMEM_LIMIT = 56 * 2 ** 20
MESH = pl.DeviceIdType.MESH

NN = (((1,), (0,)), ((), ()))
NT = (((1,), (1,)), ((), ()))
TN = (((0,), (0,)), ((), ()))


def _tile(dim, pref):
    t = min(pref, dim)
    while dim % t:
        t //= 2
    return t


def _params(*sem):
    return pltpu.CompilerParams(dimension_semantics=sem, vmem_limit_bytes=VMEM_LIMIT)


def _sig(x):
    return 1.0 / (1.0 + jnp.exp(-x))


def _dot(a, b, dims=NN):
    return lax.dot_general(a, b, dims, preferred_element_type=f32)


def _rowsum8(x):
    return x.reshape(x.shape[0] // 8, 8, x.shape[1]).sum(axis=0)


def rmsnorm_fwd(x, g, name, deps=()):
    S, D = x.shape
    tm = _tile(S, 512)

    def body(x_ref, g_ref, *rest):
        o_ref, ot_ref = rest[-2:]
        xv = x_ref[...]
        r = lax.rsqrt(jnp.mean(xv * xv, axis=-1, keepdims=True) + NORM_EPS)
        hb = (xv * r * g_ref[...]).astype(bf16)
        o_ref[...] = hb
        ot_ref[...] = hb.T

    return pl.pallas_call(
        body, name=name, grid=(S // tm,),
        in_specs=[pl.BlockSpec((tm, D), lambda i: (i, 0)), pl.BlockSpec((1, D), lambda i: (0, 0))] + [ANY] * len(deps),
        out_specs=[pl.BlockSpec((tm, D), lambda i: (i, 0)), pl.BlockSpec((D, tm), lambda i: (0, i))],
        out_shape=[jax.ShapeDtypeStruct((S, D), bf16), jax.ShapeDtypeStruct((D, S), bf16)],
        compiler_params=_params("parallel"),
    )(x, g, *deps)


def rmsnorm_bwd(dh, x, g, dres, name, deps=()):
    S, D = x.shape
    tm = _tile(S, 256)
    n = S // tm

    def body(dh_ref, x_ref, g_ref, dres_ref, *rest):
        dx_ref, dxb_ref, dg_ref, acc = rest[-4:]
        i = pl.program_id(0)

        @pl.when(i == 0)
        def _():
            acc[...] = jnp.zeros_like(acc)

        xv = x_ref[...]
        r = lax.rsqrt(jnp.mean(xv * xv, axis=-1, keepdims=True) + NORM_EPS)
        xh = xv * r
        dy = dh_ref[...]
        gy = dy * g_ref[...]
        dx = dres_ref[...] + r * (gy - xh * jnp.mean(gy * xh, axis=-1, keepdims=True))
        dx_ref[...] = dx
        dxb_ref[...] = dx.astype(bf16)
        acc[...] += _rowsum8(dy * xh)

        @pl.when(i == n - 1)
        def _():
            dg_ref[...] = jnp.sum(acc[...], axis=0, keepdims=True)

    row = pl.BlockSpec((tm, D), lambda i: (i, 0))
    vec = pl.BlockSpec((1, D), lambda i: (0, 0))
    return pl.pallas_call(
        body, name=name, grid=(n,),
        in_specs=[row, row, vec, row] + [ANY] * len(deps),
        out_specs=[row, row, vec],
        out_shape=[jax.ShapeDtypeStruct((S, D), f32), jax.ShapeDtypeStruct((S, D), bf16),
                   jax.ShapeDtypeStruct((1, D), f32)],
        scratch_shapes=[pltpu.VMEM((8, D), f32)],
        compiler_params=_params("arbitrary"),
    )(dh, x, g, dres, *deps)


def final_loss(x, g, tgt, name):
    S, D = x.shape
    tm = _tile(S, 256)
    n = S // tm

    def body(x_ref, g_ref, t_ref, dx_ref, dxb_ref, dg_ref, loss_ref, acc):
        i = pl.program_id(0)

        @pl.when(i == 0)
        def _():
            acc[...] = jnp.zeros_like(acc)
            loss_ref[...] = jnp.zeros_like(loss_ref)

        xv = x_ref[...]
        gv = g_ref[...]
        r = lax.rsqrt(jnp.mean(xv * xv, axis=-1, keepdims=True) + NORM_EPS)
        xh = xv * r
        diff = xh * gv - t_ref[...]
        tok = jnp.mean(diff * diff, axis=-1, keepdims=True)
        loss_ref[...] += 0.5 * jnp.sum(tok, axis=0, keepdims=True)
        dy = diff / D
        gy = dy * gv
        dx = r * (gy - xh * jnp.mean(gy * xh, axis=-1, keepdims=True))
        dx_ref[...] = dx
        dxb_ref[...] = dx.astype(bf16)
        acc[...] += _rowsum8(dy * xh)

        @pl.when(i == n - 1)
        def _():
            dg_ref[...] = jnp.sum(acc[...], axis=0, keepdims=True)

    row = pl.BlockSpec((tm, D), lambda i: (i, 0))
    vec = pl.BlockSpec((1, D), lambda i: (0, 0))
    return pl.pallas_call(
        body, name=name, grid=(n,),
        in_specs=[row, vec, row],
        out_specs=[row, row, vec, pl.BlockSpec((1, 1), lambda i: (0, 0))],
        out_shape=[jax.ShapeDtypeStruct((S, D), f32), jax.ShapeDtypeStruct((S, D), bf16),
                   jax.ShapeDtypeStruct((1, D), f32), jax.ShapeDtypeStruct((1, 1), f32)],
        scratch_shapes=[pltpu.VMEM((8, D), f32)],
        compiler_params=_params("arbitrary"),
    )(x, g, tgt)


def ffn_in(h, w2, name):
    S, D = h.shape
    F = w2.shape[2]
    tm, tn = _tile(S, 512), _tile(F, 512)

    def body(h_ref, w_ref, gu_ref, a_ref, at_ref):
        hv = h_ref[...]
        g = _dot(hv, w_ref[0])
        u = _dot(hv, w_ref[1])
        gu_ref[0] = g.astype(bf16)
        gu_ref[1] = u.astype(bf16)
        a = (g * _sig(g) * u).astype(bf16)
        a_ref[...] = a
        at_ref[...] = a.T

    return pl.pallas_call(
        body, name=name, grid=(S // tm, F // tn),
        in_specs=[pl.BlockSpec((tm, D), lambda i, j: (i, 0)), pl.BlockSpec((2, D, tn), lambda i, j: (0, 0, j))],
        out_specs=[pl.BlockSpec((2, tm, tn), lambda i, j: (0, i, j)), pl.BlockSpec((tm, tn), lambda i, j: (i, j)),
                   pl.BlockSpec((tn, tm), lambda i, j: (j, i))],
        out_shape=[jax.ShapeDtypeStruct((2, S, F), bf16), jax.ShapeDtypeStruct((S, F), bf16),
                   jax.ShapeDtypeStruct((F, S), bf16)],
        compiler_params=_params("parallel", "parallel"),
    )(h, w2)


def mm_res(a, w, x, scale, name, deps=()):
    S, K = a.shape
    N = w.shape[1]
    tm, tn = _tile(S, 512), _tile(N, 512)

    def body(a_ref, w_ref, x_ref, *rest):
        rest[-1][...] = x_ref[...] + scale * _dot(a_ref[...], w_ref[...])

    return pl.pallas_call(
        body, name=name, grid=(S // tm, N // tn),
        in_specs=[pl.BlockSpec((tm, K), lambda i, j: (i, 0)), pl.BlockSpec((K, tn), lambda i, j: (0, j)),
                  pl.BlockSpec((tm, tn), lambda i, j: (i, j))] + [ANY] * len(deps),
        out_specs=pl.BlockSpec((tm, tn), lambda i, j: (i, j)),
        out_shape=jax.ShapeDtypeStruct((S, N), f32),
        compiler_params=_params("parallel", "parallel"),
    )(a, w, x, *deps)


def mm_nn(a, b, name, out_dtype=f32, scale=1.0, deps=()):
    M, K = a.shape
    G, _, N = b.shape
    tm, tn = _tile(M, 512), _tile(N, 512)

    def body(a_ref, b_ref, *rest):
        acc = _dot(a_ref[...], b_ref[...])
        rest[-1][...] = (acc if scale == 1.0 else scale * acc).astype(out_dtype)

    return pl.pallas_call(
        body, name=name, grid=(G, M // tm, N // tn),
        in_specs=[pl.BlockSpec((tm, K), lambda g, i, j: (i, 0)),
                  pl.BlockSpec((None, K, tn), lambda g, i, j: (g, 0, j))] + [ANY] * len(deps),
        out_specs=pl.BlockSpec((None, tm, tn), lambda g, i, j: (g, i, j)),
        out_shape=jax.ShapeDtypeStruct((G, M, N), out_dtype),
        compiler_params=_params("parallel", "parallel", "parallel"),
    )(a, b, *deps)


def mm_nt(a, w, name, deps=()):
    G, S, K = a.shape
    N = w.shape[1]
    tm, tn = _tile(S, 512), _tile(N, 512)

    def body(a_ref, w_ref, *rest):
        o_ref = rest[-1]
        part = _dot(a_ref[...], w_ref[...], NT)
        if G == 1:
            o_ref[...] = part
        else:
            g = pl.program_id(2)

            @pl.when(g == 0)
            def _():
                o_ref[...] = part

            @pl.when(g > 0)
            def _():
                o_ref[...] += part

    return pl.pallas_call(
        body, name=name, grid=(S // tm, N // tn, G),
        in_specs=[pl.BlockSpec((None, tm, K), lambda i, j, g: (g, i, 0)),
                  pl.BlockSpec((None, tn, K), lambda i, j, g: (g, j, 0))] + [ANY] * len(deps),
        out_specs=pl.BlockSpec((tm, tn), lambda i, j, g: (i, j)),
        out_shape=jax.ShapeDtypeStruct((S, N), f32),
        compiler_params=_params("parallel", "parallel", "arbitrary"),
    )(a, w, *deps)


def ffn_dact(dx, wout, gu, name):
    S, D = dx.shape
    F = wout.shape[0]
    tm, tn = _tile(S, 512), _tile(F, 512)

    def body(dx_ref, w_ref, gu_ref, o_ref):
        da = FFN_RES * _dot(dx_ref[...], w_ref[...], NT)
        g = gu_ref[0].astype(f32)
        u = gu_ref[1].astype(f32)
        sg = _sig(g)
        o_ref[0] = (da * u * (sg * (1.0 + g * (1.0 - sg)))).astype(bf16)
        o_ref[1] = (da * (g * sg)).astype(bf16)

    return pl.pallas_call(
        body, name=name, grid=(S // tm, F // tn),
        in_specs=[pl.BlockSpec((tm, D), lambda i, j: (i, 0)), pl.BlockSpec((tn, D), lambda i, j: (j, 0)),
                  pl.BlockSpec((2, tm, tn), lambda i, j: (0, i, j))],
        out_specs=pl.BlockSpec((2, tm, tn), lambda i, j: (0, i, j)),
        out_shape=jax.ShapeDtypeStruct((2, S, F), bf16),
        compiler_params=_params("parallel", "parallel"),
    )(dx, wout, gu)


def _rope(t, c, s1, s2):
    w = t.shape[1]
    return t * c + pltpu.roll(t, 8, 1) * s1 + pltpu.roll(t, w - 8, 1) * s2


def _rope_t(d, c, s1, s2):
    w = d.shape[1]
    return d * c + pltpu.roll(d * s1, w - 8, 1) + pltpu.roll(d * s2, 8, 1)


def _attn_mask(n):
    qi = lax.broadcasted_iota(jnp.int32, (BLK, 2 * BLK), 0)
    kj = lax.broadcasted_iota(jnp.int32, (BLK, 2 * BLK), 1)
    dist = qi + BLK - kj
    return (dist >= 0) & (dist < BLK) & ((kj >= BLK) | (n > 0))


def _softmax_sink(s, valid, sk):
    s = jnp.where(valid, s, -1e30)
    m = jnp.maximum(jnp.max(s, axis=-1, keepdims=True), sk)
    e = jnp.exp(s - m)
    es = jnp.exp(sk - m)
    inv = 1.0 / (jnp.sum(e, axis=-1, keepdims=True) + es)
    return e * inv, es * inv


def attn_fwd(p, rope_c, rope_s1, rope_s2, sinks, name):
    S = p.shape[0]
    nb = S // BLK
    kvb = Q_END // KV_W

    def body(sink_ref, q_ref, kvc_ref, kvp_ref, cc_ref, s1c_ref, s2c_ref, cp_ref, s1p_ref, s2p_ref, o_ref):
        n = pl.program_id(0)
        cc, s1c, s2c = cc_ref[...], s1c_ref[...], s2c_ref[...]
        cp, s1p, s2p = cp_ref[...], s1p_ref[...], s2p_ref[...]
        q = _rope(q_ref[...], jnp.tile(cc, (1, 8)), jnp.tile(s1c, (1, 8)), jnp.tile(s2c, (1, 8)))
        kc = _rope(kvc_ref[:, :256], jnp.tile(cc, (1, 2)), jnp.tile(s1c, (1, 2)), jnp.tile(s2c, (1, 2)))
        kp = _rope(kvp_ref[:, :256], jnp.tile(cp, (1, 2)), jnp.tile(s1p, (1, 2)), jnp.tile(s2p, (1, 2)))
        k = jnp.concatenate([kp, kc], axis=0).astype(bf16)
        v = jnp.concatenate([kvp_ref[:, 256:], kvc_ref[:, 256:]], axis=0).astype(bf16)
        q = q.astype(bf16)
        valid = _attn_mask(n)
        for h in range(N_KV_HEADS):
            kh = k[:, h * HEAD_DIM:(h + 1) * HEAD_DIM]
            vh = v[:, h * HEAD_DIM:(h + 1) * HEAD_DIM]
            for g in range(GQ):
                hq = h * GQ + g
                qh = q[:, hq * HEAD_DIM:(hq + 1) * HEAD_DIM]
                s = _dot(qh, kh, NT) * (HEAD_DIM ** -0.5)
                pr, _ = _softmax_sink(s, valid, sink_ref[hq])
                o = _dot(pr.astype(bf16), vh)
                o_ref[:, hq * HEAD_DIM:(hq + 1) * HEAD_DIM] = o.astype(bf16)

    tab_c = pl.BlockSpec((BLK, LANE), lambda n: (n, 0))
    tab_p = pl.BlockSpec((BLK, LANE), lambda n: (jnp.maximum(n - 1, 0), 0))
    return pl.pallas_call(
        body, name=name, grid=(nb,),
        in_specs=[pl.BlockSpec(memory_space=pltpu.SMEM),
                  pl.BlockSpec((BLK, Q_END), lambda n: (n, 0)),
                  pl.BlockSpec((BLK, KV_W), lambda n: (n, kvb)),
                  pl.BlockSpec((BLK, KV_W), lambda n: (jnp.maximum(n - 1, 0), kvb)),
                  tab_c, tab_c, tab_c, tab_p, tab_p, tab_p],
        out_specs=pl.BlockSpec((BLK, Q_END), lambda n: (n, 0)),
        out_shape=jax.ShapeDtypeStruct((S, Q_END), bf16),
        compiler_params=_params("parallel"),
    )(sinks, p, p, p, rope_c, rope_s1, rope_s2, rope_c, rope_s1, rope_s2)


def attn_bwd(p, dcat, rope_c, rope_s1, rope_s2, sinks, name):
    S = p.shape[0]
    nb = S // BLK
    kvb = Q_END // KV_W

    def body(sink_ref, q_ref, kvc_ref, kvp_ref, do_ref, cc_ref, s1c_ref, s2c_ref, cp_ref, s1p_ref, s2p_ref,
             dq_ref, dkv_ref, dsink_ref, carry, dq_scr, dkv_scr):
        n = pl.program_id(0)

        @pl.when(n == 0)
        def _():
            carry[...] = jnp.zeros_like(carry)
            dsink_ref[...] = jnp.zeros_like(dsink_ref)

        cp, s1p, s2p = cp_ref[...], s1p_ref[...], s2p_ref[...]
        cp2, s1p2, s2p2 = jnp.tile(cp, (1, 2)), jnp.tile(s1p, (1, 2)), jnp.tile(s2p, (1, 2))

        @pl.when(n < nb)
        def _():
            cc, s1c, s2c = cc_ref[...], s1c_ref[...], s2c_ref[...]
            cc8, s1c8, s2c8 = jnp.tile(cc, (1, 8)), jnp.tile(s1c, (1, 8)), jnp.tile(s2c, (1, 8))
            q = _rope(q_ref[...], cc8, s1c8, s2c8).astype(bf16)
            kc = _rope(kvc_ref[:, :256], jnp.tile(cc, (1, 2)), jnp.tile(s1c, (1, 2)), jnp.tile(s2c, (1, 2)))
            kp = _rope(kvp_ref[:, :256], cp2, s1p2, s2p2)
            k = jnp.concatenate([kp, kc], axis=0).astype(bf16)
            v = jnp.concatenate([kvp_ref[:, 256:], kvc_ref[:, 256:]], axis=0).astype(bf16)
            do = do_ref[...].astype(bf16)
            valid = _attn_mask(n)
            lane = lax.broadcasted_iota(jnp.int32, (1, LANE), 1)
            dsink = jnp.zeros((1, LANE), f32)
            for h in range(N_KV_HEADS):
                kh = k[:, h * HEAD_DIM:(h + 1) * HEAD_DIM]
                vh = v[:, h * HEAD_DIM:(h + 1) * HEAD_DIM]
                dkh = jnp.zeros((2 * BLK, HEAD_DIM), f32)
                dvh = jnp.zeros((2 * BLK, HEAD_DIM), f32)
                for g in range(GQ):
                    hq = h * GQ + g
                    qh = q[:, hq * HEAD_DIM:(hq + 1) * HEAD_DIM]
                    doh = do[:, hq * HEAD_DIM:(hq + 1) * HEAD_DIM]
                    s = _dot(qh, kh, NT) * (HEAD_DIM ** -0.5)
                    pr, ps = _softmax_sink(s, valid, sink_ref[hq])
                    dpr = _dot(doh, vh, NT)
                    dvh = dvh + _dot(pr.astype(bf16), doh, TN)
                    row = jnp.sum(pr * dpr, axis=-1, keepdims=True)
                    ds = (pr * (dpr - row) * (HEAD_DIM ** -0.5)).astype(bf16)
                    dsink = dsink + jnp.where(lane == hq, -jnp.sum(ps * row, axis=0, keepdims=True), 0.0)
                    dq_scr[:, hq * HEAD_DIM:(hq + 1) * HEAD_DIM] = _dot(ds, kh)
                    dkh = dkh + _dot(ds, qh, TN)
                dkv_scr[:, h * HEAD_DIM:(h + 1) * HEAD_DIM] = dkh
                dkv_scr[:, 256 + h * HEAD_DIM:256 + (h + 1) * HEAD_DIM] = dvh
            dsink_ref[...] += dsink
            dq_ref[...] = _rope_t(dq_scr[...], cc8, s1c8, s2c8).astype(bf16)

        prev = carry[...]

        @pl.when(n < nb)
        def _():
            dkv_scr[pl.ds(0, BLK), :] = dkv_scr[pl.ds(0, BLK), :] + prev

        @pl.when(n == nb)
        def _():
            dkv_scr[pl.ds(0, BLK), :] = prev

        done = dkv_scr[pl.ds(0, BLK), :]
        dkv_ref[:, :256] = _rope_t(done[:, :256], cp2, s1p2, s2p2).astype(bf16)
        dkv_ref[:, 256:] = done[:, 256:].astype(bf16)

        @pl.when(n < nb)
        def _():
            carry[...] = dkv_scr[pl.ds(BLK, BLK), :]

    cur = lambda n: jnp.minimum(n, nb - 1)
    prv = lambda n: jnp.maximum(n - 1, 0)
    tab_c = pl.BlockSpec((BLK, LANE), lambda n: (cur(n), 0))
    tab_p = pl.BlockSpec((BLK, LANE), lambda n: (prv(n), 0))
    return pl.pallas_call(
        body, name=name, grid=(nb + 1,),
        in_specs=[pl.BlockSpec(memory_space=pltpu.SMEM),
                  pl.BlockSpec((BLK, Q_END), lambda n: (cur(n), 0)),
                  pl.BlockSpec((BLK, KV_W), lambda n: (cur(n), kvb)),
                  pl.BlockSpec((BLK, KV_W), lambda n: (prv(n), kvb)),
                  pl.BlockSpec((BLK, Q_END), lambda n: (cur(n), 0)),
                  tab_c, tab_c, tab_c, tab_p, tab_p, tab_p],
        out_specs=[pl.BlockSpec((BLK, Q_END), lambda n: (cur(n), 0)),
                   pl.BlockSpec((BLK, KV_W), lambda n: (prv(n), 0)),
                   pl.BlockSpec((1, LANE), lambda n: (0, 0))],
        out_shape=[jax.ShapeDtypeStruct((S, Q_END), bf16), jax.ShapeDtypeStruct((S, KV_W), bf16),
                   jax.ShapeDtypeStruct((1, LANE), f32)],
        scratch_shapes=[pltpu.VMEM((BLK, KV_W), f32), pltpu.VMEM((BLK, Q_END), f32), pltpu.VMEM((2 * BLK, KV_W), f32)],
        compiler_params=_params("arbitrary"),
    )(sinks, p, p, p, dcat, rope_c, rope_s1, rope_s2, rope_c, rope_s1, rope_s2)


A1_BLK = V_END // CONV_CH
A2_BLK = A1_BLK + 1


def _ln_stats(y):
    mu = jnp.mean(y, axis=-1, keepdims=True)
    xc = y - mu
    rstd = lax.rsqrt(jnp.mean(xc * xc, axis=-1, keepdims=True) + NORM_EPS)
    return xc * rstd, rstd


def conv_fwd(p, w, b, lng, lnb, name):
    S = p.shape[0]
    T = _tile(S, 256)
    r = T // HALO

    def body(a1_ref, a2_ref, h1_ref, h2_ref, w_ref, b_ref, g_ref, bb_ref, o_ref, y_ref, scr):
        i = pl.program_id(0)
        halo = h1_ref[...] * _sig(h2_ref[...])
        scr[pl.ds(0, HALO), :] = jnp.where(i > 0, halo, 0.0)
        scr[pl.ds(HALO, T), :] = a1_ref[...] * _sig(a2_ref[...])
        acc = jnp.zeros((T, CONV_CH), f32) + b_ref[...]
        for j in range(CONV_W):
            acc = acc + scr[pl.ds(HALO - (CONV_W - 1) + j, T), :] * w_ref[j:j + 1, :]
        y_ref[...] = acc
        yh, _ = _ln_stats(acc)
        z = yh * g_ref[...] + bb_ref[...]
        o_ref[...] = (z * _sig(z)).astype(bf16)

    vec = pl.BlockSpec((1, CONV_CH), lambda i: (0, 0))
    halo_map = lambda i: jnp.maximum(i * r - 1, 0)
    return pl.pallas_call(
        body, name=name, grid=(S // T,),
        in_specs=[pl.BlockSpec((T, CONV_CH), lambda i: (i, A1_BLK)), pl.BlockSpec((T, CONV_CH), lambda i: (i, A2_BLK)),
                  pl.BlockSpec((HALO, CONV_CH), lambda i: (halo_map(i), A1_BLK)),
                  pl.BlockSpec((HALO, CONV_CH), lambda i: (halo_map(i), A2_BLK)),
                  pl.BlockSpec((HALO, CONV_CH), lambda i: (0, 0)), vec, vec, vec],
        out_specs=[pl.BlockSpec((T, CONV_CH), lambda i: (i, 0)), pl.BlockSpec((T, CONV_CH), lambda i: (i, 0))],
        out_shape=[jax.ShapeDtypeStruct((S, CONV_CH), bf16), jax.ShapeDtypeStruct((S, CONV_CH), f32)],
        scratch_shapes=[pltpu.VMEM((T + HALO, CONV_CH), f32)],
        compiler_params=_params("parallel"),
    )(p, p, p, p, w, b, lng, lnb)


def conv_bwd(p, y, dcat, w, lng, lnb, name):
    S = p.shape[0]
    T = _tile(S, 256)
    n = S // T
    r = T // HALO
    dcb = Q_END // CONV_CH

    def body(a1_ref, a2_ref, h1_ref, h2_ref, y_ref, yn_ref, do_ref, don_ref, w_ref, g_ref, bb_ref,
             da_ref, dw_ref, db_ref, dg_ref, dbb_ref, scr_h, scr_dy, acc_b, acc_g, acc_bb):
        i = pl.program_id(0)

        @pl.when(i == 0)
        def _():
            dw_ref[...] = jnp.zeros_like(dw_ref)
            acc_b[...] = jnp.zeros_like(acc_b)
            acc_g[...] = jnp.zeros_like(acc_g)
            acc_bb[...] = jnp.zeros_like(acc_bb)

        gv, bv = g_ref[...], bb_ref[...]

        def ln_silu_bwd(yv, dout):
            yh, rstd = _ln_stats(yv)
            z = yh * gv + bv
            sg = _sig(z)
            dz = dout * (sg * (1.0 + z * (1.0 - sg)))
            gz = dz * gv
            dy = rstd * (gz - jnp.mean(gz, axis=-1, keepdims=True) - yh * jnp.mean(gz * yh, axis=-1, keepdims=True))
            return dy, dz, yh

        dy, dz, yh = ln_silu_bwd(y_ref[...], do_ref[...])
        dyn, _, _ = ln_silu_bwd(yn_ref[...], don_ref[...])
        acc_g[...] += _rowsum8(dz * yh)
        acc_bb[...] += _rowsum8(dz)
        acc_b[...] += _rowsum8(dy)
        scr_dy[pl.ds(0, T), :] = dy
        scr_dy[pl.ds(T, HALO), :] = jnp.where(i < n - 1, dyn, 0.0)
        a1, a2 = a1_ref[...], a2_ref[...]
        sg2 = _sig(a2)
        halo = h1_ref[...] * _sig(h2_ref[...])
        scr_h[pl.ds(0, HALO), :] = jnp.where(i > 0, halo, 0.0)
        scr_h[pl.ds(HALO, T), :] = a1 * sg2
        dh = jnp.zeros((T, CONV_CH), f32)
        for j in range(CONV_W):
            dh = dh + scr_dy[pl.ds(CONV_W - 1 - j, T), :] * w_ref[j:j + 1, :]
            dw_ref[j:j + 1, :] += jnp.sum(dy * scr_h[pl.ds(HALO - (CONV_W - 1) + j, T), :], axis=0, keepdims=True)
        da_ref[:, :CONV_CH] = (dh * sg2).astype(bf16)
        da_ref[:, CONV_CH:] = (dh * a1 * sg2 * (1.0 - sg2)).astype(bf16)

        @pl.when(i == n - 1)
        def _():
            db_ref[...] = jnp.sum(acc_b[...], axis=0, keepdims=True)
            dg_ref[...] = jnp.sum(acc_g[...], axis=0, keepdims=True)
            dbb_ref[...] = jnp.sum(acc_bb[...], axis=0, keepdims=True)

    vec = pl.BlockSpec((1, CONV_CH), lambda i: (0, 0))
    tap = pl.BlockSpec((HALO, CONV_CH), lambda i: (0, 0))
    prev_map = lambda i: jnp.maximum(i * r - 1, 0)
    next_map = lambda i: jnp.minimum((i + 1) * r, S // HALO - 1)
    return pl.pallas_call(
        body, name=name, grid=(n,),
        in_specs=[pl.BlockSpec((T, CONV_CH), lambda i: (i, A1_BLK)), pl.BlockSpec((T, CONV_CH), lambda i: (i, A2_BLK)),
                  pl.BlockSpec((HALO, CONV_CH), lambda i: (prev_map(i), A1_BLK)),
                  pl.BlockSpec((HALO, CONV_CH), lambda i: (prev_map(i), A2_BLK)),
                  pl.BlockSpec((T, CONV_CH), lambda i: (i, 0)),
                  pl.BlockSpec((HALO, CONV_CH), lambda i: (next_map(i), 0)),
                  pl.BlockSpec((T, CONV_CH), lambda i: (i, dcb)),
                  pl.BlockSpec((HALO, CONV_CH), lambda i: (next_map(i), dcb)),
                  tap, vec, vec],
        out_specs=[pl.BlockSpec((T, 2 * CONV_CH), lambda i: (i, 0)), tap, vec, vec, vec],
        out_shape=[jax.ShapeDtypeStruct((S, 2 * CONV_CH), bf16), jax.ShapeDtypeStruct((HALO, CONV_CH), f32),
                   jax.ShapeDtypeStruct((1, CONV_CH), f32), jax.ShapeDtypeStruct((1, CONV_CH), f32),
                   jax.ShapeDtypeStruct((1, CONV_CH), f32)],
        scratch_shapes=[pltpu.VMEM((T + HALO, CONV_CH), f32), pltpu.VMEM((T + HALO, CONV_CH), f32),
                        pltpu.VMEM((8, CONV_CH), f32), pltpu.VMEM((8, CONV_CH), f32), pltpu.VMEM((8, CONV_CH), f32)],
        compiler_params=_params("arbitrary"),
    )(p, p, p, p, y, y, dcat, dcat, w, lng, lnb)


U_BLK = (V_END + 2 * CONV_CH) // SGU_CH
SV_BLK = U_BLK + 1


def _tril(w, transposed=False):
    row = lax.broadcasted_iota(jnp.int32, (BLK, BLK), 0)
    col = lax.broadcasted_iota(jnp.int32, (BLK, BLK), 1)
    keep = (col >= row) if transposed else (row >= col)
    return jnp.where(keep, w, 0.0)


def sgu_fwd(p, lng, lnb, w, bias, name):
    S = p.shape[0]
    T = _tile(S, 256)

    def body(u_ref, v_ref, g_ref, bb_ref, w_ref, bias_ref, o_ref):
        yh, _ = _ln_stats(v_ref[...])
        v = (yh * g_ref[...] + bb_ref[...]).astype(bf16)
        low = lax.broadcasted_iota(jnp.int32, (BLK, LANE), 1) < HEAD_DIM
        for pr in range(SGU_HEADS // 2):
            lanes = pl.ds(pr * LANE, LANE)
            w0 = _tril(w_ref[2 * pr]).astype(bf16)
            w1 = _tril(w_ref[2 * pr + 1]).astype(bf16)
            for c in range(T // BLK):
                rows = pl.ds(c * BLK, BLK)
                vp = v[c * BLK:(c + 1) * BLK, pr * LANE:(pr + 1) * LANE]
                mixed = jnp.where(low, _dot(w0, vp), _dot(w1, vp)) + bias_ref[:, lanes]
                o_ref[rows, lanes] = (u_ref[rows, lanes] * mixed).astype(bf16)

    vec = pl.BlockSpec((1, SGU_CH), lambda i: (0, 0))
    return pl.pallas_call(
        body, name=name, grid=(S // T,),
        in_specs=[pl.BlockSpec((T, SGU_CH), lambda i: (i, U_BLK)), pl.BlockSpec((T, SGU_CH), lambda i: (i, SV_BLK)),
                  vec, vec, pl.BlockSpec((SGU_HEADS, BLK, BLK), lambda i: (0, 0, 0)),
                  pl.BlockSpec((BLK, SGU_CH), lambda i: (0, 0))],
        out_specs=pl.BlockSpec((T, SGU_CH), lambda i: (i, 0)),
        out_shape=jax.ShapeDtypeStruct((S, SGU_CH), bf16),
        compiler_params=_params("parallel"),
    )(p, p, lng, lnb, w, bias)


def sgu_bwd(p, dcat, lng, lnb, w, wt, bias, name):
    S = p.shape[0]
    T = _tile(S, 256)
    n = S // T
    dsb = (Q_END + CONV_CH) // SGU_CH

    def body(u_ref, v_ref, do_ref, g_ref, bb_ref, w_ref, wt_ref, bias_ref,
             da_ref, dw_ref, db_ref, dg_ref, dbb_ref, dv_scr, acc_bias, acc_g, acc_bb):
        i = pl.program_id(0)

        @pl.when(i == 0)
        def _():
            dw_ref[...] = jnp.zeros_like(dw_ref)
            acc_bias[...] = jnp.zeros_like(acc_bias)
            acc_g[...] = jnp.zeros_like(acc_g)
            acc_bb[...] = jnp.zeros_like(acc_bb)

        gv = g_ref[...]
        yh, rstd = _ln_stats(v_ref[...])
        v = (yh * gv + bb_ref[...]).astype(bf16)
        low = lax.broadcasted_iota(jnp.int32, (BLK, LANE), 1) < HEAD_DIM
        for pr in range(SGU_HEADS // 2):
            lanes = pl.ds(pr * LANE, LANE)
            w0 = _tril(w_ref[2 * pr]).astype(bf16)
            w1 = _tril(w_ref[2 * pr + 1]).astype(bf16)
            wt0 = _tril(wt_ref[2 * pr], True).astype(bf16)
            wt1 = _tril(wt_ref[2 * pr + 1], True).astype(bf16)
            dw0 = jnp.zeros((BLK, BLK), f32)
            dw1 = jnp.zeros((BLK, BLK), f32)
            for c in range(T // BLK):
                rows = pl.ds(c * BLK, BLK)
                vp = v[c * BLK:(c + 1) * BLK, pr * LANE:(pr + 1) * LANE]
                mixed = jnp.where(low, _dot(w0, vp), _dot(w1, vp)) + bias_ref[:, lanes]
                do = do_ref[rows, lanes]
                da_ref[rows, lanes] = (do * mixed).astype(bf16)
                dm = do * u_ref[rows, lanes]
                acc_bias[:, lanes] += dm
                dmb = dm.astype(bf16)
                dv_scr[rows, lanes] = jnp.where(low, _dot(wt0, dmb), _dot(wt1, dmb))
                zero = jnp.zeros_like(dmb)
                dw0 = dw0 + _dot(jnp.where(low, dmb, zero), vp, NT)
                dw1 = dw1 + _dot(jnp.where(low, zero, dmb), vp, NT)
            dw_ref[2 * pr] += _tril(dw0)
            dw_ref[2 * pr + 1] += _tril(dw1)
        dv = dv_scr[...]
        acc_g[...] += _rowsum8(dv * yh)
        acc_bb[...] += _rowsum8(dv)
        gz = dv * gv
        dvr = rstd * (gz - jnp.mean(gz, axis=-1, keepdims=True) - yh * jnp.mean(gz * yh, axis=-1, keepdims=True))
        da_ref[:, SGU_CH:] = dvr.astype(bf16)

        @pl.when(i == n - 1)
        def _():
            ch = lax.broadcasted_iota(jnp.int32, (SGU_CH, LANE), 0) // HEAD_DIM
            hd = lax.broadcasted_iota(jnp.int32, (SGU_CH, LANE), 1)
            fold = jnp.where(ch == hd, 1.0, 0.0).astype(f32)
            db_ref[...] = jnp.dot(acc_bias[...], fold, preferred_element_type=f32, precision=lax.Precision.HIGHEST)
            dg_ref[...] = jnp.sum(acc_g[...], axis=0, keepdims=True)
            dbb_ref[...] = jnp.sum(acc_bb[...], axis=0, keepdims=True)

    vec = pl.BlockSpec((1, SGU_CH), lambda i: (0, 0))
    wsp = pl.BlockSpec((SGU_HEADS, BLK, BLK), lambda i: (0, 0, 0))
    return pl.pallas_call(
        body, name=name, grid=(n,),
        in_specs=[pl.BlockSpec((T, SGU_CH), lambda i: (i, U_BLK)), pl.BlockSpec((T, SGU_CH), lambda i: (i, SV_BLK)),
                  pl.BlockSpec((T, SGU_CH), lambda i: (i, dsb)), vec, vec, wsp, wsp,
                  pl.BlockSpec((BLK, SGU_CH), lambda i: (0, 0))],
        out_specs=[pl.BlockSpec((T, 2 * SGU_CH), lambda i: (i, 0)), wsp,
                   pl.BlockSpec((BLK, LANE), lambda i: (0, 0)), vec, vec],
        out_shape=[jax.ShapeDtypeStruct((S, 2 * SGU_CH), bf16), jax.ShapeDtypeStruct((SGU_HEADS, BLK, BLK), f32),
                   jax.ShapeDtypeStruct((BLK, LANE), f32), jax.ShapeDtypeStruct((1, SGU_CH), f32),
                   jax.ShapeDtypeStruct((1, SGU_CH), f32)],
        scratch_shapes=[pltpu.VMEM((T, SGU_CH), f32), pltpu.VMEM((BLK, SGU_CH), f32),
                        pltpu.VMEM((8, SGU_CH), f32), pltpu.VMEM((8, SGU_CH), f32)],
        compiler_params=_params("arbitrary"),
    )(p, p, dcat, lng, lnb, w, wt, bias)


HBM = pl.BlockSpec(memory_space=pltpu.HBM)
SEM = pl.BlockSpec(memory_space=pltpu.SEMAPHORE)
ANY = pl.BlockSpec(memory_space=pl.ANY)
EFFECT = pltpu.SideEffectType.DATAFLOW_SIDE_EFFECTING


def _flip(x, y, c, k):
    px, py, pc = x ^ (k >> 2), y ^ ((k >> 1) & 1), c ^ (k & 1)
    return (px, py, pc), 4 * px + 2 * py + pc


def _routes_gather(x, y, c):
    me = 4 * x + 2 * y + c
    out = []
    for k in (1, 2, 4, 6):
        dev, idx = _flip(x, y, c, k)
        out.append((dev, None, me, idx))
    return out


def _routes_pair(x, y, c):
    dev, _ = _flip(x, y, c, 1)
    return [(dev, 2 * q + (1 - c), q, q) for q in range(N_DEV // 2)]


def _routes_chips(x, y, c):
    out = []
    for k in (2, 4, 6):
        dev, idx = _flip(x, y, c, k)
        out.append((dev, idx // 2, 2 * x + y, idx // 2))
    return out


def _routes_forward(x, y, c):
    sib, _ = _flip(x, y, c, 1)
    out = []
    for k in (2, 4, 6):
        _, idx = _flip(x, y, c, k)
        out.append((sib, idx, idx, idx ^ 1))
    return out


def _routes_all(x, y, c):
    me = 4 * x + 2 * y + c
    out = []
    for k in range(1, N_DEV):
        dev, idx = _flip(x, y, c, k)
        out.append((dev, None, me, idx))
    return out


def _slot(ref, slot, kind):
    if slot is None:
        return ref
    if kind == "cols":
        width = ref.shape[2] // (N_DEV // 2)
        return ref.at[slot // (N_DEV // 2), :, pl.ds(pl.multiple_of((slot % (N_DEV // 2)) * width, LANE), width)]
    return ref.at[slot]


def _copies(routes, srcs, lands, send_sems, recv_sems, incoming, src_kinds, land_kinds):
    x, y, c = lax.axis_index("x"), lax.axis_index("y"), lax.axis_index("c")
    out = []
    n = len(lands)
    if srcs is None:
        srcs, src_kinds = lands, land_kinds
    for k, (dev, src_slot, dst_slot, recv_slot) in enumerate(routes(x, y, c)):
        for a in range(n):
            out.append(pltpu.make_async_remote_copy(
                src_ref=_slot(srcs[a], src_slot, src_kinds[a]),
                dst_ref=_slot(lands[a], recv_slot if incoming else dst_slot, land_kinds[a]),
                send_sem=send_sems.at[k * n + a], recv_sem=recv_sems.at[k * n + a], device_id=dev, device_id_type=MESH))
    return out


def _pin(a):
    return pltpu.with_memory_space_constraint(a, pltpu.HBM)


def split_start(srcs, lands, routes, name, deps=(), src_kinds=None, land_kinds=None):
    n = len(lands)
    ns = 0 if srcs is None else n
    n_routes = len(routes(0, 0, 0))
    ops = ([] if srcs is None else list(srcs)) + list(lands)
    src_kinds = src_kinds or ["rows"] * n
    land_kinds = land_kinds or ["rows"] * n

    def body(*refs):
        src, land = (refs[:n] if ns else None), refs[ns:ns + n]
        first_out = ns + n + len(deps)
        send_sems, recv_sems, token = refs[first_out], refs[first_out + 1], refs[-1]
        for cp in _copies(routes, src, land, send_sems, recv_sems, False, src_kinds, land_kinds):
            cp.start()
        token[...] = jnp.zeros_like(token)

    thru = [pltpu.HBM(a.shape, a.dtype) for a in ops]
    res = pl.pallas_call(
        body, name=name,
        out_shape=(pltpu.SemaphoreType.DMA((n * n_routes,)), pltpu.SemaphoreType.DMA((n * n_routes,)), *thru,
                   jax.ShapeDtypeStruct((8, LANE), f32)),
        in_specs=[HBM] * len(ops) + [ANY] * len(deps),
        out_specs=(SEM, SEM, *([HBM] * len(ops)), pl.BlockSpec(memory_space=pltpu.VMEM)),
        input_output_aliases={i: 2 + i for i in range(len(ops))},
        compiler_params=pltpu.CompilerParams(has_side_effects=EFFECT),
    )(*[_pin(a) for a in ops], *deps)
    return (res[0], res[1], (list(res[2:2 + n]) if ns else None), list(res[2 + ns:2 + ns + n]), res[-1],
            (src_kinds, land_kinds))


def split_wait(started, after, routes, name):
    send_sems, recv_sems, srcs, lands, _, (src_kinds, land_kinds) = started
    n = len(lands)
    ns = 0 if srcs is None else n
    ops = ([] if srcs is None else list(srcs)) + list(lands)
    afters = list(after) if isinstance(after, (list, tuple)) else [after]

    def body(*refs):
        src, land = (refs[:n] if ns else None), refs[ns:ns + n]
        send_s, recv_s = refs[ns + n], refs[ns + n + 1]
        for cp in _copies(routes, src, land, send_s, recv_s, True, src_kinds, land_kinds):
            cp.wait_send()
            cp.wait_recv()

    thru = [pltpu.HBM(a.shape, a.dtype) for a in ops]
    res = pl.pallas_call(
        body, name=name, out_shape=tuple(thru),
        in_specs=[HBM] * len(ops) + [SEM, SEM] + [ANY] * len(afters), out_specs=tuple([HBM] * len(ops)),
        input_output_aliases={i: i for i in range(len(ops))},
        compiler_params=pltpu.CompilerParams(has_side_effects=EFFECT),
    )(*ops, send_sems, recv_sems, *afters)
    return (list(res[:n]) if ns else None), list(res[ns:ns + n])


def chip_sum(parts, land, c_idx, kind, name):
    _, R, C = land.shape
    tr = R if R * C * 2 <= 3 * 2 ** 20 else _tile(R, 512)
    half = N_DEV // 2

    def body(c_ref, p_ref, l_ref, o_ref):
        o_ref[...] = (p_ref[...].astype(f32) + l_ref[...].astype(f32)).astype(bf16)

    if kind == "cols":
        mine = lambda q, i, c_ref: ((2 * q + c_ref[0]) // half, i, (2 * q + c_ref[0]) % half)
    else:
        mine = lambda q, i, c_ref: (2 * q + c_ref[0], i, 0)
    return pl.pallas_call(
        body, name=name,
        grid_spec=pltpu.PrefetchScalarGridSpec(
            num_scalar_prefetch=1, grid=(half, R // tr),
            in_specs=[pl.BlockSpec((None, tr, C), mine), pl.BlockSpec((None, tr, C), lambda q, i, c_ref: (q, i, 0))],
            out_specs=pl.BlockSpec((None, tr, C), lambda q, i, c_ref: (q, i, 0))),
        out_shape=jax.ShapeDtypeStruct((half, R, C), bf16),
        compiler_params=_params("parallel", "parallel"),
    )(c_idx, parts, land)


def place_own(land, src, me_idx, kind, name):
    R, C = src.shape
    tr = _tile(R, 512)
    half = N_DEV // 2
    if kind == "cols":
        where = lambda i, m: (m[0] // half, i, m[0] % half)
    else:
        where = lambda i, m: (m[0], i, 0)

    def body(m_ref, land_ref, src_ref, out_ref):
        out_ref[...] = src_ref[...]

    return pl.pallas_call(
        body, name=name,
        grid_spec=pltpu.PrefetchScalarGridSpec(
            num_scalar_prefetch=1, grid=(R // tr,),
            in_specs=[ANY, pl.BlockSpec((tr, C), lambda i, m: (i, 0))],
            out_specs=pl.BlockSpec((None, tr, C), where)),
        out_shape=jax.ShapeDtypeStruct(land.shape, land.dtype),
        input_output_aliases={1: 0},
        compiler_params=_params("arbitrary"),
    )(me_idx, land, src)


def sum_slots(parts, name):
    P, R, C = parts.shape
    tr = _tile(R, 512)

    def body(p_ref, o_ref):
        total = p_ref[0]
        for j in range(1, P):
            total = total + p_ref[j]
        o_ref[...] = total

    return pl.pallas_call(
        body, name=name, grid=(R // tr,),
        in_specs=[pl.BlockSpec((P, tr, C), lambda i: (0, i, 0))],
        out_specs=pl.BlockSpec((tr, C), lambda i: (i, 0)),
        out_shape=jax.ShapeDtypeStruct((R, C), f32),
        compiler_params=_params("parallel"),
    )(parts)


def adamw(parts, owns, chip, w, m, v, name):
    L, R, C = w.shape
    P = parts[0].shape[0]
    tr = _tile(R, 128 if C > 1024 else 256)
    nr = R // tr
    c1 = 1.0 - ADAM_B1 ** ADAM_STEP
    c2 = 1.0 - ADAM_B2 ** ADAM_STEP
    n_own = L if owns is not None else 0

    def body(chip_ref, *refs):
        part_refs, own_refs = refs[:L], refs[L:L + n_own]
        w_ref, m_ref, v_ref, g_out, d_out, m_out, v_out = refs[L + n_own:]
        layer = pl.program_id(0)
        for l in range(L):
            @pl.when(layer == l)
            def _(l=l):
                g = None
                for q in range(P):
                    term = part_refs[l][q].astype(f32)
                    if n_own:
                        term = jnp.where(chip_ref[0] == q, own_refs[l][...].astype(f32), term)
                    g = term if g is None else g + term
                mn = ADAM_B1 * m_ref[...] + (1.0 - ADAM_B1) * g
                vn = ADAM_B2 * v_ref[...] + (1.0 - ADAM_B2) * (g * g)
                g_out[...] = g
                m_out[...] = mn
                v_out[...] = vn
                d_out[...] = -ADAM_LR * ((mn / c1) / (jnp.sqrt(vn / c2) + ADAM_EPS) + ADAM_WD * w_ref[...])

    def rows(l, a, i):
        return jnp.where(a == l, i, jnp.where(a < l, 0, nr - 1))

    def part_spec(l):
        return pl.BlockSpec((P, tr, C), lambda a, i, chip_ref: (0, rows(l, a, i), 0))

    def own_spec(l):
        return pl.BlockSpec((None, tr, C), lambda a, i, chip_ref: (chip_ref[0], rows(l, a, i), 0))

    slab = pl.BlockSpec((None, tr, C), lambda a, i, chip_ref: (a, i, 0))
    out = jax.ShapeDtypeStruct((L, R, C), f32)
    return pl.pallas_call(
        body, name=name,
        grid_spec=pltpu.PrefetchScalarGridSpec(
            num_scalar_prefetch=1, grid=(L, nr),
            in_specs=[part_spec(l) for l in range(L)] + [own_spec(l) for l in range(n_own)] + [slab, slab, slab],
            out_specs=[slab, slab, slab, slab]),
        out_shape=[out, out, out, out],
        compiler_params=_params("arbitrary", "arbitrary"),
    )(chip, *parts, *(owns or []), w, m, v)


PACK = 8 * LANE
PACK_ROWS = 256


def _pack(arrs):
    pieces = []
    for a in arrs:
        flat = a.astype(f32).reshape(-1)
        pad = (-flat.shape[0]) % PACK
        pieces.append(jnp.pad(flat, (0, pad)).reshape(-1, LANE))
    rows = sum(p.shape[0] for p in pieces)
    if rows > PACK_ROWS and rows % PACK_ROWS:
        pieces.append(jnp.zeros((PACK_ROWS - rows % PACK_ROWS, LANE), f32))
    return jnp.concatenate(pieces, axis=0)


def _unpack(buf, shapes):
    out, row = [], 0
    for shp in shapes:
        size = math.prod(shp)
        rows = (size + PACK - 1) // PACK * (PACK // LANE)
        out.append(buf[row:row + rows].reshape(-1)[:size].reshape(shp))
        row += rows
    return out


def _rope_tables(positions):
    half = ROT_DIM // 2
    inv_freq = 1.0 / (ROPE_THETA ** (jnp.arange(0, ROT_DIM, 2, dtype=f32) / ROT_DIM))
    ang = positions.astype(f32)[:, None] * inv_freq
    cos, sin = jnp.cos(ang), jnp.sin(ang)
    S = positions.shape[0]
    zeros, ones = jnp.zeros((S, half), f32), jnp.ones((S, HEAD_DIM - ROT_DIM), f32)
    rest = jnp.zeros((S, HEAD_DIM - ROT_DIM), f32)
    c = jnp.concatenate([cos, cos, ones], axis=1)
    s1 = jnp.concatenate([zeros, sin, rest], axis=1)
    s2 = jnp.concatenate([-sin, zeros, rest], axis=1)
    return tuple(jnp.tile(t, (1, LANE // HEAD_DIM)) for t in (c, s1, s2))


def _cols_to_shards(g):
    lead, (R, N) = g.shape[:-2], g.shape[-2:]
    g = g.reshape(lead + (R, N_DEV, N // N_DEV))
    return jnp.moveaxis(g, -2, 0)


def _shards_to_cols(g):
    g = jnp.moveaxis(g, 0, -2)
    return g.reshape(g.shape[:-2] + (g.shape[-2] * g.shape[-1],))


def kernel(x, positions, norm_ffn1, ffn1_w_in, ffn1_w_out, norm_mix, w_in, conv_dw_w, conv_dw_b, conv_ln_g, conv_ln_b, sgu_ln_g, sgu_ln_b, sgu_w, sgu_b, attn_sinks, w_out, norm_ffn2, ffn2_w_in, ffn2_w_out, final_norm, loss_target, m_norm_ffn1, m_ffn1_w_in, m_ffn1_w_out, m_norm_mix, m_w_in, m_conv_dw_w, m_conv_dw_b, m_conv_ln_g, m_conv_ln_b, m_sgu_ln_g, m_sgu_ln_b, m_sgu_w, m_sgu_b, m_attn_sinks, m_w_out, m_norm_ffn2, m_ffn2_w_in, m_ffn2_w_out, m_final_norm, v_norm_ffn1, v_ffn1_w_in, v_ffn1_w_out, v_norm_mix, v_w_in, v_conv_dw_w, v_conv_dw_b, v_conv_ln_g, v_conv_ln_b, v_sgu_ln_g, v_sgu_ln_b, v_sgu_w, v_sgu_b, v_attn_sinks, v_w_out, v_norm_ffn2, v_ffn2_w_in, v_ffn2_w_out, v_final_norm):
    L = norm_ffn1.shape[0]
    S, D = x.shape[1], x.shape[2]
    F = ffn1_w_out.shape[1] * N_DEV
    me = 4 * lax.axis_index("x") + 2 * lax.axis_index("y") + lax.axis_index("c")
    x0 = x[0]
    rope_c, rope_s1, rope_s2 = _rope_tables(positions[0])
    cw = CONV_CH // N_DEV

    c_idx = lax.axis_index("c").astype(jnp.int32).reshape(1)
    chip = (2 * lax.axis_index("x") + lax.axis_index("y")).astype(jnp.int32).reshape(1)
    no_chip = jnp.zeros((1,), jnp.int32)
    me_idx = me.astype(jnp.int32).reshape(1)

    row = lambda a: a.reshape(1, -1)
    order_fwd = [(l, g) for l in range(L) for g in (("ffn1_in", "ffn1_out") if l == 0 else ("ffn1",)) + ("mix", "ffn2")]

    def group_srcs(l, grp):
        if grp == "mix":
            taps = jnp.pad(conv_dw_w[l], ((0, HALO - CONV_W), (0, LANE - cw)))
            return [w_in[l].astype(bf16), w_out[l].astype(bf16), taps]
        both = ([ffn2_w_in[l], ffn2_w_out[l]] if grp == "ffn2" else [ffn1_w_in[l], ffn1_w_out[l]])
        both = [a.astype(bf16) for a in both]
        return both[:1] if grp == "ffn1_in" else both[1:] if grp == "ffn1_out" else both

    def kinds_of(grp):
        return {"mix": ["rows"] * 3, "ffn1_in": ["cols"], "ffn1_out": ["rows"]}.get(grp, ["cols", "rows"])

    def gather_start(k, deps=()):
        l, grp = order_fwd[k]
        srcs = group_srcs(l, grp)
        lands = [lax.empty((2, D, F) if kind == "cols" else (N_DEV,) + a.shape, a.dtype)
                 for a, kind in zip(srcs, kinds_of(grp))]
        return split_start(srcs, lands, _routes_gather, f"gather_start_{grp}_{l}", deps, land_kinds=kinds_of(grp))

    def gather_forward(k, started, after):
        l, grp = order_fwd[k]
        srcs, lands = split_wait(started, after, _routes_gather, f"gather_wait_{grp}_{l}")
        return srcs, split_start(None, lands, _routes_forward, f"forward_start_{grp}_{l}", land_kinds=kinds_of(grp))

    def gather_finish(k, srcs, started, after):
        l, grp = order_fwd[k]
        _, lands = split_wait(started, after, _routes_forward, f"forward_wait_{grp}_{l}")
        full = [place_own(ld, s, me_idx, kind, f"own_{grp}_{l}_{a}")
                for a, (ld, s, kind) in enumerate(zip(lands, srcs, kinds_of(grp)))]
        if grp == "mix":
            return dict(w_in=_shards_to_cols(full[0]), w_out=full[1].reshape(D, D),
                        taps=_shards_to_cols(full[2][:, :, :cw]))
        named = dict(zip(["w_in", "w_out"] if len(full) == 2 else ["w_in" if grp == "ffn1_in" else "w_out"], full))
        if "w_out" in named:
            named["w_out"] = named["w_out"].reshape(F, D)
        return named

    def ffn_head(xs, wts, g_norm, tag, sv):
        sv["x_in"] = xs
        h, sv["ht"] = rmsnorm_fwd(xs, g_norm, f"norm_{tag}")
        sv["gu"], a, sv["at"] = ffn_in(h, wts["w_in"], f"{tag}_in")
        return a

    def ffn_fwd(xs, wts, g_norm, tag, sv, mid):
        a = ffn_head(xs, wts, g_norm, tag, sv)
        return mm_res(a, wts["w_out"], xs, FFN_RES, f"{tag}_out", deps=mid(a))

    def mix_fwd(xs, wts, l, sv, mid):
        sv["x_in"] = xs
        h, sv["ht"] = rmsnorm_fwd(xs, row(norm_mix[l]), f"norm_mix_{l}")
        p = mm_nn(h, wts["w_in"][None], f"mix_in_{l}")[0]
        sv["p"] = p
        attn = attn_fwd(p, rope_c, rope_s1, rope_s2, attn_sinks[l], f"attn_fwd_{l}")
        conv, sv["conv_y"] = conv_fwd(p, wts["taps"], row(conv_dw_b[l]), row(conv_ln_g[l]), row(conv_ln_b[l]),
                                      f"conv_fwd_{l}")
        sv["sgu_bias"] = jnp.repeat(sgu_b[l].T, HEAD_DIM, axis=1)
        sgu = sgu_fwd(p, row(sgu_ln_g[l]), row(sgu_ln_b[l]), sgu_w[l], sv["sgu_bias"], f"sgu_fwd_{l}")
        cat = jnp.concatenate([attn, conv, sgu], axis=1)
        sv["catt"] = cat.T
        return mm_res(cat, wts["w_out"], xs, 1.0, f"mix_out_{l}", deps=mid(cat))

    weights, saved = {}, {}
    xs = x0
    starts = []
    for k in range(len(order_fwd)):
        starts.append(gather_start(k, (starts[-1][4],) if starts else ()))
    state = dict(zip(("srcs", "fwd"), gather_forward(0, starts[0], starts[-1][4])))
    for k, (l, grp) in enumerate(order_fwd):
        nxt = starts[k + 1] if k + 1 < len(order_fwd) else None
        wts = gather_finish(k, state["srcs"], state["fwd"], state["fwd"][4])

        def mid(after, k=k, nxt=nxt):
            if not nxt:
                return ()
            state["srcs"], state["fwd"] = gather_forward(k + 1, nxt, after)
            return (state["fwd"][4],)

        if grp == "ffn1_in":
            sv = saved[l, "ffn1"] = {}
            weights[l, "ffn1"] = wts
            head = ffn_head(xs, wts, row(norm_ffn1[l]), f"ffn1_{l}", sv)
            mid(head)
        elif grp == "ffn1_out":
            weights[l, "ffn1"].update(wts)
            xs = mm_res(head, wts["w_out"], xs, FFN_RES, f"ffn1_{l}_out")
            mid(xs)
        else:
            sv = saved[l, grp] = {}
            weights[l, grp] = wts
            if grp == "mix":
                xs = mix_fwd(xs, wts, l, sv, mid)
            else:
                xs = ffn_fwd(xs, wts, row(norm_ffn1[l] if grp == "ffn1" else norm_ffn2[l]), f"{grp}_{l}", sv, mid)

    dx, dxb, d_final_norm, loss = final_loss(xs, row(final_norm), loss_target[0], "final_loss")

    def scatter_start(grads, kinds, tag, deps=()):
        half = N_DEV // 2
        lands = [lax.empty((half, g.shape[1], g.shape[2] // half) if kind == "cols" else (half,) + g.shape[1:], g.dtype)
                 for g, kind in zip(grads, kinds)]
        return split_start(grads, lands, _routes_pair, f"pair_start_{tag}", deps, src_kinds=kinds)

    def pair_to_chips(started, after, tag):
        kinds = started[5][0]
        grads, landed = split_wait(started, after, _routes_pair, f"pair_wait_{tag}")
        sums = [chip_sum(g, ld, c_idx, kind, f"chip_sum_{tag}_{a}")
                for a, (g, ld, kind) in enumerate(zip(grads, landed, kinds))]
        return split_start(sums, [lax.empty(s.shape, s.dtype) for s in sums], _routes_chips, f"chips_start_{tag}")

    def ffn_bwd(dx, dxb, wts, sv, g_norm, tag, deps=(), out_first=False):
        dgu = ffn_dact(dxb, wts["w_out"], sv["gu"], f"{tag}_dact")
        d_w_out = mm_nn(sv["at"], dxb[None], f"{tag}_dwout", bf16, FFN_RES, deps=deps)[0]
        d_w_out = d_w_out.reshape(N_DEV, F // N_DEV, D)
        if not out_first:
            d_w_in = mm_nn(sv["ht"], dgu, f"{tag}_dwin", bf16)
            pair = scatter_start([d_w_in, d_w_out], ["cols", "rows"], tag)
            dh = mm_nt(dgu, wts["w_in"], f"{tag}_dh", deps=(pair[4],))
            chips = [pair_to_chips(pair, dh, tag)]
        else:
            pair_out = scatter_start([d_w_out], ["rows"], f"{tag}_out")
            d_w_in = mm_nn(sv["ht"], dgu, f"{tag}_dwin", bf16, deps=(pair_out[4],))
            chips_out = pair_to_chips(pair_out, d_w_in, f"{tag}_out")
            pair_in = scatter_start([d_w_in], ["cols"], f"{tag}_in", deps=(chips_out[4],))
            dh = mm_nt(dgu, wts["w_in"], f"{tag}_dh", deps=(pair_in[4],))
            chips = [pair_to_chips(pair_in, dh, f"{tag}_in"), chips_out]
        dx, dxb, dg = rmsnorm_bwd(dh, sv["x_in"], g_norm, dx, f"{tag}_dnorm", deps=(chips[0][4],))
        return dx, dxb, dg, chips

    small = [None] * L
    chips_pending = {}
    for l in reversed(range(L)):
        dx, dxb, d_norm_ffn2, chips_pending[l, "ffn2"] = ffn_bwd(
            dx, dxb, weights[l, "ffn2"], saved[l, "ffn2"], row(norm_ffn2[l]), f"ffn2_{l}")

        wts, sv = weights[l, "mix"], saved[l, "mix"]
        d_w_out = mm_nn(sv["catt"], dxb[None], f"mix_dwout_{l}", bf16)[0]
        dcat = mm_nt(dxb[None], wts["w_out"][None], f"mix_dcat_{l}", deps=(d_w_out,))
        p = sv["p"]
        dq, dkv, d_sinks = attn_bwd(p, dcat, rope_c, rope_s1, rope_s2, attn_sinks[l], f"attn_bwd_{l}")
        da_conv, d_taps, d_conv_b, d_conv_g, d_conv_bb = conv_bwd(
            p, sv["conv_y"], dcat, wts["taps"], row(conv_ln_g[l]), row(conv_ln_b[l]), f"conv_bwd_{l}")
        da_sgu, d_sgu_w, d_sgu_bias, d_sgu_g, d_sgu_bb = sgu_bwd(
            p, dcat, row(sgu_ln_g[l]), row(sgu_ln_b[l]), sgu_w[l], jnp.swapaxes(sgu_w[l], 1, 2), sv["sgu_bias"],
            f"sgu_bwd_{l}")
        dp = jnp.concatenate([dq, dkv, da_conv, da_sgu], axis=1)
        d_w_in = mm_nn(sv["ht"], dp[None], f"mix_dwin_{l}", bf16)[0]
        pair = scatter_start([_cols_to_shards(d_w_in), d_w_out.reshape(N_DEV, D // N_DEV, D)], ["rows", "rows"],
                             f"mix_{l}")
        dh = mm_nt(dp[None], wts["w_in"][None], f"mix_dh_{l}", deps=(pair[4],))
        chips_pending[l, "mix"] = [pair_to_chips(pair, dh, f"mix_{l}")]
        dx, dxb, d_norm_mix = rmsnorm_bwd(dh, sv["x_in"], row(norm_mix[l]), dx, f"mix_dnorm_{l}",
                                          deps=(chips_pending[l, "mix"][0][4],))

        small[l] = dict(norm_mix=d_norm_mix[0], conv_dw_w=d_taps[:CONV_W],
                        conv_dw_b=d_conv_b[0], conv_ln_g=d_conv_g[0], conv_ln_b=d_conv_bb[0], sgu_ln_g=d_sgu_g[0],
                        sgu_ln_b=d_sgu_bb[0], sgu_w=d_sgu_w, sgu_b=d_sgu_bias[:, :SGU_HEADS].T,
                        attn_sinks=d_sinks[0, :N_Q_HEADS], norm_ffn2=d_norm_ffn2[0])
        if l == 0:
            early_names = ["norm_mix", "conv_dw_w", "conv_dw_b", "conv_ln_g", "conv_ln_b", "sgu_ln_g", "sgu_ln_b",
                           "sgu_w", "sgu_b", "attn_sinks", "norm_ffn2"]
            early = [jnp.stack([small[k][n] for k in range(L)]) for n in early_names]
            early += [d_final_norm[0], loss.reshape(1)]
            early_shapes = [a.shape for a in early]
            early = _pack(early)
            early_pending = split_start([early], [lax.empty((N_DEV,) + early.shape, f32)], _routes_all,
                                        "small_start", deps=(dxb,))
            early_token = (early_pending[4],)
        else:
            early_token = ()

        dx, dxb, d_norm_ffn1, chips_pending[l, "ffn1"] = ffn_bwd(
            dx, dxb, weights[l, "ffn1"], saved[l, "ffn1"], row(norm_ffn1[l]), f"ffn1_{l}", early_token, l == 0)
        small[l]["norm_ffn1"] = d_norm_ffn1[0]

    grad_x = dx[None]
    late = _pack([jnp.stack([small[l]["norm_ffn1"] for l in range(L)])])
    late_pending = split_start([late], [lax.empty((N_DEV,) + late.shape, f32)], _routes_all, "late_start", deps=(dx,))

    def landed(grp, after):
        sums, lands = [], []
        for l in range(L):
            got = [split_wait(st, after, _routes_chips, f"chips_wait_{grp}_{l}_{a}")
                   for a, st in enumerate(chips_pending[l, grp])]
            sums.append([s for g in got for s in g[0]])
            lands.append([s for g in got for s in g[1]])
        return sums, lands

    given = dict(norm_ffn1=(norm_ffn1, m_norm_ffn1, v_norm_ffn1), norm_mix=(norm_mix, m_norm_mix, v_norm_mix),
                 conv_dw_w=(conv_dw_w, m_conv_dw_w, v_conv_dw_w), conv_dw_b=(conv_dw_b, m_conv_dw_b, v_conv_dw_b),
                 conv_ln_g=(conv_ln_g, m_conv_ln_g, v_conv_ln_g), conv_ln_b=(conv_ln_b, m_conv_ln_b, v_conv_ln_b),
                 sgu_ln_g=(sgu_ln_g, m_sgu_ln_g, v_sgu_ln_g), sgu_ln_b=(sgu_ln_b, m_sgu_ln_b, v_sgu_ln_b),
                 sgu_w=(sgu_w, m_sgu_w, v_sgu_w), sgu_b=(sgu_b, m_sgu_b, v_sgu_b),
                 attn_sinks=(attn_sinks, m_attn_sinks, v_attn_sinks), norm_ffn2=(norm_ffn2, m_norm_ffn2, v_norm_ffn2),
                 final_norm=(final_norm, m_final_norm, v_final_norm))

    def small_update(pending, after, names, shapes, tag):
        (own,), (others,) = split_wait(pending, after, _routes_all, f"{tag}_wait")
        total = _unpack(sum_slots(place_own(others, own, me_idx, "rows", f"{tag}_own"), f"{tag}_sum"), shapes)
        g = dict(zip(names, total))
        if "conv_dw_w" in g:
            g["conv_dw_w"] = lax.dynamic_slice_in_dim(g["conv_dw_w"], me * cw, cw, axis=2)
        upd_names = [n for n in names if n in given]
        upd_shapes = [given[n][0].shape for n in upd_names]
        packed = [_pack([g[n] for n in upd_names])[None]] + [_pack([given[n][k] for n in upd_names])[None]
                                                              for k in range(3)]
        res = adamw([packed[0]], None, no_chip, packed[1], packed[2], packed[3], f"{tag}_adamw")
        return g, [dict(zip(upd_names, _unpack(r[0], upd_shapes))) for r in res], res[0]

    big = {}
    done = [dx]

    def big_update(grp, names):
        sums, lands = landed(grp, done)
        for idx, (name, w, m, v) in enumerate(names):
            big[name] = adamw([lands[l][idx] for l in range(L)], [sums[l][idx] for l in range(L)], chip, w, m, v,
                              f"adamw_{name}")
            done.append(big[name][0])

    big_update("ffn2", (("ffn2_w_in", ffn2_w_in, m_ffn2_w_in, v_ffn2_w_in),
                        ("ffn2_w_out", ffn2_w_out, m_ffn2_w_out, v_ffn2_w_out)))
    big_update("mix", (("w_in", w_in, m_w_in, v_w_in), ("w_out", w_out, m_w_out, v_w_out)))
    g_early, upd_early, marker = small_update(early_pending, done, early_names + ["final_norm", "loss"], early_shapes,
                                              "small")
    done.append(marker)
    big_update("ffn1", (("ffn1_w_in", ffn1_w_in, m_ffn1_w_in, v_ffn1_w_in),
                        ("ffn1_w_out", ffn1_w_out, m_ffn1_w_out, v_ffn1_w_out)))
    _, upd_late, _ = small_update(late_pending, done, ["norm_ffn1"], [(L, D)], "late")
    upd = [{**upd_early[k], **upd_late[k]} for k in range(4)]

    order = ["norm_ffn1", "ffn1_w_in", "ffn1_w_out", "norm_mix", "w_in", "conv_dw_w", "conv_dw_b", "conv_ln_g",
             "conv_ln_b", "sgu_ln_g", "sgu_ln_b", "sgu_w", "sgu_b", "attn_sinks", "w_out", "norm_ffn2", "ffn2_w_in",
             "ffn2_w_out", "final_norm"]
    outs = [g_early["loss"].reshape(()), grad_x]
    for k in range(4):
        outs += [big[n][k] if n in big else upd[k][n] for n in order]
    return tuple(outs)
```

```python
import functools
import math

import jax
import jax.numpy as jnp
from jax import lax
from jax.experimental import pallas as pl
from jax.experimental.pallas import tpu as pltpu

f32 = jnp.float32
bf16 = jnp.bfloat16

N_DEV = 8
HEAD_DIM = 64
N_Q_HEADS = 16
N_KV_HEADS = 4
GQ = N_Q_HEADS // N_KV_HEADS
BLK = 128
ROT_DIM = 16
ROPE_THETA = 500000.0
CONV_W = 31
CONV_CH = 512
SGU_CH = 512
SGU_HEADS = 8
Q_END = N_Q_HEADS * HEAD_DIM
KV_W = 2 * N_KV_HEADS * HEAD_DIM
V_END = Q_END + KV_W
IN_COLS = V_END + 2 * CONV_CH + 2 * SGU_CH
HALO = 32
NORM_EPS = 1e-5
FFN_RES = 0.5
ADAM_LR, ADAM_B1, ADAM_B2, ADAM_EPS, ADAM_WD, ADAM_STEP = 0.001, 0.9, 0.999, 1e-08, 0.01, 10
LANE = 128
VMEM_LIMIT = 56 * 2 ** 20
MM_ROWS = 1024
MESH = pl.DeviceIdType.MESH

NN = (((1,), (0,)), ((), ()))
NT = (((1,), (1,)), ((), ()))
TN = (((0,), (0,)), ((), ()))


def _tile(dim, pref):
    t = min(pref, dim)
    while dim % t:
        t //= 2
    return t


def _params(*sem):
    return pltpu.CompilerParams(dimension_semantics=sem, vmem_limit_bytes=VMEM_LIMIT)


def _sig(x):
    return 1.0 / (1.0 + jnp.exp(-x))


def _dot(a, b, dims=NN):
    return lax.dot_general(a, b, dims, preferred_element_type=f32)


def _rowsum8(x):
    return x.reshape(x.shape[0] // 8, 8, x.shape[1]).sum(axis=0)


def rmsnorm_fwd(x, g, name, deps=()):
    S, D = x.shape
    tm = _tile(S, 512)

    def body(x_ref, g_ref, *rest):
        o_ref, ot_ref = rest[-2:]
        xv = x_ref[...]
        r = lax.rsqrt(jnp.mean(xv * xv, axis=-1, keepdims=True) + NORM_EPS)
        hb = (xv * r * g_ref[...]).astype(bf16)
        o_ref[...] = hb
        ot_ref[...] = hb.T

    return pl.pallas_call(
        body, name=name, grid=(S // tm,),
        in_specs=[pl.BlockSpec((tm, D), lambda i: (i, 0)), pl.BlockSpec((1, D), lambda i: (0, 0))] + [ANY] * len(deps),
        out_specs=[pl.BlockSpec((tm, D), lambda i: (i, 0)), pl.BlockSpec((D, tm), lambda i: (0, i))],
        out_shape=[jax.ShapeDtypeStruct((S, D), bf16), jax.ShapeDtypeStruct((D, S), bf16)],
        compiler_params=_params("parallel"),
    )(x, g, *deps)


def rmsnorm_bwd(dh, x, g, dres, name, deps=()):
    S, D = x.shape
    tm = _tile(S, 256)
    n = S // tm

    def body(dh_ref, x_ref, g_ref, dres_ref, *rest):
        dx_ref, dxb_ref, dg_ref, acc = rest[-4:]
        i = pl.program_id(0)

        @pl.when(i == 0)
        def _():
            acc[...] = jnp.zeros_like(acc)

        xv = x_ref[...]
        r = lax.rsqrt(jnp.mean(xv * xv, axis=-1, keepdims=True) + NORM_EPS)
        xh = xv * r
        dy = dh_ref[...]
        gy = dy * g_ref[...]
        dx = dres_ref[...] + r * (gy - xh * jnp.mean(gy * xh, axis=-1, keepdims=True))
        dx_ref[...] = dx
        dxb_ref[...] = dx.astype(bf16)
        acc[...] += _rowsum8(dy * xh)

        @pl.when(i == n - 1)
        def _():
            dg_ref[...] = jnp.sum(acc[...], axis=0, keepdims=True)

    row = pl.BlockSpec((tm, D), lambda i: (i, 0))
    vec = pl.BlockSpec((1, D), lambda i: (0, 0))
    return pl.pallas_call(
        body, name=name, grid=(n,),
        in_specs=[row, row, vec, row] + [ANY] * len(deps),
        out_specs=[row, row, vec],
        out_shape=[jax.ShapeDtypeStruct((S, D), f32), jax.ShapeDtypeStruct((S, D), bf16),
                   jax.ShapeDtypeStruct((1, D), f32)],
        scratch_shapes=[pltpu.VMEM((8, D), f32)],
        compiler_params=_params("arbitrary"),
    )(dh, x, g, dres, *deps)


def final_loss(x, g, tgt, name):
    S, D = x.shape
    tm = _tile(S, 256)
    n = S // tm

    def body(x_ref, g_ref, t_ref, dx_ref, dxb_ref, dg_ref, loss_ref, acc):
        i = pl.program_id(0)

        @pl.when(i == 0)
        def _():
            acc[...] = jnp.zeros_like(acc)
            loss_ref[...] = jnp.zeros_like(loss_ref)

        xv = x_ref[...]
        gv = g_ref[...]
        r = lax.rsqrt(jnp.mean(xv * xv, axis=-1, keepdims=True) + NORM_EPS)
        xh = xv * r
        diff = xh * gv - t_ref[...]
        tok = jnp.mean(diff * diff, axis=-1, keepdims=True)
        loss_ref[...] += 0.5 * jnp.sum(tok, axis=0, keepdims=True)
        dy = diff / D
        gy = dy * gv
        dx = r * (gy - xh * jnp.mean(gy * xh, axis=-1, keepdims=True))
        dx_ref[...] = dx
        dxb_ref[...] = dx.astype(bf16)
        acc[...] += _rowsum8(dy * xh)

        @pl.when(i == n - 1)
        def _():
            dg_ref[...] = jnp.sum(acc[...], axis=0, keepdims=True)

    row = pl.BlockSpec((tm, D), lambda i: (i, 0))
    vec = pl.BlockSpec((1, D), lambda i: (0, 0))
    return pl.pallas_call(
        body, name=name, grid=(n,),
        in_specs=[row, vec, row],
        out_specs=[row, row, vec, pl.BlockSpec((1, 1), lambda i: (0, 0))],
        out_shape=[jax.ShapeDtypeStruct((S, D), f32), jax.ShapeDtypeStruct((S, D), bf16),
                   jax.ShapeDtypeStruct((1, D), f32), jax.ShapeDtypeStruct((1, 1), f32)],
        scratch_shapes=[pltpu.VMEM((8, D), f32)],
        compiler_params=_params("arbitrary"),
    )(x, g, tgt)


def ffn_in(h, w2, name):
    S, D = h.shape
    F = w2.shape[2]
    tm, tn = _tile(S, MM_ROWS), _tile(F, 512)

    def body(h_ref, w_ref, gu_ref, a_ref, at_ref):
        hv = h_ref[...]
        g = _dot(hv, w_ref[0])
        u = _dot(hv, w_ref[1])
        gu_ref[0] = g.astype(bf16)
        gu_ref[1] = u.astype(bf16)
        a = (g * _sig(g) * u).astype(bf16)
        a_ref[...] = a
        at_ref[...] = a.T

    return pl.pallas_call(
        body, name=name, grid=(S // tm, F // tn),
        in_specs=[pl.BlockSpec((tm, D), lambda i, j: (i, 0)), pl.BlockSpec((2, D, tn), lambda i, j: (0, 0, j))],
        out_specs=[pl.BlockSpec((2, tm, tn), lambda i, j: (0, i, j)), pl.BlockSpec((tm, tn), lambda i, j: (i, j)),
                   pl.BlockSpec((tn, tm), lambda i, j: (j, i))],
        out_shape=[jax.ShapeDtypeStruct((2, S, F), bf16), jax.ShapeDtypeStruct((S, F), bf16),
                   jax.ShapeDtypeStruct((F, S), bf16)],
        compiler_params=_params("parallel", "parallel"),
    )(h, w2)


def mm_res(a, w, x, scale, name, deps=()):
    S, K = a.shape
    N = w.shape[1]
    tm, tn = _tile(S, MM_ROWS), _tile(N, 512)

    def body(a_ref, w_ref, x_ref, *rest):
        rest[-1][...] = x_ref[...] + scale * _dot(a_ref[...], w_ref[...])

    return pl.pallas_call(
        body, name=name, grid=(S // tm, N // tn),
        in_specs=[pl.BlockSpec((tm, K), lambda i, j: (i, 0)), pl.BlockSpec((K, tn), lambda i, j: (0, j)),
                  pl.BlockSpec((tm, tn), lambda i, j: (i, j))] + [ANY] * len(deps),
        out_specs=pl.BlockSpec((tm, tn), lambda i, j: (i, j)),
        out_shape=jax.ShapeDtypeStruct((S, N), f32),
        compiler_params=_params("parallel", "parallel"),
    )(a, w, x, *deps)


def mm_nn(a, b, name, out_dtype=f32, scale=1.0, deps=()):
    M, K = a.shape
    G, _, N = b.shape
    tm, tn = _tile(M, MM_ROWS), _tile(N, 512)

    def body(a_ref, b_ref, *rest):
        acc = _dot(a_ref[...], b_ref[...])
        rest[-1][...] = (acc if scale == 1.0 else scale * acc).astype(out_dtype)

    return pl.pallas_call(
        body, name=name, grid=(G, M // tm, N // tn),
        in_specs=[pl.BlockSpec((tm, K), lambda g, i, j: (i, 0)),
                  pl.BlockSpec((None, K, tn), lambda g, i, j: (g, 0, j))] + [ANY] * len(deps),
        out_specs=pl.BlockSpec((None, tm, tn), lambda g, i, j: (g, i, j)),
        out_shape=jax.ShapeDtypeStruct((G, M, N), out_dtype),
        compiler_params=_params("parallel", "parallel", "parallel"),
    )(a, b, *deps)


def mm_nt(a, w, name, deps=()):
    G, S, K = a.shape
    N = w.shape[1]
    tm, tn = _tile(S, MM_ROWS), _tile(N, 512)

    def body(a_ref, w_ref, *rest):
        o_ref = rest[-1]
        part = _dot(a_ref[...], w_ref[...], NT)
        if G == 1:
            o_ref[...] = part
        else:
            g = pl.program_id(2)

            @pl.when(g == 0)
            def _():
                o_ref[...] = part

            @pl.when(g > 0)
            def _():
                o_ref[...] += part

    return pl.pallas_call(
        body, name=name, grid=(S // tm, N // tn, G),
        in_specs=[pl.BlockSpec((None, tm, K), lambda i, j, g: (g, i, 0)),
                  pl.BlockSpec((None, tn, K), lambda i, j, g: (g, j, 0))] + [ANY] * len(deps),
        out_specs=pl.BlockSpec((tm, tn), lambda i, j, g: (i, j)),
        out_shape=jax.ShapeDtypeStruct((S, N), f32),
        compiler_params=_params("parallel", "parallel", "arbitrary"),
    )(a, w, *deps)


def ffn_dact(dx, wout, gu, name):
    S, D = dx.shape
    F = wout.shape[0]
    tm, tn = _tile(S, 512), _tile(F, 512)

    def body(dx_ref, w_ref, gu_ref, o_ref):
        da = FFN_RES * _dot(dx_ref[...], w_ref[...], NT)
        g = gu_ref[0].astype(f32)
        u = gu_ref[1].astype(f32)
        sg = _sig(g)
        o_ref[0] = (da * u * (sg * (1.0 + g * (1.0 - sg)))).astype(bf16)
        o_ref[1] = (da * (g * sg)).astype(bf16)

    return pl.pallas_call(
        body, name=name, grid=(S // tm, F // tn),
        in_specs=[pl.BlockSpec((tm, D), lambda i, j: (i, 0)), pl.BlockSpec((tn, D), lambda i, j: (j, 0)),
                  pl.BlockSpec((2, tm, tn), lambda i, j: (0, i, j))],
        out_specs=pl.BlockSpec((2, tm, tn), lambda i, j: (0, i, j)),
        out_shape=jax.ShapeDtypeStruct((2, S, F), bf16),
        compiler_params=_params("parallel", "parallel"),
    )(dx, wout, gu)


def _rope(t, c, s1, s2):
    w = t.shape[1]
    return t * c + pltpu.roll(t, 8, 1) * s1 + pltpu.roll(t, w - 8, 1) * s2


def _rope_t(d, c, s1, s2):
    w = d.shape[1]
    return d * c + pltpu.roll(d * s1, w - 8, 1) + pltpu.roll(d * s2, 8, 1)


def _attn_mask(n):
    qi = lax.broadcasted_iota(jnp.int32, (BLK, 2 * BLK), 0)
    kj = lax.broadcasted_iota(jnp.int32, (BLK, 2 * BLK), 1)
    dist = qi + BLK - kj
    return (dist >= 0) & (dist < BLK) & ((kj >= BLK) | (n > 0))


def _softmax_sink(s, valid, sk):
    s = jnp.where(valid, s, -1e30)
    m = jnp.maximum(jnp.max(s, axis=-1, keepdims=True), sk)
    e = jnp.exp(s - m)
    es = jnp.exp(sk - m)
    inv = 1.0 / (jnp.sum(e, axis=-1, keepdims=True) + es)
    return e * inv, es * inv


def attn_fwd(p, rope_c, rope_s1, rope_s2, sinks, name):
    S = p.shape[0]
    nb = S // BLK
    kvb = Q_END // KV_W

    def body(sink_ref, q_ref, kvc_ref, kvp_ref, cc_ref, s1c_ref, s2c_ref, cp_ref, s1p_ref, s2p_ref, o_ref):
        n = pl.program_id(0)
        cc, s1c, s2c = cc_ref[...], s1c_ref[...], s2c_ref[...]
        cp, s1p, s2p = cp_ref[...], s1p_ref[...], s2p_ref[...]
        q = _rope(q_ref[...], jnp.tile(cc, (1, 8)), jnp.tile(s1c, (1, 8)), jnp.tile(s2c, (1, 8)))
        kc = _rope(kvc_ref[:, :256], jnp.tile(cc, (1, 2)), jnp.tile(s1c, (1, 2)), jnp.tile(s2c, (1, 2)))
        kp = _rope(kvp_ref[:, :256], jnp.tile(cp, (1, 2)), jnp.tile(s1p, (1, 2)), jnp.tile(s2p, (1, 2)))
        k = jnp.concatenate([kp, kc], axis=0).astype(bf16)
        v = jnp.concatenate([kvp_ref[:, 256:], kvc_ref[:, 256:]], axis=0).astype(bf16)
        q = q.astype(bf16)
        valid = _attn_mask(n)
        for h in range(N_KV_HEADS):
            kh = k[:, h * HEAD_DIM:(h + 1) * HEAD_DIM]
            vh = v[:, h * HEAD_DIM:(h + 1) * HEAD_DIM]
            for g in range(GQ):
                hq = h * GQ + g
                qh = q[:, hq * HEAD_DIM:(hq + 1) * HEAD_DIM]
                s = _dot(qh, kh, NT) * (HEAD_DIM ** -0.5)
                pr, _ = _softmax_sink(s, valid, sink_ref[hq])
                o = _dot(pr.astype(bf16), vh)
                o_ref[:, hq * HEAD_DIM:(hq + 1) * HEAD_DIM] = o.astype(bf16)

    tab_c = pl.BlockSpec((BLK, LANE), lambda n: (n, 0))
    tab_p = pl.BlockSpec((BLK, LANE), lambda n: (jnp.maximum(n - 1, 0), 0))
    return pl.pallas_call(
        body, name=name, grid=(nb,),
        in_specs=[pl.BlockSpec(memory_space=pltpu.SMEM),
                  pl.BlockSpec((BLK, Q_END), lambda n: (n, 0)),
                  pl.BlockSpec((BLK, KV_W), lambda n: (n, kvb)),
                  pl.BlockSpec((BLK, KV_W), lambda n: (jnp.maximum(n - 1, 0), kvb)),
                  tab_c, tab_c, tab_c, tab_p, tab_p, tab_p],
        out_specs=pl.BlockSpec((BLK, Q_END), lambda n: (n, 0)),
        out_shape=jax.ShapeDtypeStruct((S, Q_END), bf16),
        compiler_params=_params("parallel"),
    )(sinks, p, p, p, rope_c, rope_s1, rope_s2, rope_c, rope_s1, rope_s2)


def attn_bwd(p, dcat, rope_c, rope_s1, rope_s2, sinks, name):
    S = p.shape[0]
    nb = S // BLK
    kvb = Q_END // KV_W

    def body(sink_ref, q_ref, kvc_ref, kvp_ref, do_ref, cc_ref, s1c_ref, s2c_ref, cp_ref, s1p_ref, s2p_ref,
             dq_ref, dkv_ref, dsink_ref, carry, dq_scr, dkv_scr):
        n = pl.program_id(0)

        @pl.when(n == 0)
        def _():
            carry[...] = jnp.zeros_like(carry)
            dsink_ref[...] = jnp.zeros_like(dsink_ref)

        cp, s1p, s2p = cp_ref[...], s1p_ref[...], s2p_ref[...]
        cp2, s1p2, s2p2 = jnp.tile(cp, (1, 2)), jnp.tile(s1p, (1, 2)), jnp.tile(s2p, (1, 2))

        @pl.when(n < nb)
        def _():
            cc, s1c, s2c = cc_ref[...], s1c_ref[...], s2c_ref[...]
            cc8, s1c8, s2c8 = jnp.tile(cc, (1, 8)), jnp.tile(s1c, (1, 8)), jnp.tile(s2c, (1, 8))
            q = _rope(q_ref[...], cc8, s1c8, s2c8).astype(bf16)
            kc = _rope(kvc_ref[:, :256], jnp.tile(cc, (1, 2)), jnp.tile(s1c, (1, 2)), jnp.tile(s2c, (1, 2)))
            kp = _rope(kvp_ref[:, :256], cp2, s1p2, s2p2)
            k = jnp.concatenate([kp, kc], axis=0).astype(bf16)
            v = jnp.concatenate([kvp_ref[:, 256:], kvc_ref[:, 256:]], axis=0).astype(bf16)
            do = do_ref[...].astype(bf16)
            valid = _attn_mask(n)
            lane = lax.broadcasted_iota(jnp.int32, (1, LANE), 1)
            dsink = jnp.zeros((1, LANE), f32)
            for h in range(N_KV_HEADS):
                kh = k[:, h * HEAD_DIM:(h + 1) * HEAD_DIM]
                vh = v[:, h * HEAD_DIM:(h + 1) * HEAD_DIM]
                dkh = jnp.zeros((2 * BLK, HEAD_DIM), f32)
                dvh = jnp.zeros((2 * BLK, HEAD_DIM), f32)
                for g in range(GQ):
                    hq = h * GQ + g
                    qh = q[:, hq * HEAD_DIM:(hq + 1) * HEAD_DIM]
                    doh = do[:, hq * HEAD_DIM:(hq + 1) * HEAD_DIM]
                    s = _dot(qh, kh, NT) * (HEAD_DIM ** -0.5)
                    pr, ps = _softmax_sink(s, valid, sink_ref[hq])
                    dpr = _dot(doh, vh, NT)
                    dvh = dvh + _dot(pr.astype(bf16), doh, TN)
                    row = jnp.sum(pr * dpr, axis=-1, keepdims=True)
                    ds = (pr * (dpr - row) * (HEAD_DIM ** -0.5)).astype(bf16)
                    dsink = dsink + jnp.where(lane == hq, -jnp.sum(ps * row, axis=0, keepdims=True), 0.0)
                    dq_scr[:, hq * HEAD_DIM:(hq + 1) * HEAD_DIM] = _dot(ds, kh)
                    dkh = dkh + _dot(ds, qh, TN)
                dkv_scr[:, h * HEAD_DIM:(h + 1) * HEAD_DIM] = dkh
                dkv_scr[:, 256 + h * HEAD_DIM:256 + (h + 1) * HEAD_DIM] = dvh
            dsink_ref[...] += dsink
            dq_ref[...] = _rope_t(dq_scr[...], cc8, s1c8, s2c8).astype(bf16)

        prev = carry[...]

        @pl.when(n < nb)
        def _():
            dkv_scr[pl.ds(0, BLK), :] = dkv_scr[pl.ds(0, BLK), :] + prev

        @pl.when(n == nb)
        def _():
            dkv_scr[pl.ds(0, BLK), :] = prev

        done = dkv_scr[pl.ds(0, BLK), :]
        dkv_ref[:, :256] = _rope_t(done[:, :256], cp2, s1p2, s2p2).astype(bf16)
        dkv_ref[:, 256:] = done[:, 256:].astype(bf16)

        @pl.when(n < nb)
        def _():
            carry[...] = dkv_scr[pl.ds(BLK, BLK), :]

    cur = lambda n: jnp.minimum(n, nb - 1)
    prv = lambda n: jnp.maximum(n - 1, 0)
    tab_c = pl.BlockSpec((BLK, LANE), lambda n: (cur(n), 0))
    tab_p = pl.BlockSpec((BLK, LANE), lambda n: (prv(n), 0))
    return pl.pallas_call(
        body, name=name, grid=(nb + 1,),
        in_specs=[pl.BlockSpec(memory_space=pltpu.SMEM),
                  pl.BlockSpec((BLK, Q_END), lambda n: (cur(n), 0)),
                  pl.BlockSpec((BLK, KV_W), lambda n: (cur(n), kvb)),
                  pl.BlockSpec((BLK, KV_W), lambda n: (prv(n), kvb)),
                  pl.BlockSpec((BLK, Q_END), lambda n: (cur(n), 0)),
                  tab_c, tab_c, tab_c, tab_p, tab_p, tab_p],
        out_specs=[pl.BlockSpec((BLK, Q_END), lambda n: (cur(n), 0)),
                   pl.BlockSpec((BLK, KV_W), lambda n: (prv(n), 0)),
                   pl.BlockSpec((1, LANE), lambda n: (0, 0))],
        out_shape=[jax.ShapeDtypeStruct((S, Q_END), bf16), jax.ShapeDtypeStruct((S, KV_W), bf16),
                   jax.ShapeDtypeStruct((1, LANE), f32)],
        scratch_shapes=[pltpu.VMEM((BLK, KV_W), f32), pltpu.VMEM((BLK, Q_END), f32), pltpu.VMEM((2 * BLK, KV_W), f32)],
        compiler_params=_params("arbitrary"),
    )(sinks, p, p, p, dcat, rope_c, rope_s1, rope_s2, rope_c, rope_s1, rope_s2)


CONV_ROWS = 32
A1_BLK = V_END // CONV_CH
A2_BLK = A1_BLK + 1


def _ln_stats(y):
    mu = jnp.mean(y, axis=-1, keepdims=True)
    xc = y - mu
    rstd = lax.rsqrt(jnp.mean(xc * xc, axis=-1, keepdims=True) + NORM_EPS)
    return xc * rstd, rstd


def conv_fwd(p, w, b, lng, lnb, name):
    S = p.shape[0]
    T = _tile(S, 256)
    r = T // HALO

    def body(a1_ref, a2_ref, h1_ref, h2_ref, w_ref, b_ref, g_ref, bb_ref, o_ref, y_ref, scr):
        i = pl.program_id(0)
        halo = h1_ref[...] * _sig(h2_ref[...])
        scr[pl.ds(0, HALO), :] = jnp.where(i > 0, halo, 0.0)
        scr[pl.ds(HALO, T), :] = a1_ref[...] * _sig(a2_ref[...])
        acc = jnp.zeros((T, CONV_CH), f32) + b_ref[...]
        for j in range(CONV_W):
            acc = acc + scr[pl.ds(HALO - (CONV_W - 1) + j, T), :] * w_ref[j:j + 1, :]
        y_ref[...] = acc
        yh, _ = _ln_stats(acc)
        z = yh * g_ref[...] + bb_ref[...]
        o_ref[...] = (z * _sig(z)).astype(bf16)

    vec = pl.BlockSpec((1, CONV_CH), lambda i: (0, 0))
    halo_map = lambda i: jnp.maximum(i * r - 1, 0)
    return pl.pallas_call(
        body, name=name, grid=(S // T,),
        in_specs=[pl.BlockSpec((T, CONV_CH), lambda i: (i, A1_BLK)), pl.BlockSpec((T, CONV_CH), lambda i: (i, A2_BLK)),
                  pl.BlockSpec((HALO, CONV_CH), lambda i: (halo_map(i), A1_BLK)),
                  pl.BlockSpec((HALO, CONV_CH), lambda i: (halo_map(i), A2_BLK)),
                  pl.BlockSpec((HALO, CONV_CH), lambda i: (0, 0)), vec, vec, vec],
        out_specs=[pl.BlockSpec((T, CONV_CH), lambda i: (i, 0)), pl.BlockSpec((T, CONV_CH), lambda i: (i, 0))],
        out_shape=[jax.ShapeDtypeStruct((S, CONV_CH), bf16), jax.ShapeDtypeStruct((S, CONV_CH), f32)],
        scratch_shapes=[pltpu.VMEM((T + HALO, CONV_CH), f32)],
        compiler_params=_params("parallel"),
    )(p, p, p, p, w, b, lng, lnb)


def conv_bwd(p, y, dcat, w, lng, lnb, name):
    S = p.shape[0]
    T = _tile(S, 256)
    n = S // T
    r = T // HALO
    dcb = Q_END // CONV_CH
    rc = _tile(T, CONV_ROWS)

    def body(a1_ref, a2_ref, h1_ref, h2_ref, y_ref, yn_ref, do_ref, don_ref, w_ref, g_ref, bb_ref,
             da_ref, dw_ref, db_ref, dg_ref, dbb_ref, scr_h, scr_dy, acc_b, acc_g, acc_bb):
        i = pl.program_id(0)

        @pl.when(i == 0)
        def _():
            dw_ref[...] = jnp.zeros_like(dw_ref)
            acc_b[...] = jnp.zeros_like(acc_b)
            acc_g[...] = jnp.zeros_like(acc_g)
            acc_bb[...] = jnp.zeros_like(acc_bb)

        gv, bv = g_ref[...], bb_ref[...]

        def ln_silu_bwd(yv, dout):
            yh, rstd = _ln_stats(yv)
            z = yh * gv + bv
            sg = _sig(z)
            dz = dout * (sg * (1.0 + z * (1.0 - sg)))
            gz = dz * gv
            dy = rstd * (gz - jnp.mean(gz, axis=-1, keepdims=True) - yh * jnp.mean(gz * yh, axis=-1, keepdims=True))
            return dy, dz, yh

        chunks = [pl.ds(c * rc, rc) for c in range(T // rc)]
        dyn, _, _ = ln_silu_bwd(yn_ref[...], don_ref[...])
        scr_dy[pl.ds(T, HALO), :] = jnp.where(i < n - 1, dyn, 0.0)
        halo = h1_ref[...] * _sig(h2_ref[...])
        scr_h[pl.ds(0, HALO), :] = jnp.where(i > 0, halo, 0.0)
        for c, rows in enumerate(chunks):
            dy, dz, yh = ln_silu_bwd(y_ref[rows, :], do_ref[rows, :])
            acc_g[...] += _rowsum8(dz * yh)
            acc_bb[...] += _rowsum8(dz)
            acc_b[...] += _rowsum8(dy)
            scr_dy[rows, :] = dy
            scr_h[pl.ds(HALO + c * rc, rc), :] = a1_ref[rows, :] * _sig(a2_ref[rows, :])
        for c, rows in enumerate(chunks):
            dh = jnp.zeros((rc, CONV_CH), f32)
            for j in range(CONV_W):
                dh = dh + scr_dy[pl.ds(c * rc + CONV_W - 1 - j, rc), :] * w_ref[j:j + 1, :]
            a1 = a1_ref[rows, :]
            sg2 = _sig(a2_ref[rows, :])
            da_ref[rows, :CONV_CH] = (dh * sg2).astype(bf16)
            da_ref[rows, CONV_CH:] = (dh * a1 * sg2 * (1.0 - sg2)).astype(bf16)
        for j in range(CONV_W):
            part = jnp.zeros((8, CONV_CH), f32)
            for c, rows in enumerate(chunks):
                part = part + _rowsum8(scr_dy[rows, :] * scr_h[pl.ds(c * rc + HALO - (CONV_W - 1) + j, rc), :])
            dw_ref[j:j + 1, :] += jnp.sum(part, axis=0, keepdims=True)

        @pl.when(i == n - 1)
        def _():
            db_ref[...] = jnp.sum(acc_b[...], axis=0, keepdims=True)
            dg_ref[...] = jnp.sum(acc_g[...], axis=0, keepdims=True)
            dbb_ref[...] = jnp.sum(acc_bb[...], axis=0, keepdims=True)

    vec = pl.BlockSpec((1, CONV_CH), lambda i: (0, 0))
    tap = pl.BlockSpec((HALO, CONV_CH), lambda i: (0, 0))
    prev_map = lambda i: jnp.maximum(i * r - 1, 0)
    next_map = lambda i: jnp.minimum((i + 1) * r, S // HALO - 1)
    return pl.pallas_call(
        body, name=name, grid=(n,),
        in_specs=[pl.BlockSpec((T, CONV_CH), lambda i: (i, A1_BLK)), pl.BlockSpec((T, CONV_CH), lambda i: (i, A2_BLK)),
                  pl.BlockSpec((HALO, CONV_CH), lambda i: (prev_map(i), A1_BLK)),
                  pl.BlockSpec((HALO, CONV_CH), lambda i: (prev_map(i), A2_BLK)),
                  pl.BlockSpec((T, CONV_CH), lambda i: (i, 0)),
                  pl.BlockSpec((HALO, CONV_CH), lambda i: (next_map(i), 0)),
                  pl.BlockSpec((T, CONV_CH), lambda i: (i, dcb)),
                  pl.BlockSpec((HALO, CONV_CH), lambda i: (next_map(i), dcb)),
                  tap, vec, vec],
        out_specs=[pl.BlockSpec((T, 2 * CONV_CH), lambda i: (i, 0)), tap, vec, vec, vec],
        out_shape=[jax.ShapeDtypeStruct((S, 2 * CONV_CH), bf16), jax.ShapeDtypeStruct((HALO, CONV_CH), f32),
                   jax.ShapeDtypeStruct((1, CONV_CH), f32), jax.ShapeDtypeStruct((1, CONV_CH), f32),
                   jax.ShapeDtypeStruct((1, CONV_CH), f32)],
        scratch_shapes=[pltpu.VMEM((T + HALO, CONV_CH), f32), pltpu.VMEM((T + HALO, CONV_CH), f32),
                        pltpu.VMEM((8, CONV_CH), f32), pltpu.VMEM((8, CONV_CH), f32), pltpu.VMEM((8, CONV_CH), f32)],
        compiler_params=_params("arbitrary"),
    )(p, p, p, p, y, y, dcat, dcat, w, lng, lnb)


U_BLK = (V_END + 2 * CONV_CH) // SGU_CH
SV_BLK = U_BLK + 1


def _tril(w, transposed=False):
    row = lax.broadcasted_iota(jnp.int32, (BLK, BLK), 0)
    col = lax.broadcasted_iota(jnp.int32, (BLK, BLK), 1)
    keep = (col >= row) if transposed else (row >= col)
    return jnp.where(keep, w, 0.0)


def sgu_fwd(p, lng, lnb, w, bias, name):
    S = p.shape[0]
    T = _tile(S, 256)

    def body(u_ref, v_ref, g_ref, bb_ref, w_ref, bias_ref, o_ref):
        yh, _ = _ln_stats(v_ref[...])
        v = (yh * g_ref[...] + bb_ref[...]).astype(bf16)
        low = lax.broadcasted_iota(jnp.int32, (BLK, LANE), 1) < HEAD_DIM
        for pr in range(SGU_HEADS // 2):
            lanes = pl.ds(pr * LANE, LANE)
            w0 = _tril(w_ref[2 * pr]).astype(bf16)
            w1 = _tril(w_ref[2 * pr + 1]).astype(bf16)
            for c in range(T // BLK):
                rows = pl.ds(c * BLK, BLK)
                vp = v[c * BLK:(c + 1) * BLK, pr * LANE:(pr + 1) * LANE]
                mixed = jnp.where(low, _dot(w0, vp), _dot(w1, vp)) + bias_ref[:, lanes]
                o_ref[rows, lanes] = (u_ref[rows, lanes] * mixed).astype(bf16)

    vec = pl.BlockSpec((1, SGU_CH), lambda i: (0, 0))
    return pl.pallas_call(
        body, name=name, grid=(S // T,),
        in_specs=[pl.BlockSpec((T, SGU_CH), lambda i: (i, U_BLK)), pl.BlockSpec((T, SGU_CH), lambda i: (i, SV_BLK)),
                  vec, vec, pl.BlockSpec((SGU_HEADS, BLK, BLK), lambda i: (0, 0, 0)),
                  pl.BlockSpec((BLK, SGU_CH), lambda i: (0, 0))],
        out_specs=pl.BlockSpec((T, SGU_CH), lambda i: (i, 0)),
        out_shape=jax.ShapeDtypeStruct((S, SGU_CH), bf16),
        compiler_params=_params("parallel"),
    )(p, p, lng, lnb, w, bias)


def sgu_bwd(p, dcat, lng, lnb, w, wt, bias, name):
    S = p.shape[0]
    T = _tile(S, 256)
    n = S // T
    dsb = (Q_END + CONV_CH) // SGU_CH

    def body(u_ref, v_ref, do_ref, g_ref, bb_ref, w_ref, wt_ref, bias_ref,
             da_ref, dw_ref, db_ref, dg_ref, dbb_ref, dv_scr, acc_bias, acc_g, acc_bb):
        i = pl.program_id(0)

        @pl.when(i == 0)
        def _():
            dw_ref[...] = jnp.zeros_like(dw_ref)
            acc_bias[...] = jnp.zeros_like(acc_bias)
            acc_g[...] = jnp.zeros_like(acc_g)
            acc_bb[...] = jnp.zeros_like(acc_bb)

        gv = g_ref[...]
        yh, rstd = _ln_stats(v_ref[...])
        v = (yh * gv + bb_ref[...]).astype(bf16)
        low = lax.broadcasted_iota(jnp.int32, (BLK, LANE), 1) < HEAD_DIM
        for pr in range(SGU_HEADS // 2):
            lanes = pl.ds(pr * LANE, LANE)
            w0 = _tril(w_ref[2 * pr]).astype(bf16)
            w1 = _tril(w_ref[2 * pr + 1]).astype(bf16)
            wt0 = _tril(wt_ref[2 * pr], True).astype(bf16)
            wt1 = _tril(wt_ref[2 * pr + 1], True).astype(bf16)
            dw0 = jnp.zeros((BLK, BLK), f32)
            dw1 = jnp.zeros((BLK, BLK), f32)
            for c in range(T // BLK):
                rows = pl.ds(c * BLK, BLK)
                vp = v[c * BLK:(c + 1) * BLK, pr * LANE:(pr + 1) * LANE]
                mixed = jnp.where(low, _dot(w0, vp), _dot(w1, vp)) + bias_ref[:, lanes]
                do = do_ref[rows, lanes]
                da_ref[rows, lanes] = (do * mixed).astype(bf16)
                dm = do * u_ref[rows, lanes]
                acc_bias[:, lanes] += dm
                dmb = dm.astype(bf16)
                dv_scr[rows, lanes] = jnp.where(low, _dot(wt0, dmb), _dot(wt1, dmb))
                zero = jnp.zeros_like(dmb)
                dw0 = dw0 + _dot(jnp.where(low, dmb, zero), vp, NT)
                dw1 = dw1 + _dot(jnp.where(low, zero, dmb), vp, NT)
            dw_ref[2 * pr] += _tril(dw0)
            dw_ref[2 * pr + 1] += _tril(dw1)
        dv = dv_scr[...]
        acc_g[...] += _rowsum8(dv * yh)
        acc_bb[...] += _rowsum8(dv)
        gz = dv * gv
        dvr = rstd * (gz - jnp.mean(gz, axis=-1, keepdims=True) - yh * jnp.mean(gz * yh, axis=-1, keepdims=True))
        da_ref[:, SGU_CH:] = dvr.astype(bf16)

        @pl.when(i == n - 1)
        def _():
            ch = lax.broadcasted_iota(jnp.int32, (SGU_CH, LANE), 0) // HEAD_DIM
            hd = lax.broadcasted_iota(jnp.int32, (SGU_CH, LANE), 1)
            fold = jnp.where(ch == hd, 1.0, 0.0).astype(f32)
            db_ref[...] = jnp.dot(acc_bias[...], fold, preferred_element_type=f32, precision=lax.Precision.HIGHEST)
            dg_ref[...] = jnp.sum(acc_g[...], axis=0, keepdims=True)
            dbb_ref[...] = jnp.sum(acc_bb[...], axis=0, keepdims=True)

    vec = pl.BlockSpec((1, SGU_CH), lambda i: (0, 0))
    wsp = pl.BlockSpec((SGU_HEADS, BLK, BLK), lambda i: (0, 0, 0))
    return pl.pallas_call(
        body, name=name, grid=(n,),
        in_specs=[pl.BlockSpec((T, SGU_CH), lambda i: (i, U_BLK)), pl.BlockSpec((T, SGU_CH), lambda i: (i, SV_BLK)),
                  pl.BlockSpec((T, SGU_CH), lambda i: (i, dsb)), vec, vec, wsp, wsp,
                  pl.BlockSpec((BLK, SGU_CH), lambda i: (0, 0))],
        out_specs=[pl.BlockSpec((T, 2 * SGU_CH), lambda i: (i, 0)), wsp,
                   pl.BlockSpec((BLK, LANE), lambda i: (0, 0)), vec, vec],
        out_shape=[jax.ShapeDtypeStruct((S, 2 * SGU_CH), bf16), jax.ShapeDtypeStruct((SGU_HEADS, BLK, BLK), f32),
                   jax.ShapeDtypeStruct((BLK, LANE), f32), jax.ShapeDtypeStruct((1, SGU_CH), f32),
                   jax.ShapeDtypeStruct((1, SGU_CH), f32)],
        scratch_shapes=[pltpu.VMEM((T, SGU_CH), f32), pltpu.VMEM((BLK, SGU_CH), f32),
                        pltpu.VMEM((8, SGU_CH), f32), pltpu.VMEM((8, SGU_CH), f32)],
        compiler_params=_params("arbitrary"),
    )(p, p, dcat, lng, lnb, w, wt, bias)


HBM = pl.BlockSpec(memory_space=pltpu.HBM)
SEM = pl.BlockSpec(memory_space=pltpu.SEMAPHORE)
ANY = pl.BlockSpec(memory_space=pl.ANY)
EFFECT = pltpu.SideEffectType.DATAFLOW_SIDE_EFFECTING


def _flip(x, y, c, k):
    px, py, pc = x ^ (k >> 2), y ^ ((k >> 1) & 1), c ^ (k & 1)
    return (px, py, pc), 4 * px + 2 * py + pc


def _routes_gather(x, y, c):
    me = 4 * x + 2 * y + c
    out = []
    for k in (1, 2, 4, 6):
        dev, idx = _flip(x, y, c, k)
        out.append((dev, None, me, idx))
    return out


def _routes_pair(x, y, c):
    dev, _ = _flip(x, y, c, 1)
    return [(dev, 2 * q + (1 - c), q, q) for q in range(N_DEV // 2)]


def _routes_chips(x, y, c):
    out = []
    for k in (2, 4, 6):
        dev, idx = _flip(x, y, c, k)
        out.append((dev, idx // 2, 2 * x + y, idx // 2))
    return out


def _routes_forward(x, y, c):
    sib, _ = _flip(x, y, c, 1)
    out = []
    for k in (2, 4, 6):
        _, idx = _flip(x, y, c, k)
        out.append((sib, idx, idx, idx ^ 1))
    return out


def _routes_all(x, y, c):
    me = 4 * x + 2 * y + c
    out = []
    for k in range(1, N_DEV):
        dev, idx = _flip(x, y, c, k)
        out.append((dev, None, me, idx))
    return out


def _slot(ref, slot, kind):
    if slot is None:
        return ref
    if kind == "cols":
        width = ref.shape[2] // (N_DEV // 2)
        return ref.at[slot // (N_DEV // 2), :, pl.ds(pl.multiple_of((slot % (N_DEV // 2)) * width, LANE), width)]
    return ref.at[slot]


def _copies(routes, srcs, lands, send_sems, recv_sems, incoming, src_kinds, land_kinds):
    x, y, c = lax.axis_index("x"), lax.axis_index("y"), lax.axis_index("c")
    out = []
    n = len(lands)
    if srcs is None:
        srcs, src_kinds = lands, land_kinds
    for k, (dev, src_slot, dst_slot, recv_slot) in enumerate(routes(x, y, c)):
        for a in range(n):
            out.append(pltpu.make_async_remote_copy(
                src_ref=_slot(srcs[a], src_slot, src_kinds[a]),
                dst_ref=_slot(lands[a], recv_slot if incoming else dst_slot, land_kinds[a]),
                send_sem=send_sems.at[k * n + a], recv_sem=recv_sems.at[k * n + a], device_id=dev, device_id_type=MESH))
    return out


def _pin(a):
    return pltpu.with_memory_space_constraint(a, pltpu.HBM)


def split_start(srcs, lands, routes, name, deps=(), src_kinds=None, land_kinds=None):
    n = len(lands)
    ns = 0 if srcs is None else n
    n_routes = len(routes(0, 0, 0))
    ops = ([] if srcs is None else list(srcs)) + list(lands)
    src_kinds = src_kinds or ["rows"] * n
    land_kinds = land_kinds or ["rows"] * n

    def body(*refs):
        src, land = (refs[:n] if ns else None), refs[ns:ns + n]
        first_out = ns + n + len(deps)
        send_sems, recv_sems, token = refs[first_out], refs[first_out + 1], refs[-1]
        for cp in _copies(routes, src, land, send_sems, recv_sems, False, src_kinds, land_kinds):
            cp.start()
        token[...] = jnp.zeros_like(token)

    thru = [pltpu.HBM(a.shape, a.dtype) for a in ops]
    res = pl.pallas_call(
        body, name=name,
        out_shape=(pltpu.SemaphoreType.DMA((n * n_routes,)), pltpu.SemaphoreType.DMA((n * n_routes,)), *thru,
                   jax.ShapeDtypeStruct((8, LANE), f32)),
        in_specs=[HBM] * len(ops) + [ANY] * len(deps),
        out_specs=(SEM, SEM, *([HBM] * len(ops)), pl.BlockSpec(memory_space=pltpu.VMEM)),
        input_output_aliases={i: 2 + i for i in range(len(ops))},
        compiler_params=pltpu.CompilerParams(has_side_effects=EFFECT),
    )(*[_pin(a) for a in ops], *deps)
    return (res[0], res[1], (list(res[2:2 + n]) if ns else None), list(res[2 + ns:2 + ns + n]), res[-1],
            (src_kinds, land_kinds))


def split_wait(started, after, routes, name):
    send_sems, recv_sems, srcs, lands, _, (src_kinds, land_kinds) = started
    n = len(lands)
    ns = 0 if srcs is None else n
    ops = ([] if srcs is None else list(srcs)) + list(lands)
    afters = list(after) if isinstance(after, (list, tuple)) else [after]

    def body(*refs):
        src, land = (refs[:n] if ns else None), refs[ns:ns + n]
        send_s, recv_s = refs[ns + n], refs[ns + n + 1]
        for cp in _copies(routes, src, land, send_s, recv_s, True, src_kinds, land_kinds):
            cp.wait_send()
            cp.wait_recv()

    thru = [pltpu.HBM(a.shape, a.dtype) for a in ops]
    res = pl.pallas_call(
        body, name=name, out_shape=tuple(thru),
        in_specs=[HBM] * len(ops) + [SEM, SEM] + [ANY] * len(afters), out_specs=tuple([HBM] * len(ops)),
        input_output_aliases={i: i for i in range(len(ops))},
        compiler_params=pltpu.CompilerParams(has_side_effects=EFFECT),
    )(*ops, send_sems, recv_sems, *afters)
    return (list(res[:n]) if ns else None), list(res[ns:ns + n])


def chip_sum(parts, land, c_idx, kind, name):
    _, R, C = land.shape
    tr = R if R * C * 2 <= 3 * 2 ** 20 else _tile(R, 512)
    half = N_DEV // 2

    def body(c_ref, p_ref, l_ref, o_ref):
        o_ref[...] = (p_ref[...].astype(f32) + l_ref[...].astype(f32)).astype(bf16)

    if kind == "cols":
        mine = lambda q, i, c_ref: ((2 * q + c_ref[0]) // half, i, (2 * q + c_ref[0]) % half)
    else:
        mine = lambda q, i, c_ref: (2 * q + c_ref[0], i, 0)
    return pl.pallas_call(
        body, name=name,
        grid_spec=pltpu.PrefetchScalarGridSpec(
            num_scalar_prefetch=1, grid=(half, R // tr),
            in_specs=[pl.BlockSpec((None, tr, C), mine), pl.BlockSpec((None, tr, C), lambda q, i, c_ref: (q, i, 0))],
            out_specs=pl.BlockSpec((None, tr, C), lambda q, i, c_ref: (q, i, 0))),
        out_shape=jax.ShapeDtypeStruct((half, R, C), bf16),
        compiler_params=_params("parallel", "parallel"),
    )(c_idx, parts, land)


def place_own(land, src, me_idx, kind, name):
    R, C = src.shape
    tr = _tile(R, 512)
    half = N_DEV // 2
    if kind == "cols":
        where = lambda i, m: (m[0] // half, i, m[0] % half)
    else:
        where = lambda i, m: (m[0], i, 0)

    def body(m_ref, land_ref, src_ref, out_ref):
        out_ref[...] = src_ref[...]

    return pl.pallas_call(
        body, name=name,
        grid_spec=pltpu.PrefetchScalarGridSpec(
            num_scalar_prefetch=1, grid=(R // tr,),
            in_specs=[ANY, pl.BlockSpec((tr, C), lambda i, m: (i, 0))],
            out_specs=pl.BlockSpec((None, tr, C), where)),
        out_shape=jax.ShapeDtypeStruct(land.shape, land.dtype),
        input_output_aliases={1: 0},
        compiler_params=_params("arbitrary"),
    )(me_idx, land, src)


def sum_slots(parts, name):
    P, R, C = parts.shape
    tr = _tile(R, 512)

    def body(p_ref, o_ref):
        total = p_ref[0]
        for j in range(1, P):
            total = total + p_ref[j]
        o_ref[...] = total

    return pl.pallas_call(
        body, name=name, grid=(R // tr,),
        in_specs=[pl.BlockSpec((P, tr, C), lambda i: (0, i, 0))],
        out_specs=pl.BlockSpec((tr, C), lambda i: (i, 0)),
        out_shape=jax.ShapeDtypeStruct((R, C), f32),
        compiler_params=_params("parallel"),
    )(parts)


def adamw(parts, owns, chip, w, m, v, name):
    L, R, C = w.shape
    P = parts[0].shape[0]
    tr = _tile(R, 128 if C > 1024 else 256)
    nr = R // tr
    c1 = 1.0 - ADAM_B1 ** ADAM_STEP
    c2 = 1.0 - ADAM_B2 ** ADAM_STEP
    n_own = L if owns is not None else 0

    def body(chip_ref, *refs):
        part_refs, own_refs = refs[:L], refs[L:L + n_own]
        w_ref, m_ref, v_ref, g_out, d_out, m_out, v_out = refs[L + n_own:]
        layer = pl.program_id(0)
        for l in range(L):
            @pl.when(layer == l)
            def _(l=l):
                g = None
                for q in range(P):
                    term = part_refs[l][q].astype(f32)
                    if n_own:
                        term = jnp.where(chip_ref[0] == q, own_refs[l][...].astype(f32), term)
                    g = term if g is None else g + term
                mn = ADAM_B1 * m_ref[...] + (1.0 - ADAM_B1) * g
                vn = ADAM_B2 * v_ref[...] + (1.0 - ADAM_B2) * (g * g)
                g_out[...] = g
                m_out[...] = mn
                v_out[...] = vn
                d_out[...] = -ADAM_LR * ((mn / c1) / (jnp.sqrt(vn / c2) + ADAM_EPS) + ADAM_WD * w_ref[...])

    def rows(l, a, i):
        return jnp.where(a == l, i, jnp.where(a < l, 0, nr - 1))

    def part_spec(l):
        return pl.BlockSpec((P, tr, C), lambda a, i, chip_ref: (0, rows(l, a, i), 0))

    def own_spec(l):
        return pl.BlockSpec((None, tr, C), lambda a, i, chip_ref: (chip_ref[0], rows(l, a, i), 0))

    slab = pl.BlockSpec((None, tr, C), lambda a, i, chip_ref: (a, i, 0))
    out = jax.ShapeDtypeStruct((L, R, C), f32)
    return pl.pallas_call(
        body, name=name,
        grid_spec=pltpu.PrefetchScalarGridSpec(
            num_scalar_prefetch=1, grid=(L, nr),
            in_specs=[part_spec(l) for l in range(L)] + [own_spec(l) for l in range(n_own)] + [slab, slab, slab],
            out_specs=[slab, slab, slab, slab]),
        out_shape=[out, out, out, out],
        compiler_params=_params("arbitrary", "arbitrary"),
    )(chip, *parts, *(owns or []), w, m, v)


PACK = 8 * LANE
PACK_ROWS = 256


def _pack(arrs):
    pieces = []
    for a in arrs:
        flat = a.astype(f32).reshape(-1)
        pad = (-flat.shape[0]) % PACK
        pieces.append(jnp.pad(flat, (0, pad)).reshape(-1, LANE))
    rows = sum(p.shape[0] for p in pieces)
    if rows > PACK_ROWS and rows % PACK_ROWS:
        pieces.append(jnp.zeros((PACK_ROWS - rows % PACK_ROWS, LANE), f32))
    return jnp.concatenate(pieces, axis=0)


def _unpack(buf, shapes):
    out, row = [], 0
    for shp in shapes:
        size = math.prod(shp)
        rows = (size + PACK - 1) // PACK * (PACK // LANE)
        out.append(buf[row:row + rows].reshape(-1)[:size].reshape(shp))
        row += rows
    return out


def _rope_tables(positions):
    half = ROT_DIM // 2
    inv_freq = 1.0 / (ROPE_THETA ** (jnp.arange(0, ROT_DIM, 2, dtype=f32) / ROT_DIM))
    ang = positions.astype(f32)[:, None] * inv_freq
    cos, sin = jnp.cos(ang), jnp.sin(ang)
    S = positions.shape[0]
    zeros, ones = jnp.zeros((S, half), f32), jnp.ones((S, HEAD_DIM - ROT_DIM), f32)
    rest = jnp.zeros((S, HEAD_DIM - ROT_DIM), f32)
    c = jnp.concatenate([cos, cos, ones], axis=1)
    s1 = jnp.concatenate([zeros, sin, rest], axis=1)
    s2 = jnp.concatenate([-sin, zeros, rest], axis=1)
    return tuple(jnp.tile(t, (1, LANE // HEAD_DIM)) for t in (c, s1, s2))


def _cols_to_shards(g):
    lead, (R, N) = g.shape[:-2], g.shape[-2:]
    g = g.reshape(lead + (R, N_DEV, N // N_DEV))
    return jnp.moveaxis(g, -2, 0)


def _shards_to_cols(g):
    g = jnp.moveaxis(g, 0, -2)
    return g.reshape(g.shape[:-2] + (g.shape[-2] * g.shape[-1],))


def kernel(x, positions, norm_ffn1, ffn1_w_in, ffn1_w_out, norm_mix, w_in, conv_dw_w, conv_dw_b, conv_ln_g, conv_ln_b, sgu_ln_g, sgu_ln_b, sgu_w, sgu_b, attn_sinks, w_out, norm_ffn2, ffn2_w_in, ffn2_w_out, final_norm, loss_target, m_norm_ffn1, m_ffn1_w_in, m_ffn1_w_out, m_norm_mix, m_w_in, m_conv_dw_w, m_conv_dw_b, m_conv_ln_g, m_conv_ln_b, m_sgu_ln_g, m_sgu_ln_b, m_sgu_w, m_sgu_b, m_attn_sinks, m_w_out, m_norm_ffn2, m_ffn2_w_in, m_ffn2_w_out, m_final_norm, v_norm_ffn1, v_ffn1_w_in, v_ffn1_w_out, v_norm_mix, v_w_in, v_conv_dw_w, v_conv_dw_b, v_conv_ln_g, v_conv_ln_b, v_sgu_ln_g, v_sgu_ln_b, v_sgu_w, v_sgu_b, v_attn_sinks, v_w_out, v_norm_ffn2, v_ffn2_w_in, v_ffn2_w_out, v_final_norm):
    L = norm_ffn1.shape[0]
    S, D = x.shape[1], x.shape[2]
    F = ffn1_w_out.shape[1] * N_DEV
    me = 4 * lax.axis_index("x") + 2 * lax.axis_index("y") + lax.axis_index("c")
    x0 = x[0]
    rope_c, rope_s1, rope_s2 = _rope_tables(positions[0])
    cw = CONV_CH // N_DEV

    c_idx = lax.axis_index("c").astype(jnp.int32).reshape(1)
    chip = (2 * lax.axis_index("x") + lax.axis_index("y")).astype(jnp.int32).reshape(1)
    no_chip = jnp.zeros((1,), jnp.int32)
    me_idx = me.astype(jnp.int32).reshape(1)

    row = lambda a: a.reshape(1, -1)
    order_fwd = [(l, g) for l in range(L) for g in (("ffn1_in", "ffn1_out") if l == 0 else ("ffn1",)) + ("mix", "ffn2")]

    def group_srcs(l, grp):
        if grp == "mix":
            taps = jnp.pad(conv_dw_w[l], ((0, HALO - CONV_W), (0, LANE - cw)))
            return [w_in[l].astype(bf16), w_out[l].astype(bf16), taps]
        both = ([ffn2_w_in[l], ffn2_w_out[l]] if grp == "ffn2" else [ffn1_w_in[l], ffn1_w_out[l]])
        both = [a.astype(bf16) for a in both]
        return both[:1] if grp == "ffn1_in" else both[1:] if grp == "ffn1_out" else both

    def kinds_of(grp):
        return {"mix": ["rows"] * 3, "ffn1_in": ["cols"], "ffn1_out": ["rows"]}.get(grp, ["cols", "rows"])

    def gather_start(k, deps=()):
        l, grp = order_fwd[k]
        srcs = group_srcs(l, grp)
        lands = [lax.empty((2, D, F) if kind == "cols" else (N_DEV,) + a.shape, a.dtype)
                 for a, kind in zip(srcs, kinds_of(grp))]
        return split_start(srcs, lands, _routes_gather, f"gather_start_{grp}_{l}", deps, land_kinds=kinds_of(grp))

    def gather_forward(k, started, after):
        l, grp = order_fwd[k]
        srcs, lands = split_wait(started, after, _routes_gather, f"gather_wait_{grp}_{l}")
        return srcs, split_start(None, lands, _routes_forward, f"forward_start_{grp}_{l}", land_kinds=kinds_of(grp))

    def gather_finish(k, srcs, started, after):
        l, grp = order_fwd[k]
        _, lands = split_wait(started, after, _routes_forward, f"forward_wait_{grp}_{l}")
        full = [place_own(ld, s, me_idx, kind, f"own_{grp}_{l}_{a}")
                for a, (ld, s, kind) in enumerate(zip(lands, srcs, kinds_of(grp)))]
        if grp == "mix":
            return dict(w_in=_shards_to_cols(full[0]), w_out=full[1].reshape(D, D),
                        taps=_shards_to_cols(full[2][:, :, :cw]))
        named = dict(zip(["w_in", "w_out"] if len(full) == 2 else ["w_in" if grp == "ffn1_in" else "w_out"], full))
        if "w_out" in named:
            named["w_out"] = named["w_out"].reshape(F, D)
        return named

    def ffn_head(xs, wts, g_norm, tag, sv):
        sv["x_in"] = xs
        h, sv["ht"] = rmsnorm_fwd(xs, g_norm, f"norm_{tag}")
        sv["gu"], a, sv["at"] = ffn_in(h, wts["w_in"], f"{tag}_in")
        return a

    def ffn_fwd(xs, wts, g_norm, tag, sv, mid):
        a = ffn_head(xs, wts, g_norm, tag, sv)
        return mm_res(a, wts["w_out"], xs, FFN_RES, f"{tag}_out", deps=mid(a))

    def mix_fwd(xs, wts, l, sv, mid):
        sv["x_in"] = xs
        h, sv["ht"] = rmsnorm_fwd(xs, row(norm_mix[l]), f"norm_mix_{l}")
        p = mm_nn(h, wts["w_in"][None], f"mix_in_{l}")[0]
        sv["p"] = p
        attn = attn_fwd(p, rope_c, rope_s1, rope_s2, attn_sinks[l], f"attn_fwd_{l}")
        conv, sv["conv_y"] = conv_fwd(p, wts["taps"], row(conv_dw_b[l]), row(conv_ln_g[l]), row(conv_ln_b[l]),
                                      f"conv_fwd_{l}")
        sv["sgu_bias"] = jnp.repeat(sgu_b[l].T, HEAD_DIM, axis=1)
        sgu = sgu_fwd(p, row(sgu_ln_g[l]), row(sgu_ln_b[l]), sgu_w[l], sv["sgu_bias"], f"sgu_fwd_{l}")
        cat = jnp.concatenate([attn, conv, sgu], axis=1)
        sv["catt"] = cat.T
        return mm_res(cat, wts["w_out"], xs, 1.0, f"mix_out_{l}", deps=mid(cat))

    weights, saved = {}, {}
    xs = x0
    starts = []
    for k in range(len(order_fwd)):
        starts.append(gather_start(k, (starts[-1][4],) if starts else ()))
    state = dict(zip(("srcs", "fwd"), gather_forward(0, starts[0], starts[-1][4])))
    for k, (l, grp) in enumerate(order_fwd):
        nxt = starts[k + 1] if k + 1 < len(order_fwd) else None
        wts = gather_finish(k, state["srcs"], state["fwd"], state["fwd"][4])

        def mid(after, k=k, nxt=nxt):
            if not nxt:
                return ()
            state["srcs"], state["fwd"] = gather_forward(k + 1, nxt, after)
            return (state["fwd"][4],)

        if grp == "ffn1_in":
            sv = saved[l, "ffn1"] = {}
            weights[l, "ffn1"] = wts
            head = ffn_head(xs, wts, row(norm_ffn1[l]), f"ffn1_{l}", sv)
            mid(head)
        elif grp == "ffn1_out":
            weights[l, "ffn1"].update(wts)
            xs = mm_res(head, wts["w_out"], xs, FFN_RES, f"ffn1_{l}_out")
            mid(xs)
        else:
            sv = saved[l, grp] = {}
            weights[l, grp] = wts
            if grp == "mix":
                xs = mix_fwd(xs, wts, l, sv, mid)
            else:
                xs = ffn_fwd(xs, wts, row(norm_ffn1[l] if grp == "ffn1" else norm_ffn2[l]), f"{grp}_{l}", sv, mid)

    dx, dxb, d_final_norm, loss = final_loss(xs, row(final_norm), loss_target[0], "final_loss")

    def scatter_start(grads, kinds, tag, deps=()):
        half = N_DEV // 2
        lands = [lax.empty((half, g.shape[1], g.shape[2] // half) if kind == "cols" else (half,) + g.shape[1:], g.dtype)
                 for g, kind in zip(grads, kinds)]
        return split_start(grads, lands, _routes_pair, f"pair_start_{tag}", deps, src_kinds=kinds)

    def pair_to_chips(started, after, tag):
        kinds = started[5][0]
        grads, landed = split_wait(started, after, _routes_pair, f"pair_wait_{tag}")
        sums = [chip_sum(g, ld, c_idx, kind, f"chip_sum_{tag}_{a}")
                for a, (g, ld, kind) in enumerate(zip(grads, landed, kinds))]
        return split_start(sums, [lax.empty(s.shape, s.dtype) for s in sums], _routes_chips, f"chips_start_{tag}")

    def ffn_bwd(dx, dxb, wts, sv, g_norm, tag, deps=(), out_first=False):
        dgu = ffn_dact(dxb, wts["w_out"], sv["gu"], f"{tag}_dact")
        d_w_out = mm_nn(sv["at"], dxb[None], f"{tag}_dwout", bf16, FFN_RES, deps=deps)[0]
        d_w_out = d_w_out.reshape(N_DEV, F // N_DEV, D)
        if not out_first:
            d_w_in = mm_nn(sv["ht"], dgu, f"{tag}_dwin", bf16)
            pair = scatter_start([d_w_in, d_w_out], ["cols", "rows"], tag)
            dh = mm_nt(dgu, wts["w_in"], f"{tag}_dh", deps=(pair[4],))
            chips = [pair_to_chips(pair, dh, tag)]
        else:
            pair_out = scatter_start([d_w_out], ["rows"], f"{tag}_out")
            d_w_in = mm_nn(sv["ht"], dgu, f"{tag}_dwin", bf16, deps=(pair_out[4],))
            chips_out = pair_to_chips(pair_out, d_w_in, f"{tag}_out")
            pair_in = scatter_start([d_w_in], ["cols"], f"{tag}_in", deps=(chips_out[4],))
            dh = mm_nt(dgu, wts["w_in"], f"{tag}_dh", deps=(pair_in[4],))
            chips = [pair_to_chips(pair_in, dh, f"{tag}_in"), chips_out]
        dx, dxb, dg = rmsnorm_bwd(dh, sv["x_in"], g_norm, dx, f"{tag}_dnorm", deps=(chips[0][4],))
        return dx, dxb, dg, chips

    small = [None] * L
    chips_pending = {}
    for l in reversed(range(L)):
        dx, dxb, d_norm_ffn2, chips_pending[l, "ffn2"] = ffn_bwd(
            dx, dxb, weights[l, "ffn2"], saved[l, "ffn2"], row(norm_ffn2[l]), f"ffn2_{l}")

        wts, sv = weights[l, "mix"], saved[l, "mix"]
        d_w_out = mm_nn(sv["catt"], dxb[None], f"mix_dwout_{l}", bf16)[0]
        dcat = mm_nt(dxb[None], wts["w_out"][None], f"mix_dcat_{l}", deps=(d_w_out,))
        p = sv["p"]
        dq, dkv, d_sinks = attn_bwd(p, dcat, rope_c, rope_s1, rope_s2, attn_sinks[l], f"attn_bwd_{l}")
        da_conv, d_taps, d_conv_b, d_conv_g, d_conv_bb = conv_bwd(
            p, sv["conv_y"], dcat, wts["taps"], row(conv_ln_g[l]), row(conv_ln_b[l]), f"conv_bwd_{l}")
        da_sgu, d_sgu_w, d_sgu_bias, d_sgu_g, d_sgu_bb = sgu_bwd(
            p, dcat, row(sgu_ln_g[l]), row(sgu_ln_b[l]), sgu_w[l], jnp.swapaxes(sgu_w[l], 1, 2), sv["sgu_bias"],
            f"sgu_bwd_{l}")
        dp = jnp.concatenate([dq, dkv, da_conv, da_sgu], axis=1)
        d_w_in = mm_nn(sv["ht"], dp[None], f"mix_dwin_{l}", bf16)[0]
        pair = scatter_start([_cols_to_shards(d_w_in), d_w_out.reshape(N_DEV, D // N_DEV, D)], ["rows", "rows"],
                             f"mix_{l}")
        dh = mm_nt(dp[None], wts["w_in"][None], f"mix_dh_{l}", deps=(pair[4],))
        chips_pending[l, "mix"] = [pair_to_chips(pair, dh, f"mix_{l}")]
        dx, dxb, d_norm_mix = rmsnorm_bwd(dh, sv["x_in"], row(norm_mix[l]), dx, f"mix_dnorm_{l}",
                                          deps=(chips_pending[l, "mix"][0][4],))

        small[l] = dict(norm_mix=d_norm_mix[0], conv_dw_w=d_taps[:CONV_W],
                        conv_dw_b=d_conv_b[0], conv_ln_g=d_conv_g[0], conv_ln_b=d_conv_bb[0], sgu_ln_g=d_sgu_g[0],
                        sgu_ln_b=d_sgu_bb[0], sgu_w=d_sgu_w, sgu_b=d_sgu_bias[:, :SGU_HEADS].T,
                        attn_sinks=d_sinks[0, :N_Q_HEADS], norm_ffn2=d_norm_ffn2[0])
        if l == 0:
            early_names = ["norm_mix", "conv_dw_w", "conv_dw_b", "conv_ln_g", "conv_ln_b", "sgu_ln_g", "sgu_ln_b",
                           "sgu_w", "sgu_b", "attn_sinks", "norm_ffn2"]
            early = [jnp.stack([small[k][n] for k in range(L)]) for n in early_names]
            early += [d_final_norm[0], loss.reshape(1)]
            early_shapes = [a.shape for a in early]
            early = _pack(early)
            early_pending = split_start([early], [lax.empty((N_DEV,) + early.shape, f32)], _routes_all,
                                        "small_start", deps=(dxb,))
            early_token = (early_pending[4],)
        else:
            early_token = ()

        dx, dxb, d_norm_ffn1, chips_pending[l, "ffn1"] = ffn_bwd(
            dx, dxb, weights[l, "ffn1"], saved[l, "ffn1"], row(norm_ffn1[l]), f"ffn1_{l}", early_token, l == 0)
        small[l]["norm_ffn1"] = d_norm_ffn1[0]

    grad_x = dx[None]
    late = _pack([jnp.stack([small[l]["norm_ffn1"] for l in range(L)])])
    late_pending = split_start([late], [lax.empty((N_DEV,) + late.shape, f32)], _routes_all, "late_start", deps=(dx,))

    def landed(grp, after):
        sums, lands = [], []
        for l in range(L):
            got = [split_wait(st, after, _routes_chips, f"chips_wait_{grp}_{l}_{a}")
                   for a, st in enumerate(chips_pending[l, grp])]
            sums.append([s for g in got for s in g[0]])
            lands.append([s for g in got for s in g[1]])
        return sums, lands

    given = dict(norm_ffn1=(norm_ffn1, m_norm_ffn1, v_norm_ffn1), norm_mix=(norm_mix, m_norm_mix, v_norm_mix),
                 conv_dw_w=(conv_dw_w, m_conv_dw_w, v_conv_dw_w), conv_dw_b=(conv_dw_b, m_conv_dw_b, v_conv_dw_b),
                 conv_ln_g=(conv_ln_g, m_conv_ln_g, v_conv_ln_g), conv_ln_b=(conv_ln_b, m_conv_ln_b, v_conv_ln_b),
                 sgu_ln_g=(sgu_ln_g, m_sgu_ln_g, v_sgu_ln_g), sgu_ln_b=(sgu_ln_b, m_sgu_ln_b, v_sgu_ln_b),
                 sgu_w=(sgu_w, m_sgu_w, v_sgu_w), sgu_b=(sgu_b, m_sgu_b, v_sgu_b),
                 attn_sinks=(attn_sinks, m_attn_sinks, v_attn_sinks), norm_ffn2=(norm_ffn2, m_norm_ffn2, v_norm_ffn2),
                 final_norm=(final_norm, m_final_norm, v_final_norm))

    def small_update(pending, after, names, shapes, tag):
        (own,), (others,) = split_wait(pending, after, _routes_all, f"{tag}_wait")
        total = _unpack(sum_slots(place_own(others, own, me_idx, "rows", f"{tag}_own"), f"{tag}_sum"), shapes)
        g = dict(zip(names, total))
        if "conv_dw_w" in g:
            g["conv_dw_w"] = lax.dynamic_slice_in_dim(g["conv_dw_w"], me * cw, cw, axis=2)
        upd_names = [n for n in names if n in given]
        upd_shapes = [given[n][0].shape for n in upd_names]
        packed = [_pack([g[n] for n in upd_names])[None]] + [_pack([given[n][k] for n in upd_names])[None]
                                                              for k in range(3)]
        res = adamw([packed[0]], None, no_chip, packed[1], packed[2], packed[3], f"{tag}_adamw")
        return g, [dict(zip(upd_names, _unpack(r[0], upd_shapes))) for r in res], res[0]

    big = {}
    done = [dx]

    def big_update(grp, names):
        sums, lands = landed(grp, done)
        for idx, (name, w, m, v) in enumerate(names):
            big[name] = adamw([lands[l][idx] for l in range(L)], [sums[l][idx] for l in range(L)], chip, w, m, v,
                              f"adamw_{name}")
            done.append(big[name][0])

    big_update("ffn2", (("ffn2_w_in", ffn2_w_in, m_ffn2_w_in, v_ffn2_w_in),
                        ("ffn2_w_out", ffn2_w_out, m_ffn2_w_out, v_ffn2_w_out)))
    big_update("mix", (("w_in", w_in, m_w_in, v_w_in), ("w_out", w_out, m_w_out, v_w_out)))
    g_early, upd_early, marker = small_update(early_pending, done, early_names + ["final_norm", "loss"], early_shapes,
                                              "small")
    done.append(marker)
    big_update("ffn1", (("ffn1_w_in", ffn1_w_in, m_ffn1_w_in, v_ffn1_w_in),
                        ("ffn1_w_out", ffn1_w_out, m_ffn1_w_out, v_ffn1_w_out)))
    _, upd_late, _ = small_update(late_pending, done, ["norm_ffn1"], [(L, D)], "late")
    upd = [{**upd_early[k], **upd_late[k]} for k in range(4)]

    order = ["norm_ffn1", "ffn1_w_in", "ffn1_w_out", "norm_mix", "w_in", "conv_dw_w", "conv_dw_b", "conv_ln_g",
             "conv_ln_b", "sgu_ln_g", "sgu_ln_b", "sgu_w", "sgu_b", "attn_sinks", "w_out", "norm_ffn2", "ffn2_w_in",
             "ffn2_w_out", "final_norm"]
    outs = [g_early["loss"].reshape(()), grad_x]
    for k in range(4):
        outs += [big[n][k] if n in big else upd[k][n] for n in order]
    return tuple(outs)
```

```python
import functools
import math

import jax
import jax.numpy as jnp
from jax import lax
from jax.experimental import pallas as pl
from jax.experimental.pallas import tpu as pltpu

f32 = jnp.float32
bf16 = jnp.bfloat16

N_DEV = 8
HEAD_DIM = 64
N_Q_HEADS = 16
N_KV_HEADS = 4
GQ = N_Q_HEADS // N_KV_HEADS
BLK = 128
ROT_DIM = 16
ROPE_THETA = 500000.0
CONV_W = 31
CONV_CH = 512
SGU_CH = 512
SGU_HEADS = 8
Q_END = N_Q_HEADS * HEAD_DIM
KV_W = 2 * N_KV_HEADS * HEAD_DIM
V_END = Q_END + KV_W
IN_COLS = V_END + 2 * CONV_CH + 2 * SGU_CH
HALO = 32
NORM_EPS = 1e-5
FFN_RES = 0.5
ADAM_LR, ADAM_B1, ADAM_B2, ADAM_EPS, ADAM_WD, ADAM_STEP = 0.001, 0.9, 0.999, 1e-08, 0.01, 10
LANE = 128
VMEM_LIMIT = 56 * 2 ** 20
MM_ROWS = 1024
MM_BLOCK_BYTES = 16 * 2 ** 20
MESH = pl.DeviceIdType.MESH

NN = (((1,), (0,)), ((), ()))
NT = (((1,), (1,)), ((), ()))
TN = (((0,), (0,)), ((), ()))


def _tile(dim, pref):
    t = min(pref, dim)
    while dim % t:
        t //= 2
    return t


def _row_tile(rows, k, most=2048):
    for n in range(1, 17):
        t = rows // n
        if rows % n == 0 and t % LANE == 0 and t <= most and t * k * 2 <= MM_BLOCK_BYTES:
            return t
    return _tile(rows, 512)


def _k_blocks(k):
    return 2 if k > 4096 and k % (2 * LANE) == 0 else 1


def _params(*sem):
    return pltpu.CompilerParams(dimension_semantics=sem, vmem_limit_bytes=VMEM_LIMIT)


def _sig(x):
    return 1.0 / (1.0 + jnp.exp(-x))


def _dot(a, b, dims=NN):
    return lax.dot_general(a, b, dims, preferred_element_type=f32)


def _rowsum8(x):
    return x.reshape(x.shape[0] // 8, 8, x.shape[1]).sum(axis=0)


def rmsnorm_fwd(x, g, name, deps=()):
    S, D = x.shape
    tm = _tile(S, 512)

    def body(x_ref, g_ref, *rest):
        o_ref, ot_ref = rest[-2:]
        xv = x_ref[...]
        r = lax.rsqrt(jnp.mean(xv * xv, axis=-1, keepdims=True) + NORM_EPS)
        hb = (xv * r * g_ref[...]).astype(bf16)
        o_ref[...] = hb
        ot_ref[...] = hb.T

    return pl.pallas_call(
        body, name=name, grid=(S // tm,),
        in_specs=[pl.BlockSpec((tm, D), lambda i: (i, 0)), pl.BlockSpec((1, D), lambda i: (0, 0))] + [ANY] * len(deps),
        out_specs=[pl.BlockSpec((tm, D), lambda i: (i, 0)), pl.BlockSpec((D, tm), lambda i: (0, i))],
        out_shape=[jax.ShapeDtypeStruct((S, D), bf16), jax.ShapeDtypeStruct((D, S), bf16)],
        compiler_params=_params("parallel"),
    )(x, g, *deps)


def rmsnorm_bwd(dh, x, g, dres, name, deps=()):
    S, D = x.shape
    tm = _tile(S, 256)
    n = S // tm

    def body(dh_ref, x_ref, g_ref, dres_ref, *rest):
        dx_ref, dxb_ref, dg_ref, acc = rest[-4:]
        i = pl.program_id(0)

        @pl.when(i == 0)
        def _():
            acc[...] = jnp.zeros_like(acc)

        xv = x_ref[...]
        r = lax.rsqrt(jnp.mean(xv * xv, axis=-1, keepdims=True) + NORM_EPS)
        xh = xv * r
        dy = dh_ref[...]
        gy = dy * g_ref[...]
        dx = dres_ref[...] + r * (gy - xh * jnp.mean(gy * xh, axis=-1, keepdims=True))
        dx_ref[...] = dx
        dxb_ref[...] = dx.astype(bf16)
        acc[...] += _rowsum8(dy * xh)

        @pl.when(i == n - 1)
        def _():
            dg_ref[...] = jnp.sum(acc[...], axis=0, keepdims=True)

    row = pl.BlockSpec((tm, D), lambda i: (i, 0))
    vec = pl.BlockSpec((1, D), lambda i: (0, 0))
    return pl.pallas_call(
        body, name=name, grid=(n,),
        in_specs=[row, row, vec, row] + [ANY] * len(deps),
        out_specs=[row, row, vec],
        out_shape=[jax.ShapeDtypeStruct((S, D), f32), jax.ShapeDtypeStruct((S, D), bf16),
                   jax.ShapeDtypeStruct((1, D), f32)],
        scratch_shapes=[pltpu.VMEM((8, D), f32)],
        compiler_params=_params("arbitrary"),
    )(dh, x, g, dres, *deps)


def final_loss(x, g, tgt, name):
    S, D = x.shape
    tm = _tile(S, 256)
    n = S // tm

    def body(x_ref, g_ref, t_ref, dx_ref, dxb_ref, dg_ref, loss_ref, acc):
        i = pl.program_id(0)

        @pl.when(i == 0)
        def _():
            acc[...] = jnp.zeros_like(acc)
            loss_ref[...] = jnp.zeros_like(loss_ref)

        xv = x_ref[...]
        gv = g_ref[...]
        r = lax.rsqrt(jnp.mean(xv * xv, axis=-1, keepdims=True) + NORM_EPS)
        xh = xv * r
        diff = xh * gv - t_ref[...]
        tok = jnp.mean(diff * diff, axis=-1, keepdims=True)
        loss_ref[...] += 0.5 * jnp.sum(tok, axis=0, keepdims=True)
        dy = diff / D
        gy = dy * gv
        dx = r * (gy - xh * jnp.mean(gy * xh, axis=-1, keepdims=True))
        dx_ref[...] = dx
        dxb_ref[...] = dx.astype(bf16)
        acc[...] += _rowsum8(dy * xh)

        @pl.when(i == n - 1)
        def _():
            dg_ref[...] = jnp.sum(acc[...], axis=0, keepdims=True)

    row = pl.BlockSpec((tm, D), lambda i: (i, 0))
    vec = pl.BlockSpec((1, D), lambda i: (0, 0))
    return pl.pallas_call(
        body, name=name, grid=(n,),
        in_specs=[row, vec, row],
        out_specs=[row, row, vec, pl.BlockSpec((1, 1), lambda i: (0, 0))],
        out_shape=[jax.ShapeDtypeStruct((S, D), f32), jax.ShapeDtypeStruct((S, D), bf16),
                   jax.ShapeDtypeStruct((1, D), f32), jax.ShapeDtypeStruct((1, 1), f32)],
        scratch_shapes=[pltpu.VMEM((8, D), f32)],
        compiler_params=_params("arbitrary"),
    )(x, g, tgt)


def ffn_in(h, w2, name):
    S, D = h.shape
    F = w2.shape[2]
    tm, tn = _row_tile(S, D), _tile(F, 512)

    def body(h_ref, w_ref, gu_ref, a_ref, at_ref):
        hv = h_ref[...]
        g = _dot(hv, w_ref[0])
        u = _dot(hv, w_ref[1])
        gu_ref[0] = g.astype(bf16)
        gu_ref[1] = u.astype(bf16)
        a = (g * _sig(g) * u).astype(bf16)
        a_ref[...] = a
        at_ref[...] = a.T

    return pl.pallas_call(
        body, name=name, grid=(S // tm, F // tn),
        in_specs=[pl.BlockSpec((tm, D), lambda i, j: (i, 0)), pl.BlockSpec((2, D, tn), lambda i, j: (0, 0, j))],
        out_specs=[pl.BlockSpec((2, tm, tn), lambda i, j: (0, i, j)), pl.BlockSpec((tm, tn), lambda i, j: (i, j)),
                   pl.BlockSpec((tn, tm), lambda i, j: (j, i))],
        out_shape=[jax.ShapeDtypeStruct((2, S, F), bf16), jax.ShapeDtypeStruct((S, F), bf16),
                   jax.ShapeDtypeStruct((F, S), bf16)],
        compiler_params=_params("parallel", "parallel"),
    )(h, w2)


def mm_res(a, w, x, scale, name, deps=()):
    S, K = a.shape
    N = w.shape[1]
    nk = _k_blocks(K)
    tk = K // nk
    tm, tn = _row_tile(S, tk), _tile(N, 512)

    def body(a_ref, w_ref, x_ref, *rest):
        o_ref = rest[-1]
        part = _dot(a_ref[...], w_ref[...])
        if nk == 1:
            o_ref[...] = x_ref[...] + scale * part
        else:
            k = pl.program_id(2)

            @pl.when(k == 0)
            def _():
                o_ref[...] = part

            @pl.when((k > 0) & (k < nk - 1))
            def _():
                o_ref[...] += part

            @pl.when(k == nk - 1)
            def _():
                o_ref[...] = x_ref[...] + scale * (o_ref[...] + part)

    return pl.pallas_call(
        body, name=name, grid=(S // tm, N // tn, nk),
        in_specs=[pl.BlockSpec((tm, tk), lambda i, j, k: (i, k)), pl.BlockSpec((tk, tn), lambda i, j, k: (k, j)),
                  pl.BlockSpec((tm, tn), lambda i, j, k: (i, j))] + [ANY] * len(deps),
        out_specs=pl.BlockSpec((tm, tn), lambda i, j, k: (i, j)),
        out_shape=jax.ShapeDtypeStruct((S, N), f32),
        compiler_params=_params("parallel", "parallel", "arbitrary"),
    )(a, w, x, *deps)


def mm_nn(a, b, name, out_dtype=f32, scale=1.0, deps=()):
    M, K = a.shape
    G, _, N = b.shape
    tm, tn = _row_tile(M, K), _tile(N, 512)

    def body(a_ref, b_ref, *rest):
        acc = _dot(a_ref[...], b_ref[...])
        rest[-1][...] = (acc if scale == 1.0 else scale * acc).astype(out_dtype)

    return pl.pallas_call(
        body, name=name, grid=(G, M // tm, N // tn),
        in_specs=[pl.BlockSpec((tm, K), lambda g, i, j: (i, 0)),
                  pl.BlockSpec((None, K, tn), lambda g, i, j: (g, 0, j))] + [ANY] * len(deps),
        out_specs=pl.BlockSpec((None, tm, tn), lambda g, i, j: (g, i, j)),
        out_shape=jax.ShapeDtypeStruct((G, M, N), out_dtype),
        compiler_params=_params("parallel", "parallel", "parallel"),
    )(a, b, *deps)


def mm_nt(a, w, name, deps=()):
    G, S, K = a.shape
    N = w.shape[1]
    nk = _k_blocks(K)
    tk = K // nk
    tm, tn = _row_tile(S, tk), _tile(N, 512)
    steps = G * nk

    def body(a_ref, w_ref, *rest):
        o_ref = rest[-1]
        part = _dot(a_ref[...], w_ref[...], NT)
        if steps == 1:
            o_ref[...] = part
        else:
            s = pl.program_id(2)

            @pl.when(s == 0)
            def _():
                o_ref[...] = part

            @pl.when(s > 0)
            def _():
                o_ref[...] += part

    return pl.pallas_call(
        body, name=name, grid=(S // tm, N // tn, steps),
        in_specs=[pl.BlockSpec((None, tm, tk), lambda i, j, s: (s // nk, i, s % nk)),
                  pl.BlockSpec((None, tn, tk), lambda i, j, s: (s // nk, j, s % nk))] + [ANY] * len(deps),
        out_specs=pl.BlockSpec((tm, tn), lambda i, j, s: (i, j)),
        out_shape=jax.ShapeDtypeStruct((S, N), f32),
        compiler_params=_params("parallel", "parallel", "arbitrary"),
    )(a, w, *deps)


def ffn_dact(dx, wout, gu, name):
    S, D = dx.shape
    F = wout.shape[0]
    tm, tn = _tile(S, MM_ROWS), _tile(F, 512)

    def body(dx_ref, w_ref, gu_ref, o_ref):
        da = FFN_RES * _dot(dx_ref[...], w_ref[...], NT)
        g = gu_ref[0].astype(f32)
        u = gu_ref[1].astype(f32)
        sg = _sig(g)
        o_ref[0] = (da * u * (sg * (1.0 + g * (1.0 - sg)))).astype(bf16)
        o_ref[1] = (da * (g * sg)).astype(bf16)

    return pl.pallas_call(
        body, name=name, grid=(S // tm, F // tn),
        in_specs=[pl.BlockSpec((tm, D), lambda i, j: (i, 0)), pl.BlockSpec((tn, D), lambda i, j: (j, 0)),
                  pl.BlockSpec((2, tm, tn), lambda i, j: (0, i, j))],
        out_specs=pl.BlockSpec((2, tm, tn), lambda i, j: (0, i, j)),
        out_shape=jax.ShapeDtypeStruct((2, S, F), bf16),
        compiler_params=_params("parallel", "parallel"),
    )(dx, wout, gu)


def _rope(t, c, s1, s2):
    w = t.shape[1]
    return t * c + pltpu.roll(t, 8, 1) * s1 + pltpu.roll(t, w - 8, 1) * s2


def _rope_t(d, c, s1, s2):
    w = d.shape[1]
    return d * c + pltpu.roll(d * s1, w - 8, 1) + pltpu.roll(d * s2, 8, 1)


def _attn_mask(n):
    qi = lax.broadcasted_iota(jnp.int32, (BLK, 2 * BLK), 0)
    kj = lax.broadcasted_iota(jnp.int32, (BLK, 2 * BLK), 1)
    dist = qi + BLK - kj
    return (dist >= 0) & (dist < BLK) & ((kj >= BLK) | (n > 0))


def _softmax_sink(s, valid, sk):
    s = jnp.where(valid, s, -1e30)
    m = jnp.maximum(jnp.max(s, axis=-1, keepdims=True), sk)
    e = jnp.exp(s - m)
    es = jnp.exp(sk - m)
    inv = 1.0 / (jnp.sum(e, axis=-1, keepdims=True) + es)
    return e * inv, es * inv


def attn_fwd(p, rope_c, rope_s1, rope_s2, sinks, name):
    S = p.shape[0]
    nb = S // BLK
    kvb = Q_END // KV_W

    def body(sink_ref, q_ref, kvc_ref, kvp_ref, cc_ref, s1c_ref, s2c_ref, cp_ref, s1p_ref, s2p_ref, o_ref):
        n = pl.program_id(0)
        cc, s1c, s2c = cc_ref[...], s1c_ref[...], s2c_ref[...]
        cp, s1p, s2p = cp_ref[...], s1p_ref[...], s2p_ref[...]
        q = _rope(q_ref[...], jnp.tile(cc, (1, 8)), jnp.tile(s1c, (1, 8)), jnp.tile(s2c, (1, 8)))
        kc = _rope(kvc_ref[:, :256], jnp.tile(cc, (1, 2)), jnp.tile(s1c, (1, 2)), jnp.tile(s2c, (1, 2)))
        kp = _rope(kvp_ref[:, :256], jnp.tile(cp, (1, 2)), jnp.tile(s1p, (1, 2)), jnp.tile(s2p, (1, 2)))
        k = jnp.concatenate([kp, kc], axis=0).astype(bf16)
        v = jnp.concatenate([kvp_ref[:, 256:], kvc_ref[:, 256:]], axis=0).astype(bf16)
        q = q.astype(bf16)
        valid = _attn_mask(n)
        for h in range(N_KV_HEADS):
            kh = k[:, h * HEAD_DIM:(h + 1) * HEAD_DIM]
            vh = v[:, h * HEAD_DIM:(h + 1) * HEAD_DIM]
            for g in range(GQ):
                hq = h * GQ + g
                qh = q[:, hq * HEAD_DIM:(hq + 1) * HEAD_DIM]
                s = _dot(qh, kh, NT) * (HEAD_DIM ** -0.5)
                pr, _ = _softmax_sink(s, valid, sink_ref[hq])
                o = _dot(pr.astype(bf16), vh)
                o_ref[:, hq * HEAD_DIM:(hq + 1) * HEAD_DIM] = o.astype(bf16)

    tab_c = pl.BlockSpec((BLK, LANE), lambda n: (n, 0))
    tab_p = pl.BlockSpec((BLK, LANE), lambda n: (jnp.maximum(n - 1, 0), 0))
    return pl.pallas_call(
        body, name=name, grid=(nb,),
        in_specs=[pl.BlockSpec(memory_space=pltpu.SMEM),
                  pl.BlockSpec((BLK, Q_END), lambda n: (n, 0)),
                  pl.BlockSpec((BLK, KV_W), lambda n: (n, kvb)),
                  pl.BlockSpec((BLK, KV_W), lambda n: (jnp.maximum(n - 1, 0), kvb)),
                  tab_c, tab_c, tab_c, tab_p, tab_p, tab_p],
        out_specs=pl.BlockSpec((BLK, Q_END), lambda n: (n, 0)),
        out_shape=jax.ShapeDtypeStruct((S, Q_END), bf16),
        compiler_params=_params("parallel"),
    )(sinks, p, p, p, rope_c, rope_s1, rope_s2, rope_c, rope_s1, rope_s2)


def attn_bwd(p, dcat, rope_c, rope_s1, rope_s2, sinks, name):
    S = p.shape[0]
    nb = S // BLK
    kvb = Q_END // KV_W

    def body(sink_ref, q_ref, kvc_ref, kvp_ref, do_ref, cc_ref, s1c_ref, s2c_ref, cp_ref, s1p_ref, s2p_ref,
             dq_ref, dkv_ref, dsink_ref, carry, dq_scr, dkv_scr):
        n = pl.program_id(0)

        @pl.when(n == 0)
        def _():
            carry[...] = jnp.zeros_like(carry)
            dsink_ref[...] = jnp.zeros_like(dsink_ref)

        cp, s1p, s2p = cp_ref[...], s1p_ref[...], s2p_ref[...]
        cp2, s1p2, s2p2 = jnp.tile(cp, (1, 2)), jnp.tile(s1p, (1, 2)), jnp.tile(s2p, (1, 2))

        @pl.when(n < nb)
        def _():
            cc, s1c, s2c = cc_ref[...], s1c_ref[...], s2c_ref[...]
            cc8, s1c8, s2c8 = jnp.tile(cc, (1, 8)), jnp.tile(s1c, (1, 8)), jnp.tile(s2c, (1, 8))
            q = _rope(q_ref[...], cc8, s1c8, s2c8).astype(bf16)
            kc = _rope(kvc_ref[:, :256], jnp.tile(cc, (1, 2)), jnp.tile(s1c, (1, 2)), jnp.tile(s2c, (1, 2)))
            kp = _rope(kvp_ref[:, :256], cp2, s1p2, s2p2)
            k = jnp.concatenate([kp, kc], axis=0).astype(bf16)
            v = jnp.concatenate([kvp_ref[:, 256:], kvc_ref[:, 256:]], axis=0).astype(bf16)
            do = do_ref[...].astype(bf16)
            valid = _attn_mask(n)
            lane = lax.broadcasted_iota(jnp.int32, (1, LANE), 1)
            dsink = jnp.zeros((1, LANE), f32)
            for h in range(N_KV_HEADS):
                kh = k[:, h * HEAD_DIM:(h + 1) * HEAD_DIM]
                vh = v[:, h * HEAD_DIM:(h + 1) * HEAD_DIM]
                dkh = jnp.zeros((2 * BLK, HEAD_DIM), f32)
                dvh = jnp.zeros((2 * BLK, HEAD_DIM), f32)
                for g in range(GQ):
                    hq = h * GQ + g
                    qh = q[:, hq * HEAD_DIM:(hq + 1) * HEAD_DIM]
                    doh = do[:, hq * HEAD_DIM:(hq + 1) * HEAD_DIM]
                    s = _dot(qh, kh, NT) * (HEAD_DIM ** -0.5)
                    pr, ps = _softmax_sink(s, valid, sink_ref[hq])
                    dpr = _dot(doh, vh, NT)
                    dvh = dvh + _dot(pr.astype(bf16), doh, TN)
                    row = jnp.sum(pr * dpr, axis=-1, keepdims=True)
                    ds = (pr * (dpr - row) * (HEAD_DIM ** -0.5)).astype(bf16)
                    dsink = dsink + jnp.where(lane == hq, -jnp.sum(ps * row, axis=0, keepdims=True), 0.0)
                    dq_scr[:, hq * HEAD_DIM:(hq + 1) * HEAD_DIM] = _dot(ds, kh)
                    dkh = dkh + _dot(ds, qh, TN)
                dkv_scr[:, h * HEAD_DIM:(h + 1) * HEAD_DIM] = dkh
                dkv_scr[:, 256 + h * HEAD_DIM:256 + (h + 1) * HEAD_DIM] = dvh
            dsink_ref[...] += dsink
            dq_ref[...] = _rope_t(dq_scr[...], cc8, s1c8, s2c8).astype(bf16)

        prev = carry[...]

        @pl.when(n < nb)
        def _():
            dkv_scr[pl.ds(0, BLK), :] = dkv_scr[pl.ds(0, BLK), :] + prev

        @pl.when(n == nb)
        def _():
            dkv_scr[pl.ds(0, BLK), :] = prev

        done = dkv_scr[pl.ds(0, BLK), :]
        dkv_ref[:, :256] = _rope_t(done[:, :256], cp2, s1p2, s2p2).astype(bf16)
        dkv_ref[:, 256:] = done[:, 256:].astype(bf16)

        @pl.when(n < nb)
        def _():
            carry[...] = dkv_scr[pl.ds(BLK, BLK), :]

    cur = lambda n: jnp.minimum(n, nb - 1)
    prv = lambda n: jnp.maximum(n - 1, 0)
    tab_c = pl.BlockSpec((BLK, LANE), lambda n: (cur(n), 0))
    tab_p = pl.BlockSpec((BLK, LANE), lambda n: (prv(n), 0))
    return pl.pallas_call(
        body, name=name, grid=(nb + 1,),
        in_specs=[pl.BlockSpec(memory_space=pltpu.SMEM),
                  pl.BlockSpec((BLK, Q_END), lambda n: (cur(n), 0)),
                  pl.BlockSpec((BLK, KV_W), lambda n: (cur(n), kvb)),
                  pl.BlockSpec((BLK, KV_W), lambda n: (prv(n), kvb)),
                  pl.BlockSpec((BLK, Q_END), lambda n: (cur(n), 0)),
                  tab_c, tab_c, tab_c, tab_p, tab_p, tab_p],
        out_specs=[pl.BlockSpec((BLK, Q_END), lambda n: (cur(n), 0)),
                   pl.BlockSpec((BLK, KV_W), lambda n: (prv(n), 0)),
                   pl.BlockSpec((1, LANE), lambda n: (0, 0))],
        out_shape=[jax.ShapeDtypeStruct((S, Q_END), bf16), jax.ShapeDtypeStruct((S, KV_W), bf16),
                   jax.ShapeDtypeStruct((1, LANE), f32)],
        scratch_shapes=[pltpu.VMEM((BLK, KV_W), f32), pltpu.VMEM((BLK, Q_END), f32), pltpu.VMEM((2 * BLK, KV_W), f32)],
        compiler_params=_params("arbitrary"),
    )(sinks, p, p, p, dcat, rope_c, rope_s1, rope_s2, rope_c, rope_s1, rope_s2)


CONV_ROWS = 32
A1_BLK = V_END // CONV_CH
A2_BLK = A1_BLK + 1


def _ln_stats(y):
    mu = jnp.mean(y, axis=-1, keepdims=True)
    xc = y - mu
    rstd = lax.rsqrt(jnp.mean(xc * xc, axis=-1, keepdims=True) + NORM_EPS)
    return xc * rstd, rstd


def conv_fwd(p, w, b, lng, lnb, name):
    S = p.shape[0]
    T = _tile(S, 256)
    r = T // HALO

    def body(a1_ref, a2_ref, h1_ref, h2_ref, w_ref, b_ref, g_ref, bb_ref, o_ref, y_ref, scr):
        i = pl.program_id(0)
        halo = h1_ref[...] * _sig(h2_ref[...])
        scr[pl.ds(0, HALO), :] = jnp.where(i > 0, halo, 0.0)
        scr[pl.ds(HALO, T), :] = a1_ref[...] * _sig(a2_ref[...])
        acc = jnp.zeros((T, CONV_CH), f32) + b_ref[...]
        for j in range(CONV_W):
            acc = acc + scr[pl.ds(HALO - (CONV_W - 1) + j, T), :] * w_ref[j:j + 1, :]
        y_ref[...] = acc
        yh, _ = _ln_stats(acc)
        z = yh * g_ref[...] + bb_ref[...]
        o_ref[...] = (z * _sig(z)).astype(bf16)

    vec = pl.BlockSpec((1, CONV_CH), lambda i: (0, 0))
    halo_map = lambda i: jnp.maximum(i * r - 1, 0)
    return pl.pallas_call(
        body, name=name, grid=(S // T,),
        in_specs=[pl.BlockSpec((T, CONV_CH), lambda i: (i, A1_BLK)), pl.BlockSpec((T, CONV_CH), lambda i: (i, A2_BLK)),
                  pl.BlockSpec((HALO, CONV_CH), lambda i: (halo_map(i), A1_BLK)),
                  pl.BlockSpec((HALO, CONV_CH), lambda i: (halo_map(i), A2_BLK)),
                  pl.BlockSpec((HALO, CONV_CH), lambda i: (0, 0)), vec, vec, vec],
        out_specs=[pl.BlockSpec((T, CONV_CH), lambda i: (i, 0)), pl.BlockSpec((T, CONV_CH), lambda i: (i, 0))],
        out_shape=[jax.ShapeDtypeStruct((S, CONV_CH), bf16), jax.ShapeDtypeStruct((S, CONV_CH), f32)],
        scratch_shapes=[pltpu.VMEM((T + HALO, CONV_CH), f32)],
        compiler_params=_params("parallel"),
    )(p, p, p, p, w, b, lng, lnb)


def conv_bwd(p, y, dcat, w, lng, lnb, name):
    S = p.shape[0]
    T = _tile(S, 256)
    n = S // T
    r = T // HALO
    dcb = Q_END // CONV_CH
    rc = _tile(T, CONV_ROWS)

    def body(a1_ref, a2_ref, h1_ref, h2_ref, y_ref, yn_ref, do_ref, don_ref, w_ref, g_ref, bb_ref,
             da_ref, dw_ref, db_ref, dg_ref, dbb_ref, scr_h, scr_dy, acc_b, acc_g, acc_bb):
        i = pl.program_id(0)

        @pl.when(i == 0)
        def _():
            dw_ref[...] = jnp.zeros_like(dw_ref)
            acc_b[...] = jnp.zeros_like(acc_b)
            acc_g[...] = jnp.zeros_like(acc_g)
            acc_bb[...] = jnp.zeros_like(acc_bb)

        gv, bv = g_ref[...], bb_ref[...]

        def ln_silu_bwd(yv, dout):
            yh, rstd = _ln_stats(yv)
            z = yh * gv + bv
            sg = _sig(z)
            dz = dout * (sg * (1.0 + z * (1.0 - sg)))
            gz = dz * gv
            dy = rstd * (gz - jnp.mean(gz, axis=-1, keepdims=True) - yh * jnp.mean(gz * yh, axis=-1, keepdims=True))
            return dy, dz, yh

        chunks = [pl.ds(c * rc, rc) for c in range(T // rc)]
        dyn, _, _ = ln_silu_bwd(yn_ref[...], don_ref[...])
        scr_dy[pl.ds(T, HALO), :] = jnp.where(i < n - 1, dyn, 0.0)
        halo = h1_ref[...] * _sig(h2_ref[...])
        scr_h[pl.ds(0, HALO), :] = jnp.where(i > 0, halo, 0.0)
        for c, rows in enumerate(chunks):
            dy, dz, yh = ln_silu_bwd(y_ref[rows, :], do_ref[rows, :])
            acc_g[...] += _rowsum8(dz * yh)
            acc_bb[...] += _rowsum8(dz)
            acc_b[...] += _rowsum8(dy)
            scr_dy[rows, :] = dy
            scr_h[pl.ds(HALO + c * rc, rc), :] = a1_ref[rows, :] * _sig(a2_ref[rows, :])
        for c, rows in enumerate(chunks):
            dh = jnp.zeros((rc, CONV_CH), f32)
            for j in range(CONV_W):
                dh = dh + scr_dy[pl.ds(c * rc + CONV_W - 1 - j, rc), :] * w_ref[j:j + 1, :]
            a1 = a1_ref[rows, :]
            sg2 = _sig(a2_ref[rows, :])
            da_ref[rows, :CONV_CH] = (dh * sg2).astype(bf16)
            da_ref[rows, CONV_CH:] = (dh * a1 * sg2 * (1.0 - sg2)).astype(bf16)
        for j in range(CONV_W):
            part = jnp.zeros((8, CONV_CH), f32)
            for c, rows in enumerate(chunks):
                part = part + _rowsum8(scr_dy[rows, :] * scr_h[pl.ds(c * rc + HALO - (CONV_W - 1) + j, rc), :])
            dw_ref[j:j + 1, :] += jnp.sum(part, axis=0, keepdims=True)

        @pl.when(i == n - 1)
        def _():
            db_ref[...] = jnp.sum(acc_b[...], axis=0, keepdims=True)
            dg_ref[...] = jnp.sum(acc_g[...], axis=0, keepdims=True)
            dbb_ref[...] = jnp.sum(acc_bb[...], axis=0, keepdims=True)

    vec = pl.BlockSpec((1, CONV_CH), lambda i: (0, 0))
    tap = pl.BlockSpec((HALO, CONV_CH), lambda i: (0, 0))
    prev_map = lambda i: jnp.maximum(i * r - 1, 0)
    next_map = lambda i: jnp.minimum((i + 1) * r, S // HALO - 1)
    return pl.pallas_call(
        body, name=name, grid=(n,),
        in_specs=[pl.BlockSpec((T, CONV_CH), lambda i: (i, A1_BLK)), pl.BlockSpec((T, CONV_CH), lambda i: (i, A2_BLK)),
                  pl.BlockSpec((HALO, CONV_CH), lambda i: (prev_map(i), A1_BLK)),
                  pl.BlockSpec((HALO, CONV_CH), lambda i: (prev_map(i), A2_BLK)),
                  pl.BlockSpec((T, CONV_CH), lambda i: (i, 0)),
                  pl.BlockSpec((HALO, CONV_CH), lambda i: (next_map(i), 0)),
                  pl.BlockSpec((T, CONV_CH), lambda i: (i, dcb)),
                  pl.BlockSpec((HALO, CONV_CH), lambda i: (next_map(i), dcb)),
                  tap, vec, vec],
        out_specs=[pl.BlockSpec((T, 2 * CONV_CH), lambda i: (i, 0)), tap, vec, vec, vec],
        out_shape=[jax.ShapeDtypeStruct((S, 2 * CONV_CH), bf16), jax.ShapeDtypeStruct((HALO, CONV_CH), f32),
                   jax.ShapeDtypeStruct((1, CONV_CH), f32), jax.ShapeDtypeStruct((1, CONV_CH), f32),
                   jax.ShapeDtypeStruct((1, CONV_CH), f32)],
        scratch_shapes=[pltpu.VMEM((T + HALO, CONV_CH), f32), pltpu.VMEM((T + HALO, CONV_CH), f32),
                        pltpu.VMEM((8, CONV_CH), f32), pltpu.VMEM((8, CONV_CH), f32), pltpu.VMEM((8, CONV_CH), f32)],
        compiler_params=_params("arbitrary"),
    )(p, p, p, p, y, y, dcat, dcat, w, lng, lnb)


U_BLK = (V_END + 2 * CONV_CH) // SGU_CH
SV_BLK = U_BLK + 1


def _tril(w, transposed=False):
    row = lax.broadcasted_iota(jnp.int32, (BLK, BLK), 0)
    col = lax.broadcasted_iota(jnp.int32, (BLK, BLK), 1)
    keep = (col >= row) if transposed else (row >= col)
    return jnp.where(keep, w, 0.0)


def sgu_fwd(p, lng, lnb, w, bias, name):
    S = p.shape[0]
    T = _tile(S, 256)

    def body(u_ref, v_ref, g_ref, bb_ref, w_ref, bias_ref, o_ref):
        yh, _ = _ln_stats(v_ref[...])
        v = (yh * g_ref[...] + bb_ref[...]).astype(bf16)
        low = lax.broadcasted_iota(jnp.int32, (BLK, LANE), 1) < HEAD_DIM
        for pr in range(SGU_HEADS // 2):
            lanes = pl.ds(pr * LANE, LANE)
            w0 = _tril(w_ref[2 * pr]).astype(bf16)
            w1 = _tril(w_ref[2 * pr + 1]).astype(bf16)
            for c in range(T // BLK):
                rows = pl.ds(c * BLK, BLK)
                vp = v[c * BLK:(c + 1) * BLK, pr * LANE:(pr + 1) * LANE]
                mixed = jnp.where(low, _dot(w0, vp), _dot(w1, vp)) + bias_ref[:, lanes]
                o_ref[rows, lanes] = (u_ref[rows, lanes] * mixed).astype(bf16)

    vec = pl.BlockSpec((1, SGU_CH), lambda i: (0, 0))
    return pl.pallas_call(
        body, name=name, grid=(S // T,),
        in_specs=[pl.BlockSpec((T, SGU_CH), lambda i: (i, U_BLK)), pl.BlockSpec((T, SGU_CH), lambda i: (i, SV_BLK)),
                  vec, vec, pl.BlockSpec((SGU_HEADS, BLK, BLK), lambda i: (0, 0, 0)),
                  pl.BlockSpec((BLK, SGU_CH), lambda i: (0, 0))],
        out_specs=pl.BlockSpec((T, SGU_CH), lambda i: (i, 0)),
        out_shape=jax.ShapeDtypeStruct((S, SGU_CH), bf16),
        compiler_params=_params("parallel"),
    )(p, p, lng, lnb, w, bias)


def sgu_bwd(p, dcat, lng, lnb, w, wt, bias, name):
    S = p.shape[0]
    T = _tile(S, 256)
    n = S // T
    dsb = (Q_END + CONV_CH) // SGU_CH

    def body(u_ref, v_ref, do_ref, g_ref, bb_ref, w_ref, wt_ref, bias_ref,
             da_ref, dw_ref, db_ref, dg_ref, dbb_ref, dv_scr, acc_bias, acc_g, acc_bb):
        i = pl.program_id(0)

        @pl.when(i == 0)
        def _():
            dw_ref[...] = jnp.zeros_like(dw_ref)
            acc_bias[...] = jnp.zeros_like(acc_bias)
            acc_g[...] = jnp.zeros_like(acc_g)
            acc_bb[...] = jnp.zeros_like(acc_bb)

        gv = g_ref[...]
        yh, rstd = _ln_stats(v_ref[...])
        v = (yh * gv + bb_ref[...]).astype(bf16)
        low = lax.broadcasted_iota(jnp.int32, (BLK, LANE), 1) < HEAD_DIM
        for pr in range(SGU_HEADS // 2):
            lanes = pl.ds(pr * LANE, LANE)
            w0 = _tril(w_ref[2 * pr]).astype(bf16)
            w1 = _tril(w_ref[2 * pr + 1]).astype(bf16)
            wt0 = _tril(wt_ref[2 * pr], True).astype(bf16)
            wt1 = _tril(wt_ref[2 * pr + 1], True).astype(bf16)
            dw0 = jnp.zeros((BLK, BLK), f32)
            dw1 = jnp.zeros((BLK, BLK), f32)
            for c in range(T // BLK):
                rows = pl.ds(c * BLK, BLK)
                vp = v[c * BLK:(c + 1) * BLK, pr * LANE:(pr + 1) * LANE]
                mixed = jnp.where(low, _dot(w0, vp), _dot(w1, vp)) + bias_ref[:, lanes]
                do = do_ref[rows, lanes]
                da_ref[rows, lanes] = (do * mixed).astype(bf16)
                dm = do * u_ref[rows, lanes]
                acc_bias[:, lanes] += dm
                dmb = dm.astype(bf16)
                dv_scr[rows, lanes] = jnp.where(low, _dot(wt0, dmb), _dot(wt1, dmb))
                zero = jnp.zeros_like(dmb)
                dw0 = dw0 + _dot(jnp.where(low, dmb, zero), vp, NT)
                dw1 = dw1 + _dot(jnp.where(low, zero, dmb), vp, NT)
            dw_ref[2 * pr] += _tril(dw0)
            dw_ref[2 * pr + 1] += _tril(dw1)
        dv = dv_scr[...]
        acc_g[...] += _rowsum8(dv * yh)
        acc_bb[...] += _rowsum8(dv)
        gz = dv * gv
        dvr = rstd * (gz - jnp.mean(gz, axis=-1, keepdims=True) - yh * jnp.mean(gz * yh, axis=-1, keepdims=True))
        da_ref[:, SGU_CH:] = dvr.astype(bf16)

        @pl.when(i == n - 1)
        def _():
            ch = lax.broadcasted_iota(jnp.int32, (SGU_CH, LANE), 0) // HEAD_DIM
            hd = lax.broadcasted_iota(jnp.int32, (SGU_CH, LANE), 1)
            fold = jnp.where(ch == hd, 1.0, 0.0).astype(f32)
            db_ref[...] = jnp.dot(acc_bias[...], fold, preferred_element_type=f32, precision=lax.Precision.HIGHEST)
            dg_ref[...] = jnp.sum(acc_g[...], axis=0, keepdims=True)
            dbb_ref[...] = jnp.sum(acc_bb[...], axis=0, keepdims=True)

    vec = pl.BlockSpec((1, SGU_CH), lambda i: (0, 0))
    wsp = pl.BlockSpec((SGU_HEADS, BLK, BLK), lambda i: (0, 0, 0))
    return pl.pallas_call(
        body, name=name, grid=(n,),
        in_specs=[pl.BlockSpec((T, SGU_CH), lambda i: (i, U_BLK)), pl.BlockSpec((T, SGU_CH), lambda i: (i, SV_BLK)),
                  pl.BlockSpec((T, SGU_CH), lambda i: (i, dsb)), vec, vec, wsp, wsp,
                  pl.BlockSpec((BLK, SGU_CH), lambda i: (0, 0))],
        out_specs=[pl.BlockSpec((T, 2 * SGU_CH), lambda i: (i, 0)), wsp,
                   pl.BlockSpec((BLK, LANE), lambda i: (0, 0)), vec, vec],
        out_shape=[jax.ShapeDtypeStruct((S, 2 * SGU_CH), bf16), jax.ShapeDtypeStruct((SGU_HEADS, BLK, BLK), f32),
                   jax.ShapeDtypeStruct((BLK, LANE), f32), jax.ShapeDtypeStruct((1, SGU_CH), f32),
                   jax.ShapeDtypeStruct((1, SGU_CH), f32)],
        scratch_shapes=[pltpu.VMEM((T, SGU_CH), f32), pltpu.VMEM((BLK, SGU_CH), f32),
                        pltpu.VMEM((8, SGU_CH), f32), pltpu.VMEM((8, SGU_CH), f32)],
        compiler_params=_params("arbitrary"),
    )(p, p, dcat, lng, lnb, w, wt, bias)


HBM = pl.BlockSpec(memory_space=pltpu.HBM)
SEM = pl.BlockSpec(memory_space=pltpu.SEMAPHORE)
ANY = pl.BlockSpec(memory_space=pl.ANY)
EFFECT = pltpu.SideEffectType.DATAFLOW_SIDE_EFFECTING


def _flip(x, y, c, k):
    px, py, pc = x ^ (k >> 2), y ^ ((k >> 1) & 1), c ^ (k & 1)
    return (px, py, pc), 4 * px + 2 * py + pc


def _routes_gather(x, y, c):
    me = 4 * x + 2 * y + c
    out = []
    for k in (1, 2, 4, 6):
        dev, idx = _flip(x, y, c, k)
        out.append((dev, None, me, idx))
    return out


def _routes_pair(x, y, c):
    dev, _ = _flip(x, y, c, 1)
    return [(dev, 2 * q + (1 - c), q, q) for q in range(N_DEV // 2)]


def _routes_chips(x, y, c):
    out = []
    for k in (2, 4, 6):
        dev, idx = _flip(x, y, c, k)
        out.append((dev, idx // 2, 2 * x + y, idx // 2))
    return out


def _routes_forward(x, y, c):
    sib, _ = _flip(x, y, c, 1)
    out = []
    for k in (2, 4, 6):
        _, idx = _flip(x, y, c, k)
        out.append((sib, idx, idx, idx ^ 1))
    return out


def _routes_all(x, y, c):
    me = 4 * x + 2 * y + c
    out = []
    for k in range(1, N_DEV):
        dev, idx = _flip(x, y, c, k)
        out.append((dev, None, me, idx))
    return out


def _slot(ref, slot, kind):
    if slot is None:
        return ref
    if kind == "cols":
        width = ref.shape[2] // (N_DEV // 2)
        return ref.at[slot // (N_DEV // 2), :, pl.ds(pl.multiple_of((slot % (N_DEV // 2)) * width, LANE), width)]
    return ref.at[slot]


def _copies(routes, srcs, lands, send_sems, recv_sems, incoming, src_kinds, land_kinds):
    x, y, c = lax.axis_index("x"), lax.axis_index("y"), lax.axis_index("c")
    out = []
    n = len(lands)
    if srcs is None:
        srcs, src_kinds = lands, land_kinds
    for k, (dev, src_slot, dst_slot, recv_slot) in enumerate(routes(x, y, c)):
        for a in range(n):
            out.append(pltpu.make_async_remote_copy(
                src_ref=_slot(srcs[a], src_slot, src_kinds[a]),
                dst_ref=_slot(lands[a], recv_slot if incoming else dst_slot, land_kinds[a]),
                send_sem=send_sems.at[k * n + a], recv_sem=recv_sems.at[k * n + a], device_id=dev, device_id_type=MESH))
    return out


def _pin(a):
    return pltpu.with_memory_space_constraint(a, pltpu.HBM)


def split_start(srcs, lands, routes, name, deps=(), src_kinds=None, land_kinds=None):
    n = len(lands)
    ns = 0 if srcs is None else n
    n_routes = len(routes(0, 0, 0))
    ops = ([] if srcs is None else list(srcs)) + list(lands)
    src_kinds = src_kinds or ["rows"] * n
    land_kinds = land_kinds or ["rows"] * n

    def body(*refs):
        src, land = (refs[:n] if ns else None), refs[ns:ns + n]
        first_out = ns + n + len(deps)
        send_sems, recv_sems, token = refs[first_out], refs[first_out + 1], refs[-1]
        for cp in _copies(routes, src, land, send_sems, recv_sems, False, src_kinds, land_kinds):
            cp.start()
        token[...] = jnp.zeros_like(token)

    thru = [pltpu.HBM(a.shape, a.dtype) for a in ops]
    res = pl.pallas_call(
        body, name=name,
        out_shape=(pltpu.SemaphoreType.DMA((n * n_routes,)), pltpu.SemaphoreType.DMA((n * n_routes,)), *thru,
                   jax.ShapeDtypeStruct((8, LANE), f32)),
        in_specs=[HBM] * len(ops) + [ANY] * len(deps),
        out_specs=(SEM, SEM, *([HBM] * len(ops)), pl.BlockSpec(memory_space=pltpu.VMEM)),
        input_output_aliases={i: 2 + i for i in range(len(ops))},
        compiler_params=pltpu.CompilerParams(has_side_effects=EFFECT),
    )(*[_pin(a) for a in ops], *deps)
    return (res[0], res[1], (list(res[2:2 + n]) if ns else None), list(res[2 + ns:2 + ns + n]), res[-1],
            (src_kinds, land_kinds))


def split_wait(started, after, routes, name):
    send_sems, recv_sems, srcs, lands, _, (src_kinds, land_kinds) = started
    n = len(lands)
    ns = 0 if srcs is None else n
    ops = ([] if srcs is None else list(srcs)) + list(lands)
    afters = list(after) if isinstance(after, (list, tuple)) else [after]

    def body(*refs):
        src, land = (refs[:n] if ns else None), refs[ns:ns + n]
        send_s, recv_s = refs[ns + n], refs[ns + n + 1]
        for cp in _copies(routes, src, land, send_s, recv_s, True, src_kinds, land_kinds):
            cp.wait_send()
            cp.wait_recv()

    thru = [pltpu.HBM(a.shape, a.dtype) for a in ops]
    res = pl.pallas_call(
        body, name=name, out_shape=tuple(thru),
        in_specs=[HBM] * len(ops) + [SEM, SEM] + [ANY] * len(afters), out_specs=tuple([HBM] * len(ops)),
        input_output_aliases={i: i for i in range(len(ops))},
        compiler_params=pltpu.CompilerParams(has_side_effects=EFFECT),
    )(*ops, send_sems, recv_sems, *afters)
    return (list(res[:n]) if ns else None), list(res[ns:ns + n])


def chip_sum(parts, land, c_idx, kind, name):
    _, R, C = land.shape
    tr = R if R * C * 2 <= 3 * 2 ** 20 else _tile(R, 512)
    half = N_DEV // 2

    def body(c_ref, p_ref, l_ref, o_ref):
        o_ref[...] = (p_ref[...].astype(f32) + l_ref[...].astype(f32)).astype(bf16)

    if kind == "cols":
        mine = lambda q, i, c_ref: ((2 * q + c_ref[0]) // half, i, (2 * q + c_ref[0]) % half)
    else:
        mine = lambda q, i, c_ref: (2 * q + c_ref[0], i, 0)
    return pl.pallas_call(
        body, name=name,
        grid_spec=pltpu.PrefetchScalarGridSpec(
            num_scalar_prefetch=1, grid=(half, R // tr),
            in_specs=[pl.BlockSpec((None, tr, C), mine), pl.BlockSpec((None, tr, C), lambda q, i, c_ref: (q, i, 0))],
            out_specs=pl.BlockSpec((None, tr, C), lambda q, i, c_ref: (q, i, 0))),
        out_shape=jax.ShapeDtypeStruct((half, R, C), bf16),
        compiler_params=_params("parallel", "parallel"),
    )(c_idx, parts, land)


def place_own(land, src, me_idx, kind, name):
    R, C = src.shape
    tr = _tile(R, 512)
    half = N_DEV // 2
    if kind == "cols":
        where = lambda i, m: (m[0] // half, i, m[0] % half)
    else:
        where = lambda i, m: (m[0], i, 0)

    def body(m_ref, land_ref, src_ref, out_ref):
        out_ref[...] = src_ref[...]

    return pl.pallas_call(
        body, name=name,
        grid_spec=pltpu.PrefetchScalarGridSpec(
            num_scalar_prefetch=1, grid=(R // tr,),
            in_specs=[ANY, pl.BlockSpec((tr, C), lambda i, m: (i, 0))],
            out_specs=pl.BlockSpec((None, tr, C), where)),
        out_shape=jax.ShapeDtypeStruct(land.shape, land.dtype),
        input_output_aliases={1: 0},
        compiler_params=_params("arbitrary"),
    )(me_idx, land, src)


def sum_slots(parts, name):
    P, R, C = parts.shape
    tr = _tile(R, 512)

    def body(p_ref, o_ref):
        total = p_ref[0]
        for j in range(1, P):
            total = total + p_ref[j]
        o_ref[...] = total

    return pl.pallas_call(
        body, name=name, grid=(R // tr,),
        in_specs=[pl.BlockSpec((P, tr, C), lambda i: (0, i, 0))],
        out_specs=pl.BlockSpec((tr, C), lambda i: (i, 0)),
        out_shape=jax.ShapeDtypeStruct((R, C), f32),
        compiler_params=_params("parallel"),
    )(parts)


def adamw(parts, owns, chip, w, m, v, name):
    L, R, C = w.shape
    P = parts[0].shape[0]
    tr = _tile(R, 128 if C > 1024 else 256)
    nr = R // tr
    c1 = 1.0 - ADAM_B1 ** ADAM_STEP
    c2 = 1.0 - ADAM_B2 ** ADAM_STEP
    n_own = L if owns is not None else 0

    def body(chip_ref, *refs):
        part_refs, own_refs = refs[:L], refs[L:L + n_own]
        w_ref, m_ref, v_ref, g_out, d_out, m_out, v_out = refs[L + n_own:]
        layer = pl.program_id(0)
        for l in range(L):
            @pl.when(layer == l)
            def _(l=l):
                g = None
                for q in range(P):
                    term = part_refs[l][q].astype(f32)
                    if n_own:
                        term = jnp.where(chip_ref[0] == q, own_refs[l][...].astype(f32), term)
                    g = term if g is None else g + term
                mn = ADAM_B1 * m_ref[...] + (1.0 - ADAM_B1) * g
                vn = ADAM_B2 * v_ref[...] + (1.0 - ADAM_B2) * (g * g)
                g_out[...] = g
                m_out[...] = mn
                v_out[...] = vn
                d_out[...] = -ADAM_LR * ((mn / c1) / (jnp.sqrt(vn / c2) + ADAM_EPS) + ADAM_WD * w_ref[...])

    def rows(l, a, i):
        return jnp.where(a == l, i, jnp.where(a < l, 0, nr - 1))

    def part_spec(l):
        return pl.BlockSpec((P, tr, C), lambda a, i, chip_ref: (0, rows(l, a, i), 0))

    def own_spec(l):
        return pl.BlockSpec((None, tr, C), lambda a, i, chip_ref: (chip_ref[0], rows(l, a, i), 0))

    slab = pl.BlockSpec((None, tr, C), lambda a, i, chip_ref: (a, i, 0))
    out = jax.ShapeDtypeStruct((L, R, C), f32)
    return pl.pallas_call(
        body, name=name,
        grid_spec=pltpu.PrefetchScalarGridSpec(
            num_scalar_prefetch=1, grid=(L, nr),
            in_specs=[part_spec(l) for l in range(L)] + [own_spec(l) for l in range(n_own)] + [slab, slab, slab],
            out_specs=[slab, slab, slab, slab]),
        out_shape=[out, out, out, out],
        compiler_params=_params("arbitrary", "arbitrary"),
    )(chip, *parts, *(owns or []), w, m, v)


PACK = 8 * LANE
PACK_ROWS = 256


def _pack(arrs):
    pieces = []
    for a in arrs:
        flat = a.astype(f32).reshape(-1)
        pad = (-flat.shape[0]) % PACK
        pieces.append(jnp.pad(flat, (0, pad)).reshape(-1, LANE))
    rows = sum(p.shape[0] for p in pieces)
    if rows > PACK_ROWS and rows % PACK_ROWS:
        pieces.append(jnp.zeros((PACK_ROWS - rows % PACK_ROWS, LANE), f32))
    return jnp.concatenate(pieces, axis=0)


def _unpack(buf, shapes):
    out, row = [], 0
    for shp in shapes:
        size = math.prod(shp)
        rows = (size + PACK - 1) // PACK * (PACK // LANE)
        out.append(buf[row:row + rows].reshape(-1)[:size].reshape(shp))
        row += rows
    return out


def _rope_tables(positions):
    half = ROT_DIM // 2
    inv_freq = 1.0 / (ROPE_THETA ** (jnp.arange(0, ROT_DIM, 2, dtype=f32) / ROT_DIM))
    ang = positions.astype(f32)[:, None] * inv_freq
    cos, sin = jnp.cos(ang), jnp.sin(ang)
    S = positions.shape[0]
    zeros, ones = jnp.zeros((S, half), f32), jnp.ones((S, HEAD_DIM - ROT_DIM), f32)
    rest = jnp.zeros((S, HEAD_DIM - ROT_DIM), f32)
    c = jnp.concatenate([cos, cos, ones], axis=1)
    s1 = jnp.concatenate([zeros, sin, rest], axis=1)
    s2 = jnp.concatenate([-sin, zeros, rest], axis=1)
    return tuple(jnp.tile(t, (1, LANE // HEAD_DIM)) for t in (c, s1, s2))


def _cols_to_shards(g):
    lead, (R, N) = g.shape[:-2], g.shape[-2:]
    g = g.reshape(lead + (R, N_DEV, N // N_DEV))
    return jnp.moveaxis(g, -2, 0)


def _shards_to_cols(g):
    g = jnp.moveaxis(g, 0, -2)
    return g.reshape(g.shape[:-2] + (g.shape[-2] * g.shape[-1],))


def kernel(x, positions, norm_ffn1, ffn1_w_in, ffn1_w_out, norm_mix, w_in, conv_dw_w, conv_dw_b, conv_ln_g, conv_ln_b, sgu_ln_g, sgu_ln_b, sgu_w, sgu_b, attn_sinks, w_out, norm_ffn2, ffn2_w_in, ffn2_w_out, final_norm, loss_target, m_norm_ffn1, m_ffn1_w_in, m_ffn1_w_out, m_norm_mix, m_w_in, m_conv_dw_w, m_conv_dw_b, m_conv_ln_g, m_conv_ln_b, m_sgu_ln_g, m_sgu_ln_b, m_sgu_w, m_sgu_b, m_attn_sinks, m_w_out, m_norm_ffn2, m_ffn2_w_in, m_ffn2_w_out, m_final_norm, v_norm_ffn1, v_ffn1_w_in, v_ffn1_w_out, v_norm_mix, v_w_in, v_conv_dw_w, v_conv_dw_b, v_conv_ln_g, v_conv_ln_b, v_sgu_ln_g, v_sgu_ln_b, v_sgu_w, v_sgu_b, v_attn_sinks, v_w_out, v_norm_ffn2, v_ffn2_w_in, v_ffn2_w_out, v_final_norm):
    L = norm_ffn1.shape[0]
    S, D = x.shape[1], x.shape[2]
    F = ffn1_w_out.shape[1] * N_DEV
    me = 4 * lax.axis_index("x") + 2 * lax.axis_index("y") + lax.axis_index("c")
    x0 = x[0]
    rope_c, rope_s1, rope_s2 = _rope_tables(positions[0])
    cw = CONV_CH // N_DEV

    c_idx = lax.axis_index("c").astype(jnp.int32).reshape(1)
    chip = (2 * lax.axis_index("x") + lax.axis_index("y")).astype(jnp.int32).reshape(1)
    no_chip = jnp.zeros((1,), jnp.int32)
    me_idx = me.astype(jnp.int32).reshape(1)

    row = lambda a: a.reshape(1, -1)
    order_fwd = [(l, g) for l in range(L) for g in (("ffn1_in", "ffn1_out") if l == 0 else ("ffn1",)) + ("mix", "ffn2")]

    def group_srcs(l, grp):
        if grp == "mix":
            taps = jnp.pad(conv_dw_w[l], ((0, HALO - CONV_W), (0, LANE - cw)))
            return [w_in[l].astype(bf16), w_out[l].astype(bf16), taps]
        both = ([ffn2_w_in[l], ffn2_w_out[l]] if grp == "ffn2" else [ffn1_w_in[l], ffn1_w_out[l]])
        both = [a.astype(bf16) for a in both]
        return both[:1] if grp == "ffn1_in" else both[1:] if grp == "ffn1_out" else both

    def kinds_of(grp):
        return {"mix": ["rows"] * 3, "ffn1_in": ["cols"], "ffn1_out": ["rows"]}.get(grp, ["cols", "rows"])

    def gather_start(k, deps=()):
        l, grp = order_fwd[k]
        srcs = group_srcs(l, grp)
        lands = [lax.empty((2, D, F) if kind == "cols" else (N_DEV,) + a.shape, a.dtype)
                 for a, kind in zip(srcs, kinds_of(grp))]
        return split_start(srcs, lands, _routes_gather, f"gather_start_{grp}_{l}", deps, land_kinds=kinds_of(grp))

    def gather_forward(k, started, after):
        l, grp = order_fwd[k]
        srcs, lands = split_wait(started, after, _routes_gather, f"gather_wait_{grp}_{l}")
        return srcs, split_start(None, lands, _routes_forward, f"forward_start_{grp}_{l}", land_kinds=kinds_of(grp))

    def gather_finish(k, srcs, started, after):
        l, grp = order_fwd[k]
        _, lands = split_wait(started, after, _routes_forward, f"forward_wait_{grp}_{l}")
        full = [place_own(ld, s, me_idx, kind, f"own_{grp}_{l}_{a}")
                for a, (ld, s, kind) in enumerate(zip(lands, srcs, kinds_of(grp)))]
        if grp == "mix":
            return dict(w_in=_shards_to_cols(full[0]), w_out=full[1].reshape(D, D),
                        taps=_shards_to_cols(full[2][:, :, :cw]))
        named = dict(zip(["w_in", "w_out"] if len(full) == 2 else ["w_in" if grp == "ffn1_in" else "w_out"], full))
        if "w_out" in named:
            named["w_out"] = named["w_out"].reshape(F, D)
        return named

    def ffn_head(xs, wts, g_norm, tag, sv):
        sv["x_in"] = xs
        h, sv["ht"] = rmsnorm_fwd(xs, g_norm, f"norm_{tag}")
        sv["gu"], a, sv["at"] = ffn_in(h, wts["w_in"], f"{tag}_in")
        return a

    def ffn_fwd(xs, wts, g_norm, tag, sv, mid):
        a = ffn_head(xs, wts, g_norm, tag, sv)
        return mm_res(a, wts["w_out"], xs, FFN_RES, f"{tag}_out", deps=mid(a))

    def mix_fwd(xs, wts, l, sv, mid):
        sv["x_in"] = xs
        h, sv["ht"] = rmsnorm_fwd(xs, row(norm_mix[l]), f"norm_mix_{l}")
        p = mm_nn(h, wts["w_in"][None], f"mix_in_{l}")[0]
        sv["p"] = p
        attn = attn_fwd(p, rope_c, rope_s1, rope_s2, attn_sinks[l], f"attn_fwd_{l}")
        conv, sv["conv_y"] = conv_fwd(p, wts["taps"], row(conv_dw_b[l]), row(conv_ln_g[l]), row(conv_ln_b[l]),
                                      f"conv_fwd_{l}")
        sv["sgu_bias"] = jnp.repeat(sgu_b[l].T, HEAD_DIM, axis=1)
        sgu = sgu_fwd(p, row(sgu_ln_g[l]), row(sgu_ln_b[l]), sgu_w[l], sv["sgu_bias"], f"sgu_fwd_{l}")
        cat = jnp.concatenate([attn, conv, sgu], axis=1)
        sv["catt"] = cat.T
        return mm_res(cat, wts["w_out"], xs, 1.0, f"mix_out_{l}", deps=mid(cat))

    weights, saved = {}, {}
    xs = x0
    starts = []
    for k in range(len(order_fwd)):
        starts.append(gather_start(k, (starts[-1][4],) if starts else ()))
    state = dict(zip(("srcs", "fwd"), gather_forward(0, starts[0], starts[-1][4])))
    for k, (l, grp) in enumerate(order_fwd):
        nxt = starts[k + 1] if k + 1 < len(order_fwd) else None
        wts = gather_finish(k, state["srcs"], state["fwd"], state["fwd"][4])

        def mid(after, k=k, nxt=nxt):
            if not nxt:
                return ()
            state["srcs"], state["fwd"] = gather_forward(k + 1, nxt, after)
            return (state["fwd"][4],)

        if grp == "ffn1_in":
            sv = saved[l, "ffn1"] = {}
            weights[l, "ffn1"] = wts
            head = ffn_head(xs, wts, row(norm_ffn1[l]), f"ffn1_{l}", sv)
            mid(head)
        elif grp == "ffn1_out":
            weights[l, "ffn1"].update(wts)
            xs = mm_res(head, wts["w_out"], xs, FFN_RES, f"ffn1_{l}_out")
            mid(xs)
        else:
            sv = saved[l, grp] = {}
            weights[l, grp] = wts
            if grp == "mix":
                xs = mix_fwd(xs, wts, l, sv, mid)
            else:
                xs = ffn_fwd(xs, wts, row(norm_ffn1[l] if grp == "ffn1" else norm_ffn2[l]), f"{grp}_{l}", sv, mid)

    dx, dxb, d_final_norm, loss = final_loss(xs, row(final_norm), loss_target[0], "final_loss")

    def scatter_start(grads, kinds, tag, deps=()):
        half = N_DEV // 2
        lands = [lax.empty((half, g.shape[1], g.shape[2] // half) if kind == "cols" else (half,) + g.shape[1:], g.dtype)
                 for g, kind in zip(grads, kinds)]
        return split_start(grads, lands, _routes_pair, f"pair_start_{tag}", deps, src_kinds=kinds)

    def pair_to_chips(started, after, tag):
        kinds = started[5][0]
        grads, landed = split_wait(started, after, _routes_pair, f"pair_wait_{tag}")
        sums = [chip_sum(g, ld, c_idx, kind, f"chip_sum_{tag}_{a}")
                for a, (g, ld, kind) in enumerate(zip(grads, landed, kinds))]
        return split_start(sums, [lax.empty(s.shape, s.dtype) for s in sums], _routes_chips, f"chips_start_{tag}")

    def ffn_bwd(dx, dxb, wts, sv, g_norm, tag, deps=(), out_first=False):
        dgu = ffn_dact(dxb, wts["w_out"], sv["gu"], f"{tag}_dact")
        d_w_out = mm_nn(sv["at"], dxb[None], f"{tag}_dwout", bf16, FFN_RES, deps=deps)[0]
        d_w_out = d_w_out.reshape(N_DEV, F // N_DEV, D)
        if not out_first:
            d_w_in = mm_nn(sv["ht"], dgu, f"{tag}_dwin", bf16)
            pair = scatter_start([d_w_in, d_w_out], ["cols", "rows"], tag)
            dh = mm_nt(dgu, wts["w_in"], f"{tag}_dh", deps=(pair[4],))
            chips = [pair_to_chips(pair, dh, tag)]
        else:
            pair_out = scatter_start([d_w_out], ["rows"], f"{tag}_out")
            d_w_in = mm_nn(sv["ht"], dgu, f"{tag}_dwin", bf16, deps=(pair_out[4],))
            chips_out = pair_to_chips(pair_out, d_w_in, f"{tag}_out")
            pair_in = scatter_start([d_w_in], ["cols"], f"{tag}_in", deps=(chips_out[4],))
            dh = mm_nt(dgu, wts["w_in"], f"{tag}_dh", deps=(pair_in[4],))
            chips = [pair_to_chips(pair_in, dh, f"{tag}_in"), chips_out]
        dx, dxb, dg = rmsnorm_bwd(dh, sv["x_in"], g_norm, dx, f"{tag}_dnorm", deps=(chips[0][4],))
        return dx, dxb, dg, chips

    small = [None] * L
    chips_pending = {}
    for l in reversed(range(L)):
        dx, dxb, d_norm_ffn2, chips_pending[l, "ffn2"] = ffn_bwd(
            dx, dxb, weights[l, "ffn2"], saved[l, "ffn2"], row(norm_ffn2[l]), f"ffn2_{l}")

        wts, sv = weights[l, "mix"], saved[l, "mix"]
        d_w_out = mm_nn(sv["catt"], dxb[None], f"mix_dwout_{l}", bf16)[0]
        dcat = mm_nt(dxb[None], wts["w_out"][None], f"mix_dcat_{l}", deps=(d_w_out,))
        p = sv["p"]
        dq, dkv, d_sinks = attn_bwd(p, dcat, rope_c, rope_s1, rope_s2, attn_sinks[l], f"attn_bwd_{l}")
        da_conv, d_taps, d_conv_b, d_conv_g, d_conv_bb = conv_bwd(
            p, sv["conv_y"], dcat, wts["taps"], row(conv_ln_g[l]), row(conv_ln_b[l]), f"conv_bwd_{l}")
        da_sgu, d_sgu_w, d_sgu_bias, d_sgu_g, d_sgu_bb = sgu_bwd(
            p, dcat, row(sgu_ln_g[l]), row(sgu_ln_b[l]), sgu_w[l], jnp.swapaxes(sgu_w[l], 1, 2), sv["sgu_bias"],
            f"sgu_bwd_{l}")
        dp = jnp.concatenate([dq, dkv, da_conv, da_sgu], axis=1)
        d_w_in = mm_nn(sv["ht"], dp[None], f"mix_dwin_{l}", bf16)[0]
        pair = scatter_start([_cols_to_shards(d_w_in), d_w_out.reshape(N_DEV, D // N_DEV, D)], ["rows", "rows"],
                             f"mix_{l}")
        dh = mm_nt(dp[None], wts["w_in"][None], f"mix_dh_{l}", deps=(pair[4],))
        chips_pending[l, "mix"] = [pair_to_chips(pair, dh, f"mix_{l}")]
        dx, dxb, d_norm_mix = rmsnorm_bwd(dh, sv["x_in"], row(norm_mix[l]), dx, f"mix_dnorm_{l}",
                                          deps=(chips_pending[l, "mix"][0][4],))

        small[l] = dict(norm_mix=d_norm_mix[0], conv_dw_w=d_taps[:CONV_W],
                        conv_dw_b=d_conv_b[0], conv_ln_g=d_conv_g[0], conv_ln_b=d_conv_bb[0], sgu_ln_g=d_sgu_g[0],
                        sgu_ln_b=d_sgu_bb[0], sgu_w=d_sgu_w, sgu_b=d_sgu_bias[:, :SGU_HEADS].T,
                        attn_sinks=d_sinks[0, :N_Q_HEADS], norm_ffn2=d_norm_ffn2[0])
        if l == 0:
            early_names = ["norm_mix", "conv_dw_w", "conv_dw_b", "conv_ln_g", "conv_ln_b", "sgu_ln_g", "sgu_ln_b",
                           "sgu_w", "sgu_b", "attn_sinks", "norm_ffn2"]
            early = [jnp.stack([small[k][n] for k in range(L)]) for n in early_names]
            early += [d_final_norm[0], loss.reshape(1)]
            early_shapes = [a.shape for a in early]
            early = _pack(early)
            early_pending = split_start([early], [lax.empty((N_DEV,) + early.shape, f32)], _routes_all,
                                        "small_start", deps=(dxb,))
            early_token = (early_pending[4],)
        else:
            early_token = ()

        dx, dxb, d_norm_ffn1, chips_pending[l, "ffn1"] = ffn_bwd(
            dx, dxb, weights[l, "ffn1"], saved[l, "ffn1"], row(norm_ffn1[l]), f"ffn1_{l}", early_token, l == 0)
        small[l]["norm_ffn1"] = d_norm_ffn1[0]

    grad_x = dx[None]
    late = _pack([jnp.stack([small[l]["norm_ffn1"] for l in range(L)])])
    late_pending = split_start([late], [lax.empty((N_DEV,) + late.shape, f32)], _routes_all, "late_start", deps=(dx,))

    def landed(grp, after):
        sums, lands = [], []
        for l in range(L):
            got = [split_wait(st, after, _routes_chips, f"chips_wait_{grp}_{l}_{a}")
                   for a, st in enumerate(chips_pending[l, grp])]
            sums.append([s for g in got for s in g[0]])
            lands.append([s for g in got for s in g[1]])
        return sums, lands

    given = dict(norm_ffn1=(norm_ffn1, m_norm_ffn1, v_norm_ffn1), norm_mix=(norm_mix, m_norm_mix, v_norm_mix),
                 conv_dw_w=(conv_dw_w, m_conv_dw_w, v_conv_dw_w), conv_dw_b=(conv_dw_b, m_conv_dw_b, v_conv_dw_b),
                 conv_ln_g=(conv_ln_g, m_conv_ln_g, v_conv_ln_g), conv_ln_b=(conv_ln_b, m_conv_ln_b, v_conv_ln_b),
                 sgu_ln_g=(sgu_ln_g, m_sgu_ln_g, v_sgu_ln_g), sgu_ln_b=(sgu_ln_b, m_sgu_ln_b, v_sgu_ln_b),
                 sgu_w=(sgu_w, m_sgu_w, v_sgu_w), sgu_b=(sgu_b, m_sgu_b, v_sgu_b),
                 attn_sinks=(attn_sinks, m_attn_sinks, v_attn_sinks), norm_ffn2=(norm_ffn2, m_norm_ffn2, v_norm_ffn2),
                 final_norm=(final_norm, m_final_norm, v_final_norm))

    def small_update(pending, after, names, shapes, tag):
        (own,), (others,) = split_wait(pending, after, _routes_all, f"{tag}_wait")
        total = _unpack(sum_slots(place_own(others, own, me_idx, "rows", f"{tag}_own"), f"{tag}_sum"), shapes)
        g = dict(zip(names, total))
        if "conv_dw_w" in g:
            g["conv_dw_w"] = lax.dynamic_slice_in_dim(g["conv_dw_w"], me * cw, cw, axis=2)
        upd_names = [n for n in names if n in given]
        upd_shapes = [given[n][0].shape for n in upd_names]
        packed = [_pack([g[n] for n in upd_names])[None]] + [_pack([given[n][k] for n in upd_names])[None]
                                                              for k in range(3)]
        res = adamw([packed[0]], None, no_chip, packed[1], packed[2], packed[3], f"{tag}_adamw")
        return g, [dict(zip(upd_names, _unpack(r[0], upd_shapes))) for r in res], res[0]

    big = {}
    done = [dx]

    def big_update(grp, names):
        sums, lands = landed(grp, done)
        for idx, (name, w, m, v) in enumerate(names):
            big[name] = adamw([lands[l][idx] for l in range(L)], [sums[l][idx] for l in range(L)], chip, w, m, v,
                              f"adamw_{name}")
            done.append(big[name][0])

    big_update("ffn2", (("ffn2_w_in", ffn2_w_in, m_ffn2_w_in, v_ffn2_w_in),
                        ("ffn2_w_out", ffn2_w_out, m_ffn2_w_out, v_ffn2_w_out)))
    big_update("mix", (("w_in", w_in, m_w_in, v_w_in), ("w_out", w_out, m_w_out, v_w_out)))
    g_early, upd_early, marker = small_update(early_pending, done, early_names + ["final_norm", "loss"], early_shapes,
                                              "small")
    done.append(marker)
    big_update("ffn1", (("ffn1_w_in", ffn1_w_in, m_ffn1_w_in, v_ffn1_w_in),
                        ("ffn1_w_out", ffn1_w_out, m_ffn1_w_out, v_ffn1_w_out)))
    _, upd_late, _ = small_update(late_pending, done, ["norm_ffn1"], [(L, D)], "late")
    upd = [{**upd_early[k], **upd_late[k]} for k in range(4)]

    order = ["norm_ffn1", "ffn1_w_in", "ffn1_w_out", "norm_mix", "w_in", "conv_dw_w", "conv_dw_b", "conv_ln_g",
             "conv_ln_b", "sgu_ln_g", "sgu_ln_b", "sgu_w", "sgu_b", "attn_sinks", "w_out", "norm_ffn2", "ffn2_w_in",
             "ffn2_w_out", "final_norm"]
    outs = [g_early["loss"].reshape(()), grad_x]
    for k in range(4):
        outs += [big[n][k] if n in big else upd[k][n] for n in order]
    return tuple(outs)
```

```python
import functools
import math

import jax
import jax.numpy as jnp
from jax import lax
from jax.experimental import pallas as pl
from jax.experimental.pallas import tpu as pltpu

f32 = jnp.float32
bf16 = jnp.bfloat16

N_DEV = 8
HEAD_DIM = 64
N_Q_HEADS = 16
N_KV_HEADS = 4
GQ = N_Q_HEADS // N_KV_HEADS
BLK = 128
ROT_DIM = 16
ROPE_THETA = 500000.0
CONV_W = 31
CONV_CH = 512
SGU_CH = 512
SGU_HEADS = 8
Q_END = N_Q_HEADS * HEAD_DIM
KV_W = 2 * N_KV_HEADS * HEAD_DIM
V_END = Q_END + KV_W
IN_COLS = V_END + 2 * CONV_CH + 2 * SGU_CH
HALO = 32
NORM_EPS = 1e-5
FFN_RES = 0.5
ADAM_LR, ADAM_B1, ADAM_B2, ADAM_EPS, ADAM_WD, ADAM_STEP = 0.001, 0.9, 0.999, 1e-08, 0.01, 10
LANE = 128
VMEM_LIMIT = 56 * 2 ** 20
MM_ROWS = 1024
MM_BLOCK_BYTES = 16 * 2 ** 20
MESH = pl.DeviceIdType.MESH

NN = (((1,), (0,)), ((), ()))
NT = (((1,), (1,)), ((), ()))
TN = (((0,), (0,)), ((), ()))


def _tile(dim, pref):
    t = min(pref, dim)
    while dim % t:
        t //= 2
    return t


def _row_tile(rows, k, most=2048):
    for n in range(1, 17):
        t = rows // n
        if rows % n == 0 and t % LANE == 0 and t <= most and t * k * 2 <= MM_BLOCK_BYTES:
            return t
    return _tile(rows, 512)


def _params(*sem):
    return pltpu.CompilerParams(dimension_semantics=sem, vmem_limit_bytes=VMEM_LIMIT)


def _sig(x):
    return 1.0 / (1.0 + jnp.exp(-x))


def _dot(a, b, dims=NN):
    return lax.dot_general(a, b, dims, preferred_element_type=f32)


def _rowsum8(x):
    return x.reshape(x.shape[0] // 8, 8, x.shape[1]).sum(axis=0)


def rmsnorm_fwd(x, g, name, deps=()):
    S, D = x.shape
    tm = _tile(S, 512)

    def body(x_ref, g_ref, *rest):
        o_ref, ot_ref = rest[-2:]
        xv = x_ref[...]
        r = lax.rsqrt(jnp.mean(xv * xv, axis=-1, keepdims=True) + NORM_EPS)
        hb = (xv * r * g_ref[...]).astype(bf16)
        o_ref[...] = hb
        ot_ref[...] = hb.T

    return pl.pallas_call(
        body, name=name, grid=(S // tm,),
        in_specs=[pl.BlockSpec((tm, D), lambda i: (i, 0)), pl.BlockSpec((1, D), lambda i: (0, 0))] + [ANY] * len(deps),
        out_specs=[pl.BlockSpec((tm, D), lambda i: (i, 0)), pl.BlockSpec((D, tm), lambda i: (0, i))],
        out_shape=[jax.ShapeDtypeStruct((S, D), bf16), jax.ShapeDtypeStruct((D, S), bf16)],
        compiler_params=_params("parallel"),
    )(x, g, *deps)


def rmsnorm_bwd(dh, x, g, dres, name, deps=()):
    S, D = x.shape
    tm = _tile(S, 256)
    n = S // tm

    def body(dh_ref, x_ref, g_ref, dres_ref, *rest):
        dx_ref, dxb_ref, dg_ref, acc = rest[-4:]
        i = pl.program_id(0)

        @pl.when(i == 0)
        def _():
            acc[...] = jnp.zeros_like(acc)

        xv = x_ref[...]
        r = lax.rsqrt(jnp.mean(xv * xv, axis=-1, keepdims=True) + NORM_EPS)
        xh = xv * r
        dy = dh_ref[...]
        gy = dy * g_ref[...]
        dx = dres_ref[...] + r * (gy - xh * jnp.mean(gy * xh, axis=-1, keepdims=True))
        dx_ref[...] = dx
        dxb_ref[...] = dx.astype(bf16)
        acc[...] += _rowsum8(dy * xh)

        @pl.when(i == n - 1)
        def _():
            dg_ref[...] = jnp.sum(acc[...], axis=0, keepdims=True)

    row = pl.BlockSpec((tm, D), lambda i: (i, 0))
    vec = pl.BlockSpec((1, D), lambda i: (0, 0))
    return pl.pallas_call(
        body, name=name, grid=(n,),
        in_specs=[row, row, vec, row] + [ANY] * len(deps),
        out_specs=[row, row, vec],
        out_shape=[jax.ShapeDtypeStruct((S, D), f32), jax.ShapeDtypeStruct((S, D), bf16),
                   jax.ShapeDtypeStruct((1, D), f32)],
        scratch_shapes=[pltpu.VMEM((8, D), f32)],
        compiler_params=_params("arbitrary"),
    )(dh, x, g, dres, *deps)


def final_loss(x, g, tgt, name):
    S, D = x.shape
    tm = _tile(S, 256)
    n = S // tm

    def body(x_ref, g_ref, t_ref, dx_ref, dxb_ref, dg_ref, loss_ref, acc):
        i = pl.program_id(0)

        @pl.when(i == 0)
        def _():
            acc[...] = jnp.zeros_like(acc)
            loss_ref[...] = jnp.zeros_like(loss_ref)

        xv = x_ref[...]
        gv = g_ref[...]
        r = lax.rsqrt(jnp.mean(xv * xv, axis=-1, keepdims=True) + NORM_EPS)
        xh = xv * r
        diff = xh * gv - t_ref[...]
        tok = jnp.mean(diff * diff, axis=-1, keepdims=True)
        loss_ref[...] += 0.5 * jnp.sum(tok, axis=0, keepdims=True)
        dy = diff / D
        gy = dy * gv
        dx = r * (gy - xh * jnp.mean(gy * xh, axis=-1, keepdims=True))
        dx_ref[...] = dx
        dxb_ref[...] = dx.astype(bf16)
        acc[...] += _rowsum8(dy * xh)

        @pl.when(i == n - 1)
        def _():
            dg_ref[...] = jnp.sum(acc[...], axis=0, keepdims=True)

    row = pl.BlockSpec((tm, D), lambda i: (i, 0))
    vec = pl.BlockSpec((1, D), lambda i: (0, 0))
    return pl.pallas_call(
        body, name=name, grid=(n,),
        in_specs=[row, vec, row],
        out_specs=[row, row, vec, pl.BlockSpec((1, 1), lambda i: (0, 0))],
        out_shape=[jax.ShapeDtypeStruct((S, D), f32), jax.ShapeDtypeStruct((S, D), bf16),
                   jax.ShapeDtypeStruct((1, D), f32), jax.ShapeDtypeStruct((1, 1), f32)],
        scratch_shapes=[pltpu.VMEM((8, D), f32)],
        compiler_params=_params("arbitrary"),
    )(x, g, tgt)


def ffn_in(h, w2, name):
    S, D = h.shape
    F = w2.shape[2]
    tm, tn = _tile(S, MM_ROWS), _tile(F, 512)

    def body(h_ref, w_ref, gu_ref, a_ref, at_ref):
        hv = h_ref[...]
        g = _dot(hv, w_ref[0])
        u = _dot(hv, w_ref[1])
        gu_ref[0] = g.astype(bf16)
        gu_ref[1] = u.astype(bf16)
        a = (g * _sig(g) * u).astype(bf16)
        a_ref[...] = a
        at_ref[...] = a.T

    return pl.pallas_call(
        body, name=name, grid=(S // tm, F // tn),
        in_specs=[pl.BlockSpec((tm, D), lambda i, j: (i, 0)), pl.BlockSpec((2, D, tn), lambda i, j: (0, 0, j))],
        out_specs=[pl.BlockSpec((2, tm, tn), lambda i, j: (0, i, j)), pl.BlockSpec((tm, tn), lambda i, j: (i, j)),
                   pl.BlockSpec((tn, tm), lambda i, j: (j, i))],
        out_shape=[jax.ShapeDtypeStruct((2, S, F), bf16), jax.ShapeDtypeStruct((S, F), bf16),
                   jax.ShapeDtypeStruct((F, S), bf16)],
        compiler_params=_params("parallel", "parallel"),
    )(h, w2)


def mm_res(a, w, x, scale, name, deps=()):
    S, K = a.shape
    N = w.shape[1]
    tm, tn = _row_tile(S, K, MM_ROWS), _tile(N, 512)

    def body(a_ref, w_ref, x_ref, *rest):
        rest[-1][...] = x_ref[...] + scale * _dot(a_ref[...], w_ref[...])

    return pl.pallas_call(
        body, name=name, grid=(S // tm, N // tn),
        in_specs=[pl.BlockSpec((tm, K), lambda i, j: (i, 0)), pl.BlockSpec((K, tn), lambda i, j: (0, j)),
                  pl.BlockSpec((tm, tn), lambda i, j: (i, j))] + [ANY] * len(deps),
        out_specs=pl.BlockSpec((tm, tn), lambda i, j: (i, j)),
        out_shape=jax.ShapeDtypeStruct((S, N), f32),
        compiler_params=_params("parallel", "parallel"),
    )(a, w, x, *deps)


def mm_nn(a, b, name, out_dtype=f32, scale=1.0, deps=()):
    M, K = a.shape
    G, _, N = b.shape
    tm, tn = _row_tile(M, K), _tile(N, 512)

    def body(a_ref, b_ref, *rest):
        acc = _dot(a_ref[...], b_ref[...])
        rest[-1][...] = (acc if scale == 1.0 else scale * acc).astype(out_dtype)

    return pl.pallas_call(
        body, name=name, grid=(G, M // tm, N // tn),
        in_specs=[pl.BlockSpec((tm, K), lambda g, i, j: (i, 0)),
                  pl.BlockSpec((None, K, tn), lambda g, i, j: (g, 0, j))] + [ANY] * len(deps),
        out_specs=pl.BlockSpec((None, tm, tn), lambda g, i, j: (g, i, j)),
        out_shape=jax.ShapeDtypeStruct((G, M, N), out_dtype),
        compiler_params=_params("parallel", "parallel", "parallel"),
    )(a, b, *deps)


def mm_nt(a, w, name, deps=()):
    G, S, K = a.shape
    N = w.shape[1]
    tm, tn = _row_tile(S, K), _tile(N, 512)

    def body(a_ref, w_ref, *rest):
        o_ref = rest[-1]
        part = _dot(a_ref[...], w_ref[...], NT)
        if G == 1:
            o_ref[...] = part
        else:
            g = pl.program_id(2)

            @pl.when(g == 0)
            def _():
                o_ref[...] = part

            @pl.when(g > 0)
            def _():
                o_ref[...] += part

    return pl.pallas_call(
        body, name=name, grid=(S // tm, N // tn, G),
        in_specs=[pl.BlockSpec((None, tm, K), lambda i, j, g: (g, i, 0)),
                  pl.BlockSpec((None, tn, K), lambda i, j, g: (g, j, 0))] + [ANY] * len(deps),
        out_specs=pl.BlockSpec((tm, tn), lambda i, j, g: (i, j)),
        out_shape=jax.ShapeDtypeStruct((S, N), f32),
        compiler_params=_params("parallel", "parallel", "arbitrary"),
    )(a, w, *deps)


def ffn_dact(dx, wout, gu, name):
    S, D = dx.shape
    F = wout.shape[0]
    tm, tn = _tile(S, MM_ROWS), _tile(F, 512)

    def body(dx_ref, w_ref, gu_ref, o_ref):
        da = FFN_RES * _dot(dx_ref[...], w_ref[...], NT)
        g = gu_ref[0].astype(f32)
        u = gu_ref[1].astype(f32)
        sg = _sig(g)
        o_ref[0] = (da * u * (sg * (1.0 + g * (1.0 - sg)))).astype(bf16)
        o_ref[1] = (da * (g * sg)).astype(bf16)

    return pl.pallas_call(
        body, name=name, grid=(S // tm, F // tn),
        in_specs=[pl.BlockSpec((tm, D), lambda i, j: (i, 0)), pl.BlockSpec((tn, D), lambda i, j: (j, 0)),
                  pl.BlockSpec((2, tm, tn), lambda i, j: (0, i, j))],
        out_specs=pl.BlockSpec((2, tm, tn), lambda i, j: (0, i, j)),
        out_shape=jax.ShapeDtypeStruct((2, S, F), bf16),
        compiler_params=_params("parallel", "parallel"),
    )(dx, wout, gu)


def _rope(t, c, s1, s2):
    w = t.shape[1]
    return t * c + pltpu.roll(t, 8, 1) * s1 + pltpu.roll(t, w - 8, 1) * s2


def _rope_t(d, c, s1, s2):
    w = d.shape[1]
    return d * c + pltpu.roll(d * s1, w - 8, 1) + pltpu.roll(d * s2, 8, 1)


def _attn_mask(n):
    qi = lax.broadcasted_iota(jnp.int32, (BLK, 2 * BLK), 0)
    kj = lax.broadcasted_iota(jnp.int32, (BLK, 2 * BLK), 1)
    dist = qi + BLK - kj
    return (dist >= 0) & (dist < BLK) & ((kj >= BLK) | (n > 0))


def _softmax_sink(s, valid, sk):
    s = jnp.where(valid, s, -1e30)
    m = jnp.maximum(jnp.max(s, axis=-1, keepdims=True), sk)
    e = jnp.exp(s - m)
    es = jnp.exp(sk - m)
    inv = 1.0 / (jnp.sum(e, axis=-1, keepdims=True) + es)
    return e * inv, es * inv


def attn_fwd(p, rope_c, rope_s1, rope_s2, sinks, name):
    S = p.shape[0]
    nb = S // BLK
    kvb = Q_END // KV_W

    def body(sink_ref, q_ref, kvc_ref, kvp_ref, cc_ref, s1c_ref, s2c_ref, cp_ref, s1p_ref, s2p_ref, o_ref):
        n = pl.program_id(0)
        cc, s1c, s2c = cc_ref[...], s1c_ref[...], s2c_ref[...]
        cp, s1p, s2p = cp_ref[...], s1p_ref[...], s2p_ref[...]
        q = _rope(q_ref[...], jnp.tile(cc, (1, 8)), jnp.tile(s1c, (1, 8)), jnp.tile(s2c, (1, 8)))
        kc = _rope(kvc_ref[:, :256], jnp.tile(cc, (1, 2)), jnp.tile(s1c, (1, 2)), jnp.tile(s2c, (1, 2)))
        kp = _rope(kvp_ref[:, :256], jnp.tile(cp, (1, 2)), jnp.tile(s1p, (1, 2)), jnp.tile(s2p, (1, 2)))
        k = jnp.concatenate([kp, kc], axis=0).astype(bf16)
        v = jnp.concatenate([kvp_ref[:, 256:], kvc_ref[:, 256:]], axis=0).astype(bf16)
        q = q.astype(bf16)
        valid = _attn_mask(n)
        for h in range(N_KV_HEADS):
            kh = k[:, h * HEAD_DIM:(h + 1) * HEAD_DIM]
            vh = v[:, h * HEAD_DIM:(h + 1) * HEAD_DIM]
            for g in range(GQ):
                hq = h * GQ + g
                qh = q[:, hq * HEAD_DIM:(hq + 1) * HEAD_DIM]
                s = _dot(qh, kh, NT) * (HEAD_DIM ** -0.5)
                pr, _ = _softmax_sink(s, valid, sink_ref[hq])
                o = _dot(pr.astype(bf16), vh)
                o_ref[:, hq * HEAD_DIM:(hq + 1) * HEAD_DIM] = o.astype(bf16)

    tab_c = pl.BlockSpec((BLK, LANE), lambda n: (n, 0))
    tab_p = pl.BlockSpec((BLK, LANE), lambda n: (jnp.maximum(n - 1, 0), 0))
    return pl.pallas_call(
        body, name=name, grid=(nb,),
        in_specs=[pl.BlockSpec(memory_space=pltpu.SMEM),
                  pl.BlockSpec((BLK, Q_END), lambda n: (n, 0)),
                  pl.BlockSpec((BLK, KV_W), lambda n: (n, kvb)),
                  pl.BlockSpec((BLK, KV_W), lambda n: (jnp.maximum(n - 1, 0), kvb)),
                  tab_c, tab_c, tab_c, tab_p, tab_p, tab_p],
        out_specs=pl.BlockSpec((BLK, Q_END), lambda n: (n, 0)),
        out_shape=jax.ShapeDtypeStruct((S, Q_END), bf16),
        compiler_params=_params("parallel"),
    )(sinks, p, p, p, rope_c, rope_s1, rope_s2, rope_c, rope_s1, rope_s2)


def attn_bwd(p, dcat, rope_c, rope_s1, rope_s2, sinks, name):
    S = p.shape[0]
    nb = S // BLK
    kvb = Q_END // KV_W

    def body(sink_ref, q_ref, kvc_ref, kvp_ref, do_ref, cc_ref, s1c_ref, s2c_ref, cp_ref, s1p_ref, s2p_ref,
             dq_ref, dkv_ref, dsink_ref, carry, dq_scr, dkv_scr):
        n = pl.program_id(0)

        @pl.when(n == 0)
        def _():
            carry[...] = jnp.zeros_like(carry)
            dsink_ref[...] = jnp.zeros_like(dsink_ref)

        cp, s1p, s2p = cp_ref[...], s1p_ref[...], s2p_ref[...]
        cp2, s1p2, s2p2 = jnp.tile(cp, (1, 2)), jnp.tile(s1p, (1, 2)), jnp.tile(s2p, (1, 2))

        @pl.when(n < nb)
        def _():
            cc, s1c, s2c = cc_ref[...], s1c_ref[...], s2c_ref[...]
            cc8, s1c8, s2c8 = jnp.tile(cc, (1, 8)), jnp.tile(s1c, (1, 8)), jnp.tile(s2c, (1, 8))
            q = _rope(q_ref[...], cc8, s1c8, s2c8).astype(bf16)
            kc = _rope(kvc_ref[:, :256], jnp.tile(cc, (1, 2)), jnp.tile(s1c, (1, 2)), jnp.tile(s2c, (1, 2)))
            kp = _rope(kvp_ref[:, :256], cp2, s1p2, s2p2)
            k = jnp.concatenate([kp, kc], axis=0).astype(bf16)
            v = jnp.concatenate([kvp_ref[:, 256:], kvc_ref[:, 256:]], axis=0).astype(bf16)
            do = do_ref[...].astype(bf16)
            valid = _attn_mask(n)
            lane = lax.broadcasted_iota(jnp.int32, (1, LANE), 1)
            dsink = jnp.zeros((1, LANE), f32)
            for h in range(N_KV_HEADS):
                kh = k[:, h * HEAD_DIM:(h + 1) * HEAD_DIM]
                vh = v[:, h * HEAD_DIM:(h + 1) * HEAD_DIM]
                dkh = jnp.zeros((2 * BLK, HEAD_DIM), f32)
                dvh = jnp.zeros((2 * BLK, HEAD_DIM), f32)
                for g in range(GQ):
                    hq = h * GQ + g
                    qh = q[:, hq * HEAD_DIM:(hq + 1) * HEAD_DIM]
                    doh = do[:, hq * HEAD_DIM:(hq + 1) * HEAD_DIM]
                    s = _dot(qh, kh, NT) * (HEAD_DIM ** -0.5)
                    pr, ps = _softmax_sink(s, valid, sink_ref[hq])
                    dpr = _dot(doh, vh, NT)
                    dvh = dvh + _dot(pr.astype(bf16), doh, TN)
                    row = jnp.sum(pr * dpr, axis=-1, keepdims=True)
                    ds = (pr * (dpr - row) * (HEAD_DIM ** -0.5)).astype(bf16)
                    dsink = dsink + jnp.where(lane == hq, -jnp.sum(ps * row, axis=0, keepdims=True), 0.0)
                    dq_scr[:, hq * HEAD_DIM:(hq + 1) * HEAD_DIM] = _dot(ds, kh)
                    dkh = dkh + _dot(ds, qh, TN)
                dkv_scr[:, h * HEAD_DIM:(h + 1) * HEAD_DIM] = dkh
                dkv_scr[:, 256 + h * HEAD_DIM:256 + (h + 1) * HEAD_DIM] = dvh
            dsink_ref[...] += dsink
            dq_ref[...] = _rope_t(dq_scr[...], cc8, s1c8, s2c8).astype(bf16)

        prev = carry[...]

        @pl.when(n < nb)
        def _():
            dkv_scr[pl.ds(0, BLK), :] = dkv_scr[pl.ds(0, BLK), :] + prev

        @pl.when(n == nb)
        def _():
            dkv_scr[pl.ds(0, BLK), :] = prev

        done = dkv_scr[pl.ds(0, BLK), :]
        dkv_ref[:, :256] = _rope_t(done[:, :256], cp2, s1p2, s2p2).astype(bf16)
        dkv_ref[:, 256:] = done[:, 256:].astype(bf16)

        @pl.when(n < nb)
        def _():
            carry[...] = dkv_scr[pl.ds(BLK, BLK), :]

    cur = lambda n: jnp.minimum(n, nb - 1)
    prv = lambda n: jnp.maximum(n - 1, 0)
    tab_c = pl.BlockSpec((BLK, LANE), lambda n: (cur(n), 0))
    tab_p = pl.BlockSpec((BLK, LANE), lambda n: (prv(n), 0))
    return pl.pallas_call(
        body, name=name, grid=(nb + 1,),
        in_specs=[pl.BlockSpec(memory_space=pltpu.SMEM),
                  pl.BlockSpec((BLK, Q_END), lambda n: (cur(n), 0)),
                  pl.BlockSpec((BLK, KV_W), lambda n: (cur(n), kvb)),
                  pl.BlockSpec((BLK, KV_W), lambda n: (prv(n), kvb)),
                  pl.BlockSpec((BLK, Q_END), lambda n: (cur(n), 0)),
                  tab_c, tab_c, tab_c, tab_p, tab_p, tab_p],
        out_specs=[pl.BlockSpec((BLK, Q_END), lambda n: (cur(n), 0)),
                   pl.BlockSpec((BLK, KV_W), lambda n: (prv(n), 0)),
                   pl.BlockSpec((1, LANE), lambda n: (0, 0))],
        out_shape=[jax.ShapeDtypeStruct((S, Q_END), bf16), jax.ShapeDtypeStruct((S, KV_W), bf16),
                   jax.ShapeDtypeStruct((1, LANE), f32)],
        scratch_shapes=[pltpu.VMEM((BLK, KV_W), f32), pltpu.VMEM((BLK, Q_END), f32), pltpu.VMEM((2 * BLK, KV_W), f32)],
        compiler_params=_params("arbitrary"),
    )(sinks, p, p, p, dcat, rope_c, rope_s1, rope_s2, rope_c, rope_s1, rope_s2)


CONV_ROWS = 32
A1_BLK = V_END // CONV_CH
A2_BLK = A1_BLK + 1


def _ln_stats(y):
    mu = jnp.mean(y, axis=-1, keepdims=True)
    xc = y - mu
    rstd = lax.rsqrt(jnp.mean(xc * xc, axis=-1, keepdims=True) + NORM_EPS)
    return xc * rstd, rstd


def conv_fwd(p, w, b, lng, lnb, name):
    S = p.shape[0]
    T = _tile(S, 256)
    r = T // HALO

    def body(a1_ref, a2_ref, h1_ref, h2_ref, w_ref, b_ref, g_ref, bb_ref, o_ref, y_ref, scr):
        i = pl.program_id(0)
        halo = h1_ref[...] * _sig(h2_ref[...])
        scr[pl.ds(0, HALO), :] = jnp.where(i > 0, halo, 0.0)
        scr[pl.ds(HALO, T), :] = a1_ref[...] * _sig(a2_ref[...])
        acc = jnp.zeros((T, CONV_CH), f32) + b_ref[...]
        for j in range(CONV_W):
            acc = acc + scr[pl.ds(HALO - (CONV_W - 1) + j, T), :] * w_ref[j:j + 1, :]
        y_ref[...] = acc
        yh, _ = _ln_stats(acc)
        z = yh * g_ref[...] + bb_ref[...]
        o_ref[...] = (z * _sig(z)).astype(bf16)

    vec = pl.BlockSpec((1, CONV_CH), lambda i: (0, 0))
    halo_map = lambda i: jnp.maximum(i * r - 1, 0)
    return pl.pallas_call(
        body, name=name, grid=(S // T,),
        in_specs=[pl.BlockSpec((T, CONV_CH), lambda i: (i, A1_BLK)), pl.BlockSpec((T, CONV_CH), lambda i: (i, A2_BLK)),
                  pl.BlockSpec((HALO, CONV_CH), lambda i: (halo_map(i), A1_BLK)),
                  pl.BlockSpec((HALO, CONV_CH), lambda i: (halo_map(i), A2_BLK)),
                  pl.BlockSpec((HALO, CONV_CH), lambda i: (0, 0)), vec, vec, vec],
        out_specs=[pl.BlockSpec((T, CONV_CH), lambda i: (i, 0)), pl.BlockSpec((T, CONV_CH), lambda i: (i, 0))],
        out_shape=[jax.ShapeDtypeStruct((S, CONV_CH), bf16), jax.ShapeDtypeStruct((S, CONV_CH), f32)],
        scratch_shapes=[pltpu.VMEM((T + HALO, CONV_CH), f32)],
        compiler_params=_params("parallel"),
    )(p, p, p, p, w, b, lng, lnb)


def conv_bwd(p, y, dcat, w, lng, lnb, name):
    S = p.shape[0]
    T = _tile(S, 256)
    n = S // T
    r = T // HALO
    dcb = Q_END // CONV_CH
    rc = _tile(T, CONV_ROWS)

    def body(a1_ref, a2_ref, h1_ref, h2_ref, y_ref, yn_ref, do_ref, don_ref, w_ref, g_ref, bb_ref,
             da_ref, dw_ref, db_ref, dg_ref, dbb_ref, scr_h, scr_dy, acc_b, acc_g, acc_bb):
        i = pl.program_id(0)

        @pl.when(i == 0)
        def _():
            dw_ref[...] = jnp.zeros_like(dw_ref)
            acc_b[...] = jnp.zeros_like(acc_b)
            acc_g[...] = jnp.zeros_like(acc_g)
            acc_bb[...] = jnp.zeros_like(acc_bb)

        gv, bv = g_ref[...], bb_ref[...]

        def ln_silu_bwd(yv, dout):
            yh, rstd = _ln_stats(yv)
            z = yh * gv + bv
            sg = _sig(z)
            dz = dout * (sg * (1.0 + z * (1.0 - sg)))
            gz = dz * gv
            dy = rstd * (gz - jnp.mean(gz, axis=-1, keepdims=True) - yh * jnp.mean(gz * yh, axis=-1, keepdims=True))
            return dy, dz, yh

        chunks = [pl.ds(c * rc, rc) for c in range(T // rc)]
        dyn, _, _ = ln_silu_bwd(yn_ref[...], don_ref[...])
        scr_dy[pl.ds(T, HALO), :] = jnp.where(i < n - 1, dyn, 0.0)
        halo = h1_ref[...] * _sig(h2_ref[...])
        scr_h[pl.ds(0, HALO), :] = jnp.where(i > 0, halo, 0.0)
        for c, rows in enumerate(chunks):
            dy, dz, yh = ln_silu_bwd(y_ref[rows, :], do_ref[rows, :])
            acc_g[...] += _rowsum8(dz * yh)
            acc_bb[...] += _rowsum8(dz)
            acc_b[...] += _rowsum8(dy)
            scr_dy[rows, :] = dy
            scr_h[pl.ds(HALO + c * rc, rc), :] = a1_ref[rows, :] * _sig(a2_ref[rows, :])
        for c, rows in enumerate(chunks):
            dh = jnp.zeros((rc, CONV_CH), f32)
            for j in range(CONV_W):
                dh = dh + scr_dy[pl.ds(c * rc + CONV_W - 1 - j, rc), :] * w_ref[j:j + 1, :]
            a1 = a1_ref[rows, :]
            sg2 = _sig(a2_ref[rows, :])
            da_ref[rows, :CONV_CH] = (dh * sg2).astype(bf16)
            da_ref[rows, CONV_CH:] = (dh * a1 * sg2 * (1.0 - sg2)).astype(bf16)
        for j in range(CONV_W):
            part = jnp.zeros((8, CONV_CH), f32)
            for c, rows in enumerate(chunks):
                part = part + _rowsum8(scr_dy[rows, :] * scr_h[pl.ds(c * rc + HALO - (CONV_W - 1) + j, rc), :])
            dw_ref[j:j + 1, :] += jnp.sum(part, axis=0, keepdims=True)

        @pl.when(i == n - 1)
        def _():
            db_ref[...] = jnp.sum(acc_b[...], axis=0, keepdims=True)
            dg_ref[...] = jnp.sum(acc_g[...], axis=0, keepdims=True)
            dbb_ref[...] = jnp.sum(acc_bb[...], axis=0, keepdims=True)

    vec = pl.BlockSpec((1, CONV_CH), lambda i: (0, 0))
    tap = pl.BlockSpec((HALO, CONV_CH), lambda i: (0, 0))
    prev_map = lambda i: jnp.maximum(i * r - 1, 0)
    next_map = lambda i: jnp.minimum((i + 1) * r, S // HALO - 1)
    return pl.pallas_call(
        body, name=name, grid=(n,),
        in_specs=[pl.BlockSpec((T, CONV_CH), lambda i: (i, A1_BLK)), pl.BlockSpec((T, CONV_CH), lambda i: (i, A2_BLK)),
                  pl.BlockSpec((HALO, CONV_CH), lambda i: (prev_map(i), A1_BLK)),
                  pl.BlockSpec((HALO, CONV_CH), lambda i: (prev_map(i), A2_BLK)),
                  pl.BlockSpec((T, CONV_CH), lambda i: (i, 0)),
                  pl.BlockSpec((HALO, CONV_CH), lambda i: (next_map(i), 0)),
                  pl.BlockSpec((T, CONV_CH), lambda i: (i, dcb)),
                  pl.BlockSpec((HALO, CONV_CH), lambda i: (next_map(i), dcb)),
                  tap, vec, vec],
        out_specs=[pl.BlockSpec((T, 2 * CONV_CH), lambda i: (i, 0)), tap, vec, vec, vec],
        out_shape=[jax.ShapeDtypeStruct((S, 2 * CONV_CH), bf16), jax.ShapeDtypeStruct((HALO, CONV_CH), f32),
                   jax.ShapeDtypeStruct((1, CONV_CH), f32), jax.ShapeDtypeStruct((1, CONV_CH), f32),
                   jax.ShapeDtypeStruct((1, CONV_CH), f32)],
        scratch_shapes=[pltpu.VMEM((T + HALO, CONV_CH), f32), pltpu.VMEM((T + HALO, CONV_CH), f32),
                        pltpu.VMEM((8, CONV_CH), f32), pltpu.VMEM((8, CONV_CH), f32), pltpu.VMEM((8, CONV_CH), f32)],
        compiler_params=_params("arbitrary"),
    )(p, p, p, p, y, y, dcat, dcat, w, lng, lnb)


U_BLK = (V_END + 2 * CONV_CH) // SGU_CH
SV_BLK = U_BLK + 1


def _tril(w, transposed=False):
    row = lax.broadcasted_iota(jnp.int32, (BLK, BLK), 0)
    col = lax.broadcasted_iota(jnp.int32, (BLK, BLK), 1)
    keep = (col >= row) if transposed else (row >= col)
    return jnp.where(keep, w, 0.0)


def sgu_fwd(p, lng, lnb, w, bias, name):
    S = p.shape[0]
    T = _tile(S, 256)

    def body(u_ref, v_ref, g_ref, bb_ref, w_ref, bias_ref, o_ref):
        yh, _ = _ln_stats(v_ref[...])
        v = (yh * g_ref[...] + bb_ref[...]).astype(bf16)
        low = lax.broadcasted_iota(jnp.int32, (BLK, LANE), 1) < HEAD_DIM
        for pr in range(SGU_HEADS // 2):
            lanes = pl.ds(pr * LANE, LANE)
            w0 = _tril(w_ref[2 * pr]).astype(bf16)
            w1 = _tril(w_ref[2 * pr + 1]).astype(bf16)
            for c in range(T // BLK):
                rows = pl.ds(c * BLK, BLK)
                vp = v[c * BLK:(c + 1) * BLK, pr * LANE:(pr + 1) * LANE]
                mixed = jnp.where(low, _dot(w0, vp), _dot(w1, vp)) + bias_ref[:, lanes]
                o_ref[rows, lanes] = (u_ref[rows, lanes] * mixed).astype(bf16)

    vec = pl.BlockSpec((1, SGU_CH), lambda i: (0, 0))
    return pl.pallas_call(
        body, name=name, grid=(S // T,),
        in_specs=[pl.BlockSpec((T, SGU_CH), lambda i: (i, U_BLK)), pl.BlockSpec((T, SGU_CH), lambda i: (i, SV_BLK)),
                  vec, vec, pl.BlockSpec((SGU_HEADS, BLK, BLK), lambda i: (0, 0, 0)),
                  pl.BlockSpec((BLK, SGU_CH), lambda i: (0, 0))],
        out_specs=pl.BlockSpec((T, SGU_CH), lambda i: (i, 0)),
        out_shape=jax.ShapeDtypeStruct((S, SGU_CH), bf16),
        compiler_params=_params("parallel"),
    )(p, p, lng, lnb, w, bias)


def sgu_bwd(p, dcat, lng, lnb, w, wt, bias, name):
    S = p.shape[0]
    T = _tile(S, 256)
    n = S // T
    dsb = (Q_END + CONV_CH) // SGU_CH

    def body(u_ref, v_ref, do_ref, g_ref, bb_ref, w_ref, wt_ref, bias_ref,
             da_ref, dw_ref, db_ref, dg_ref, dbb_ref, dv_scr, acc_bias, acc_g, acc_bb):
        i = pl.program_id(0)

        @pl.when(i == 0)
        def _():
            dw_ref[...] = jnp.zeros_like(dw_ref)
            acc_bias[...] = jnp.zeros_like(acc_bias)
            acc_g[...] = jnp.zeros_like(acc_g)
            acc_bb[...] = jnp.zeros_like(acc_bb)

        gv = g_ref[...]
        yh, rstd = _ln_stats(v_ref[...])
        v = (yh * gv + bb_ref[...]).astype(bf16)
        low = lax.broadcasted_iota(jnp.int32, (BLK, LANE), 1) < HEAD_DIM
        for pr in range(SGU_HEADS // 2):
            lanes = pl.ds(pr * LANE, LANE)
            w0 = _tril(w_ref[2 * pr]).astype(bf16)
            w1 = _tril(w_ref[2 * pr + 1]).astype(bf16)
            wt0 = _tril(wt_ref[2 * pr], True).astype(bf16)
            wt1 = _tril(wt_ref[2 * pr + 1], True).astype(bf16)
            dw0 = jnp.zeros((BLK, BLK), f32)
            dw1 = jnp.zeros((BLK, BLK), f32)
            for c in range(T // BLK):
                rows = pl.ds(c * BLK, BLK)
                vp = v[c * BLK:(c + 1) * BLK, pr * LANE:(pr + 1) * LANE]
                mixed = jnp.where(low, _dot(w0, vp), _dot(w1, vp)) + bias_ref[:, lanes]
                do = do_ref[rows, lanes]
                da_ref[rows, lanes] = (do * mixed).astype(bf16)
                dm = do * u_ref[rows, lanes]
                acc_bias[:, lanes] += dm
                dmb = dm.astype(bf16)
                dv_scr[rows, lanes] = jnp.where(low, _dot(wt0, dmb), _dot(wt1, dmb))
                zero = jnp.zeros_like(dmb)
                dw0 = dw0 + _dot(jnp.where(low, dmb, zero), vp, NT)
                dw1 = dw1 + _dot(jnp.where(low, zero, dmb), vp, NT)
            dw_ref[2 * pr] += _tril(dw0)
            dw_ref[2 * pr + 1] += _tril(dw1)
        dv = dv_scr[...]
        acc_g[...] += _rowsum8(dv * yh)
        acc_bb[...] += _rowsum8(dv)
        gz = dv * gv
        dvr = rstd * (gz - jnp.mean(gz, axis=-1, keepdims=True) - yh * jnp.mean(gz * yh, axis=-1, keepdims=True))
        da_ref[:, SGU_CH:] = dvr.astype(bf16)

        @pl.when(i == n - 1)
        def _():
            ch = lax.broadcasted_iota(jnp.int32, (SGU_CH, LANE), 0) // HEAD_DIM
            hd = lax.broadcasted_iota(jnp.int32, (SGU_CH, LANE), 1)
            fold = jnp.where(ch == hd, 1.0, 0.0).astype(f32)
            db_ref[...] = jnp.dot(acc_bias[...], fold, preferred_element_type=f32, precision=lax.Precision.HIGHEST)
            dg_ref[...] = jnp.sum(acc_g[...], axis=0, keepdims=True)
            dbb_ref[...] = jnp.sum(acc_bb[...], axis=0, keepdims=True)

    vec = pl.BlockSpec((1, SGU_CH), lambda i: (0, 0))
    wsp = pl.BlockSpec((SGU_HEADS, BLK, BLK), lambda i: (0, 0, 0))
    return pl.pallas_call(
        body, name=name, grid=(n,),
        in_specs=[pl.BlockSpec((T, SGU_CH), lambda i: (i, U_BLK)), pl.BlockSpec((T, SGU_CH), lambda i: (i, SV_BLK)),
                  pl.BlockSpec((T, SGU_CH), lambda i: (i, dsb)), vec, vec, wsp, wsp,
                  pl.BlockSpec((BLK, SGU_CH), lambda i: (0, 0))],
        out_specs=[pl.BlockSpec((T, 2 * SGU_CH), lambda i: (i, 0)), wsp,
                   pl.BlockSpec((BLK, LANE), lambda i: (0, 0)), vec, vec],
        out_shape=[jax.ShapeDtypeStruct((S, 2 * SGU_CH), bf16), jax.ShapeDtypeStruct((SGU_HEADS, BLK, BLK), f32),
                   jax.ShapeDtypeStruct((BLK, LANE), f32), jax.ShapeDtypeStruct((1, SGU_CH), f32),
                   jax.ShapeDtypeStruct((1, SGU_CH), f32)],
        scratch_shapes=[pltpu.VMEM((T, SGU_CH), f32), pltpu.VMEM((BLK, SGU_CH), f32),
                        pltpu.VMEM((8, SGU_CH), f32), pltpu.VMEM((8, SGU_CH), f32)],
        compiler_params=_params("arbitrary"),
    )(p, p, dcat, lng, lnb, w, wt, bias)


HBM = pl.BlockSpec(memory_space=pltpu.HBM)
SEM = pl.BlockSpec(memory_space=pltpu.SEMAPHORE)
ANY = pl.BlockSpec(memory_space=pl.ANY)
EFFECT = pltpu.SideEffectType.DATAFLOW_SIDE_EFFECTING


def _flip(x, y, c, k):
    px, py, pc = x ^ (k >> 2), y ^ ((k >> 1) & 1), c ^ (k & 1)
    return (px, py, pc), 4 * px + 2 * py + pc


def _routes_gather(x, y, c):
    me = 4 * x + 2 * y + c
    out = []
    for k in (1, 2, 4, 6):
        dev, idx = _flip(x, y, c, k)
        out.append((dev, None, me, idx))
    return out


def _routes_pair(x, y, c):
    dev, _ = _flip(x, y, c, 1)
    return [(dev, 2 * q + (1 - c), q, q) for q in range(N_DEV // 2)]


def _routes_chips(x, y, c):
    out = []
    for k in (2, 4, 6):
        dev, idx = _flip(x, y, c, k)
        out.append((dev, idx // 2, 2 * x + y, idx // 2))
    return out


def _routes_forward(x, y, c):
    sib, _ = _flip(x, y, c, 1)
    out = []
    for k in (2, 4, 6):
        _, idx = _flip(x, y, c, k)
        out.append((sib, idx, idx, idx ^ 1))
    return out


def _routes_all(x, y, c):
    me = 4 * x + 2 * y + c
    out = []
    for k in range(1, N_DEV):
        dev, idx = _flip(x, y, c, k)
        out.append((dev, None, me, idx))
    return out


def _slot(ref, slot, kind):
    if slot is None:
        return ref
    if kind == "cols":
        width = ref.shape[2] // (N_DEV // 2)
        return ref.at[slot // (N_DEV // 2), :, pl.ds(pl.multiple_of((slot % (N_DEV // 2)) * width, LANE), width)]
    return ref.at[slot]


def _copies(routes, srcs, lands, send_sems, recv_sems, incoming, src_kinds, land_kinds):
    x, y, c = lax.axis_index("x"), lax.axis_index("y"), lax.axis_index("c")
    out = []
    n = len(lands)
    if srcs is None:
        srcs, src_kinds = lands, land_kinds
    for k, (dev, src_slot, dst_slot, recv_slot) in enumerate(routes(x, y, c)):
        for a in range(n):
            out.append(pltpu.make_async_remote_copy(
                src_ref=_slot(srcs[a], src_slot, src_kinds[a]),
                dst_ref=_slot(lands[a], recv_slot if incoming else dst_slot, land_kinds[a]),
                send_sem=send_sems.at[k * n + a], recv_sem=recv_sems.at[k * n + a], device_id=dev, device_id_type=MESH))
    return out


def _pin(a):
    return pltpu.with_memory_space_constraint(a, pltpu.HBM)


def split_start(srcs, lands, routes, name, deps=(), src_kinds=None, land_kinds=None):
    n = len(lands)
    ns = 0 if srcs is None else n
    n_routes = len(routes(0, 0, 0))
    ops = ([] if srcs is None else list(srcs)) + list(lands)
    src_kinds = src_kinds or ["rows"] * n
    land_kinds = land_kinds or ["rows"] * n

    def body(*refs):
        src, land = (refs[:n] if ns else None), refs[ns:ns + n]
        first_out = ns + n + len(deps)
        send_sems, recv_sems, token = refs[first_out], refs[first_out + 1], refs[-1]
        for cp in _copies(routes, src, land, send_sems, recv_sems, False, src_kinds, land_kinds):
            cp.start()
        token[...] = jnp.zeros_like(token)

    thru = [pltpu.HBM(a.shape, a.dtype) for a in ops]
    res = pl.pallas_call(
        body, name=name,
        out_shape=(pltpu.SemaphoreType.DMA((n * n_routes,)), pltpu.SemaphoreType.DMA((n * n_routes,)), *thru,
                   jax.ShapeDtypeStruct((8, LANE), f32)),
        in_specs=[HBM] * len(ops) + [ANY] * len(deps),
        out_specs=(SEM, SEM, *([HBM] * len(ops)), pl.BlockSpec(memory_space=pltpu.VMEM)),
        input_output_aliases={i: 2 + i for i in range(len(ops))},
        compiler_params=pltpu.CompilerParams(has_side_effects=EFFECT),
    )(*[_pin(a) for a in ops], *deps)
    return (res[0], res[1], (list(res[2:2 + n]) if ns else None), list(res[2 + ns:2 + ns + n]), res[-1],
            (src_kinds, land_kinds))


def split_wait(started, after, routes, name):
    send_sems, recv_sems, srcs, lands, _, (src_kinds, land_kinds) = started
    n = len(lands)
    ns = 0 if srcs is None else n
    ops = ([] if srcs is None else list(srcs)) + list(lands)
    afters = list(after) if isinstance(after, (list, tuple)) else [after]

    def body(*refs):
        src, land = (refs[:n] if ns else None), refs[ns:ns + n]
        send_s, recv_s = refs[ns + n], refs[ns + n + 1]
        for cp in _copies(routes, src, land, send_s, recv_s, True, src_kinds, land_kinds):
            cp.wait_send()
            cp.wait_recv()

    thru = [pltpu.HBM(a.shape, a.dtype) for a in ops]
    res = pl.pallas_call(
        body, name=name, out_shape=tuple(thru),
        in_specs=[HBM] * len(ops) + [SEM, SEM] + [ANY] * len(afters), out_specs=tuple([HBM] * len(ops)),
        input_output_aliases={i: i for i in range(len(ops))},
        compiler_params=pltpu.CompilerParams(has_side_effects=EFFECT),
    )(*ops, send_sems, recv_sems, *afters)
    return (list(res[:n]) if ns else None), list(res[ns:ns + n])


def chip_sum(parts, land, c_idx, kind, name):
    _, R, C = land.shape
    tr = R if R * C * 2 <= 3 * 2 ** 20 else _tile(R, 512)
    half = N_DEV // 2

    def body(c_ref, p_ref, l_ref, o_ref):
        o_ref[...] = (p_ref[...].astype(f32) + l_ref[...].astype(f32)).astype(bf16)

    if kind == "cols":
        mine = lambda q, i, c_ref: ((2 * q + c_ref[0]) // half, i, (2 * q + c_ref[0]) % half)
    else:
        mine = lambda q, i, c_ref: (2 * q + c_ref[0], i, 0)
    return pl.pallas_call(
        body, name=name,
        grid_spec=pltpu.PrefetchScalarGridSpec(
            num_scalar_prefetch=1, grid=(half, R // tr),
            in_specs=[pl.BlockSpec((None, tr, C), mine), pl.BlockSpec((None, tr, C), lambda q, i, c_ref: (q, i, 0))],
            out_specs=pl.BlockSpec((None, tr, C), lambda q, i, c_ref: (q, i, 0))),
        out_shape=jax.ShapeDtypeStruct((half, R, C), bf16),
        compiler_params=_params("parallel", "parallel"),
    )(c_idx, parts, land)


def place_own(land, src, me_idx, kind, name):
    R, C = src.shape
    tr = _tile(R, 512)
    half = N_DEV // 2
    if kind == "cols":
        where = lambda i, m: (m[0] // half, i, m[0] % half)
    else:
        where = lambda i, m: (m[0], i, 0)

    def body(m_ref, land_ref, src_ref, out_ref):
        out_ref[...] = src_ref[...]

    return pl.pallas_call(
        body, name=name,
        grid_spec=pltpu.PrefetchScalarGridSpec(
            num_scalar_prefetch=1, grid=(R // tr,),
            in_specs=[ANY, pl.BlockSpec((tr, C), lambda i, m: (i, 0))],
            out_specs=pl.BlockSpec((None, tr, C), where)),
        out_shape=jax.ShapeDtypeStruct(land.shape, land.dtype),
        input_output_aliases={1: 0},
        compiler_params=_params("arbitrary"),
    )(me_idx, land, src)


def sum_slots(parts, name):
    P, R, C = parts.shape
    tr = _tile(R, 512)

    def body(p_ref, o_ref):
        total = p_ref[0]
        for j in range(1, P):
            total = total + p_ref[j]
        o_ref[...] = total

    return pl.pallas_call(
        body, name=name, grid=(R // tr,),
        in_specs=[pl.BlockSpec((P, tr, C), lambda i: (0, i, 0))],
        out_specs=pl.BlockSpec((tr, C), lambda i: (i, 0)),
        out_shape=jax.ShapeDtypeStruct((R, C), f32),
        compiler_params=_params("parallel"),
    )(parts)


def adamw(parts, owns, chip, w, m, v, name):
    L, R, C = w.shape
    P = parts[0].shape[0]
    tr = _tile(R, 128 if C > 1024 else 256)
    nr = R // tr
    c1 = 1.0 - ADAM_B1 ** ADAM_STEP
    c2 = 1.0 - ADAM_B2 ** ADAM_STEP
    n_own = L if owns is not None else 0

    def body(chip_ref, *refs):
        part_refs, own_refs = refs[:L], refs[L:L + n_own]
        w_ref, m_ref, v_ref, g_out, d_out, m_out, v_out = refs[L + n_own:]
        layer = pl.program_id(0)
        for l in range(L):
            @pl.when(layer == l)
            def _(l=l):
                g = None
                for q in range(P):
                    term = part_refs[l][q].astype(f32)
                    if n_own:
                        term = jnp.where(chip_ref[0] == q, own_refs[l][...].astype(f32), term)
                    g = term if g is None else g + term
                mn = ADAM_B1 * m_ref[...] + (1.0 - ADAM_B1) * g
                vn = ADAM_B2 * v_ref[...] + (1.0 - ADAM_B2) * (g * g)
                g_out[...] = g
                m_out[...] = mn
                v_out[...] = vn
                d_out[...] = -ADAM_LR * ((mn / c1) / (jnp.sqrt(vn / c2) + ADAM_EPS) + ADAM_WD * w_ref[...])

    def rows(l, a, i):
        return jnp.where(a == l, i, jnp.where(a < l, 0, nr - 1))

    def part_spec(l):
        return pl.BlockSpec((P, tr, C), lambda a, i, chip_ref: (0, rows(l, a, i), 0))

    def own_spec(l):
        return pl.BlockSpec((None, tr, C), lambda a, i, chip_ref: (chip_ref[0], rows(l, a, i), 0))

    slab = pl.BlockSpec((None, tr, C), lambda a, i, chip_ref: (a, i, 0))
    out = jax.ShapeDtypeStruct((L, R, C), f32)
    return pl.pallas_call(
        body, name=name,
        grid_spec=pltpu.PrefetchScalarGridSpec(
            num_scalar_prefetch=1, grid=(L, nr),
            in_specs=[part_spec(l) for l in range(L)] + [own_spec(l) for l in range(n_own)] + [slab, slab, slab],
            out_specs=[slab, slab, slab, slab]),
        out_shape=[out, out, out, out],
        compiler_params=_params("arbitrary", "arbitrary"),
    )(chip, *parts, *(owns or []), w, m, v)


PACK = 8 * LANE
PACK_ROWS = 256


def _pack(arrs):
    pieces = []
    for a in arrs:
        flat = a.astype(f32).reshape(-1)
        pad = (-flat.shape[0]) % PACK
        pieces.append(jnp.pad(flat, (0, pad)).reshape(-1, LANE))
    rows = sum(p.shape[0] for p in pieces)
    if rows > PACK_ROWS and rows % PACK_ROWS:
        pieces.append(jnp.zeros((PACK_ROWS - rows % PACK_ROWS, LANE), f32))
    return jnp.concatenate(pieces, axis=0)


def _unpack(buf, shapes):
    out, row = [], 0
    for shp in shapes:
        size = math.prod(shp)
        rows = (size + PACK - 1) // PACK * (PACK // LANE)
        out.append(buf[row:row + rows].reshape(-1)[:size].reshape(shp))
        row += rows
    return out


def _rope_tables(positions):
    half = ROT_DIM // 2
    inv_freq = 1.0 / (ROPE_THETA ** (jnp.arange(0, ROT_DIM, 2, dtype=f32) / ROT_DIM))
    ang = positions.astype(f32)[:, None] * inv_freq
    cos, sin = jnp.cos(ang), jnp.sin(ang)
    S = positions.shape[0]
    zeros, ones = jnp.zeros((S, half), f32), jnp.ones((S, HEAD_DIM - ROT_DIM), f32)
    rest = jnp.zeros((S, HEAD_DIM - ROT_DIM), f32)
    c = jnp.concatenate([cos, cos, ones], axis=1)
    s1 = jnp.concatenate([zeros, sin, rest], axis=1)
    s2 = jnp.concatenate([-sin, zeros, rest], axis=1)
    return tuple(jnp.tile(t, (1, LANE // HEAD_DIM)) for t in (c, s1, s2))


def _cols_to_shards(g):
    lead, (R, N) = g.shape[:-2], g.shape[-2:]
    g = g.reshape(lead + (R, N_DEV, N // N_DEV))
    return jnp.moveaxis(g, -2, 0)


def _shards_to_cols(g):
    g = jnp.moveaxis(g, 0, -2)
    return g.reshape(g.shape[:-2] + (g.shape[-2] * g.shape[-1],))


def kernel(x, positions, norm_ffn1, ffn1_w_in, ffn1_w_out, norm_mix, w_in, conv_dw_w, conv_dw_b, conv_ln_g, conv_ln_b, sgu_ln_g, sgu_ln_b, sgu_w, sgu_b, attn_sinks, w_out, norm_ffn2, ffn2_w_in, ffn2_w_out, final_norm, loss_target, m_norm_ffn1, m_ffn1_w_in, m_ffn1_w_out, m_norm_mix, m_w_in, m_conv_dw_w, m_conv_dw_b, m_conv_ln_g, m_conv_ln_b, m_sgu_ln_g, m_sgu_ln_b, m_sgu_w, m_sgu_b, m_attn_sinks, m_w_out, m_norm_ffn2, m_ffn2_w_in, m_ffn2_w_out, m_final_norm, v_norm_ffn1, v_ffn1_w_in, v_ffn1_w_out, v_norm_mix, v_w_in, v_conv_dw_w, v_conv_dw_b, v_conv_ln_g, v_conv_ln_b, v_sgu_ln_g, v_sgu_ln_b, v_sgu_w, v_sgu_b, v_attn_sinks, v_w_out, v_norm_ffn2, v_ffn2_w_in, v_ffn2_w_out, v_final_norm):
    L = norm_ffn1.shape[0]
    S, D = x.shape[1], x.shape[2]
    F = ffn1_w_out.shape[1] * N_DEV
    me = 4 * lax.axis_index("x") + 2 * lax.axis_index("y") + lax.axis_index("c")
    x0 = x[0]
    rope_c, rope_s1, rope_s2 = _rope_tables(positions[0])
    cw = CONV_CH // N_DEV

    c_idx = lax.axis_index("c").astype(jnp.int32).reshape(1)
    chip = (2 * lax.axis_index("x") + lax.axis_index("y")).astype(jnp.int32).reshape(1)
    no_chip = jnp.zeros((1,), jnp.int32)
    me_idx = me.astype(jnp.int32).reshape(1)

    row = lambda a: a.reshape(1, -1)
    order_fwd = [(l, g) for l in range(L) for g in (("ffn1_in", "ffn1_out") if l == 0 else ("ffn1",)) + ("mix", "ffn2")]

    def group_srcs(l, grp):
        if grp == "mix":
            taps = jnp.pad(conv_dw_w[l], ((0, HALO - CONV_W), (0, LANE - cw)))
            return [w_in[l].astype(bf16), w_out[l].astype(bf16), taps]
        both = ([ffn2_w_in[l], ffn2_w_out[l]] if grp == "ffn2" else [ffn1_w_in[l], ffn1_w_out[l]])
        both = [a.astype(bf16) for a in both]
        return both[:1] if grp == "ffn1_in" else both[1:] if grp == "ffn1_out" else both

    def kinds_of(grp):
        return {"mix": ["rows"] * 3, "ffn1_in": ["cols"], "ffn1_out": ["rows"]}.get(grp, ["cols", "rows"])

    def gather_start(k, deps=()):
        l, grp = order_fwd[k]
        srcs = group_srcs(l, grp)
        lands = [lax.empty((2, D, F) if kind == "cols" else (N_DEV,) + a.shape, a.dtype)
                 for a, kind in zip(srcs, kinds_of(grp))]
        return split_start(srcs, lands, _routes_gather, f"gather_start_{grp}_{l}", deps, land_kinds=kinds_of(grp))

    def gather_forward(k, started, after):
        l, grp = order_fwd[k]
        srcs, lands = split_wait(started, after, _routes_gather, f"gather_wait_{grp}_{l}")
        return srcs, split_start(None, lands, _routes_forward, f"forward_start_{grp}_{l}", land_kinds=kinds_of(grp))

    def gather_finish(k, srcs, started, after):
        l, grp = order_fwd[k]
        _, lands = split_wait(started, after, _routes_forward, f"forward_wait_{grp}_{l}")
        full = [place_own(ld, s, me_idx, kind, f"own_{grp}_{l}_{a}")
                for a, (ld, s, kind) in enumerate(zip(lands, srcs, kinds_of(grp)))]
        if grp == "mix":
            return dict(w_in=_shards_to_cols(full[0]), w_out=full[1].reshape(D, D),
                        taps=_shards_to_cols(full[2][:, :, :cw]))
        named = dict(zip(["w_in", "w_out"] if len(full) == 2 else ["w_in" if grp == "ffn1_in" else "w_out"], full))
        if "w_out" in named:
            named["w_out"] = named["w_out"].reshape(F, D)
        return named

    def ffn_head(xs, wts, g_norm, tag, sv):
        sv["x_in"] = xs
        h, sv["ht"] = rmsnorm_fwd(xs, g_norm, f"norm_{tag}")
        sv["gu"], a, sv["at"] = ffn_in(h, wts["w_in"], f"{tag}_in")
        return a

    def ffn_fwd(xs, wts, g_norm, tag, sv, mid):
        a = ffn_head(xs, wts, g_norm, tag, sv)
        return mm_res(a, wts["w_out"], xs, FFN_RES, f"{tag}_out", deps=mid(a))

    def mix_fwd(xs, wts, l, sv, mid):
        sv["x_in"] = xs
        h, sv["ht"] = rmsnorm_fwd(xs, row(norm_mix[l]), f"norm_mix_{l}")
        p = mm_nn(h, wts["w_in"][None], f"mix_in_{l}")[0]
        sv["p"] = p
        attn = attn_fwd(p, rope_c, rope_s1, rope_s2, attn_sinks[l], f"attn_fwd_{l}")
        conv, sv["conv_y"] = conv_fwd(p, wts["taps"], row(conv_dw_b[l]), row(conv_ln_g[l]), row(conv_ln_b[l]),
                                      f"conv_fwd_{l}")
        sv["sgu_bias"] = jnp.repeat(sgu_b[l].T, HEAD_DIM, axis=1)
        sgu = sgu_fwd(p, row(sgu_ln_g[l]), row(sgu_ln_b[l]), sgu_w[l], sv["sgu_bias"], f"sgu_fwd_{l}")
        cat = jnp.concatenate([attn, conv, sgu], axis=1)
        sv["catt"] = cat.T
        return mm_res(cat, wts["w_out"], xs, 1.0, f"mix_out_{l}", deps=mid(cat))

    weights, saved = {}, {}
    xs = x0
    starts = []
    for k in range(len(order_fwd)):
        starts.append(gather_start(k, (starts[-1][4],) if starts else ()))
    state = dict(zip(("srcs", "fwd"), gather_forward(0, starts[0], starts[-1][4])))
    for k, (l, grp) in enumerate(order_fwd):
        nxt = starts[k + 1] if k + 1 < len(order_fwd) else None
        wts = gather_finish(k, state["srcs"], state["fwd"], state["fwd"][4])

        def mid(after, k=k, nxt=nxt):
            if not nxt:
                return ()
            state["srcs"], state["fwd"] = gather_forward(k + 1, nxt, after)
            return (state["fwd"][4],)

        if grp == "ffn1_in":
            sv = saved[l, "ffn1"] = {}
            weights[l, "ffn1"] = wts
            head = ffn_head(xs, wts, row(norm_ffn1[l]), f"ffn1_{l}", sv)
            mid(head)
        elif grp == "ffn1_out":
            weights[l, "ffn1"].update(wts)
            xs = mm_res(head, wts["w_out"], xs, FFN_RES, f"ffn1_{l}_out")
            mid(xs)
        else:
            sv = saved[l, grp] = {}
            weights[l, grp] = wts
            if grp == "mix":
                xs = mix_fwd(xs, wts, l, sv, mid)
            else:
                xs = ffn_fwd(xs, wts, row(norm_ffn1[l] if grp == "ffn1" else norm_ffn2[l]), f"{grp}_{l}", sv, mid)

    dx, dxb, d_final_norm, loss = final_loss(xs, row(final_norm), loss_target[0], "final_loss")

    def scatter_start(grads, kinds, tag, deps=()):
        half = N_DEV // 2
        lands = [lax.empty((half, g.shape[1], g.shape[2] // half) if kind == "cols" else (half,) + g.shape[1:], g.dtype)
                 for g, kind in zip(grads, kinds)]
        return split_start(grads, lands, _routes_pair, f"pair_start_{tag}", deps, src_kinds=kinds)

    def pair_to_chips(started, after, tag):
        kinds = started[5][0]
        grads, landed = split_wait(started, after, _routes_pair, f"pair_wait_{tag}")
        sums = [chip_sum(g, ld, c_idx, kind, f"chip_sum_{tag}_{a}")
                for a, (g, ld, kind) in enumerate(zip(grads, landed, kinds))]
        return split_start(sums, [lax.empty(s.shape, s.dtype) for s in sums], _routes_chips, f"chips_start_{tag}")

    def ffn_bwd(dx, dxb, wts, sv, g_norm, tag, deps=(), out_first=False):
        dgu = ffn_dact(dxb, wts["w_out"], sv["gu"], f"{tag}_dact")
        d_w_out = mm_nn(sv["at"], dxb[None], f"{tag}_dwout", bf16, FFN_RES, deps=deps)[0]
        d_w_out = d_w_out.reshape(N_DEV, F // N_DEV, D)
        if not out_first:
            d_w_in = mm_nn(sv["ht"], dgu, f"{tag}_dwin", bf16)
            pair = scatter_start([d_w_in, d_w_out], ["cols", "rows"], tag)
            dh = mm_nt(dgu, wts["w_in"], f"{tag}_dh", deps=(pair[4],))
            chips = [pair_to_chips(pair, dh, tag)]
        else:
            pair_out = scatter_start([d_w_out], ["rows"], f"{tag}_out")
            d_w_in = mm_nn(sv["ht"], dgu, f"{tag}_dwin", bf16, deps=(pair_out[4],))
            chips_out = pair_to_chips(pair_out, d_w_in, f"{tag}_out")
            pair_in = scatter_start([d_w_in], ["cols"], f"{tag}_in", deps=(chips_out[4],))
            dh = mm_nt(dgu, wts["w_in"], f"{tag}_dh", deps=(pair_in[4],))
            chips = [pair_to_chips(pair_in, dh, f"{tag}_in"), chips_out]
        dx, dxb, dg = rmsnorm_bwd(dh, sv["x_in"], g_norm, dx, f"{tag}_dnorm", deps=(chips[0][4],))
        return dx, dxb, dg, chips

    small = [None] * L
    chips_pending = {}
    for l in reversed(range(L)):
        dx, dxb, d_norm_ffn2, chips_pending[l, "ffn2"] = ffn_bwd(
            dx, dxb, weights[l, "ffn2"], saved[l, "ffn2"], row(norm_ffn2[l]), f"ffn2_{l}")

        wts, sv = weights[l, "mix"], saved[l, "mix"]
        d_w_out = mm_nn(sv["catt"], dxb[None], f"mix_dwout_{l}", bf16)[0]
        dcat = mm_nt(dxb[None], wts["w_out"][None], f"mix_dcat_{l}", deps=(d_w_out,))
        p = sv["p"]
        dq, dkv, d_sinks = attn_bwd(p, dcat, rope_c, rope_s1, rope_s2, attn_sinks[l], f"attn_bwd_{l}")
        da_conv, d_taps, d_conv_b, d_conv_g, d_conv_bb = conv_bwd(
            p, sv["conv_y"], dcat, wts["taps"], row(conv_ln_g[l]), row(conv_ln_b[l]), f"conv_bwd_{l}")
        da_sgu, d_sgu_w, d_sgu_bias, d_sgu_g, d_sgu_bb = sgu_bwd(
            p, dcat, row(sgu_ln_g[l]), row(sgu_ln_b[l]), sgu_w[l], jnp.swapaxes(sgu_w[l], 1, 2), sv["sgu_bias"],
            f"sgu_bwd_{l}")
        dp = jnp.concatenate([dq, dkv, da_conv, da_sgu], axis=1)
        d_w_in = mm_nn(sv["ht"], dp[None], f"mix_dwin_{l}", bf16)[0]
        pair = scatter_start([_cols_to_shards(d_w_in), d_w_out.reshape(N_DEV, D // N_DEV, D)], ["rows", "rows"],
                             f"mix_{l}")
        dh = mm_nt(dp[None], wts["w_in"][None], f"mix_dh_{l}", deps=(pair[4],))
        chips_pending[l, "mix"] = [pair_to_chips(pair, dh, f"mix_{l}")]
        dx, dxb, d_norm_mix = rmsnorm_bwd(dh, sv["x_in"], row(norm_mix[l]), dx, f"mix_dnorm_{l}",
                                          deps=(chips_pending[l, "mix"][0][4],))

        small[l] = dict(norm_mix=d_norm_mix[0], conv_dw_w=d_taps[:CONV_W],
                        conv_dw_b=d_conv_b[0], conv_ln_g=d_conv_g[0], conv_ln_b=d_conv_bb[0], sgu_ln_g=d_sgu_g[0],
                        sgu_ln_b=d_sgu_bb[0], sgu_w=d_sgu_w, sgu_b=d_sgu_bias[:, :SGU_HEADS].T,
                        attn_sinks=d_sinks[0, :N_Q_HEADS], norm_ffn2=d_norm_ffn2[0])
        if l == 0:
            early_names = ["norm_mix", "conv_dw_w", "conv_dw_b", "conv_ln_g", "conv_ln_b", "sgu_ln_g", "sgu_ln_b",
                           "sgu_w", "sgu_b", "attn_sinks", "norm_ffn2"]
            early = [jnp.stack([small[k][n] for k in range(L)]) for n in early_names]
            early += [d_final_norm[0], loss.reshape(1)]
            early_shapes = [a.shape for a in early]
            early = _pack(early)
            early_pending = split_start([early], [lax.empty((N_DEV,) + early.shape, f32)], _routes_all,
                                        "small_start", deps=(dxb,))
            early_token = (early_pending[4],)
        else:
            early_token = ()

        dx, dxb, d_norm_ffn1, chips_pending[l, "ffn1"] = ffn_bwd(
            dx, dxb, weights[l, "ffn1"], saved[l, "ffn1"], row(norm_ffn1[l]), f"ffn1_{l}", early_token, l == 0)
        small[l]["norm_ffn1"] = d_norm_ffn1[0]

    grad_x = dx[None]
    late = _pack([jnp.stack([small[l]["norm_ffn1"] for l in range(L)])])
    late_pending = split_start([late], [lax.empty((N_DEV,) + late.shape, f32)], _routes_all, "late_start", deps=(dx,))

    def landed(grp, after):
        sums, lands = [], []
        for l in range(L):
            got = [split_wait(st, after, _routes_chips, f"chips_wait_{grp}_{l}_{a}")
                   for a, st in enumerate(chips_pending[l, grp])]
            sums.append([s for g in got for s in g[0]])
            lands.append([s for g in got for s in g[1]])
        return sums, lands

    given = dict(norm_ffn1=(norm_ffn1, m_norm_ffn1, v_norm_ffn1), norm_mix=(norm_mix, m_norm_mix, v_norm_mix),
                 conv_dw_w=(conv_dw_w, m_conv_dw_w, v_conv_dw_w), conv_dw_b=(conv_dw_b, m_conv_dw_b, v_conv_dw_b),
                 conv_ln_g=(conv_ln_g, m_conv_ln_g, v_conv_ln_g), conv_ln_b=(conv_ln_b, m_conv_ln_b, v_conv_ln_b),
                 sgu_ln_g=(sgu_ln_g, m_sgu_ln_g, v_sgu_ln_g), sgu_ln_b=(sgu_ln_b, m_sgu_ln_b, v_sgu_ln_b),
                 sgu_w=(sgu_w, m_sgu_w, v_sgu_w), sgu_b=(sgu_b, m_sgu_b, v_sgu_b),
                 attn_sinks=(attn_sinks, m_attn_sinks, v_attn_sinks), norm_ffn2=(norm_ffn2, m_norm_ffn2, v_norm_ffn2),
                 final_norm=(final_norm, m_final_norm, v_final_norm))

    def small_update(pending, after, names, shapes, tag):
        (own,), (others,) = split_wait(pending, after, _routes_all, f"{tag}_wait")
        total = _unpack(sum_slots(place_own(others, own, me_idx, "rows", f"{tag}_own"), f"{tag}_sum"), shapes)
        g = dict(zip(names, total))
        if "conv_dw_w" in g:
            g["conv_dw_w"] = lax.dynamic_slice_in_dim(g["conv_dw_w"], me * cw, cw, axis=2)
        upd_names = [n for n in names if n in given]
        upd_shapes = [given[n][0].shape for n in upd_names]
        packed = [_pack([g[n] for n in upd_names])[None]] + [_pack([given[n][k] for n in upd_names])[None]
                                                              for k in range(3)]
        res = adamw([packed[0]], None, no_chip, packed[1], packed[2], packed[3], f"{tag}_adamw")
        return g, [dict(zip(upd_names, _unpack(r[0], upd_shapes))) for r in res], res[0]

    big = {}
    done = [dx]

    def big_update(grp, names):
        sums, lands = landed(grp, done)
        for idx, (name, w, m, v) in enumerate(names):
            big[name] = adamw([lands[l][idx] for l in range(L)], [sums[l][idx] for l in range(L)], chip, w, m, v,
                              f"adamw_{name}")
            done.append(big[name][0])

    big_update("ffn2", (("ffn2_w_in", ffn2_w_in, m_ffn2_w_in, v_ffn2_w_in),
                        ("ffn2_w_out", ffn2_w_out, m_ffn2_w_out, v_ffn2_w_out)))
    big_update("mix", (("w_in", w_in, m_w_in, v_w_in), ("w_out", w_out, m_w_out, v_w_out)))
    g_early, upd_early, marker = small_update(early_pending, done, early_names + ["final_norm", "loss"], early_shapes,
                                              "small")
    done.append(marker)
    big_update("ffn1", (("ffn1_w_in", ffn1_w_in, m_ffn1_w_in, v_ffn1_w_in),
                        ("ffn1_w_out", ffn1_w_out, m_ffn1_w_out, v_ffn1_w_out)))
    _, upd_late, _ = small_update(late_pending, done, ["norm_ffn1"], [(L, D)], "late")
    upd = [{**upd_early[k], **upd_late[k]} for k in range(4)]

    order = ["norm_ffn1", "ffn1_w_in", "ffn1_w_out", "norm_mix", "w_in", "conv_dw_w", "conv_dw_b", "conv_ln_g",
             "conv_ln_b", "sgu_ln_g", "sgu_ln_b", "sgu_w", "sgu_b", "attn_sinks", "w_out", "norm_ffn2", "ffn2_w_in",
             "ffn2_w_out", "final_norm"]
    outs = [g_early["loss"].reshape(()), grad_x]
    for k in range(4):
        outs += [big[n][k] if n in big else upd[k][n] for n in order]
    return tuple(outs)
```

```python
import functools
import math

import jax
import jax.numpy as jnp
from jax import lax
from jax.experimental import pallas as pl
from jax.experimental.pallas import tpu as pltpu

f32 = jnp.float32
bf16 = jnp.bfloat16

N_DEV = 8
HEAD_DIM = 64
N_Q_HEADS = 16
N_KV_HEADS = 4
GQ = N_Q_HEADS // N_KV_HEADS
BLK = 128
ROT_DIM = 16
ROPE_THETA = 500000.0
CONV_W = 31
CONV_CH = 512
SGU_CH = 512
SGU_HEADS = 8
Q_END = N_Q_HEADS * HEAD_DIM
KV_W = 2 * N_KV_HEADS * HEAD_DIM
V_END = Q_END + KV_W
IN_COLS = V_END + 2 * CONV_CH + 2 * SGU_CH
HALO = 32
NORM_EPS = 1e-5
FFN_RES = 0.5
ADAM_LR, ADAM_B1, ADAM_B2, ADAM_EPS, ADAM_WD, ADAM_STEP = 0.001, 0.9, 0.999, 1e-08, 0.01, 10
LANE = 128
VMEM_LIMIT = 56 * 2 ** 20
MM_ROWS = 1024
MM_BLOCK_BYTES = 16 * 2 ** 20
MESH = pl.DeviceIdType.MESH

NN = (((1,), (0,)), ((), ()))
NT = (((1,), (1,)), ((), ()))
TN = (((0,), (0,)), ((), ()))


def _tile(dim, pref):
    t = min(pref, dim)
    while dim % t:
        t //= 2
    return t


def _row_tile(rows, k, most=2048):
    for n in range(1, 17):
        t = rows // n
        if rows % n == 0 and t % LANE == 0 and t <= most and t * k * 2 <= MM_BLOCK_BYTES:
            return t
    return _tile(rows, 512)


def _params(*sem):
    return pltpu.CompilerParams(dimension_semantics=sem, vmem_limit_bytes=VMEM_LIMIT)


def _sig(x):
    return 1.0 / (1.0 + jnp.exp(-x))


def _dot(a, b, dims=NN):
    return lax.dot_general(a, b, dims, preferred_element_type=f32)


def _rowsum8(x):
    return x.reshape(x.shape[0] // 8, 8, x.shape[1]).sum(axis=0)


def rmsnorm_fwd(x, g, name, deps=()):
    S, D = x.shape
    tm = _tile(S, 512)

    def body(x_ref, g_ref, *rest):
        o_ref, ot_ref = rest[-2:]
        xv = x_ref[...]
        r = lax.rsqrt(jnp.mean(xv * xv, axis=-1, keepdims=True) + NORM_EPS)
        hb = (xv * r * g_ref[...]).astype(bf16)
        o_ref[...] = hb
        ot_ref[...] = hb.T

    return pl.pallas_call(
        body, name=name, grid=(S // tm,),
        in_specs=[pl.BlockSpec((tm, D), lambda i: (i, 0)), pl.BlockSpec((1, D), lambda i: (0, 0))] + [ANY] * len(deps),
        out_specs=[pl.BlockSpec((tm, D), lambda i: (i, 0)), pl.BlockSpec((D, tm), lambda i: (0, i))],
        out_shape=[jax.ShapeDtypeStruct((S, D), bf16), jax.ShapeDtypeStruct((D, S), bf16)],
        compiler_params=_params("parallel"),
    )(x, g, *deps)


def rmsnorm_bwd(dh, x, g, dres, name, deps=()):
    S, D = x.shape
    tm = _tile(S, 256)
    n = S // tm

    def body(dh_ref, x_ref, g_ref, dres_ref, *rest):
        dx_ref, dxb_ref, dg_ref, acc = rest[-4:]
        i = pl.program_id(0)

        @pl.when(i == 0)
        def _():
            acc[...] = jnp.zeros_like(acc)

        xv = x_ref[...]
        r = lax.rsqrt(jnp.mean(xv * xv, axis=-1, keepdims=True) + NORM_EPS)
        xh = xv * r
        dy = dh_ref[...]
        gy = dy * g_ref[...]
        dx = dres_ref[...] + r * (gy - xh * jnp.mean(gy * xh, axis=-1, keepdims=True))
        dx_ref[...] = dx
        dxb_ref[...] = dx.astype(bf16)
        acc[...] += _rowsum8(dy * xh)

        @pl.when(i == n - 1)
        def _():
            dg_ref[...] = jnp.sum(acc[...], axis=0, keepdims=True)

    row = pl.BlockSpec((tm, D), lambda i: (i, 0))
    vec = pl.BlockSpec((1, D), lambda i: (0, 0))
    return pl.pallas_call(
        body, name=name, grid=(n,),
        in_specs=[row, row, vec, row] + [ANY] * len(deps),
        out_specs=[row, row, vec],
        out_shape=[jax.ShapeDtypeStruct((S, D), f32), jax.ShapeDtypeStruct((S, D), bf16),
                   jax.ShapeDtypeStruct((1, D), f32)],
        scratch_shapes=[pltpu.VMEM((8, D), f32)],
        compiler_params=_params("arbitrary"),
    )(dh, x, g, dres, *deps)


def final_loss(x, g, tgt, name):
    S, D = x.shape
    tm = _tile(S, 256)
    n = S // tm

    def body(x_ref, g_ref, t_ref, dx_ref, dxb_ref, dg_ref, loss_ref, acc):
        i = pl.program_id(0)

        @pl.when(i == 0)
        def _():
            acc[...] = jnp.zeros_like(acc)
            loss_ref[...] = jnp.zeros_like(loss_ref)

        xv = x_ref[...]
        gv = g_ref[...]
        r = lax.rsqrt(jnp.mean(xv * xv, axis=-1, keepdims=True) + NORM_EPS)
        xh = xv * r
        diff = xh * gv - t_ref[...]
        tok = jnp.mean(diff * diff, axis=-1, keepdims=True)
        loss_ref[...] += 0.5 * jnp.sum(tok, axis=0, keepdims=True)
        dy = diff / D
        gy = dy * gv
        dx = r * (gy - xh * jnp.mean(gy * xh, axis=-1, keepdims=True))
        dx_ref[...] = dx
        dxb_ref[...] = dx.astype(bf16)
        acc[...] += _rowsum8(dy * xh)

        @pl.when(i == n - 1)
        def _():
            dg_ref[...] = jnp.sum(acc[...], axis=0, keepdims=True)

    row = pl.BlockSpec((tm, D), lambda i: (i, 0))
    vec = pl.BlockSpec((1, D), lambda i: (0, 0))
    return pl.pallas_call(
        body, name=name, grid=(n,),
        in_specs=[row, vec, row],
        out_specs=[row, row, vec, pl.BlockSpec((1, 1), lambda i: (0, 0))],
        out_shape=[jax.ShapeDtypeStruct((S, D), f32), jax.ShapeDtypeStruct((S, D), bf16),
                   jax.ShapeDtypeStruct((1, D), f32), jax.ShapeDtypeStruct((1, 1), f32)],
        scratch_shapes=[pltpu.VMEM((8, D), f32)],
        compiler_params=_params("arbitrary"),
    )(x, g, tgt)


def ffn_in(h, w2, name):
    S, D = h.shape
    F = w2.shape[2]
    tm, tn = _tile(S, MM_ROWS), _tile(F, 512)

    def body(h_ref, w_ref, gu_ref, a_ref, at_ref):
        hv = h_ref[...]
        g = _dot(hv, w_ref[0])
        u = _dot(hv, w_ref[1])
        gu_ref[0] = g.astype(bf16)
        gu_ref[1] = u.astype(bf16)
        a = (g * _sig(g) * u).astype(bf16)
        a_ref[...] = a
        at_ref[...] = a.T

    return pl.pallas_call(
        body, name=name, grid=(S // tm, F // tn),
        in_specs=[pl.BlockSpec((tm, D), lambda i, j: (i, 0)), pl.BlockSpec((2, D, tn), lambda i, j: (0, 0, j))],
        out_specs=[pl.BlockSpec((2, tm, tn), lambda i, j: (0, i, j)), pl.BlockSpec((tm, tn), lambda i, j: (i, j)),
                   pl.BlockSpec((tn, tm), lambda i, j: (j, i))],
        out_shape=[jax.ShapeDtypeStruct((2, S, F), bf16), jax.ShapeDtypeStruct((S, F), bf16),
                   jax.ShapeDtypeStruct((F, S), bf16)],
        compiler_params=_params("parallel", "parallel"),
    )(h, w2)


def mm_res(a, w, x, scale, name, deps=()):
    S, K = a.shape
    N = w.shape[1]
    tm, tn = _row_tile(S, K, MM_ROWS), _tile(N, 512)

    def body(a_ref, w_ref, x_ref, *rest):
        rest[-1][...] = x_ref[...] + scale * _dot(a_ref[...], w_ref[...])

    return pl.pallas_call(
        body, name=name, grid=(S // tm, N // tn),
        in_specs=[pl.BlockSpec((tm, K), lambda i, j: (i, 0)), pl.BlockSpec((K, tn), lambda i, j: (0, j)),
                  pl.BlockSpec((tm, tn), lambda i, j: (i, j))] + [ANY] * len(deps),
        out_specs=pl.BlockSpec((tm, tn), lambda i, j: (i, j)),
        out_shape=jax.ShapeDtypeStruct((S, N), f32),
        compiler_params=_params("parallel", "parallel"),
    )(a, w, x, *deps)


def mm_nn(a, b, name, out_dtype=f32, scale=1.0, deps=()):
    M, K = a.shape
    G, _, N = b.shape
    tm, tn = _row_tile(M, K), _tile(N, 512)

    def body(a_ref, b_ref, *rest):
        acc = _dot(a_ref[...], b_ref[...])
        rest[-1][...] = (acc if scale == 1.0 else scale * acc).astype(out_dtype)

    return pl.pallas_call(
        body, name=name, grid=(G, M // tm, N // tn),
        in_specs=[pl.BlockSpec((tm, K), lambda g, i, j: (i, 0)),
                  pl.BlockSpec((None, K, tn), lambda g, i, j: (g, 0, j))] + [ANY] * len(deps),
        out_specs=pl.BlockSpec((None, tm, tn), lambda g, i, j: (g, i, j)),
        out_shape=jax.ShapeDtypeStruct((G, M, N), out_dtype),
        compiler_params=_params("parallel", "parallel", "parallel"),
    )(a, b, *deps)


def mm_nt(a, w, name, deps=()):
    G, S, K = a.shape
    N = w.shape[1]
    tm, tn = _row_tile(S, K), _tile(N, 512)

    def body(a_ref, w_ref, *rest):
        o_ref = rest[-1]
        part = _dot(a_ref[...], w_ref[...], NT)
        if G == 1:
            o_ref[...] = part
        else:
            g = pl.program_id(2)

            @pl.when(g == 0)
            def _():
                o_ref[...] = part

            @pl.when(g > 0)
            def _():
                o_ref[...] += part

    return pl.pallas_call(
        body, name=name, grid=(S // tm, N // tn, G),
        in_specs=[pl.BlockSpec((None, tm, K), lambda i, j, g: (g, i, 0)),
                  pl.BlockSpec((None, tn, K), lambda i, j, g: (g, j, 0))] + [ANY] * len(deps),
        out_specs=pl.BlockSpec((tm, tn), lambda i, j, g: (i, j)),
        out_shape=jax.ShapeDtypeStruct((S, N), f32),
        compiler_params=_params("parallel", "parallel", "arbitrary"),
    )(a, w, *deps)


def ffn_dact(dx, wout, gu, name):
    S, D = dx.shape
    F = wout.shape[0]
    tm, tn = _row_tile(S, D), _tile(F, 512)

    def body(dx_ref, w_ref, gu_ref, o_ref):
        da = FFN_RES * _dot(dx_ref[...], w_ref[...], NT)
        g = gu_ref[0].astype(f32)
        u = gu_ref[1].astype(f32)
        sg = _sig(g)
        o_ref[0] = (da * u * (sg * (1.0 + g * (1.0 - sg)))).astype(bf16)
        o_ref[1] = (da * (g * sg)).astype(bf16)

    return pl.pallas_call(
        body, name=name, grid=(S // tm, F // tn),
        in_specs=[pl.BlockSpec((tm, D), lambda i, j: (i, 0)), pl.BlockSpec((tn, D), lambda i, j: (j, 0)),
                  pl.BlockSpec((2, tm, tn), lambda i, j: (0, i, j))],
        out_specs=pl.BlockSpec((2, tm, tn), lambda i, j: (0, i, j)),
        out_shape=jax.ShapeDtypeStruct((2, S, F), bf16),
        compiler_params=_params("parallel", "parallel"),
    )(dx, wout, gu)


def _rope(t, c, s1, s2):
    w = t.shape[1]
    return t * c + pltpu.roll(t, 8, 1) * s1 + pltpu.roll(t, w - 8, 1) * s2


def _rope_t(d, c, s1, s2):
    w = d.shape[1]
    return d * c + pltpu.roll(d * s1, w - 8, 1) + pltpu.roll(d * s2, 8, 1)


def _attn_mask(n):
    qi = lax.broadcasted_iota(jnp.int32, (BLK, 2 * BLK), 0)
    kj = lax.broadcasted_iota(jnp.int32, (BLK, 2 * BLK), 1)
    dist = qi + BLK - kj
    return (dist >= 0) & (dist < BLK) & ((kj >= BLK) | (n > 0))


def _softmax_sink(s, valid, sk):
    s = jnp.where(valid, s, -1e30)
    m = jnp.maximum(jnp.max(s, axis=-1, keepdims=True), sk)
    e = jnp.exp(s - m)
    es = jnp.exp(sk - m)
    inv = 1.0 / (jnp.sum(e, axis=-1, keepdims=True) + es)
    return e * inv, es * inv


def attn_fwd(p, rope_c, rope_s1, rope_s2, sinks, name):
    S = p.shape[0]
    nb = S // BLK
    kvb = Q_END // KV_W

    def body(sink_ref, q_ref, kvc_ref, kvp_ref, cc_ref, s1c_ref, s2c_ref, cp_ref, s1p_ref, s2p_ref, o_ref):
        n = pl.program_id(0)
        cc, s1c, s2c = cc_ref[...], s1c_ref[...], s2c_ref[...]
        cp, s1p, s2p = cp_ref[...], s1p_ref[...], s2p_ref[...]
        q = _rope(q_ref[...], jnp.tile(cc, (1, 8)), jnp.tile(s1c, (1, 8)), jnp.tile(s2c, (1, 8)))
        kc = _rope(kvc_ref[:, :256], jnp.tile(cc, (1, 2)), jnp.tile(s1c, (1, 2)), jnp.tile(s2c, (1, 2)))
        kp = _rope(kvp_ref[:, :256], jnp.tile(cp, (1, 2)), jnp.tile(s1p, (1, 2)), jnp.tile(s2p, (1, 2)))
        k = jnp.concatenate([kp, kc], axis=0).astype(bf16)
        v = jnp.concatenate([kvp_ref[:, 256:], kvc_ref[:, 256:]], axis=0).astype(bf16)
        q = q.astype(bf16)
        valid = _attn_mask(n)
        for h in range(N_KV_HEADS):
            kh = k[:, h * HEAD_DIM:(h + 1) * HEAD_DIM]
            vh = v[:, h * HEAD_DIM:(h + 1) * HEAD_DIM]
            for g in range(GQ):
                hq = h * GQ + g
                qh = q[:, hq * HEAD_DIM:(hq + 1) * HEAD_DIM]
                s = _dot(qh, kh, NT) * (HEAD_DIM ** -0.5)
                pr, _ = _softmax_sink(s, valid, sink_ref[hq])
                o = _dot(pr.astype(bf16), vh)
                o_ref[:, hq * HEAD_DIM:(hq + 1) * HEAD_DIM] = o.astype(bf16)

    tab_c = pl.BlockSpec((BLK, LANE), lambda n: (n, 0))
    tab_p = pl.BlockSpec((BLK, LANE), lambda n: (jnp.maximum(n - 1, 0), 0))
    return pl.pallas_call(
        body, name=name, grid=(nb,),
        in_specs=[pl.BlockSpec(memory_space=pltpu.SMEM),
                  pl.BlockSpec((BLK, Q_END), lambda n: (n, 0)),
                  pl.BlockSpec((BLK, KV_W), lambda n: (n, kvb)),
                  pl.BlockSpec((BLK, KV_W), lambda n: (jnp.maximum(n - 1, 0), kvb)),
                  tab_c, tab_c, tab_c, tab_p, tab_p, tab_p],
        out_specs=pl.BlockSpec((BLK, Q_END), lambda n: (n, 0)),
        out_shape=jax.ShapeDtypeStruct((S, Q_END), bf16),
        compiler_params=_params("parallel"),
    )(sinks, p, p, p, rope_c, rope_s1, rope_s2, rope_c, rope_s1, rope_s2)


def attn_bwd(p, dcat, rope_c, rope_s1, rope_s2, sinks, name):
    S = p.shape[0]
    nb = S // BLK
    kvb = Q_END // KV_W

    def body(sink_ref, q_ref, kvc_ref, kvp_ref, do_ref, cc_ref, s1c_ref, s2c_ref, cp_ref, s1p_ref, s2p_ref,
             dq_ref, dkv_ref, dsink_ref, carry, dq_scr, dkv_scr):
        n = pl.program_id(0)

        @pl.when(n == 0)
        def _():
            carry[...] = jnp.zeros_like(carry)
            dsink_ref[...] = jnp.zeros_like(dsink_ref)

        cp, s1p, s2p = cp_ref[...], s1p_ref[...], s2p_ref[...]
        cp2, s1p2, s2p2 = jnp.tile(cp, (1, 2)), jnp.tile(s1p, (1, 2)), jnp.tile(s2p, (1, 2))

        @pl.when(n < nb)
        def _():
            cc, s1c, s2c = cc_ref[...], s1c_ref[...], s2c_ref[...]
            cc8, s1c8, s2c8 = jnp.tile(cc, (1, 8)), jnp.tile(s1c, (1, 8)), jnp.tile(s2c, (1, 8))
            q = _rope(q_ref[...], cc8, s1c8, s2c8).astype(bf16)
            kc = _rope(kvc_ref[:, :256], jnp.tile(cc, (1, 2)), jnp.tile(s1c, (1, 2)), jnp.tile(s2c, (1, 2)))
            kp = _rope(kvp_ref[:, :256], cp2, s1p2, s2p2)
            k = jnp.concatenate([kp, kc], axis=0).astype(bf16)
            v = jnp.concatenate([kvp_ref[:, 256:], kvc_ref[:, 256:]], axis=0).astype(bf16)
            do = do_ref[...].astype(bf16)
            valid = _attn_mask(n)
            lane = lax.broadcasted_iota(jnp.int32, (1, LANE), 1)
            dsink = jnp.zeros((1, LANE), f32)
            for h in range(N_KV_HEADS):
                kh = k[:, h * HEAD_DIM:(h + 1) * HEAD_DIM]
                vh = v[:, h * HEAD_DIM:(h + 1) * HEAD_DIM]
                dkh = jnp.zeros((2 * BLK, HEAD_DIM), f32)
                dvh = jnp.zeros((2 * BLK, HEAD_DIM), f32)
                for g in range(GQ):
                    hq = h * GQ + g
                    qh = q[:, hq * HEAD_DIM:(hq + 1) * HEAD_DIM]
                    doh = do[:, hq * HEAD_DIM:(hq + 1) * HEAD_DIM]
                    s = _dot(qh, kh, NT) * (HEAD_DIM ** -0.5)
                    pr, ps = _softmax_sink(s, valid, sink_ref[hq])
                    dpr = _dot(doh, vh, NT)
                    dvh = dvh + _dot(pr.astype(bf16), doh, TN)
                    row = jnp.sum(pr * dpr, axis=-1, keepdims=True)
                    ds = (pr * (dpr - row) * (HEAD_DIM ** -0.5)).astype(bf16)
                    dsink = dsink + jnp.where(lane == hq, -jnp.sum(ps * row, axis=0, keepdims=True), 0.0)
                    dq_scr[:, hq * HEAD_DIM:(hq + 1) * HEAD_DIM] = _dot(ds, kh)
                    dkh = dkh + _dot(ds, qh, TN)
                dkv_scr[:, h * HEAD_DIM:(h + 1) * HEAD_DIM] = dkh
                dkv_scr[:, 256 + h * HEAD_DIM:256 + (h + 1) * HEAD_DIM] = dvh
            dsink_ref[...] += dsink
            dq_ref[...] = _rope_t(dq_scr[...], cc8, s1c8, s2c8).astype(bf16)

        prev = carry[...]

        @pl.when(n < nb)
        def _():
            dkv_scr[pl.ds(0, BLK), :] = dkv_scr[pl.ds(0, BLK), :] + prev

        @pl.when(n == nb)
        def _():
            dkv_scr[pl.ds(0, BLK), :] = prev

        done = dkv_scr[pl.ds(0, BLK), :]
        dkv_ref[:, :256] = _rope_t(done[:, :256], cp2, s1p2, s2p2).astype(bf16)
        dkv_ref[:, 256:] = done[:, 256:].astype(bf16)

        @pl.when(n < nb)
        def _():
            carry[...] = dkv_scr[pl.ds(BLK, BLK), :]

    cur = lambda n: jnp.minimum(n, nb - 1)
    prv = lambda n: jnp.maximum(n - 1, 0)
    tab_c = pl.BlockSpec((BLK, LANE), lambda n: (cur(n), 0))
    tab_p = pl.BlockSpec((BLK, LANE), lambda n: (prv(n), 0))
    return pl.pallas_call(
        body, name=name, grid=(nb + 1,),
        in_specs=[pl.BlockSpec(memory_space=pltpu.SMEM),
                  pl.BlockSpec((BLK, Q_END), lambda n: (cur(n), 0)),
                  pl.BlockSpec((BLK, KV_W), lambda n: (cur(n), kvb)),
                  pl.BlockSpec((BLK, KV_W), lambda n: (prv(n), kvb)),
                  pl.BlockSpec((BLK, Q_END), lambda n: (cur(n), 0)),
                  tab_c, tab_c, tab_c, tab_p, tab_p, tab_p],
        out_specs=[pl.BlockSpec((BLK, Q_END), lambda n: (cur(n), 0)),
                   pl.BlockSpec((BLK, KV_W), lambda n: (prv(n), 0)),
                   pl.BlockSpec((1, LANE), lambda n: (0, 0))],
        out_shape=[jax.ShapeDtypeStruct((S, Q_END), bf16), jax.ShapeDtypeStruct((S, KV_W), bf16),
                   jax.ShapeDtypeStruct((1, LANE), f32)],
        scratch_shapes=[pltpu.VMEM((BLK, KV_W), f32), pltpu.VMEM((BLK, Q_END), f32), pltpu.VMEM((2 * BLK, KV_W), f32)],
        compiler_params=_params("arbitrary"),
    )(sinks, p, p, p, dcat, rope_c, rope_s1, rope_s2, rope_c, rope_s1, rope_s2)


CONV_ROWS = 32
A1_BLK = V_END // CONV_CH
A2_BLK = A1_BLK + 1


def _ln_stats(y):
    mu = jnp.mean(y, axis=-1, keepdims=True)
    xc = y - mu
    rstd = lax.rsqrt(jnp.mean(xc * xc, axis=-1, keepdims=True) + NORM_EPS)
    return xc * rstd, rstd


def conv_fwd(p, w, b, lng, lnb, name):
    S = p.shape[0]
    T = _tile(S, 256)
    r = T // HALO

    def body(a1_ref, a2_ref, h1_ref, h2_ref, w_ref, b_ref, g_ref, bb_ref, o_ref, y_ref, scr):
        i = pl.program_id(0)
        halo = h1_ref[...] * _sig(h2_ref[...])
        scr[pl.ds(0, HALO), :] = jnp.where(i > 0, halo, 0.0)
        scr[pl.ds(HALO, T), :] = a1_ref[...] * _sig(a2_ref[...])
        acc = jnp.zeros((T, CONV_CH), f32) + b_ref[...]
        for j in range(CONV_W):
            acc = acc + scr[pl.ds(HALO - (CONV_W - 1) + j, T), :] * w_ref[j:j + 1, :]
        y_ref[...] = acc
        yh, _ = _ln_stats(acc)
        z = yh * g_ref[...] + bb_ref[...]
        o_ref[...] = (z * _sig(z)).astype(bf16)

    vec = pl.BlockSpec((1, CONV_CH), lambda i: (0, 0))
    halo_map = lambda i: jnp.maximum(i * r - 1, 0)
    return pl.pallas_call(
        body, name=name, grid=(S // T,),
        in_specs=[pl.BlockSpec((T, CONV_CH), lambda i: (i, A1_BLK)), pl.BlockSpec((T, CONV_CH), lambda i: (i, A2_BLK)),
                  pl.BlockSpec((HALO, CONV_CH), lambda i: (halo_map(i), A1_BLK)),
                  pl.BlockSpec((HALO, CONV_CH), lambda i: (halo_map(i), A2_BLK)),
                  pl.BlockSpec((HALO, CONV_CH), lambda i: (0, 0)), vec, vec, vec],
        out_specs=[pl.BlockSpec((T, CONV_CH), lambda i: (i, 0)), pl.BlockSpec((T, CONV_CH), lambda i: (i, 0))],
        out_shape=[jax.ShapeDtypeStruct((S, CONV_CH), bf16), jax.ShapeDtypeStruct((S, CONV_CH), f32)],
        scratch_shapes=[pltpu.VMEM((T + HALO, CONV_CH), f32)],
        compiler_params=_params("parallel"),
    )(p, p, p, p, w, b, lng, lnb)


def conv_bwd(p, y, dcat, w, lng, lnb, name):
    S = p.shape[0]
    T = _tile(S, 256)
    n = S // T
    r = T // HALO
    dcb = Q_END // CONV_CH
    rc = _tile(T, CONV_ROWS)

    def body(a1_ref, a2_ref, h1_ref, h2_ref, y_ref, yn_ref, do_ref, don_ref, w_ref, g_ref, bb_ref,
             da_ref, dw_ref, db_ref, dg_ref, dbb_ref, scr_h, scr_dy, acc_b, acc_g, acc_bb):
        i = pl.program_id(0)

        @pl.when(i == 0)
        def _():
            dw_ref[...] = jnp.zeros_like(dw_ref)
            acc_b[...] = jnp.zeros_like(acc_b)
            acc_g[...] = jnp.zeros_like(acc_g)
            acc_bb[...] = jnp.zeros_like(acc_bb)

        gv, bv = g_ref[...], bb_ref[...]

        def ln_silu_bwd(yv, dout):
            yh, rstd = _ln_stats(yv)
            z = yh * gv + bv
            sg = _sig(z)
            dz = dout * (sg * (1.0 + z * (1.0 - sg)))
            gz = dz * gv
            dy = rstd * (gz - jnp.mean(gz, axis=-1, keepdims=True) - yh * jnp.mean(gz * yh, axis=-1, keepdims=True))
            return dy, dz, yh

        chunks = [pl.ds(c * rc, rc) for c in range(T // rc)]
        dyn, _, _ = ln_silu_bwd(yn_ref[...], don_ref[...])
        scr_dy[pl.ds(T, HALO), :] = jnp.where(i < n - 1, dyn, 0.0)
        halo = h1_ref[...] * _sig(h2_ref[...])
        scr_h[pl.ds(0, HALO), :] = jnp.where(i > 0, halo, 0.0)
        for c, rows in enumerate(chunks):
            dy, dz, yh = ln_silu_bwd(y_ref[rows, :], do_ref[rows, :])
            acc_g[...] += _rowsum8(dz * yh)
            acc_bb[...] += _rowsum8(dz)
            acc_b[...] += _rowsum8(dy)
            scr_dy[rows, :] = dy
            scr_h[pl.ds(HALO + c * rc, rc), :] = a1_ref[rows, :] * _sig(a2_ref[rows, :])
        for c, rows in enumerate(chunks):
            dh = jnp.zeros((rc, CONV_CH), f32)
            for j in range(CONV_W):
                dh = dh + scr_dy[pl.ds(c * rc + CONV_W - 1 - j, rc), :] * w_ref[j:j + 1, :]
            a1 = a1_ref[rows, :]
            sg2 = _sig(a2_ref[rows, :])
            da_ref[rows, :CONV_CH] = (dh * sg2).astype(bf16)
            da_ref[rows, CONV_CH:] = (dh * a1 * sg2 * (1.0 - sg2)).astype(bf16)
        for j in range(CONV_W):
            part = jnp.zeros((8, CONV_CH), f32)
            for c, rows in enumerate(chunks):
                part = part + _rowsum8(scr_dy[rows, :] * scr_h[pl.ds(c * rc + HALO - (CONV_W - 1) + j, rc), :])
            dw_ref[j:j + 1, :] += jnp.sum(part, axis=0, keepdims=True)

        @pl.when(i == n - 1)
        def _():
            db_ref[...] = jnp.sum(acc_b[...], axis=0, keepdims=True)
            dg_ref[...] = jnp.sum(acc_g[...], axis=0, keepdims=True)
            dbb_ref[...] = jnp.sum(acc_bb[...], axis=0, keepdims=True)

    vec = pl.BlockSpec((1, CONV_CH), lambda i: (0, 0))
    tap = pl.BlockSpec((HALO, CONV_CH), lambda i: (0, 0))
    prev_map = lambda i: jnp.maximum(i * r - 1, 0)
    next_map = lambda i: jnp.minimum((i + 1) * r, S // HALO - 1)
    return pl.pallas_call(
        body, name=name, grid=(n,),
        in_specs=[pl.BlockSpec((T, CONV_CH), lambda i: (i, A1_BLK)), pl.BlockSpec((T, CONV_CH), lambda i: (i, A2_BLK)),
                  pl.BlockSpec((HALO, CONV_CH), lambda i: (prev_map(i), A1_BLK)),
                  pl.BlockSpec((HALO, CONV_CH), lambda i: (prev_map(i), A2_BLK)),
                  pl.BlockSpec((T, CONV_CH), lambda i: (i, 0)),
                  pl.BlockSpec((HALO, CONV_CH), lambda i: (next_map(i), 0)),
                  pl.BlockSpec((T, CONV_CH), lambda i: (i, dcb)),
                  pl.BlockSpec((HALO, CONV_CH), lambda i: (next_map(i), dcb)),
                  tap, vec, vec],
        out_specs=[pl.BlockSpec((T, 2 * CONV_CH), lambda i: (i, 0)), tap, vec, vec, vec],
        out_shape=[jax.ShapeDtypeStruct((S, 2 * CONV_CH), bf16), jax.ShapeDtypeStruct((HALO, CONV_CH), f32),
                   jax.ShapeDtypeStruct((1, CONV_CH), f32), jax.ShapeDtypeStruct((1, CONV_CH), f32),
                   jax.ShapeDtypeStruct((1, CONV_CH), f32)],
        scratch_shapes=[pltpu.VMEM((T + HALO, CONV_CH), f32), pltpu.VMEM((T + HALO, CONV_CH), f32),
                        pltpu.VMEM((8, CONV_CH), f32), pltpu.VMEM((8, CONV_CH), f32), pltpu.VMEM((8, CONV_CH), f32)],
        compiler_params=_params("arbitrary"),
    )(p, p, p, p, y, y, dcat, dcat, w, lng, lnb)


U_BLK = (V_END + 2 * CONV_CH) // SGU_CH
SV_BLK = U_BLK + 1


def _tril(w, transposed=False):
    row = lax.broadcasted_iota(jnp.int32, (BLK, BLK), 0)
    col = lax.broadcasted_iota(jnp.int32, (BLK, BLK), 1)
    keep = (col >= row) if transposed else (row >= col)
    return jnp.where(keep, w, 0.0)


def sgu_fwd(p, lng, lnb, w, bias, name):
    S = p.shape[0]
    T = _tile(S, 256)

    def body(u_ref, v_ref, g_ref, bb_ref, w_ref, bias_ref, o_ref):
        yh, _ = _ln_stats(v_ref[...])
        v = (yh * g_ref[...] + bb_ref[...]).astype(bf16)
        low = lax.broadcasted_iota(jnp.int32, (BLK, LANE), 1) < HEAD_DIM
        for pr in range(SGU_HEADS // 2):
            lanes = pl.ds(pr * LANE, LANE)
            w0 = _tril(w_ref[2 * pr]).astype(bf16)
            w1 = _tril(w_ref[2 * pr + 1]).astype(bf16)
            for c in range(T // BLK):
                rows = pl.ds(c * BLK, BLK)
                vp = v[c * BLK:(c + 1) * BLK, pr * LANE:(pr + 1) * LANE]
                mixed = jnp.where(low, _dot(w0, vp), _dot(w1, vp)) + bias_ref[:, lanes]
                o_ref[rows, lanes] = (u_ref[rows, lanes] * mixed).astype(bf16)

    vec = pl.BlockSpec((1, SGU_CH), lambda i: (0, 0))
    return pl.pallas_call(
        body, name=name, grid=(S // T,),
        in_specs=[pl.BlockSpec((T, SGU_CH), lambda i: (i, U_BLK)), pl.BlockSpec((T, SGU_CH), lambda i: (i, SV_BLK)),
                  vec, vec, pl.BlockSpec((SGU_HEADS, BLK, BLK), lambda i: (0, 0, 0)),
                  pl.BlockSpec((BLK, SGU_CH), lambda i: (0, 0))],
        out_specs=pl.BlockSpec((T, SGU_CH), lambda i: (i, 0)),
        out_shape=jax.ShapeDtypeStruct((S, SGU_CH), bf16),
        compiler_params=_params("parallel"),
    )(p, p, lng, lnb, w, bias)


def sgu_bwd(p, dcat, lng, lnb, w, wt, bias, name):
    S = p.shape[0]
    T = _tile(S, 256)
    n = S // T
    dsb = (Q_END + CONV_CH) // SGU_CH

    def body(u_ref, v_ref, do_ref, g_ref, bb_ref, w_ref, wt_ref, bias_ref,
             da_ref, dw_ref, db_ref, dg_ref, dbb_ref, dv_scr, acc_bias, acc_g, acc_bb):
        i = pl.program_id(0)

        @pl.when(i == 0)
        def _():
            dw_ref[...] = jnp.zeros_like(dw_ref)
            acc_bias[...] = jnp.zeros_like(acc_bias)
            acc_g[...] = jnp.zeros_like(acc_g)
            acc_bb[...] = jnp.zeros_like(acc_bb)

        gv = g_ref[...]
        yh, rstd = _ln_stats(v_ref[...])
        v = (yh * gv + bb_ref[...]).astype(bf16)
        low = lax.broadcasted_iota(jnp.int32, (BLK, LANE), 1) < HEAD_DIM
        for pr in range(SGU_HEADS // 2):
            lanes = pl.ds(pr * LANE, LANE)
            w0 = _tril(w_ref[2 * pr]).astype(bf16)
            w1 = _tril(w_ref[2 * pr + 1]).astype(bf16)
            wt0 = _tril(wt_ref[2 * pr], True).astype(bf16)
            wt1 = _tril(wt_ref[2 * pr + 1], True).astype(bf16)
            dw0 = jnp.zeros((BLK, BLK), f32)
            dw1 = jnp.zeros((BLK, BLK), f32)
            for c in range(T // BLK):
                rows = pl.ds(c * BLK, BLK)
                vp = v[c * BLK:(c + 1) * BLK, pr * LANE:(pr + 1) * LANE]
                mixed = jnp.where(low, _dot(w0, vp), _dot(w1, vp)) + bias_ref[:, lanes]
                do = do_ref[rows, lanes]
                da_ref[rows, lanes] = (do * mixed).astype(bf16)
                dm = do * u_ref[rows, lanes]
                acc_bias[:, lanes] += dm
                dmb = dm.astype(bf16)
                dv_scr[rows, lanes] = jnp.where(low, _dot(wt0, dmb), _dot(wt1, dmb))
                zero = jnp.zeros_like(dmb)
                dw0 = dw0 + _dot(jnp.where(low, dmb, zero), vp, NT)
                dw1 = dw1 + _dot(jnp.where(low, zero, dmb), vp, NT)
            dw_ref[2 * pr] += _tril(dw0)
            dw_ref[2 * pr + 1] += _tril(dw1)
        dv = dv_scr[...]
        acc_g[...] += _rowsum8(dv * yh)
        acc_bb[...] += _rowsum8(dv)
        gz = dv * gv
        dvr = rstd * (gz - jnp.mean(gz, axis=-1, keepdims=True) - yh * jnp.mean(gz * yh, axis=-1, keepdims=True))
        da_ref[:, SGU_CH:] = dvr.astype(bf16)

        @pl.when(i == n - 1)
        def _():
            ch = lax.broadcasted_iota(jnp.int32, (SGU_CH, LANE), 0) // HEAD_DIM
            hd = lax.broadcasted_iota(jnp.int32, (SGU_CH, LANE), 1)
            fold = jnp.where(ch == hd, 1.0, 0.0).astype(f32)
            db_ref[...] = jnp.dot(acc_bias[...], fold, preferred_element_type=f32, precision=lax.Precision.HIGHEST)
            dg_ref[...] = jnp.sum(acc_g[...], axis=0, keepdims=True)
            dbb_ref[...] = jnp.sum(acc_bb[...], axis=0, keepdims=True)

    vec = pl.BlockSpec((1, SGU_CH), lambda i: (0, 0))
    wsp = pl.BlockSpec((SGU_HEADS, BLK, BLK), lambda i: (0, 0, 0))
    return pl.pallas_call(
        body, name=name, grid=(n,),
        in_specs=[pl.BlockSpec((T, SGU_CH), lambda i: (i, U_BLK)), pl.BlockSpec((T, SGU_CH), lambda i: (i, SV_BLK)),
                  pl.BlockSpec((T, SGU_CH), lambda i: (i, dsb)), vec, vec, wsp, wsp,
                  pl.BlockSpec((BLK, SGU_CH), lambda i: (0, 0))],
        out_specs=[pl.BlockSpec((T, 2 * SGU_CH), lambda i: (i, 0)), wsp,
                   pl.BlockSpec((BLK, LANE), lambda i: (0, 0)), vec, vec],
        out_shape=[jax.ShapeDtypeStruct((S, 2 * SGU_CH), bf16), jax.ShapeDtypeStruct((SGU_HEADS, BLK, BLK), f32),
                   jax.ShapeDtypeStruct((BLK, LANE), f32), jax.ShapeDtypeStruct((1, SGU_CH), f32),
                   jax.ShapeDtypeStruct((1, SGU_CH), f32)],
        scratch_shapes=[pltpu.VMEM((T, SGU_CH), f32), pltpu.VMEM((BLK, SGU_CH), f32),
                        pltpu.VMEM((8, SGU_CH), f32), pltpu.VMEM((8, SGU_CH), f32)],
        compiler_params=_params("arbitrary"),
    )(p, p, dcat, lng, lnb, w, wt, bias)


HBM = pl.BlockSpec(memory_space=pltpu.HBM)
SEM = pl.BlockSpec(memory_space=pltpu.SEMAPHORE)
ANY = pl.BlockSpec(memory_space=pl.ANY)
EFFECT = pltpu.SideEffectType.DATAFLOW_SIDE_EFFECTING


def _flip(x, y, c, k):
    px, py, pc = x ^ (k >> 2), y ^ ((k >> 1) & 1), c ^ (k & 1)
    return (px, py, pc), 4 * px + 2 * py + pc


def _routes_gather(x, y, c):
    me = 4 * x + 2 * y + c
    out = []
    for k in (1, 2, 4, 6):
        dev, idx = _flip(x, y, c, k)
        out.append((dev, None, me, idx))
    return out


def _routes_pair(x, y, c):
    dev, _ = _flip(x, y, c, 1)
    return [(dev, 2 * q + (1 - c), q, q) for q in range(N_DEV // 2)]


def _routes_chips(x, y, c):
    out = []
    for k in (2, 4, 6):
        dev, idx = _flip(x, y, c, k)
        out.append((dev, idx // 2, 2 * x + y, idx // 2))
    return out


def _routes_forward(x, y, c):
    sib, _ = _flip(x, y, c, 1)
    out = []
    for k in (2, 4, 6):
        _, idx = _flip(x, y, c, k)
        out.append((sib, idx, idx, idx ^ 1))
    return out


def _routes_all(x, y, c):
    me = 4 * x + 2 * y + c
    out = []
    for k in range(1, N_DEV):
        dev, idx = _flip(x, y, c, k)
        out.append((dev, None, me, idx))
    return out


def _slot(ref, slot, kind):
    if slot is None:
        return ref
    if kind == "cols":
        width = ref.shape[2] // (N_DEV // 2)
        return ref.at[slot // (N_DEV // 2), :, pl.ds(pl.multiple_of((slot % (N_DEV // 2)) * width, LANE), width)]
    return ref.at[slot]


def _copies(routes, srcs, lands, send_sems, recv_sems, incoming, src_kinds, land_kinds):
    x, y, c = lax.axis_index("x"), lax.axis_index("y"), lax.axis_index("c")
    out = []
    n = len(lands)
    if srcs is None:
        srcs, src_kinds = lands, land_kinds
    for k, (dev, src_slot, dst_slot, recv_slot) in enumerate(routes(x, y, c)):
        for a in range(n):
            out.append(pltpu.make_async_remote_copy(
                src_ref=_slot(srcs[a], src_slot, src_kinds[a]),
                dst_ref=_slot(lands[a], recv_slot if incoming else dst_slot, land_kinds[a]),
                send_sem=send_sems.at[k * n + a], recv_sem=recv_sems.at[k * n + a], device_id=dev, device_id_type=MESH))
    return out


def _pin(a):
    return pltpu.with_memory_space_constraint(a, pltpu.HBM)


def split_start(srcs, lands, routes, name, deps=(), src_kinds=None, land_kinds=None):
    n = len(lands)
    ns = 0 if srcs is None else n
    n_routes = len(routes(0, 0, 0))
    ops = ([] if srcs is None else list(srcs)) + list(lands)
    src_kinds = src_kinds or ["rows"] * n
    land_kinds = land_kinds or ["rows"] * n

    def body(*refs):
        src, land = (refs[:n] if ns else None), refs[ns:ns + n]
        first_out = ns + n + len(deps)
        send_sems, recv_sems, token = refs[first_out], refs[first_out + 1], refs[-1]
        for cp in _copies(routes, src, land, send_sems, recv_sems, False, src_kinds, land_kinds):
            cp.start()
        token[...] = jnp.zeros_like(token)

    thru = [pltpu.HBM(a.shape, a.dtype) for a in ops]
    res = pl.pallas_call(
        body, name=name,
        out_shape=(pltpu.SemaphoreType.DMA((n * n_routes,)), pltpu.SemaphoreType.DMA((n * n_routes,)), *thru,
                   jax.ShapeDtypeStruct((8, LANE), f32)),
        in_specs=[HBM] * len(ops) + [ANY] * len(deps),
        out_specs=(SEM, SEM, *([HBM] * len(ops)), pl.BlockSpec(memory_space=pltpu.VMEM)),
        input_output_aliases={i: 2 + i for i in range(len(ops))},
        compiler_params=pltpu.CompilerParams(has_side_effects=EFFECT),
    )(*[_pin(a) for a in ops], *deps)
    return (res[0], res[1], (list(res[2:2 + n]) if ns else None), list(res[2 + ns:2 + ns + n]), res[-1],
            (src_kinds, land_kinds))


def split_wait(started, after, routes, name):
    send_sems, recv_sems, srcs, lands, _, (src_kinds, land_kinds) = started
    n = len(lands)
    ns = 0 if srcs is None else n
    ops = ([] if srcs is None else list(srcs)) + list(lands)
    afters = list(after) if isinstance(after, (list, tuple)) else [after]

    def body(*refs):
        src, land = (refs[:n] if ns else None), refs[ns:ns + n]
        send_s, recv_s = refs[ns + n], refs[ns + n + 1]
        for cp in _copies(routes, src, land, send_s, recv_s, True, src_kinds, land_kinds):
            cp.wait_send()
            cp.wait_recv()

    thru = [pltpu.HBM(a.shape, a.dtype) for a in ops]
    res = pl.pallas_call(
        body, name=name, out_shape=tuple(thru),
        in_specs=[HBM] * len(ops) + [SEM, SEM] + [ANY] * len(afters), out_specs=tuple([HBM] * len(ops)),
        input_output_aliases={i: i for i in range(len(ops))},
        compiler_params=pltpu.CompilerParams(has_side_effects=EFFECT),
    )(*ops, send_sems, recv_sems, *afters)
    return (list(res[:n]) if ns else None), list(res[ns:ns + n])


def chip_sum(parts, land, c_idx, kind, name):
    _, R, C = land.shape
    tr = R if R * C * 2 <= 3 * 2 ** 20 else _tile(R, 512)
    half = N_DEV // 2

    def body(c_ref, p_ref, l_ref, o_ref):
        o_ref[...] = (p_ref[...].astype(f32) + l_ref[...].astype(f32)).astype(bf16)

    if kind == "cols":
        mine = lambda q, i, c_ref: ((2 * q + c_ref[0]) // half, i, (2 * q + c_ref[0]) % half)
    else:
        mine = lambda q, i, c_ref: (2 * q + c_ref[0], i, 0)
    return pl.pallas_call(
        body, name=name,
        grid_spec=pltpu.PrefetchScalarGridSpec(
            num_scalar_prefetch=1, grid=(half, R // tr),
            in_specs=[pl.BlockSpec((None, tr, C), mine), pl.BlockSpec((None, tr, C), lambda q, i, c_ref: (q, i, 0))],
            out_specs=pl.BlockSpec((None, tr, C), lambda q, i, c_ref: (q, i, 0))),
        out_shape=jax.ShapeDtypeStruct((half, R, C), bf16),
        compiler_params=_params("parallel", "parallel"),
    )(c_idx, parts, land)


def place_own(land, src, me_idx, kind, name):
    R, C = src.shape
    tr = _tile(R, 512)
    half = N_DEV // 2
    if kind == "cols":
        where = lambda i, m: (m[0] // half, i, m[0] % half)
    else:
        where = lambda i, m: (m[0], i, 0)

    def body(m_ref, land_ref, src_ref, out_ref):
        out_ref[...] = src_ref[...]

    return pl.pallas_call(
        body, name=name,
        grid_spec=pltpu.PrefetchScalarGridSpec(
            num_scalar_prefetch=1, grid=(R // tr,),
            in_specs=[ANY, pl.BlockSpec((tr, C), lambda i, m: (i, 0))],
            out_specs=pl.BlockSpec((None, tr, C), where)),
        out_shape=jax.ShapeDtypeStruct(land.shape, land.dtype),
        input_output_aliases={1: 0},
        compiler_params=_params("arbitrary"),
    )(me_idx, land, src)


def sum_slots(parts, name):
    P, R, C = parts.shape
    tr = _tile(R, 512)

    def body(p_ref, o_ref):
        total = p_ref[0]
        for j in range(1, P):
            total = total + p_ref[j]
        o_ref[...] = total

    return pl.pallas_call(
        body, name=name, grid=(R // tr,),
        in_specs=[pl.BlockSpec((P, tr, C), lambda i: (0, i, 0))],
        out_specs=pl.BlockSpec((tr, C), lambda i: (i, 0)),
        out_shape=jax.ShapeDtypeStruct((R, C), f32),
        compiler_params=_params("parallel"),
    )(parts)


def adamw(parts, owns, chip, w, m, v, name):
    L, R, C = w.shape
    P = parts[0].shape[0]
    tr = _tile(R, 128 if C > 1024 else 256)
    nr = R // tr
    c1 = 1.0 - ADAM_B1 ** ADAM_STEP
    c2 = 1.0 - ADAM_B2 ** ADAM_STEP
    n_own = L if owns is not None else 0

    def body(chip_ref, *refs):
        part_refs, own_refs = refs[:L], refs[L:L + n_own]
        w_ref, m_ref, v_ref, g_out, d_out, m_out, v_out = refs[L + n_own:]
        layer = pl.program_id(0)
        for l in range(L):
            @pl.when(layer == l)
            def _(l=l):
                g = None
                for q in range(P):
                    term = part_refs[l][q].astype(f32)
                    if n_own:
                        term = jnp.where(chip_ref[0] == q, own_refs[l][...].astype(f32), term)
                    g = term if g is None else g + term
                mn = ADAM_B1 * m_ref[...] + (1.0 - ADAM_B1) * g
                vn = ADAM_B2 * v_ref[...] + (1.0 - ADAM_B2) * (g * g)
                g_out[...] = g
                m_out[...] = mn
                v_out[...] = vn
                d_out[...] = -ADAM_LR * ((mn / c1) / (jnp.sqrt(vn / c2) + ADAM_EPS) + ADAM_WD * w_ref[...])

    def rows(l, a, i):
        return jnp.where(a == l, i, jnp.where(a < l, 0, nr - 1))

    def part_spec(l):
        return pl.BlockSpec((P, tr, C), lambda a, i, chip_ref: (0, rows(l, a, i), 0))

    def own_spec(l):
        return pl.BlockSpec((None, tr, C), lambda a, i, chip_ref: (chip_ref[0], rows(l, a, i), 0))

    slab = pl.BlockSpec((None, tr, C), lambda a, i, chip_ref: (a, i, 0))
    out = jax.ShapeDtypeStruct((L, R, C), f32)
    return pl.pallas_call(
        body, name=name,
        grid_spec=pltpu.PrefetchScalarGridSpec(
            num_scalar_prefetch=1, grid=(L, nr),
            in_specs=[part_spec(l) for l in range(L)] + [own_spec(l) for l in range(n_own)] + [slab, slab, slab],
            out_specs=[slab, slab, slab, slab]),
        out_shape=[out, out, out, out],
        compiler_params=_params("arbitrary", "arbitrary"),
    )(chip, *parts, *(owns or []), w, m, v)


PACK = 8 * LANE
PACK_ROWS = 256


def _pack(arrs):
    pieces = []
    for a in arrs:
        flat = a.astype(f32).reshape(-1)
        pad = (-flat.shape[0]) % PACK
        pieces.append(jnp.pad(flat, (0, pad)).reshape(-1, LANE))
    rows = sum(p.shape[0] for p in pieces)
    if rows > PACK_ROWS and rows % PACK_ROWS:
        pieces.append(jnp.zeros((PACK_ROWS - rows % PACK_ROWS, LANE), f32))
    return jnp.concatenate(pieces, axis=0)


def _unpack(buf, shapes):
    out, row = [], 0
    for shp in shapes:
        size = math.prod(shp)
        rows = (size + PACK - 1) // PACK * (PACK // LANE)
        out.append(buf[row:row + rows].reshape(-1)[:size].reshape(shp))
        row += rows
    return out


def _rope_tables(positions):
    half = ROT_DIM // 2
    inv_freq = 1.0 / (ROPE_THETA ** (jnp.arange(0, ROT_DIM, 2, dtype=f32) / ROT_DIM))
    ang = positions.astype(f32)[:, None] * inv_freq
    cos, sin = jnp.cos(ang), jnp.sin(ang)
    S = positions.shape[0]
    zeros, ones = jnp.zeros((S, half), f32), jnp.ones((S, HEAD_DIM - ROT_DIM), f32)
    rest = jnp.zeros((S, HEAD_DIM - ROT_DIM), f32)
    c = jnp.concatenate([cos, cos, ones], axis=1)
    s1 = jnp.concatenate([zeros, sin, rest], axis=1)
    s2 = jnp.concatenate([-sin, zeros, rest], axis=1)
    return tuple(jnp.tile(t, (1, LANE // HEAD_DIM)) for t in (c, s1, s2))


def _cols_to_shards(g):
    lead, (R, N) = g.shape[:-2], g.shape[-2:]
    g = g.reshape(lead + (R, N_DEV, N // N_DEV))
    return jnp.moveaxis(g, -2, 0)


def _shards_to_cols(g):
    g = jnp.moveaxis(g, 0, -2)
    return g.reshape(g.shape[:-2] + (g.shape[-2] * g.shape[-1],))


def kernel(x, positions, norm_ffn1, ffn1_w_in, ffn1_w_out, norm_mix, w_in, conv_dw_w, conv_dw_b, conv_ln_g, conv_ln_b, sgu_ln_g, sgu_ln_b, sgu_w, sgu_b, attn_sinks, w_out, norm_ffn2, ffn2_w_in, ffn2_w_out, final_norm, loss_target, m_norm_ffn1, m_ffn1_w_in, m_ffn1_w_out, m_norm_mix, m_w_in, m_conv_dw_w, m_conv_dw_b, m_conv_ln_g, m_conv_ln_b, m_sgu_ln_g, m_sgu_ln_b, m_sgu_w, m_sgu_b, m_attn_sinks, m_w_out, m_norm_ffn2, m_ffn2_w_in, m_ffn2_w_out, m_final_norm, v_norm_ffn1, v_ffn1_w_in, v_ffn1_w_out, v_norm_mix, v_w_in, v_conv_dw_w, v_conv_dw_b, v_conv_ln_g, v_conv_ln_b, v_sgu_ln_g, v_sgu_ln_b, v_sgu_w, v_sgu_b, v_attn_sinks, v_w_out, v_norm_ffn2, v_ffn2_w_in, v_ffn2_w_out, v_final_norm):
    L = norm_ffn1.shape[0]
    S, D = x.shape[1], x.shape[2]
    F = ffn1_w_out.shape[1] * N_DEV
    me = 4 * lax.axis_index("x") + 2 * lax.axis_index("y") + lax.axis_index("c")
    x0 = x[0]
    rope_c, rope_s1, rope_s2 = _rope_tables(positions[0])
    cw = CONV_CH // N_DEV

    c_idx = lax.axis_index("c").astype(jnp.int32).reshape(1)
    chip = (2 * lax.axis_index("x") + lax.axis_index("y")).astype(jnp.int32).reshape(1)
    no_chip = jnp.zeros((1,), jnp.int32)
    me_idx = me.astype(jnp.int32).reshape(1)

    row = lambda a: a.reshape(1, -1)
    order_fwd = [(l, g) for l in range(L) for g in (("ffn1_in", "ffn1_out") if l == 0 else ("ffn1",)) + ("mix", "ffn2")]

    def group_srcs(l, grp):
        if grp == "mix":
            taps = jnp.pad(conv_dw_w[l], ((0, HALO - CONV_W), (0, LANE - cw)))
            return [w_in[l].astype(bf16), w_out[l].astype(bf16), taps]
        both = ([ffn2_w_in[l], ffn2_w_out[l]] if grp == "ffn2" else [ffn1_w_in[l], ffn1_w_out[l]])
        both = [a.astype(bf16) for a in both]
        return both[:1] if grp == "ffn1_in" else both[1:] if grp == "ffn1_out" else both

    def kinds_of(grp):
        return {"mix": ["rows"] * 3, "ffn1_in": ["cols"], "ffn1_out": ["rows"]}.get(grp, ["cols", "rows"])

    def gather_start(k, deps=()):
        l, grp = order_fwd[k]
        srcs = group_srcs(l, grp)
        lands = [lax.empty((2, D, F) if kind == "cols" else (N_DEV,) + a.shape, a.dtype)
                 for a, kind in zip(srcs, kinds_of(grp))]
        return split_start(srcs, lands, _routes_gather, f"gather_start_{grp}_{l}", deps, land_kinds=kinds_of(grp))

    def gather_forward(k, started, after):
        l, grp = order_fwd[k]
        srcs, lands = split_wait(started, after, _routes_gather, f"gather_wait_{grp}_{l}")
        return srcs, split_start(None, lands, _routes_forward, f"forward_start_{grp}_{l}", land_kinds=kinds_of(grp))

    def gather_finish(k, srcs, started, after):
        l, grp = order_fwd[k]
        _, lands = split_wait(started, after, _routes_forward, f"forward_wait_{grp}_{l}")
        full = [place_own(ld, s, me_idx, kind, f"own_{grp}_{l}_{a}")
                for a, (ld, s, kind) in enumerate(zip(lands, srcs, kinds_of(grp)))]
        if grp == "mix":
            return dict(w_in=_shards_to_cols(full[0]), w_out=full[1].reshape(D, D),
                        taps=_shards_to_cols(full[2][:, :, :cw]))
        named = dict(zip(["w_in", "w_out"] if len(full) == 2 else ["w_in" if grp == "ffn1_in" else "w_out"], full))
        if "w_out" in named:
            named["w_out"] = named["w_out"].reshape(F, D)
        return named

    def ffn_head(xs, wts, g_norm, tag, sv):
        sv["x_in"] = xs
        h, sv["ht"] = rmsnorm_fwd(xs, g_norm, f"norm_{tag}")
        sv["gu"], a, sv["at"] = ffn_in(h, wts["w_in"], f"{tag}_in")
        return a

    def ffn_fwd(xs, wts, g_norm, tag, sv, mid):
        a = ffn_head(xs, wts, g_norm, tag, sv)
        return mm_res(a, wts["w_out"], xs, FFN_RES, f"{tag}_out", deps=mid(a))

    def mix_fwd(xs, wts, l, sv, mid):
        sv["x_in"] = xs
        h, sv["ht"] = rmsnorm_fwd(xs, row(norm_mix[l]), f"norm_mix_{l}")
        p = mm_nn(h, wts["w_in"][None], f"mix_in_{l}")[0]
        sv["p"] = p
        attn = attn_fwd(p, rope_c, rope_s1, rope_s2, attn_sinks[l], f"attn_fwd_{l}")
        conv, sv["conv_y"] = conv_fwd(p, wts["taps"], row(conv_dw_b[l]), row(conv_ln_g[l]), row(conv_ln_b[l]),
                                      f"conv_fwd_{l}")
        sv["sgu_bias"] = jnp.repeat(sgu_b[l].T, HEAD_DIM, axis=1)
        sgu = sgu_fwd(p, row(sgu_ln_g[l]), row(sgu_ln_b[l]), sgu_w[l], sv["sgu_bias"], f"sgu_fwd_{l}")
        cat = jnp.concatenate([attn, conv, sgu], axis=1)
        sv["catt"] = cat.T
        return mm_res(cat, wts["w_out"], xs, 1.0, f"mix_out_{l}", deps=mid(cat))

    weights, saved = {}, {}
    xs = x0
    starts = []
    for k in range(len(order_fwd)):
        starts.append(gather_start(k, (starts[-1][4],) if starts else ()))
    state = dict(zip(("srcs", "fwd"), gather_forward(0, starts[0], starts[-1][4])))
    for k, (l, grp) in enumerate(order_fwd):
        nxt = starts[k + 1] if k + 1 < len(order_fwd) else None
        wts = gather_finish(k, state["srcs"], state["fwd"], state["fwd"][4])

        def mid(after, k=k, nxt=nxt):
            if not nxt:
                return ()
            state["srcs"], state["fwd"] = gather_forward(k + 1, nxt, after)
            return (state["fwd"][4],)

        if grp == "ffn1_in":
            sv = saved[l, "ffn1"] = {}
            weights[l, "ffn1"] = wts
            head = ffn_head(xs, wts, row(norm_ffn1[l]), f"ffn1_{l}", sv)
            mid(head)
        elif grp == "ffn1_out":
            weights[l, "ffn1"].update(wts)
            xs = mm_res(head, wts["w_out"], xs, FFN_RES, f"ffn1_{l}_out")
            mid(xs)
        else:
            sv = saved[l, grp] = {}
            weights[l, grp] = wts
            if grp == "mix":
                xs = mix_fwd(xs, wts, l, sv, mid)
            else:
                xs = ffn_fwd(xs, wts, row(norm_ffn1[l] if grp == "ffn1" else norm_ffn2[l]), f"{grp}_{l}", sv, mid)

    dx, dxb, d_final_norm, loss = final_loss(xs, row(final_norm), loss_target[0], "final_loss")

    def scatter_start(grads, kinds, tag, deps=()):
        half = N_DEV // 2
        lands = [lax.empty((half, g.shape[1], g.shape[2] // half) if kind == "cols" else (half,) + g.shape[1:], g.dtype)
                 for g, kind in zip(grads, kinds)]
        return split_start(grads, lands, _routes_pair, f"pair_start_{tag}", deps, src_kinds=kinds)

    def pair_to_chips(started, after, tag):
        kinds = started[5][0]
        grads, landed = split_wait(started, after, _routes_pair, f"pair_wait_{tag}")
        sums = [chip_sum(g, ld, c_idx, kind, f"chip_sum_{tag}_{a}")
                for a, (g, ld, kind) in enumerate(zip(grads, landed, kinds))]
        return split_start(sums, [lax.empty(s.shape, s.dtype) for s in sums], _routes_chips, f"chips_start_{tag}")

    def ffn_bwd(dx, dxb, wts, sv, g_norm, tag, deps=(), out_first=False):
        dgu = ffn_dact(dxb, wts["w_out"], sv["gu"], f"{tag}_dact")
        d_w_out = mm_nn(sv["at"], dxb[None], f"{tag}_dwout", bf16, FFN_RES, deps=deps)[0]
        d_w_out = d_w_out.reshape(N_DEV, F // N_DEV, D)
        if not out_first:
            d_w_in = mm_nn(sv["ht"], dgu, f"{tag}_dwin", bf16)
            pair = scatter_start([d_w_in, d_w_out], ["cols", "rows"], tag)
            dh = mm_nt(dgu, wts["w_in"], f"{tag}_dh", deps=(pair[4],))
            chips = [pair_to_chips(pair, dh, tag)]
        else:
            pair_out = scatter_start([d_w_out], ["rows"], f"{tag}_out")
            d_w_in = mm_nn(sv["ht"], dgu, f"{tag}_dwin", bf16, deps=(pair_out[4],))
            chips_out = pair_to_chips(pair_out, d_w_in, f"{tag}_out")
            pair_in = scatter_start([d_w_in], ["cols"], f"{tag}_in", deps=(chips_out[4],))
            dh = mm_nt(dgu, wts["w_in"], f"{tag}_dh", deps=(pair_in[4],))
            chips = [pair_to_chips(pair_in, dh, f"{tag}_in"), chips_out]
        dx, dxb, dg = rmsnorm_bwd(dh, sv["x_in"], g_norm, dx, f"{tag}_dnorm", deps=(chips[0][4],))
        return dx, dxb, dg, chips

    small = [None] * L
    chips_pending = {}
    for l in reversed(range(L)):
        dx, dxb, d_norm_ffn2, chips_pending[l, "ffn2"] = ffn_bwd(
            dx, dxb, weights[l, "ffn2"], saved[l, "ffn2"], row(norm_ffn2[l]), f"ffn2_{l}")

        wts, sv = weights[l, "mix"], saved[l, "mix"]
        d_w_out = mm_nn(sv["catt"], dxb[None], f"mix_dwout_{l}", bf16)[0]
        dcat = mm_nt(dxb[None], wts["w_out"][None], f"mix_dcat_{l}", deps=(d_w_out,))
        p = sv["p"]
        dq, dkv, d_sinks = attn_bwd(p, dcat, rope_c, rope_s1, rope_s2, attn_sinks[l], f"attn_bwd_{l}")
        da_conv, d_taps, d_conv_b, d_conv_g, d_conv_bb = conv_bwd(
            p, sv["conv_y"], dcat, wts["taps"], row(conv_ln_g[l]), row(conv_ln_b[l]), f"conv_bwd_{l}")
        da_sgu, d_sgu_w, d_sgu_bias, d_sgu_g, d_sgu_bb = sgu_bwd(
            p, dcat, row(sgu_ln_g[l]), row(sgu_ln_b[l]), sgu_w[l], jnp.swapaxes(sgu_w[l], 1, 2), sv["sgu_bias"],
            f"sgu_bwd_{l}")
        dp = jnp.concatenate([dq, dkv, da_conv, da_sgu], axis=1)
        d_w_in = mm_nn(sv["ht"], dp[None], f"mix_dwin_{l}", bf16)[0]
        pair = scatter_start([_cols_to_shards(d_w_in), d_w_out.reshape(N_DEV, D // N_DEV, D)], ["rows", "rows"],
                             f"mix_{l}")
        dh = mm_nt(dp[None], wts["w_in"][None], f"mix_dh_{l}", deps=(pair[4],))
        chips_pending[l, "mix"] = [pair_to_chips(pair, dh, f"mix_{l}")]
        dx, dxb, d_norm_mix = rmsnorm_bwd(dh, sv["x_in"], row(norm_mix[l]), dx, f"mix_dnorm_{l}",
                                          deps=(chips_pending[l, "mix"][0][4],))

        small[l] = dict(norm_mix=d_norm_mix[0], conv_dw_w=d_taps[:CONV_W],
                        conv_dw_b=d_conv_b[0], conv_ln_g=d_conv_g[0], conv_ln_b=d_conv_bb[0], sgu_ln_g=d_sgu_g[0],
                        sgu_ln_b=d_sgu_bb[0], sgu_w=d_sgu_w, sgu_b=d_sgu_bias[:, :SGU_HEADS].T,
                        attn_sinks=d_sinks[0, :N_Q_HEADS], norm_ffn2=d_norm_ffn2[0])
        if l == 0:
            early_names = ["norm_mix", "conv_dw_w", "conv_dw_b", "conv_ln_g", "conv_ln_b", "sgu_ln_g", "sgu_ln_b",
                           "sgu_w", "sgu_b", "attn_sinks", "norm_ffn2"]
            early = [jnp.stack([small[k][n] for k in range(L)]) for n in early_names]
            early += [d_final_norm[0], loss.reshape(1)]
            early_shapes = [a.shape for a in early]
            early = _pack(early)
            early_pending = split_start([early], [lax.empty((N_DEV,) + early.shape, f32)], _routes_all,
                                        "small_start", deps=(dxb,))
            early_token = (early_pending[4],)
        else:
            early_token = ()

        dx, dxb, d_norm_ffn1, chips_pending[l, "ffn1"] = ffn_bwd(
            dx, dxb, weights[l, "ffn1"], saved[l, "ffn1"], row(norm_ffn1[l]), f"ffn1_{l}", early_token, l == 0)
        small[l]["norm_ffn1"] = d_norm_ffn1[0]

    grad_x = dx[None]
    late = _pack([jnp.stack([small[l]["norm_ffn1"] for l in range(L)])])
    late_pending = split_start([late], [lax.empty((N_DEV,) + late.shape, f32)], _routes_all, "late_start", deps=(dx,))

    def landed(grp, after):
        sums, lands = [], []
        for l in range(L):
            got = [split_wait(st, after, _routes_chips, f"chips_wait_{grp}_{l}_{a}")
                   for a, st in enumerate(chips_pending[l, grp])]
            sums.append([s for g in got for s in g[0]])
            lands.append([s for g in got for s in g[1]])
        return sums, lands

    given = dict(norm_ffn1=(norm_ffn1, m_norm_ffn1, v_norm_ffn1), norm_mix=(norm_mix, m_norm_mix, v_norm_mix),
                 conv_dw_w=(conv_dw_w, m_conv_dw_w, v_conv_dw_w), conv_dw_b=(conv_dw_b, m_conv_dw_b, v_conv_dw_b),
                 conv_ln_g=(conv_ln_g, m_conv_ln_g, v_conv_ln_g), conv_ln_b=(conv_ln_b, m_conv_ln_b, v_conv_ln_b),
                 sgu_ln_g=(sgu_ln_g, m_sgu_ln_g, v_sgu_ln_g), sgu_ln_b=(sgu_ln_b, m_sgu_ln_b, v_sgu_ln_b),
                 sgu_w=(sgu_w, m_sgu_w, v_sgu_w), sgu_b=(sgu_b, m_sgu_b, v_sgu_b),
                 attn_sinks=(attn_sinks, m_attn_sinks, v_attn_sinks), norm_ffn2=(norm_ffn2, m_norm_ffn2, v_norm_ffn2),
                 final_norm=(final_norm, m_final_norm, v_final_norm))

    def small_update(pending, after, names, shapes, tag):
        (own,), (others,) = split_wait(pending, after, _routes_all, f"{tag}_wait")
        total = _unpack(sum_slots(place_own(others, own, me_idx, "rows", f"{tag}_own"), f"{tag}_sum"), shapes)
        g = dict(zip(names, total))
        if "conv_dw_w" in g:
            g["conv_dw_w"] = lax.dynamic_slice_in_dim(g["conv_dw_w"], me * cw, cw, axis=2)
        upd_names = [n for n in names if n in given]
        upd_shapes = [given[n][0].shape for n in upd_names]
        packed = [_pack([g[n] for n in upd_names])[None]] + [_pack([given[n][k] for n in upd_names])[None]
                                                              for k in range(3)]
        res = adamw([packed[0]], None, no_chip, packed[1], packed[2], packed[3], f"{tag}_adamw")
        return g, [dict(zip(upd_names, _unpack(r[0], upd_shapes))) for r in res], res[0]

    big = {}
    done = [dx]

    def big_update(grp, names):
        sums, lands = landed(grp, done)
        for idx, (name, w, m, v) in enumerate(names):
            big[name] = adamw([lands[l][idx] for l in range(L)], [sums[l][idx] for l in range(L)], chip, w, m, v,
                              f"adamw_{name}")
            done.append(big[name][0])

    big_update("ffn2", (("ffn2_w_in", ffn2_w_in, m_ffn2_w_in, v_ffn2_w_in),
                        ("ffn2_w_out", ffn2_w_out, m_ffn2_w_out, v_ffn2_w_out)))
    big_update("mix", (("w_in", w_in, m_w_in, v_w_in), ("w_out", w_out, m_w_out, v_w_out)))
    g_early, upd_early, marker = small_update(early_pending, done, early_names + ["final_norm", "loss"], early_shapes,
                                              "small")
    done.append(marker)
    big_update("ffn1", (("ffn1_w_in", ffn1_w_in, m_ffn1_w_in, v_ffn1_w_in),
                        ("ffn1_w_out", ffn1_w_out, m_ffn1_w_out, v_ffn1_w_out)))
    _, upd_late, _ = small_update(late_pending, done, ["norm_ffn1"], [(L, D)], "late")
    upd = [{**upd_early[k], **upd_late[k]} for k in range(4)]

    order = ["norm_ffn1", "ffn1_w_in", "ffn1_w_out", "norm_mix", "w_in", "conv_dw_w", "conv_dw_b", "conv_ln_g",
             "conv_ln_b", "sgu_ln_g", "sgu_ln_b", "sgu_w", "sgu_b", "attn_sinks", "w_out", "norm_ffn2", "ffn2_w_in",
             "ffn2_w_out", "final_norm"]
    outs = [g_early["loss"].reshape(()), grad_x]
    for k in range(4):
        outs += [big[n][k] if n in big else upd[k][n] for n in order]
    return tuple(outs)
```

```python
import functools
import math

import jax
import jax.numpy as jnp
from jax import lax
from jax.experimental import pallas as pl
from jax.experimental.pallas import tpu as pltpu

f32 = jnp.float32
bf16 = jnp.bfloat16

N_DEV = 8
HEAD_DIM = 64
N_Q_HEADS = 16
N_KV_HEADS = 4
GQ = N_Q_HEADS // N_KV_HEADS
BLK = 128
ROT_DIM = 16
ROPE_THETA = 500000.0
CONV_W = 31
CONV_CH = 512
SGU_CH = 512
SGU_HEADS = 8
Q_END = N_Q_HEADS * HEAD_DIM
KV_W = 2 * N_KV_HEADS * HEAD_DIM
V_END = Q_END + KV_W
IN_COLS = V_END + 2 * CONV_CH + 2 * SGU_CH
HALO = 32
NORM_EPS = 1e-5
FFN_RES = 0.5
ADAM_LR, ADAM_B1, ADAM_B2, ADAM_EPS, ADAM_WD, ADAM_STEP = 0.001, 0.9, 0.999, 1e-08, 0.01, 10
LANE = 128
VMEM_LIMIT = 56 * 2 ** 20
MM_ROWS = 1024
MM_BLOCK_BYTES = 16 * 2 ** 20
MESH = pl.DeviceIdType.MESH

NN = (((1,), (0,)), ((), ()))
NT = (((1,), (1,)), ((), ()))
TN = (((0,), (0,)), ((), ()))


def _tile(dim, pref):
    t = min(pref, dim)
    while dim % t:
        t //= 2
    return t


def _row_tile(rows, k, most=2048):
    for n in range(1, 17):
        t = rows // n
        if rows % n == 0 and t % LANE == 0 and t <= most and t * k * 2 <= MM_BLOCK_BYTES:
            return t
    return _tile(rows, 512)


def _params(*sem):
    return pltpu.CompilerParams(dimension_semantics=sem, vmem_limit_bytes=VMEM_LIMIT)


def _sig(x):
    return 1.0 / (1.0 + jnp.exp(-x))


def _dot(a, b, dims=NN):
    return lax.dot_general(a, b, dims, preferred_element_type=f32)


def _rowsum8(x):
    return x.reshape(x.shape[0] // 8, 8, x.shape[1]).sum(axis=0)


def rmsnorm_fwd(x, g, name, deps=()):
    S, D = x.shape
    tm = _tile(S, 512)

    def body(x_ref, g_ref, *rest):
        o_ref, ot_ref = rest[-2:]
        xv = x_ref[...]
        r = lax.rsqrt(jnp.mean(xv * xv, axis=-1, keepdims=True) + NORM_EPS)
        hb = (xv * r * g_ref[...]).astype(bf16)
        o_ref[...] = hb
        ot_ref[...] = hb.T

    return pl.pallas_call(
        body, name=name, grid=(S // tm,),
        in_specs=[pl.BlockSpec((tm, D), lambda i: (i, 0)), pl.BlockSpec((1, D), lambda i: (0, 0))] + [ANY] * len(deps),
        out_specs=[pl.BlockSpec((tm, D), lambda i: (i, 0)), pl.BlockSpec((D, tm), lambda i: (0, i))],
        out_shape=[jax.ShapeDtypeStruct((S, D), bf16), jax.ShapeDtypeStruct((D, S), bf16)],
        compiler_params=_params("parallel"),
    )(x, g, *deps)


def rmsnorm_bwd(dh, x, g, dres, name, deps=()):
    S, D = x.shape
    tm = _tile(S, 512)
    n = S // tm

    def body(dh_ref, x_ref, g_ref, dres_ref, *rest):
        dx_ref, dxb_ref, dg_ref, acc = rest[-4:]
        i = pl.program_id(0)

        @pl.when(i == 0)
        def _():
            acc[...] = jnp.zeros_like(acc)

        xv = x_ref[...]
        r = lax.rsqrt(jnp.mean(xv * xv, axis=-1, keepdims=True) + NORM_EPS)
        xh = xv * r
        dy = dh_ref[...]
        gy = dy * g_ref[...]
        dx = dres_ref[...] + r * (gy - xh * jnp.mean(gy * xh, axis=-1, keepdims=True))
        dx_ref[...] = dx
        dxb_ref[...] = dx.astype(bf16)
        acc[...] += _rowsum8(dy * xh)

        @pl.when(i == n - 1)
        def _():
            dg_ref[...] = jnp.sum(acc[...], axis=0, keepdims=True)

    row = pl.BlockSpec((tm, D), lambda i: (i, 0))
    vec = pl.BlockSpec((1, D), lambda i: (0, 0))
    return pl.pallas_call(
        body, name=name, grid=(n,),
        in_specs=[row, row, vec, row] + [ANY] * len(deps),
        out_specs=[row, row, vec],
        out_shape=[jax.ShapeDtypeStruct((S, D), f32), jax.ShapeDtypeStruct((S, D), bf16),
                   jax.ShapeDtypeStruct((1, D), f32)],
        scratch_shapes=[pltpu.VMEM((8, D), f32)],
        compiler_params=_params("arbitrary"),
    )(dh, x, g, dres, *deps)


def final_loss(x, g, tgt, name):
    S, D = x.shape
    tm = _tile(S, 256)
    n = S // tm

    def body(x_ref, g_ref, t_ref, dx_ref, dxb_ref, dg_ref, loss_ref, acc):
        i = pl.program_id(0)

        @pl.when(i == 0)
        def _():
            acc[...] = jnp.zeros_like(acc)
            loss_ref[...] = jnp.zeros_like(loss_ref)

        xv = x_ref[...]
        gv = g_ref[...]
        r = lax.rsqrt(jnp.mean(xv * xv, axis=-1, keepdims=True) + NORM_EPS)
        xh = xv * r
        diff = xh * gv - t_ref[...]
        tok = jnp.mean(diff * diff, axis=-1, keepdims=True)
        loss_ref[...] += 0.5 * jnp.sum(tok, axis=0, keepdims=True)
        dy = diff / D
        gy = dy * gv
        dx = r * (gy - xh * jnp.mean(gy * xh, axis=-1, keepdims=True))
        dx_ref[...] = dx
        dxb_ref[...] = dx.astype(bf16)
        acc[...] += _rowsum8(dy * xh)

        @pl.when(i == n - 1)
        def _():
            dg_ref[...] = jnp.sum(acc[...], axis=0, keepdims=True)

    row = pl.BlockSpec((tm, D), lambda i: (i, 0))
    vec = pl.BlockSpec((1, D), lambda i: (0, 0))
    return pl.pallas_call(
        body, name=name, grid=(n,),
        in_specs=[row, vec, row],
        out_specs=[row, row, vec, pl.BlockSpec((1, 1), lambda i: (0, 0))],
        out_shape=[jax.ShapeDtypeStruct((S, D), f32), jax.ShapeDtypeStruct((S, D), bf16),
                   jax.ShapeDtypeStruct((1, D), f32), jax.ShapeDtypeStruct((1, 1), f32)],
        scratch_shapes=[pltpu.VMEM((8, D), f32)],
        compiler_params=_params("arbitrary"),
    )(x, g, tgt)


def ffn_in(h, w2, name):
    S, D = h.shape
    F = w2.shape[2]
    tm, tn = _tile(S, MM_ROWS), _tile(F, 512)

    def body(h_ref, w_ref, gu_ref, a_ref, at_ref):
        hv = h_ref[...]
        g = _dot(hv, w_ref[0])
        u = _dot(hv, w_ref[1])
        gu_ref[0] = g.astype(bf16)
        gu_ref[1] = u.astype(bf16)
        a = (g * _sig(g) * u).astype(bf16)
        a_ref[...] = a
        at_ref[...] = a.T

    return pl.pallas_call(
        body, name=name, grid=(S // tm, F // tn),
        in_specs=[pl.BlockSpec((tm, D), lambda i, j: (i, 0)), pl.BlockSpec((2, D, tn), lambda i, j: (0, 0, j))],
        out_specs=[pl.BlockSpec((2, tm, tn), lambda i, j: (0, i, j)), pl.BlockSpec((tm, tn), lambda i, j: (i, j)),
                   pl.BlockSpec((tn, tm), lambda i, j: (j, i))],
        out_shape=[jax.ShapeDtypeStruct((2, S, F), bf16), jax.ShapeDtypeStruct((S, F), bf16),
                   jax.ShapeDtypeStruct((F, S), bf16)],
        compiler_params=_params("parallel", "parallel"),
    )(h, w2)


def mm_res(a, w, x, scale, name, deps=()):
    S, K = a.shape
    N = w.shape[1]
    tm, tn = _row_tile(S, K, MM_ROWS), _tile(N, 512)

    def body(a_ref, w_ref, x_ref, *rest):
        rest[-1][...] = x_ref[...] + scale * _dot(a_ref[...], w_ref[...])

    return pl.pallas_call(
        body, name=name, grid=(S // tm, N // tn),
        in_specs=[pl.BlockSpec((tm, K), lambda i, j: (i, 0)), pl.BlockSpec((K, tn), lambda i, j: (0, j)),
                  pl.BlockSpec((tm, tn), lambda i, j: (i, j))] + [ANY] * len(deps),
        out_specs=pl.BlockSpec((tm, tn), lambda i, j: (i, j)),
        out_shape=jax.ShapeDtypeStruct((S, N), f32),
        compiler_params=_params("parallel", "parallel"),
    )(a, w, x, *deps)


def mm_nn(a, b, name, out_dtype=f32, scale=1.0, deps=()):
    M, K = a.shape
    G, _, N = b.shape
    tm, tn = _row_tile(M, K), _tile(N, 512)

    def body(a_ref, b_ref, *rest):
        acc = _dot(a_ref[...], b_ref[...])
        rest[-1][...] = (acc if scale == 1.0 else scale * acc).astype(out_dtype)

    return pl.pallas_call(
        body, name=name, grid=(G, M // tm, N // tn),
        in_specs=[pl.BlockSpec((tm, K), lambda g, i, j: (i, 0)),
                  pl.BlockSpec((None, K, tn), lambda g, i, j: (g, 0, j))] + [ANY] * len(deps),
        out_specs=pl.BlockSpec((None, tm, tn), lambda g, i, j: (g, i, j)),
        out_shape=jax.ShapeDtypeStruct((G, M, N), out_dtype),
        compiler_params=_params("parallel", "parallel", "parallel"),
    )(a, b, *deps)


def mm_nt(a, w, name, deps=()):
    G, S, K = a.shape
    N = w.shape[1]
    tm, tn = _row_tile(S, K), _tile(N, 512)

    def body(a_ref, w_ref, *rest):
        o_ref = rest[-1]
        part = _dot(a_ref[...], w_ref[...], NT)
        if G == 1:
            o_ref[...] = part
        else:
            g = pl.program_id(2)

            @pl.when(g == 0)
            def _():
                o_ref[...] = part

            @pl.when(g > 0)
            def _():
                o_ref[...] += part

    return pl.pallas_call(
        body, name=name, grid=(S // tm, N // tn, G),
        in_specs=[pl.BlockSpec((None, tm, K), lambda i, j, g: (g, i, 0)),
                  pl.BlockSpec((None, tn, K), lambda i, j, g: (g, j, 0))] + [ANY] * len(deps),
        out_specs=pl.BlockSpec((tm, tn), lambda i, j, g: (i, j)),
        out_shape=jax.ShapeDtypeStruct((S, N), f32),
        compiler_params=_params("parallel", "parallel", "arbitrary"),
    )(a, w, *deps)


def ffn_dact(dx, wout, gu, name):
    S, D = dx.shape
    F = wout.shape[0]
    tm, tn = _tile(S, MM_ROWS), _tile(F, 512)

    def body(dx_ref, w_ref, gu_ref, o_ref):
        da = FFN_RES * _dot(dx_ref[...], w_ref[...], NT)
        g = gu_ref[0].astype(f32)
        u = gu_ref[1].astype(f32)
        sg = _sig(g)
        o_ref[0] = (da * u * (sg * (1.0 + g * (1.0 - sg)))).astype(bf16)
        o_ref[1] = (da * (g * sg)).astype(bf16)

    return pl.pallas_call(
        body, name=name, grid=(S // tm, F // tn),
        in_specs=[pl.BlockSpec((tm, D), lambda i, j: (i, 0)), pl.BlockSpec((tn, D), lambda i, j: (j, 0)),
                  pl.BlockSpec((2, tm, tn), lambda i, j: (0, i, j))],
        out_specs=pl.BlockSpec((2, tm, tn), lambda i, j: (0, i, j)),
        out_shape=jax.ShapeDtypeStruct((2, S, F), bf16),
        compiler_params=_params("parallel", "parallel"),
    )(dx, wout, gu)


def _rope(t, c, s1, s2):
    w = t.shape[1]
    return t * c + pltpu.roll(t, 8, 1) * s1 + pltpu.roll(t, w - 8, 1) * s2


def _rope_t(d, c, s1, s2):
    w = d.shape[1]
    return d * c + pltpu.roll(d * s1, w - 8, 1) + pltpu.roll(d * s2, 8, 1)


def _attn_mask(n):
    qi = lax.broadcasted_iota(jnp.int32, (BLK, 2 * BLK), 0)
    kj = lax.broadcasted_iota(jnp.int32, (BLK, 2 * BLK), 1)
    dist = qi + BLK - kj
    return (dist >= 0) & (dist < BLK) & ((kj >= BLK) | (n > 0))


def _softmax_sink(s, valid, sk):
    s = jnp.where(valid, s, -1e30)
    m = jnp.maximum(jnp.max(s, axis=-1, keepdims=True), sk)
    e = jnp.exp(s - m)
    es = jnp.exp(sk - m)
    inv = 1.0 / (jnp.sum(e, axis=-1, keepdims=True) + es)
    return e * inv, es * inv


def attn_fwd(p, rope_c, rope_s1, rope_s2, sinks, name):
    S = p.shape[0]
    nb = S // BLK
    kvb = Q_END // KV_W

    def body(sink_ref, q_ref, kvc_ref, kvp_ref, cc_ref, s1c_ref, s2c_ref, cp_ref, s1p_ref, s2p_ref, o_ref):
        n = pl.program_id(0)
        cc, s1c, s2c = cc_ref[...], s1c_ref[...], s2c_ref[...]
        cp, s1p, s2p = cp_ref[...], s1p_ref[...], s2p_ref[...]
        q = _rope(q_ref[...], jnp.tile(cc, (1, 8)), jnp.tile(s1c, (1, 8)), jnp.tile(s2c, (1, 8)))
        kc = _rope(kvc_ref[:, :256], jnp.tile(cc, (1, 2)), jnp.tile(s1c, (1, 2)), jnp.tile(s2c, (1, 2)))
        kp = _rope(kvp_ref[:, :256], jnp.tile(cp, (1, 2)), jnp.tile(s1p, (1, 2)), jnp.tile(s2p, (1, 2)))
        k = jnp.concatenate([kp, kc], axis=0).astype(bf16)
        v = jnp.concatenate([kvp_ref[:, 256:], kvc_ref[:, 256:]], axis=0).astype(bf16)
        q = q.astype(bf16)
        valid = _attn_mask(n)
        for h in range(N_KV_HEADS):
            kh = k[:, h * HEAD_DIM:(h + 1) * HEAD_DIM]
            vh = v[:, h * HEAD_DIM:(h + 1) * HEAD_DIM]
            for g in range(GQ):
                hq = h * GQ + g
                qh = q[:, hq * HEAD_DIM:(hq + 1) * HEAD_DIM]
                s = _dot(qh, kh, NT) * (HEAD_DIM ** -0.5)
                pr, _ = _softmax_sink(s, valid, sink_ref[hq])
                o = _dot(pr.astype(bf16), vh)
                o_ref[:, hq * HEAD_DIM:(hq + 1) * HEAD_DIM] = o.astype(bf16)

    tab_c = pl.BlockSpec((BLK, LANE), lambda n: (n, 0))
    tab_p = pl.BlockSpec((BLK, LANE), lambda n: (jnp.maximum(n - 1, 0), 0))
    return pl.pallas_call(
        body, name=name, grid=(nb,),
        in_specs=[pl.BlockSpec(memory_space=pltpu.SMEM),
                  pl.BlockSpec((BLK, Q_END), lambda n: (n, 0)),
                  pl.BlockSpec((BLK, KV_W), lambda n: (n, kvb)),
                  pl.BlockSpec((BLK, KV_W), lambda n: (jnp.maximum(n - 1, 0), kvb)),
                  tab_c, tab_c, tab_c, tab_p, tab_p, tab_p],
        out_specs=pl.BlockSpec((BLK, Q_END), lambda n: (n, 0)),
        out_shape=jax.ShapeDtypeStruct((S, Q_END), bf16),
        compiler_params=_params("parallel"),
    )(sinks, p, p, p, rope_c, rope_s1, rope_s2, rope_c, rope_s1, rope_s2)


def attn_bwd(p, dcat, rope_c, rope_s1, rope_s2, sinks, name):
    S = p.shape[0]
    nb = S // BLK
    kvb = Q_END // KV_W

    def body(sink_ref, q_ref, kvc_ref, kvp_ref, do_ref, cc_ref, s1c_ref, s2c_ref, cp_ref, s1p_ref, s2p_ref,
             dq_ref, dkv_ref, dsink_ref, carry, dq_scr, dkv_scr):
        n = pl.program_id(0)

        @pl.when(n == 0)
        def _():
            carry[...] = jnp.zeros_like(carry)
            dsink_ref[...] = jnp.zeros_like(dsink_ref)

        cp, s1p, s2p = cp_ref[...], s1p_ref[...], s2p_ref[...]
        cp2, s1p2, s2p2 = jnp.tile(cp, (1, 2)), jnp.tile(s1p, (1, 2)), jnp.tile(s2p, (1, 2))

        @pl.when(n < nb)
        def _():
            cc, s1c, s2c = cc_ref[...], s1c_ref[...], s2c_ref[...]
            cc8, s1c8, s2c8 = jnp.tile(cc, (1, 8)), jnp.tile(s1c, (1, 8)), jnp.tile(s2c, (1, 8))
            q = _rope(q_ref[...], cc8, s1c8, s2c8).astype(bf16)
            kc = _rope(kvc_ref[:, :256], jnp.tile(cc, (1, 2)), jnp.tile(s1c, (1, 2)), jnp.tile(s2c, (1, 2)))
            kp = _rope(kvp_ref[:, :256], cp2, s1p2, s2p2)
            k = jnp.concatenate([kp, kc], axis=0).astype(bf16)
            v = jnp.concatenate([kvp_ref[:, 256:], kvc_ref[:, 256:]], axis=0).astype(bf16)
            do = do_ref[...].astype(bf16)
            valid = _attn_mask(n)
            lane = lax.broadcasted_iota(jnp.int32, (1, LANE), 1)
            dsink = jnp.zeros((1, LANE), f32)
            for h in range(N_KV_HEADS):
                kh = k[:, h * HEAD_DIM:(h + 1) * HEAD_DIM]
                vh = v[:, h * HEAD_DIM:(h + 1) * HEAD_DIM]
                dkh = jnp.zeros((2 * BLK, HEAD_DIM), f32)
                dvh = jnp.zeros((2 * BLK, HEAD_DIM), f32)
                for g in range(GQ):
                    hq = h * GQ + g
                    qh = q[:, hq * HEAD_DIM:(hq + 1) * HEAD_DIM]
                    doh = do[:, hq * HEAD_DIM:(hq + 1) * HEAD_DIM]
                    s = _dot(qh, kh, NT) * (HEAD_DIM ** -0.5)
                    pr, ps = _softmax_sink(s, valid, sink_ref[hq])
                    dpr = _dot(doh, vh, NT)
                    dvh = dvh + _dot(pr.astype(bf16), doh, TN)
                    row = jnp.sum(pr * dpr, axis=-1, keepdims=True)
                    ds = (pr * (dpr - row) * (HEAD_DIM ** -0.5)).astype(bf16)
                    dsink = dsink + jnp.where(lane == hq, -jnp.sum(ps * row, axis=0, keepdims=True), 0.0)
                    dq_scr[:, hq * HEAD_DIM:(hq + 1) * HEAD_DIM] = _dot(ds, kh)
                    dkh = dkh + _dot(ds, qh, TN)
                dkv_scr[:, h * HEAD_DIM:(h + 1) * HEAD_DIM] = dkh
                dkv_scr[:, 256 + h * HEAD_DIM:256 + (h + 1) * HEAD_DIM] = dvh
            dsink_ref[...] += dsink
            dq_ref[...] = _rope_t(dq_scr[...], cc8, s1c8, s2c8).astype(bf16)

        prev = carry[...]

        @pl.when(n < nb)
        def _():
            dkv_scr[pl.ds(0, BLK), :] = dkv_scr[pl.ds(0, BLK), :] + prev

        @pl.when(n == nb)
        def _():
            dkv_scr[pl.ds(0, BLK), :] = prev

        done = dkv_scr[pl.ds(0, BLK), :]
        dkv_ref[:, :256] = _rope_t(done[:, :256], cp2, s1p2, s2p2).astype(bf16)
        dkv_ref[:, 256:] = done[:, 256:].astype(bf16)

        @pl.when(n < nb)
        def _():
            carry[...] = dkv_scr[pl.ds(BLK, BLK), :]

    cur = lambda n: jnp.minimum(n, nb - 1)
    prv = lambda n: jnp.maximum(n - 1, 0)
    tab_c = pl.BlockSpec((BLK, LANE), lambda n: (cur(n), 0))
    tab_p = pl.BlockSpec((BLK, LANE), lambda n: (prv(n), 0))
    return pl.pallas_call(
        body, name=name, grid=(nb + 1,),
        in_specs=[pl.BlockSpec(memory_space=pltpu.SMEM),
                  pl.BlockSpec((BLK, Q_END), lambda n: (cur(n), 0)),
                  pl.BlockSpec((BLK, KV_W), lambda n: (cur(n), kvb)),
                  pl.BlockSpec((BLK, KV_W), lambda n: (prv(n), kvb)),
                  pl.BlockSpec((BLK, Q_END), lambda n: (cur(n), 0)),
                  tab_c, tab_c, tab_c, tab_p, tab_p, tab_p],
        out_specs=[pl.BlockSpec((BLK, Q_END), lambda n: (cur(n), 0)),
                   pl.BlockSpec((BLK, KV_W), lambda n: (prv(n), 0)),
                   pl.BlockSpec((1, LANE), lambda n: (0, 0))],
        out_shape=[jax.ShapeDtypeStruct((S, Q_END), bf16), jax.ShapeDtypeStruct((S, KV_W), bf16),
                   jax.ShapeDtypeStruct((1, LANE), f32)],
        scratch_shapes=[pltpu.VMEM((BLK, KV_W), f32), pltpu.VMEM((BLK, Q_END), f32), pltpu.VMEM((2 * BLK, KV_W), f32)],
        compiler_params=_params("arbitrary"),
    )(sinks, p, p, p, dcat, rope_c, rope_s1, rope_s2, rope_c, rope_s1, rope_s2)


CONV_ROWS = 32
A1_BLK = V_END // CONV_CH
A2_BLK = A1_BLK + 1


def _ln_stats(y):
    mu = jnp.mean(y, axis=-1, keepdims=True)
    xc = y - mu
    rstd = lax.rsqrt(jnp.mean(xc * xc, axis=-1, keepdims=True) + NORM_EPS)
    return xc * rstd, rstd


def conv_fwd(p, w, b, lng, lnb, name):
    S = p.shape[0]
    T = _tile(S, 256)
    r = T // HALO

    def body(a1_ref, a2_ref, h1_ref, h2_ref, w_ref, b_ref, g_ref, bb_ref, o_ref, y_ref, scr):
        i = pl.program_id(0)
        halo = h1_ref[...] * _sig(h2_ref[...])
        scr[pl.ds(0, HALO), :] = jnp.where(i > 0, halo, 0.0)
        scr[pl.ds(HALO, T), :] = a1_ref[...] * _sig(a2_ref[...])
        acc = jnp.zeros((T, CONV_CH), f32) + b_ref[...]
        for j in range(CONV_W):
            acc = acc + scr[pl.ds(HALO - (CONV_W - 1) + j, T), :] * w_ref[j:j + 1, :]
        y_ref[...] = acc
        yh, _ = _ln_stats(acc)
        z = yh * g_ref[...] + bb_ref[...]
        o_ref[...] = (z * _sig(z)).astype(bf16)

    vec = pl.BlockSpec((1, CONV_CH), lambda i: (0, 0))
    halo_map = lambda i: jnp.maximum(i * r - 1, 0)
    return pl.pallas_call(
        body, name=name, grid=(S // T,),
        in_specs=[pl.BlockSpec((T, CONV_CH), lambda i: (i, A1_BLK)), pl.BlockSpec((T, CONV_CH), lambda i: (i, A2_BLK)),
                  pl.BlockSpec((HALO, CONV_CH), lambda i: (halo_map(i), A1_BLK)),
                  pl.BlockSpec((HALO, CONV_CH), lambda i: (halo_map(i), A2_BLK)),
                  pl.BlockSpec((HALO, CONV_CH), lambda i: (0, 0)), vec, vec, vec],
        out_specs=[pl.BlockSpec((T, CONV_CH), lambda i: (i, 0)), pl.BlockSpec((T, CONV_CH), lambda i: (i, 0))],
        out_shape=[jax.ShapeDtypeStruct((S, CONV_CH), bf16), jax.ShapeDtypeStruct((S, CONV_CH), f32)],
        scratch_shapes=[pltpu.VMEM((T + HALO, CONV_CH), f32)],
        compiler_params=_params("parallel"),
    )(p, p, p, p, w, b, lng, lnb)


def conv_bwd(p, y, dcat, w, lng, lnb, name):
    S = p.shape[0]
    T = _tile(S, 256)
    n = S // T
    r = T // HALO
    dcb = Q_END // CONV_CH
    rc = _tile(T, CONV_ROWS)

    def body(a1_ref, a2_ref, h1_ref, h2_ref, y_ref, yn_ref, do_ref, don_ref, w_ref, g_ref, bb_ref,
             da_ref, dw_ref, db_ref, dg_ref, dbb_ref, scr_h, scr_dy, acc_b, acc_g, acc_bb):
        i = pl.program_id(0)

        @pl.when(i == 0)
        def _():
            dw_ref[...] = jnp.zeros_like(dw_ref)
            acc_b[...] = jnp.zeros_like(acc_b)
            acc_g[...] = jnp.zeros_like(acc_g)
            acc_bb[...] = jnp.zeros_like(acc_bb)

        gv, bv = g_ref[...], bb_ref[...]

        def ln_silu_bwd(yv, dout):
            yh, rstd = _ln_stats(yv)
            z = yh * gv + bv
            sg = _sig(z)
            dz = dout * (sg * (1.0 + z * (1.0 - sg)))
            gz = dz * gv
            dy = rstd * (gz - jnp.mean(gz, axis=-1, keepdims=True) - yh * jnp.mean(gz * yh, axis=-1, keepdims=True))
            return dy, dz, yh

        chunks = [pl.ds(c * rc, rc) for c in range(T // rc)]
        dyn, _, _ = ln_silu_bwd(yn_ref[...], don_ref[...])
        scr_dy[pl.ds(T, HALO), :] = jnp.where(i < n - 1, dyn, 0.0)
        halo = h1_ref[...] * _sig(h2_ref[...])
        scr_h[pl.ds(0, HALO), :] = jnp.where(i > 0, halo, 0.0)
        for c, rows in enumerate(chunks):
            dy, dz, yh = ln_silu_bwd(y_ref[rows, :], do_ref[rows, :])
            acc_g[...] += _rowsum8(dz * yh)
            acc_bb[...] += _rowsum8(dz)
            acc_b[...] += _rowsum8(dy)
            scr_dy[rows, :] = dy
            scr_h[pl.ds(HALO + c * rc, rc), :] = a1_ref[rows, :] * _sig(a2_ref[rows, :])
        for c, rows in enumerate(chunks):
            dh = jnp.zeros((rc, CONV_CH), f32)
            for j in range(CONV_W):
                dh = dh + scr_dy[pl.ds(c * rc + CONV_W - 1 - j, rc), :] * w_ref[j:j + 1, :]
            a1 = a1_ref[rows, :]
            sg2 = _sig(a2_ref[rows, :])
            da_ref[rows, :CONV_CH] = (dh * sg2).astype(bf16)
            da_ref[rows, CONV_CH:] = (dh * a1 * sg2 * (1.0 - sg2)).astype(bf16)
        for j in range(CONV_W):
            part = jnp.zeros((8, CONV_CH), f32)
            for c, rows in enumerate(chunks):
                part = part + _rowsum8(scr_dy[rows, :] * scr_h[pl.ds(c * rc + HALO - (CONV_W - 1) + j, rc), :])
            dw_ref[j:j + 1, :] += jnp.sum(part, axis=0, keepdims=True)

        @pl.when(i == n - 1)
        def _():
            db_ref[...] = jnp.sum(acc_b[...], axis=0, keepdims=True)
            dg_ref[...] = jnp.sum(acc_g[...], axis=0, keepdims=True)
            dbb_ref[...] = jnp.sum(acc_bb[...], axis=0, keepdims=True)

    vec = pl.BlockSpec((1, CONV_CH), lambda i: (0, 0))
    tap = pl.BlockSpec((HALO, CONV_CH), lambda i: (0, 0))
    prev_map = lambda i: jnp.maximum(i * r - 1, 0)
    next_map = lambda i: jnp.minimum((i + 1) * r, S // HALO - 1)
    return pl.pallas_call(
        body, name=name, grid=(n,),
        in_specs=[pl.BlockSpec((T, CONV_CH), lambda i: (i, A1_BLK)), pl.BlockSpec((T, CONV_CH), lambda i: (i, A2_BLK)),
                  pl.BlockSpec((HALO, CONV_CH), lambda i: (prev_map(i), A1_BLK)),
                  pl.BlockSpec((HALO, CONV_CH), lambda i: (prev_map(i), A2_BLK)),
                  pl.BlockSpec((T, CONV_CH), lambda i: (i, 0)),
                  pl.BlockSpec((HALO, CONV_CH), lambda i: (next_map(i), 0)),
                  pl.BlockSpec((T, CONV_CH), lambda i: (i, dcb)),
                  pl.BlockSpec((HALO, CONV_CH), lambda i: (next_map(i), dcb)),
                  tap, vec, vec],
        out_specs=[pl.BlockSpec((T, 2 * CONV_CH), lambda i: (i, 0)), tap, vec, vec, vec],
        out_shape=[jax.ShapeDtypeStruct((S, 2 * CONV_CH), bf16), jax.ShapeDtypeStruct((HALO, CONV_CH), f32),
                   jax.ShapeDtypeStruct((1, CONV_CH), f32), jax.ShapeDtypeStruct((1, CONV_CH), f32),
                   jax.ShapeDtypeStruct((1, CONV_CH), f32)],
        scratch_shapes=[pltpu.VMEM((T + HALO, CONV_CH), f32), pltpu.VMEM((T + HALO, CONV_CH), f32),
                        pltpu.VMEM((8, CONV_CH), f32), pltpu.VMEM((8, CONV_CH), f32), pltpu.VMEM((8, CONV_CH), f32)],
        compiler_params=_params("arbitrary"),
    )(p, p, p, p, y, y, dcat, dcat, w, lng, lnb)


U_BLK = (V_END + 2 * CONV_CH) // SGU_CH
SV_BLK = U_BLK + 1


def _tril(w, transposed=False):
    row = lax.broadcasted_iota(jnp.int32, (BLK, BLK), 0)
    col = lax.broadcasted_iota(jnp.int32, (BLK, BLK), 1)
    keep = (col >= row) if transposed else (row >= col)
    return jnp.where(keep, w, 0.0)


def sgu_fwd(p, lng, lnb, w, bias, name):
    S = p.shape[0]
    T = _tile(S, 256)

    def body(u_ref, v_ref, g_ref, bb_ref, w_ref, bias_ref, o_ref):
        yh, _ = _ln_stats(v_ref[...])
        v = (yh * g_ref[...] + bb_ref[...]).astype(bf16)
        low = lax.broadcasted_iota(jnp.int32, (BLK, LANE), 1) < HEAD_DIM
        for pr in range(SGU_HEADS // 2):
            lanes = pl.ds(pr * LANE, LANE)
            w0 = _tril(w_ref[2 * pr]).astype(bf16)
            w1 = _tril(w_ref[2 * pr + 1]).astype(bf16)
            for c in range(T // BLK):
                rows = pl.ds(c * BLK, BLK)
                vp = v[c * BLK:(c + 1) * BLK, pr * LANE:(pr + 1) * LANE]
                mixed = jnp.where(low, _dot(w0, vp), _dot(w1, vp)) + bias_ref[:, lanes]
                o_ref[rows, lanes] = (u_ref[rows, lanes] * mixed).astype(bf16)

    vec = pl.BlockSpec((1, SGU_CH), lambda i: (0, 0))
    return pl.pallas_call(
        body, name=name, grid=(S // T,),
        in_specs=[pl.BlockSpec((T, SGU_CH), lambda i: (i, U_BLK)), pl.BlockSpec((T, SGU_CH), lambda i: (i, SV_BLK)),
                  vec, vec, pl.BlockSpec((SGU_HEADS, BLK, BLK), lambda i: (0, 0, 0)),
                  pl.BlockSpec((BLK, SGU_CH), lambda i: (0, 0))],
        out_specs=pl.BlockSpec((T, SGU_CH), lambda i: (i, 0)),
        out_shape=jax.ShapeDtypeStruct((S, SGU_CH), bf16),
        compiler_params=_params("parallel"),
    )(p, p, lng, lnb, w, bias)


def sgu_bwd(p, dcat, lng, lnb, w, wt, bias, name):
    S = p.shape[0]
    T = _tile(S, 256)
    n = S // T
    dsb = (Q_END + CONV_CH) // SGU_CH

    def body(u_ref, v_ref, do_ref, g_ref, bb_ref, w_ref, wt_ref, bias_ref,
             da_ref, dw_ref, db_ref, dg_ref, dbb_ref, dv_scr, acc_bias, acc_g, acc_bb):
        i = pl.program_id(0)

        @pl.when(i == 0)
        def _():
            dw_ref[...] = jnp.zeros_like(dw_ref)
            acc_bias[...] = jnp.zeros_like(acc_bias)
            acc_g[...] = jnp.zeros_like(acc_g)
            acc_bb[...] = jnp.zeros_like(acc_bb)

        gv = g_ref[...]
        yh, rstd = _ln_stats(v_ref[...])
        v = (yh * gv + bb_ref[...]).astype(bf16)
        low = lax.broadcasted_iota(jnp.int32, (BLK, LANE), 1) < HEAD_DIM
        for pr in range(SGU_HEADS // 2):
            lanes = pl.ds(pr * LANE, LANE)
            w0 = _tril(w_ref[2 * pr]).astype(bf16)
            w1 = _tril(w_ref[2 * pr + 1]).astype(bf16)
            wt0 = _tril(wt_ref[2 * pr], True).astype(bf16)
            wt1 = _tril(wt_ref[2 * pr + 1], True).astype(bf16)
            dw0 = jnp.zeros((BLK, BLK), f32)
            dw1 = jnp.zeros((BLK, BLK), f32)
            for c in range(T // BLK):
                rows = pl.ds(c * BLK, BLK)
                vp = v[c * BLK:(c + 1) * BLK, pr * LANE:(pr + 1) * LANE]
                mixed = jnp.where(low, _dot(w0, vp), _dot(w1, vp)) + bias_ref[:, lanes]
                do = do_ref[rows, lanes]
                da_ref[rows, lanes] = (do * mixed).astype(bf16)
                dm = do * u_ref[rows, lanes]
                acc_bias[:, lanes] += dm
                dmb = dm.astype(bf16)
                dv_scr[rows, lanes] = jnp.where(low, _dot(wt0, dmb), _dot(wt1, dmb))
                zero = jnp.zeros_like(dmb)
                dw0 = dw0 + _dot(jnp.where(low, dmb, zero), vp, NT)
                dw1 = dw1 + _dot(jnp.where(low, zero, dmb), vp, NT)
            dw_ref[2 * pr] += _tril(dw0)
            dw_ref[2 * pr + 1] += _tril(dw1)
        dv = dv_scr[...]
        acc_g[...] += _rowsum8(dv * yh)
        acc_bb[...] += _rowsum8(dv)
        gz = dv * gv
        dvr = rstd * (gz - jnp.mean(gz, axis=-1, keepdims=True) - yh * jnp.mean(gz * yh, axis=-1, keepdims=True))
        da_ref[:, SGU_CH:] = dvr.astype(bf16)

        @pl.when(i == n - 1)
        def _():
            ch = lax.broadcasted_iota(jnp.int32, (SGU_CH, LANE), 0) // HEAD_DIM
            hd = lax.broadcasted_iota(jnp.int32, (SGU_CH, LANE), 1)
            fold = jnp.where(ch == hd, 1.0, 0.0).astype(f32)
            db_ref[...] = jnp.dot(acc_bias[...], fold, preferred_element_type=f32, precision=lax.Precision.HIGHEST)
            dg_ref[...] = jnp.sum(acc_g[...], axis=0, keepdims=True)
            dbb_ref[...] = jnp.sum(acc_bb[...], axis=0, keepdims=True)

    vec = pl.BlockSpec((1, SGU_CH), lambda i: (0, 0))
    wsp = pl.BlockSpec((SGU_HEADS, BLK, BLK), lambda i: (0, 0, 0))
    return pl.pallas_call(
        body, name=name, grid=(n,),
        in_specs=[pl.BlockSpec((T, SGU_CH), lambda i: (i, U_BLK)), pl.BlockSpec((T, SGU_CH), lambda i: (i, SV_BLK)),
                  pl.BlockSpec((T, SGU_CH), lambda i: (i, dsb)), vec, vec, wsp, wsp,
                  pl.BlockSpec((BLK, SGU_CH), lambda i: (0, 0))],
        out_specs=[pl.BlockSpec((T, 2 * SGU_CH), lambda i: (i, 0)), wsp,
                   pl.BlockSpec((BLK, LANE), lambda i: (0, 0)), vec, vec],
        out_shape=[jax.ShapeDtypeStruct((S, 2 * SGU_CH), bf16), jax.ShapeDtypeStruct((SGU_HEADS, BLK, BLK), f32),
                   jax.ShapeDtypeStruct((BLK, LANE), f32), jax.ShapeDtypeStruct((1, SGU_CH), f32),
                   jax.ShapeDtypeStruct((1, SGU_CH), f32)],
        scratch_shapes=[pltpu.VMEM((T, SGU_CH), f32), pltpu.VMEM((BLK, SGU_CH), f32),
                        pltpu.VMEM((8, SGU_CH), f32), pltpu.VMEM((8, SGU_CH), f32)],
        compiler_params=_params("arbitrary"),
    )(p, p, dcat, lng, lnb, w, wt, bias)


HBM = pl.BlockSpec(memory_space=pltpu.HBM)
SEM = pl.BlockSpec(memory_space=pltpu.SEMAPHORE)
ANY = pl.BlockSpec(memory_space=pl.ANY)
EFFECT = pltpu.SideEffectType.DATAFLOW_SIDE_EFFECTING


def _flip(x, y, c, k):
    px, py, pc = x ^ (k >> 2), y ^ ((k >> 1) & 1), c ^ (k & 1)
    return (px, py, pc), 4 * px + 2 * py + pc


def _routes_gather(x, y, c):
    me = 4 * x + 2 * y + c
    out = []
    for k in (1, 2, 4, 6):
        dev, idx = _flip(x, y, c, k)
        out.append((dev, None, me, idx))
    return out


def _routes_pair(x, y, c):
    dev, _ = _flip(x, y, c, 1)
    return [(dev, 2 * q + (1 - c), q, q) for q in range(N_DEV // 2)]


def _routes_chips(x, y, c):
    out = []
    for k in (2, 4, 6):
        dev, idx = _flip(x, y, c, k)
        out.append((dev, idx // 2, 2 * x + y, idx // 2))
    return out


def _routes_forward(x, y, c):
    sib, _ = _flip(x, y, c, 1)
    out = []
    for k in (2, 4, 6):
        _, idx = _flip(x, y, c, k)
        out.append((sib, idx, idx, idx ^ 1))
    return out


def _routes_all(x, y, c):
    me = 4 * x + 2 * y + c
    out = []
    for k in range(1, N_DEV):
        dev, idx = _flip(x, y, c, k)
        out.append((dev, None, me, idx))
    return out


def _slot(ref, slot, kind):
    if slot is None:
        return ref
    if kind == "cols":
        width = ref.shape[2] // (N_DEV // 2)
        return ref.at[slot // (N_DEV // 2), :, pl.ds(pl.multiple_of((slot % (N_DEV // 2)) * width, LANE), width)]
    return ref.at[slot]


def _copies(routes, srcs, lands, send_sems, recv_sems, incoming, src_kinds, land_kinds):
    x, y, c = lax.axis_index("x"), lax.axis_index("y"), lax.axis_index("c")
    out = []
    n = len(lands)
    if srcs is None:
        srcs, src_kinds = lands, land_kinds
    for k, (dev, src_slot, dst_slot, recv_slot) in enumerate(routes(x, y, c)):
        for a in range(n):
            out.append(pltpu.make_async_remote_copy(
                src_ref=_slot(srcs[a], src_slot, src_kinds[a]),
                dst_ref=_slot(lands[a], recv_slot if incoming else dst_slot, land_kinds[a]),
                send_sem=send_sems.at[k * n + a], recv_sem=recv_sems.at[k * n + a], device_id=dev, device_id_type=MESH))
    return out


def _pin(a):
    return pltpu.with_memory_space_constraint(a, pltpu.HBM)


def split_start(srcs, lands, routes, name, deps=(), src_kinds=None, land_kinds=None):
    n = len(lands)
    ns = 0 if srcs is None else n
    n_routes = len(routes(0, 0, 0))
    ops = ([] if srcs is None else list(srcs)) + list(lands)
    src_kinds = src_kinds or ["rows"] * n
    land_kinds = land_kinds or ["rows"] * n

    def body(*refs):
        src, land = (refs[:n] if ns else None), refs[ns:ns + n]
        first_out = ns + n + len(deps)
        send_sems, recv_sems, token = refs[first_out], refs[first_out + 1], refs[-1]
        for cp in _copies(routes, src, land, send_sems, recv_sems, False, src_kinds, land_kinds):
            cp.start()
        token[...] = jnp.zeros_like(token)

    thru = [pltpu.HBM(a.shape, a.dtype) for a in ops]
    res = pl.pallas_call(
        body, name=name,
        out_shape=(pltpu.SemaphoreType.DMA((n * n_routes,)), pltpu.SemaphoreType.DMA((n * n_routes,)), *thru,
                   jax.ShapeDtypeStruct((8, LANE), f32)),
        in_specs=[HBM] * len(ops) + [ANY] * len(deps),
        out_specs=(SEM, SEM, *([HBM] * len(ops)), pl.BlockSpec(memory_space=pltpu.VMEM)),
        input_output_aliases={i: 2 + i for i in range(len(ops))},
        compiler_params=pltpu.CompilerParams(has_side_effects=EFFECT),
    )(*[_pin(a) for a in ops], *deps)
    return (res[0], res[1], (list(res[2:2 + n]) if ns else None), list(res[2 + ns:2 + ns + n]), res[-1],
            (src_kinds, land_kinds))


def split_wait(started, after, routes, name):
    send_sems, recv_sems, srcs, lands, _, (src_kinds, land_kinds) = started
    n = len(lands)
    ns = 0 if srcs is None else n
    ops = ([] if srcs is None else list(srcs)) + list(lands)
    afters = list(after) if isinstance(after, (list, tuple)) else [after]

    def body(*refs):
        src, land = (refs[:n] if ns else None), refs[ns:ns + n]
        send_s, recv_s = refs[ns + n], refs[ns + n + 1]
        for cp in _copies(routes, src, land, send_s, recv_s, True, src_kinds, land_kinds):
            cp.wait_send()
            cp.wait_recv()

    thru = [pltpu.HBM(a.shape, a.dtype) for a in ops]
    res = pl.pallas_call(
        body, name=name, out_shape=tuple(thru),
        in_specs=[HBM] * len(ops) + [SEM, SEM] + [ANY] * len(afters), out_specs=tuple([HBM] * len(ops)),
        input_output_aliases={i: i for i in range(len(ops))},
        compiler_params=pltpu.CompilerParams(has_side_effects=EFFECT),
    )(*ops, send_sems, recv_sems, *afters)
    return (list(res[:n]) if ns else None), list(res[ns:ns + n])


def chip_sum(parts, land, c_idx, kind, name):
    _, R, C = land.shape
    tr = R if R * C * 2 <= 3 * 2 ** 20 else _tile(R, 512)
    half = N_DEV // 2

    def body(c_ref, p_ref, l_ref, o_ref):
        o_ref[...] = (p_ref[...].astype(f32) + l_ref[...].astype(f32)).astype(bf16)

    if kind == "cols":
        mine = lambda q, i, c_ref: ((2 * q + c_ref[0]) // half, i, (2 * q + c_ref[0]) % half)
    else:
        mine = lambda q, i, c_ref: (2 * q + c_ref[0], i, 0)
    return pl.pallas_call(
        body, name=name,
        grid_spec=pltpu.PrefetchScalarGridSpec(
            num_scalar_prefetch=1, grid=(half, R // tr),
            in_specs=[pl.BlockSpec((None, tr, C), mine), pl.BlockSpec((None, tr, C), lambda q, i, c_ref: (q, i, 0))],
            out_specs=pl.BlockSpec((None, tr, C), lambda q, i, c_ref: (q, i, 0))),
        out_shape=jax.ShapeDtypeStruct((half, R, C), bf16),
        compiler_params=_params("parallel", "parallel"),
    )(c_idx, parts, land)


def place_own(land, src, me_idx, kind, name):
    R, C = src.shape
    tr = _tile(R, 512)
    half = N_DEV // 2
    if kind == "cols":
        where = lambda i, m: (m[0] // half, i, m[0] % half)
    else:
        where = lambda i, m: (m[0], i, 0)

    def body(m_ref, land_ref, src_ref, out_ref):
        out_ref[...] = src_ref[...]

    return pl.pallas_call(
        body, name=name,
        grid_spec=pltpu.PrefetchScalarGridSpec(
            num_scalar_prefetch=1, grid=(R // tr,),
            in_specs=[ANY, pl.BlockSpec((tr, C), lambda i, m: (i, 0))],
            out_specs=pl.BlockSpec((None, tr, C), where)),
        out_shape=jax.ShapeDtypeStruct(land.shape, land.dtype),
        input_output_aliases={1: 0},
        compiler_params=_params("arbitrary"),
    )(me_idx, land, src)


def sum_slots(parts, name):
    P, R, C = parts.shape
    tr = _tile(R, 512)

    def body(p_ref, o_ref):
        total = p_ref[0]
        for j in range(1, P):
            total = total + p_ref[j]
        o_ref[...] = total

    return pl.pallas_call(
        body, name=name, grid=(R // tr,),
        in_specs=[pl.BlockSpec((P, tr, C), lambda i: (0, i, 0))],
        out_specs=pl.BlockSpec((tr, C), lambda i: (i, 0)),
        out_shape=jax.ShapeDtypeStruct((R, C), f32),
        compiler_params=_params("parallel"),
    )(parts)


def adamw(parts, owns, chip, w, m, v, name):
    L, R, C = w.shape
    P = parts[0].shape[0]
    tr = _tile(R, 128 if C > 1024 else 256)
    nr = R // tr
    c1 = 1.0 - ADAM_B1 ** ADAM_STEP
    c2 = 1.0 - ADAM_B2 ** ADAM_STEP
    n_own = L if owns is not None else 0

    def body(chip_ref, *refs):
        part_refs, own_refs = refs[:L], refs[L:L + n_own]
        w_ref, m_ref, v_ref, g_out, d_out, m_out, v_out = refs[L + n_own:]
        layer = pl.program_id(0)
        for l in range(L):
            @pl.when(layer == l)
            def _(l=l):
                g = None
                for q in range(P):
                    term = part_refs[l][q].astype(f32)
                    if n_own:
                        term = jnp.where(chip_ref[0] == q, own_refs[l][...].astype(f32), term)
                    g = term if g is None else g + term
                mn = ADAM_B1 * m_ref[...] + (1.0 - ADAM_B1) * g
                vn = ADAM_B2 * v_ref[...] + (1.0 - ADAM_B2) * (g * g)
                g_out[...] = g
                m_out[...] = mn
                v_out[...] = vn
                d_out[...] = -ADAM_LR * ((mn / c1) / (jnp.sqrt(vn / c2) + ADAM_EPS) + ADAM_WD * w_ref[...])

    def rows(l, a, i):
        return jnp.where(a == l, i, jnp.where(a < l, 0, nr - 1))

    def part_spec(l):
        return pl.BlockSpec((P, tr, C), lambda a, i, chip_ref: (0, rows(l, a, i), 0))

    def own_spec(l):
        return pl.BlockSpec((None, tr, C), lambda a, i, chip_ref: (chip_ref[0], rows(l, a, i), 0))

    slab = pl.BlockSpec((None, tr, C), lambda a, i, chip_ref: (a, i, 0))
    out = jax.ShapeDtypeStruct((L, R, C), f32)
    return pl.pallas_call(
        body, name=name,
        grid_spec=pltpu.PrefetchScalarGridSpec(
            num_scalar_prefetch=1, grid=(L, nr),
            in_specs=[part_spec(l) for l in range(L)] + [own_spec(l) for l in range(n_own)] + [slab, slab, slab],
            out_specs=[slab, slab, slab, slab]),
        out_shape=[out, out, out, out],
        compiler_params=_params("arbitrary", "arbitrary"),
    )(chip, *parts, *(owns or []), w, m, v)


PACK = 8 * LANE
PACK_ROWS = 256


def _pack(arrs):
    pieces = []
    for a in arrs:
        flat = a.astype(f32).reshape(-1)
        pad = (-flat.shape[0]) % PACK
        pieces.append(jnp.pad(flat, (0, pad)).reshape(-1, LANE))
    rows = sum(p.shape[0] for p in pieces)
    if rows > PACK_ROWS and rows % PACK_ROWS:
        pieces.append(jnp.zeros((PACK_ROWS - rows % PACK_ROWS, LANE), f32))
    return jnp.concatenate(pieces, axis=0)


def _unpack(buf, shapes):
    out, row = [], 0
    for shp in shapes:
        size = math.prod(shp)
        rows = (size + PACK - 1) // PACK * (PACK // LANE)
        out.append(buf[row:row + rows].reshape(-1)[:size].reshape(shp))
        row += rows
    return out


def _rope_tables(positions):
    half = ROT_DIM // 2
    inv_freq = 1.0 / (ROPE_THETA ** (jnp.arange(0, ROT_DIM, 2, dtype=f32) / ROT_DIM))
    ang = positions.astype(f32)[:, None] * inv_freq
    cos, sin = jnp.cos(ang), jnp.sin(ang)
    S = positions.shape[0]
    zeros, ones = jnp.zeros((S, half), f32), jnp.ones((S, HEAD_DIM - ROT_DIM), f32)
    rest = jnp.zeros((S, HEAD_DIM - ROT_DIM), f32)
    c = jnp.concatenate([cos, cos, ones], axis=1)
    s1 = jnp.concatenate([zeros, sin, rest], axis=1)
    s2 = jnp.concatenate([-sin, zeros, rest], axis=1)
    return tuple(jnp.tile(t, (1, LANE // HEAD_DIM)) for t in (c, s1, s2))


def _cols_to_shards(g):
    lead, (R, N) = g.shape[:-2], g.shape[-2:]
    g = g.reshape(lead + (R, N_DEV, N // N_DEV))
    return jnp.moveaxis(g, -2, 0)


def _shards_to_cols(g):
    g = jnp.moveaxis(g, 0, -2)
    return g.reshape(g.shape[:-2] + (g.shape[-2] * g.shape[-1],))


def kernel(x, positions, norm_ffn1, ffn1_w_in, ffn1_w_out, norm_mix, w_in, conv_dw_w, conv_dw_b, conv_ln_g, conv_ln_b, sgu_ln_g, sgu_ln_b, sgu_w, sgu_b, attn_sinks, w_out, norm_ffn2, ffn2_w_in, ffn2_w_out, final_norm, loss_target, m_norm_ffn1, m_ffn1_w_in, m_ffn1_w_out, m_norm_mix, m_w_in, m_conv_dw_w, m_conv_dw_b, m_conv_ln_g, m_conv_ln_b, m_sgu_ln_g, m_sgu_ln_b, m_sgu_w, m_sgu_b, m_attn_sinks, m_w_out, m_norm_ffn2, m_ffn2_w_in, m_ffn2_w_out, m_final_norm, v_norm_ffn1, v_ffn1_w_in, v_ffn1_w_out, v_norm_mix, v_w_in, v_conv_dw_w, v_conv_dw_b, v_conv_ln_g, v_conv_ln_b, v_sgu_ln_g, v_sgu_ln_b, v_sgu_w, v_sgu_b, v_attn_sinks, v_w_out, v_norm_ffn2, v_ffn2_w_in, v_ffn2_w_out, v_final_norm):
    L = norm_ffn1.shape[0]
    S, D = x.shape[1], x.shape[2]
    F = ffn1_w_out.shape[1] * N_DEV
    me = 4 * lax.axis_index("x") + 2 * lax.axis_index("y") + lax.axis_index("c")
    x0 = x[0]
    rope_c, rope_s1, rope_s2 = _rope_tables(positions[0])
    cw = CONV_CH // N_DEV

    c_idx = lax.axis_index("c").astype(jnp.int32).reshape(1)
    chip = (2 * lax.axis_index("x") + lax.axis_index("y")).astype(jnp.int32).reshape(1)
    no_chip = jnp.zeros((1,), jnp.int32)
    me_idx = me.astype(jnp.int32).reshape(1)

    row = lambda a: a.reshape(1, -1)
    order_fwd = [(l, g) for l in range(L) for g in (("ffn1_in", "ffn1_out") if l == 0 else ("ffn1",)) + ("mix", "ffn2")]

    def group_srcs(l, grp):
        if grp == "mix":
            taps = jnp.pad(conv_dw_w[l], ((0, HALO - CONV_W), (0, LANE - cw)))
            return [w_in[l].astype(bf16), w_out[l].astype(bf16), taps]
        both = ([ffn2_w_in[l], ffn2_w_out[l]] if grp == "ffn2" else [ffn1_w_in[l], ffn1_w_out[l]])
        both = [a.astype(bf16) for a in both]
        return both[:1] if grp == "ffn1_in" else both[1:] if grp == "ffn1_out" else both

    def kinds_of(grp):
        return {"mix": ["rows"] * 3, "ffn1_in": ["cols"], "ffn1_out": ["rows"]}.get(grp, ["cols", "rows"])

    def gather_start(k, deps=()):
        l, grp = order_fwd[k]
        srcs = group_srcs(l, grp)
        lands = [lax.empty((2, D, F) if kind == "cols" else (N_DEV,) + a.shape, a.dtype)
                 for a, kind in zip(srcs, kinds_of(grp))]
        return split_start(srcs, lands, _routes_gather, f"gather_start_{grp}_{l}", deps, land_kinds=kinds_of(grp))

    def gather_forward(k, started, after):
        l, grp = order_fwd[k]
        srcs, lands = split_wait(started, after, _routes_gather, f"gather_wait_{grp}_{l}")
        return srcs, split_start(None, lands, _routes_forward, f"forward_start_{grp}_{l}", land_kinds=kinds_of(grp))

    def gather_finish(k, srcs, started, after):
        l, grp = order_fwd[k]
        _, lands = split_wait(started, after, _routes_forward, f"forward_wait_{grp}_{l}")
        full = [place_own(ld, s, me_idx, kind, f"own_{grp}_{l}_{a}")
                for a, (ld, s, kind) in enumerate(zip(lands, srcs, kinds_of(grp)))]
        if grp == "mix":
            return dict(w_in=_shards_to_cols(full[0]), w_out=full[1].reshape(D, D),
                        taps=_shards_to_cols(full[2][:, :, :cw]))
        named = dict(zip(["w_in", "w_out"] if len(full) == 2 else ["w_in" if grp == "ffn1_in" else "w_out"], full))
        if "w_out" in named:
            named["w_out"] = named["w_out"].reshape(F, D)
        return named

    def ffn_head(xs, wts, g_norm, tag, sv):
        sv["x_in"] = xs
        h, sv["ht"] = rmsnorm_fwd(xs, g_norm, f"norm_{tag}")
        sv["gu"], a, sv["at"] = ffn_in(h, wts["w_in"], f"{tag}_in")
        return a

    def ffn_fwd(xs, wts, g_norm, tag, sv, mid):
        a = ffn_head(xs, wts, g_norm, tag, sv)
        return mm_res(a, wts["w_out"], xs, FFN_RES, f"{tag}_out", deps=mid(a))

    def mix_fwd(xs, wts, l, sv, mid):
        sv["x_in"] = xs
        h, sv["ht"] = rmsnorm_fwd(xs, row(norm_mix[l]), f"norm_mix_{l}")
        p = mm_nn(h, wts["w_in"][None], f"mix_in_{l}")[0]
        sv["p"] = p
        attn = attn_fwd(p, rope_c, rope_s1, rope_s2, attn_sinks[l], f"attn_fwd_{l}")
        conv, sv["conv_y"] = conv_fwd(p, wts["taps"], row(conv_dw_b[l]), row(conv_ln_g[l]), row(conv_ln_b[l]),
                                      f"conv_fwd_{l}")
        sv["sgu_bias"] = jnp.repeat(sgu_b[l].T, HEAD_DIM, axis=1)
        sgu = sgu_fwd(p, row(sgu_ln_g[l]), row(sgu_ln_b[l]), sgu_w[l], sv["sgu_bias"], f"sgu_fwd_{l}")
        cat = jnp.concatenate([attn, conv, sgu], axis=1)
        sv["catt"] = cat.T
        return mm_res(cat, wts["w_out"], xs, 1.0, f"mix_out_{l}", deps=mid(cat))

    weights, saved = {}, {}
    xs = x0
    starts = []
    for k in range(len(order_fwd)):
        starts.append(gather_start(k, (starts[-1][4],) if starts else ()))
    state = dict(zip(("srcs", "fwd"), gather_forward(0, starts[0], starts[-1][4])))
    for k, (l, grp) in enumerate(order_fwd):
        nxt = starts[k + 1] if k + 1 < len(order_fwd) else None
        wts = gather_finish(k, state["srcs"], state["fwd"], state["fwd"][4])

        def mid(after, k=k, nxt=nxt):
            if not nxt:
                return ()
            state["srcs"], state["fwd"] = gather_forward(k + 1, nxt, after)
            return (state["fwd"][4],)

        if grp == "ffn1_in":
            sv = saved[l, "ffn1"] = {}
            weights[l, "ffn1"] = wts
            head = ffn_head(xs, wts, row(norm_ffn1[l]), f"ffn1_{l}", sv)
            mid(head)
        elif grp == "ffn1_out":
            weights[l, "ffn1"].update(wts)
            xs = mm_res(head, wts["w_out"], xs, FFN_RES, f"ffn1_{l}_out")
            mid(xs)
        else:
            sv = saved[l, grp] = {}
            weights[l, grp] = wts
            if grp == "mix":
                xs = mix_fwd(xs, wts, l, sv, mid)
            else:
                xs = ffn_fwd(xs, wts, row(norm_ffn1[l] if grp == "ffn1" else norm_ffn2[l]), f"{grp}_{l}", sv, mid)

    dx, dxb, d_final_norm, loss = final_loss(xs, row(final_norm), loss_target[0], "final_loss")

    def scatter_start(grads, kinds, tag, deps=()):
        half = N_DEV // 2
        lands = [lax.empty((half, g.shape[1], g.shape[2] // half) if kind == "cols" else (half,) + g.shape[1:], g.dtype)
                 for g, kind in zip(grads, kinds)]
        return split_start(grads, lands, _routes_pair, f"pair_start_{tag}", deps, src_kinds=kinds)

    def pair_to_chips(started, after, tag):
        kinds = started[5][0]
        grads, landed = split_wait(started, after, _routes_pair, f"pair_wait_{tag}")
        sums = [chip_sum(g, ld, c_idx, kind, f"chip_sum_{tag}_{a}")
                for a, (g, ld, kind) in enumerate(zip(grads, landed, kinds))]
        return split_start(sums, [lax.empty(s.shape, s.dtype) for s in sums], _routes_chips, f"chips_start_{tag}")

    def ffn_bwd(dx, dxb, wts, sv, g_norm, tag, deps=(), out_first=False):
        dgu = ffn_dact(dxb, wts["w_out"], sv["gu"], f"{tag}_dact")
        d_w_out = mm_nn(sv["at"], dxb[None], f"{tag}_dwout", bf16, FFN_RES, deps=deps)[0]
        d_w_out = d_w_out.reshape(N_DEV, F // N_DEV, D)
        if not out_first:
            d_w_in = mm_nn(sv["ht"], dgu, f"{tag}_dwin", bf16)
            pair = scatter_start([d_w_in, d_w_out], ["cols", "rows"], tag)
            dh = mm_nt(dgu, wts["w_in"], f"{tag}_dh", deps=(pair[4],))
            chips = [pair_to_chips(pair, dh, tag)]
        else:
            pair_out = scatter_start([d_w_out], ["rows"], f"{tag}_out")
            d_w_in = mm_nn(sv["ht"], dgu, f"{tag}_dwin", bf16, deps=(pair_out[4],))
            chips_out = pair_to_chips(pair_out, d_w_in, f"{tag}_out")
            pair_in = scatter_start([d_w_in], ["cols"], f"{tag}_in", deps=(chips_out[4],))
            dh = mm_nt(dgu, wts["w_in"], f"{tag}_dh", deps=(pair_in[4],))
            chips = [pair_to_chips(pair_in, dh, f"{tag}_in"), chips_out]
        dx, dxb, dg = rmsnorm_bwd(dh, sv["x_in"], g_norm, dx, f"{tag}_dnorm", deps=(chips[0][4],))
        return dx, dxb, dg, chips

    small = [None] * L
    chips_pending = {}
    for l in reversed(range(L)):
        dx, dxb, d_norm_ffn2, chips_pending[l, "ffn2"] = ffn_bwd(
            dx, dxb, weights[l, "ffn2"], saved[l, "ffn2"], row(norm_ffn2[l]), f"ffn2_{l}")

        wts, sv = weights[l, "mix"], saved[l, "mix"]
        d_w_out = mm_nn(sv["catt"], dxb[None], f"mix_dwout_{l}", bf16)[0]
        dcat = mm_nt(dxb[None], wts["w_out"][None], f"mix_dcat_{l}", deps=(d_w_out,))
        p = sv["p"]
        dq, dkv, d_sinks = attn_bwd(p, dcat, rope_c, rope_s1, rope_s2, attn_sinks[l], f"attn_bwd_{l}")
        da_conv, d_taps, d_conv_b, d_conv_g, d_conv_bb = conv_bwd(
            p, sv["conv_y"], dcat, wts["taps"], row(conv_ln_g[l]), row(conv_ln_b[l]), f"conv_bwd_{l}")
        da_sgu, d_sgu_w, d_sgu_bias, d_sgu_g, d_sgu_bb = sgu_bwd(
            p, dcat, row(sgu_ln_g[l]), row(sgu_ln_b[l]), sgu_w[l], jnp.swapaxes(sgu_w[l], 1, 2), sv["sgu_bias"],
            f"sgu_bwd_{l}")
        dp = jnp.concatenate([dq, dkv, da_conv, da_sgu], axis=1)
        d_w_in = mm_nn(sv["ht"], dp[None], f"mix_dwin_{l}", bf16)[0]
        pair = scatter_start([_cols_to_shards(d_w_in), d_w_out.reshape(N_DEV, D // N_DEV, D)], ["rows", "rows"],
                             f"mix_{l}")
        dh = mm_nt(dp[None], wts["w_in"][None], f"mix_dh_{l}", deps=(pair[4],))
        chips_pending[l, "mix"] = [pair_to_chips(pair, dh, f"mix_{l}")]
        dx, dxb, d_norm_mix = rmsnorm_bwd(dh, sv["x_in"], row(norm_mix[l]), dx, f"mix_dnorm_{l}",
                                          deps=(chips_pending[l, "mix"][0][4],))

        small[l] = dict(norm_mix=d_norm_mix[0], conv_dw_w=d_taps[:CONV_W],
                        conv_dw_b=d_conv_b[0], conv_ln_g=d_conv_g[0], conv_ln_b=d_conv_bb[0], sgu_ln_g=d_sgu_g[0],
                        sgu_ln_b=d_sgu_bb[0], sgu_w=d_sgu_w, sgu_b=d_sgu_bias[:, :SGU_HEADS].T,
                        attn_sinks=d_sinks[0, :N_Q_HEADS], norm_ffn2=d_norm_ffn2[0])
        if l == 0:
            early_names = ["norm_mix", "conv_dw_w", "conv_dw_b", "conv_ln_g", "conv_ln_b", "sgu_ln_g", "sgu_ln_b",
                           "sgu_w", "sgu_b", "attn_sinks", "norm_ffn2"]
            early = [jnp.stack([small[k][n] for k in range(L)]) for n in early_names]
            early += [d_final_norm[0], loss.reshape(1)]
            early_shapes = [a.shape for a in early]
            early = _pack(early)
            early_pending = split_start([early], [lax.empty((N_DEV,) + early.shape, f32)], _routes_all,
                                        "small_start", deps=(dxb,))
            early_token = (early_pending[4],)
        else:
            early_token = ()

        dx, dxb, d_norm_ffn1, chips_pending[l, "ffn1"] = ffn_bwd(
            dx, dxb, weights[l, "ffn1"], saved[l, "ffn1"], row(norm_ffn1[l]), f"ffn1_{l}", early_token, l == 0)
        small[l]["norm_ffn1"] = d_norm_ffn1[0]

    grad_x = dx[None]
    late = _pack([jnp.stack([small[l]["norm_ffn1"] for l in range(L)])])
    late_pending = split_start([late], [lax.empty((N_DEV,) + late.shape, f32)], _routes_all, "late_start", deps=(dx,))

    def landed(grp, after):
        sums, lands = [], []
        for l in range(L):
            got = [split_wait(st, after, _routes_chips, f"chips_wait_{grp}_{l}_{a}")
                   for a, st in enumerate(chips_pending[l, grp])]
            sums.append([s for g in got for s in g[0]])
            lands.append([s for g in got for s in g[1]])
        return sums, lands

    given = dict(norm_ffn1=(norm_ffn1, m_norm_ffn1, v_norm_ffn1), norm_mix=(norm_mix, m_norm_mix, v_norm_mix),
                 conv_dw_w=(conv_dw_w, m_conv_dw_w, v_conv_dw_w), conv_dw_b=(conv_dw_b, m_conv_dw_b, v_conv_dw_b),
                 conv_ln_g=(conv_ln_g, m_conv_ln_g, v_conv_ln_g), conv_ln_b=(conv_ln_b, m_conv_ln_b, v_conv_ln_b),
                 sgu_ln_g=(sgu_ln_g, m_sgu_ln_g, v_sgu_ln_g), sgu_ln_b=(sgu_ln_b, m_sgu_ln_b, v_sgu_ln_b),
                 sgu_w=(sgu_w, m_sgu_w, v_sgu_w), sgu_b=(sgu_b, m_sgu_b, v_sgu_b),
                 attn_sinks=(attn_sinks, m_attn_sinks, v_attn_sinks), norm_ffn2=(norm_ffn2, m_norm_ffn2, v_norm_ffn2),
                 final_norm=(final_norm, m_final_norm, v_final_norm))

    def small_update(pending, after, names, shapes, tag):
        (own,), (others,) = split_wait(pending, after, _routes_all, f"{tag}_wait")
        total = _unpack(sum_slots(place_own(others, own, me_idx, "rows", f"{tag}_own"), f"{tag}_sum"), shapes)
        g = dict(zip(names, total))
        if "conv_dw_w" in g:
            g["conv_dw_w"] = lax.dynamic_slice_in_dim(g["conv_dw_w"], me * cw, cw, axis=2)
        upd_names = [n for n in names if n in given]
        upd_shapes = [given[n][0].shape for n in upd_names]
        packed = [_pack([g[n] for n in upd_names])[None]] + [_pack([given[n][k] for n in upd_names])[None]
                                                              for k in range(3)]
        res = adamw([packed[0]], None, no_chip, packed[1], packed[2], packed[3], f"{tag}_adamw")
        return g, [dict(zip(upd_names, _unpack(r[0], upd_shapes))) for r in res], res[0]

    big = {}
    done = [dx]

    def big_update(grp, names):
        sums, lands = landed(grp, done)
        for idx, (name, w, m, v) in enumerate(names):
            big[name] = adamw([lands[l][idx] for l in range(L)], [sums[l][idx] for l in range(L)], chip, w, m, v,
                              f"adamw_{name}")
            done.append(big[name][0])

    big_update("ffn2", (("ffn2_w_in", ffn2_w_in, m_ffn2_w_in, v_ffn2_w_in),
                        ("ffn2_w_out", ffn2_w_out, m_ffn2_w_out, v_ffn2_w_out)))
    big_update("mix", (("w_in", w_in, m_w_in, v_w_in), ("w_out", w_out, m_w_out, v_w_out)))
    g_early, upd_early, marker = small_update(early_pending, done, early_names + ["final_norm", "loss"], early_shapes,
                                              "small")
    done.append(marker)
    big_update("ffn1", (("ffn1_w_in", ffn1_w_in, m_ffn1_w_in, v_ffn1_w_in),
                        ("ffn1_w_out", ffn1_w_out, m_ffn1_w_out, v_ffn1_w_out)))
    _, upd_late, _ = small_update(late_pending, done, ["norm_ffn1"], [(L, D)], "late")
    upd = [{**upd_early[k], **upd_late[k]} for k in range(4)]

    order = ["norm_ffn1", "ffn1_w_in", "ffn1_w_out", "norm_mix", "w_in", "conv_dw_w", "conv_dw_b", "conv_ln_g",
             "conv_ln_b", "sgu_ln_g", "sgu_ln_b", "sgu_w", "sgu_b", "attn_sinks", "w_out", "norm_ffn2", "ffn2_w_in",
             "ffn2_w_out", "final_norm"]
    outs = [g_early["loss"].reshape(()), grad_x]
    for k in range(4):
        outs += [big[n][k] if n in big else upd[k][n] for n in order]
    return tuple(outs)
```
